```python
import math
import functools
import jax
import jax.numpy as jnp
from jax import lax
import numpy as np

D_MODEL = 1024
BATCH = 8
SEQ = 8192
DEPTH = 2

F32 = jnp.float32
GRID_W = 64
CTX_LEN = 256
HEAD_DIM = 64
ATTN_SCALE = HEAD_DIM ** -0.5
ROPE_AXIS_DIM = HEAD_DIM // 2
ROPE_BASE = 10000.0
EPS = 1e-6
NEG_INF = -1e30

MLSTM_HEADS = 8
MLSTM_CHUNK = 64
SWA_HEADS = 8
SWA_KV_HEADS = 2
SWA_WINDOW = 128
SWA_BLOCK = 128
S5_CHANNELS = 256
S5_GROUP = 16
S5_GROUPS = S5_CHANNELS // S5_GROUP
S5_STATE = 64
ATT_HEADS = 12
ATT_KV_HEADS = 3
ATT_QBLOCK = 128
N_EXPERTS = 32
TOP_K = 4
EXPERT_FF = 1024
SWIGLU_LIMIT = 7.0
SWIGLU_ALPHA = 1.702
MOE_BLOCK = 256

MIX_WIDTH = MLSTM_HEADS * HEAD_DIM + SWA_HEADS * HEAD_DIM
EVEN_SPLIT_SIZES = (MLSTM_HEADS * HEAD_DIM,) * 4 + (4 * MLSTM_HEADS, SWA_HEADS * HEAD_DIM, SWA_KV_HEADS * HEAD_DIM, SWA_KV_HEADS * HEAD_DIM)
ODD_SPLIT_SIZES = (S5_CHANNELS, ATT_HEADS * HEAD_DIM, ATT_KV_HEADS * HEAD_DIM, ATT_KV_HEADS * HEAD_DIM)
EVEN_IN = sum(EVEN_SPLIT_SIZES)
ODD_IN = sum(ODD_SPLIT_SIZES)

kernel_name = 'hybrid_mlstm_swa_s5_axial_moe_dit'


def rmsnorm(x, g):
    xf = x.astype(F32)
    y = xf * lax.rsqrt(jnp.mean(xf * xf, axis=-1, keepdims=True) + EPS)
    return (y * g.astype(F32)).astype(x.dtype)


def split_cols(p, sizes):
    return jnp.split(p, np.cumsum(sizes)[:-1].tolist(), axis=-1)


def to_heads(t, n_heads):
    return t.reshape(t.shape[0], t.shape[1], n_heads, HEAD_DIM)


def axial_rope_tables(n_tokens):
    rows = n_tokens // GRID_W
    row = jnp.repeat(jnp.arange(rows, dtype=jnp.int32), GRID_W)
    col = jnp.tile(jnp.arange(GRID_W, dtype=jnp.int32), rows)
    inv = ROPE_BASE ** (-jnp.arange(0, ROPE_AXIS_DIM, 2, dtype=F32) / ROPE_AXIS_DIM)
    ang_r = row.astype(F32)[:, None] * inv
    ang_c = col.astype(F32)[:, None] * inv
    return (jnp.cos(ang_r), jnp.sin(ang_r), jnp.cos(ang_c), jnp.sin(ang_c))


def apply_axial_rope(x, rope):
    cr, sr, cc, sc = [t[:, None, :] for t in rope]

    def rot(t, cos, sin):
        t1, t2 = jnp.split(t, 2, axis=-1)
        return jnp.concatenate([t1 * cos - t2 * sin, t2 * cos + t1 * sin], axis=-1)

    xf = x.astype(F32)
    out = jnp.concatenate([rot(xf[..., :ROPE_AXIS_DIM], cr, sr), rot(xf[..., ROPE_AXIS_DIM:], cc, sc)], axis=-1)
    return out.astype(x.dtype)


def context_attention(q, k, v, sink):
    B, L, H, _ = q.shape
    hkv = k.shape[2]
    qg = q.reshape(B, L, hkv, H // hkv, HEAD_DIM)
    s = jnp.einsum('bqhgd,bkhd->bhgqk', qg, k).astype(F32) * ATTN_SCALE
    if sink is not None:
        s_sink = jnp.broadcast_to(sink.astype(F32).reshape(hkv, H // hkv, 1, 1), s.shape[:-1] + (1,))
        p = jax.nn.softmax(jnp.concatenate([s, s_sink], axis=-1), axis=-1)[..., :-1]
    else:
        p = jax.nn.softmax(s, axis=-1)
    o = jnp.einsum('bhgqk,bkhd->bqhgd', p.astype(v.dtype), v)
    return o.reshape(B, L, H * HEAD_DIM)


def window_attention(q, k, v, k_ctx, v_ctx, sink):
    B, S, H, _ = q.shape
    hkv = k.shape[2]
    G = H // hkv
    W = SWA_BLOCK
    nb = S // W
    L = k_ctx.shape[1]
    qb = q.reshape(B, nb, W, hkv, G, HEAD_DIM)

    def band(t):
        tp = jnp.pad(t, ((0, 0), (W, W), (0, 0), (0, 0))).reshape(B, nb + 2, W, hkv, HEAD_DIM)
        return jnp.concatenate([tp[:, :-2], tp[:, 1:-1], tp[:, 2:]], axis=2)

    k_band, v_band = band(k), band(v)
    s_loc = jnp.einsum('bnqhgd,bnkhd->bnhgqk', qb, k_band).astype(F32) * ATTN_SCALE
    blk = jnp.arange(nb)[:, None, None] * W
    qpos = blk + jnp.arange(W)[None, :, None]
    kpos = blk - W + jnp.arange(3 * W)[None, None, :]
    valid = (jnp.abs(qpos - kpos) <= SWA_WINDOW) & (kpos >= 0) & (kpos < S)
    s_loc = jnp.where(valid[None, :, None, None], s_loc, NEG_INF)
    s_ctx = jnp.einsum('bnqhgd,bkhd->bnhgqk', qb, k_ctx).astype(F32) * ATTN_SCALE
    s_sink = jnp.broadcast_to(sink.astype(F32).reshape(hkv, G, 1, 1), s_ctx.shape[:-1] + (1,))
    p = jax.nn.softmax(jnp.concatenate([s_ctx, s_loc, s_sink], axis=-1), axis=-1).astype(v.dtype)
    out = (jnp.einsum('bnhgqk,bkhd->bnqhgd', p[..., :L], v_ctx)
           + jnp.einsum('bnhgqk,bnkhd->bnqhgd', p[..., L:L + 3 * W], v_band))
    return out.reshape(B, S, H * HEAD_DIM)


def dense_block_attention(q, k, v, k_ctx, v_ctx):
    B, S, H, _ = q.shape
    hkv = k.shape[2]
    G = H // hkv
    nb = S // ATT_QBLOCK
    k_all = jnp.concatenate([k_ctx, k], axis=1)
    v_all = jnp.concatenate([v_ctx, v], axis=1)
    qb = q.reshape(B, nb, ATT_QBLOCK, hkv, G, HEAD_DIM).transpose(1, 0, 2, 3, 4, 5)

    def one_block(q_blk):
        s = jnp.einsum('bqhgd,bkhd->bhgqk', q_blk, k_all).astype(F32) * ATTN_SCALE
        p = jax.nn.softmax(s, axis=-1).astype(v_all.dtype)
        return jnp.einsum('bhgqk,bkhd->bqhgd', p, v_all)

    out = lax.map(one_block, qb)
    return out.transpose(1, 0, 2, 3, 4, 5).reshape(B, S, H * HEAD_DIM)


def mlstm_zero_state(batch):
    return (jnp.zeros((batch, MLSTM_HEADS, HEAD_DIM, HEAD_DIM), F32),
            jnp.zeros((batch, MLSTM_HEADS, HEAD_DIM), F32),
            jnp.full((batch, MLSTM_HEADS), NEG_INF, F32))


def mlstm_chunkwise(q, k, v, log_i, log_f, state):
    B, H, T, _ = q.shape
    CH = MLSTM_CHUNK
    nc = T // CH
    q = q.reshape(B, H, nc, CH, HEAD_DIM)
    k = k.reshape(B, H, nc, CH, HEAD_DIM)
    v = v.reshape(B, H, nc, CH, HEAD_DIM)
    log_i = log_i.reshape(B, H, nc, CH)
    b = jnp.cumsum(log_f.reshape(B, H, nc, CH), axis=-1)
    b_last = b[..., -1]
    w_log = b_last[..., None] - b + log_i
    m_loc = jnp.max(w_log, axis=-1)
    w = jnp.exp(w_log - m_loc[..., None])
    c_loc = jnp.einsum('bhcs,bhcsk,bhcsv->bhckv', w, k, v)
    n_loc = jnp.einsum('bhcs,bhcsk->bhck', w, k)

    def step(carry, inp):
        c_prev, n_prev, m_prev = carry
        c_l, n_l, m_l, bl = inp
        m_new = jnp.maximum(bl + m_prev, m_l)
        decay = jnp.exp(bl + m_prev - m_new)
        fresh = jnp.exp(m_l - m_new)
        c_new = decay[..., None, None] * c_prev + fresh[..., None, None] * c_l
        n_new = decay[..., None] * n_prev + fresh[..., None] * n_l
        return (c_new, n_new, m_new), (c_prev, n_prev, m_prev)

    chunk_major = lambda t: jnp.moveaxis(t, 2, 0)
    final, (c_s, n_s, m_s) = lax.scan(step, state, (chunk_major(c_loc), chunk_major(n_loc), chunk_major(m_loc), chunk_major(b_last)))
    c_s, n_s, m_s = jnp.moveaxis(c_s, 0, 2), jnp.moveaxis(n_s, 0, 2), jnp.moveaxis(m_s, 0, 2)
    causal = jnp.tril(jnp.ones((CH, CH), dtype=bool))
    d_log = jnp.where(causal, b[..., :, None] - b[..., None, :] + log_i[..., None, :], NEG_INF)
    inter_log = b + m_s[..., None]
    m_t = jnp.maximum(inter_log, jnp.max(d_log, axis=-1))
    s = jnp.einsum('bhctd,bhcsd->bhcts', q, k) * jnp.exp(d_log - m_t[..., None])
    inter = jnp.exp(inter_log - m_t)
    num = inter[..., None] * jnp.einsum('bhctk,bhckv->bhctv', q, c_s) + jnp.einsum('bhcts,bhcsv->bhctv', s, v)
    den = jnp.maximum(jnp.abs(inter * jnp.einsum('bhctk,bhck->bhct', q, n_s) + jnp.sum(s, axis=-1)), jnp.exp(-m_t))
    return (num / den[..., None]).reshape(B, H, T, HEAD_DIM), final


def mlstm_inputs(q, k, v, g, gate_bias):
    B, T, _ = q.shape
    heads = lambda t: t.reshape(B, T, MLSTM_HEADS, HEAD_DIM).transpose(0, 2, 1, 3).astype(F32)
    gates = (g.reshape(B, T, 4, MLSTM_HEADS).astype(F32) + gate_bias.astype(F32)).transpose(2, 0, 3, 1)
    return heads(q), heads(k) * ATTN_SCALE, heads(v), gates


def mlstm_bidir(q, k, v, gates, st_f, st_b):
    flip = lambda t: jnp.flip(t, axis=2)
    h_f, fin_f = mlstm_chunkwise(q, k, v, gates[0], jax.nn.log_sigmoid(gates[1]), st_f)
    h_b, fin_b = mlstm_chunkwise(flip(q), flip(k), flip(v), flip(gates[2]), flip(jax.nn.log_sigmoid(gates[3])), st_b)
    return h_f + flip(h_b), fin_f, fin_b


def mlstm_output(h, o, norm):
    B, H, T, _ = h.shape
    hn = rmsnorm(h.transpose(0, 2, 1, 3), norm).reshape(B, T, H * HEAD_DIM)
    return (jax.nn.sigmoid(o.astype(F32)) * hn).astype(o.dtype)


def s5_discretise(lam_re, lam_im, log_dt, b_re, b_im):
    lam = lax.complex(lam_re.astype(F32), lam_im.astype(F32))
    dt = jnp.exp(log_dt.astype(F32))[..., None]
    a_bar = jnp.exp(lam * dt)
    b_scale = (a_bar - 1.0) / lam
    b_mat = lax.complex(b_re.astype(F32), b_im.astype(F32))
    return a_bar, b_scale, b_mat


def s5_drive(u, b_mat):
    B, T, _ = u.shape
    ug = u.astype(F32).reshape(B, T, S5_GROUPS, S5_GROUP).astype(jnp.complex64)
    return jnp.einsum('gpc,btgc->btgp', b_mat, ug)


def s5_combine(e1, e2):
    a1, b1 = e1
    a2, b2 = e2
    return a1 * a2, a2 * b1 + b2


def s5_scan(bu, a, x0):
    T = bu.shape[1]
    bu = bu.at[:, 0].add(a * x0)
    a_t = jnp.broadcast_to(a, (1, T) + a.shape)
    _, xs = lax.associative_scan(s5_combine, (a_t, bu), axis=1)
    return xs, xs[:, -1]


def s5_readout(states, u, c_mat, d_skip, w_glu, b_glu):
    B, T, _ = u.shape
    uf = u.astype(F32)
    y = jnp.real(jnp.einsum('gcp,btgp->btgc', c_mat, states)).reshape(B, T, S5_CHANNELS) + d_skip.astype(F32) * uf
    y = jax.nn.gelu(y)
    y = y * jax.nn.sigmoid(y @ w_glu.astype(F32) + b_glu.astype(F32))
    return y.astype(u.dtype)


def mixer_even(px, pc, rope, last, gate_bias, mlstm_norm, sink):
    qa_x, ka_x, va_x, oa_x, ga_x, qb_x, kb_x, vb_x = split_cols(px, EVEN_SPLIT_SIZES)
    qa_c, ka_c, va_c, oa_c, ga_c, qb_c, kb_c, vb_c = split_cols(pc, EVEN_SPLIT_SIZES)
    zero = mlstm_zero_state(px.shape[0])
    h_c, st_f, st_b = mlstm_bidir(*mlstm_inputs(qa_c, ka_c, va_c, ga_c, gate_bias), zero, zero)
    h_x, _, _ = mlstm_bidir(*mlstm_inputs(qa_x, ka_x, va_x, ga_x, gate_bias), st_f, st_b)
    k_c = to_heads(kb_c, SWA_KV_HEADS)
    v_c = to_heads(vb_c, SWA_KV_HEADS)
    q_x = apply_axial_rope(to_heads(qb_x, SWA_HEADS), rope)
    k_x = apply_axial_rope(to_heads(kb_x, SWA_KV_HEADS), rope)
    out_x = jnp.concatenate([mlstm_output(h_x, oa_x, mlstm_norm),
                             window_attention(q_x, k_x, to_heads(vb_x, SWA_KV_HEADS), k_c, v_c, sink)], axis=-1)
    if last:
        return out_x, None
    out_c = jnp.concatenate([mlstm_output(h_c, oa_c, mlstm_norm),
                             context_attention(to_heads(qb_c, SWA_HEADS), k_c, v_c, sink)], axis=-1)
    return out_x, out_c


def mixer_odd(px, pc, rope, last, lam_re, lam_im, log_dt, b_re, b_im, c_re, c_im, d_skip, w_glu, b_glu, q_norm, k_norm):
    u_x, q_x, k_x, v_x = split_cols(px, ODD_SPLIT_SIZES)
    u_c, q_c, k_c, v_c = split_cols(pc, ODD_SPLIT_SIZES)
    a_bar, b_scale, b_mat = s5_discretise(lam_re, lam_im, log_dt, b_re, b_im)
    c_mat = lax.complex(c_re.astype(F32), c_im.astype(F32))
    bu_c = s5_drive(u_c, b_mat)
    bu_x = s5_drive(u_x, b_mat)
    zero = jnp.zeros((px.shape[0], S5_GROUPS, S5_STATE), jnp.complex64)
    flip = lambda t: jnp.flip(t, axis=1)
    xc_f, fin_f = s5_scan(bu_c * b_scale[0], a_bar[0], zero)
    xc_b, fin_b = s5_scan(flip(bu_c * b_scale[1]), a_bar[1], zero)
    xx_f, _ = s5_scan(bu_x * b_scale[0], a_bar[0], fin_f)
    xx_b, _ = s5_scan(flip(bu_x * b_scale[1]), a_bar[1], fin_b)
    s5_x = s5_readout(xx_f + flip(xx_b), u_x, c_mat, d_skip, w_glu, b_glu)
    kc = rmsnorm(to_heads(k_c, ATT_KV_HEADS), k_norm)
    vc = to_heads(v_c, ATT_KV_HEADS)
    qx = apply_axial_rope(rmsnorm(to_heads(q_x, ATT_HEADS), q_norm), rope)
    kx = apply_axial_rope(rmsnorm(to_heads(k_x, ATT_KV_HEADS), k_norm), rope)
    att_x = dense_block_attention(qx, kx, to_heads(v_x, ATT_KV_HEADS), kc, vc)
    out_x = jnp.concatenate([s5_x, att_x], axis=-1)
    if last:
        return out_x, None
    out_c = jnp.concatenate([s5_readout(xc_f + flip(xc_b), u_c, c_mat, d_skip, w_glu, b_glu),
                             context_attention(rmsnorm(to_heads(q_c, ATT_HEADS), q_norm), kc, vc, None)], axis=-1)
    return out_x, out_c


def moe(xt, w_router, b_router, w_gate_up, b_gate_up, w_down, b_down):
    N, D = xt.shape
    logits = xt.astype(F32) @ w_router.astype(F32) + b_router.astype(F32)
    top_val, top_idx = lax.top_k(logits, TOP_K)
    gates = jax.nn.softmax(top_val, axis=-1)
    flat_e = top_idx.reshape(-1)
    order = jnp.argsort(flat_e)
    sorted_e = flat_e[order]
    tok = order // TOP_K
    counts = jnp.bincount(flat_e, length=N_EXPERTS)
    padded = (counts + MOE_BLOCK - 1) // MOE_BLOCK * MOE_BLOCK
    pad_end = jnp.cumsum(padded)
    pad_start = pad_end - padded
    grp_start = jnp.cumsum(counts) - counts
    dest = pad_start[sorted_e] + jnp.arange(N * TOP_K) - grp_start[sorted_e]
    n_blocks = -(-N * TOP_K // MOE_BLOCK) + N_EXPERTS
    buf = jnp.zeros((n_blocks * MOE_BLOCK, D), xt.dtype).at[dest].set(xt[tok])
    blk_e = jnp.minimum(jnp.searchsorted(pad_end, jnp.arange(n_blocks) * MOE_BLOCK, side='right'), N_EXPERTS - 1)

    def expert_block(args):
        xb, e = args
        h = xb @ w_gate_up[e] + b_gate_up[e]
        glu = jnp.minimum(h[:, 0::2], SWIGLU_LIMIT)
        lin = jnp.clip(h[:, 1::2], -SWIGLU_LIMIT, SWIGLU_LIMIT)
        act = glu * jax.nn.sigmoid(SWIGLU_ALPHA * glu) * (lin + 1.0)
        return act @ w_down[e] + b_down[e]

    out = lax.map(expert_block, (buf.reshape(n_blocks, MOE_BLOCK, D), blk_e)).reshape(-1, D)
    y_sorted = out[dest] * gates.reshape(-1)[order][:, None].astype(xt.dtype)
    return jnp.zeros_like(xt).at[tok].add(y_sorted)


def adaln(cond, w_mod, b_mod):
    return jnp.split(jax.nn.silu(cond) @ w_mod + b_mod, 6, axis=-1)


def run_layer(x, h_ctx, c, c_ctx, w_mod, b_mod, g_mix, g_ffn, w_in, w_out, moe_params, mixer, rope, last):
    B, S, D = x.shape
    L = h_ctx.shape[1]
    sh1, sc1, gt1, sh2, sc2, gt2 = [m[:, None, :] for m in adaln(c, w_mod, b_mod)]
    csh1, csc1, cgt1, csh2, csc2, cgt2 = adaln(c_ctx, w_mod, b_mod)
    xn = rmsnorm(x, g_mix) * (1.0 + sc1) + sh1
    cn = rmsnorm(h_ctx, g_mix) * (1.0 + csc1) + csh1
    out_x, out_c = mixer(xn @ w_in, cn @ w_in, rope, last)
    x = x + gt1 * (out_x @ w_out)
    xn2 = rmsnorm(x, g_ffn) * (1.0 + sc2) + sh2
    if last:
        return x + gt2 * moe(xn2.reshape(B * S, D), *moe_params).reshape(B, S, D), h_ctx
    h_ctx = h_ctx + cgt1 * (out_c @ w_out)
    cn2 = rmsnorm(h_ctx, g_ffn) * (1.0 + csc2) + csh2
    y = moe(jnp.concatenate([cn2.reshape(B * L, D), xn2.reshape(B * S, D)], axis=0), *moe_params)
    return x + gt2 * y[B * L:].reshape(B, S, D), h_ctx + cgt2 * y[:B * L].reshape(B, L, D)


def setup_inputs(seed: int = 0) -> dict:
    key = jax.random.key(seed)
    keys = iter(jax.random.split(key, 64))
    D = D_MODEL

    def nrm(shape, scale=1.0):
        return scale * jax.random.normal(next(keys), shape, F32)

    def gain(shape):
        return 1.0 + nrm(shape, 0.02)

    inp = {'x': nrm((BATCH, SEQ, D)), 'c': nrm((BATCH, D)), 'ctx': nrm((BATCH, CTX_LEN, D)), 'c_ctx': nrm((D,))}

    def add_common(p, in_width):
        inp[p + 'w_mod'] = nrm((D, 6 * D), 0.5 * D ** -0.5)
        inp[p + 'b_mod'] = nrm((6 * D,), 0.02)
        inp[p + 'g_mix'] = gain((D,))
        inp[p + 'g_ffn'] = gain((D,))
        inp[p + 'w_in'] = nrm((D, in_width), D ** -0.5)
        inp[p + 'w_out'] = nrm((MIX_WIDTH, D), MIX_WIDTH ** -0.5)

    def add_moe(p):
        inp[p + 'w_router'] = nrm((D, N_EXPERTS), D ** -0.5)
        inp[p + 'b_router'] = nrm((N_EXPERTS,), 0.01)
        inp[p + 'w_gate_up'] = nrm((N_EXPERTS, D, 2 * EXPERT_FF), D ** -0.5)
        inp[p + 'b_gate_up'] = nrm((N_EXPERTS, 2 * EXPERT_FF), 0.02)
        inp[p + 'w_down'] = nrm((N_EXPERTS, EXPERT_FF, D), EXPERT_FF ** -0.5)
        inp[p + 'b_down'] = nrm((N_EXPERTS, D), 0.02)

    add_common('l0_', EVEN_IN)
    f_bias = jnp.linspace(3.0, 6.0, MLSTM_HEADS, dtype=F32)
    zeros_h = jnp.zeros((MLSTM_HEADS,), F32)
    inp['l0_gate_bias'] = nrm((4, MLSTM_HEADS), 0.1) + jnp.stack([zeros_h, f_bias, zeros_h, f_bias])
    inp['l0_mlstm_norm'] = gain((MLSTM_HEADS, HEAD_DIM))
    inp['l0_sink'] = nrm((SWA_HEADS,), 0.5)
    add_moe('l0_')

    add_common('l1_', ODD_IN)
    inp['l1_lam_re'] = -0.5 + nrm((2, S5_GROUPS, S5_STATE), 0.01)
    inp['l1_lam_im'] = math.pi * jnp.arange(S5_STATE, dtype=F32) + nrm((2, S5_GROUPS, S5_STATE), 0.01)
    inp['l1_log_dt'] = jax.random.uniform(next(keys), (2, S5_GROUPS), F32, math.log(1e-3), math.log(1e-1))
    inp['l1_b_re'] = nrm((S5_GROUPS, S5_STATE, S5_GROUP), (2 * S5_GROUP) ** -0.5)
    inp['l1_b_im'] = nrm((S5_GROUPS, S5_STATE, S5_GROUP), (2 * S5_GROUP) ** -0.5)
    inp['l1_c_re'] = nrm((S5_GROUPS, S5_GROUP, S5_STATE), S5_STATE ** -0.5)
    inp['l1_c_im'] = nrm((S5_GROUPS, S5_GROUP, S5_STATE), S5_STATE ** -0.5)
    inp['l1_d_skip'] = nrm((S5_CHANNELS,), 0.5)
    inp['l1_w_glu'] = nrm((S5_CHANNELS, S5_CHANNELS), S5_CHANNELS ** -0.5)
    inp['l1_b_glu'] = nrm((S5_CHANNELS,), 0.02)
    inp['l1_q_norm'] = gain((HEAD_DIM,))
    inp['l1_k_norm'] = gain((HEAD_DIM,))
    add_moe('l1_')

    inp['g_final'] = gain((D,))
    return inp


def reference(x, c, ctx, c_ctx,
              l0_w_mod, l0_b_mod, l0_g_mix, l0_g_ffn, l0_w_in, l0_w_out,
              l0_gate_bias, l0_mlstm_norm, l0_sink,
              l0_w_router, l0_b_router, l0_w_gate_up, l0_b_gate_up, l0_w_down, l0_b_down,
              l1_w_mod, l1_b_mod, l1_g_mix, l1_g_ffn, l1_w_in, l1_w_out,
              l1_lam_re, l1_lam_im, l1_log_dt, l1_b_re, l1_b_im, l1_c_re, l1_c_im,
              l1_d_skip, l1_w_glu, l1_b_glu, l1_q_norm, l1_k_norm,
              l1_w_router, l1_b_router, l1_w_gate_up, l1_b_gate_up, l1_w_down, l1_b_down,
              g_final):
    rope = axial_rope_tables(x.shape[1])
    layers = [
        (functools.partial(mixer_even, gate_bias=l0_gate_bias, mlstm_norm=l0_mlstm_norm, sink=l0_sink),
         (l0_w_mod, l0_b_mod, l0_g_mix, l0_g_ffn, l0_w_in, l0_w_out),
         (l0_w_router, l0_b_router, l0_w_gate_up, l0_b_gate_up, l0_w_down, l0_b_down)),
        (functools.partial(mixer_odd, lam_re=l1_lam_re, lam_im=l1_lam_im, log_dt=l1_log_dt, b_re=l1_b_re, b_im=l1_b_im,
                           c_re=l1_c_re, c_im=l1_c_im, d_skip=l1_d_skip, w_glu=l1_w_glu, b_glu=l1_b_glu,
                           q_norm=l1_q_norm, k_norm=l1_k_norm),
         (l1_w_mod, l1_b_mod, l1_g_mix, l1_g_ffn, l1_w_in, l1_w_out),
         (l1_w_router, l1_b_router, l1_w_gate_up, l1_b_gate_up, l1_w_down, l1_b_down)),
    ]
    h_ctx = ctx
    for i in range(DEPTH):
        mixer, common, moe_params = layers[i]
        x, h_ctx = run_layer(x, h_ctx, c, c_ctx, *common, moe_params, mixer, rope, i == DEPTH - 1)
    return rmsnorm(x, g_final)
```

```python
import functools
import math

import jax
import jax.numpy as jnp
import numpy as np
from jax import lax
from jax.experimental import pallas as pl
from jax.experimental.pallas import tpu as pltpu

F32 = jnp.float32
BF16 = jnp.bfloat16
I32 = jnp.int32

GRID_W = 64
HEAD_DIM = 64
ATTN_SCALE = HEAD_DIM ** -0.5
ROPE_AXIS_DIM = HEAD_DIM // 2
ROPE_BASE = 10000.0
EPS = 1e-6
NEG_INF = -1e30

MLSTM_HEADS = 8
MLSTM_CHUNK = 64
SWA_HEADS = 8
SWA_KV_HEADS = 2
SWA_WINDOW = 128
SWA_BLOCK = 128
S5_CHANNELS = 256
S5_GROUP = 16
S5_GROUPS = S5_CHANNELS // S5_GROUP
S5_STATE = 64
S5_CHUNK = 64
ATT_HEADS = 12
ATT_KV_HEADS = 3
N_EXPERTS = 32
TOP_K = 4
SWIGLU_LIMIT = 7.0
SWIGLU_ALPHA = 1.702

LANES = 128
VMEM_LIMIT = 56 * 1024 * 1024
MOE_ROWS = 512
HIGHEST = lax.Precision.HIGHEST


def _params(*sem):
    return pltpu.CompilerParams(dimension_semantics=sem, vmem_limit_bytes=VMEM_LIMIT)


def _tile(n, prefs):
    for t in prefs:
        if n % t == 0:
            return t
    return n


def _pad_cols(w, mult=LANES):
    pad = (-w.shape[-1]) % mult
    if pad:
        w = jnp.pad(w, [(0, 0)] * (w.ndim - 1) + [(0, pad)])
    return w


def _linear_kernel(x_ref, w_ref, b_ref, o_ref):
    x = x_ref[...]
    x = x * jax.nn.sigmoid(x)
    o_ref[...] = jnp.dot(x, w_ref[...], precision=HIGHEST, preferred_element_type=F32) + b_ref[...]


def _silu_linear(x, w, b):
    m, k = x.shape
    n = w.shape[1]
    tn = _tile(n, (1024, 512, 256, 128))
    return pl.pallas_call(
        _linear_kernel,
        out_shape=jax.ShapeDtypeStruct((m, n), F32),
        grid=(n // tn,),
        in_specs=[pl.BlockSpec((m, k), lambda j: (0, 0)),
                  pl.BlockSpec((k, tn), lambda j: (0, j)),
                  pl.BlockSpec((1, tn), lambda j: (0, j))],
        out_specs=pl.BlockSpec((m, tn), lambda j: (0, j)),
        compiler_params=_params("arbitrary"),
        name="adaln_linear",
    )(x, w, b.reshape(1, n))


def _in_proj_kernel(x_ref, g_ref, sc_ref, sh_ref, w_ref, *out_refs, widths):
    x = x_ref[...]
    xn = x * lax.rsqrt(jnp.mean(x * x, axis=-1, keepdims=True) + EPS) * g_ref[...]
    xb = (xn * (1.0 + sc_ref[0]) + sh_ref[0]).astype(BF16)
    off = 0
    for o_ref, w in zip(out_refs, widths):
        o_ref[...] = jnp.dot(xb, w_ref[:, off:off + w], preferred_element_type=F32).astype(o_ref.dtype)
        off += w


def _in_proj(x, g, scale, shift, mod_map, w, widths, dtypes, tm):
    r, d = x.shape
    grid = (r // tm,)
    return pl.pallas_call(
        functools.partial(_in_proj_kernel, widths=tuple(widths)),
        out_shape=[jax.ShapeDtypeStruct((r, wd), dt) for wd, dt in zip(widths, dtypes)],
        grid=grid,
        in_specs=[pl.BlockSpec((tm, d), lambda i: (i, 0)),
                  pl.BlockSpec((1, d), lambda i: (0, 0)),
                  pl.BlockSpec((1, 1, d), lambda i: (mod_map(i), 0, 0)),
                  pl.BlockSpec((1, 1, d), lambda i: (mod_map(i), 0, 0)),
                  pl.BlockSpec(w.shape, lambda i: (0, 0))],
        out_specs=[pl.BlockSpec((tm, wd), lambda i: (i, 0)) for wd in widths],
        compiler_params=_params("parallel"),
        name="in_proj",
    )(x, g.reshape(1, d), scale, shift, w)


def _qk_prep_kernel(x_ref, xr_ref, cos_ref, sin_ref, g_ref, gr_ref, o_ref, *, norm, rope, scale):
    x = x_ref[0].astype(F32)
    if norm:
        s = lax.rsqrt(jnp.mean(x * x, axis=-1, keepdims=True) + EPS)
        x = x * s * g_ref[...]
    if rope:
        xr = xr_ref[0].astype(F32)
        if norm:
            xr = xr * s * gr_ref[...]
        x = x * cos_ref[...] + xr * sin_ref[...]
    o_ref[0] = (x * scale).astype(o_ref.dtype)


def _qk_prep(x, x_rot, cos, sin, gain, gain_rot, *, norm, rope, scale):
    b, h, t, dh = x.shape
    tt = _tile(t, (512, 256, 128))
    blk = pl.BlockSpec((1, h, tt, dh), lambda i, j: (i, 0, j, 0))
    tab = pl.BlockSpec((tt, dh), lambda i, j: (j, 0))
    vec = pl.BlockSpec((1, dh), lambda i, j: (0, 0))
    return pl.pallas_call(
        functools.partial(_qk_prep_kernel, norm=norm, rope=rope, scale=scale),
        out_shape=jax.ShapeDtypeStruct(x.shape, BF16),
        grid=(b, t // tt),
        in_specs=[blk, blk, tab, tab, vec, vec],
        out_specs=blk,
        compiler_params=_params("parallel", "parallel"),
        name="qk_prep",
    )(x, x_rot, cos, sin, gain.reshape(1, dh), gain_rot.reshape(1, dh))


def _log_sigmoid(x):
    return jnp.minimum(x, 0.0) - jnp.log(1.0 + jnp.exp(-jnp.abs(x)))


def _mlstm_kernel(q_ref, k_ref, v_ref, gc_ref, gr_ref, bc_ref, br_ref, h_ref, c_scr, n_scr, m_scr):
    ch = MLSTM_CHUNK
    nh = MLSTM_HEADS

    @pl.when(pl.program_id(1) == 0)
    def _():
        c_scr[...] = jnp.zeros_like(c_scr)
        n_scr[...] = jnp.zeros_like(n_scr)
        m_scr[...] = jnp.full_like(m_scr, NEG_INF)

    row = lax.broadcasted_iota(I32, (ch, ch), 0)
    col = lax.broadcasted_iota(I32, (ch, ch), 1)
    causal = row >= col
    tri = causal.astype(F32)
    tri_t = (row <= col).astype(F32)

    for d in range(2):
        gc = gc_ref[d, 0, 0] + bc_ref[d]
        gr = gr_ref[d, 0, 0] + br_ref[d]
        li_col, lf_col = gc[:, :nh], _log_sigmoid(gc[:, nh:])
        li_row, lf_row = gr[:nh, :], _log_sigmoid(gr[nh:, :])
        b_col = jnp.dot(tri, lf_col, precision=HIGHEST, preferred_element_type=F32)
        b_row = jnp.dot(lf_row, tri_t, precision=HIGHEST, preferred_element_type=F32)
        for h in range(nh):
            q = q_ref[d, 0, h]
            k = k_ref[d, 0, h]
            v = v_ref[d, 0, h]
            bc = b_col[:, h:h + 1]
            br = b_row[h:h + 1, :]
            lir = li_row[h:h + 1, :]
            lic = li_col[:, h:h + 1]
            b_last = br[:, ch - 1:ch]
            c_prev = c_scr[d, h]
            n_prev = n_scr[d, h]
            m_prev = m_scr[d, h][:, 0:1]

            d_log = jnp.where(causal, bc - br + lir, NEG_INF)
            inter_log = bc + m_prev
            m_t = jnp.maximum(inter_log, jnp.max(d_log, axis=1, keepdims=True))
            qk = lax.dot_general(q, k, (((1,), (1,)), ((), ())), preferred_element_type=F32)
            s = qk * jnp.exp(d_log - m_t)
            inter = jnp.exp(inter_log - m_t)
            q_c = jnp.dot(q, c_prev.astype(BF16), preferred_element_type=F32)
            num = inter * q_c + jnp.dot(s.astype(BF16), v, preferred_element_type=F32)
            q_n = jnp.sum(q.astype(F32) * n_prev, axis=1, keepdims=True)
            den = jnp.maximum(jnp.abs(inter * q_n + jnp.sum(s, axis=1, keepdims=True)), jnp.exp(-m_t))
            h_ref[d, 0, h] = num / den

            w_log = b_last - bc + lic
            m_loc = jnp.max(w_log, axis=0, keepdims=True)
            m_new = jnp.maximum(b_last + m_prev, m_loc)
            decay = jnp.exp(b_last + m_prev - m_new)
            kw = k.astype(F32) * jnp.exp(w_log - m_new)
            c_loc = lax.dot_general(kw.astype(BF16), v, (((0,), (0,)), ((), ())), preferred_element_type=F32)
            c_scr[d, h] = decay * c_prev + c_loc
            n_scr[d, h] = decay * n_prev + jnp.sum(kw, axis=0, keepdims=True)
            m_scr[d, h] = jnp.broadcast_to(m_new, (1, HEAD_DIM))


def _mlstm(q, k, v, gc, gr, bias_c, bias_r):
    _, b, nh, t, dh = q.shape
    ch = MLSTM_CHUNK
    nc = t // ch
    qblk = pl.BlockSpec((2, 1, nh, ch, dh), lambda i, c: (0, i, 0, c, 0))
    return pl.pallas_call(
        _mlstm_kernel,
        out_shape=jax.ShapeDtypeStruct(q.shape, F32),
        grid=(b, nc),
        in_specs=[qblk, qblk, qblk,
                  pl.BlockSpec((2, 1, 1, ch, 2 * nh), lambda i, c: (0, i, c, 0, 0)),
                  pl.BlockSpec((2, 1, 1, 2 * nh, ch), lambda i, c: (0, i, c, 0, 0)),
                  pl.BlockSpec((2, 1, 2 * nh), lambda i, c: (0, 0, 0)),
                  pl.BlockSpec((2, 2 * nh, 1), lambda i, c: (0, 0, 0))],
        out_specs=qblk,
        scratch_shapes=[pltpu.VMEM((2, nh, dh, dh), F32),
                        pltpu.VMEM((2, nh, 1, dh), F32),
                        pltpu.VMEM((2, nh, 1, dh), F32)],
        compiler_params=_params("parallel", "arbitrary"),
        name="mlstm_scan",
    )(q, k, v, gc, gr, bias_c, bias_r)


def _mlstm_out_kernel(hf_ref, hb_ref, o_ref, nrm_ref, ones_ref, y_ref):
    h = hf_ref[...] + hb_ref[...]
    ms = jnp.dot((h * h).astype(BF16), ones_ref[...], preferred_element_type=F32) * (1.0 / HEAD_DIM)
    hn = h * lax.rsqrt(ms + EPS) * nrm_ref[...]
    y_ref[...] = (jax.nn.sigmoid(o_ref[...].astype(F32)) * hn).astype(y_ref.dtype)


def _head_ones(width):
    idx = np.arange(width) // HEAD_DIM
    return jnp.asarray(idx[:, None] == idx[None, :], BF16)


def _mlstm_out(hf, hb, o, norm, tm):
    r, w = hf.shape
    blk = pl.BlockSpec((tm, w), lambda i: (i, 0))
    return pl.pallas_call(
        _mlstm_out_kernel,
        out_shape=jax.ShapeDtypeStruct((r, w), BF16),
        grid=(r // tm,),
        in_specs=[blk, blk, blk,
                  pl.BlockSpec((1, w), lambda i: (0, 0)),
                  pl.BlockSpec((w, w), lambda i: (0, 0))],
        out_specs=blk,
        compiler_params=_params("parallel"),
        name="mlstm_out",
    )(hf, hb, o, norm.reshape(1, w), _head_ones(w))


def _swa_kernel(q_ref, kp_ref, k0_ref, kn_ref, vp_ref, v0_ref, vn_ref, kc_ref, vc_ref, sink_ref, o_ref, *, seq):
    w = SWA_BLOCK
    g = q_ref.shape[1]
    i = pl.program_id(2)
    q = q_ref[0].reshape(g * w, HEAD_DIM)
    k_loc = jnp.concatenate([kp_ref[0, 0], k0_ref[0, 0], kn_ref[0, 0]], axis=0)
    v_loc = jnp.concatenate([vp_ref[0, 0], v0_ref[0, 0], vn_ref[0, 0]], axis=0)
    dn = (((1,), (1,)), ((), ()))
    s_loc = lax.dot_general(q, k_loc, dn, preferred_element_type=F32)
    s_ctx = lax.dot_general(q, kc_ref[0, 0], dn, preferred_element_type=F32)
    qpos = i * w + (lax.broadcasted_iota(I32, (g * w, 3 * w), 0) & (w - 1))
    kpos = (i - 1) * w + lax.broadcasted_iota(I32, (g * w, 3 * w), 1)
    valid = (jnp.abs(qpos - kpos) <= SWA_WINDOW) & (kpos >= 0) & (kpos < seq)
    s_loc = jnp.where(valid, s_loc, NEG_INF)
    sink = sink_ref[0]
    m = jnp.maximum(jnp.maximum(jnp.max(s_loc, axis=1, keepdims=True),
                                jnp.max(s_ctx, axis=1, keepdims=True)), sink)
    p_loc = jnp.exp(s_loc - m)
    p_ctx = jnp.exp(s_ctx - m)
    den = jnp.sum(p_loc, axis=1, keepdims=True) + jnp.sum(p_ctx, axis=1, keepdims=True) + jnp.exp(sink - m)
    o = (jnp.dot(p_loc.astype(BF16), v_loc, preferred_element_type=F32)
         + jnp.dot(p_ctx.astype(BF16), vc_ref[0, 0], preferred_element_type=F32)) / den
    o_ref[0] = o.reshape(g, w, HEAD_DIM).astype(o_ref.dtype)


def _swa(q, k, v, k_ctx, v_ctx, sink_col):
    b, nh, s, dh = q.shape
    hkv = k.shape[1]
    g = nh // hkv
    w = SWA_BLOCK
    nb = s // w
    lc = k_ctx.shape[2]
    kv = lambda f: pl.BlockSpec((1, 1, w, dh), lambda bi, hi, i: (bi, hi, f(i), 0))
    prev, own, nxt = kv(lambda i: jnp.maximum(i - 1, 0)), kv(lambda i: i), kv(lambda i: jnp.minimum(i + 1, nb - 1))
    ctx = pl.BlockSpec((1, 1, lc, dh), lambda bi, hi, i: (bi, hi, 0, 0))
    qblk = pl.BlockSpec((1, g, w, dh), lambda bi, hi, i: (bi, hi, i, 0))
    return pl.pallas_call(
        functools.partial(_swa_kernel, seq=s),
        out_shape=jax.ShapeDtypeStruct(q.shape, BF16),
        grid=(b, hkv, nb),
        in_specs=[qblk, prev, own, nxt, prev, own, nxt, ctx, ctx,
                  pl.BlockSpec((1, g * w, 1), lambda bi, hi, i: (hi, 0, 0))],
        out_specs=qblk,
        compiler_params=_params("parallel", "parallel", "arbitrary"),
        name="window_attention",
    )(q, k, k, k, v, v, v, k_ctx, v_ctx, sink_col)


def _ctx_attn_kernel(q_ref, k_ref, v_ref, sink_ref, o_ref, *, use_sink):
    g, lq, dh = q_ref.shape[1:]
    q = q_ref[0].reshape(g * lq, dh)
    s = lax.dot_general(q, k_ref[0, 0], (((1,), (1,)), ((), ())), preferred_element_type=F32) * ATTN_SCALE
    m = jnp.max(s, axis=1, keepdims=True)
    if use_sink:
        m = jnp.maximum(m, sink_ref[0])
    p = jnp.exp(s - m)
    den = jnp.sum(p, axis=1, keepdims=True)
    if use_sink:
        den = den + jnp.exp(sink_ref[0] - m)
    o = jnp.dot(p.astype(BF16), v_ref[0, 0], preferred_element_type=F32) / den
    o_ref[0] = o.reshape(g, lq, dh).astype(o_ref.dtype)


def _ctx_attn(q, k, v, sink_col, use_sink):
    b, nh, lq, dh = q.shape
    hkv = k.shape[1]
    g = nh // hkv
    qblk = pl.BlockSpec((1, g, lq, dh), lambda bi, hi: (bi, hi, 0, 0))
    kblk = pl.BlockSpec((1, 1, k.shape[2], dh), lambda bi, hi: (bi, hi, 0, 0))
    return pl.pallas_call(
        functools.partial(_ctx_attn_kernel, use_sink=use_sink),
        out_shape=jax.ShapeDtypeStruct(q.shape, BF16),
        grid=(b, hkv),
        in_specs=[qblk, kblk, kblk, pl.BlockSpec((1, g * lq, 1), lambda bi, hi: (hi, 0, 0))],
        out_specs=qblk,
        compiler_params=_params("parallel", "parallel"),
        name="context_attention",
    )(q, k, v, sink_col)


def _flash_kernel(q_ref, k_ref, v_ref, o_ref, m_scr, l_scr, acc_scr, *, tk):
    g, tq, dh = q_ref.shape[1:]
    q = q_ref[0].reshape(g * tq, dh)
    m_scr[...] = jnp.full_like(m_scr, NEG_INF)
    l_scr[...] = jnp.zeros_like(l_scr)
    acc_scr[...] = jnp.zeros_like(acc_scr)

    def body(j, carry):
        start = pl.multiple_of(j * tk, tk)
        kk = k_ref[0, 0, pl.ds(start, tk), :]
        vv = v_ref[0, 0, pl.ds(start, tk), :]
        s = lax.dot_general(q, kk, (((1,), (1,)), ((), ())), preferred_element_type=F32)
        m_old = m_scr[...]
        m_new = jnp.maximum(m_old, jnp.max(s, axis=1, keepdims=True))
        alpha = jnp.exp(m_old - m_new)
        p = jnp.exp(s - m_new)
        l_scr[...] = alpha * l_scr[...] + jnp.sum(p, axis=1, keepdims=True)
        acc_scr[...] = alpha * acc_scr[...] + jnp.dot(p.astype(BF16), vv, preferred_element_type=F32)
        m_scr[...] = m_new
        return carry

    lax.fori_loop(0, k_ref.shape[2] // tk, body, 0)
    o_ref[0] = (acc_scr[...] / l_scr[...]).reshape(g, tq, dh).astype(o_ref.dtype)


def _flash(q, k, v):
    b, nh, s, dh = q.shape
    hkv, t = k.shape[1], k.shape[2]
    g = nh // hkv
    tq = _tile(s, (256, 128))
    tk = _tile(t, (768, 512, 384, 256, 128))
    qblk = pl.BlockSpec((1, g, tq, dh), lambda bi, hi, i: (bi, hi, i, 0))
    kblk = pl.BlockSpec((1, 1, t, dh), lambda bi, hi, i: (bi, hi, 0, 0))
    return pl.pallas_call(
        functools.partial(_flash_kernel, tk=tk),
        out_shape=jax.ShapeDtypeStruct(q.shape, BF16),
        grid=(b, hkv, s // tq),
        in_specs=[qblk, kblk, kblk],
        out_specs=qblk,
        scratch_shapes=[pltpu.VMEM((g * tq, 1), F32), pltpu.VMEM((g * tq, 1), F32),
                        pltpu.VMEM((g * tq, dh), F32)],
        compiler_params=_params("parallel", "parallel", "arbitrary"),
        name="dense_attention",
    )(q, k, v)


def _bmm_kernel(a_ref, b_ref, o_ref):
    o_ref[0] = jnp.dot(a_ref[0], b_ref[0], preferred_element_type=F32).astype(o_ref.dtype)


def _group_matmul(a, bmat, name):
    g, m, k = a.shape
    n = bmat.shape[2]
    tm = _tile(m, (1024, 512, 256, 128, 64, 32, 16, 8))
    return pl.pallas_call(
        _bmm_kernel,
        out_shape=jax.ShapeDtypeStruct((g, m, n), F32),
        grid=(g, m // tm),
        in_specs=[pl.BlockSpec((1, tm, k), lambda gi, i: (gi, i, 0)),
                  pl.BlockSpec((1, k, n), lambda gi, i: (gi, 0, 0))],
        out_specs=pl.BlockSpec((1, tm, n), lambda gi, i: (gi, i, 0)),
        compiler_params=_params("parallel", "parallel"),
        name=name,
    )(a, bmat)


def _s5_scan_kernel(sre_ref, sim_ref, are_ref, aim_ref, zre_ref, zim_ref):
    nd, nsteps = sre_ref.shape[0], sre_ref.shape[1]
    for d in range(nd):
        a_re, a_im = are_ref[d], aim_ref[d]

        def body(i, carry, d=d, a_re=a_re, a_im=a_im):
            z_re, z_im = carry
            zre_ref[d, i] = z_re
            zim_ref[d, i] = z_im
            return (a_re * z_re - a_im * z_im + sre_ref[d, i], a_re * z_im + a_im * z_re + sim_ref[d, i])

        zero = jnp.zeros(sre_ref.shape[2:], F32)
        lax.fori_loop(0, nsteps, body, (zero, zero))


def _s5_scan(s_re, s_im, a_re, a_im):
    full = lambda arr: pl.BlockSpec(arr.shape, lambda i: (0,) * arr.ndim)
    return pl.pallas_call(
        _s5_scan_kernel,
        out_shape=[jax.ShapeDtypeStruct(s_re.shape, F32)] * 2,
        grid=(1,),
        in_specs=[full(s_re), full(s_im), full(a_re), full(a_im)],
        out_specs=[full(s_re), full(s_im)],
        compiler_params=_params("arbitrary"),
        name="s5_chunk_scan",
    )(s_re, s_im, a_re, a_im)


def _s5_glu_kernel(y_ref, u_ref, d_ref, w_ref, b_ref, o_ref):
    y = y_ref[...] + d_ref[...] * u_ref[...]
    y = jax.nn.gelu(y)
    gate = jnp.dot(y.astype(BF16), w_ref[...], preferred_element_type=F32) + b_ref[...]
    o_ref[...] = (y * jax.nn.sigmoid(gate)).astype(o_ref.dtype)


def _s5_glu(y, u, d_skip, w_glu, b_glu, tm):
    r, c = y.shape
    blk = pl.BlockSpec((tm, c), lambda i: (i, 0))
    vec = pl.BlockSpec((1, c), lambda i: (0, 0))
    return pl.pallas_call(
        _s5_glu_kernel,
        out_shape=jax.ShapeDtypeStruct((r, c), BF16),
        grid=(r // tm,),
        in_specs=[blk, blk, vec, pl.BlockSpec((c, c), lambda i: (0, 0)), vec],
        out_specs=blk,
        compiler_params=_params("parallel"),
        name="s5_readout_glu",
    )(y, u, d_skip.reshape(1, c), w_glu.astype(BF16), b_glu.reshape(1, c))


def _s5_tables(lam_re, lam_im, log_dt, b_re, b_im, c_re, c_im):
    ln, p, gc = S5_CHUNK, S5_STATE, S5_GROUP
    lam = lax.complex(lam_re.astype(F32), lam_im.astype(F32))
    dt = jnp.exp(log_dt.astype(F32))[..., None]
    a_bar = jnp.exp(lam * dt)
    b_scale = (a_bar - 1.0) / lam
    b_mat = lax.complex(b_re.astype(F32), b_im.astype(F32))
    c_mat = lax.complex(c_re.astype(F32), c_im.astype(F32))
    tau = jnp.arange(ln + 1, dtype=F32)
    apow = jnp.exp((lam * dt)[:, :, None, :] * tau[None, None, :, None])
    drive = b_scale[..., None] * b_mat[None]
    kern = jnp.real(jnp.einsum('gcp,dgtp,dgpe->dgtce', c_mat, apow[:, :, :ln], drive))
    kc = jnp.concatenate([kern[1, :, :0:-1], (kern[0, :, :1] + kern[1, :, :1]), kern[0, :, 1:]], axis=1)
    t_idx = jnp.arange(ln)
    lag = t_idx[None, :] - t_idx[:, None] + ln - 1
    toep = kc[:, lag]
    toep = toep.transpose(0, 1, 4, 2, 3).reshape(S5_GROUPS, ln * gc, ln * gc)
    w_f = apow[0, :, ln - 1::-1][:, :ln, :, None] * drive[0][:, None]
    w_b = apow[1, :, :ln, :, None] * drive[1][:, None]
    def m_in(wc):
        wt = wc.transpose(0, 1, 3, 2).reshape(S5_GROUPS, ln * gc, p)
        return jnp.concatenate([jnp.real(wt), jnp.imag(wt)], axis=-1)
    min_all = jnp.concatenate([m_in(w_f), m_in(w_b)], axis=-1)
    o_f = c_mat[:, None] * apow[0, :, 1:ln + 1][:, :, None, :]
    o_b = c_mat[:, None] * apow[1, :, ln:0:-1][:, :, None, :]
    def m_out(oc):
        ot = oc.transpose(0, 3, 1, 2).reshape(S5_GROUPS, p, ln * gc)
        return jnp.concatenate([jnp.real(ot), -jnp.imag(ot)], axis=1)
    rhs = jnp.concatenate([toep, m_out(o_f), m_out(o_b)], axis=1)
    a_l = apow[:, :, ln].reshape(2, 1, S5_GROUPS * p)
    return min_all.astype(BF16), rhs.astype(BF16), jnp.real(a_l), jnp.imag(a_l)


def _s5_mixer(u_x, u_c, tables, d_skip, w_glu, b_glu):
    min_all, rhs, a_re, a_im = tables
    b, s, _ = u_x.shape
    ln, p, gc, ng = S5_CHUNK, S5_STATE, S5_GROUP, S5_GROUPS

    def chunks(u):
        nc = u.shape[1] // ln
        return u.reshape(b, nc, ln, ng, gc).transpose(3, 0, 1, 2, 4).reshape(ng, b * nc, ln * gc).astype(BF16), nc

    ux, ncx = chunks(u_x)
    uc, ncc = chunks(u_c)
    sx = _group_matmul(ux, min_all, "s5_local_state").reshape(ng, b, ncx, 4, p)
    sc = _group_matmul(uc, min_all, "s5_local_state_ctx").reshape(ng, b, ncc, 4, p)

    def scan_order(part):
        f = jnp.concatenate([sc[:, :, :, part], sx[:, :, :, part]], axis=2)
        r = jnp.concatenate([sc[:, :, ::-1, part + 2], sx[:, :, ::-1, part + 2]], axis=2)
        return jnp.stack([f, r]).transpose(0, 3, 2, 1, 4).reshape(2, ncc + ncx, b, ng * p)

    z_re, z_im = _s5_scan(scan_order(0), scan_order(1), a_re, a_im)

    def latent(z, d):
        zl = z[d, ncc:]
        if d == 1:
            zl = zl[::-1]
        return zl.reshape(ncx, b, ng, p).transpose(2, 1, 0, 3).reshape(ng, b * ncx, p)

    lhs = jnp.concatenate([ux, latent(z_re, 0).astype(BF16), latent(z_im, 0).astype(BF16),
                           latent(z_re, 1).astype(BF16), latent(z_im, 1).astype(BF16)], axis=-1)
    y = _group_matmul(lhs, rhs, "s5_outputs")
    y = y.reshape(ng, b, ncx, ln, gc).transpose(1, 2, 3, 0, 4).reshape(b * s, S5_CHANNELS)
    return _s5_glu(y, u_x.reshape(b * s, S5_CHANNELS), d_skip, w_glu, b_glu, _tile(b * s, (1024, 512, 256)))


def _out_proj_kernel(x_ref, a_ref, b_ref, wa_ref, wb_ref, gt_ref, g_ref, sc_ref, sh_ref, wr_ref, br_ref,
                     xo_ref, xn_ref, lg_ref):
    y = (jnp.dot(a_ref[...], wa_ref[...], preferred_element_type=F32)
         + jnp.dot(b_ref[...], wb_ref[...], preferred_element_type=F32))
    x = x_ref[...] + gt_ref[0] * y
    xo_ref[...] = x
    xn = x * lax.rsqrt(jnp.mean(x * x, axis=-1, keepdims=True) + EPS) * g_ref[...]
    xn = xn * (1.0 + sc_ref[0]) + sh_ref[0]
    xn_ref[...] = xn.astype(BF16)
    lg_ref[...] = jnp.dot(xn, wr_ref[...], precision=HIGHEST, preferred_element_type=F32) + br_ref[...]


def _out_proj(x, row0, a, bmix, wa, wb, gate, g_ffn, scale, shift, mod_map, w_router, b_router, tm):
    r = a.shape[0]
    d = x.shape[1]
    t0 = row0 // tm
    ne = w_router.shape[1]
    row = lambda w: pl.BlockSpec((tm, w), lambda i: (i, 0))
    mod = pl.BlockSpec((1, 1, d), lambda i: (mod_map(i), 0, 0))
    full = lambda arr: pl.BlockSpec(arr.shape, lambda i: (0, 0))
    return pl.pallas_call(
        _out_proj_kernel,
        out_shape=[jax.ShapeDtypeStruct((r, d), F32), jax.ShapeDtypeStruct((r, d), BF16),
                   jax.ShapeDtypeStruct((r, ne), F32)],
        grid=(r // tm,),
        in_specs=[pl.BlockSpec((tm, d), lambda i: (i + t0, 0)), row(a.shape[1]), row(bmix.shape[1]),
                  full(wa), full(wb), mod, pl.BlockSpec((1, d), lambda i: (0, 0)), mod, mod,
                  full(w_router), pl.BlockSpec((1, ne), lambda i: (0, 0))],
        out_specs=[row(d), row(d), row(ne)],
        compiler_params=_params("parallel"),
        name="out_proj",
    )(x, a, bmix, wa, wb, gate, g_ffn.reshape(1, d), scale, shift, w_router, b_router.reshape(1, ne))


def _route_kernel(lg_ref, idx_ref, gate_ref, pos_ref, cnt_ref, carry):
    tm, ne = lg_ref.shape

    @pl.when(pl.program_id(0) == 0)
    def _():
        carry[...] = jnp.zeros_like(carry)

    work = lg_ref[...]
    lane = lax.broadcasted_iota(I32, (tm, ne), 1).astype(F32)
    out_lane = lax.broadcasted_iota(I32, (tm, LANES), 1)
    vals, hots, idx_out = [], [], jnp.zeros((tm, LANES), I32)
    for kk in range(TOP_K):
        mx = jnp.max(work, axis=1, keepdims=True)
        idx = jnp.min(jnp.where(work == mx, lane, float(ne)), axis=1, keepdims=True)
        hot = lane == idx
        work = jnp.where(hot, -jnp.inf, work)
        vals.append(mx)
        hots.append(hot)
        idx_out = jnp.where(out_lane == kk, idx.astype(I32), idx_out)
    exps = [jnp.exp(vv - vals[0]) for vv in vals]
    tot = exps[0] + exps[1] + exps[2] + exps[3]
    multi = (hots[0] | hots[1] | hots[2] | hots[3]).astype(BF16)
    r_i = lax.broadcasted_iota(I32, (tm, tm), 0)
    c_i = lax.broadcasted_iota(I32, (tm, tm), 1)
    before = (r_i > c_i).astype(BF16)
    prefix = jnp.dot(before, multi, preferred_element_type=F32) + carry[...]
    gate_out = jnp.zeros((tm, LANES), F32)
    pos_out = jnp.zeros((tm, LANES), I32)
    for kk in range(TOP_K):
        gate_out = jnp.where(out_lane == kk, exps[kk] / tot, gate_out)
        pos = jnp.sum(jnp.where(hots[kk], prefix, 0.0), axis=1, keepdims=True).astype(I32)
        pos_out = jnp.where(out_lane == kk, pos, pos_out)
    idx_ref[...] = idx_out
    gate_ref[...] = gate_out
    pos_ref[...] = pos_out
    carry[...] = carry[...] + jnp.sum(multi.astype(F32), axis=0, keepdims=True)
    cnt_ref[...] = carry[...]


def _route(logits, tm):
    n, ne = logits.shape
    wide = pl.BlockSpec((tm, LANES), lambda i: (i, 0))
    return pl.pallas_call(
        _route_kernel,
        out_shape=[jax.ShapeDtypeStruct((n, LANES), I32), jax.ShapeDtypeStruct((n, LANES), F32),
                   jax.ShapeDtypeStruct((n, LANES), I32), jax.ShapeDtypeStruct((1, ne), F32)],
        grid=(n // tm,),
        in_specs=[pl.BlockSpec((tm, ne), lambda i: (i, 0))],
        out_specs=[wide, wide, wide, pl.BlockSpec((1, ne), lambda i: (0, 0))],
        scratch_shapes=[pltpu.VMEM((1, ne), F32)],
        compiler_params=_params("arbitrary"),
        name="moe_route",
    )(logits)


def _expert_kernel(be_ref, nb_ref, x_ref, wg_ref, wl_ref, bg_ref, bl_ref, wd_ref, bd_ref, o_ref):
    i = pl.program_id(0)

    @pl.when(i < nb_ref[0])
    def _():
        x = x_ref[...]
        hg = jnp.dot(x, wg_ref[0], preferred_element_type=F32) + bg_ref[0]
        hl = jnp.dot(x, wl_ref[0], preferred_element_type=F32) + bl_ref[0]
        glu = jnp.minimum(hg, SWIGLU_LIMIT)
        lin = jnp.clip(hl, -SWIGLU_LIMIT, SWIGLU_LIMIT)
        act = glu * jax.nn.sigmoid(SWIGLU_ALPHA * glu) * (lin + 1.0)
        o_ref[...] = (jnp.dot(act.astype(BF16), wd_ref[0], preferred_element_type=F32) + bd_ref[0]).astype(o_ref.dtype)

    @pl.when(i >= nb_ref[0])
    def _():
        o_ref[...] = jnp.zeros_like(o_ref)


def _experts(xs, blk_e, n_used, wg, wl, bg, bl, wd, bd):
    n, d = xs.shape
    ff = wg.shape[2]
    nblk = n // MOE_ROWS
    wspec = lambda shape: pl.BlockSpec((1,) + shape, lambda i, be, nb: (be[i], 0, 0))
    return pl.pallas_call(
        _expert_kernel,
        out_shape=jax.ShapeDtypeStruct((n, d), BF16),
        grid_spec=pltpu.PrefetchScalarGridSpec(
            num_scalar_prefetch=2,
            grid=(nblk,),
            in_specs=[pl.BlockSpec((MOE_ROWS, d), lambda i, be, nb: (i, 0)),
                      wspec((d, ff)), wspec((d, ff)), wspec((1, ff)), wspec((1, ff)),
                      wspec((ff, d)), wspec((1, d))],
            out_specs=pl.BlockSpec((MOE_ROWS, d), lambda i, be, nb: (i, 0))),
        compiler_params=_params("arbitrary"),
        name="moe_experts",
    )(blk_e, n_used, xs, wg, wl, bg, bl, wd, bd)


def _combine_kernel(x_ref, y_ref, gate_ref, gt_ref, g_ref, o_ref, *, final_norm):
    d = x_ref.shape[1]
    gates = gate_ref[...]
    y = gates[:, 0:1] * y_ref[:, 0:d].astype(F32)
    for kk in range(1, TOP_K):
        y = y + gates[:, kk:kk + 1] * y_ref[:, kk * d:(kk + 1) * d].astype(F32)
    x = x_ref[...] + gt_ref[0] * y
    if final_norm:
        x = x * lax.rsqrt(jnp.mean(x * x, axis=-1, keepdims=True) + EPS) * g_ref[...]
    o_ref[...] = x


def _combine(x, y4, gates, gate_mod, mod_map, g_final, final_norm, tm):
    r, d = x.shape
    return pl.pallas_call(
        functools.partial(_combine_kernel, final_norm=final_norm),
        out_shape=jax.ShapeDtypeStruct((r, d), F32),
        grid=(r // tm,),
        in_specs=[pl.BlockSpec((tm, d), lambda i: (i, 0)),
                  pl.BlockSpec((tm, TOP_K * d), lambda i: (i, 0)),
                  pl.BlockSpec((tm, LANES), lambda i: (i, 0)),
                  pl.BlockSpec((1, 1, d), lambda i: (mod_map(i), 0, 0)),
                  pl.BlockSpec((1, d), lambda i: (0, 0))],
        out_specs=pl.BlockSpec((tm, d), lambda i: (i, 0)),
        compiler_params=_params("parallel"),
        name="moe_combine",
    )(x, y4, gates, gate_mod, g_final.reshape(1, d))


def _moe(x, xn, logits, gate_mod, mod_map, weights, g_final, final_norm, tm):
    wg, wl, bg, bl, wd, bd = weights
    n, d = xn.shape
    idx_w, gates_w, pos_w, counts = _route(logits, tm)
    idx, pos = idx_w[:, :TOP_K], pos_w[:, :TOP_K]
    counts = counts[0].astype(I32)
    padded = (counts + MOE_ROWS - 1) // MOE_ROWS * MOE_ROWS
    pad_end = jnp.cumsum(padded)
    pad_start = pad_end - padded
    dest = pad_start[idx] + pos
    nblk = n * TOP_K // MOE_ROWS + N_EXPERTS
    slots = nblk * MOE_ROWS
    src = jnp.zeros((slots,), I32).at[dest.reshape(-1)].set(jnp.arange(n * TOP_K, dtype=I32) // TOP_K)
    blk_e = jnp.minimum(jnp.searchsorted(pad_end, jnp.arange(nblk, dtype=I32) * MOE_ROWS, side='right'),
                        N_EXPERTS - 1).astype(I32)
    n_used = (pad_end[-1:] // MOE_ROWS).astype(I32)
    xs = jnp.take(xn, src, axis=0)
    ys = _experts(xs, blk_e, n_used, wg, wl, bg, bl, wd, bd)
    y4 = jnp.take(ys, dest.reshape(-1), axis=0).reshape(n, TOP_K * d)
    return _combine(x, y4, gates_w, gate_mod, mod_map, g_final, final_norm, tm)


def _moe_weights(w_gate_up, b_gate_up, w_down, b_down):
    ne, d, ff2 = w_gate_up.shape
    return (w_gate_up[:, :, 0::2].astype(BF16), w_gate_up[:, :, 1::2].astype(BF16),
            b_gate_up[:, None, 0::2], b_gate_up[:, None, 1::2],
            w_down.astype(BF16), b_down[:, None, :])


def _rope_tables(n_tokens):
    rows = n_tokens // GRID_W
    row = jnp.repeat(jnp.arange(rows, dtype=I32), GRID_W).astype(F32)
    col = jnp.tile(jnp.arange(GRID_W, dtype=I32), rows).astype(F32)
    inv = ROPE_BASE ** (-jnp.arange(0, ROPE_AXIS_DIM, 2, dtype=F32) / ROPE_AXIS_DIM)
    ang_r, ang_c = row[:, None] * inv, col[:, None] * inv
    cos = jnp.concatenate([jnp.cos(ang_r)] * 2 + [jnp.cos(ang_c)] * 2, axis=-1)
    sin = jnp.concatenate([jnp.sin(ang_r)] * 2 + [jnp.sin(ang_c)] * 2, axis=-1)
    return cos, sin


def _rot_perm():
    q = ROPE_AXIS_DIM // 2
    d = np.arange(HEAD_DIM)
    first = (d % ROPE_AXIS_DIM) < q
    perm = np.where(first, d + q, d - q)
    sign = np.where(first, -1.0, 1.0).astype(np.float32)
    return perm, sign


def _rot_cols(w, n_heads):
    perm, sign = _rot_perm()
    k = w.shape[0]
    wh = w.reshape(k, n_heads, HEAD_DIM)
    return (wh[:, :, perm] * sign).reshape(k, n_heads * HEAD_DIM)


def _to_heads(t, b, n_heads):
    return t.reshape(b, -1, n_heads, HEAD_DIM).transpose(0, 2, 1, 3)


def _from_heads(t):
    b, h, tt, dh = t.shape
    return t.transpose(0, 2, 1, 3).reshape(b * tt, h * dh)


def _layer_even(xa, mod, b, s, lc, g_mix, g_ffn, w_in, w_out, gate_bias, mlstm_norm, sink, router, moe_w):
    d = xa.shape[1]
    nctx = b * lc
    tm = _tile(math.gcd(nctx, s), (512, 256, 128))
    mod_map = lambda i: jnp.where(i * tm < nctx, b, (i * tm - nctx) // s)
    sh1, sc1, gt1, sh2, sc2, gt2 = [mod[:, j][:, None, :] for j in range(6)]

    hm, hs, hk = MLSTM_HEADS * HEAD_DIM, SWA_HEADS * HEAD_DIM, SWA_KV_HEADS * HEAD_DIM
    o0 = np.cumsum([0, hm, hm, hm, hm, 4 * MLSTM_HEADS, hs, hk, hk])
    seg = lambda j: w_in[:, o0[j]:o0[j + 1]]
    w_cat = jnp.concatenate([seg(0), seg(1) * ATTN_SCALE, seg(2), seg(3),
                             seg(5), _rot_cols(seg(5), SWA_HEADS),
                             seg(6), _rot_cols(seg(6), SWA_KV_HEADS), seg(7), _pad_cols(seg(4))], axis=1).astype(BF16)
    widths = [hm, hm, hm, hm, hs, hs, hk, hk, hk, LANES]
    dts = [BF16] * 9 + [F32]
    qa, ka, va, oa, qb, qb_r, kb, kb_r, vb, gts = _in_proj(xa, g_mix, sc1, sh1, mod_map, w_cat, widths, dts, tm)

    def scan_pair(t, n_heads):
        tc, tx = _to_heads(t[:nctx], b, n_heads), _to_heads(t[nctx:], b, n_heads)
        fwd = jnp.concatenate([tc, tx], axis=2)
        bwd = jnp.concatenate([tc[:, :, ::-1], tx[:, :, ::-1]], axis=2)
        return jnp.stack([fwd, bwd])
    q2, k2, v2 = scan_pair(qa, MLSTM_HEADS), scan_pair(ka, MLSTM_HEADS), scan_pair(va, MLSTM_HEADS)
    g4 = gts[:, :4 * MLSTM_HEADS]
    gc_, gx_ = g4[:nctx].reshape(b, lc, 4, MLSTM_HEADS), g4[nctx:].reshape(b, s, 4, MLSTM_HEADS)
    def gate_seq(lo):
        fwd = jnp.concatenate([gc_[:, :, lo:lo + 2], gx_[:, :, lo:lo + 2]], axis=1)
        return fwd.reshape(b, lc + s, 2 * MLSTM_HEADS)
    g_f = gate_seq(0)
    g_b = jnp.concatenate([gc_[:, ::-1, 2:4], gx_[:, ::-1, 2:4]], axis=1).reshape(b, lc + s, 2 * MLSTM_HEADS)
    g2 = jnp.stack([g_f, g_b]).reshape(2, b, (lc + s) // MLSTM_CHUNK, MLSTM_CHUNK, 2 * MLSTM_HEADS)
    bias2 = gate_bias.astype(F32).reshape(2, 2 * MLSTM_HEADS)
    h2 = _mlstm(q2, k2, v2, g2, g2.swapaxes(-1, -2), bias2[:, None, :], bias2[:, :, None])
    hf = jnp.concatenate([_from_heads(h2[0, :, :, :lc]), _from_heads(h2[0, :, :, lc:])], axis=0)
    hb = jnp.concatenate([_from_heads(h2[1, :, :, :lc][:, :, ::-1]), _from_heads(h2[1, :, :, lc:][:, :, ::-1])], axis=0)
    mix_a = _mlstm_out(hf, hb, oa, mlstm_norm.reshape(-1), tm)

    cos, sin = _rope_tables(s)
    ones = jnp.ones((HEAD_DIM,), F32)
    g = SWA_HEADS // SWA_KV_HEADS
    heads = lambda t, n: _to_heads(t, b, n)
    q_x = _qk_prep(heads(qb[nctx:], SWA_HEADS), heads(qb_r[nctx:], SWA_HEADS), cos, sin, ones, ones,
                   norm=False, rope=True, scale=ATTN_SCALE)
    k_x = _qk_prep(heads(kb[nctx:], SWA_KV_HEADS), heads(kb_r[nctx:], SWA_KV_HEADS), cos, sin, ones, ones,
                   norm=False, rope=True, scale=1.0)
    k_c, v_c, v_x = heads(kb[:nctx], SWA_KV_HEADS), heads(vb[:nctx], SWA_KV_HEADS), heads(vb[nctx:], SWA_KV_HEADS)
    q_c = heads(qb[:nctx], SWA_HEADS)
    sink_h = sink.astype(F32).reshape(SWA_KV_HEADS, g, 1, 1)
    att_x = _swa(q_x, k_x, v_x, k_c, v_c,
                 jnp.broadcast_to(sink_h, (SWA_KV_HEADS, g, SWA_BLOCK, 1)).reshape(SWA_KV_HEADS, g * SWA_BLOCK, 1))
    att_c = _ctx_attn(q_c, k_c, v_c, jnp.broadcast_to(sink_h, (SWA_KV_HEADS, g, lc, 1)).reshape(SWA_KV_HEADS, g * lc, 1),
                      True)
    mix_b = jnp.concatenate([_from_heads(att_c), _from_heads(att_x)], axis=0)

    w_router, b_router = router
    xa, xn2, logits = _out_proj(xa, 0, mix_a, mix_b, w_out[:hm].astype(BF16), w_out[hm:].astype(BF16), gt1, g_ffn,
                                sc2, sh2, mod_map, w_router.astype(F32), b_router.astype(F32), tm)
    return _moe(xa, xn2, logits, gt2, mod_map, moe_w, g_ffn, False, tm)


def _layer_odd_last(xa, mod, b, s, lc, g_mix, g_ffn, w_in, w_out, s5_params, d_skip, w_glu, b_glu,
                    q_norm, k_norm, router, moe_w, g_final):
    nctx = b * lc
    tm = _tile(math.gcd(nctx, s), (512, 256, 128))
    mod_map = lambda i: jnp.where(i * tm < nctx, b, (i * tm - nctx) // s)
    lat_map = lambda i: i * tm // s
    sh1, sc1, gt1, sh2, sc2, gt2 = [mod[:, j][:, None, :] for j in range(6)]

    hq, hk = ATT_HEADS * HEAD_DIM, ATT_KV_HEADS * HEAD_DIM
    o1 = np.cumsum([0, S5_CHANNELS, hq, hk, hk])
    seg = lambda j: w_in[:, o1[j]:o1[j + 1]]
    kpad = lambda w: _pad_cols(w, 2 * LANES)
    w_cat = jnp.concatenate([seg(0), seg(1), _rot_cols(seg(1), ATT_HEADS), kpad(seg(2)),
                             kpad(_rot_cols(seg(2), ATT_KV_HEADS)), kpad(seg(3))], axis=1).astype(BF16)
    widths = [S5_CHANNELS, hq, hq, 2 * LANES, 2 * LANES, 2 * LANES]
    dts = [F32, BF16, BF16, BF16, BF16, BF16]
    u, q, q_r, k, k_r, v = _in_proj(xa, g_mix, sc1, sh1, mod_map, w_cat, widths, dts, tm)

    mix_a = _s5_mixer(u[nctx:].reshape(b, s, S5_CHANNELS), u[:nctx].reshape(b, lc, S5_CHANNELS),
                      _s5_tables(*s5_params), d_skip, w_glu, b_glu)

    cos, sin = _rope_tables(s)
    perm, _ = _rot_perm()
    qn, kn = q_norm.astype(F32), k_norm.astype(F32)
    heads = lambda t, n: _to_heads(t[:, :n * HEAD_DIM], b, n)
    q_x = _qk_prep(heads(q[nctx:], ATT_HEADS), heads(q_r[nctx:], ATT_HEADS), cos, sin, qn, qn[perm],
                   norm=True, rope=True, scale=ATTN_SCALE)
    k_x = _qk_prep(heads(k[nctx:], ATT_KV_HEADS), heads(k_r[nctx:], ATT_KV_HEADS), cos, sin, kn, kn[perm],
                   norm=True, rope=True, scale=1.0)
    kc_raw = heads(k[:nctx], ATT_KV_HEADS)
    k_c = _qk_prep(kc_raw, kc_raw, cos[:lc], sin[:lc], kn, kn, norm=True, rope=False, scale=1.0)
    k_all = jnp.concatenate([k_c, k_x], axis=2)
    v_all = jnp.concatenate([heads(v[:nctx], ATT_KV_HEADS), heads(v[nctx:], ATT_KV_HEADS)], axis=2)
    mix_b = _from_heads(_flash(q_x, k_all, v_all))

    w_router, b_router = router
    hs = S5_CHANNELS
    x, xn2, logits = _out_proj(xa, nctx, mix_a, mix_b, w_out[:hs].astype(BF16), w_out[hs:].astype(BF16), gt1, g_ffn,
                               sc2, sh2, lat_map, w_router.astype(F32), b_router.astype(F32), tm)
    return _moe(x, xn2, logits, gt2, lat_map, moe_w, g_final, True, tm)


def kernel(x, c, ctx, c_ctx, l0_w_mod, l0_b_mod, l0_g_mix, l0_g_ffn, l0_w_in, l0_w_out, l0_gate_bias, l0_mlstm_norm, l0_sink, l0_w_router, l0_b_router, l0_w_gate_up, l0_b_gate_up, l0_w_down, l0_b_down, l1_w_mod, l1_b_mod, l1_g_mix, l1_g_ffn, l1_w_in, l1_w_out, l1_lam_re, l1_lam_im, l1_log_dt, l1_b_re, l1_b_im, l1_c_re, l1_c_im, l1_d_skip, l1_w_glu, l1_b_glu, l1_q_norm, l1_k_norm, l1_w_router, l1_b_router, l1_w_gate_up, l1_b_gate_up, l1_w_down, l1_b_down, g_final):
    b, s, d = x.shape
    lc = ctx.shape[1]
    cond = jnp.concatenate([c, c_ctx[None, :]], axis=0)
    cond = jnp.pad(cond, ((0, (-(b + 1)) % 8), (0, 0)))
    mod0 = _silu_linear(cond, l0_w_mod, l0_b_mod)[:b + 1].reshape(b + 1, 6, d)
    mod1 = _silu_linear(cond, l1_w_mod, l1_b_mod)[:b + 1].reshape(b + 1, 6, d)

    xa = jnp.concatenate([ctx.reshape(b * lc, d), x.reshape(b * s, d)], axis=0)
    xa = _layer_even(xa, mod0, b, s, lc, l0_g_mix, l0_g_ffn, l0_w_in, l0_w_out, l0_gate_bias, l0_mlstm_norm,
                     l0_sink, (l0_w_router, l0_b_router),
                     _moe_weights(l0_w_gate_up, l0_b_gate_up, l0_w_down, l0_b_down))
    out = _layer_odd_last(xa, mod1, b, s, lc, l1_g_mix, l1_g_ffn, l1_w_in, l1_w_out,
                          (l1_lam_re, l1_lam_im, l1_log_dt, l1_b_re, l1_b_im, l1_c_re, l1_c_im),
                          l1_d_skip, l1_w_glu, l1_b_glu, l1_q_norm, l1_k_norm, (l1_w_router, l1_b_router),
                          _moe_weights(l1_w_gate_up, l1_b_gate_up, l1_w_down, l1_b_down), g_final)
    return out.reshape(b, s, d)
```

```python
import functools
import math

import jax
import jax.numpy as jnp
import numpy as np
from jax import lax
from jax.experimental import pallas as pl
from jax.experimental.pallas import tpu as pltpu

F32 = jnp.float32
BF16 = jnp.bfloat16
I32 = jnp.int32

GRID_W = 64
HEAD_DIM = 64
ATTN_SCALE = HEAD_DIM ** -0.5
ROPE_AXIS_DIM = HEAD_DIM // 2
ROPE_BASE = 10000.0
EPS = 1e-6
NEG_INF = -1e30

MLSTM_HEADS = 8
MLSTM_CHUNK = 64
SWA_HEADS = 8
SWA_KV_HEADS = 2
SWA_WINDOW = 128
SWA_BLOCK = 128
S5_CHANNELS = 256
S5_GROUP = 16
S5_GROUPS = S5_CHANNELS // S5_GROUP
S5_STATE = 64
S5_CHUNK = 64
ATT_HEADS = 12
ATT_KV_HEADS = 3
N_EXPERTS = 32
TOP_K = 4
SWIGLU_LIMIT = 7.0
SWIGLU_ALPHA = 1.702

LANES = 128
VMEM_LIMIT = 56 * 1024 * 1024
MOE_ROWS = 512
HIGHEST = lax.Precision.HIGHEST


def _params(*sem):
    return pltpu.CompilerParams(dimension_semantics=sem, vmem_limit_bytes=VMEM_LIMIT)


def _tile(n, prefs):
    for t in prefs:
        if n % t == 0:
            return t
    return n


def _pad_cols(w, mult=LANES):
    pad = (-w.shape[-1]) % mult
    if pad:
        w = jnp.pad(w, [(0, 0)] * (w.ndim - 1) + [(0, pad)])
    return w


def _linear_kernel(x_ref, w_ref, b_ref, o_ref):
    x = x_ref[...]
    x = x * jax.nn.sigmoid(x)
    o_ref[...] = jnp.dot(x, w_ref[...], precision=HIGHEST, preferred_element_type=F32) + b_ref[...]


def _silu_linear(x, w, b):
    m, k = x.shape
    n = w.shape[1]
    tn = _tile(n, (1024, 512, 256, 128))
    return pl.pallas_call(
        _linear_kernel,
        out_shape=jax.ShapeDtypeStruct((m, n), F32),
        grid=(n // tn,),
        in_specs=[pl.BlockSpec((m, k), lambda j: (0, 0)),
                  pl.BlockSpec((k, tn), lambda j: (0, j)),
                  pl.BlockSpec((1, tn), lambda j: (0, j))],
        out_specs=pl.BlockSpec((m, tn), lambda j: (0, j)),
        compiler_params=_params("arbitrary"),
        name="adaln_linear",
    )(x, w, b.reshape(1, n))


def _in_proj_kernel(x_ref, g_ref, sc_ref, sh_ref, w_ref, wt_ref, *out_refs, widths, t_widths):
    x = x_ref[...]
    xn = x * lax.rsqrt(jnp.mean(x * x, axis=-1, keepdims=True) + EPS) * g_ref[...]
    xb = (xn * (1.0 + sc_ref[0]) + sh_ref[0]).astype(BF16)
    off = 0
    for o_ref, w in zip(out_refs, widths):
        o_ref[...] = jnp.dot(xb, w_ref[:, off:off + w], preferred_element_type=F32).astype(o_ref.dtype)
        off += w
    off = 0
    for o_ref, w in zip(out_refs[len(widths):], t_widths):
        o_ref[...] = lax.dot_general(wt_ref[off:off + w, :], xb, (((1,), (1,)), ((), ())),
                                     preferred_element_type=F32).astype(o_ref.dtype)
        off += w


def _in_proj(x, g, scale, shift, mod_map, w, widths, dtypes, tm, wt=None, t_widths=(), t_dtypes=()):
    r, d = x.shape
    if wt is None:
        wt = jnp.zeros((8, d), BF16)
    return pl.pallas_call(
        functools.partial(_in_proj_kernel, widths=tuple(widths), t_widths=tuple(t_widths)),
        out_shape=([jax.ShapeDtypeStruct((r, wd), dt) for wd, dt in zip(widths, dtypes)]
                   + [jax.ShapeDtypeStruct((wd, r), dt) for wd, dt in zip(t_widths, t_dtypes)]),
        grid=(r // tm,),
        in_specs=[pl.BlockSpec((tm, d), lambda i: (i, 0)),
                  pl.BlockSpec((1, d), lambda i: (0, 0)),
                  pl.BlockSpec((1, 1, d), lambda i: (mod_map(i), 0, 0)),
                  pl.BlockSpec((1, 1, d), lambda i: (mod_map(i), 0, 0)),
                  pl.BlockSpec(w.shape, lambda i: (0, 0)),
                  pl.BlockSpec(wt.shape, lambda i: (0, 0))],
        out_specs=([pl.BlockSpec((tm, wd), lambda i: (i, 0)) for wd in widths]
                   + [pl.BlockSpec((wd, tm), lambda i: (0, i)) for wd in t_widths]),
        compiler_params=_params("parallel"),
        name="in_proj",
    )(x, g.reshape(1, d), scale, shift, w, wt)


def _qk_prep_kernel(x_ref, xr_ref, cos_ref, sin_ref, g_ref, gr_ref, o_ref, *, norm, rope, scale):
    x = x_ref[0].astype(F32)
    if norm:
        s = lax.rsqrt(jnp.mean(x * x, axis=-1, keepdims=True) + EPS)
        x = x * s * g_ref[...]
    if rope:
        xr = xr_ref[0].astype(F32)
        if norm:
            xr = xr * s * gr_ref[...]
        x = x * cos_ref[...] + xr * sin_ref[...]
    o_ref[0] = (x * scale).astype(o_ref.dtype)


def _qk_prep(x, x_rot, cos, sin, gain, gain_rot, *, norm, rope, scale):
    b, h, t, dh = x.shape
    tt = _tile(t, (512, 256, 128))
    blk = pl.BlockSpec((1, h, tt, dh), lambda i, j: (i, 0, j, 0))
    tab = pl.BlockSpec((tt, dh), lambda i, j: (j, 0))
    vec = pl.BlockSpec((1, dh), lambda i, j: (0, 0))
    return pl.pallas_call(
        functools.partial(_qk_prep_kernel, norm=norm, rope=rope, scale=scale),
        out_shape=jax.ShapeDtypeStruct(x.shape, BF16),
        grid=(b, t // tt),
        in_specs=[blk, blk, tab, tab, vec, vec],
        out_specs=blk,
        compiler_params=_params("parallel", "parallel"),
        name="qk_prep",
    )(x, x_rot, cos, sin, gain.reshape(1, dh), gain_rot.reshape(1, dh))


def _q_prep_t_kernel(x_ref, xr_ref, cos_ref, sin_ref, g_ref, gr_ref, o_ref, *, scale):
    w, tt = x_ref.shape
    nh = w // HEAD_DIM
    x = x_ref[...].astype(F32).reshape(nh, HEAD_DIM, tt)
    xr = xr_ref[...].astype(F32).reshape(nh, HEAD_DIM, tt)
    s = lax.rsqrt(jnp.mean(x * x, axis=1, keepdims=True) + EPS)
    y = (x * s * g_ref[...]) * cos_ref[...] + (xr * s * gr_ref[...]) * sin_ref[...]
    o_ref[...] = (y * scale).reshape(w, tt).astype(o_ref.dtype)


def _q_prep_t(xt, xt_rot, col0, n_cols, cos_t, sin_t, gain, gain_rot, scale):
    w = xt.shape[0]
    s = cos_t.shape[1]
    tt = _tile(math.gcd(col0, s), (512, 256, 128))
    blk = pl.BlockSpec((w, tt), lambda j: (0, j + col0 // tt))
    tab = pl.BlockSpec((HEAD_DIM, tt), lambda j: (0, j % (s // tt)))
    vec = pl.BlockSpec((HEAD_DIM, 1), lambda j: (0, 0))
    return pl.pallas_call(
        functools.partial(_q_prep_t_kernel, scale=scale),
        out_shape=jax.ShapeDtypeStruct((w, n_cols), BF16),
        grid=(n_cols // tt,),
        in_specs=[blk, blk, tab, tab, vec, vec],
        out_specs=pl.BlockSpec((w, tt), lambda j: (0, j)),
        compiler_params=_params("parallel"),
        name="q_prep_t",
    )(xt, xt_rot, cos_t, sin_t, gain.reshape(HEAD_DIM, 1), gain_rot.reshape(HEAD_DIM, 1))


def _log_sigmoid(x):
    return jnp.minimum(x, 0.0) - jnp.log(1.0 + jnp.exp(-jnp.abs(x)))


def _mlstm_kernel(qf, kf, vf, gf, qb, kb, vb, gb, bias_ref, hf_ref, hb_ref, c_scr, n_scr, m_scr):
    ch, nh, dh = MLSTM_CHUNK, MLSTM_HEADS, HEAD_DIM

    @pl.when(pl.program_id(1) == 0)
    def _():
        c_scr[...] = jnp.zeros_like(c_scr)
        n_scr[...] = jnp.zeros_like(n_scr)
        m_scr[...] = jnp.full_like(m_scr, NEG_INF)

    row = lax.broadcasted_iota(I32, (ch, ch), 0)
    col = lax.broadcasted_iota(I32, (ch, ch), 1)
    lane_lo = lax.broadcasted_iota(I32, (ch, 2 * dh), 1) < dh
    row_lo = lax.broadcasted_iota(I32, (2 * dh, 2 * dh), 0) < dh
    col_lo = lax.broadcasted_iota(I32, (2 * dh, 2 * dh), 1) < dh
    vec_lo = lax.broadcasted_iota(I32, (1, 2 * dh), 1) < dh
    pair = lambda a, b: jnp.where(lane_lo, a, b)
    refs = ((qf, kf, vf, gf, hf_ref), (qb, kb, vb, gb, hb_ref))
    jobs = [(d, p) for d in range(2) for p in range(nh // 2)]
    sl = lambda p: slice(2 * dh * p, 2 * dh * (p + 1))

    qk = {}
    for d, p in jobs:
        q2, k2 = refs[d][0][:, sl(p)], refs[d][1][:, sl(p)]
        zero = jnp.zeros_like(q2)
        q_st = jnp.concatenate([jnp.where(lane_lo, q2, zero), jnp.where(lane_lo, zero, q2)], axis=0)
        qk[d, p] = lax.dot_general(q_st, k2, (((1,), (1,)), ((), ())), preferred_element_type=F32)

    st = {}
    for d in range(2):
        seen = (col <= row) if d == 0 else (col >= row)
        tri = seen.astype(F32)
        tri_t = ((row <= col) if d == 0 else (row >= col)).astype(F32)
        last = ch - 1 if d == 0 else 0
        g = refs[d][3][...] + bias_ref[...]
        g_t = g.T
        lo = 2 * nh * d
        li_col, lf_col = g[:, lo:lo + nh], _log_sigmoid(g[:, lo + nh:lo + 2 * nh])
        li_row, lf_row = g_t[lo:lo + nh, :], _log_sigmoid(g_t[lo + nh:lo + 2 * nh, :])
        b_col = jnp.dot(tri, lf_col, precision=HIGHEST, preferred_element_type=F32)
        b_row = jnp.dot(lf_row, tri_t, precision=HIGHEST, preferred_element_type=F32)
        for h in range(nh):
            bc, br = b_col[:, h:h + 1], b_row[h:h + 1, :]
            lir, lic = li_row[h:h + 1, :], li_col[:, h:h + 1]
            b_last = br[:, last:last + 1]
            m_prev = m_scr[d, h // 2][:, (h % 2) * dh:(h % 2) * dh + 1]
            d_log = jnp.where(seen, bc - br + lir, NEG_INF)
            inter_log = bc + m_prev
            m_t = jnp.maximum(inter_log, jnp.max(d_log, axis=1, keepdims=True))
            w_log = b_last - bc + lic
            m_new = jnp.maximum(b_last + m_prev, jnp.max(w_log, axis=0, keepdims=True))
            st[d, h] = dict(dmat=jnp.exp(d_log - m_t), inter=jnp.exp(inter_log - m_t), floor=jnp.exp(-m_t),
                            m_new=m_new, decay=jnp.exp(b_last + m_prev - m_new), wn=jnp.exp(w_log - m_new))

    mm = {}
    for d, p in jobs:
        a, b = st[d, 2 * p], st[d, 2 * p + 1]
        q2, k2, v2 = refs[d][0][:, sl(p)], refs[d][1][:, sl(p)], refs[d][2][:, sl(p)]
        s = qk[d, p] * jnp.concatenate([a["dmat"], b["dmat"]], axis=0)
        kw = k2.astype(F32) * pair(a["wn"], b["wn"])
        mm[d, p] = dict(
            s_sum=jnp.sum(s, axis=1, keepdims=True), kw_sum=jnp.sum(kw, axis=0, keepdims=True),
            sv=jnp.dot(s.astype(BF16), v2, preferred_element_type=F32),
            q_c=jnp.dot(q2, c_scr[d, p].astype(BF16), preferred_element_type=F32),
            kv=lax.dot_general(kw.astype(BF16), v2, (((0,), (0,)), ((), ())), preferred_element_type=F32))

    for d, p in jobs:
        a, b, r = st[d, 2 * p], st[d, 2 * p + 1], mm[d, p]
        q2 = refs[d][0][:, sl(p)]
        n_prev = n_scr[d, p]
        num = pair(a["inter"], b["inter"]) * r["q_c"] + jnp.where(lane_lo, r["sv"][:ch], r["sv"][ch:])
        qn = q2.astype(F32) * n_prev
        qn_a = jnp.sum(jnp.where(lane_lo, qn, 0.0), axis=1, keepdims=True)
        qn_b = jnp.sum(jnp.where(lane_lo, 0.0, qn), axis=1, keepdims=True)
        den_a = jnp.maximum(jnp.abs(a["inter"] * qn_a + r["s_sum"][:ch]), a["floor"])
        den_b = jnp.maximum(jnp.abs(b["inter"] * qn_b + r["s_sum"][ch:]), b["floor"])
        refs[d][4][:, sl(p)] = num / pair(den_a, den_b)
        c_scr[d, p] = (jnp.where(row_lo, a["decay"], b["decay"]) * c_scr[d, p]
                       + jnp.where(row_lo == col_lo, r["kv"], 0.0))
        n_scr[d, p] = jnp.where(vec_lo, a["decay"], b["decay"]) * n_prev + r["kw_sum"]
        m_scr[d, p] = jnp.where(vec_lo, a["m_new"], b["m_new"])


def _mlstm(q, k, v, gates, bias, b, lc, s):
    r, w = q.shape
    ch = MLSTM_CHUNK
    ncc, ncx = lc // ch, s // ch
    base = b * ncc

    def fwd(i, c):
        return jnp.where(c < ncc, i * ncc + c, base + i * ncx + (c - ncc)), 0

    def bwd(i, c):
        return jnp.where(c < ncc, i * ncc + (ncc - 1 - c), base + i * ncx + (ncx - 1 - (c - ncc))), 0

    spec = lambda width, m: pl.BlockSpec((ch, width), m)
    npair = MLSTM_HEADS // 2
    return pl.pallas_call(
        _mlstm_kernel,
        out_shape=[jax.ShapeDtypeStruct((r, w), F32)] * 2,
        grid=(b, ncc + ncx),
        in_specs=[spec(w, fwd), spec(w, fwd), spec(w, fwd), spec(LANES, fwd),
                  spec(w, bwd), spec(w, bwd), spec(w, bwd), spec(LANES, bwd),
                  pl.BlockSpec((1, LANES), lambda i, c: (0, 0))],
        out_specs=[spec(w, fwd), spec(w, bwd)],
        scratch_shapes=[pltpu.VMEM((2, npair, 2 * HEAD_DIM, 2 * HEAD_DIM), F32),
                        pltpu.VMEM((2, npair, 1, 2 * HEAD_DIM), F32),
                        pltpu.VMEM((2, npair, 1, 2 * HEAD_DIM), F32)],
        compiler_params=_params("parallel", "arbitrary"),
        name="mlstm_scan",
    )(q, k, v, gates, q, k, v, gates, bias)


def _mlstm_out_kernel(hf_ref, hb_ref, o_ref, nrm_ref, ones_ref, y_ref):
    h = hf_ref[...] + hb_ref[...]
    ms = jnp.dot((h * h).astype(BF16), ones_ref[...], preferred_element_type=F32) * (1.0 / HEAD_DIM)
    hn = h * lax.rsqrt(ms + EPS) * nrm_ref[...]
    y_ref[...] = (jax.nn.sigmoid(o_ref[...].astype(F32)) * hn).astype(y_ref.dtype)


def _head_ones(width):
    idx = np.arange(width) // HEAD_DIM
    return jnp.asarray(idx[:, None] == idx[None, :], BF16)


def _mlstm_out(hf, hb, o, norm, tm):
    r, w = hf.shape
    blk = pl.BlockSpec((tm, w), lambda i: (i, 0))
    return pl.pallas_call(
        _mlstm_out_kernel,
        out_shape=jax.ShapeDtypeStruct((r, w), BF16),
        grid=(r // tm,),
        in_specs=[blk, blk, blk,
                  pl.BlockSpec((1, w), lambda i: (0, 0)),
                  pl.BlockSpec((w, w), lambda i: (0, 0))],
        out_specs=blk,
        compiler_params=_params("parallel"),
        name="mlstm_out",
    )(hf, hb, o, norm.reshape(1, w), _head_ones(w))


def _swa_kernel(q_ref, kp_ref, k0_ref, kn_ref, vp_ref, v0_ref, vn_ref, kc_ref, vc_ref, sink_ref, o_ref, *, seq):
    w = SWA_BLOCK
    g = q_ref.shape[1]
    i = pl.program_id(2)
    q = q_ref[0].reshape(g * w, HEAD_DIM)
    k_loc = jnp.concatenate([kp_ref[0, 0], k0_ref[0, 0], kn_ref[0, 0]], axis=0)
    v_loc = jnp.concatenate([vp_ref[0, 0], v0_ref[0, 0], vn_ref[0, 0]], axis=0)
    dn = (((1,), (1,)), ((), ()))
    s_loc = lax.dot_general(q, k_loc, dn, preferred_element_type=F32)
    s_ctx = lax.dot_general(q, kc_ref[0, 0], dn, preferred_element_type=F32)
    qpos = i * w + (lax.broadcasted_iota(I32, (g * w, 3 * w), 0) & (w - 1))
    kpos = (i - 1) * w + lax.broadcasted_iota(I32, (g * w, 3 * w), 1)
    valid = (jnp.abs(qpos - kpos) <= SWA_WINDOW) & (kpos >= 0) & (kpos < seq)
    s_loc = jnp.where(valid, s_loc, NEG_INF)
    sink = sink_ref[0]
    m = jnp.maximum(jnp.maximum(jnp.max(s_loc, axis=1, keepdims=True),
                                jnp.max(s_ctx, axis=1, keepdims=True)), sink)
    p_loc = jnp.exp(s_loc - m)
    p_ctx = jnp.exp(s_ctx - m)
    den = jnp.sum(p_loc, axis=1, keepdims=True) + jnp.sum(p_ctx, axis=1, keepdims=True) + jnp.exp(sink - m)
    o = (jnp.dot(p_loc.astype(BF16), v_loc, preferred_element_type=F32)
         + jnp.dot(p_ctx.astype(BF16), vc_ref[0, 0], preferred_element_type=F32)) / den
    o_ref[0] = o.reshape(g, w, HEAD_DIM).astype(o_ref.dtype)


def _swa(q, k, v, k_ctx, v_ctx, sink_col):
    b, nh, s, dh = q.shape
    hkv = k.shape[1]
    g = nh // hkv
    w = SWA_BLOCK
    nb = s // w
    lc = k_ctx.shape[2]
    kv = lambda f: pl.BlockSpec((1, 1, w, dh), lambda bi, hi, i: (bi, hi, f(i), 0))
    prev, own, nxt = kv(lambda i: jnp.maximum(i - 1, 0)), kv(lambda i: i), kv(lambda i: jnp.minimum(i + 1, nb - 1))
    ctx = pl.BlockSpec((1, 1, lc, dh), lambda bi, hi, i: (bi, hi, 0, 0))
    qblk = pl.BlockSpec((1, g, w, dh), lambda bi, hi, i: (bi, hi, i, 0))
    return pl.pallas_call(
        functools.partial(_swa_kernel, seq=s),
        out_shape=jax.ShapeDtypeStruct(q.shape, BF16),
        grid=(b, hkv, nb),
        in_specs=[qblk, prev, own, nxt, prev, own, nxt, ctx, ctx,
                  pl.BlockSpec((1, g * w, 1), lambda bi, hi, i: (hi, 0, 0))],
        out_specs=qblk,
        compiler_params=_params("parallel", "parallel", "arbitrary"),
        name="window_attention",
    )(q, k, k, k, v, v, v, k_ctx, v_ctx, sink_col)


def _ctx_attn_kernel(q_ref, k_ref, v_ref, sink_ref, o_ref, *, use_sink):
    g, lq, dh = q_ref.shape[1:]
    q = q_ref[0].reshape(g * lq, dh)
    s = lax.dot_general(q, k_ref[0, 0], (((1,), (1,)), ((), ())), preferred_element_type=F32) * ATTN_SCALE
    m = jnp.max(s, axis=1, keepdims=True)
    if use_sink:
        m = jnp.maximum(m, sink_ref[0])
    p = jnp.exp(s - m)
    den = jnp.sum(p, axis=1, keepdims=True)
    if use_sink:
        den = den + jnp.exp(sink_ref[0] - m)
    o = jnp.dot(p.astype(BF16), v_ref[0, 0], preferred_element_type=F32) / den
    o_ref[0] = o.reshape(g, lq, dh).astype(o_ref.dtype)


def _ctx_attn(q, k, v, sink_col, use_sink):
    b, nh, lq, dh = q.shape
    hkv = k.shape[1]
    g = nh // hkv
    qblk = pl.BlockSpec((1, g, lq, dh), lambda bi, hi: (bi, hi, 0, 0))
    kblk = pl.BlockSpec((1, 1, k.shape[2], dh), lambda bi, hi: (bi, hi, 0, 0))
    return pl.pallas_call(
        functools.partial(_ctx_attn_kernel, use_sink=use_sink),
        out_shape=jax.ShapeDtypeStruct(q.shape, BF16),
        grid=(b, hkv),
        in_specs=[qblk, kblk, kblk, pl.BlockSpec((1, g * lq, 1), lambda bi, hi: (hi, 0, 0))],
        out_specs=qblk,
        compiler_params=_params("parallel", "parallel"),
        name="context_attention",
    )(q, k, v, sink_col)


def _flash_kernel(q_ref, k_ref, v_ref, o_ref, m_scr, l_scr, acc_scr, sa_scr, sb_scr, *, tk):
    dh = HEAD_DIM
    g = q_ref.shape[0] // dh
    n = k_ref.shape[2] // tk
    m_scr[...] = jnp.full_like(m_scr, NEG_INF)
    l_scr[...] = jnp.zeros_like(l_scr)
    acc_scr[...] = jnp.zeros_like(acc_scr)

    def scores(j, dst):
        kk = k_ref[0, 0, pl.ds(pl.multiple_of(j * tk, tk), tk), :]
        for h in range(g):
            dst[h] = jnp.dot(kk, q_ref[h * dh:(h + 1) * dh, :], preferred_element_type=F32)

    def update(j, src):
        vv = v_ref[0, 0, :, pl.ds(pl.multiple_of(j * tk, tk), tk)]
        ss = [src[h] for h in range(g)]
        m_olds = [m_scr[h] for h in range(g)]
        m_news = [jnp.maximum(m_olds[h], jnp.max(ss[h], axis=0, keepdims=True)) for h in range(g)]
        ps = [jnp.exp2(ss[h] - m_news[h]) for h in range(g)]
        pvs = [jnp.dot(vv, ps[h].astype(BF16), preferred_element_type=F32) for h in range(g)]
        for h in range(g):
            alpha = jnp.exp2(m_olds[h] - m_news[h])
            l_scr[h] = alpha * l_scr[h] + jnp.sum(ps[h], axis=0, keepdims=True)
            acc_scr[h] = alpha * acc_scr[h] + pvs[h]
            m_scr[h] = m_news[h]

    scores(0, sa_scr)

    def body(i, carry):
        scores(2 * i + 1, sb_scr)
        update(2 * i, sa_scr)
        scores(2 * i + 2, sa_scr)
        update(2 * i + 1, sb_scr)
        return carry

    lax.fori_loop(0, (n - 1) // 2, body, 0)
    if n % 2 == 1:
        update(n - 1, sa_scr)
    else:
        scores(n - 1, sb_scr)
        update(n - 2, sa_scr)
        update(n - 1, sb_scr)
    out = (acc_scr[...] / l_scr[...]).reshape(g * dh, -1)
    o_ref[...] = out.T.astype(o_ref.dtype)


def _flash(qt, k, vt, b, s):
    w = qt.shape[0]
    hkv, t, dh = k.shape[1], k.shape[2], k.shape[3]
    g = w // dh // hkv
    tq = _tile(s, (256, 128))
    tk = _tile(t, (256, 128))
    nq = s // tq
    return pl.pallas_call(
        functools.partial(_flash_kernel, tk=tk),
        out_shape=jax.ShapeDtypeStruct((b * s, w), BF16),
        grid=(b, hkv, nq),
        in_specs=[pl.BlockSpec((g * dh, tq), lambda bi, hi, i: (hi, bi * nq + i)),
                  pl.BlockSpec((1, 1, t, dh), lambda bi, hi, i: (bi, hi, 0, 0)),
                  pl.BlockSpec((1, 1, dh, t), lambda bi, hi, i: (bi, hi, 0, 0))],
        out_specs=pl.BlockSpec((tq, g * dh), lambda bi, hi, i: (bi * nq + i, hi)),
        scratch_shapes=[pltpu.VMEM((g, 1, tq), F32), pltpu.VMEM((g, 1, tq), F32), pltpu.VMEM((g, dh, tq), F32),
                        pltpu.VMEM((g, tk, tq), F32), pltpu.VMEM((g, tk, tq), F32)],
        compiler_params=_params("parallel", "parallel", "arbitrary"),
        name="dense_attention",
    )(qt, k, vt)


def _bmm_kernel(a_ref, b_ref, o_ref):
    o_ref[0] = jnp.dot(a_ref[0], b_ref[0], preferred_element_type=F32).astype(o_ref.dtype)


def _group_matmul(a, bmat, name):
    g, m, k = a.shape
    n = bmat.shape[2]
    tm = _tile(m, (1024, 512, 256, 128, 64, 32, 16, 8))
    return pl.pallas_call(
        _bmm_kernel,
        out_shape=jax.ShapeDtypeStruct((g, m, n), F32),
        grid=(g, m // tm),
        in_specs=[pl.BlockSpec((1, tm, k), lambda gi, i: (gi, i, 0)),
                  pl.BlockSpec((1, k, n), lambda gi, i: (gi, 0, 0))],
        out_specs=pl.BlockSpec((1, tm, n), lambda gi, i: (gi, i, 0)),
        compiler_params=_params("parallel", "parallel"),
        name=name,
    )(a, bmat)


def _s5_scan_kernel(sre_ref, sim_ref, are_ref, aim_ref, zre_ref, zim_ref):
    nd, nsteps = sre_ref.shape[0], sre_ref.shape[1]
    for d in range(nd):
        a_re, a_im = are_ref[d], aim_ref[d]

        def body(i, carry, d=d, a_re=a_re, a_im=a_im):
            z_re, z_im = carry
            zre_ref[d, i] = z_re
            zim_ref[d, i] = z_im
            return (a_re * z_re - a_im * z_im + sre_ref[d, i], a_re * z_im + a_im * z_re + sim_ref[d, i])

        zero = jnp.zeros(sre_ref.shape[2:], F32)
        lax.fori_loop(0, nsteps, body, (zero, zero))


def _s5_scan(s_re, s_im, a_re, a_im):
    full = lambda arr: pl.BlockSpec(arr.shape, lambda i: (0,) * arr.ndim)
    return pl.pallas_call(
        _s5_scan_kernel,
        out_shape=[jax.ShapeDtypeStruct(s_re.shape, F32)] * 2,
        grid=(1,),
        in_specs=[full(s_re), full(s_im), full(a_re), full(a_im)],
        out_specs=[full(s_re), full(s_im)],
        compiler_params=_params("arbitrary"),
        name="s5_chunk_scan",
    )(s_re, s_im, a_re, a_im)


def _s5_glu_kernel(y_ref, u_ref, d_ref, w_ref, b_ref, o_ref):
    y = y_ref[...] + d_ref[...] * u_ref[...]
    y = jax.nn.gelu(y)
    gate = jnp.dot(y.astype(BF16), w_ref[...], preferred_element_type=F32) + b_ref[...]
    o_ref[...] = (y * jax.nn.sigmoid(gate)).astype(o_ref.dtype)


def _s5_glu(y, u, d_skip, w_glu, b_glu, tm):
    r, c = y.shape
    blk = pl.BlockSpec((tm, c), lambda i: (i, 0))
    vec = pl.BlockSpec((1, c), lambda i: (0, 0))
    return pl.pallas_call(
        _s5_glu_kernel,
        out_shape=jax.ShapeDtypeStruct((r, c), BF16),
        grid=(r // tm,),
        in_specs=[blk, blk, vec, pl.BlockSpec((c, c), lambda i: (0, 0)), vec],
        out_specs=blk,
        compiler_params=_params("parallel"),
        name="s5_readout_glu",
    )(y, u, d_skip.reshape(1, c), w_glu.astype(BF16), b_glu.reshape(1, c))


def _s5_tables(lam_re, lam_im, log_dt, b_re, b_im, c_re, c_im):
    ln, p, gc = S5_CHUNK, S5_STATE, S5_GROUP
    lam = lax.complex(lam_re.astype(F32), lam_im.astype(F32))
    dt = jnp.exp(log_dt.astype(F32))[..., None]
    a_bar = jnp.exp(lam * dt)
    b_scale = (a_bar - 1.0) / lam
    b_mat = lax.complex(b_re.astype(F32), b_im.astype(F32))
    c_mat = lax.complex(c_re.astype(F32), c_im.astype(F32))
    tau = jnp.arange(ln + 1, dtype=F32)
    apow = jnp.exp((lam * dt)[:, :, None, :] * tau[None, None, :, None])
    drive = b_scale[..., None] * b_mat[None]
    kern = jnp.real(jnp.einsum('gcp,dgtp,dgpe->dgtce', c_mat, apow[:, :, :ln], drive))
    kc = jnp.concatenate([kern[1, :, :0:-1], (kern[0, :, :1] + kern[1, :, :1]), kern[0, :, 1:]], axis=1)
    t_idx = jnp.arange(ln)
    lag = t_idx[None, :] - t_idx[:, None] + ln - 1
    toep = kc[:, lag]
    toep = toep.transpose(0, 1, 4, 2, 3).reshape(S5_GROUPS, ln * gc, ln * gc)
    w_f = apow[0, :, ln - 1::-1][:, :ln, :, None] * drive[0][:, None]
    w_b = apow[1, :, :ln, :, None] * drive[1][:, None]
    def m_in(wc):
        wt = wc.transpose(0, 1, 3, 2).reshape(S5_GROUPS, ln * gc, p)
        return jnp.concatenate([jnp.real(wt), jnp.imag(wt)], axis=-1)
    min_all = jnp.concatenate([m_in(w_f), m_in(w_b)], axis=-1)
    o_f = c_mat[:, None] * apow[0, :, 1:ln + 1][:, :, None, :]
    o_b = c_mat[:, None] * apow[1, :, ln:0:-1][:, :, None, :]
    def m_out(oc):
        ot = oc.transpose(0, 3, 1, 2).reshape(S5_GROUPS, p, ln * gc)
        return jnp.concatenate([jnp.real(ot), -jnp.imag(ot)], axis=1)
    rhs = jnp.concatenate([toep, m_out(o_f), m_out(o_b)], axis=1)
    a_l = apow[:, :, ln].reshape(2, 1, S5_GROUPS * p)
    return min_all.astype(BF16), rhs.astype(BF16), jnp.real(a_l), jnp.imag(a_l)


def _s5_mixer(u_x, u_c, tables, d_skip, w_glu, b_glu):
    min_all, rhs, a_re, a_im = tables
    b, s, _ = u_x.shape
    ln, p, gc, ng = S5_CHUNK, S5_STATE, S5_GROUP, S5_GROUPS

    def chunks(u):
        nc = u.shape[1] // ln
        return u.reshape(b, nc, ln, ng, gc).transpose(3, 0, 1, 2, 4).reshape(ng, b * nc, ln * gc).astype(BF16), nc

    ux, ncx = chunks(u_x)
    uc, ncc = chunks(u_c)
    sx = _group_matmul(ux, min_all, "s5_local_state").reshape(ng, b, ncx, 4, p)
    sc = _group_matmul(uc, min_all, "s5_local_state_ctx").reshape(ng, b, ncc, 4, p)

    def scan_order(part):
        f = jnp.concatenate([sc[:, :, :, part], sx[:, :, :, part]], axis=2)
        r = jnp.concatenate([sc[:, :, ::-1, part + 2], sx[:, :, ::-1, part + 2]], axis=2)
        return jnp.stack([f, r]).transpose(0, 3, 2, 1, 4).reshape(2, ncc + ncx, b, ng * p)

    z_re, z_im = _s5_scan(scan_order(0), scan_order(1), a_re, a_im)

    def latent(z, d):
        zl = z[d, ncc:]
        if d == 1:
            zl = zl[::-1]
        return zl.reshape(ncx, b, ng, p).transpose(2, 1, 0, 3).reshape(ng, b * ncx, p)

    lhs = jnp.concatenate([ux, latent(z_re, 0).astype(BF16), latent(z_im, 0).astype(BF16),
                           latent(z_re, 1).astype(BF16), latent(z_im, 1).astype(BF16)], axis=-1)
    y = _group_matmul(lhs, rhs, "s5_outputs")
    y = y.reshape(ng, b, ncx, ln, gc).transpose(1, 2, 3, 0, 4).reshape(b * s, S5_CHANNELS)
    return _s5_glu(y, u_x.reshape(b * s, S5_CHANNELS), d_skip, w_glu, b_glu, _tile(b * s, (1024, 512, 256)))


def _out_proj_kernel(x_ref, a_ref, b_ref, wa_ref, wb_ref, gt_ref, g_ref, sc_ref, sh_ref, wr_ref, br_ref,
                     xo_ref, xn_ref, lg_ref):
    y = (jnp.dot(a_ref[...], wa_ref[...], preferred_element_type=F32)
         + jnp.dot(b_ref[...], wb_ref[...], preferred_element_type=F32))
    x = x_ref[...] + gt_ref[0] * y
    xo_ref[...] = x
    xn = x * lax.rsqrt(jnp.mean(x * x, axis=-1, keepdims=True) + EPS) * g_ref[...]
    xn = xn * (1.0 + sc_ref[0]) + sh_ref[0]
    xn_ref[...] = xn.astype(BF16)
    lg_ref[...] = jnp.dot(xn, wr_ref[...], precision=HIGHEST, preferred_element_type=F32) + br_ref[...]


def _out_proj(x, row0, a, bmix, wa, wb, gate, g_ffn, scale, shift, mod_map, w_router, b_router, tm):
    r = a.shape[0]
    d = x.shape[1]
    t0 = row0 // tm
    ne = w_router.shape[1]
    row = lambda w: pl.BlockSpec((tm, w), lambda i: (i, 0))
    mod = pl.BlockSpec((1, 1, d), lambda i: (mod_map(i), 0, 0))
    full = lambda arr: pl.BlockSpec(arr.shape, lambda i: (0, 0))
    return pl.pallas_call(
        _out_proj_kernel,
        out_shape=[jax.ShapeDtypeStruct((r, d), F32), jax.ShapeDtypeStruct((r, d), BF16),
                   jax.ShapeDtypeStruct((r, ne), F32)],
        grid=(r // tm,),
        in_specs=[pl.BlockSpec((tm, d), lambda i: (i + t0, 0)), row(a.shape[1]), row(bmix.shape[1]),
                  full(wa), full(wb), mod, pl.BlockSpec((1, d), lambda i: (0, 0)), mod, mod,
                  full(w_router), pl.BlockSpec((1, ne), lambda i: (0, 0))],
        out_specs=[row(d), row(d), row(ne)],
        compiler_params=_params("parallel"),
        name="out_proj",
    )(x, a, bmix, wa, wb, gate, g_ffn.reshape(1, d), scale, shift, w_router, b_router.reshape(1, ne))


def _route_kernel(lg_ref, idx_ref, gate_ref, pos_ref, cnt_ref, carry):
    tm, ne = lg_ref.shape

    @pl.when(pl.program_id(0) == 0)
    def _():
        carry[...] = jnp.zeros_like(carry)

    work = lg_ref[...]
    lane = lax.broadcasted_iota(I32, (tm, ne), 1).astype(F32)
    out_lane = lax.broadcasted_iota(I32, (tm, LANES), 1)
    vals, hots, idx_out = [], [], jnp.zeros((tm, LANES), I32)
    for kk in range(TOP_K):
        mx = jnp.max(work, axis=1, keepdims=True)
        idx = jnp.min(jnp.where(work == mx, lane, float(ne)), axis=1, keepdims=True)
        hot = lane == idx
        work = jnp.where(hot, -jnp.inf, work)
        vals.append(mx)
        hots.append(hot)
        idx_out = jnp.where(out_lane == kk, idx.astype(I32), idx_out)
    exps = [jnp.exp(vv - vals[0]) for vv in vals]
    tot = exps[0] + exps[1] + exps[2] + exps[3]
    multi = (hots[0] | hots[1] | hots[2] | hots[3]).astype(BF16)
    r_i = lax.broadcasted_iota(I32, (tm, tm), 0)
    c_i = lax.broadcasted_iota(I32, (tm, tm), 1)
    before = (r_i > c_i).astype(BF16)
    prefix = jnp.dot(before, multi, preferred_element_type=F32) + carry[...]
    gate_out = jnp.zeros((tm, LANES), F32)
    pos_out = jnp.zeros((tm, LANES), I32)
    for kk in range(TOP_K):
        gate_out = jnp.where(out_lane == kk, exps[kk] / tot, gate_out)
        pos = jnp.sum(jnp.where(hots[kk], prefix, 0.0), axis=1, keepdims=True).astype(I32)
        pos_out = jnp.where(out_lane == kk, pos, pos_out)
    idx_ref[...] = idx_out
    gate_ref[...] = gate_out
    pos_ref[...] = pos_out
    carry[...] = carry[...] + jnp.sum(multi.astype(F32), axis=0, keepdims=True)
    cnt_ref[...] = carry[...]


def _route(logits, tm):
    n, ne = logits.shape
    wide = pl.BlockSpec((tm, LANES), lambda i: (i, 0))
    return pl.pallas_call(
        _route_kernel,
        out_shape=[jax.ShapeDtypeStruct((n, LANES), I32), jax.ShapeDtypeStruct((n, LANES), F32),
                   jax.ShapeDtypeStruct((n, LANES), I32), jax.ShapeDtypeStruct((1, ne), F32)],
        grid=(n // tm,),
        in_specs=[pl.BlockSpec((tm, ne), lambda i: (i, 0))],
        out_specs=[wide, wide, wide, pl.BlockSpec((1, ne), lambda i: (0, 0))],
        scratch_shapes=[pltpu.VMEM((1, ne), F32)],
        compiler_params=_params("arbitrary"),
        name="moe_route",
    )(logits)


def _expert_kernel(be_ref, nb_ref, x_ref, wg_ref, wl_ref, bg_ref, bl_ref, wd_ref, bd_ref, o_ref):
    i = pl.program_id(0)

    @pl.when(i < nb_ref[0])
    def _():
        x = x_ref[...]
        hg = jnp.dot(x, wg_ref[0], preferred_element_type=F32) + bg_ref[0]
        hl = jnp.dot(x, wl_ref[0], preferred_element_type=F32) + bl_ref[0]
        glu = jnp.minimum(hg, SWIGLU_LIMIT)
        lin = jnp.clip(hl, -SWIGLU_LIMIT, SWIGLU_LIMIT)
        act = glu * jax.nn.sigmoid(SWIGLU_ALPHA * glu) * (lin + 1.0)
        o_ref[...] = (jnp.dot(act.astype(BF16), wd_ref[0], preferred_element_type=F32) + bd_ref[0]).astype(o_ref.dtype)

    @pl.when(i >= nb_ref[0])
    def _():
        o_ref[...] = jnp.zeros_like(o_ref)


def _experts(xs, blk_e, n_used, wg, wl, bg, bl, wd, bd):
    n, d = xs.shape
    ff = wg.shape[2]
    nblk = n // MOE_ROWS
    wspec = lambda shape: pl.BlockSpec((1,) + shape, lambda i, be, nb: (be[i], 0, 0))
    return pl.pallas_call(
        _expert_kernel,
        out_shape=jax.ShapeDtypeStruct((n, d), BF16),
        grid_spec=pltpu.PrefetchScalarGridSpec(
            num_scalar_prefetch=2,
            grid=(nblk,),
            in_specs=[pl.BlockSpec((MOE_ROWS, d), lambda i, be, nb: (i, 0)),
                      wspec((d, ff)), wspec((d, ff)), wspec((1, ff)), wspec((1, ff)),
                      wspec((ff, d)), wspec((1, d))],
            out_specs=pl.BlockSpec((MOE_ROWS, d), lambda i, be, nb: (i, 0))),
        compiler_params=_params("arbitrary"),
        name="moe_experts",
    )(blk_e, n_used, xs, wg, wl, bg, bl, wd, bd)


def _combine_kernel(x_ref, y_ref, gate_ref, gt_ref, g_ref, o_ref, *, final_norm):
    d = x_ref.shape[1]
    gates = gate_ref[...]
    y = gates[:, 0:1] * y_ref[:, 0:d].astype(F32)
    for kk in range(1, TOP_K):
        y = y + gates[:, kk:kk + 1] * y_ref[:, kk * d:(kk + 1) * d].astype(F32)
    x = x_ref[...] + gt_ref[0] * y
    if final_norm:
        x = x * lax.rsqrt(jnp.mean(x * x, axis=-1, keepdims=True) + EPS) * g_ref[...]
    o_ref[...] = x


def _combine(x, y4, gates, gate_mod, mod_map, g_final, final_norm, tm):
    r, d = x.shape
    return pl.pallas_call(
        functools.partial(_combine_kernel, final_norm=final_norm),
        out_shape=jax.ShapeDtypeStruct((r, d), F32),
        grid=(r // tm,),
        in_specs=[pl.BlockSpec((tm, d), lambda i: (i, 0)),
                  pl.BlockSpec((tm, TOP_K * d), lambda i: (i, 0)),
                  pl.BlockSpec((tm, LANES), lambda i: (i, 0)),
                  pl.BlockSpec((1, 1, d), lambda i: (mod_map(i), 0, 0)),
                  pl.BlockSpec((1, d), lambda i: (0, 0))],
        out_specs=pl.BlockSpec((tm, d), lambda i: (i, 0)),
        compiler_params=_params("parallel"),
        name="moe_combine",
    )(x, y4, gates, gate_mod, g_final.reshape(1, d))


def _moe(x, xn, logits, gate_mod, mod_map, weights, g_final, final_norm, tm):
    wg, wl, bg, bl, wd, bd = weights
    n, d = xn.shape
    idx_w, gates_w, pos_w, counts = _route(logits, tm)
    idx, pos = idx_w[:, :TOP_K], pos_w[:, :TOP_K]
    counts = counts[0].astype(I32)
    padded = (counts + MOE_ROWS - 1) // MOE_ROWS * MOE_ROWS
    pad_end = jnp.cumsum(padded)
    pad_start = pad_end - padded
    dest = pad_start[idx] + pos
    nblk = n * TOP_K // MOE_ROWS + N_EXPERTS
    slots = nblk * MOE_ROWS
    src = jnp.zeros((slots,), I32).at[dest.reshape(-1)].set(jnp.arange(n * TOP_K, dtype=I32) // TOP_K)
    blk_row0 = jnp.arange(nblk, dtype=I32) * MOE_ROWS
    blk_e = jnp.minimum(jnp.sum((pad_end[None, :] <= blk_row0[:, None]).astype(I32), axis=1), N_EXPERTS - 1)
    n_used = (pad_end[-1:] // MOE_ROWS).astype(I32)
    xs = jnp.take(xn, src, axis=0)
    ys = _experts(xs, blk_e, n_used, wg, wl, bg, bl, wd, bd)
    y4 = jnp.take(ys, dest.reshape(-1), axis=0).reshape(n, TOP_K * d)
    return _combine(x, y4, gates_w, gate_mod, mod_map, g_final, final_norm, tm)


def _moe_weights(w_gate_up, b_gate_up, w_down, b_down):
    ne, d, ff2 = w_gate_up.shape
    return (w_gate_up[:, :, 0::2].astype(BF16), w_gate_up[:, :, 1::2].astype(BF16),
            b_gate_up[:, None, 0::2], b_gate_up[:, None, 1::2],
            w_down.astype(BF16), b_down[:, None, :])


def _rope_tables(n_tokens):
    rows = n_tokens // GRID_W
    row = jnp.repeat(jnp.arange(rows, dtype=I32), GRID_W).astype(F32)
    col = jnp.tile(jnp.arange(GRID_W, dtype=I32), rows).astype(F32)
    inv = ROPE_BASE ** (-jnp.arange(0, ROPE_AXIS_DIM, 2, dtype=F32) / ROPE_AXIS_DIM)
    ang_r, ang_c = row[:, None] * inv, col[:, None] * inv
    cos = jnp.concatenate([jnp.cos(ang_r)] * 2 + [jnp.cos(ang_c)] * 2, axis=-1)
    sin = jnp.concatenate([jnp.sin(ang_r)] * 2 + [jnp.sin(ang_c)] * 2, axis=-1)
    return cos, sin


def _rot_perm():
    q = ROPE_AXIS_DIM // 2
    d = np.arange(HEAD_DIM)
    first = (d % ROPE_AXIS_DIM) < q
    perm = np.where(first, d + q, d - q)
    sign = np.where(first, -1.0, 1.0).astype(np.float32)
    return perm, sign


def _rot_cols(w, n_heads):
    perm, sign = _rot_perm()
    k = w.shape[0]
    wh = w.reshape(k, n_heads, HEAD_DIM)
    return (wh[:, :, perm] * sign).reshape(k, n_heads * HEAD_DIM)


def _to_heads(t, b, n_heads):
    return t.reshape(b, -1, n_heads, HEAD_DIM).transpose(0, 2, 1, 3)


def _from_heads(t):
    b, h, tt, dh = t.shape
    return t.transpose(0, 2, 1, 3).reshape(b * tt, h * dh)


def _layer_even(xa, mod, b, s, lc, g_mix, g_ffn, w_in, w_out, gate_bias, mlstm_norm, sink, router, moe_w):
    d = xa.shape[1]
    nctx = b * lc
    tm = _tile(math.gcd(nctx, s), (512, 256, 128))
    mod_map = lambda i: jnp.where(i * tm < nctx, b, (i * tm - nctx) // s)
    sh1, sc1, gt1, sh2, sc2, gt2 = [mod[:, j][:, None, :] for j in range(6)]

    hm, hs, hk = MLSTM_HEADS * HEAD_DIM, SWA_HEADS * HEAD_DIM, SWA_KV_HEADS * HEAD_DIM
    o0 = np.cumsum([0, hm, hm, hm, hm, 4 * MLSTM_HEADS, hs, hk, hk])
    seg = lambda j: w_in[:, o0[j]:o0[j + 1]]
    w_cat = jnp.concatenate([seg(0), seg(1) * ATTN_SCALE, seg(2), seg(3),
                             seg(5), _rot_cols(seg(5), SWA_HEADS),
                             seg(6), _rot_cols(seg(6), SWA_KV_HEADS), seg(7), _pad_cols(seg(4))], axis=1).astype(BF16)
    widths = [hm, hm, hm, hm, hs, hs, hk, hk, hk, LANES]
    dts = [BF16] * 9 + [F32]
    qa, ka, va, oa, qb, qb_r, kb, kb_r, vb, gts = _in_proj(xa, g_mix, sc1, sh1, mod_map, w_cat, widths, dts, tm)

    hf, hb = _mlstm(qa, ka, va, gts, _pad_cols(gate_bias.astype(F32).reshape(1, -1)), b, lc, s)
    mix_a = _mlstm_out(hf, hb, oa, mlstm_norm.reshape(-1), tm)

    cos, sin = _rope_tables(s)
    ones = jnp.ones((HEAD_DIM,), F32)
    g = SWA_HEADS // SWA_KV_HEADS
    heads = lambda t, n: _to_heads(t, b, n)
    q_x = _qk_prep(heads(qb[nctx:], SWA_HEADS), heads(qb_r[nctx:], SWA_HEADS), cos, sin, ones, ones,
                   norm=False, rope=True, scale=ATTN_SCALE)
    k_x = _qk_prep(heads(kb[nctx:], SWA_KV_HEADS), heads(kb_r[nctx:], SWA_KV_HEADS), cos, sin, ones, ones,
                   norm=False, rope=True, scale=1.0)
    k_c, v_c, v_x = heads(kb[:nctx], SWA_KV_HEADS), heads(vb[:nctx], SWA_KV_HEADS), heads(vb[nctx:], SWA_KV_HEADS)
    q_c = heads(qb[:nctx], SWA_HEADS)
    sink_h = sink.astype(F32).reshape(SWA_KV_HEADS, g, 1, 1)
    att_x = _swa(q_x, k_x, v_x, k_c, v_c,
                 jnp.broadcast_to(sink_h, (SWA_KV_HEADS, g, SWA_BLOCK, 1)).reshape(SWA_KV_HEADS, g * SWA_BLOCK, 1))
    att_c = _ctx_attn(q_c, k_c, v_c, jnp.broadcast_to(sink_h, (SWA_KV_HEADS, g, lc, 1)).reshape(SWA_KV_HEADS, g * lc, 1),
                      True)
    mix_b = jnp.concatenate([_from_heads(att_c), _from_heads(att_x)], axis=0)

    w_router, b_router = router
    xa, xn2, logits = _out_proj(xa, 0, mix_a, mix_b, w_out[:hm].astype(BF16), w_out[hm:].astype(BF16), gt1, g_ffn,
                                sc2, sh2, mod_map, w_router.astype(F32), b_router.astype(F32), tm)
    return _moe(xa, xn2, logits, gt2, mod_map, moe_w, g_ffn, False, tm)


def _layer_odd_last(xa, mod, b, s, lc, g_mix, g_ffn, w_in, w_out, s5_params, d_skip, w_glu, b_glu,
                    q_norm, k_norm, router, moe_w, g_final):
    nctx = b * lc
    tm = _tile(math.gcd(nctx, s), (512, 256, 128))
    mod_map = lambda i: jnp.where(i * tm < nctx, b, (i * tm - nctx) // s)
    lat_map = lambda i: i * tm // s
    sh1, sc1, gt1, sh2, sc2, gt2 = [mod[:, j][:, None, :] for j in range(6)]

    hq, hk = ATT_HEADS * HEAD_DIM, ATT_KV_HEADS * HEAD_DIM
    o1 = np.cumsum([0, S5_CHANNELS, hq, hk, hk])
    seg = lambda j: w_in[:, o1[j]:o1[j + 1]]
    kpad = lambda w: _pad_cols(w, 2 * LANES)
    w_cat = jnp.concatenate([seg(0), kpad(seg(2)), kpad(_rot_cols(seg(2), ATT_KV_HEADS)), kpad(seg(3))],
                            axis=1).astype(BF16)
    w_q = jnp.concatenate([seg(1), _rot_cols(seg(1), ATT_HEADS)], axis=1).T.astype(BF16)
    widths = [S5_CHANNELS, 2 * LANES, 2 * LANES, 2 * LANES]
    dts = [F32, BF16, BF16, BF16]
    u, k, k_r, v, qt, qt_r = _in_proj(xa, g_mix, sc1, sh1, mod_map, w_cat, widths, dts, tm,
                                      wt=w_q, t_widths=[hq, hq], t_dtypes=[BF16, BF16])

    mix_a = _s5_mixer(u[nctx:].reshape(b, s, S5_CHANNELS), u[:nctx].reshape(b, lc, S5_CHANNELS),
                      _s5_tables(*s5_params), d_skip, w_glu, b_glu)

    cos, sin = _rope_tables(s)
    perm, _ = _rot_perm()
    qn, kn = q_norm.astype(F32), k_norm.astype(F32)
    heads = lambda t, n: _to_heads(t[:, :n * HEAD_DIM], b, n)
    q_x = _q_prep_t(qt, qt_r, nctx, b * s, cos.T, sin.T, qn, qn[perm], ATTN_SCALE * math.log2(math.e))
    k_x = _qk_prep(heads(k[nctx:], ATT_KV_HEADS), heads(k_r[nctx:], ATT_KV_HEADS), cos, sin, kn, kn[perm],
                   norm=True, rope=True, scale=1.0)
    kc_raw = heads(k[:nctx], ATT_KV_HEADS)
    k_c = _qk_prep(kc_raw, kc_raw, cos[:lc], sin[:lc], kn, kn, norm=True, rope=False, scale=1.0)
    k_all = jnp.concatenate([k_c, k_x], axis=2)
    v_all = jnp.concatenate([heads(v[:nctx], ATT_KV_HEADS), heads(v[nctx:], ATT_KV_HEADS)], axis=2)
    mix_b = _flash(q_x, k_all, v_all.swapaxes(-1, -2), b, s)

    w_router, b_router = router
    hs = S5_CHANNELS
    x, xn2, logits = _out_proj(xa, nctx, mix_a, mix_b, w_out[:hs].astype(BF16), w_out[hs:].astype(BF16), gt1, g_ffn,
                               sc2, sh2, lat_map, w_router.astype(F32), b_router.astype(F32), tm)
    return _moe(x, xn2, logits, gt2, lat_map, moe_w, g_final, True, tm)


def kernel(x, c, ctx, c_ctx, l0_w_mod, l0_b_mod, l0_g_mix, l0_g_ffn, l0_w_in, l0_w_out, l0_gate_bias, l0_mlstm_norm, l0_sink, l0_w_router, l0_b_router, l0_w_gate_up, l0_b_gate_up, l0_w_down, l0_b_down, l1_w_mod, l1_b_mod, l1_g_mix, l1_g_ffn, l1_w_in, l1_w_out, l1_lam_re, l1_lam_im, l1_log_dt, l1_b_re, l1_b_im, l1_c_re, l1_c_im, l1_d_skip, l1_w_glu, l1_b_glu, l1_q_norm, l1_k_norm, l1_w_router, l1_b_router, l1_w_gate_up, l1_b_gate_up, l1_w_down, l1_b_down, g_final):
    b, s, d = x.shape
    lc = ctx.shape[1]
    cond = jnp.concatenate([c, c_ctx[None, :]], axis=0)
    cond = jnp.pad(cond, ((0, (-(b + 1)) % 8), (0, 0)))
    mod0 = _silu_linear(cond, l0_w_mod, l0_b_mod)[:b + 1].reshape(b + 1, 6, d)
    mod1 = _silu_linear(cond, l1_w_mod, l1_b_mod)[:b + 1].reshape(b + 1, 6, d)

    xa = jnp.concatenate([ctx.reshape(b * lc, d), x.reshape(b * s, d)], axis=0)
    xa = _layer_even(xa, mod0, b, s, lc, l0_g_mix, l0_g_ffn, l0_w_in, l0_w_out, l0_gate_bias, l0_mlstm_norm,
                     l0_sink, (l0_w_router, l0_b_router),
                     _moe_weights(l0_w_gate_up, l0_b_gate_up, l0_w_down, l0_b_down))
    out = _layer_odd_last(xa, mod1, b, s, lc, l1_g_mix, l1_g_ffn, l1_w_in, l1_w_out,
                          (l1_lam_re, l1_lam_im, l1_log_dt, l1_b_re, l1_b_im, l1_c_re, l1_c_im),
                          l1_d_skip, l1_w_glu, l1_b_glu, l1_q_norm, l1_k_norm, (l1_w_router, l1_b_router),
                          _moe_weights(l1_w_gate_up, l1_b_gate_up, l1_w_down, l1_b_down), g_final)
    return out.reshape(b, s, d)
```

```python
import functools
import math

import jax
import jax.numpy as jnp
import numpy as np
from jax import lax
from jax.experimental import pallas as pl
from jax.experimental.pallas import tpu as pltpu

F32 = jnp.float32
BF16 = jnp.bfloat16
I32 = jnp.int32

GRID_W = 64
HEAD_DIM = 64
ATTN_SCALE = HEAD_DIM ** -0.5
ROPE_AXIS_DIM = HEAD_DIM // 2
ROPE_BASE = 10000.0
EPS = 1e-6
NEG_INF = -1e30

MLSTM_HEADS = 8
MLSTM_CHUNK = 64
SWA_HEADS = 8
SWA_KV_HEADS = 2
SWA_WINDOW = 128
SWA_BLOCK = 128
S5_CHANNELS = 256
S5_GROUP = 16
S5_GROUPS = S5_CHANNELS // S5_GROUP
S5_STATE = 64
S5_CHUNK = 64
ATT_HEADS = 12
ATT_KV_HEADS = 3
N_EXPERTS = 32
TOP_K = 4
SWIGLU_LIMIT = 7.0
SWIGLU_ALPHA = 1.702

LANES = 128
VMEM_LIMIT = 56 * 1024 * 1024
MOE_ROWS = 512
HIGHEST = lax.Precision.HIGHEST


def _params(*sem):
    return pltpu.CompilerParams(dimension_semantics=sem, vmem_limit_bytes=VMEM_LIMIT)


def _tile(n, prefs):
    for t in prefs:
        if n % t == 0:
            return t
    return n


def _pad_cols(w, mult=LANES):
    pad = (-w.shape[-1]) % mult
    if pad:
        w = jnp.pad(w, [(0, 0)] * (w.ndim - 1) + [(0, pad)])
    return w


def _linear_kernel(x_ref, w_ref, b_ref, o_ref):
    x = x_ref[...]
    x = x * jax.nn.sigmoid(x)
    o_ref[...] = jnp.dot(x, w_ref[...], precision=HIGHEST, preferred_element_type=F32) + b_ref[...]


def _silu_linear(x, w, b):
    m, k = x.shape
    n = w.shape[1]
    tn = _tile(n, (1024, 512, 256, 128))
    return pl.pallas_call(
        _linear_kernel,
        out_shape=jax.ShapeDtypeStruct((m, n), F32),
        grid=(n // tn,),
        in_specs=[pl.BlockSpec((m, k), lambda j: (0, 0)),
                  pl.BlockSpec((k, tn), lambda j: (0, j)),
                  pl.BlockSpec((1, tn), lambda j: (0, j))],
        out_specs=pl.BlockSpec((m, tn), lambda j: (0, j)),
        compiler_params=_params("arbitrary"),
        name="adaln_linear",
    )(x, w, b.reshape(1, n))


def _in_proj_kernel(x_ref, g_ref, sc_ref, sh_ref, w_ref, wt_ref, *out_refs, widths, t_widths):
    x = x_ref[...]
    xn = x * lax.rsqrt(jnp.mean(x * x, axis=-1, keepdims=True) + EPS) * g_ref[...]
    xb = (xn * (1.0 + sc_ref[0]) + sh_ref[0]).astype(BF16)
    off = 0
    for o_ref, w in zip(out_refs, widths):
        o_ref[...] = jnp.dot(xb, w_ref[:, off:off + w], preferred_element_type=F32).astype(o_ref.dtype)
        off += w
    off = 0
    for o_ref, w in zip(out_refs[len(widths):], t_widths):
        o_ref[...] = lax.dot_general(wt_ref[off:off + w, :], xb, (((1,), (1,)), ((), ())),
                                     preferred_element_type=F32).astype(o_ref.dtype)
        off += w


def _in_proj(x, g, scale, shift, mod_map, w, widths, dtypes, tm, wt=None, t_widths=(), t_dtypes=()):
    r, d = x.shape
    if wt is None:
        wt = jnp.zeros((8, d), BF16)
    return pl.pallas_call(
        functools.partial(_in_proj_kernel, widths=tuple(widths), t_widths=tuple(t_widths)),
        out_shape=([jax.ShapeDtypeStruct((r, wd), dt) for wd, dt in zip(widths, dtypes)]
                   + [jax.ShapeDtypeStruct((wd, r), dt) for wd, dt in zip(t_widths, t_dtypes)]),
        grid=(r // tm,),
        in_specs=[pl.BlockSpec((tm, d), lambda i: (i, 0)),
                  pl.BlockSpec((1, d), lambda i: (0, 0)),
                  pl.BlockSpec((1, 1, d), lambda i: (mod_map(i), 0, 0)),
                  pl.BlockSpec((1, 1, d), lambda i: (mod_map(i), 0, 0)),
                  pl.BlockSpec(w.shape, lambda i: (0, 0)),
                  pl.BlockSpec(wt.shape, lambda i: (0, 0))],
        out_specs=([pl.BlockSpec((tm, wd), lambda i: (i, 0)) for wd in widths]
                   + [pl.BlockSpec((wd, tm), lambda i: (0, i)) for wd in t_widths]),
        compiler_params=_params("parallel"),
        name="in_proj",
    )(x, g.reshape(1, d), scale, shift, w, wt)


def _qk_prep_kernel(x_ref, xr_ref, cos_ref, sin_ref, g_ref, gr_ref, o_ref, *, norm, rope, scale):
    x = x_ref[0].astype(F32)
    if norm:
        s = lax.rsqrt(jnp.mean(x * x, axis=-1, keepdims=True) + EPS)
        x = x * s * g_ref[...]
    if rope:
        xr = xr_ref[0].astype(F32)
        if norm:
            xr = xr * s * gr_ref[...]
        x = x * cos_ref[...] + xr * sin_ref[...]
    o_ref[0] = (x * scale).astype(o_ref.dtype)


def _qk_prep(x, x_rot, cos, sin, gain, gain_rot, *, norm, rope, scale):
    b, h, t, dh = x.shape
    tt = _tile(t, (512, 256, 128))
    blk = pl.BlockSpec((1, h, tt, dh), lambda i, j: (i, 0, j, 0))
    tab = pl.BlockSpec((tt, dh), lambda i, j: (j, 0))
    vec = pl.BlockSpec((1, dh), lambda i, j: (0, 0))
    return pl.pallas_call(
        functools.partial(_qk_prep_kernel, norm=norm, rope=rope, scale=scale),
        out_shape=jax.ShapeDtypeStruct(x.shape, BF16),
        grid=(b, t // tt),
        in_specs=[blk, blk, tab, tab, vec, vec],
        out_specs=blk,
        compiler_params=_params("parallel", "parallel"),
        name="qk_prep",
    )(x, x_rot, cos, sin, gain.reshape(1, dh), gain_rot.reshape(1, dh))


def _q_prep_t_kernel(x_ref, xr_ref, cos_ref, sin_ref, g_ref, gr_ref, o_ref, *, norm, scale):
    w, tt = x_ref.shape
    nh = w // HEAD_DIM
    x = x_ref[...].astype(F32).reshape(nh, HEAD_DIM, tt)
    xr = xr_ref[...].astype(F32).reshape(nh, HEAD_DIM, tt)
    if norm:
        s = lax.rsqrt(jnp.mean(x * x, axis=1, keepdims=True) + EPS)
        x, xr = x * s * g_ref[...], xr * s * gr_ref[...]
    y = x * cos_ref[...] + xr * sin_ref[...]
    o_ref[...] = (y * scale).reshape(w, tt).astype(o_ref.dtype)


def _q_prep_t(xt, xt_rot, col0, n_cols, cos_t, sin_t, gain, gain_rot, scale, norm=True):
    w = xt.shape[0]
    s = cos_t.shape[1]
    tt = _tile(math.gcd(col0, s), (512, 256, 128))
    blk = pl.BlockSpec((w, tt), lambda j: (0, j + col0 // tt))
    tab = pl.BlockSpec((HEAD_DIM, tt), lambda j: (0, j % (s // tt)))
    vec = pl.BlockSpec((HEAD_DIM, 1), lambda j: (0, 0))
    return pl.pallas_call(
        functools.partial(_q_prep_t_kernel, norm=norm, scale=scale),
        out_shape=jax.ShapeDtypeStruct((w, n_cols), BF16),
        grid=(n_cols // tt,),
        in_specs=[blk, blk, tab, tab, vec, vec],
        out_specs=pl.BlockSpec((w, tt), lambda j: (0, j)),
        compiler_params=_params("parallel"),
        name="q_prep_t",
    )(xt, xt_rot, cos_t, sin_t, gain.reshape(HEAD_DIM, 1), gain_rot.reshape(HEAD_DIM, 1))


def _log_sigmoid(x):
    return jnp.minimum(x, 0.0) - jnp.log(1.0 + jnp.exp(-jnp.abs(x)))


def _mlstm_kernel(qf, kf, vf, gf, qb, kb, vb, gb, bias_ref, hf_ref, hb_ref, c_scr, n_scr, m_scr):
    ch, nh, dh = MLSTM_CHUNK, MLSTM_HEADS, HEAD_DIM

    @pl.when(pl.program_id(1) == 0)
    def _():
        c_scr[...] = jnp.zeros_like(c_scr)
        n_scr[...] = jnp.zeros_like(n_scr)
        m_scr[...] = jnp.full_like(m_scr, NEG_INF)

    row = lax.broadcasted_iota(I32, (ch, ch), 0)
    col = lax.broadcasted_iota(I32, (ch, ch), 1)
    lane_lo = lax.broadcasted_iota(I32, (ch, 2 * dh), 1) < dh
    row_lo = lax.broadcasted_iota(I32, (2 * dh, 2 * dh), 0) < dh
    col_lo = lax.broadcasted_iota(I32, (2 * dh, 2 * dh), 1) < dh
    vec_lo = lax.broadcasted_iota(I32, (1, 2 * dh), 1) < dh
    pair = lambda a, b: jnp.where(lane_lo, a, b)
    refs = ((qf, kf, vf, gf, hf_ref), (qb, kb, vb, gb, hb_ref))
    jobs = [(d, p) for d in range(2) for p in range(nh // 2)]
    sl = lambda p: slice(2 * dh * p, 2 * dh * (p + 1))

    qk = {}
    for d, p in jobs:
        q2, k2 = refs[d][0][:, sl(p)], refs[d][1][:, sl(p)]
        zero = jnp.zeros_like(q2)
        q_st = jnp.concatenate([jnp.where(lane_lo, q2, zero), jnp.where(lane_lo, zero, q2)], axis=0)
        qk[d, p] = lax.dot_general(q_st, k2, (((1,), (1,)), ((), ())), preferred_element_type=F32)

    st = {}
    for d in range(2):
        seen = (col <= row) if d == 0 else (col >= row)
        tri = seen.astype(F32)
        tri_t = ((row <= col) if d == 0 else (row >= col)).astype(F32)
        last = ch - 1 if d == 0 else 0
        g = refs[d][3][...] + bias_ref[...]
        g_t = g.T
        lo = 2 * nh * d
        li_col, lf_col = g[:, lo:lo + nh], _log_sigmoid(g[:, lo + nh:lo + 2 * nh])
        li_row, lf_row = g_t[lo:lo + nh, :], _log_sigmoid(g_t[lo + nh:lo + 2 * nh, :])
        b_col = jnp.dot(tri, lf_col, precision=HIGHEST, preferred_element_type=F32)
        b_row = jnp.dot(lf_row, tri_t, precision=HIGHEST, preferred_element_type=F32)
        for h in range(nh):
            bc, br = b_col[:, h:h + 1], b_row[h:h + 1, :]
            lir, lic = li_row[h:h + 1, :], li_col[:, h:h + 1]
            b_last = br[:, last:last + 1]
            m_prev = m_scr[d, h // 2][:, (h % 2) * dh:(h % 2) * dh + 1]
            d_log = jnp.where(seen, bc - br + lir, NEG_INF)
            inter_log = bc + m_prev
            m_t = jnp.maximum(inter_log, jnp.max(d_log, axis=1, keepdims=True))
            w_log = b_last - bc + lic
            m_new = jnp.maximum(b_last + m_prev, jnp.max(w_log, axis=0, keepdims=True))
            st[d, h] = dict(dmat=jnp.exp(d_log - m_t), inter=jnp.exp(inter_log - m_t), floor=jnp.exp(-m_t),
                            m_new=m_new, decay=jnp.exp(b_last + m_prev - m_new), wn=jnp.exp(w_log - m_new))

    mm = {}
    for d, p in jobs:
        a, b = st[d, 2 * p], st[d, 2 * p + 1]
        q2, k2, v2 = refs[d][0][:, sl(p)], refs[d][1][:, sl(p)], refs[d][2][:, sl(p)]
        s = qk[d, p] * jnp.concatenate([a["dmat"], b["dmat"]], axis=0)
        kw = k2.astype(F32) * pair(a["wn"], b["wn"])
        mm[d, p] = dict(
            s_sum=jnp.sum(s, axis=1, keepdims=True), kw_sum=jnp.sum(kw, axis=0, keepdims=True),
            sv=jnp.dot(s.astype(BF16), v2, preferred_element_type=F32),
            q_c=jnp.dot(q2, c_scr[d, p].astype(BF16), preferred_element_type=F32),
            kv=lax.dot_general(kw.astype(BF16), v2, (((0,), (0,)), ((), ())), preferred_element_type=F32))

    for d, p in jobs:
        a, b, r = st[d, 2 * p], st[d, 2 * p + 1], mm[d, p]
        q2 = refs[d][0][:, sl(p)]
        n_prev = n_scr[d, p]
        num = pair(a["inter"], b["inter"]) * r["q_c"] + jnp.where(lane_lo, r["sv"][:ch], r["sv"][ch:])
        qn = q2.astype(F32) * n_prev
        qn_a = jnp.sum(jnp.where(lane_lo, qn, 0.0), axis=1, keepdims=True)
        qn_b = jnp.sum(jnp.where(lane_lo, 0.0, qn), axis=1, keepdims=True)
        den_a = jnp.maximum(jnp.abs(a["inter"] * qn_a + r["s_sum"][:ch]), a["floor"])
        den_b = jnp.maximum(jnp.abs(b["inter"] * qn_b + r["s_sum"][ch:]), b["floor"])
        refs[d][4][:, sl(p)] = num / pair(den_a, den_b)
        c_scr[d, p] = (jnp.where(row_lo, a["decay"], b["decay"]) * c_scr[d, p]
                       + jnp.where(row_lo == col_lo, r["kv"], 0.0))
        n_scr[d, p] = jnp.where(vec_lo, a["decay"], b["decay"]) * n_prev + r["kw_sum"]
        m_scr[d, p] = jnp.where(vec_lo, a["m_new"], b["m_new"])


def _mlstm(q, k, v, gates, bias, b, lc, s):
    r, w = q.shape
    ch = MLSTM_CHUNK
    ncc, ncx = lc // ch, s // ch
    base = b * ncc

    def fwd(i, c):
        return jnp.where(c < ncc, i * ncc + c, base + i * ncx + (c - ncc)), 0

    def bwd(i, c):
        return jnp.where(c < ncc, i * ncc + (ncc - 1 - c), base + i * ncx + (ncx - 1 - (c - ncc))), 0

    spec = lambda width, m: pl.BlockSpec((ch, width), m)
    npair = MLSTM_HEADS // 2
    return pl.pallas_call(
        _mlstm_kernel,
        out_shape=[jax.ShapeDtypeStruct((r, w), F32)] * 2,
        grid=(b, ncc + ncx),
        in_specs=[spec(w, fwd), spec(w, fwd), spec(w, fwd), spec(LANES, fwd),
                  spec(w, bwd), spec(w, bwd), spec(w, bwd), spec(LANES, bwd),
                  pl.BlockSpec((1, LANES), lambda i, c: (0, 0))],
        out_specs=[spec(w, fwd), spec(w, bwd)],
        scratch_shapes=[pltpu.VMEM((2, npair, 2 * HEAD_DIM, 2 * HEAD_DIM), F32),
                        pltpu.VMEM((2, npair, 1, 2 * HEAD_DIM), F32),
                        pltpu.VMEM((2, npair, 1, 2 * HEAD_DIM), F32)],
        compiler_params=_params("parallel", "arbitrary"),
        name="mlstm_scan",
    )(q, k, v, gates, q, k, v, gates, bias)


def _mlstm_out_kernel(hf_ref, hb_ref, o_ref, nrm_ref, ones_ref, y_ref):
    h = hf_ref[...] + hb_ref[...]
    ms = jnp.dot((h * h).astype(BF16), ones_ref[...], preferred_element_type=F32) * (1.0 / HEAD_DIM)
    hn = h * lax.rsqrt(ms + EPS) * nrm_ref[...]
    y_ref[...] = (jax.nn.sigmoid(o_ref[...].astype(F32)) * hn).astype(y_ref.dtype)


def _head_ones(width):
    idx = np.arange(width) // HEAD_DIM
    return jnp.asarray(idx[:, None] == idx[None, :], BF16)


def _mlstm_out(hf, hb, o, norm, tm):
    r, w = hf.shape
    blk = pl.BlockSpec((tm, w), lambda i: (i, 0))
    return pl.pallas_call(
        _mlstm_out_kernel,
        out_shape=jax.ShapeDtypeStruct((r, w), BF16),
        grid=(r // tm,),
        in_specs=[blk, blk, blk,
                  pl.BlockSpec((1, w), lambda i: (0, 0)),
                  pl.BlockSpec((w, w), lambda i: (0, 0))],
        out_specs=blk,
        compiler_params=_params("parallel"),
        name="mlstm_out",
    )(hf, hb, o, norm.reshape(1, w), _head_ones(w))


def _swa_kernel(q_ref, k0, k1, k2, k3, v0, v1, v2, v3, kc_ref, vc_ref, sink_ref, o_ref, *, seq):
    w, dh = SWA_BLOCK, HEAD_DIM
    tq = q_ref.shape[1]
    g = q_ref.shape[0] // dh
    i = pl.program_id(2)
    k_loc = jnp.concatenate([k0[0, 0], k1[0, 0], k2[0, 0], k3[0, 0]], axis=0)
    v_loc = jnp.concatenate([v0[0, 0], v1[0, 0], v2[0, 0], v3[0, 0]], axis=1)
    kpos = (2 * i - 1) * w + lax.broadcasted_iota(I32, (4 * w, tq), 0)
    qpos = i * tq + lax.broadcasted_iota(I32, (4 * w, tq), 1)
    valid = (jnp.abs(qpos - kpos) <= SWA_WINDOW) & (kpos >= 0) & (kpos < seq)
    qs = [q_ref[h * dh:(h + 1) * dh, :] for h in range(g)]
    s_loc = [jnp.where(valid, jnp.dot(k_loc, qs[h], preferred_element_type=F32), NEG_INF) for h in range(g)]
    s_ctx = [jnp.dot(kc_ref[0, 0], qs[h], preferred_element_type=F32) for h in range(g)]
    outs = []
    for h in range(g):
        sink = sink_ref[0, h:h + 1, :]
        m = jnp.maximum(jnp.maximum(jnp.max(s_loc[h], axis=0, keepdims=True),
                                    jnp.max(s_ctx[h], axis=0, keepdims=True)), sink)
        p_loc = jnp.exp2(s_loc[h] - m)
        p_ctx = jnp.exp2(s_ctx[h] - m)
        den = jnp.sum(p_loc, axis=0, keepdims=True) + jnp.sum(p_ctx, axis=0, keepdims=True) + jnp.exp2(sink - m)
        o = (jnp.dot(v_loc, p_loc.astype(BF16), preferred_element_type=F32)
             + jnp.dot(vc_ref[0, 0], p_ctx.astype(BF16), preferred_element_type=F32))
        outs.append(o / den)
    o_ref[...] = jnp.concatenate(outs, axis=0).T.astype(o_ref.dtype)


def _swa(qt, k, vt, k_ctx, vt_ctx, sink2, b, s):
    hkv, dh = k.shape[1], k.shape[3]
    nh = qt.shape[0] // dh
    g = nh // hkv
    w = SWA_BLOCK
    tq = 2 * w
    nb, nq = s // w, s // tq
    lc = k_ctx.shape[2]
    clampi = lambda j: jnp.clip(j, 0, nb - 1)
    kspec = lambda o: pl.BlockSpec((1, 1, w, dh), lambda bi, hi, i: (bi, hi, clampi(2 * i + o), 0))
    vspec = lambda o: pl.BlockSpec((1, 1, dh, w), lambda bi, hi, i: (bi, hi, 0, clampi(2 * i + o)))
    return pl.pallas_call(
        functools.partial(_swa_kernel, seq=s),
        out_shape=jax.ShapeDtypeStruct((b * s, nh * dh), BF16),
        grid=(b, hkv, nq),
        in_specs=[pl.BlockSpec((g * dh, tq), lambda bi, hi, i: (hi, bi * nq + i)),
                  kspec(-1), kspec(0), kspec(1), kspec(2), vspec(-1), vspec(0), vspec(1), vspec(2),
                  pl.BlockSpec((1, 1, lc, dh), lambda bi, hi, i: (bi, hi, 0, 0)),
                  pl.BlockSpec((1, 1, dh, lc), lambda bi, hi, i: (bi, hi, 0, 0)),
                  pl.BlockSpec((1, g, tq), lambda bi, hi, i: (hi, 0, 0))],
        out_specs=pl.BlockSpec((tq, g * dh), lambda bi, hi, i: (bi * nq + i, hi)),
        compiler_params=_params("parallel", "parallel", "arbitrary"),
        name="window_attention",
    )(qt, k, k, k, k, vt, vt, vt, vt, k_ctx, vt_ctx, sink2)


def _ctx_attn_kernel(q_ref, k_ref, v_ref, sink_ref, o_ref, *, use_sink):
    g, lq, dh = q_ref.shape[1:]
    q = q_ref[0].reshape(g * lq, dh)
    s = lax.dot_general(q, k_ref[0, 0], (((1,), (1,)), ((), ())), preferred_element_type=F32) * ATTN_SCALE
    m = jnp.max(s, axis=1, keepdims=True)
    if use_sink:
        m = jnp.maximum(m, sink_ref[0])
    p = jnp.exp(s - m)
    den = jnp.sum(p, axis=1, keepdims=True)
    if use_sink:
        den = den + jnp.exp(sink_ref[0] - m)
    o = jnp.dot(p.astype(BF16), v_ref[0, 0], preferred_element_type=F32) / den
    o_ref[0] = o.reshape(g, lq, dh).astype(o_ref.dtype)


def _ctx_attn(q, k, v, sink_col, use_sink):
    b, nh, lq, dh = q.shape
    hkv = k.shape[1]
    g = nh // hkv
    qblk = pl.BlockSpec((1, g, lq, dh), lambda bi, hi: (bi, hi, 0, 0))
    kblk = pl.BlockSpec((1, 1, k.shape[2], dh), lambda bi, hi: (bi, hi, 0, 0))
    return pl.pallas_call(
        functools.partial(_ctx_attn_kernel, use_sink=use_sink),
        out_shape=jax.ShapeDtypeStruct(q.shape, BF16),
        grid=(b, hkv),
        in_specs=[qblk, kblk, kblk, pl.BlockSpec((1, g * lq, 1), lambda bi, hi: (hi, 0, 0))],
        out_specs=qblk,
        compiler_params=_params("parallel", "parallel"),
        name="context_attention",
    )(q, k, v, sink_col)


def _flash_kernel(q_ref, k_ref, v_ref, o_ref, m_scr, l_scr, acc_scr, sa_scr, sb_scr, *, tk):
    dh = HEAD_DIM
    g = q_ref.shape[0] // dh
    n = k_ref.shape[2] // tk
    m_scr[...] = jnp.full_like(m_scr, NEG_INF)
    l_scr[...] = jnp.zeros_like(l_scr)
    acc_scr[...] = jnp.zeros_like(acc_scr)

    def scores(j, dst):
        kk = k_ref[0, 0, pl.ds(pl.multiple_of(j * tk, tk), tk), :]
        for h in range(g):
            dst[h] = jnp.dot(kk, q_ref[h * dh:(h + 1) * dh, :], preferred_element_type=F32)

    def update(j, src):
        vv = v_ref[0, 0, :, pl.ds(pl.multiple_of(j * tk, tk), tk)]
        ss = [src[h] for h in range(g)]
        m_olds = [m_scr[h] for h in range(g)]
        m_news = [jnp.maximum(m_olds[h], jnp.max(ss[h], axis=0, keepdims=True)) for h in range(g)]
        ps = [jnp.exp2(ss[h] - m_news[h]) for h in range(g)]
        pvs = [jnp.dot(vv, ps[h].astype(BF16), preferred_element_type=F32) for h in range(g)]
        for h in range(g):
            alpha = jnp.exp2(m_olds[h] - m_news[h])
            l_scr[h] = alpha * l_scr[h] + jnp.sum(ps[h], axis=0, keepdims=True)
            acc_scr[h] = alpha * acc_scr[h] + pvs[h]
            m_scr[h] = m_news[h]

    scores(0, sa_scr)

    def body(i, carry):
        scores(2 * i + 1, sb_scr)
        update(2 * i, sa_scr)
        scores(2 * i + 2, sa_scr)
        update(2 * i + 1, sb_scr)
        return carry

    lax.fori_loop(0, (n - 1) // 2, body, 0)
    if n % 2 == 1:
        update(n - 1, sa_scr)
    else:
        scores(n - 1, sb_scr)
        update(n - 2, sa_scr)
        update(n - 1, sb_scr)
    out = (acc_scr[...] / l_scr[...]).reshape(g * dh, -1)
    o_ref[...] = out.T.astype(o_ref.dtype)


def _flash(qt, k, vt, b, s):
    w = qt.shape[0]
    hkv, t, dh = k.shape[1], k.shape[2], k.shape[3]
    g = w // dh // hkv
    tq = _tile(s, (256, 128))
    tk = _tile(t, (256, 128))
    nq = s // tq
    return pl.pallas_call(
        functools.partial(_flash_kernel, tk=tk),
        out_shape=jax.ShapeDtypeStruct((b * s, w), BF16),
        grid=(b, hkv, nq),
        in_specs=[pl.BlockSpec((g * dh, tq), lambda bi, hi, i: (hi, bi * nq + i)),
                  pl.BlockSpec((1, 1, t, dh), lambda bi, hi, i: (bi, hi, 0, 0)),
                  pl.BlockSpec((1, 1, dh, t), lambda bi, hi, i: (bi, hi, 0, 0))],
        out_specs=pl.BlockSpec((tq, g * dh), lambda bi, hi, i: (bi * nq + i, hi)),
        scratch_shapes=[pltpu.VMEM((g, 1, tq), F32), pltpu.VMEM((g, 1, tq), F32), pltpu.VMEM((g, dh, tq), F32),
                        pltpu.VMEM((g, tk, tq), F32), pltpu.VMEM((g, tk, tq), F32)],
        compiler_params=_params("parallel", "parallel", "arbitrary"),
        name="dense_attention",
    )(qt, k, vt)


def _bmm_kernel(a_ref, b_ref, o_ref):
    o_ref[0] = jnp.dot(a_ref[0], b_ref[0], preferred_element_type=F32).astype(o_ref.dtype)


def _group_matmul(a, bmat, name):
    g, m, k = a.shape
    n = bmat.shape[2]
    tm = _tile(m, (1024, 512, 256, 128, 64, 32, 16, 8))
    return pl.pallas_call(
        _bmm_kernel,
        out_shape=jax.ShapeDtypeStruct((g, m, n), F32),
        grid=(g, m // tm),
        in_specs=[pl.BlockSpec((1, tm, k), lambda gi, i: (gi, i, 0)),
                  pl.BlockSpec((1, k, n), lambda gi, i: (gi, 0, 0))],
        out_specs=pl.BlockSpec((1, tm, n), lambda gi, i: (gi, i, 0)),
        compiler_params=_params("parallel", "parallel"),
        name=name,
    )(a, bmat)


def _s5_scan_kernel(sre_ref, sim_ref, are_ref, aim_ref, zre_ref, zim_ref):
    nd, nsteps = sre_ref.shape[0], sre_ref.shape[1]
    for d in range(nd):
        a_re, a_im = are_ref[d], aim_ref[d]

        def body(i, carry, d=d, a_re=a_re, a_im=a_im):
            z_re, z_im = carry
            zre_ref[d, i] = z_re
            zim_ref[d, i] = z_im
            return (a_re * z_re - a_im * z_im + sre_ref[d, i], a_re * z_im + a_im * z_re + sim_ref[d, i])

        zero = jnp.zeros(sre_ref.shape[2:], F32)
        lax.fori_loop(0, nsteps, body, (zero, zero))


def _s5_scan(s_re, s_im, a_re, a_im):
    full = lambda arr: pl.BlockSpec(arr.shape, lambda i: (0,) * arr.ndim)
    return pl.pallas_call(
        _s5_scan_kernel,
        out_shape=[jax.ShapeDtypeStruct(s_re.shape, F32)] * 2,
        grid=(1,),
        in_specs=[full(s_re), full(s_im), full(a_re), full(a_im)],
        out_specs=[full(s_re), full(s_im)],
        compiler_params=_params("arbitrary"),
        name="s5_chunk_scan",
    )(s_re, s_im, a_re, a_im)


def _s5_glu_kernel(y_ref, u_ref, d_ref, w_ref, b_ref, o_ref):
    y = y_ref[...] + d_ref[...] * u_ref[...]
    y = jax.nn.gelu(y)
    gate = jnp.dot(y.astype(BF16), w_ref[...], preferred_element_type=F32) + b_ref[...]
    o_ref[...] = (y * jax.nn.sigmoid(gate)).astype(o_ref.dtype)


def _s5_glu(y, u, d_skip, w_glu, b_glu, tm):
    r, c = y.shape
    blk = pl.BlockSpec((tm, c), lambda i: (i, 0))
    vec = pl.BlockSpec((1, c), lambda i: (0, 0))
    return pl.pallas_call(
        _s5_glu_kernel,
        out_shape=jax.ShapeDtypeStruct((r, c), BF16),
        grid=(r // tm,),
        in_specs=[blk, blk, vec, pl.BlockSpec((c, c), lambda i: (0, 0)), vec],
        out_specs=blk,
        compiler_params=_params("parallel"),
        name="s5_readout_glu",
    )(y, u, d_skip.reshape(1, c), w_glu.astype(BF16), b_glu.reshape(1, c))


def _s5_tables(lam_re, lam_im, log_dt, b_re, b_im, c_re, c_im):
    ln, p, gc = S5_CHUNK, S5_STATE, S5_GROUP
    lam = lax.complex(lam_re.astype(F32), lam_im.astype(F32))
    dt = jnp.exp(log_dt.astype(F32))[..., None]
    a_bar = jnp.exp(lam * dt)
    b_scale = (a_bar - 1.0) / lam
    b_mat = lax.complex(b_re.astype(F32), b_im.astype(F32))
    c_mat = lax.complex(c_re.astype(F32), c_im.astype(F32))
    tau = jnp.arange(ln + 1, dtype=F32)
    apow = jnp.exp((lam * dt)[:, :, None, :] * tau[None, None, :, None])
    drive = b_scale[..., None] * b_mat[None]
    kern = jnp.real(jnp.einsum('gcp,dgtp,dgpe->dgtce', c_mat, apow[:, :, :ln], drive))
    kc = jnp.concatenate([kern[1, :, :0:-1], (kern[0, :, :1] + kern[1, :, :1]), kern[0, :, 1:]], axis=1)
    t_idx = jnp.arange(ln)
    lag = t_idx[None, :] - t_idx[:, None] + ln - 1
    toep = kc[:, lag]
    toep = toep.transpose(0, 1, 4, 2, 3).reshape(S5_GROUPS, ln * gc, ln * gc)
    w_f = apow[0, :, ln - 1::-1][:, :ln, :, None] * drive[0][:, None]
    w_b = apow[1, :, :ln, :, None] * drive[1][:, None]
    def m_in(wc):
        wt = wc.transpose(0, 1, 3, 2).reshape(S5_GROUPS, ln * gc, p)
        return jnp.concatenate([jnp.real(wt), jnp.imag(wt)], axis=-1)
    min_all = jnp.concatenate([m_in(w_f), m_in(w_b)], axis=-1)
    o_f = c_mat[:, None] * apow[0, :, 1:ln + 1][:, :, None, :]
    o_b = c_mat[:, None] * apow[1, :, ln:0:-1][:, :, None, :]
    def m_out(oc):
        ot = oc.transpose(0, 3, 1, 2).reshape(S5_GROUPS, p, ln * gc)
        return jnp.concatenate([jnp.real(ot), -jnp.imag(ot)], axis=1)
    rhs = jnp.concatenate([toep, m_out(o_f), m_out(o_b)], axis=1)
    a_l = apow[:, :, ln].reshape(2, 1, S5_GROUPS * p)
    return min_all.astype(BF16), rhs.astype(BF16), jnp.real(a_l), jnp.imag(a_l)


def _s5_mixer(u_x, u_c, tables, d_skip, w_glu, b_glu):
    min_all, rhs, a_re, a_im = tables
    b, s, _ = u_x.shape
    ln, p, gc, ng = S5_CHUNK, S5_STATE, S5_GROUP, S5_GROUPS

    def chunks(u):
        nc = u.shape[1] // ln
        return u.reshape(b, nc, ln, ng, gc).transpose(3, 0, 1, 2, 4).reshape(ng, b * nc, ln * gc).astype(BF16), nc

    ux, ncx = chunks(u_x)
    uc, ncc = chunks(u_c)
    sx = _group_matmul(ux, min_all, "s5_local_state").reshape(ng, b, ncx, 4, p)
    sc = _group_matmul(uc, min_all, "s5_local_state_ctx").reshape(ng, b, ncc, 4, p)

    def scan_order(part):
        f = jnp.concatenate([sc[:, :, :, part], sx[:, :, :, part]], axis=2)
        r = jnp.concatenate([sc[:, :, ::-1, part + 2], sx[:, :, ::-1, part + 2]], axis=2)
        return jnp.stack([f, r]).transpose(0, 3, 2, 1, 4).reshape(2, ncc + ncx, b, ng * p)

    z_re, z_im = _s5_scan(scan_order(0), scan_order(1), a_re, a_im)

    def latent(z, d):
        zl = z[d, ncc:]
        if d == 1:
            zl = zl[::-1]
        return zl.reshape(ncx, b, ng, p).transpose(2, 1, 0, 3).reshape(ng, b * ncx, p)

    lhs = jnp.concatenate([ux, latent(z_re, 0).astype(BF16), latent(z_im, 0).astype(BF16),
                           latent(z_re, 1).astype(BF16), latent(z_im, 1).astype(BF16)], axis=-1)
    y = _group_matmul(lhs, rhs, "s5_outputs")
    y = y.reshape(ng, b, ncx, ln, gc).transpose(1, 2, 3, 0, 4).reshape(b * s, S5_CHANNELS)
    return _s5_glu(y, u_x.reshape(b * s, S5_CHANNELS), d_skip, w_glu, b_glu, _tile(b * s, (1024, 512, 256)))


def _out_proj_kernel(x_ref, a_ref, b_ref, wa_ref, wb_ref, gt_ref, g_ref, sc_ref, sh_ref, wr_ref, br_ref,
                     xo_ref, xn_ref, lg_ref):
    y = (jnp.dot(a_ref[...], wa_ref[...], preferred_element_type=F32)
         + jnp.dot(b_ref[...], wb_ref[...], preferred_element_type=F32))
    x = x_ref[...] + gt_ref[0] * y
    xo_ref[...] = x
    xn = x * lax.rsqrt(jnp.mean(x * x, axis=-1, keepdims=True) + EPS) * g_ref[...]
    xn = xn * (1.0 + sc_ref[0]) + sh_ref[0]
    xn_ref[...] = xn.astype(BF16)
    lg_ref[...] = jnp.dot(xn, wr_ref[...], precision=HIGHEST, preferred_element_type=F32) + br_ref[...]


def _out_proj(x, row0, a, bmix, wa, wb, gate, g_ffn, scale, shift, mod_map, w_router, b_router, tm):
    r = a.shape[0]
    d = x.shape[1]
    t0 = row0 // tm
    ne = w_router.shape[1]
    row = lambda w: pl.BlockSpec((tm, w), lambda i: (i, 0))
    mod = pl.BlockSpec((1, 1, d), lambda i: (mod_map(i), 0, 0))
    full = lambda arr: pl.BlockSpec(arr.shape, lambda i: (0, 0))
    return pl.pallas_call(
        _out_proj_kernel,
        out_shape=[jax.ShapeDtypeStruct((r, d), F32), jax.ShapeDtypeStruct((r, d), BF16),
                   jax.ShapeDtypeStruct((r, ne), F32)],
        grid=(r // tm,),
        in_specs=[pl.BlockSpec((tm, d), lambda i: (i + t0, 0)), row(a.shape[1]), row(bmix.shape[1]),
                  full(wa), full(wb), mod, pl.BlockSpec((1, d), lambda i: (0, 0)), mod, mod,
                  full(w_router), pl.BlockSpec((1, ne), lambda i: (0, 0))],
        out_specs=[row(d), row(d), row(ne)],
        compiler_params=_params("parallel"),
        name="out_proj",
    )(x, a, bmix, wa, wb, gate, g_ffn.reshape(1, d), scale, shift, w_router, b_router.reshape(1, ne))


def _route_kernel(lg_ref, idx_ref, gate_ref, pos_ref, cnt_ref, carry):
    tm, ne = lg_ref.shape

    @pl.when(pl.program_id(0) == 0)
    def _():
        carry[...] = jnp.zeros_like(carry)

    work = lg_ref[...]
    lane = lax.broadcasted_iota(I32, (tm, ne), 1).astype(F32)
    out_lane = lax.broadcasted_iota(I32, (tm, LANES), 1)
    vals, hots, idx_out = [], [], jnp.zeros((tm, LANES), I32)
    for kk in range(TOP_K):
        mx = jnp.max(work, axis=1, keepdims=True)
        idx = jnp.min(jnp.where(work == mx, lane, float(ne)), axis=1, keepdims=True)
        hot = lane == idx
        work = jnp.where(hot, -jnp.inf, work)
        vals.append(mx)
        hots.append(hot)
        idx_out = jnp.where(out_lane == kk, idx.astype(I32), idx_out)
    exps = [jnp.exp(vv - vals[0]) for vv in vals]
    tot = exps[0] + exps[1] + exps[2] + exps[3]
    multi = (hots[0] | hots[1] | hots[2] | hots[3]).astype(BF16)
    r_i = lax.broadcasted_iota(I32, (tm, tm), 0)
    c_i = lax.broadcasted_iota(I32, (tm, tm), 1)
    before = (r_i > c_i).astype(BF16)
    prefix = jnp.dot(before, multi, preferred_element_type=F32) + carry[...]
    gate_out = jnp.zeros((tm, LANES), F32)
    pos_out = jnp.zeros((tm, LANES), I32)
    for kk in range(TOP_K):
        gate_out = jnp.where(out_lane == kk, exps[kk] / tot, gate_out)
        pos = jnp.sum(jnp.where(hots[kk], prefix, 0.0), axis=1, keepdims=True).astype(I32)
        pos_out = jnp.where(out_lane == kk, pos, pos_out)
    idx_ref[...] = idx_out
    gate_ref[...] = gate_out
    pos_ref[...] = pos_out
    carry[...] = carry[...] + jnp.sum(multi.astype(F32), axis=0, keepdims=True)
    cnt_ref[...] = carry[...]


def _route(logits, tm):
    n, ne = logits.shape
    wide = pl.BlockSpec((tm, LANES), lambda i: (i, 0))
    return pl.pallas_call(
        _route_kernel,
        out_shape=[jax.ShapeDtypeStruct((n, LANES), I32), jax.ShapeDtypeStruct((n, LANES), F32),
                   jax.ShapeDtypeStruct((n, LANES), I32), jax.ShapeDtypeStruct((1, ne), F32)],
        grid=(n // tm,),
        in_specs=[pl.BlockSpec((tm, ne), lambda i: (i, 0))],
        out_specs=[wide, wide, wide, pl.BlockSpec((1, ne), lambda i: (0, 0))],
        scratch_shapes=[pltpu.VMEM((1, ne), F32)],
        compiler_params=_params("arbitrary"),
        name="moe_route",
    )(logits)


def _gate_up_prep_kernel(w_ref, p_ref, o_ref):
    for t in range(w_ref.shape[2] // p_ref.shape[0]):
        sl = slice(t * p_ref.shape[0], (t + 1) * p_ref.shape[0])
        o_ref[0, :, sl] = jnp.dot(w_ref[0, :, sl].astype(BF16), p_ref[...], preferred_element_type=F32).astype(BF16)


def _gate_up_prep(w_gate_up):
    ne, d, ff2 = w_gate_up.shape
    tile = 2 * LANES
    j = np.arange(tile)
    src = np.where(j < LANES, 2 * j, 2 * (j - LANES) + 1)
    perm = jnp.asarray(np.arange(tile)[:, None] == src[None, :], BF16)
    tr = _tile(d, (512, 256, 128))
    return pl.pallas_call(
        _gate_up_prep_kernel,
        out_shape=jax.ShapeDtypeStruct((ne, d, ff2), BF16),
        grid=(ne, d // tr),
        in_specs=[pl.BlockSpec((1, tr, ff2), lambda e, i: (e, i, 0)),
                  pl.BlockSpec((tile, tile), lambda e, i: (0, 0))],
        out_specs=pl.BlockSpec((1, tr, ff2), lambda e, i: (e, i, 0)),
        compiler_params=_params("parallel", "parallel"),
        name="moe_gate_up_prep",
    )(w_gate_up, perm)


def _expert_kernel(be_ref, nb_ref, x_ref, wgu_ref, bgu_ref, wd_ref, bd_ref, o_ref):
    i = pl.program_id(0)

    @pl.when(i < nb_ref[0])
    def _():
        h = jnp.dot(x_ref[...], wgu_ref[0], preferred_element_type=F32) + bgu_ref[0]
        acts = []
        for t in range(h.shape[1] // (2 * LANES)):
            glu = jnp.minimum(h[:, 2 * LANES * t:2 * LANES * t + LANES], SWIGLU_LIMIT)
            lin = jnp.clip(h[:, 2 * LANES * t + LANES:2 * LANES * (t + 1)], -SWIGLU_LIMIT, SWIGLU_LIMIT)
            acts.append((glu * jax.nn.sigmoid(SWIGLU_ALPHA * glu) * (lin + 1.0)).astype(BF16))
        act = jnp.concatenate(acts, axis=1)
        o_ref[...] = (jnp.dot(act, wd_ref[0], preferred_element_type=F32) + bd_ref[0]).astype(o_ref.dtype)

    @pl.when(i >= nb_ref[0])
    def _():
        o_ref[...] = jnp.zeros_like(o_ref)


def _experts(xs, blk_e, n_used, wgu, bgu, wd, bd):
    n, d = xs.shape
    ff2 = wgu.shape[2]
    nblk = n // MOE_ROWS
    wspec = lambda shape: pl.BlockSpec((1,) + shape, lambda i, be, nb: (be[i], 0, 0))
    return pl.pallas_call(
        _expert_kernel,
        out_shape=jax.ShapeDtypeStruct((n, d), BF16),
        grid_spec=pltpu.PrefetchScalarGridSpec(
            num_scalar_prefetch=2,
            grid=(nblk,),
            in_specs=[pl.BlockSpec((MOE_ROWS, d), lambda i, be, nb: (i, 0)),
                      wspec((d, ff2)), wspec((1, ff2)), wspec((ff2 // 2, d)), wspec((1, d))],
            out_specs=pl.BlockSpec((MOE_ROWS, d), lambda i, be, nb: (i, 0))),
        compiler_params=_params("arbitrary"),
        name="moe_experts",
    )(blk_e, n_used, xs, wgu, bgu, wd, bd)


def _combine_kernel(x_ref, y_ref, gate_ref, gt_ref, g_ref, o_ref, *, final_norm):
    gates = gate_ref[...]
    y = gates[:, 0:1] * y_ref[0].astype(F32)
    for kk in range(1, TOP_K):
        y = y + gates[:, kk:kk + 1] * y_ref[kk].astype(F32)
    x = x_ref[...] + gt_ref[0] * y
    if final_norm:
        x = x * lax.rsqrt(jnp.mean(x * x, axis=-1, keepdims=True) + EPS) * g_ref[...]
    o_ref[...] = x


def _combine(x, y4, gates, gate_mod, mod_map, g_final, final_norm, tm):
    r, d = x.shape
    return pl.pallas_call(
        functools.partial(_combine_kernel, final_norm=final_norm),
        out_shape=jax.ShapeDtypeStruct((r, d), F32),
        grid=(r // tm,),
        in_specs=[pl.BlockSpec((tm, d), lambda i: (i, 0)),
                  pl.BlockSpec((TOP_K, tm, d), lambda i: (0, i, 0)),
                  pl.BlockSpec((tm, LANES), lambda i: (i, 0)),
                  pl.BlockSpec((1, 1, d), lambda i: (mod_map(i), 0, 0)),
                  pl.BlockSpec((1, d), lambda i: (0, 0))],
        out_specs=pl.BlockSpec((tm, d), lambda i: (i, 0)),
        compiler_params=_params("parallel"),
        name="moe_combine",
    )(x, y4, gates, gate_mod, g_final.reshape(1, d))


def _moe(x, xn, logits, gate_mod, mod_map, weights, g_final, final_norm, tm):
    wgu, bgu, wd, bd = weights
    n, d = xn.shape
    idx_w, gates_w, pos_w, counts = _route(logits, tm)
    idx, pos = idx_w[:, :TOP_K], pos_w[:, :TOP_K]
    counts = counts[0].astype(I32)
    padded = (counts + MOE_ROWS - 1) // MOE_ROWS * MOE_ROWS
    pad_end = jnp.cumsum(padded)
    pad_start = pad_end - padded
    dest = pad_start[idx] + pos
    nblk = n * TOP_K // MOE_ROWS + N_EXPERTS
    slots = nblk * MOE_ROWS
    src = jnp.zeros((slots,), I32).at[dest.reshape(-1)].set(
        jnp.arange(n * TOP_K, dtype=I32) // TOP_K, unique_indices=True, mode="promise_in_bounds")
    blk_row0 = jnp.arange(nblk, dtype=I32) * MOE_ROWS
    blk_e = jnp.minimum(jnp.sum((pad_end[None, :] <= blk_row0[:, None]).astype(I32), axis=1), N_EXPERTS - 1)
    n_used = (pad_end[-1:] // MOE_ROWS).astype(I32)
    xs = xn.at[src].get(mode="promise_in_bounds")
    ys = _experts(xs, blk_e, n_used, wgu, bgu, wd, bd)
    y4 = ys.at[dest.T.reshape(-1)].get(mode="promise_in_bounds").reshape(TOP_K, n, d)
    return _combine(x, y4, gates_w, gate_mod, mod_map, g_final, final_norm, tm)


def _moe_weights(w_gate_up, b_gate_up, w_down, b_down):
    ne, d, ff2 = w_gate_up.shape
    b_tiled = b_gate_up.reshape(ne, ff2 // (2 * LANES), LANES, 2).swapaxes(-1, -2).reshape(ne, 1, ff2)
    return _gate_up_prep(w_gate_up), b_tiled, w_down.astype(BF16), b_down[:, None, :]


def _rope_tables(n_tokens):
    rows = n_tokens // GRID_W
    row = jnp.repeat(jnp.arange(rows, dtype=I32), GRID_W).astype(F32)
    col = jnp.tile(jnp.arange(GRID_W, dtype=I32), rows).astype(F32)
    inv = ROPE_BASE ** (-jnp.arange(0, ROPE_AXIS_DIM, 2, dtype=F32) / ROPE_AXIS_DIM)
    ang_r, ang_c = row[:, None] * inv, col[:, None] * inv
    cos = jnp.concatenate([jnp.cos(ang_r)] * 2 + [jnp.cos(ang_c)] * 2, axis=-1)
    sin = jnp.concatenate([jnp.sin(ang_r)] * 2 + [jnp.sin(ang_c)] * 2, axis=-1)
    return cos, sin


def _rot_perm():
    q = ROPE_AXIS_DIM // 2
    d = np.arange(HEAD_DIM)
    first = (d % ROPE_AXIS_DIM) < q
    perm = np.where(first, d + q, d - q)
    sign = np.where(first, -1.0, 1.0).astype(np.float32)
    return perm, sign


def _rot_cols(w, n_heads):
    perm, sign = _rot_perm()
    k = w.shape[0]
    wh = w.reshape(k, n_heads, HEAD_DIM)
    return (wh[:, :, perm] * sign).reshape(k, n_heads * HEAD_DIM)


def _to_heads(t, b, n_heads):
    return t.reshape(b, -1, n_heads, HEAD_DIM).transpose(0, 2, 1, 3)


def _from_heads(t):
    b, h, tt, dh = t.shape
    return t.transpose(0, 2, 1, 3).reshape(b * tt, h * dh)


def _layer_even(xa, mod, b, s, lc, g_mix, g_ffn, w_in, w_out, gate_bias, mlstm_norm, sink, router, moe_w):
    d = xa.shape[1]
    nctx = b * lc
    tm = _tile(math.gcd(nctx, s), (512, 256, 128))
    mod_map = lambda i: jnp.where(i * tm < nctx, b, (i * tm - nctx) // s)
    sh1, sc1, gt1, sh2, sc2, gt2 = [mod[:, j][:, None, :] for j in range(6)]

    hm, hs, hk = MLSTM_HEADS * HEAD_DIM, SWA_HEADS * HEAD_DIM, SWA_KV_HEADS * HEAD_DIM
    o0 = np.cumsum([0, hm, hm, hm, hm, 4 * MLSTM_HEADS, hs, hk, hk])
    seg = lambda j: w_in[:, o0[j]:o0[j + 1]]
    w_cat = jnp.concatenate([seg(0), seg(1) * ATTN_SCALE, seg(2), seg(3),
                             seg(6), _rot_cols(seg(6), SWA_KV_HEADS), seg(7), _pad_cols(seg(4))], axis=1).astype(BF16)
    w_q = jnp.concatenate([seg(5), _rot_cols(seg(5), SWA_HEADS)], axis=1).T.astype(BF16)
    widths = [hm, hm, hm, hm, hk, hk, hk, LANES]
    dts = [BF16] * 7 + [F32]
    qa, ka, va, oa, kb, kb_r, vb, gts, qt, qt_r = _in_proj(xa, g_mix, sc1, sh1, mod_map, w_cat, widths, dts, tm,
                                                           wt=w_q, t_widths=[hs, hs], t_dtypes=[BF16, BF16])

    hf, hb = _mlstm(qa, ka, va, gts, _pad_cols(gate_bias.astype(F32).reshape(1, -1)), b, lc, s)
    mix_a = _mlstm_out(hf, hb, oa, mlstm_norm.reshape(-1), tm)

    cos, sin = _rope_tables(s)
    ones = jnp.ones((HEAD_DIM,), F32)
    g = SWA_HEADS // SWA_KV_HEADS
    heads = lambda t, n: _to_heads(t, b, n)
    log2e = math.log2(math.e)
    q_x = _q_prep_t(qt, qt_r, nctx, b * s, cos.T, sin.T, ones, ones, ATTN_SCALE * log2e, norm=False)
    k_x = _qk_prep(heads(kb[nctx:], SWA_KV_HEADS), heads(kb_r[nctx:], SWA_KV_HEADS), cos, sin, ones, ones,
                   norm=False, rope=True, scale=1.0)
    k_c, v_c, v_x = heads(kb[:nctx], SWA_KV_HEADS), heads(vb[:nctx], SWA_KV_HEADS), heads(vb[nctx:], SWA_KV_HEADS)
    q_c = heads(qt[:, :nctx].T, SWA_HEADS)
    sink_h = sink.astype(F32).reshape(SWA_KV_HEADS, g, 1)
    att_x = _swa(q_x, k_x, v_x.swapaxes(-1, -2), k_c, v_c.swapaxes(-1, -2),
                 jnp.broadcast_to(sink_h * log2e, (SWA_KV_HEADS, g, 2 * SWA_BLOCK)), b, s)
    att_c = _ctx_attn(q_c, k_c, v_c, jnp.broadcast_to(sink_h[..., None], (SWA_KV_HEADS, g, lc, 1))
                      .reshape(SWA_KV_HEADS, g * lc, 1), True)
    mix_b = jnp.concatenate([_from_heads(att_c), att_x], axis=0)

    w_router, b_router = router
    xa, xn2, logits = _out_proj(xa, 0, mix_a, mix_b, w_out[:hm].astype(BF16), w_out[hm:].astype(BF16), gt1, g_ffn,
                                sc2, sh2, mod_map, w_router.astype(F32), b_router.astype(F32), tm)
    return _moe(xa, xn2, logits, gt2, mod_map, moe_w, g_ffn, False, tm)


def _layer_odd_last(xa, mod, b, s, lc, g_mix, g_ffn, w_in, w_out, s5_params, d_skip, w_glu, b_glu,
                    q_norm, k_norm, router, moe_w, g_final):
    nctx = b * lc
    tm = _tile(math.gcd(nctx, s), (512, 256, 128))
    mod_map = lambda i: jnp.where(i * tm < nctx, b, (i * tm - nctx) // s)
    lat_map = lambda i: i * tm // s
    sh1, sc1, gt1, sh2, sc2, gt2 = [mod[:, j][:, None, :] for j in range(6)]

    hq, hk = ATT_HEADS * HEAD_DIM, ATT_KV_HEADS * HEAD_DIM
    o1 = np.cumsum([0, S5_CHANNELS, hq, hk, hk])
    seg = lambda j: w_in[:, o1[j]:o1[j + 1]]
    kpad = lambda w: _pad_cols(w, 2 * LANES)
    w_cat = jnp.concatenate([seg(0), kpad(seg(2)), kpad(_rot_cols(seg(2), ATT_KV_HEADS)), kpad(seg(3))],
                            axis=1).astype(BF16)
    w_q = jnp.concatenate([seg(1), _rot_cols(seg(1), ATT_HEADS)], axis=1).T.astype(BF16)
    widths = [S5_CHANNELS, 2 * LANES, 2 * LANES, 2 * LANES]
    dts = [F32, BF16, BF16, BF16]
    u, k, k_r, v, qt, qt_r = _in_proj(xa, g_mix, sc1, sh1, mod_map, w_cat, widths, dts, tm,
                                      wt=w_q, t_widths=[hq, hq], t_dtypes=[BF16, BF16])

    mix_a = _s5_mixer(u[nctx:].reshape(b, s, S5_CHANNELS), u[:nctx].reshape(b, lc, S5_CHANNELS),
                      _s5_tables(*s5_params), d_skip, w_glu, b_glu)

    cos, sin = _rope_tables(s)
    perm, _ = _rot_perm()
    qn, kn = q_norm.astype(F32), k_norm.astype(F32)
    heads = lambda t, n: _to_heads(t[:, :n * HEAD_DIM], b, n)
    q_x = _q_prep_t(qt, qt_r, nctx, b * s, cos.T, sin.T, qn, qn[perm], ATTN_SCALE * math.log2(math.e))
    k_x = _qk_prep(heads(k[nctx:], ATT_KV_HEADS), heads(k_r[nctx:], ATT_KV_HEADS), cos, sin, kn, kn[perm],
                   norm=True, rope=True, scale=1.0)
    kc_raw = heads(k[:nctx], ATT_KV_HEADS)
    k_c = _qk_prep(kc_raw, kc_raw, cos[:lc], sin[:lc], kn, kn, norm=True, rope=False, scale=1.0)
    k_all = jnp.concatenate([k_c, k_x], axis=2)
    v_all = jnp.concatenate([heads(v[:nctx], ATT_KV_HEADS), heads(v[nctx:], ATT_KV_HEADS)], axis=2)
    mix_b = _flash(q_x, k_all, v_all.swapaxes(-1, -2), b, s)

    w_router, b_router = router
    hs = S5_CHANNELS
    x, xn2, logits = _out_proj(xa, nctx, mix_a, mix_b, w_out[:hs].astype(BF16), w_out[hs:].astype(BF16), gt1, g_ffn,
                               sc2, sh2, lat_map, w_router.astype(F32), b_router.astype(F32), tm)
    return _moe(x, xn2, logits, gt2, lat_map, moe_w, g_final, True, tm)


def kernel(x, c, ctx, c_ctx, l0_w_mod, l0_b_mod, l0_g_mix, l0_g_ffn, l0_w_in, l0_w_out, l0_gate_bias, l0_mlstm_norm, l0_sink, l0_w_router, l0_b_router, l0_w_gate_up, l0_b_gate_up, l0_w_down, l0_b_down, l1_w_mod, l1_b_mod, l1_g_mix, l1_g_ffn, l1_w_in, l1_w_out, l1_lam_re, l1_lam_im, l1_log_dt, l1_b_re, l1_b_im, l1_c_re, l1_c_im, l1_d_skip, l1_w_glu, l1_b_glu, l1_q_norm, l1_k_norm, l1_w_router, l1_b_router, l1_w_gate_up, l1_b_gate_up, l1_w_down, l1_b_down, g_final):
    b, s, d = x.shape
    lc = ctx.shape[1]
    cond = jnp.concatenate([c, c_ctx[None, :]], axis=0)
    cond = jnp.pad(cond, ((0, (-(b + 1)) % 8), (0, 0)))
    mod0 = _silu_linear(cond, l0_w_mod, l0_b_mod)[:b + 1].reshape(b + 1, 6, d)
    mod1 = _silu_linear(cond, l1_w_mod, l1_b_mod)[:b + 1].reshape(b + 1, 6, d)

    xa = jnp.concatenate([ctx.reshape(b * lc, d), x.reshape(b * s, d)], axis=0)
    xa = _layer_even(xa, mod0, b, s, lc, l0_g_mix, l0_g_ffn, l0_w_in, l0_w_out, l0_gate_bias, l0_mlstm_norm,
                     l0_sink, (l0_w_router, l0_b_router),
                     _moe_weights(l0_w_gate_up, l0_b_gate_up, l0_w_down, l0_b_down))
    out = _layer_odd_last(xa, mod1, b, s, lc, l1_g_mix, l1_g_ffn, l1_w_in, l1_w_out,
                          (l1_lam_re, l1_lam_im, l1_log_dt, l1_b_re, l1_b_im, l1_c_re, l1_c_im),
                          l1_d_skip, l1_w_glu, l1_b_glu, l1_q_norm, l1_k_norm, (l1_w_router, l1_b_router),
                          _moe_weights(l1_w_gate_up, l1_b_gate_up, l1_w_down, l1_b_down), g_final)
    return out.reshape(b, s, d)
```

```python
import functools
import math

import jax
import jax.numpy as jnp
import numpy as np
from jax import lax
from jax.experimental import pallas as pl
from jax.experimental.pallas import tpu as pltpu

F32 = jnp.float32
BF16 = jnp.bfloat16
I32 = jnp.int32

GRID_W = 64
HEAD_DIM = 64
ATTN_SCALE = HEAD_DIM ** -0.5
ROPE_AXIS_DIM = HEAD_DIM // 2
ROPE_BASE = 10000.0
EPS = 1e-6
NEG_INF = -1e30

MLSTM_HEADS = 8
MLSTM_CHUNKS = (256, 128, 64)
SWA_HEADS = 8
SWA_KV_HEADS = 2
SWA_WINDOW = 128
SWA_BLOCK = 128
S5_CHANNELS = 256
S5_GROUP = 16
S5_GROUPS = S5_CHANNELS // S5_GROUP
S5_STATE = 64
S5_CHUNK = 64
ATT_HEADS = 12
ATT_KV_HEADS = 3
N_EXPERTS = 32
TOP_K = 4
SWIGLU_LIMIT = 7.0
SWIGLU_ALPHA = 1.702

LANES = 128
VMEM_LIMIT = 56 * 1024 * 1024
MOE_ROWS = 512
HIGHEST = lax.Precision.HIGHEST


def _params(*sem):
    return pltpu.CompilerParams(dimension_semantics=sem, vmem_limit_bytes=VMEM_LIMIT)


def _tile(n, prefs):
    for t in prefs:
        if n % t == 0:
            return t
    return n


def _pad_cols(w, mult=LANES):
    pad = (-w.shape[-1]) % mult
    if pad:
        w = jnp.pad(w, [(0, 0)] * (w.ndim - 1) + [(0, pad)])
    return w


def _linear_kernel(x_ref, w_ref, b_ref, o_ref):
    x = x_ref[...]
    x = x * jax.nn.sigmoid(x)
    o_ref[...] = jnp.dot(x, w_ref[...], precision=HIGHEST, preferred_element_type=F32) + b_ref[...]


def _silu_linear(x, w, b):
    m, k = x.shape
    n = w.shape[1]
    tn = _tile(n, (1024, 512, 256, 128))
    return pl.pallas_call(
        _linear_kernel,
        out_shape=jax.ShapeDtypeStruct((m, n), F32),
        grid=(n // tn,),
        in_specs=[pl.BlockSpec((m, k), lambda j: (0, 0)),
                  pl.BlockSpec((k, tn), lambda j: (0, j)),
                  pl.BlockSpec((1, tn), lambda j: (0, j))],
        out_specs=pl.BlockSpec((m, tn), lambda j: (0, j)),
        compiler_params=_params("arbitrary"),
        name="adaln_linear",
    )(x, w, b.reshape(1, n))


def _in_proj_kernel(x_ref, g_ref, sc_ref, sh_ref, w_ref, wt_ref, *out_refs, widths, t_widths):
    x = x_ref[...]
    xn = x * lax.rsqrt(jnp.mean(x * x, axis=-1, keepdims=True) + EPS) * g_ref[...]
    xb = (xn * (1.0 + sc_ref[0]) + sh_ref[0]).astype(BF16)
    off = 0
    for o_ref, w in zip(out_refs, widths):
        o_ref[...] = jnp.dot(xb, w_ref[:, off:off + w], preferred_element_type=F32).astype(o_ref.dtype)
        off += w
    off = 0
    for o_ref, w in zip(out_refs[len(widths):], t_widths):
        o_ref[...] = lax.dot_general(wt_ref[off:off + w, :], xb, (((1,), (1,)), ((), ())),
                                     preferred_element_type=F32).astype(o_ref.dtype)
        off += w


def _in_proj(x, g, scale, shift, mod_map, w, widths, dtypes, tm, wt=None, t_widths=(), t_dtypes=()):
    r, d = x.shape
    if wt is None:
        wt = jnp.zeros((8, d), BF16)
    return pl.pallas_call(
        functools.partial(_in_proj_kernel, widths=tuple(widths), t_widths=tuple(t_widths)),
        out_shape=([jax.ShapeDtypeStruct((r, wd), dt) for wd, dt in zip(widths, dtypes)]
                   + [jax.ShapeDtypeStruct((wd, r), dt) for wd, dt in zip(t_widths, t_dtypes)]),
        grid=(r // tm,),
        in_specs=[pl.BlockSpec((tm, d), lambda i: (i, 0)),
                  pl.BlockSpec((1, d), lambda i: (0, 0)),
                  pl.BlockSpec((1, 1, d), lambda i: (mod_map(i), 0, 0)),
                  pl.BlockSpec((1, 1, d), lambda i: (mod_map(i), 0, 0)),
                  pl.BlockSpec(w.shape, lambda i: (0, 0)),
                  pl.BlockSpec(wt.shape, lambda i: (0, 0))],
        out_specs=([pl.BlockSpec((tm, wd), lambda i: (i, 0)) for wd in widths]
                   + [pl.BlockSpec((wd, tm), lambda i: (0, i)) for wd in t_widths]),
        compiler_params=_params("parallel"),
        name="in_proj",
    )(x, g.reshape(1, d), scale, shift, w, wt)


def _qk_prep_kernel(x_ref, xr_ref, cos_ref, sin_ref, g_ref, gr_ref, o_ref, *, norm, rope, scale):
    x = x_ref[0].astype(F32)
    if norm:
        s = lax.rsqrt(jnp.mean(x * x, axis=-1, keepdims=True) + EPS)
        x = x * s * g_ref[...]
    if rope:
        xr = xr_ref[0].astype(F32)
        if norm:
            xr = xr * s * gr_ref[...]
        x = x * cos_ref[...] + xr * sin_ref[...]
    o_ref[0] = (x * scale).astype(o_ref.dtype)


def _qk_prep(x, x_rot, cos, sin, gain, gain_rot, *, norm, rope, scale):
    b, h, t, dh = x.shape
    tt = _tile(t, (512, 256, 128))
    blk = pl.BlockSpec((1, h, tt, dh), lambda i, j: (i, 0, j, 0))
    tab = pl.BlockSpec((tt, dh), lambda i, j: (j, 0))
    vec = pl.BlockSpec((1, dh), lambda i, j: (0, 0))
    return pl.pallas_call(
        functools.partial(_qk_prep_kernel, norm=norm, rope=rope, scale=scale),
        out_shape=jax.ShapeDtypeStruct(x.shape, BF16),
        grid=(b, t // tt),
        in_specs=[blk, blk, tab, tab, vec, vec],
        out_specs=blk,
        compiler_params=_params("parallel", "parallel"),
        name="qk_prep",
    )(x, x_rot, cos, sin, gain.reshape(1, dh), gain_rot.reshape(1, dh))


def _q_prep_t_kernel(x_ref, xr_ref, cos_ref, sin_ref, g_ref, gr_ref, o_ref, *, norm, scale):
    w, tt = x_ref.shape
    nh = w // HEAD_DIM
    x = x_ref[...].astype(F32).reshape(nh, HEAD_DIM, tt)
    xr = xr_ref[...].astype(F32).reshape(nh, HEAD_DIM, tt)
    if norm:
        s = lax.rsqrt(jnp.mean(x * x, axis=1, keepdims=True) + EPS)
        x, xr = x * s * g_ref[...], xr * s * gr_ref[...]
    y = x * cos_ref[...] + xr * sin_ref[...]
    o_ref[...] = (y * scale).reshape(w, tt).astype(o_ref.dtype)


def _q_prep_t(xt, xt_rot, col0, n_cols, cos_t, sin_t, gain, gain_rot, scale, norm=True):
    w = xt.shape[0]
    s = cos_t.shape[1]
    tt = _tile(math.gcd(col0, s), (512, 256, 128))
    blk = pl.BlockSpec((w, tt), lambda j: (0, j + col0 // tt))
    tab = pl.BlockSpec((HEAD_DIM, tt), lambda j: (0, j % (s // tt)))
    vec = pl.BlockSpec((HEAD_DIM, 1), lambda j: (0, 0))
    return pl.pallas_call(
        functools.partial(_q_prep_t_kernel, norm=norm, scale=scale),
        out_shape=jax.ShapeDtypeStruct((w, n_cols), BF16),
        grid=(n_cols // tt,),
        in_specs=[blk, blk, tab, tab, vec, vec],
        out_specs=pl.BlockSpec((w, tt), lambda j: (0, j)),
        compiler_params=_params("parallel"),
        name="q_prep_t",
    )(xt, xt_rot, cos_t, sin_t, gain.reshape(HEAD_DIM, 1), gain_rot.reshape(HEAD_DIM, 1))


def _log_sigmoid(x):
    return jnp.minimum(x, 0.0) - jnp.log(1.0 + jnp.exp(-jnp.abs(x)))


def _mlstm_kernel(qf, kf, vf, gf, qb, kb, vb, gb, bias_ref, hf_ref, hb_ref, c_scr, n_scr, m_scr):
    ch, nh, dh = qf.shape[0], MLSTM_HEADS, HEAD_DIM

    @pl.when(pl.program_id(1) == 0)
    def _():
        c_scr[...] = jnp.zeros_like(c_scr)
        n_scr[...] = jnp.zeros_like(n_scr)
        m_scr[...] = jnp.full_like(m_scr, NEG_INF)

    row = lax.broadcasted_iota(I32, (ch, ch), 0)
    col = lax.broadcasted_iota(I32, (ch, ch), 1)
    lane_lo = lax.broadcasted_iota(I32, (ch, 2 * dh), 1) < dh
    row_lo = lax.broadcasted_iota(I32, (2 * dh, 2 * dh), 0) < dh
    col_lo = lax.broadcasted_iota(I32, (2 * dh, 2 * dh), 1) < dh
    vec_lo = lax.broadcasted_iota(I32, (1, 2 * dh), 1) < dh
    pair = lambda a, b: jnp.where(lane_lo, a, b)
    refs = ((qf, kf, vf, gf, hf_ref), (qb, kb, vb, gb, hb_ref))
    jobs = [(d, p) for d in range(2) for p in range(nh // 2)]
    sl = lambda p: slice(2 * dh * p, 2 * dh * (p + 1))

    qk = {}
    for d, p in jobs:
        q2, k2 = refs[d][0][:, sl(p)], refs[d][1][:, sl(p)]
        zero = jnp.zeros_like(q2)
        q_st = jnp.concatenate([jnp.where(lane_lo, q2, zero), jnp.where(lane_lo, zero, q2)], axis=0)
        qk[d, p] = lax.dot_general(q_st, k2, (((1,), (1,)), ((), ())), preferred_element_type=F32)

    st = {}
    for d in range(2):
        seen = (col <= row) if d == 0 else (col >= row)
        tri = seen.astype(F32)
        tri_t = ((row <= col) if d == 0 else (row >= col)).astype(F32)
        last = ch - 1 if d == 0 else 0
        g = refs[d][3][...] + bias_ref[...]
        g_t = g.T
        lo = 2 * nh * d
        li_col, lf_col = g[:, lo:lo + nh], _log_sigmoid(g[:, lo + nh:lo + 2 * nh])
        li_row, lf_row = g_t[lo:lo + nh, :], _log_sigmoid(g_t[lo + nh:lo + 2 * nh, :])
        b_col = jnp.dot(tri, lf_col, precision=HIGHEST, preferred_element_type=F32)
        b_row = jnp.dot(lf_row, tri_t, precision=HIGHEST, preferred_element_type=F32)
        for h in range(nh):
            bc, br = b_col[:, h:h + 1], b_row[h:h + 1, :]
            lir, lic = li_row[h:h + 1, :], li_col[:, h:h + 1]
            b_last = br[:, last:last + 1]
            m_prev = m_scr[d, h // 2][:, (h % 2) * dh:(h % 2) * dh + 1]
            d_log = jnp.where(seen, bc - br + lir, NEG_INF)
            inter_log = bc + m_prev
            m_t = jnp.maximum(inter_log, jnp.max(d_log, axis=1, keepdims=True))
            w_log = b_last - bc + lic
            m_new = jnp.maximum(b_last + m_prev, jnp.max(w_log, axis=0, keepdims=True))
            st[d, h] = dict(dmat=jnp.exp(d_log - m_t), inter=jnp.exp(inter_log - m_t), floor=jnp.exp(-m_t),
                            m_new=m_new, decay=jnp.exp(b_last + m_prev - m_new), wn=jnp.exp(w_log - m_new))

    mm = {}
    for d, p in jobs:
        a, b = st[d, 2 * p], st[d, 2 * p + 1]
        q2, k2, v2 = refs[d][0][:, sl(p)], refs[d][1][:, sl(p)], refs[d][2][:, sl(p)]
        s = qk[d, p] * jnp.concatenate([a["dmat"], b["dmat"]], axis=0)
        kw = k2.astype(F32) * pair(a["wn"], b["wn"])
        mm[d, p] = dict(
            s_sum=jnp.sum(s, axis=1, keepdims=True), kw_sum=jnp.sum(kw, axis=0, keepdims=True),
            sv=jnp.dot(s.astype(BF16), v2, preferred_element_type=F32),
            q_c=jnp.dot(q2, c_scr[d, p].astype(BF16), preferred_element_type=F32),
            kv=lax.dot_general(kw.astype(BF16), v2, (((0,), (0,)), ((), ())), preferred_element_type=F32))

    for d, p in jobs:
        a, b, r = st[d, 2 * p], st[d, 2 * p + 1], mm[d, p]
        q2 = refs[d][0][:, sl(p)]
        n_prev = n_scr[d, p]
        num = pair(a["inter"], b["inter"]) * r["q_c"] + jnp.where(lane_lo, r["sv"][:ch], r["sv"][ch:])
        qn = q2.astype(F32) * n_prev
        qn_a = jnp.sum(jnp.where(lane_lo, qn, 0.0), axis=1, keepdims=True)
        qn_b = jnp.sum(jnp.where(lane_lo, 0.0, qn), axis=1, keepdims=True)
        den_a = jnp.maximum(jnp.abs(a["inter"] * qn_a + r["s_sum"][:ch]), a["floor"])
        den_b = jnp.maximum(jnp.abs(b["inter"] * qn_b + r["s_sum"][ch:]), b["floor"])
        refs[d][4][:, sl(p)] = num / pair(den_a, den_b)
        c_scr[d, p] = (jnp.where(row_lo, a["decay"], b["decay"]) * c_scr[d, p]
                       + jnp.where(row_lo == col_lo, r["kv"], 0.0))
        n_scr[d, p] = jnp.where(vec_lo, a["decay"], b["decay"]) * n_prev + r["kw_sum"]
        m_scr[d, p] = jnp.where(vec_lo, a["m_new"], b["m_new"])


def _mlstm(q, k, v, gates, bias, b, lc, s):
    r, w = q.shape
    ch = _tile(math.gcd(lc, s), MLSTM_CHUNKS)
    ncc, ncx = lc // ch, s // ch
    base = b * ncc

    def fwd(i, c):
        return jnp.where(c < ncc, i * ncc + c, base + i * ncx + (c - ncc)), 0

    def bwd(i, c):
        return jnp.where(c < ncc, i * ncc + (ncc - 1 - c), base + i * ncx + (ncx - 1 - (c - ncc))), 0

    spec = lambda width, m: pl.BlockSpec((ch, width), m)
    npair = MLSTM_HEADS // 2
    return pl.pallas_call(
        _mlstm_kernel,
        out_shape=[jax.ShapeDtypeStruct((r, w), F32)] * 2,
        grid=(b, ncc + ncx),
        in_specs=[spec(w, fwd), spec(w, fwd), spec(w, fwd), spec(LANES, fwd),
                  spec(w, bwd), spec(w, bwd), spec(w, bwd), spec(LANES, bwd),
                  pl.BlockSpec((1, LANES), lambda i, c: (0, 0))],
        out_specs=[spec(w, fwd), spec(w, bwd)],
        scratch_shapes=[pltpu.VMEM((2, npair, 2 * HEAD_DIM, 2 * HEAD_DIM), F32),
                        pltpu.VMEM((2, npair, 1, 2 * HEAD_DIM), F32),
                        pltpu.VMEM((2, npair, 1, 2 * HEAD_DIM), F32)],
        compiler_params=_params("parallel", "arbitrary"),
        name="mlstm_scan",
    )(q, k, v, gates, q, k, v, gates, bias)


def _mlstm_out_kernel(hf_ref, hb_ref, o_ref, nrm_ref, ones_ref, y_ref):
    h = hf_ref[...] + hb_ref[...]
    ms = jnp.dot((h * h).astype(BF16), ones_ref[...], preferred_element_type=F32) * (1.0 / HEAD_DIM)
    hn = h * lax.rsqrt(ms + EPS) * nrm_ref[...]
    y_ref[...] = (jax.nn.sigmoid(o_ref[...].astype(F32)) * hn).astype(y_ref.dtype)


def _head_ones(width):
    idx = np.arange(width) // HEAD_DIM
    return jnp.asarray(idx[:, None] == idx[None, :], BF16)


def _mlstm_out(hf, hb, o, norm, tm):
    r, w = hf.shape
    blk = pl.BlockSpec((tm, w), lambda i: (i, 0))
    return pl.pallas_call(
        _mlstm_out_kernel,
        out_shape=jax.ShapeDtypeStruct((r, w), BF16),
        grid=(r // tm,),
        in_specs=[blk, blk, blk,
                  pl.BlockSpec((1, w), lambda i: (0, 0)),
                  pl.BlockSpec((w, w), lambda i: (0, 0))],
        out_specs=blk,
        compiler_params=_params("parallel"),
        name="mlstm_out",
    )(hf, hb, o, norm.reshape(1, w), _head_ones(w))


def _swa_kernel(q_ref, k0, k1, k2, k3, v0, v1, v2, v3, kc_ref, vc_ref, sink_ref, o_ref, *, seq):
    w, dh = SWA_BLOCK, HEAD_DIM
    tq = q_ref.shape[1]
    g = q_ref.shape[0] // dh
    i = pl.program_id(2)
    k_loc = jnp.concatenate([k0[0, 0], k1[0, 0], k2[0, 0], k3[0, 0]], axis=0)
    v_loc = jnp.concatenate([v0[0, 0], v1[0, 0], v2[0, 0], v3[0, 0]], axis=1)
    kpos = (2 * i - 1) * w + lax.broadcasted_iota(I32, (4 * w, tq), 0)
    qpos = i * tq + lax.broadcasted_iota(I32, (4 * w, tq), 1)
    valid = (jnp.abs(qpos - kpos) <= SWA_WINDOW) & (kpos >= 0) & (kpos < seq)
    qs = [q_ref[h * dh:(h + 1) * dh, :] for h in range(g)]
    s_loc = [jnp.where(valid, jnp.dot(k_loc, qs[h], preferred_element_type=F32), NEG_INF) for h in range(g)]
    s_ctx = [jnp.dot(kc_ref[0, 0], qs[h], preferred_element_type=F32) for h in range(g)]
    outs = []
    for h in range(g):
        sink = sink_ref[0, h:h + 1, :]
        m = jnp.maximum(jnp.maximum(jnp.max(s_loc[h], axis=0, keepdims=True),
                                    jnp.max(s_ctx[h], axis=0, keepdims=True)), sink)
        p_loc = jnp.exp2(s_loc[h] - m)
        p_ctx = jnp.exp2(s_ctx[h] - m)
        den = jnp.sum(p_loc, axis=0, keepdims=True) + jnp.sum(p_ctx, axis=0, keepdims=True) + jnp.exp2(sink - m)
        o = (jnp.dot(v_loc, p_loc.astype(BF16), preferred_element_type=F32)
             + jnp.dot(vc_ref[0, 0], p_ctx.astype(BF16), preferred_element_type=F32))
        outs.append(o / den)
    o_ref[...] = jnp.concatenate(outs, axis=0).T.astype(o_ref.dtype)


def _swa(qt, k, vt, k_ctx, vt_ctx, sink2, b, s):
    hkv, dh = k.shape[1], k.shape[3]
    nh = qt.shape[0] // dh
    g = nh // hkv
    w = SWA_BLOCK
    tq = 2 * w
    nb, nq = s // w, s // tq
    lc = k_ctx.shape[2]
    clampi = lambda j: jnp.clip(j, 0, nb - 1)
    kspec = lambda o: pl.BlockSpec((1, 1, w, dh), lambda bi, hi, i: (bi, hi, clampi(2 * i + o), 0))
    vspec = lambda o: pl.BlockSpec((1, 1, dh, w), lambda bi, hi, i: (bi, hi, 0, clampi(2 * i + o)))
    return pl.pallas_call(
        functools.partial(_swa_kernel, seq=s),
        out_shape=jax.ShapeDtypeStruct((b * s, nh * dh), BF16),
        grid=(b, hkv, nq),
        in_specs=[pl.BlockSpec((g * dh, tq), lambda bi, hi, i: (hi, bi * nq + i)),
                  kspec(-1), kspec(0), kspec(1), kspec(2), vspec(-1), vspec(0), vspec(1), vspec(2),
                  pl.BlockSpec((1, 1, lc, dh), lambda bi, hi, i: (bi, hi, 0, 0)),
                  pl.BlockSpec((1, 1, dh, lc), lambda bi, hi, i: (bi, hi, 0, 0)),
                  pl.BlockSpec((1, g, tq), lambda bi, hi, i: (hi, 0, 0))],
        out_specs=pl.BlockSpec((tq, g * dh), lambda bi, hi, i: (bi * nq + i, hi)),
        compiler_params=_params("parallel", "parallel", "arbitrary"),
        name="window_attention",
    )(qt, k, k, k, k, vt, vt, vt, vt, k_ctx, vt_ctx, sink2)


def _ctx_attn_kernel(q_ref, k_ref, v_ref, sink_ref, o_ref, *, use_sink):
    g, lq, dh = q_ref.shape[1:]
    q = q_ref[0].reshape(g * lq, dh)
    s = lax.dot_general(q, k_ref[0, 0], (((1,), (1,)), ((), ())), preferred_element_type=F32) * ATTN_SCALE
    m = jnp.max(s, axis=1, keepdims=True)
    if use_sink:
        m = jnp.maximum(m, sink_ref[0])
    p = jnp.exp(s - m)
    den = jnp.sum(p, axis=1, keepdims=True)
    if use_sink:
        den = den + jnp.exp(sink_ref[0] - m)
    o = jnp.dot(p.astype(BF16), v_ref[0, 0], preferred_element_type=F32) / den
    o_ref[0] = o.reshape(g, lq, dh).astype(o_ref.dtype)


def _ctx_attn(q, k, v, sink_col, use_sink):
    b, nh, lq, dh = q.shape
    hkv = k.shape[1]
    g = nh // hkv
    qblk = pl.BlockSpec((1, g, lq, dh), lambda bi, hi: (bi, hi, 0, 0))
    kblk = pl.BlockSpec((1, 1, k.shape[2], dh), lambda bi, hi: (bi, hi, 0, 0))
    return pl.pallas_call(
        functools.partial(_ctx_attn_kernel, use_sink=use_sink),
        out_shape=jax.ShapeDtypeStruct(q.shape, BF16),
        grid=(b, hkv),
        in_specs=[qblk, kblk, kblk, pl.BlockSpec((1, g * lq, 1), lambda bi, hi: (hi, 0, 0))],
        out_specs=qblk,
        compiler_params=_params("parallel", "parallel"),
        name="context_attention",
    )(q, k, v, sink_col)


def _flash_kernel(q_ref, k_ref, v_ref, o_ref, m_scr, l_scr, acc_scr, sa_scr, sb_scr, *, tk):
    dh = HEAD_DIM
    g = q_ref.shape[0] // dh
    n = k_ref.shape[2] // tk
    m_scr[...] = jnp.full_like(m_scr, NEG_INF)
    l_scr[...] = jnp.zeros_like(l_scr)
    acc_scr[...] = jnp.zeros_like(acc_scr)

    def scores(j, dst):
        kk = k_ref[0, 0, pl.ds(pl.multiple_of(j * tk, tk), tk), :]
        for h in range(g):
            dst[h] = jnp.dot(kk, q_ref[h * dh:(h + 1) * dh, :], preferred_element_type=F32)

    def update(j, src):
        vv = v_ref[0, 0, :, pl.ds(pl.multiple_of(j * tk, tk), tk)]
        ss = [src[h] for h in range(g)]
        m_olds = [m_scr[h] for h in range(g)]
        m_news = [jnp.maximum(m_olds[h], jnp.max(ss[h], axis=0, keepdims=True)) for h in range(g)]
        ps = [jnp.exp2(ss[h] - m_news[h]) for h in range(g)]
        pvs = [jnp.dot(vv, ps[h].astype(BF16), preferred_element_type=F32) for h in range(g)]
        for h in range(g):
            alpha = jnp.exp2(m_olds[h] - m_news[h])
            l_scr[h] = alpha * l_scr[h] + jnp.sum(ps[h], axis=0, keepdims=True)
            acc_scr[h] = alpha * acc_scr[h] + pvs[h]
            m_scr[h] = m_news[h]

    scores(0, sa_scr)

    def body(i, carry):
        scores(2 * i + 1, sb_scr)
        update(2 * i, sa_scr)
        scores(2 * i + 2, sa_scr)
        update(2 * i + 1, sb_scr)
        return carry

    lax.fori_loop(0, (n - 1) // 2, body, 0)
    if n % 2 == 1:
        update(n - 1, sa_scr)
    else:
        scores(n - 1, sb_scr)
        update(n - 2, sa_scr)
        update(n - 1, sb_scr)
    out = (acc_scr[...] / l_scr[...]).reshape(g * dh, -1)
    o_ref[...] = out.T.astype(o_ref.dtype)


def _flash(qt, k, vt, b, s, tq=None, tk=None):
    w = qt.shape[0]
    hkv, t, dh = k.shape[1], k.shape[2], k.shape[3]
    g = w // dh // hkv
    tq = tq or _tile(s, (256, 128))
    tk = tk or _tile(t, (256, 128))
    nq = s // tq
    return pl.pallas_call(
        functools.partial(_flash_kernel, tk=tk),
        out_shape=jax.ShapeDtypeStruct((b * s, w), BF16),
        grid=(b, hkv, nq),
        in_specs=[pl.BlockSpec((g * dh, tq), lambda bi, hi, i: (hi, bi * nq + i)),
                  pl.BlockSpec((1, 1, t, dh), lambda bi, hi, i: (bi, hi, 0, 0)),
                  pl.BlockSpec((1, 1, dh, t), lambda bi, hi, i: (bi, hi, 0, 0))],
        out_specs=pl.BlockSpec((tq, g * dh), lambda bi, hi, i: (bi * nq + i, hi)),
        scratch_shapes=[pltpu.VMEM((g, 1, tq), F32), pltpu.VMEM((g, 1, tq), F32), pltpu.VMEM((g, dh, tq), F32),
                        pltpu.VMEM((g, tk, tq), F32), pltpu.VMEM((g, tk, tq), F32)],
        compiler_params=_params("parallel", "parallel", "arbitrary"),
        name="dense_attention",
    )(qt, k, vt)


def _bmm_kernel(a_ref, b_ref, o_ref):
    o_ref[0] = jnp.dot(a_ref[0], b_ref[0], preferred_element_type=F32).astype(o_ref.dtype)


def _group_matmul(a, bmat, name):
    g, m, k = a.shape
    n = bmat.shape[2]
    tm = _tile(m, (1024, 512, 256, 128, 64, 32, 16, 8))
    return pl.pallas_call(
        _bmm_kernel,
        out_shape=jax.ShapeDtypeStruct((g, m, n), F32),
        grid=(g, m // tm),
        in_specs=[pl.BlockSpec((1, tm, k), lambda gi, i: (gi, i, 0)),
                  pl.BlockSpec((1, k, n), lambda gi, i: (gi, 0, 0))],
        out_specs=pl.BlockSpec((1, tm, n), lambda gi, i: (gi, i, 0)),
        compiler_params=_params("parallel", "parallel"),
        name=name,
    )(a, bmat)


def _s5_scan_kernel(sre_ref, sim_ref, are_ref, aim_ref, zre_ref, zim_ref):
    nd, nsteps = sre_ref.shape[0], sre_ref.shape[1]
    for d in range(nd):
        a_re, a_im = are_ref[d], aim_ref[d]

        def body(i, carry, d=d, a_re=a_re, a_im=a_im):
            z_re, z_im = carry
            zre_ref[d, i] = z_re
            zim_ref[d, i] = z_im
            return (a_re * z_re - a_im * z_im + sre_ref[d, i], a_re * z_im + a_im * z_re + sim_ref[d, i])

        zero = jnp.zeros(sre_ref.shape[2:], F32)
        lax.fori_loop(0, nsteps, body, (zero, zero))


def _s5_scan(s_re, s_im, a_re, a_im):
    full = lambda arr: pl.BlockSpec(arr.shape, lambda i: (0,) * arr.ndim)
    return pl.pallas_call(
        _s5_scan_kernel,
        out_shape=[jax.ShapeDtypeStruct(s_re.shape, F32)] * 2,
        grid=(1,),
        in_specs=[full(s_re), full(s_im), full(a_re), full(a_im)],
        out_specs=[full(s_re), full(s_im)],
        compiler_params=_params("arbitrary"),
        name="s5_chunk_scan",
    )(s_re, s_im, a_re, a_im)


def _s5_glu_kernel(y_ref, u_ref, d_ref, w_ref, b_ref, o_ref):
    y = y_ref[...] + d_ref[...] * u_ref[...]
    y = jax.nn.gelu(y)
    gate = jnp.dot(y.astype(BF16), w_ref[...], preferred_element_type=F32) + b_ref[...]
    o_ref[...] = (y * jax.nn.sigmoid(gate)).astype(o_ref.dtype)


def _s5_glu(y, u, d_skip, w_glu, b_glu, tm):
    r, c = y.shape
    blk = pl.BlockSpec((tm, c), lambda i: (i, 0))
    vec = pl.BlockSpec((1, c), lambda i: (0, 0))
    return pl.pallas_call(
        _s5_glu_kernel,
        out_shape=jax.ShapeDtypeStruct((r, c), BF16),
        grid=(r // tm,),
        in_specs=[blk, blk, vec, pl.BlockSpec((c, c), lambda i: (0, 0)), vec],
        out_specs=blk,
        compiler_params=_params("parallel"),
        name="s5_readout_glu",
    )(y, u, d_skip.reshape(1, c), w_glu.astype(BF16), b_glu.reshape(1, c))


def _s5_tables(lam_re, lam_im, log_dt, b_re, b_im, c_re, c_im):
    ln, p, gc = S5_CHUNK, S5_STATE, S5_GROUP
    lam = lax.complex(lam_re.astype(F32), lam_im.astype(F32))
    dt = jnp.exp(log_dt.astype(F32))[..., None]
    a_bar = jnp.exp(lam * dt)
    b_scale = (a_bar - 1.0) / lam
    b_mat = lax.complex(b_re.astype(F32), b_im.astype(F32))
    c_mat = lax.complex(c_re.astype(F32), c_im.astype(F32))
    tau = jnp.arange(ln + 1, dtype=F32)
    apow = jnp.exp((lam * dt)[:, :, None, :] * tau[None, None, :, None])
    drive = b_scale[..., None] * b_mat[None]
    kern = jnp.real(jnp.einsum('gcp,dgtp,dgpe->dgtce', c_mat, apow[:, :, :ln], drive))
    kc = jnp.concatenate([kern[1, :, :0:-1], (kern[0, :, :1] + kern[1, :, :1]), kern[0, :, 1:]], axis=1)
    kc_e = kc.transpose(0, 3, 1, 2).reshape(S5_GROUPS, gc, (2 * ln - 1) * gc)
    toep = jnp.stack([kc_e[:, :, (ln - 1 - s) * gc:(2 * ln - 1 - s) * gc] for s in range(ln)], axis=1)
    toep = toep.reshape(S5_GROUPS, ln * gc, ln * gc)
    w_f = apow[0, :, ln - 1::-1][:, :ln, :, None] * drive[0][:, None]
    w_b = apow[1, :, :ln, :, None] * drive[1][:, None]
    def m_in(wc):
        wt = wc.transpose(0, 1, 3, 2).reshape(S5_GROUPS, ln * gc, p)
        return jnp.concatenate([jnp.real(wt), jnp.imag(wt)], axis=-1)
    min_all = jnp.concatenate([m_in(w_f), m_in(w_b)], axis=-1)
    o_f = c_mat[:, None] * apow[0, :, 1:ln + 1][:, :, None, :]
    o_b = c_mat[:, None] * apow[1, :, ln:0:-1][:, :, None, :]
    def m_out(oc):
        ot = oc.transpose(0, 3, 1, 2).reshape(S5_GROUPS, p, ln * gc)
        return jnp.concatenate([jnp.real(ot), -jnp.imag(ot)], axis=1)
    rhs = jnp.concatenate([toep, m_out(o_f), m_out(o_b)], axis=1)
    a_l = apow[:, :, ln].reshape(2, 1, S5_GROUPS * p)
    return min_all.astype(BF16), rhs.astype(BF16), jnp.real(a_l), jnp.imag(a_l)


def _s5_mixer(u_x, u_c, tables, d_skip, w_glu, b_glu):
    min_all, rhs, a_re, a_im = tables
    b, s, _ = u_x.shape
    ln, p, gc, ng = S5_CHUNK, S5_STATE, S5_GROUP, S5_GROUPS

    def chunks(u):
        nc = u.shape[1] // ln
        return u.reshape(b, nc, ln, ng, gc).transpose(3, 0, 1, 2, 4).reshape(ng, b * nc, ln * gc).astype(BF16), nc

    ux, ncx = chunks(u_x)
    uc, ncc = chunks(u_c)
    sx = _group_matmul(ux, min_all, "s5_local_state").reshape(ng, b, ncx, 4, p)
    sc = _group_matmul(uc, min_all, "s5_local_state_ctx").reshape(ng, b, ncc, 4, p)

    def scan_order(part):
        f = jnp.concatenate([sc[:, :, :, part], sx[:, :, :, part]], axis=2)
        r = jnp.concatenate([sc[:, :, ::-1, part + 2], sx[:, :, ::-1, part + 2]], axis=2)
        return jnp.stack([f, r]).transpose(0, 3, 2, 1, 4).reshape(2, ncc + ncx, b, ng * p)

    z_re, z_im = _s5_scan(scan_order(0), scan_order(1), a_re, a_im)

    def latent(z, d):
        zl = z[d, ncc:]
        if d == 1:
            zl = zl[::-1]
        return zl.reshape(ncx, b, ng, p).transpose(2, 1, 0, 3).reshape(ng, b * ncx, p)

    lhs = jnp.concatenate([ux, latent(z_re, 0).astype(BF16), latent(z_im, 0).astype(BF16),
                           latent(z_re, 1).astype(BF16), latent(z_im, 1).astype(BF16)], axis=-1)
    y = _group_matmul(lhs, rhs, "s5_outputs")
    y = y.reshape(ng, b, ncx, ln, gc).transpose(1, 2, 3, 0, 4).reshape(b * s, S5_CHANNELS)
    return _s5_glu(y, u_x.reshape(b * s, S5_CHANNELS), d_skip, w_glu, b_glu, _tile(b * s, (1024, 512, 256)))


def _out_proj_kernel(x_ref, a_ref, b_ref, wa_ref, wb_ref, gt_ref, g_ref, sc_ref, sh_ref, wr_ref, br_ref,
                     xo_ref, xn_ref, lg_ref):
    y = (jnp.dot(a_ref[...], wa_ref[...], preferred_element_type=F32)
         + jnp.dot(b_ref[...], wb_ref[...], preferred_element_type=F32))
    x = x_ref[...] + gt_ref[0] * y
    xo_ref[...] = x
    xn = x * lax.rsqrt(jnp.mean(x * x, axis=-1, keepdims=True) + EPS) * g_ref[...]
    xn = xn * (1.0 + sc_ref[0]) + sh_ref[0]
    hi = xn.astype(BF16)
    xn_ref[...] = hi
    lo = (xn - hi.astype(F32)).astype(BF16)
    part = jnp.dot(hi, wr_ref[...], preferred_element_type=F32)
    part = part[:, :LANES] + part[:, LANES:] + jnp.dot(lo, wr_ref[:, :LANES], preferred_element_type=F32)
    lg_ref[...] = part[:, :lg_ref.shape[1]] + br_ref[...]


def _out_proj(x, row0, a, bmix, wa, wb, gate, g_ffn, scale, shift, mod_map, w_router, b_router, tm):
    r = a.shape[0]
    d = x.shape[1]
    t0 = row0 // tm
    ne = w_router.shape[1]
    w_hi = w_router.astype(BF16)
    w_lo = (w_router - w_hi.astype(F32)).astype(BF16)
    w_router = jnp.concatenate([_pad_cols(w_hi), _pad_cols(w_lo)], axis=1)
    row = lambda w: pl.BlockSpec((tm, w), lambda i: (i, 0))
    mod = pl.BlockSpec((1, 1, d), lambda i: (mod_map(i), 0, 0))
    full = lambda arr: pl.BlockSpec(arr.shape, lambda i: (0, 0))
    return pl.pallas_call(
        _out_proj_kernel,
        out_shape=[jax.ShapeDtypeStruct((r, d), F32), jax.ShapeDtypeStruct((r, d), BF16),
                   jax.ShapeDtypeStruct((r, ne), F32)],
        grid=(r // tm,),
        in_specs=[pl.BlockSpec((tm, d), lambda i: (i + t0, 0)), row(a.shape[1]), row(bmix.shape[1]),
                  full(wa), full(wb), mod, pl.BlockSpec((1, d), lambda i: (0, 0)), mod, mod,
                  full(w_router), pl.BlockSpec((1, ne), lambda i: (0, 0))],
        out_specs=[row(d), row(d), row(ne)],
        compiler_params=_params("parallel"),
        name="out_proj",
    )(x, a, bmix, wa, wb, gate, g_ffn.reshape(1, d), scale, shift, w_router, b_router.reshape(1, ne))


def _route_kernel(lg_ref, idx_ref, gate_ref, pos_ref, cnt_ref, carry):
    tm, ne = lg_ref.shape

    @pl.when(pl.program_id(0) == 0)
    def _():
        carry[...] = jnp.zeros_like(carry)

    work = lg_ref[...]
    lane = lax.broadcasted_iota(I32, (tm, ne), 1).astype(F32)
    out_lane = lax.broadcasted_iota(I32, (tm, LANES), 1)
    vals, hots, idx_out = [], [], jnp.zeros((tm, LANES), I32)
    for kk in range(TOP_K):
        mx = jnp.max(work, axis=1, keepdims=True)
        idx = jnp.min(jnp.where(work == mx, lane, float(ne)), axis=1, keepdims=True)
        hot = lane == idx
        work = jnp.where(hot, -jnp.inf, work)
        vals.append(mx)
        hots.append(hot)
        idx_out = jnp.where(out_lane == kk, idx.astype(I32), idx_out)
    exps = [jnp.exp(vv - vals[0]) for vv in vals]
    tot = exps[0] + exps[1] + exps[2] + exps[3]
    multi = (hots[0] | hots[1] | hots[2] | hots[3]).astype(BF16)
    r_i = lax.broadcasted_iota(I32, (tm, tm), 0)
    c_i = lax.broadcasted_iota(I32, (tm, tm), 1)
    before = (r_i > c_i).astype(BF16)
    prefix = jnp.dot(before, multi, preferred_element_type=F32) + carry[...]
    gate_out = jnp.zeros((tm, LANES), F32)
    pos_out = jnp.zeros((tm, LANES), I32)
    for kk in range(TOP_K):
        gate_out = jnp.where(out_lane == kk, exps[kk] / tot, gate_out)
        pos = jnp.sum(jnp.where(hots[kk], prefix, 0.0), axis=1, keepdims=True).astype(I32)
        pos_out = jnp.where(out_lane == kk, pos, pos_out)
    idx_ref[...] = idx_out
    gate_ref[...] = gate_out
    pos_ref[...] = pos_out
    carry[...] = carry[...] + jnp.sum(multi.astype(F32), axis=0, keepdims=True)
    cnt_ref[...] = carry[...]


def _route(logits, tm):
    n, ne = logits.shape
    wide = pl.BlockSpec((tm, LANES), lambda i: (i, 0))
    return pl.pallas_call(
        _route_kernel,
        out_shape=[jax.ShapeDtypeStruct((n, LANES), I32), jax.ShapeDtypeStruct((n, LANES), F32),
                   jax.ShapeDtypeStruct((n, LANES), I32), jax.ShapeDtypeStruct((1, ne), F32)],
        grid=(n // tm,),
        in_specs=[pl.BlockSpec((tm, ne), lambda i: (i, 0))],
        out_specs=[wide, wide, wide, pl.BlockSpec((1, ne), lambda i: (0, 0))],
        scratch_shapes=[pltpu.VMEM((1, ne), F32)],
        compiler_params=_params("arbitrary"),
        name="moe_route",
    )(logits)


def _gate_up_prep_kernel(w_ref, p_ref, o_ref):
    for t in range(w_ref.shape[2] // p_ref.shape[0]):
        sl = slice(t * p_ref.shape[0], (t + 1) * p_ref.shape[0])
        o_ref[0, :, sl] = jnp.dot(w_ref[0, :, sl].astype(BF16), p_ref[...], preferred_element_type=F32).astype(BF16)


def _gate_up_prep(w_gate_up):
    ne, d, ff2 = w_gate_up.shape
    tile = 2 * LANES
    j = np.arange(tile)
    src = np.where(j < LANES, 2 * j, 2 * (j - LANES) + 1)
    perm = jnp.asarray(np.arange(tile)[:, None] == src[None, :], BF16)
    tr = _tile(d, (512, 256, 128))
    return pl.pallas_call(
        _gate_up_prep_kernel,
        out_shape=jax.ShapeDtypeStruct((ne, d, ff2), BF16),
        grid=(ne, d // tr),
        in_specs=[pl.BlockSpec((1, tr, ff2), lambda e, i: (e, i, 0)),
                  pl.BlockSpec((tile, tile), lambda e, i: (0, 0))],
        out_specs=pl.BlockSpec((1, tr, ff2), lambda e, i: (e, i, 0)),
        compiler_params=_params("parallel", "parallel"),
        name="moe_gate_up_prep",
    )(w_gate_up, perm)


def _expert_kernel(be_ref, nb_ref, x_ref, wgu_ref, bgu_ref, wd_ref, bd_ref, o_ref):
    i = pl.program_id(0)

    @pl.when(i < nb_ref[0])
    def _():
        h = jnp.dot(x_ref[...], wgu_ref[0], preferred_element_type=F32) + bgu_ref[0]
        acts = []
        for t in range(h.shape[1] // (2 * LANES)):
            glu = jnp.minimum(h[:, 2 * LANES * t:2 * LANES * t + LANES], SWIGLU_LIMIT)
            lin = jnp.clip(h[:, 2 * LANES * t + LANES:2 * LANES * (t + 1)], -SWIGLU_LIMIT, SWIGLU_LIMIT)
            acts.append((glu * jax.nn.sigmoid(SWIGLU_ALPHA * glu) * (lin + 1.0)).astype(BF16))
        act = jnp.concatenate(acts, axis=1)
        o_ref[...] = (jnp.dot(act, wd_ref[0], preferred_element_type=F32) + bd_ref[0]).astype(o_ref.dtype)

    @pl.when(i >= nb_ref[0])
    def _():
        o_ref[...] = jnp.zeros_like(o_ref)


def _experts(xs, blk_e, n_used, wgu, bgu, wd, bd):
    n, d = xs.shape
    ff2 = wgu.shape[2]
    nblk = n // MOE_ROWS
    wspec = lambda shape: pl.BlockSpec((1,) + shape, lambda i, be, nb: (be[i], 0, 0))
    return pl.pallas_call(
        _expert_kernel,
        out_shape=jax.ShapeDtypeStruct((n, d), BF16),
        grid_spec=pltpu.PrefetchScalarGridSpec(
            num_scalar_prefetch=2,
            grid=(nblk,),
            in_specs=[pl.BlockSpec((MOE_ROWS, d), lambda i, be, nb: (i, 0)),
                      wspec((d, ff2)), wspec((1, ff2)), wspec((ff2 // 2, d)), wspec((1, d))],
            out_specs=pl.BlockSpec((MOE_ROWS, d), lambda i, be, nb: (i, 0))),
        compiler_params=_params("arbitrary"),
        name="moe_experts",
    )(blk_e, n_used, xs, wgu, bgu, wd, bd)


def _combine_kernel(x_ref, y_ref, gate_ref, gt_ref, g_ref, o_ref, *, final_norm):
    gates = gate_ref[...]
    y = gates[:, 0:1] * y_ref[0].astype(F32)
    for kk in range(1, TOP_K):
        y = y + gates[:, kk:kk + 1] * y_ref[kk].astype(F32)
    x = x_ref[...] + gt_ref[0] * y
    if final_norm:
        x = x * lax.rsqrt(jnp.mean(x * x, axis=-1, keepdims=True) + EPS) * g_ref[...]
    o_ref[...] = x


def _combine(x, y4, gates, gate_mod, mod_map, g_final, final_norm, tm):
    r, d = x.shape
    return pl.pallas_call(
        functools.partial(_combine_kernel, final_norm=final_norm),
        out_shape=jax.ShapeDtypeStruct((r, d), F32),
        grid=(r // tm,),
        in_specs=[pl.BlockSpec((tm, d), lambda i: (i, 0)),
                  pl.BlockSpec((TOP_K, tm, d), lambda i: (0, i, 0)),
                  pl.BlockSpec((tm, LANES), lambda i: (i, 0)),
                  pl.BlockSpec((1, 1, d), lambda i: (mod_map(i), 0, 0)),
                  pl.BlockSpec((1, d), lambda i: (0, 0))],
        out_specs=pl.BlockSpec((tm, d), lambda i: (i, 0)),
        compiler_params=_params("parallel"),
        name="moe_combine",
    )(x, y4, gates, gate_mod, g_final.reshape(1, d))


def _moe(x, xn, logits, gate_mod, mod_map, weights, g_final, final_norm, tm):
    wgu, bgu, wd, bd = weights
    n, d = xn.shape
    idx_w, gates_w, pos_w, counts = _route(logits, tm)
    idx, pos = idx_w[:, :TOP_K], pos_w[:, :TOP_K]
    counts = counts[0].astype(I32)
    padded = (counts + MOE_ROWS - 1) // MOE_ROWS * MOE_ROWS
    pad_end = jnp.cumsum(padded)
    pad_start = pad_end - padded
    dest = pad_start[idx] + pos
    nblk = n * TOP_K // MOE_ROWS + N_EXPERTS
    slots = nblk * MOE_ROWS
    src = jnp.zeros((slots,), I32).at[dest.reshape(-1)].set(
        jnp.arange(n * TOP_K, dtype=I32) // TOP_K, unique_indices=True, mode="promise_in_bounds")
    blk_row0 = jnp.arange(nblk, dtype=I32) * MOE_ROWS
    blk_e = jnp.minimum(jnp.sum((pad_end[None, :] <= blk_row0[:, None]).astype(I32), axis=1), N_EXPERTS - 1)
    n_used = (pad_end[-1:] // MOE_ROWS).astype(I32)
    xs = xn.at[src].get(mode="promise_in_bounds")
    ys = _experts(xs, blk_e, n_used, wgu, bgu, wd, bd)
    y4 = ys.at[dest.T.reshape(-1)].get(mode="promise_in_bounds").reshape(TOP_K, n, d)
    return _combine(x, y4, gates_w, gate_mod, mod_map, g_final, final_norm, tm)


def _moe_weights(w_gate_up, b_gate_up, w_down, b_down):
    ne, d, ff2 = w_gate_up.shape
    b_tiled = b_gate_up.reshape(ne, ff2 // (2 * LANES), LANES, 2).swapaxes(-1, -2).reshape(ne, 1, ff2)
    return _gate_up_prep(w_gate_up), b_tiled, w_down.astype(BF16), b_down[:, None, :]


def _rope_tables(n_tokens):
    rows = n_tokens // GRID_W
    row = jnp.repeat(jnp.arange(rows, dtype=I32), GRID_W).astype(F32)
    col = jnp.tile(jnp.arange(GRID_W, dtype=I32), rows).astype(F32)
    inv = ROPE_BASE ** (-jnp.arange(0, ROPE_AXIS_DIM, 2, dtype=F32) / ROPE_AXIS_DIM)
    ang_r, ang_c = row[:, None] * inv, col[:, None] * inv
    cos = jnp.concatenate([jnp.cos(ang_r)] * 2 + [jnp.cos(ang_c)] * 2, axis=-1)
    sin = jnp.concatenate([jnp.sin(ang_r)] * 2 + [jnp.sin(ang_c)] * 2, axis=-1)
    return cos, sin


def _rot_perm():
    q = ROPE_AXIS_DIM // 2
    d = np.arange(HEAD_DIM)
    first = (d % ROPE_AXIS_DIM) < q
    perm = np.where(first, d + q, d - q)
    sign = np.where(first, -1.0, 1.0).astype(np.float32)
    return perm, sign


def _rot_cols(w, n_heads):
    perm, sign = _rot_perm()
    k = w.shape[0]
    wh = w.reshape(k, n_heads, HEAD_DIM)
    return (wh[:, :, perm] * sign).reshape(k, n_heads * HEAD_DIM)


def _to_heads(t, b, n_heads):
    return t.reshape(b, -1, n_heads, HEAD_DIM).transpose(0, 2, 1, 3)


def _from_heads(t):
    b, h, tt, dh = t.shape
    return t.transpose(0, 2, 1, 3).reshape(b * tt, h * dh)


def _layer_even(xa, mod, b, s, lc, g_mix, g_ffn, w_in, w_out, gate_bias, mlstm_norm, sink, router, moe_w):
    d = xa.shape[1]
    nctx = b * lc
    tm = _tile(math.gcd(nctx, s), (512, 256, 128))
    mod_map = lambda i: jnp.where(i * tm < nctx, b, (i * tm - nctx) // s)
    sh1, sc1, gt1, sh2, sc2, gt2 = [mod[:, j][:, None, :] for j in range(6)]

    hm, hs, hk = MLSTM_HEADS * HEAD_DIM, SWA_HEADS * HEAD_DIM, SWA_KV_HEADS * HEAD_DIM
    o0 = np.cumsum([0, hm, hm, hm, hm, 4 * MLSTM_HEADS, hs, hk, hk])
    seg = lambda j: w_in[:, o0[j]:o0[j + 1]]
    w_cat = jnp.concatenate([seg(0), seg(1) * ATTN_SCALE, seg(2), seg(3),
                             seg(6), _rot_cols(seg(6), SWA_KV_HEADS), seg(7), _pad_cols(seg(4))], axis=1).astype(BF16)
    w_q = jnp.concatenate([seg(5), _rot_cols(seg(5), SWA_HEADS)], axis=1).T.astype(BF16)
    widths = [hm, hm, hm, hm, hk, hk, hk, LANES]
    dts = [BF16] * 7 + [F32]
    qa, ka, va, oa, kb, kb_r, vb, gts, qt, qt_r = _in_proj(xa, g_mix, sc1, sh1, mod_map, w_cat, widths, dts, tm,
                                                           wt=w_q, t_widths=[hs, hs], t_dtypes=[BF16, BF16])

    hf, hb = _mlstm(qa, ka, va, gts, _pad_cols(gate_bias.astype(F32).reshape(1, -1)), b, lc, s)
    mix_a = _mlstm_out(hf, hb, oa, mlstm_norm.reshape(-1), tm)

    cos, sin = _rope_tables(s)
    ones = jnp.ones((HEAD_DIM,), F32)
    g = SWA_HEADS // SWA_KV_HEADS
    heads = lambda t, n: _to_heads(t, b, n)
    log2e = math.log2(math.e)
    q_x = _q_prep_t(qt, qt_r, nctx, b * s, cos.T, sin.T, ones, ones, ATTN_SCALE * log2e, norm=False)
    k_x = _qk_prep(heads(kb[nctx:], SWA_KV_HEADS), heads(kb_r[nctx:], SWA_KV_HEADS), cos, sin, ones, ones,
                   norm=False, rope=True, scale=1.0)
    k_c, v_c, v_x = heads(kb[:nctx], SWA_KV_HEADS), heads(vb[:nctx], SWA_KV_HEADS), heads(vb[nctx:], SWA_KV_HEADS)
    q_c = heads(qt[:, :nctx].T, SWA_HEADS)
    sink_h = sink.astype(F32).reshape(SWA_KV_HEADS, g, 1)
    att_x = _swa(q_x, k_x, v_x.swapaxes(-1, -2), k_c, v_c.swapaxes(-1, -2),
                 jnp.broadcast_to(sink_h * log2e, (SWA_KV_HEADS, g, 2 * SWA_BLOCK)), b, s)
    att_c = _ctx_attn(q_c, k_c, v_c, jnp.broadcast_to(sink_h[..., None], (SWA_KV_HEADS, g, lc, 1))
                      .reshape(SWA_KV_HEADS, g * lc, 1), True)
    mix_b = jnp.concatenate([_from_heads(att_c), att_x], axis=0)

    w_router, b_router = router
    xa, xn2, logits = _out_proj(xa, 0, mix_a, mix_b, w_out[:hm].astype(BF16), w_out[hm:].astype(BF16), gt1, g_ffn,
                                sc2, sh2, mod_map, w_router.astype(F32), b_router.astype(F32), tm)
    return _moe(xa, xn2, logits, gt2, mod_map, moe_w, g_ffn, False, tm)


def _layer_odd_last(xa, mod, b, s, lc, g_mix, g_ffn, w_in, w_out, s5_params, d_skip, w_glu, b_glu,
                    q_norm, k_norm, router, moe_w, g_final):
    nctx = b * lc
    tm = _tile(math.gcd(nctx, s), (512, 256, 128))
    mod_map = lambda i: jnp.where(i * tm < nctx, b, (i * tm - nctx) // s)
    lat_map = lambda i: i * tm // s
    sh1, sc1, gt1, sh2, sc2, gt2 = [mod[:, j][:, None, :] for j in range(6)]

    hq, hk = ATT_HEADS * HEAD_DIM, ATT_KV_HEADS * HEAD_DIM
    o1 = np.cumsum([0, S5_CHANNELS, hq, hk, hk])
    seg = lambda j: w_in[:, o1[j]:o1[j + 1]]
    kpad = lambda w: _pad_cols(w, 2 * LANES)
    w_cat = jnp.concatenate([seg(0), kpad(seg(2)), kpad(_rot_cols(seg(2), ATT_KV_HEADS)), kpad(seg(3))],
                            axis=1).astype(BF16)
    w_q = jnp.concatenate([seg(1), _rot_cols(seg(1), ATT_HEADS)], axis=1).T.astype(BF16)
    widths = [S5_CHANNELS, 2 * LANES, 2 * LANES, 2 * LANES]
    dts = [F32, BF16, BF16, BF16]
    u, k, k_r, v, qt, qt_r = _in_proj(xa, g_mix, sc1, sh1, mod_map, w_cat, widths, dts, tm,
                                      wt=w_q, t_widths=[hq, hq], t_dtypes=[BF16, BF16])

    mix_a = _s5_mixer(u[nctx:].reshape(b, s, S5_CHANNELS), u[:nctx].reshape(b, lc, S5_CHANNELS),
                      _s5_tables(*s5_params), d_skip, w_glu, b_glu)

    cos, sin = _rope_tables(s)
    perm, _ = _rot_perm()
    qn, kn = q_norm.astype(F32), k_norm.astype(F32)
    heads = lambda t, n: _to_heads(t[:, :n * HEAD_DIM], b, n)
    q_x = _q_prep_t(qt, qt_r, nctx, b * s, cos.T, sin.T, qn, qn[perm], ATTN_SCALE * math.log2(math.e))
    k_x = _qk_prep(heads(k[nctx:], ATT_KV_HEADS), heads(k_r[nctx:], ATT_KV_HEADS), cos, sin, kn, kn[perm],
                   norm=True, rope=True, scale=1.0)
    kc_raw = heads(k[:nctx], ATT_KV_HEADS)
    k_c = _qk_prep(kc_raw, kc_raw, cos[:lc], sin[:lc], kn, kn, norm=True, rope=False, scale=1.0)
    k_all = jnp.concatenate([k_c, k_x], axis=2)
    v_all = jnp.concatenate([heads(v[:nctx], ATT_KV_HEADS), heads(v[nctx:], ATT_KV_HEADS)], axis=2)
    mix_b = _flash(q_x, k_all, v_all.swapaxes(-1, -2), b, s)

    w_router, b_router = router
    hs = S5_CHANNELS
    x, xn2, logits = _out_proj(xa, nctx, mix_a, mix_b, w_out[:hs].astype(BF16), w_out[hs:].astype(BF16), gt1, g_ffn,
                               sc2, sh2, lat_map, w_router.astype(F32), b_router.astype(F32), tm)
    return _moe(x, xn2, logits, gt2, lat_map, moe_w, g_final, True, tm)


def kernel(x, c, ctx, c_ctx, l0_w_mod, l0_b_mod, l0_g_mix, l0_g_ffn, l0_w_in, l0_w_out, l0_gate_bias, l0_mlstm_norm, l0_sink, l0_w_router, l0_b_router, l0_w_gate_up, l0_b_gate_up, l0_w_down, l0_b_down, l1_w_mod, l1_b_mod, l1_g_mix, l1_g_ffn, l1_w_in, l1_w_out, l1_lam_re, l1_lam_im, l1_log_dt, l1_b_re, l1_b_im, l1_c_re, l1_c_im, l1_d_skip, l1_w_glu, l1_b_glu, l1_q_norm, l1_k_norm, l1_w_router, l1_b_router, l1_w_gate_up, l1_b_gate_up, l1_w_down, l1_b_down, g_final):
    b, s, d = x.shape
    lc = ctx.shape[1]
    cond = jnp.concatenate([c, c_ctx[None, :]], axis=0)
    cond = jnp.pad(cond, ((0, (-(b + 1)) % 8), (0, 0)))
    mod0 = _silu_linear(cond, l0_w_mod, l0_b_mod)[:b + 1].reshape(b + 1, 6, d)
    mod1 = _silu_linear(cond, l1_w_mod, l1_b_mod)[:b + 1].reshape(b + 1, 6, d)

    xa = jnp.concatenate([ctx.reshape(b * lc, d), x.reshape(b * s, d)], axis=0)
    xa = _layer_even(xa, mod0, b, s, lc, l0_g_mix, l0_g_ffn, l0_w_in, l0_w_out, l0_gate_bias, l0_mlstm_norm,
                     l0_sink, (l0_w_router, l0_b_router),
                     _moe_weights(l0_w_gate_up, l0_b_gate_up, l0_w_down, l0_b_down))
    out = _layer_odd_last(xa, mod1, b, s, lc, l1_g_mix, l1_g_ffn, l1_w_in, l1_w_out,
                          (l1_lam_re, l1_lam_im, l1_log_dt, l1_b_re, l1_b_im, l1_c_re, l1_c_im),
                          l1_d_skip, l1_w_glu, l1_b_glu, l1_q_norm, l1_k_norm, (l1_w_router, l1_b_router),
                          _moe_weights(l1_w_gate_up, l1_b_gate_up, l1_w_down, l1_b_down), g_final)
    return out.reshape(b, s, d)
```

```python
import functools
import math

import jax
import jax.numpy as jnp
import numpy as np
from jax import lax
from jax.experimental import pallas as pl
from jax.experimental.pallas import tpu as pltpu

F32 = jnp.float32
BF16 = jnp.bfloat16
I32 = jnp.int32

GRID_W = 64
HEAD_DIM = 64
ATTN_SCALE = HEAD_DIM ** -0.5
ROPE_AXIS_DIM = HEAD_DIM // 2
ROPE_BASE = 10000.0
EPS = 1e-6
NEG_INF = -1e30

MLSTM_HEADS = 8
MLSTM_CHUNKS = (256, 128, 64)
SWA_HEADS = 8
SWA_KV_HEADS = 2
SWA_WINDOW = 128
SWA_BLOCK = 128
S5_CHANNELS = 256
S5_GROUP = 16
S5_GROUPS = S5_CHANNELS // S5_GROUP
S5_STATE = 64
S5_CHUNK = 64
ATT_HEADS = 12
ATT_KV_HEADS = 3
N_EXPERTS = 32
TOP_K = 4
SWIGLU_LIMIT = 7.0
SWIGLU_ALPHA = 1.702

LANES = 128
VMEM_LIMIT = 56 * 1024 * 1024
MOE_ROWS = 512
MOE_PARTS = 2
HIGHEST = lax.Precision.HIGHEST


def _params(*sem):
    return pltpu.CompilerParams(dimension_semantics=sem, vmem_limit_bytes=VMEM_LIMIT)


def _tile(n, prefs):
    for t in prefs:
        if n % t == 0:
            return t
    return n


def _pad_cols(w, mult=LANES):
    pad = (-w.shape[-1]) % mult
    if pad:
        w = jnp.pad(w, [(0, 0)] * (w.ndim - 1) + [(0, pad)])
    return w


def _linear_kernel(x_ref, w_ref, b_ref, o_ref):
    x = x_ref[...]
    x = x * jax.nn.sigmoid(x)
    o_ref[...] = jnp.dot(x, w_ref[...], precision=HIGHEST, preferred_element_type=F32) + b_ref[...]


def _silu_linear(x, w, b):
    m, k = x.shape
    n = w.shape[1]
    tn = _tile(n, (1024, 512, 256, 128))
    return pl.pallas_call(
        _linear_kernel,
        out_shape=jax.ShapeDtypeStruct((m, n), F32),
        grid=(n // tn,),
        in_specs=[pl.BlockSpec((m, k), lambda j: (0, 0)),
                  pl.BlockSpec((k, tn), lambda j: (0, j)),
                  pl.BlockSpec((1, tn), lambda j: (0, j))],
        out_specs=pl.BlockSpec((m, tn), lambda j: (0, j)),
        compiler_params=_params("arbitrary"),
        name="adaln_linear",
    )(x, w, b.reshape(1, n))


def _in_proj_kernel(x_ref, g_ref, sc_ref, sh_ref, w_ref, wt_ref, *out_refs, widths, t_widths):
    x = x_ref[...]
    xn = x * lax.rsqrt(jnp.mean(x * x, axis=-1, keepdims=True) + EPS) * g_ref[...]
    xb = (xn * (1.0 + sc_ref[0]) + sh_ref[0]).astype(BF16)
    off = 0
    for o_ref, w in zip(out_refs, widths):
        o_ref[...] = jnp.dot(xb, w_ref[:, off:off + w], preferred_element_type=F32).astype(o_ref.dtype)
        off += w
    off = 0
    for o_ref, w in zip(out_refs[len(widths):], t_widths):
        o_ref[...] = lax.dot_general(wt_ref[off:off + w, :], xb, (((1,), (1,)), ((), ())),
                                     preferred_element_type=F32).astype(o_ref.dtype)
        off += w


def _in_proj(x, g, scale, shift, mod_map, w, widths, dtypes, tm, wt=None, t_widths=(), t_dtypes=()):
    r, d = x.shape
    if wt is None:
        wt = jnp.zeros((8, d), BF16)
    return pl.pallas_call(
        functools.partial(_in_proj_kernel, widths=tuple(widths), t_widths=tuple(t_widths)),
        out_shape=([jax.ShapeDtypeStruct((r, wd), dt) for wd, dt in zip(widths, dtypes)]
                   + [jax.ShapeDtypeStruct((wd, r), dt) for wd, dt in zip(t_widths, t_dtypes)]),
        grid=(r // tm,),
        in_specs=[pl.BlockSpec((tm, d), lambda i: (i, 0)),
                  pl.BlockSpec((1, d), lambda i: (0, 0)),
                  pl.BlockSpec((1, 1, d), lambda i: (mod_map(i), 0, 0)),
                  pl.BlockSpec((1, 1, d), lambda i: (mod_map(i), 0, 0)),
                  pl.BlockSpec(w.shape, lambda i: (0, 0)),
                  pl.BlockSpec(wt.shape, lambda i: (0, 0))],
        out_specs=([pl.BlockSpec((tm, wd), lambda i: (i, 0)) for wd in widths]
                   + [pl.BlockSpec((wd, tm), lambda i: (0, i)) for wd in t_widths]),
        compiler_params=_params("parallel"),
        name="in_proj",
    )(x, g.reshape(1, d), scale, shift, w, wt)


def _qk_prep_kernel(x_ref, xr_ref, cos_ref, sin_ref, g_ref, gr_ref, o_ref, *, norm, rope, scale):
    x = x_ref[0].astype(F32)
    if norm:
        s = lax.rsqrt(jnp.mean(x * x, axis=-1, keepdims=True) + EPS)
        x = x * s * g_ref[...]
    if rope:
        xr = xr_ref[0].astype(F32)
        if norm:
            xr = xr * s * gr_ref[...]
        x = x * cos_ref[...] + xr * sin_ref[...]
    o_ref[0] = (x * scale).astype(o_ref.dtype)


def _qk_prep(x, x_rot, cos, sin, gain, gain_rot, *, norm, rope, scale):
    b, h, t, dh = x.shape
    tt = _tile(t, (512, 256, 128))
    blk = pl.BlockSpec((1, h, tt, dh), lambda i, j: (i, 0, j, 0))
    tab = pl.BlockSpec((tt, dh), lambda i, j: (j, 0))
    vec = pl.BlockSpec((1, dh), lambda i, j: (0, 0))
    return pl.pallas_call(
        functools.partial(_qk_prep_kernel, norm=norm, rope=rope, scale=scale),
        out_shape=jax.ShapeDtypeStruct(x.shape, BF16),
        grid=(b, t // tt),
        in_specs=[blk, blk, tab, tab, vec, vec],
        out_specs=blk,
        compiler_params=_params("parallel", "parallel"),
        name="qk_prep",
    )(x, x_rot, cos, sin, gain.reshape(1, dh), gain_rot.reshape(1, dh))


def _q_prep_t_kernel(x_ref, xr_ref, cos_ref, sin_ref, g_ref, gr_ref, o_ref, *, norm, scale):
    w, tt = x_ref.shape
    nh = w // HEAD_DIM
    x = x_ref[...].astype(F32).reshape(nh, HEAD_DIM, tt)
    xr = xr_ref[...].astype(F32).reshape(nh, HEAD_DIM, tt)
    if norm:
        s = lax.rsqrt(jnp.mean(x * x, axis=1, keepdims=True) + EPS)
        x, xr = x * s * g_ref[...], xr * s * gr_ref[...]
    y = x * cos_ref[...] + xr * sin_ref[...]
    o_ref[...] = (y * scale).reshape(w, tt).astype(o_ref.dtype)


def _q_prep_t(xt, xt_rot, col0, n_cols, cos_t, sin_t, gain, gain_rot, scale, norm=True):
    w = xt.shape[0]
    s = cos_t.shape[1]
    tt = _tile(math.gcd(col0, s), (512, 256, 128))
    blk = pl.BlockSpec((w, tt), lambda j: (0, j + col0 // tt))
    tab = pl.BlockSpec((HEAD_DIM, tt), lambda j: (0, j % (s // tt)))
    vec = pl.BlockSpec((HEAD_DIM, 1), lambda j: (0, 0))
    return pl.pallas_call(
        functools.partial(_q_prep_t_kernel, norm=norm, scale=scale),
        out_shape=jax.ShapeDtypeStruct((w, n_cols), BF16),
        grid=(n_cols // tt,),
        in_specs=[blk, blk, tab, tab, vec, vec],
        out_specs=pl.BlockSpec((w, tt), lambda j: (0, j)),
        compiler_params=_params("parallel"),
        name="q_prep_t",
    )(xt, xt_rot, cos_t, sin_t, gain.reshape(HEAD_DIM, 1), gain_rot.reshape(HEAD_DIM, 1))


def _log_sigmoid(x):
    return jnp.minimum(x, 0.0) - jnp.log(1.0 + jnp.exp(-jnp.abs(x)))


def _mlstm_kernel(qf, kf, vf, gf, qb, kb, vb, gb, bias_ref, hf_ref, hb_ref, c_scr, n_scr, m_scr):
    ch, nh, dh = qf.shape[0], MLSTM_HEADS, HEAD_DIM

    @pl.when(pl.program_id(1) == 0)
    def _():
        c_scr[...] = jnp.zeros_like(c_scr)
        n_scr[...] = jnp.zeros_like(n_scr)
        m_scr[...] = jnp.full_like(m_scr, NEG_INF)

    row = lax.broadcasted_iota(I32, (ch, ch), 0)
    col = lax.broadcasted_iota(I32, (ch, ch), 1)
    lane_lo = lax.broadcasted_iota(I32, (ch, 2 * dh), 1) < dh
    row_lo = lax.broadcasted_iota(I32, (2 * dh, 2 * dh), 0) < dh
    col_lo = lax.broadcasted_iota(I32, (2 * dh, 2 * dh), 1) < dh
    vec_lo = lax.broadcasted_iota(I32, (1, 2 * dh), 1) < dh
    pair = lambda a, b: jnp.where(lane_lo, a, b)
    refs = ((qf, kf, vf, gf, hf_ref), (qb, kb, vb, gb, hb_ref))
    jobs = [(d, p) for d in range(2) for p in range(nh // 2)]
    sl = lambda p: slice(2 * dh * p, 2 * dh * (p + 1))

    qk = {}
    for d, p in jobs:
        q2, k2 = refs[d][0][:, sl(p)], refs[d][1][:, sl(p)]
        zero = jnp.zeros_like(q2)
        q_st = jnp.concatenate([jnp.where(lane_lo, q2, zero), jnp.where(lane_lo, zero, q2)], axis=0)
        qk[d, p] = lax.dot_general(q_st, k2, (((1,), (1,)), ((), ())), preferred_element_type=F32)

    st = {}
    for d in range(2):
        seen = (col <= row) if d == 0 else (col >= row)
        tri = seen.astype(F32)
        tri_t = ((row <= col) if d == 0 else (row >= col)).astype(F32)
        last = ch - 1 if d == 0 else 0
        g = refs[d][3][...] + bias_ref[...]
        g_t = g.T
        lo = 2 * nh * d
        li_col, lf_col = g[:, lo:lo + nh], _log_sigmoid(g[:, lo + nh:lo + 2 * nh])
        li_row, lf_row = g_t[lo:lo + nh, :], _log_sigmoid(g_t[lo + nh:lo + 2 * nh, :])
        b_col = jnp.dot(tri, lf_col, precision=HIGHEST, preferred_element_type=F32)
        b_row = jnp.dot(lf_row, tri_t, precision=HIGHEST, preferred_element_type=F32)
        for h in range(nh):
            bc, br = b_col[:, h:h + 1], b_row[h:h + 1, :]
            lir, lic = li_row[h:h + 1, :], li_col[:, h:h + 1]
            b_last = br[:, last:last + 1]
            m_prev = m_scr[d, h // 2][:, (h % 2) * dh:(h % 2) * dh + 1]
            d_log = jnp.where(seen, bc - br + lir, NEG_INF)
            inter_log = bc + m_prev
            m_t = jnp.maximum(inter_log, jnp.max(d_log, axis=1, keepdims=True))
            w_log = b_last - bc + lic
            m_new = jnp.maximum(b_last + m_prev, jnp.max(w_log, axis=0, keepdims=True))
            st[d, h] = dict(dmat=jnp.exp(d_log - m_t), inter=jnp.exp(inter_log - m_t), floor=jnp.exp(-m_t),
                            m_new=m_new, decay=jnp.exp(b_last + m_prev - m_new), wn=jnp.exp(w_log - m_new))

    mm = {}
    for d, p in jobs:
        a, b = st[d, 2 * p], st[d, 2 * p + 1]
        q2, k2, v2 = refs[d][0][:, sl(p)], refs[d][1][:, sl(p)], refs[d][2][:, sl(p)]
        s = qk[d, p] * jnp.concatenate([a["dmat"], b["dmat"]], axis=0)
        kw = k2.astype(F32) * pair(a["wn"], b["wn"])
        mm[d, p] = dict(
            s_sum=jnp.sum(s, axis=1, keepdims=True), kw_sum=jnp.sum(kw, axis=0, keepdims=True),
            sv=jnp.dot(s.astype(BF16), v2, preferred_element_type=F32),
            q_c=jnp.dot(q2, c_scr[d, p].astype(BF16), preferred_element_type=F32),
            kv=lax.dot_general(kw.astype(BF16), v2, (((0,), (0,)), ((), ())), preferred_element_type=F32))

    for d, p in jobs:
        a, b, r = st[d, 2 * p], st[d, 2 * p + 1], mm[d, p]
        q2 = refs[d][0][:, sl(p)]
        n_prev = n_scr[d, p]
        num = pair(a["inter"], b["inter"]) * r["q_c"] + jnp.where(lane_lo, r["sv"][:ch], r["sv"][ch:])
        qn = q2.astype(F32) * n_prev
        qn_a = jnp.sum(jnp.where(lane_lo, qn, 0.0), axis=1, keepdims=True)
        qn_b = jnp.sum(jnp.where(lane_lo, 0.0, qn), axis=1, keepdims=True)
        den_a = jnp.maximum(jnp.abs(a["inter"] * qn_a + r["s_sum"][:ch]), a["floor"])
        den_b = jnp.maximum(jnp.abs(b["inter"] * qn_b + r["s_sum"][ch:]), b["floor"])
        refs[d][4][:, sl(p)] = num / pair(den_a, den_b)
        c_scr[d, p] = (jnp.where(row_lo, a["decay"], b["decay"]) * c_scr[d, p]
                       + jnp.where(row_lo == col_lo, r["kv"], 0.0))
        n_scr[d, p] = jnp.where(vec_lo, a["decay"], b["decay"]) * n_prev + r["kw_sum"]
        m_scr[d, p] = jnp.where(vec_lo, a["m_new"], b["m_new"])


def _mlstm(q, k, v, gates, bias, b, lc, s):
    r, w = q.shape
    ch = _tile(math.gcd(lc, s), MLSTM_CHUNKS)
    ncc, ncx = lc // ch, s // ch
    base = b * ncc

    def fwd(i, c):
        return jnp.where(c < ncc, i * ncc + c, base + i * ncx + (c - ncc)), 0

    def bwd(i, c):
        return jnp.where(c < ncc, i * ncc + (ncc - 1 - c), base + i * ncx + (ncx - 1 - (c - ncc))), 0

    spec = lambda width, m: pl.BlockSpec((ch, width), m)
    npair = MLSTM_HEADS // 2
    return pl.pallas_call(
        _mlstm_kernel,
        out_shape=[jax.ShapeDtypeStruct((r, w), F32)] * 2,
        grid=(b, ncc + ncx),
        in_specs=[spec(w, fwd), spec(w, fwd), spec(w, fwd), spec(LANES, fwd),
                  spec(w, bwd), spec(w, bwd), spec(w, bwd), spec(LANES, bwd),
                  pl.BlockSpec((1, LANES), lambda i, c: (0, 0))],
        out_specs=[spec(w, fwd), spec(w, bwd)],
        scratch_shapes=[pltpu.VMEM((2, npair, 2 * HEAD_DIM, 2 * HEAD_DIM), F32),
                        pltpu.VMEM((2, npair, 1, 2 * HEAD_DIM), F32),
                        pltpu.VMEM((2, npair, 1, 2 * HEAD_DIM), F32)],
        compiler_params=_params("parallel", "arbitrary"),
        name="mlstm_scan",
    )(q, k, v, gates, q, k, v, gates, bias)


def _mlstm_out_kernel(hf_ref, hb_ref, o_ref, nrm_ref, ones_ref, y_ref):
    h = hf_ref[...] + hb_ref[...]
    ms = jnp.dot((h * h).astype(BF16), ones_ref[...], preferred_element_type=F32) * (1.0 / HEAD_DIM)
    hn = h * lax.rsqrt(ms + EPS) * nrm_ref[...]
    y_ref[...] = (jax.nn.sigmoid(o_ref[...].astype(F32)) * hn).astype(y_ref.dtype)


def _head_ones(width):
    idx = np.arange(width) // HEAD_DIM
    return jnp.asarray(idx[:, None] == idx[None, :], BF16)


def _mlstm_out(hf, hb, o, norm, tm):
    r, w = hf.shape
    blk = pl.BlockSpec((tm, w), lambda i: (i, 0))
    return pl.pallas_call(
        _mlstm_out_kernel,
        out_shape=jax.ShapeDtypeStruct((r, w), BF16),
        grid=(r // tm,),
        in_specs=[blk, blk, blk,
                  pl.BlockSpec((1, w), lambda i: (0, 0)),
                  pl.BlockSpec((w, w), lambda i: (0, 0))],
        out_specs=blk,
        compiler_params=_params("parallel"),
        name="mlstm_out",
    )(hf, hb, o, norm.reshape(1, w), _head_ones(w))


def _swa_kernel(q_ref, k0, k1, k2, k3, v0, v1, v2, v3, kc_ref, vc_ref, sink_ref, o_ref, *, seq):
    w, dh = SWA_BLOCK, HEAD_DIM
    tq = q_ref.shape[1]
    g = q_ref.shape[0] // dh
    i = pl.program_id(2)
    k_loc = jnp.concatenate([k0[0, 0], k1[0, 0], k2[0, 0], k3[0, 0]], axis=0)
    v_loc = jnp.concatenate([v0[0, 0], v1[0, 0], v2[0, 0], v3[0, 0]], axis=1)
    kpos = (2 * i - 1) * w + lax.broadcasted_iota(I32, (4 * w, tq), 0)
    qpos = i * tq + lax.broadcasted_iota(I32, (4 * w, tq), 1)
    valid = (jnp.abs(qpos - kpos) <= SWA_WINDOW) & (kpos >= 0) & (kpos < seq)
    qs = [q_ref[h * dh:(h + 1) * dh, :] for h in range(g)]
    s_loc = [jnp.where(valid, jnp.dot(k_loc, qs[h], preferred_element_type=F32), NEG_INF) for h in range(g)]
    s_ctx = [jnp.dot(kc_ref[0, 0], qs[h], preferred_element_type=F32) for h in range(g)]
    outs = []
    for h in range(g):
        sink = sink_ref[0, h:h + 1, :]
        m = jnp.maximum(jnp.maximum(jnp.max(s_loc[h], axis=0, keepdims=True),
                                    jnp.max(s_ctx[h], axis=0, keepdims=True)), sink)
        p_loc = jnp.exp2(s_loc[h] - m)
        p_ctx = jnp.exp2(s_ctx[h] - m)
        den = jnp.sum(p_loc, axis=0, keepdims=True) + jnp.sum(p_ctx, axis=0, keepdims=True) + jnp.exp2(sink - m)
        o = (jnp.dot(v_loc, p_loc.astype(BF16), preferred_element_type=F32)
             + jnp.dot(vc_ref[0, 0], p_ctx.astype(BF16), preferred_element_type=F32))
        outs.append(o / den)
    o_ref[...] = jnp.concatenate(outs, axis=0).T.astype(o_ref.dtype)


def _swa(qt, k, vt, k_ctx, vt_ctx, sink2, b, s):
    hkv, dh = k.shape[1], k.shape[3]
    nh = qt.shape[0] // dh
    g = nh // hkv
    w = SWA_BLOCK
    tq = 2 * w
    nb, nq = s // w, s // tq
    lc = k_ctx.shape[2]
    clampi = lambda j: jnp.clip(j, 0, nb - 1)
    kspec = lambda o: pl.BlockSpec((1, 1, w, dh), lambda bi, hi, i: (bi, hi, clampi(2 * i + o), 0))
    vspec = lambda o: pl.BlockSpec((1, 1, dh, w), lambda bi, hi, i: (bi, hi, 0, clampi(2 * i + o)))
    return pl.pallas_call(
        functools.partial(_swa_kernel, seq=s),
        out_shape=jax.ShapeDtypeStruct((b * s, nh * dh), BF16),
        grid=(b, hkv, nq),
        in_specs=[pl.BlockSpec((g * dh, tq), lambda bi, hi, i: (hi, bi * nq + i)),
                  kspec(-1), kspec(0), kspec(1), kspec(2), vspec(-1), vspec(0), vspec(1), vspec(2),
                  pl.BlockSpec((1, 1, lc, dh), lambda bi, hi, i: (bi, hi, 0, 0)),
                  pl.BlockSpec((1, 1, dh, lc), lambda bi, hi, i: (bi, hi, 0, 0)),
                  pl.BlockSpec((1, g, tq), lambda bi, hi, i: (hi, 0, 0))],
        out_specs=pl.BlockSpec((tq, g * dh), lambda bi, hi, i: (bi * nq + i, hi)),
        compiler_params=_params("parallel", "parallel", "arbitrary"),
        name="window_attention",
    )(qt, k, k, k, k, vt, vt, vt, vt, k_ctx, vt_ctx, sink2)


def _ctx_attn_kernel(q_ref, k_ref, v_ref, sink_ref, o_ref, *, use_sink):
    g, lq, dh = q_ref.shape[1:]
    q = q_ref[0].reshape(g * lq, dh)
    s = lax.dot_general(q, k_ref[0, 0], (((1,), (1,)), ((), ())), preferred_element_type=F32) * ATTN_SCALE
    m = jnp.max(s, axis=1, keepdims=True)
    if use_sink:
        m = jnp.maximum(m, sink_ref[0])
    p = jnp.exp(s - m)
    den = jnp.sum(p, axis=1, keepdims=True)
    if use_sink:
        den = den + jnp.exp(sink_ref[0] - m)
    o = jnp.dot(p.astype(BF16), v_ref[0, 0], preferred_element_type=F32) / den
    o_ref[0] = o.reshape(g, lq, dh).astype(o_ref.dtype)


def _ctx_attn(q, k, v, sink_col, use_sink):
    b, nh, lq, dh = q.shape
    hkv = k.shape[1]
    g = nh // hkv
    qblk = pl.BlockSpec((1, g, lq, dh), lambda bi, hi: (bi, hi, 0, 0))
    kblk = pl.BlockSpec((1, 1, k.shape[2], dh), lambda bi, hi: (bi, hi, 0, 0))
    return pl.pallas_call(
        functools.partial(_ctx_attn_kernel, use_sink=use_sink),
        out_shape=jax.ShapeDtypeStruct(q.shape, BF16),
        grid=(b, hkv),
        in_specs=[qblk, kblk, kblk, pl.BlockSpec((1, g * lq, 1), lambda bi, hi: (hi, 0, 0))],
        out_specs=qblk,
        compiler_params=_params("parallel", "parallel"),
        name="context_attention",
    )(q, k, v, sink_col)


def _flash_kernel(q_ref, k_ref, v_ref, o_ref, m_scr, l_scr, acc_scr, sa_scr, sb_scr, *, tk):
    dh = HEAD_DIM
    g = q_ref.shape[0] // dh
    n = k_ref.shape[2] // tk
    m_scr[...] = jnp.full_like(m_scr, NEG_INF)
    l_scr[...] = jnp.zeros_like(l_scr)
    acc_scr[...] = jnp.zeros_like(acc_scr)

    def scores(j, dst):
        kk = k_ref[0, 0, pl.ds(pl.multiple_of(j * tk, tk), tk), :]
        for h in range(g):
            dst[h] = jnp.dot(kk, q_ref[h * dh:(h + 1) * dh, :], preferred_element_type=F32)

    def update(j, src):
        vv = v_ref[0, 0, :, pl.ds(pl.multiple_of(j * tk, tk), tk)]
        ss = [src[h] for h in range(g)]
        m_olds = [m_scr[h] for h in range(g)]
        m_news = [jnp.maximum(m_olds[h], jnp.max(ss[h], axis=0, keepdims=True)) for h in range(g)]
        ps = [jnp.exp2(ss[h] - m_news[h]) for h in range(g)]
        pvs = [jnp.dot(vv, ps[h].astype(BF16), preferred_element_type=F32) for h in range(g)]
        for h in range(g):
            alpha = jnp.exp2(m_olds[h] - m_news[h])
            l_scr[h] = alpha * l_scr[h] + jnp.sum(ps[h], axis=0, keepdims=True)
            acc_scr[h] = alpha * acc_scr[h] + pvs[h]
            m_scr[h] = m_news[h]

    scores(0, sa_scr)

    def body(i, carry):
        scores(2 * i + 1, sb_scr)
        update(2 * i, sa_scr)
        scores(2 * i + 2, sa_scr)
        update(2 * i + 1, sb_scr)
        return carry

    lax.fori_loop(0, (n - 1) // 2, body, 0)
    if n % 2 == 1:
        update(n - 1, sa_scr)
    else:
        scores(n - 1, sb_scr)
        update(n - 2, sa_scr)
        update(n - 1, sb_scr)
    out = (acc_scr[...] / l_scr[...]).reshape(g * dh, -1)
    o_ref[...] = out.T.astype(o_ref.dtype)


def _flash(qt, k, vt, b, s, tq=None, tk=None):
    w = qt.shape[0]
    hkv, t, dh = k.shape[1], k.shape[2], k.shape[3]
    g = w // dh // hkv
    tq = tq or _tile(s, (256, 128))
    tk = tk or _tile(t, (256, 128))
    nq = s // tq
    return pl.pallas_call(
        functools.partial(_flash_kernel, tk=tk),
        out_shape=jax.ShapeDtypeStruct((b * s, w), BF16),
        grid=(b, hkv, nq),
        in_specs=[pl.BlockSpec((g * dh, tq), lambda bi, hi, i: (hi, bi * nq + i)),
                  pl.BlockSpec((1, 1, t, dh), lambda bi, hi, i: (bi, hi, 0, 0)),
                  pl.BlockSpec((1, 1, dh, t), lambda bi, hi, i: (bi, hi, 0, 0))],
        out_specs=pl.BlockSpec((tq, g * dh), lambda bi, hi, i: (bi * nq + i, hi)),
        scratch_shapes=[pltpu.VMEM((g, 1, tq), F32), pltpu.VMEM((g, 1, tq), F32), pltpu.VMEM((g, dh, tq), F32),
                        pltpu.VMEM((g, tk, tq), F32), pltpu.VMEM((g, tk, tq), F32)],
        compiler_params=_params("parallel", "parallel", "arbitrary"),
        name="dense_attention",
    )(qt, k, vt)


def _bmm_kernel(a_ref, b_ref, o_ref):
    o_ref[0] = jnp.dot(a_ref[0], b_ref[0], preferred_element_type=F32).astype(o_ref.dtype)


def _group_matmul(a, bmat, name):
    g, m, k = a.shape
    n = bmat.shape[2]
    tm = _tile(m, (1024, 512, 256, 128, 64, 32, 16, 8))
    return pl.pallas_call(
        _bmm_kernel,
        out_shape=jax.ShapeDtypeStruct((g, m, n), F32),
        grid=(g, m // tm),
        in_specs=[pl.BlockSpec((1, tm, k), lambda gi, i: (gi, i, 0)),
                  pl.BlockSpec((1, k, n), lambda gi, i: (gi, 0, 0))],
        out_specs=pl.BlockSpec((1, tm, n), lambda gi, i: (gi, i, 0)),
        compiler_params=_params("parallel", "parallel"),
        name=name,
    )(a, bmat)


def _s5_scan_kernel(sre_ref, sim_ref, are_ref, aim_ref, zre_ref, zim_ref):
    nd, nsteps = sre_ref.shape[0], sre_ref.shape[1]
    for d in range(nd):
        a_re, a_im = are_ref[d], aim_ref[d]

        def body(i, carry, d=d, a_re=a_re, a_im=a_im):
            z_re, z_im = carry
            zre_ref[d, i] = z_re
            zim_ref[d, i] = z_im
            return (a_re * z_re - a_im * z_im + sre_ref[d, i], a_re * z_im + a_im * z_re + sim_ref[d, i])

        zero = jnp.zeros(sre_ref.shape[2:], F32)
        lax.fori_loop(0, nsteps, body, (zero, zero))


def _s5_scan(s_re, s_im, a_re, a_im):
    full = lambda arr: pl.BlockSpec(arr.shape, lambda i: (0,) * arr.ndim)
    return pl.pallas_call(
        _s5_scan_kernel,
        out_shape=[jax.ShapeDtypeStruct(s_re.shape, F32)] * 2,
        grid=(1,),
        in_specs=[full(s_re), full(s_im), full(a_re), full(a_im)],
        out_specs=[full(s_re), full(s_im)],
        compiler_params=_params("arbitrary"),
        name="s5_chunk_scan",
    )(s_re, s_im, a_re, a_im)


def _s5_glu_kernel(y_ref, u_ref, d_ref, w_ref, b_ref, o_ref):
    y = y_ref[...] + d_ref[...] * u_ref[...]
    y = jax.nn.gelu(y)
    gate = jnp.dot(y.astype(BF16), w_ref[...], preferred_element_type=F32) + b_ref[...]
    o_ref[...] = (y * jax.nn.sigmoid(gate)).astype(o_ref.dtype)


def _s5_glu(y, u, d_skip, w_glu, b_glu, tm):
    r, c = y.shape
    blk = pl.BlockSpec((tm, c), lambda i: (i, 0))
    vec = pl.BlockSpec((1, c), lambda i: (0, 0))
    return pl.pallas_call(
        _s5_glu_kernel,
        out_shape=jax.ShapeDtypeStruct((r, c), BF16),
        grid=(r // tm,),
        in_specs=[blk, blk, vec, pl.BlockSpec((c, c), lambda i: (0, 0)), vec],
        out_specs=blk,
        compiler_params=_params("parallel"),
        name="s5_readout_glu",
    )(y, u, d_skip.reshape(1, c), w_glu.astype(BF16), b_glu.reshape(1, c))


def _s5_tables(lam_re, lam_im, log_dt, b_re, b_im, c_re, c_im):
    ln, p, gc = S5_CHUNK, S5_STATE, S5_GROUP
    lam = lax.complex(lam_re.astype(F32), lam_im.astype(F32))
    dt = jnp.exp(log_dt.astype(F32))[..., None]
    a_bar = jnp.exp(lam * dt)
    b_scale = (a_bar - 1.0) / lam
    b_mat = lax.complex(b_re.astype(F32), b_im.astype(F32))
    c_mat = lax.complex(c_re.astype(F32), c_im.astype(F32))
    tau = jnp.arange(ln + 1, dtype=F32)
    apow = jnp.exp((lam * dt)[:, :, None, :] * tau[None, None, :, None])
    drive = b_scale[..., None] * b_mat[None]
    kern = jnp.real(jnp.einsum('gcp,dgtp,dgpe->dgtce', c_mat, apow[:, :, :ln], drive))
    kc = jnp.concatenate([kern[1, :, :0:-1], (kern[0, :, :1] + kern[1, :, :1]), kern[0, :, 1:]], axis=1)
    kc_e = kc.transpose(0, 3, 1, 2).reshape(S5_GROUPS, gc, (2 * ln - 1) * gc)
    toep = jnp.stack([kc_e[:, :, (ln - 1 - s) * gc:(2 * ln - 1 - s) * gc] for s in range(ln)], axis=1)
    toep = toep.reshape(S5_GROUPS, ln * gc, ln * gc)
    w_f = apow[0, :, ln - 1::-1][:, :ln, :, None] * drive[0][:, None]
    w_b = apow[1, :, :ln, :, None] * drive[1][:, None]
    def m_in(wc):
        wt = wc.transpose(0, 1, 3, 2).reshape(S5_GROUPS, ln * gc, p)
        return jnp.concatenate([jnp.real(wt), jnp.imag(wt)], axis=-1)
    min_all = jnp.concatenate([m_in(w_f), m_in(w_b)], axis=-1)
    o_f = c_mat[:, None] * apow[0, :, 1:ln + 1][:, :, None, :]
    o_b = c_mat[:, None] * apow[1, :, ln:0:-1][:, :, None, :]
    def m_out(oc):
        ot = oc.transpose(0, 3, 1, 2).reshape(S5_GROUPS, p, ln * gc)
        return jnp.concatenate([jnp.real(ot), -jnp.imag(ot)], axis=1)
    rhs = jnp.concatenate([toep, m_out(o_f), m_out(o_b)], axis=1)
    a_l = apow[:, :, ln].reshape(2, 1, S5_GROUPS * p)
    return min_all.astype(BF16), rhs.astype(BF16), jnp.real(a_l), jnp.imag(a_l)


def _s5_mixer(u_x, u_c, tables, d_skip, w_glu, b_glu):
    min_all, rhs, a_re, a_im = tables
    b, s, _ = u_x.shape
    ln, p, gc, ng = S5_CHUNK, S5_STATE, S5_GROUP, S5_GROUPS

    def chunks(u):
        nc = u.shape[1] // ln
        return u.reshape(b, nc, ln, ng, gc).transpose(3, 0, 1, 2, 4).reshape(ng, b * nc, ln * gc).astype(BF16), nc

    ux, ncx = chunks(u_x)
    uc, ncc = chunks(u_c)
    sx = _group_matmul(ux, min_all, "s5_local_state").reshape(ng, b, ncx, 4, p)
    sc = _group_matmul(uc, min_all, "s5_local_state_ctx").reshape(ng, b, ncc, 4, p)

    def scan_order(part):
        f = jnp.concatenate([sc[:, :, :, part], sx[:, :, :, part]], axis=2)
        r = jnp.concatenate([sc[:, :, ::-1, part + 2], sx[:, :, ::-1, part + 2]], axis=2)
        return jnp.stack([f, r]).transpose(0, 3, 2, 1, 4).reshape(2, ncc + ncx, b, ng * p)

    z_re, z_im = _s5_scan(scan_order(0), scan_order(1), a_re, a_im)

    def latent(z, d):
        zl = z[d, ncc:]
        if d == 1:
            zl = zl[::-1]
        return zl.reshape(ncx, b, ng, p).transpose(2, 1, 0, 3).reshape(ng, b * ncx, p)

    lhs = jnp.concatenate([ux, latent(z_re, 0).astype(BF16), latent(z_im, 0).astype(BF16),
                           latent(z_re, 1).astype(BF16), latent(z_im, 1).astype(BF16)], axis=-1)
    y = _group_matmul(lhs, rhs, "s5_outputs")
    y = y.reshape(ng, b, ncx, ln, gc).transpose(1, 2, 3, 0, 4).reshape(b * s, S5_CHANNELS)
    return _s5_glu(y, u_x.reshape(b * s, S5_CHANNELS), d_skip, w_glu, b_glu, _tile(b * s, (1024, 512, 256)))


def _out_proj_kernel(x_ref, a_ref, b_ref, wa_ref, wb_ref, gt_ref, g_ref, sc_ref, sh_ref, wr_ref, br_ref,
                     xo_ref, xn_ref, lg_ref):
    y = (jnp.dot(a_ref[...], wa_ref[...], preferred_element_type=F32)
         + jnp.dot(b_ref[...], wb_ref[...], preferred_element_type=F32))
    x = x_ref[...] + gt_ref[0] * y
    xo_ref[...] = x
    xn = x * lax.rsqrt(jnp.mean(x * x, axis=-1, keepdims=True) + EPS) * g_ref[...]
    xn = xn * (1.0 + sc_ref[0]) + sh_ref[0]
    hi = xn.astype(BF16)
    xn_ref[...] = hi
    lo = (xn - hi.astype(F32)).astype(BF16)
    part = jnp.dot(hi, wr_ref[...], preferred_element_type=F32)
    part = part[:, :LANES] + part[:, LANES:] + jnp.dot(lo, wr_ref[:, :LANES], preferred_element_type=F32)
    lg_ref[...] = part[:, :lg_ref.shape[1]] + br_ref[...]


def _out_proj(x, row0, a, bmix, wa, wb, gate, g_ffn, scale, shift, mod_map, w_router, b_router, tm):
    r = a.shape[0]
    d = x.shape[1]
    t0 = row0 // tm
    ne = w_router.shape[1]
    w_hi = w_router.astype(BF16)
    w_lo = (w_router - w_hi.astype(F32)).astype(BF16)
    w_router = jnp.concatenate([_pad_cols(w_hi), _pad_cols(w_lo)], axis=1)
    row = lambda w: pl.BlockSpec((tm, w), lambda i: (i, 0))
    mod = pl.BlockSpec((1, 1, d), lambda i: (mod_map(i), 0, 0))
    full = lambda arr: pl.BlockSpec(arr.shape, lambda i: (0, 0))
    return pl.pallas_call(
        _out_proj_kernel,
        out_shape=[jax.ShapeDtypeStruct((r, d), F32), jax.ShapeDtypeStruct((r, d), BF16),
                   jax.ShapeDtypeStruct((r, ne), F32)],
        grid=(r // tm,),
        in_specs=[pl.BlockSpec((tm, d), lambda i: (i + t0, 0)), row(a.shape[1]), row(bmix.shape[1]),
                  full(wa), full(wb), mod, pl.BlockSpec((1, d), lambda i: (0, 0)), mod, mod,
                  full(w_router), pl.BlockSpec((1, ne), lambda i: (0, 0))],
        out_specs=[row(d), row(d), row(ne)],
        compiler_params=_params("parallel"),
        name="out_proj",
    )(x, a, bmix, wa, wb, gate, g_ffn.reshape(1, d), scale, shift, w_router, b_router.reshape(1, ne))


def _route_kernel(lg_ref, idx_ref, gate_ref, pos_ref, cnt_ref, carry):
    tm, ne = lg_ref.shape

    @pl.when(pl.program_id(0) == 0)
    def _():
        carry[...] = jnp.zeros_like(carry)

    work = lg_ref[...]
    lane = lax.broadcasted_iota(I32, (tm, ne), 1).astype(F32)
    out_lane = lax.broadcasted_iota(I32, (tm, LANES), 1)
    vals, hots, idx_out = [], [], jnp.zeros((tm, LANES), I32)
    for kk in range(TOP_K):
        mx = jnp.max(work, axis=1, keepdims=True)
        idx = jnp.min(jnp.where(work == mx, lane, float(ne)), axis=1, keepdims=True)
        hot = lane == idx
        work = jnp.where(hot, -jnp.inf, work)
        vals.append(mx)
        hots.append(hot)
        idx_out = jnp.where(out_lane == kk, idx.astype(I32), idx_out)
    exps = [jnp.exp(vv - vals[0]) for vv in vals]
    tot = exps[0] + exps[1] + exps[2] + exps[3]
    multi = (hots[0] | hots[1] | hots[2] | hots[3]).astype(BF16)
    r_i = lax.broadcasted_iota(I32, (tm, tm), 0)
    c_i = lax.broadcasted_iota(I32, (tm, tm), 1)
    before = (r_i > c_i).astype(BF16)
    prefix = jnp.dot(before, multi, preferred_element_type=F32) + carry[...]
    gate_out = jnp.zeros((tm, LANES), F32)
    pos_out = jnp.zeros((tm, LANES), I32)
    for kk in range(TOP_K):
        gate_out = jnp.where(out_lane == kk, exps[kk] / tot, gate_out)
        pos = jnp.sum(jnp.where(hots[kk], prefix, 0.0), axis=1, keepdims=True).astype(I32)
        pos_out = jnp.where(out_lane == kk, pos, pos_out)
    idx_ref[...] = idx_out
    gate_ref[...] = gate_out
    pos_ref[...] = pos_out
    carry[...] = carry[...] + jnp.sum(multi.astype(F32), axis=0, keepdims=True)
    cnt_ref[...] = carry[...]


def _route(logits, tm, row0, n):
    ne = logits.shape[1]
    t0 = row0 // tm
    wide = pl.BlockSpec((tm, LANES), lambda i: (i, 0))
    return pl.pallas_call(
        _route_kernel,
        out_shape=[jax.ShapeDtypeStruct((n, LANES), I32), jax.ShapeDtypeStruct((n, LANES), F32),
                   jax.ShapeDtypeStruct((n, LANES), I32), jax.ShapeDtypeStruct((1, ne), F32)],
        grid=(n // tm,),
        in_specs=[pl.BlockSpec((tm, ne), lambda i: (i + t0, 0))],
        out_specs=[wide, wide, wide, pl.BlockSpec((1, ne), lambda i: (0, 0))],
        scratch_shapes=[pltpu.VMEM((1, ne), F32)],
        compiler_params=_params("arbitrary"),
        name="moe_route",
    )(logits)


def _gate_up_prep_kernel(w_ref, p_ref, o_ref):
    for t in range(w_ref.shape[2] // p_ref.shape[0]):
        sl = slice(t * p_ref.shape[0], (t + 1) * p_ref.shape[0])
        o_ref[0, :, sl] = jnp.dot(w_ref[0, :, sl].astype(BF16), p_ref[...], preferred_element_type=F32).astype(BF16)


def _gate_up_prep(w_gate_up):
    ne, d, ff2 = w_gate_up.shape
    tile = 2 * LANES
    j = np.arange(tile)
    src = np.where(j < LANES, 2 * j, 2 * (j - LANES) + 1)
    perm = jnp.asarray(np.arange(tile)[:, None] == src[None, :], BF16)
    tr = _tile(d, (512, 256, 128))
    return pl.pallas_call(
        _gate_up_prep_kernel,
        out_shape=jax.ShapeDtypeStruct((ne, d, ff2), BF16),
        grid=(ne, d // tr),
        in_specs=[pl.BlockSpec((1, tr, ff2), lambda e, i: (e, i, 0)),
                  pl.BlockSpec((tile, tile), lambda e, i: (0, 0))],
        out_specs=pl.BlockSpec((1, tr, ff2), lambda e, i: (e, i, 0)),
        compiler_params=_params("parallel", "parallel"),
        name="moe_gate_up_prep",
    )(w_gate_up, perm)


def _expert_kernel(be_ref, nb_ref, x_ref, wgu_ref, bgu_ref, wd_ref, bd_ref, o_ref):
    i = pl.program_id(0)

    @pl.when(i < nb_ref[0])
    def _():
        h = jnp.dot(x_ref[...], wgu_ref[0], preferred_element_type=F32) + bgu_ref[0]
        acts = []
        for t in range(h.shape[1] // (2 * LANES)):
            glu = jnp.minimum(h[:, 2 * LANES * t:2 * LANES * t + LANES], SWIGLU_LIMIT)
            lin = jnp.clip(h[:, 2 * LANES * t + LANES:2 * LANES * (t + 1)], -SWIGLU_LIMIT, SWIGLU_LIMIT)
            acts.append((glu * jax.nn.sigmoid(SWIGLU_ALPHA * glu) * (lin + 1.0)).astype(BF16))
        act = jnp.concatenate(acts, axis=1)
        o_ref[...] = (jnp.dot(act, wd_ref[0], preferred_element_type=F32) + bd_ref[0]).astype(o_ref.dtype)

    @pl.when(i >= nb_ref[0])
    def _():
        o_ref[...] = jnp.zeros_like(o_ref)


def _experts(xs, blk_e, n_used, wgu, bgu, wd, bd):
    n, d = xs.shape
    ff2 = wgu.shape[2]
    nblk = n // MOE_ROWS
    wspec = lambda shape: pl.BlockSpec((1,) + shape, lambda i, be, nb: (be[i], 0, 0))
    return pl.pallas_call(
        _expert_kernel,
        out_shape=jax.ShapeDtypeStruct((n, d), BF16),
        grid_spec=pltpu.PrefetchScalarGridSpec(
            num_scalar_prefetch=2,
            grid=(nblk,),
            in_specs=[pl.BlockSpec((MOE_ROWS, d), lambda i, be, nb: (i, 0)),
                      wspec((d, ff2)), wspec((1, ff2)), wspec((ff2 // 2, d)), wspec((1, d))],
            out_specs=pl.BlockSpec((MOE_ROWS, d), lambda i, be, nb: (i, 0))),
        compiler_params=_params("arbitrary"),
        name="moe_experts",
    )(blk_e, n_used, xs, wgu, bgu, wd, bd)


def _combine_kernel(x_ref, y_ref, gate_ref, gt_ref, g_ref, o_ref, *, final_norm):
    gates = gate_ref[...]
    y = gates[:, 0:1] * y_ref[0].astype(F32)
    for kk in range(1, TOP_K):
        y = y + gates[:, kk:kk + 1] * y_ref[kk].astype(F32)
    x = x_ref[...] + gt_ref[0] * y
    if final_norm:
        x = x * lax.rsqrt(jnp.mean(x * x, axis=-1, keepdims=True) + EPS) * g_ref[...]
    o_ref[...] = x


def _combine_alias_kernel(x_ref, y_ref, gate_ref, gt_ref, g_ref, prev_ref, o_ref, *, final_norm):
    del prev_ref
    _combine_kernel(x_ref, y_ref, gate_ref, gt_ref, g_ref, o_ref, final_norm=final_norm)


def _combine(x, y4, gates, gate_mod, mod_map, g_final, final_norm, tm, row0, prev):
    r, d = x.shape
    n = y4.shape[1]
    t0 = row0 // tm
    in_specs = [pl.BlockSpec((tm, d), lambda i: (i + t0, 0)),
                pl.BlockSpec((TOP_K, tm, d), lambda i: (0, i, 0)),
                pl.BlockSpec((tm, LANES), lambda i: (i, 0)),
                pl.BlockSpec((1, 1, d), lambda i: (mod_map(i + t0), 0, 0)),
                pl.BlockSpec((1, d), lambda i: (0, 0))]
    args = [x, y4, gates, gate_mod, g_final.reshape(1, d)]
    body, aliases = _combine_kernel, {}
    if prev is not None:
        in_specs.append(pl.BlockSpec(memory_space=pl.ANY))
        args.append(prev)
        body, aliases = _combine_alias_kernel, {len(args) - 1: 0}
    return pl.pallas_call(
        functools.partial(body, final_norm=final_norm),
        out_shape=jax.ShapeDtypeStruct((r, d), F32),
        grid=(n // tm,),
        in_specs=in_specs,
        out_specs=pl.BlockSpec((tm, d), lambda i: (i + t0, 0)),
        input_output_aliases=aliases,
        compiler_params=_params("parallel"),
        name="moe_combine",
    )(*args)


def _moe(x, xn, logits, gate_mod, mod_map, weights, g_final, final_norm, tm):
    wgu, bgu, wd, bd = weights
    r, d = xn.shape
    parts = MOE_PARTS if (r // tm) % MOE_PARTS == 0 else 1
    n = r // parts
    nblk = n * TOP_K // MOE_ROWS + N_EXPERTS
    blk_row0 = jnp.arange(nblk, dtype=I32) * MOE_ROWS
    routed = []
    for p in range(parts):
        idx_w, gates_w, pos_w, counts = _route(logits, tm, p * n, n)
        idx, pos = idx_w[:, :TOP_K], pos_w[:, :TOP_K]
        counts = counts[0].astype(I32)
        padded = (counts + MOE_ROWS - 1) // MOE_ROWS * MOE_ROWS
        pad_end = jnp.cumsum(padded)
        dest = (pad_end - padded)[idx] + pos
        src = jnp.zeros((nblk * MOE_ROWS,), I32).at[dest.reshape(-1)].set(
            jnp.arange(n * TOP_K, dtype=I32) // TOP_K, unique_indices=True, mode="promise_in_bounds")
        blk_e = jnp.minimum(jnp.sum((pad_end[None, :] <= blk_row0[:, None]).astype(I32), axis=1), N_EXPERTS - 1)
        n_used = (pad_end[-1:] // MOE_ROWS).astype(I32)
        xs = xn.at[src + p * n].get(mode="promise_in_bounds")
        routed.append((xs, blk_e, n_used, dest, gates_w))
    ys = [_experts(xs, blk_e, n_used, wgu, bgu, wd, bd) for xs, blk_e, n_used, _, _ in routed]
    out = None
    for p, (_, _, _, dest, gates_w) in enumerate(routed):
        y4 = ys[p].at[dest.T.reshape(-1)].get(mode="promise_in_bounds").reshape(TOP_K, n, d)
        out = _combine(x, y4, gates_w, gate_mod, mod_map, g_final, final_norm, tm, p * n, out)
    return out


def _moe_weights(w_gate_up, b_gate_up, w_down, b_down):
    ne, d, ff2 = w_gate_up.shape
    b_tiled = b_gate_up.reshape(ne, ff2 // (2 * LANES), LANES, 2).swapaxes(-1, -2).reshape(ne, 1, ff2)
    return _gate_up_prep(w_gate_up), b_tiled, w_down.astype(BF16), b_down[:, None, :]


def _rope_tables(n_tokens):
    rows = n_tokens // GRID_W
    row = jnp.repeat(jnp.arange(rows, dtype=I32), GRID_W).astype(F32)
    col = jnp.tile(jnp.arange(GRID_W, dtype=I32), rows).astype(F32)
    inv = ROPE_BASE ** (-jnp.arange(0, ROPE_AXIS_DIM, 2, dtype=F32) / ROPE_AXIS_DIM)
    ang_r, ang_c = row[:, None] * inv, col[:, None] * inv
    cos = jnp.concatenate([jnp.cos(ang_r)] * 2 + [jnp.cos(ang_c)] * 2, axis=-1)
    sin = jnp.concatenate([jnp.sin(ang_r)] * 2 + [jnp.sin(ang_c)] * 2, axis=-1)
    return cos, sin


def _rot_perm():
    q = ROPE_AXIS_DIM // 2
    d = np.arange(HEAD_DIM)
    first = (d % ROPE_AXIS_DIM) < q
    perm = np.where(first, d + q, d - q)
    sign = np.where(first, -1.0, 1.0).astype(np.float32)
    return perm, sign


def _rot_cols(w, n_heads):
    perm, sign = _rot_perm()
    k = w.shape[0]
    wh = w.reshape(k, n_heads, HEAD_DIM)
    return (wh[:, :, perm] * sign).reshape(k, n_heads * HEAD_DIM)


def _to_heads(t, b, n_heads):
    return t.reshape(b, -1, n_heads, HEAD_DIM).transpose(0, 2, 1, 3)


def _from_heads(t):
    b, h, tt, dh = t.shape
    return t.transpose(0, 2, 1, 3).reshape(b * tt, h * dh)


def _layer_even(xa, mod, b, s, lc, g_mix, g_ffn, w_in, w_out, gate_bias, mlstm_norm, sink, router, moe_w):
    d = xa.shape[1]
    nctx = b * lc
    tm = _tile(math.gcd(nctx, s), (512, 256, 128))
    mod_map = lambda i: jnp.where(i * tm < nctx, b, (i * tm - nctx) // s)
    sh1, sc1, gt1, sh2, sc2, gt2 = [mod[:, j][:, None, :] for j in range(6)]

    hm, hs, hk = MLSTM_HEADS * HEAD_DIM, SWA_HEADS * HEAD_DIM, SWA_KV_HEADS * HEAD_DIM
    o0 = np.cumsum([0, hm, hm, hm, hm, 4 * MLSTM_HEADS, hs, hk, hk])
    seg = lambda j: w_in[:, o0[j]:o0[j + 1]]
    w_cat = jnp.concatenate([seg(0), seg(1) * ATTN_SCALE, seg(2), seg(3),
                             seg(6), _rot_cols(seg(6), SWA_KV_HEADS), seg(7), _pad_cols(seg(4))], axis=1).astype(BF16)
    w_q = jnp.concatenate([seg(5), _rot_cols(seg(5), SWA_HEADS)], axis=1).T.astype(BF16)
    widths = [hm, hm, hm, hm, hk, hk, hk, LANES]
    dts = [BF16] * 7 + [F32]
    qa, ka, va, oa, kb, kb_r, vb, gts, qt, qt_r = _in_proj(xa, g_mix, sc1, sh1, mod_map, w_cat, widths, dts, tm,
                                                           wt=w_q, t_widths=[hs, hs], t_dtypes=[BF16, BF16])

    hf, hb = _mlstm(qa, ka, va, gts, _pad_cols(gate_bias.astype(F32).reshape(1, -1)), b, lc, s)
    mix_a = _mlstm_out(hf, hb, oa, mlstm_norm.reshape(-1), tm)

    cos, sin = _rope_tables(s)
    ones = jnp.ones((HEAD_DIM,), F32)
    g = SWA_HEADS // SWA_KV_HEADS
    heads = lambda t, n: _to_heads(t, b, n)
    log2e = math.log2(math.e)
    q_x = _q_prep_t(qt, qt_r, nctx, b * s, cos.T, sin.T, ones, ones, ATTN_SCALE * log2e, norm=False)
    k_x = _qk_prep(heads(kb[nctx:], SWA_KV_HEADS), heads(kb_r[nctx:], SWA_KV_HEADS), cos, sin, ones, ones,
                   norm=False, rope=True, scale=1.0)
    k_c, v_c, v_x = heads(kb[:nctx], SWA_KV_HEADS), heads(vb[:nctx], SWA_KV_HEADS), heads(vb[nctx:], SWA_KV_HEADS)
    q_c = heads(qt[:, :nctx].T, SWA_HEADS)
    sink_h = sink.astype(F32).reshape(SWA_KV_HEADS, g, 1)
    att_x = _swa(q_x, k_x, v_x.swapaxes(-1, -2), k_c, v_c.swapaxes(-1, -2),
                 jnp.broadcast_to(sink_h * log2e, (SWA_KV_HEADS, g, 2 * SWA_BLOCK)), b, s)
    att_c = _ctx_attn(q_c, k_c, v_c, jnp.broadcast_to(sink_h[..., None], (SWA_KV_HEADS, g, lc, 1))
                      .reshape(SWA_KV_HEADS, g * lc, 1), True)
    mix_b = jnp.concatenate([_from_heads(att_c), att_x], axis=0)

    w_router, b_router = router
    xa, xn2, logits = _out_proj(xa, 0, mix_a, mix_b, w_out[:hm].astype(BF16), w_out[hm:].astype(BF16), gt1, g_ffn,
                                sc2, sh2, mod_map, w_router.astype(F32), b_router.astype(F32), tm)
    return _moe(xa, xn2, logits, gt2, mod_map, moe_w, g_ffn, False, tm)


def _layer_odd_last(xa, mod, b, s, lc, g_mix, g_ffn, w_in, w_out, s5_params, d_skip, w_glu, b_glu,
                    q_norm, k_norm, router, moe_w, g_final):
    nctx = b * lc
    tm = _tile(math.gcd(nctx, s), (512, 256, 128))
    mod_map = lambda i: jnp.where(i * tm < nctx, b, (i * tm - nctx) // s)
    lat_map = lambda i: i * tm // s
    sh1, sc1, gt1, sh2, sc2, gt2 = [mod[:, j][:, None, :] for j in range(6)]

    hq, hk = ATT_HEADS * HEAD_DIM, ATT_KV_HEADS * HEAD_DIM
    o1 = np.cumsum([0, S5_CHANNELS, hq, hk, hk])
    seg = lambda j: w_in[:, o1[j]:o1[j + 1]]
    kpad = lambda w: _pad_cols(w, 2 * LANES)
    w_cat = jnp.concatenate([seg(0), kpad(seg(2)), kpad(_rot_cols(seg(2), ATT_KV_HEADS)), kpad(seg(3))],
                            axis=1).astype(BF16)
    w_q = jnp.concatenate([seg(1), _rot_cols(seg(1), ATT_HEADS)], axis=1).T.astype(BF16)
    widths = [S5_CHANNELS, 2 * LANES, 2 * LANES, 2 * LANES]
    dts = [F32, BF16, BF16, BF16]
    u, k, k_r, v, qt, qt_r = _in_proj(xa, g_mix, sc1, sh1, mod_map, w_cat, widths, dts, tm,
                                      wt=w_q, t_widths=[hq, hq], t_dtypes=[BF16, BF16])

    mix_a = _s5_mixer(u[nctx:].reshape(b, s, S5_CHANNELS), u[:nctx].reshape(b, lc, S5_CHANNELS),
                      _s5_tables(*s5_params), d_skip, w_glu, b_glu)

    cos, sin = _rope_tables(s)
    perm, _ = _rot_perm()
    qn, kn = q_norm.astype(F32), k_norm.astype(F32)
    heads = lambda t, n: _to_heads(t[:, :n * HEAD_DIM], b, n)
    q_x = _q_prep_t(qt, qt_r, nctx, b * s, cos.T, sin.T, qn, qn[perm], ATTN_SCALE * math.log2(math.e))
    k_x = _qk_prep(heads(k[nctx:], ATT_KV_HEADS), heads(k_r[nctx:], ATT_KV_HEADS), cos, sin, kn, kn[perm],
                   norm=True, rope=True, scale=1.0)
    kc_raw = heads(k[:nctx], ATT_KV_HEADS)
    k_c = _qk_prep(kc_raw, kc_raw, cos[:lc], sin[:lc], kn, kn, norm=True, rope=False, scale=1.0)
    k_all = jnp.concatenate([k_c, k_x], axis=2)
    v_all = jnp.concatenate([heads(v[:nctx], ATT_KV_HEADS), heads(v[nctx:], ATT_KV_HEADS)], axis=2)
    mix_b = _flash(q_x, k_all, v_all.swapaxes(-1, -2), b, s)

    w_router, b_router = router
    hs = S5_CHANNELS
    x, xn2, logits = _out_proj(xa, nctx, mix_a, mix_b, w_out[:hs].astype(BF16), w_out[hs:].astype(BF16), gt1, g_ffn,
                               sc2, sh2, lat_map, w_router.astype(F32), b_router.astype(F32), tm)
    return _moe(x, xn2, logits, gt2, lat_map, moe_w, g_final, True, tm)


def kernel(x, c, ctx, c_ctx, l0_w_mod, l0_b_mod, l0_g_mix, l0_g_ffn, l0_w_in, l0_w_out, l0_gate_bias, l0_mlstm_norm, l0_sink, l0_w_router, l0_b_router, l0_w_gate_up, l0_b_gate_up, l0_w_down, l0_b_down, l1_w_mod, l1_b_mod, l1_g_mix, l1_g_ffn, l1_w_in, l1_w_out, l1_lam_re, l1_lam_im, l1_log_dt, l1_b_re, l1_b_im, l1_c_re, l1_c_im, l1_d_skip, l1_w_glu, l1_b_glu, l1_q_norm, l1_k_norm, l1_w_router, l1_b_router, l1_w_gate_up, l1_b_gate_up, l1_w_down, l1_b_down, g_final):
    b, s, d = x.shape
    lc = ctx.shape[1]
    cond = jnp.concatenate([c, c_ctx[None, :]], axis=0)
    cond = jnp.pad(cond, ((0, (-(b + 1)) % 8), (0, 0)))
    mod0 = _silu_linear(cond, l0_w_mod, l0_b_mod)[:b + 1].reshape(b + 1, 6, d)
    mod1 = _silu_linear(cond, l1_w_mod, l1_b_mod)[:b + 1].reshape(b + 1, 6, d)

    xa = jnp.concatenate([ctx.reshape(b * lc, d), x.reshape(b * s, d)], axis=0)
    xa = _layer_even(xa, mod0, b, s, lc, l0_g_mix, l0_g_ffn, l0_w_in, l0_w_out, l0_gate_bias, l0_mlstm_norm,
                     l0_sink, (l0_w_router, l0_b_router),
                     _moe_weights(l0_w_gate_up, l0_b_gate_up, l0_w_down, l0_b_down))
    out = _layer_odd_last(xa, mod1, b, s, lc, l1_g_mix, l1_g_ffn, l1_w_in, l1_w_out,
                          (l1_lam_re, l1_lam_im, l1_log_dt, l1_b_re, l1_b_im, l1_c_re, l1_c_im),
                          l1_d_skip, l1_w_glu, l1_b_glu, l1_q_norm, l1_k_norm, (l1_w_router, l1_b_router),
                          _moe_weights(l1_w_gate_up, l1_b_gate_up, l1_w_down, l1_b_down), g_final)
    return out.reshape(b, s, d)
```

```python
import functools
import math

import jax
import jax.numpy as jnp
import numpy as np
from jax import lax
from jax.experimental import pallas as pl
from jax.experimental.pallas import tpu as pltpu

F32 = jnp.float32
BF16 = jnp.bfloat16
I32 = jnp.int32

GRID_W = 64
HEAD_DIM = 64
ATTN_SCALE = HEAD_DIM ** -0.5
ROPE_AXIS_DIM = HEAD_DIM // 2
ROPE_BASE = 10000.0
EPS = 1e-6
NEG_INF = -1e30

MLSTM_HEADS = 8
MLSTM_CHUNKS = (256, 128, 64)
SWA_HEADS = 8
SWA_KV_HEADS = 2
SWA_WINDOW = 128
SWA_BLOCK = 128
S5_CHANNELS = 256
S5_GROUP = 16
S5_GROUPS = S5_CHANNELS // S5_GROUP
S5_STATE = 64
S5_CHUNK = 64
ATT_HEADS = 12
ATT_KV_HEADS = 3
N_EXPERTS = 32
TOP_K = 4
SWIGLU_LIMIT = 7.0
SWIGLU_ALPHA = 1.702

LANES = 128
VMEM_LIMIT = 56 * 1024 * 1024
MOE_ROWS = 512
MOE_PARTS = 2
HIGHEST = lax.Precision.HIGHEST


def _params(*sem):
    return pltpu.CompilerParams(dimension_semantics=sem, vmem_limit_bytes=VMEM_LIMIT)


def _tile(n, prefs):
    for t in prefs:
        if n % t == 0:
            return t
    return n


def _pad_cols(w, mult=LANES):
    pad = (-w.shape[-1]) % mult
    if pad:
        w = jnp.pad(w, [(0, 0)] * (w.ndim - 1) + [(0, pad)])
    return w


def _linear_kernel(x_ref, w_ref, b_ref, o_ref):
    x = x_ref[...]
    x = x * jax.nn.sigmoid(x)
    o_ref[...] = jnp.dot(x, w_ref[...], precision=HIGHEST, preferred_element_type=F32) + b_ref[...]


def _silu_linear(x, w, b):
    m, k = x.shape
    n = w.shape[1]
    tn = _tile(n, (1024, 512, 256, 128))
    return pl.pallas_call(
        _linear_kernel,
        out_shape=jax.ShapeDtypeStruct((m, n), F32),
        grid=(n // tn,),
        in_specs=[pl.BlockSpec((m, k), lambda j: (0, 0)),
                  pl.BlockSpec((k, tn), lambda j: (0, j)),
                  pl.BlockSpec((1, tn), lambda j: (0, j))],
        out_specs=pl.BlockSpec((m, tn), lambda j: (0, j)),
        compiler_params=_params("arbitrary"),
        name="adaln_linear",
    )(x, w, b.reshape(1, n))


def _in_proj_kernel(x_ref, g_ref, sc_ref, sh_ref, w_ref, wt_ref, *out_refs, widths, t_widths):
    x = x_ref[...]
    xn = x * lax.rsqrt(jnp.mean(x * x, axis=-1, keepdims=True) + EPS) * g_ref[...]
    xb = (xn * (1.0 + sc_ref[0]) + sh_ref[0]).astype(BF16)
    off = 0
    for o_ref, w in zip(out_refs, widths):
        o_ref[...] = jnp.dot(xb, w_ref[:, off:off + w], preferred_element_type=F32).astype(o_ref.dtype)
        off += w
    off = 0
    for o_ref, w in zip(out_refs[len(widths):], t_widths):
        o_ref[...] = lax.dot_general(wt_ref[off:off + w, :], xb, (((1,), (1,)), ((), ())),
                                     preferred_element_type=F32).astype(o_ref.dtype)
        off += w


def _in_proj(x, g, scale, shift, mod_map, w, widths, dtypes, tm, wt=None, t_widths=(), t_dtypes=()):
    r, d = x.shape
    if wt is None:
        wt = jnp.zeros((8, d), BF16)
    return pl.pallas_call(
        functools.partial(_in_proj_kernel, widths=tuple(widths), t_widths=tuple(t_widths)),
        out_shape=([jax.ShapeDtypeStruct((r, wd), dt) for wd, dt in zip(widths, dtypes)]
                   + [jax.ShapeDtypeStruct((wd, r), dt) for wd, dt in zip(t_widths, t_dtypes)]),
        grid=(r // tm,),
        in_specs=[pl.BlockSpec((tm, d), lambda i: (i, 0)),
                  pl.BlockSpec((1, d), lambda i: (0, 0)),
                  pl.BlockSpec((1, 1, d), lambda i: (mod_map(i), 0, 0)),
                  pl.BlockSpec((1, 1, d), lambda i: (mod_map(i), 0, 0)),
                  pl.BlockSpec(w.shape, lambda i: (0, 0)),
                  pl.BlockSpec(wt.shape, lambda i: (0, 0))],
        out_specs=([pl.BlockSpec((tm, wd), lambda i: (i, 0)) for wd in widths]
                   + [pl.BlockSpec((wd, tm), lambda i: (0, i)) for wd in t_widths]),
        compiler_params=_params("parallel"),
        name="in_proj",
    )(x, g.reshape(1, d), scale, shift, w, wt)


def _qk_prep_kernel(x_ref, xr_ref, cos_ref, sin_ref, g_ref, gr_ref, o_ref, *, norm, rope, scale):
    x = x_ref[0].astype(F32)
    if norm:
        s = lax.rsqrt(jnp.mean(x * x, axis=-1, keepdims=True) + EPS)
        x = x * s * g_ref[...]
    if rope:
        xr = xr_ref[0].astype(F32)
        if norm:
            xr = xr * s * gr_ref[...]
        x = x * cos_ref[...] + xr * sin_ref[...]
    o_ref[0] = (x * scale).astype(o_ref.dtype)


def _qk_prep(x, x_rot, cos, sin, gain, gain_rot, *, norm, rope, scale):
    b, h, t, dh = x.shape
    tt = _tile(t, (512, 256, 128))
    blk = pl.BlockSpec((1, h, tt, dh), lambda i, j: (i, 0, j, 0))
    tab = pl.BlockSpec((tt, dh), lambda i, j: (j, 0))
    vec = pl.BlockSpec((1, dh), lambda i, j: (0, 0))
    return pl.pallas_call(
        functools.partial(_qk_prep_kernel, norm=norm, rope=rope, scale=scale),
        out_shape=jax.ShapeDtypeStruct(x.shape, BF16),
        grid=(b, t // tt),
        in_specs=[blk, blk, tab, tab, vec, vec],
        out_specs=blk,
        compiler_params=_params("parallel", "parallel"),
        name="qk_prep",
    )(x, x_rot, cos, sin, gain.reshape(1, dh), gain_rot.reshape(1, dh))


def _q_prep_t_kernel(x_ref, xr_ref, cos_ref, sin_ref, g_ref, gr_ref, o_ref, *, norm, scale):
    w, tt = x_ref.shape
    nh = w // HEAD_DIM
    x = x_ref[...].astype(F32).reshape(nh, HEAD_DIM, tt)
    xr = xr_ref[...].astype(F32).reshape(nh, HEAD_DIM, tt)
    if norm:
        s = lax.rsqrt(jnp.mean(x * x, axis=1, keepdims=True) + EPS)
        x, xr = x * s * g_ref[...], xr * s * gr_ref[...]
    y = x * cos_ref[...] + xr * sin_ref[...]
    o_ref[...] = (y * scale).reshape(w, tt).astype(o_ref.dtype)


def _q_prep_t(xt, xt_rot, col0, n_cols, cos_t, sin_t, gain, gain_rot, scale, norm=True):
    w = xt.shape[0]
    s = cos_t.shape[1]
    tt = _tile(math.gcd(col0, s), (512, 256, 128))
    blk = pl.BlockSpec((w, tt), lambda j: (0, j + col0 // tt))
    tab = pl.BlockSpec((HEAD_DIM, tt), lambda j: (0, j % (s // tt)))
    vec = pl.BlockSpec((HEAD_DIM, 1), lambda j: (0, 0))
    return pl.pallas_call(
        functools.partial(_q_prep_t_kernel, norm=norm, scale=scale),
        out_shape=jax.ShapeDtypeStruct((w, n_cols), BF16),
        grid=(n_cols // tt,),
        in_specs=[blk, blk, tab, tab, vec, vec],
        out_specs=pl.BlockSpec((w, tt), lambda j: (0, j)),
        compiler_params=_params("parallel"),
        name="q_prep_t",
    )(xt, xt_rot, cos_t, sin_t, gain.reshape(HEAD_DIM, 1), gain_rot.reshape(HEAD_DIM, 1))


def _log_sigmoid(x):
    return jnp.minimum(x, 0.0) - jnp.log(1.0 + jnp.exp(-jnp.abs(x)))


def _mlstm_kernel(qf, kf, vf, gf, qb, kb, vb, gb, bias_ref, hf_ref, hb_ref, c_scr, n_scr, m_scr):
    ch, nh, dh = qf.shape[0], MLSTM_HEADS, HEAD_DIM

    @pl.when(pl.program_id(1) == 0)
    def _():
        c_scr[...] = jnp.zeros_like(c_scr)
        n_scr[...] = jnp.zeros_like(n_scr)
        m_scr[...] = jnp.full_like(m_scr, NEG_INF)

    row = lax.broadcasted_iota(I32, (ch, ch), 0)
    col = lax.broadcasted_iota(I32, (ch, ch), 1)
    lane_lo = lax.broadcasted_iota(I32, (ch, 2 * dh), 1) < dh
    row_lo = lax.broadcasted_iota(I32, (2 * dh, 2 * dh), 0) < dh
    col_lo = lax.broadcasted_iota(I32, (2 * dh, 2 * dh), 1) < dh
    vec_lo = lax.broadcasted_iota(I32, (1, 2 * dh), 1) < dh
    pair = lambda a, b: jnp.where(lane_lo, a, b)
    refs = ((qf, kf, vf, gf, hf_ref), (qb, kb, vb, gb, hb_ref))
    jobs = [(d, p) for d in range(2) for p in range(nh // 2)]
    sl = lambda p: slice(2 * dh * p, 2 * dh * (p + 1))

    qk = {}
    for d, p in jobs:
        q2, k2 = refs[d][0][:, sl(p)], refs[d][1][:, sl(p)]
        zero = jnp.zeros_like(q2)
        q_st = jnp.concatenate([jnp.where(lane_lo, q2, zero), jnp.where(lane_lo, zero, q2)], axis=0)
        qk[d, p] = lax.dot_general(q_st, k2, (((1,), (1,)), ((), ())), preferred_element_type=F32)

    st = {}
    for d in range(2):
        seen = (col <= row) if d == 0 else (col >= row)
        tri = seen.astype(F32)
        tri_t = ((row <= col) if d == 0 else (row >= col)).astype(F32)
        last = ch - 1 if d == 0 else 0
        g = refs[d][3][...] + bias_ref[...]
        g_t = g.T
        lo = 2 * nh * d
        li_col, lf_col = g[:, lo:lo + nh], _log_sigmoid(g[:, lo + nh:lo + 2 * nh])
        li_row, lf_row = g_t[lo:lo + nh, :], _log_sigmoid(g_t[lo + nh:lo + 2 * nh, :])
        b_col = jnp.dot(tri, lf_col, precision=HIGHEST, preferred_element_type=F32)
        b_row = jnp.dot(lf_row, tri_t, precision=HIGHEST, preferred_element_type=F32)
        for h in range(nh):
            bc, br = b_col[:, h:h + 1], b_row[h:h + 1, :]
            lir, lic = li_row[h:h + 1, :], li_col[:, h:h + 1]
            b_last = br[:, last:last + 1]
            m_prev = m_scr[d, h // 2][:, (h % 2) * dh:(h % 2) * dh + 1]
            d_log = jnp.where(seen, bc - br + lir, NEG_INF)
            inter_log = bc + m_prev
            m_t = jnp.maximum(inter_log, jnp.max(d_log, axis=1, keepdims=True))
            w_log = b_last - bc + lic
            m_new = jnp.maximum(b_last + m_prev, jnp.max(w_log, axis=0, keepdims=True))
            st[d, h] = dict(dmat=jnp.exp(d_log - m_t), inter=jnp.exp(inter_log - m_t), floor=jnp.exp(-m_t),
                            m_new=m_new, decay=jnp.exp(b_last + m_prev - m_new), wn=jnp.exp(w_log - m_new))

    mm = {}
    for d, p in jobs:
        a, b = st[d, 2 * p], st[d, 2 * p + 1]
        q2, k2, v2 = refs[d][0][:, sl(p)], refs[d][1][:, sl(p)], refs[d][2][:, sl(p)]
        s = qk[d, p] * jnp.concatenate([a["dmat"], b["dmat"]], axis=0)
        kw = k2.astype(F32) * pair(a["wn"], b["wn"])
        mm[d, p] = dict(
            s_sum=jnp.sum(s, axis=1, keepdims=True), kw_sum=jnp.sum(kw, axis=0, keepdims=True),
            sv=jnp.dot(s.astype(BF16), v2, preferred_element_type=F32),
            q_c=jnp.dot(q2, c_scr[d, p].astype(BF16), preferred_element_type=F32),
            kv=lax.dot_general(kw.astype(BF16), v2, (((0,), (0,)), ((), ())), preferred_element_type=F32))

    for d, p in jobs:
        a, b, r = st[d, 2 * p], st[d, 2 * p + 1], mm[d, p]
        q2 = refs[d][0][:, sl(p)]
        n_prev = n_scr[d, p]
        num = pair(a["inter"], b["inter"]) * r["q_c"] + jnp.where(lane_lo, r["sv"][:ch], r["sv"][ch:])
        qn = q2.astype(F32) * n_prev
        qn_a = jnp.sum(jnp.where(lane_lo, qn, 0.0), axis=1, keepdims=True)
        qn_b = jnp.sum(jnp.where(lane_lo, 0.0, qn), axis=1, keepdims=True)
        den_a = jnp.maximum(jnp.abs(a["inter"] * qn_a + r["s_sum"][:ch]), a["floor"])
        den_b = jnp.maximum(jnp.abs(b["inter"] * qn_b + r["s_sum"][ch:]), b["floor"])
        refs[d][4][:, sl(p)] = num / pair(den_a, den_b)
        c_scr[d, p] = (jnp.where(row_lo, a["decay"], b["decay"]) * c_scr[d, p]
                       + jnp.where(row_lo == col_lo, r["kv"], 0.0))
        n_scr[d, p] = jnp.where(vec_lo, a["decay"], b["decay"]) * n_prev + r["kw_sum"]
        m_scr[d, p] = jnp.where(vec_lo, a["m_new"], b["m_new"])


def _mlstm(q, k, v, gates, bias, b, lc, s):
    r, w = q.shape
    ch = _tile(math.gcd(lc, s), MLSTM_CHUNKS)
    ncc, ncx = lc // ch, s // ch
    base = b * ncc

    def fwd(i, c):
        return jnp.where(c < ncc, i * ncc + c, base + i * ncx + (c - ncc)), 0

    def bwd(i, c):
        return jnp.where(c < ncc, i * ncc + (ncc - 1 - c), base + i * ncx + (ncx - 1 - (c - ncc))), 0

    spec = lambda width, m: pl.BlockSpec((ch, width), m)
    npair = MLSTM_HEADS // 2
    return pl.pallas_call(
        _mlstm_kernel,
        out_shape=[jax.ShapeDtypeStruct((r, w), F32)] * 2,
        grid=(b, ncc + ncx),
        in_specs=[spec(w, fwd), spec(w, fwd), spec(w, fwd), spec(LANES, fwd),
                  spec(w, bwd), spec(w, bwd), spec(w, bwd), spec(LANES, bwd),
                  pl.BlockSpec((1, LANES), lambda i, c: (0, 0))],
        out_specs=[spec(w, fwd), spec(w, bwd)],
        scratch_shapes=[pltpu.VMEM((2, npair, 2 * HEAD_DIM, 2 * HEAD_DIM), F32),
                        pltpu.VMEM((2, npair, 1, 2 * HEAD_DIM), F32),
                        pltpu.VMEM((2, npair, 1, 2 * HEAD_DIM), F32)],
        compiler_params=_params("parallel", "arbitrary"),
        name="mlstm_scan",
    )(q, k, v, gates, q, k, v, gates, bias)


def _mlstm_out_kernel(hf_ref, hb_ref, o_ref, nrm_ref, ones_ref, y_ref):
    h = hf_ref[...] + hb_ref[...]
    ms = jnp.dot((h * h).astype(BF16), ones_ref[...], preferred_element_type=F32) * (1.0 / HEAD_DIM)
    hn = h * lax.rsqrt(ms + EPS) * nrm_ref[...]
    y_ref[...] = (jax.nn.sigmoid(o_ref[...].astype(F32)) * hn).astype(y_ref.dtype)


def _head_ones(width):
    idx = np.arange(width) // HEAD_DIM
    return jnp.asarray(idx[:, None] == idx[None, :], BF16)


def _mlstm_out(hf, hb, o, norm, tm):
    r, w = hf.shape
    blk = pl.BlockSpec((tm, w), lambda i: (i, 0))
    return pl.pallas_call(
        _mlstm_out_kernel,
        out_shape=jax.ShapeDtypeStruct((r, w), BF16),
        grid=(r // tm,),
        in_specs=[blk, blk, blk,
                  pl.BlockSpec((1, w), lambda i: (0, 0)),
                  pl.BlockSpec((w, w), lambda i: (0, 0))],
        out_specs=blk,
        compiler_params=_params("parallel"),
        name="mlstm_out",
    )(hf, hb, o, norm.reshape(1, w), _head_ones(w))


def _swa_kernel(q_ref, k0, k1, k2, k3, v0, v1, v2, v3, kc_ref, vc_ref, sink_ref, o_ref, *, seq):
    w, dh = SWA_BLOCK, HEAD_DIM
    tq = q_ref.shape[1]
    g = q_ref.shape[0] // dh
    i = pl.program_id(2)
    k_loc = jnp.concatenate([k0[0, 0], k1[0, 0], k2[0, 0], k3[0, 0]], axis=0)
    v_loc = jnp.concatenate([v0[0, 0], v1[0, 0], v2[0, 0], v3[0, 0]], axis=1)
    kpos = (2 * i - 1) * w + lax.broadcasted_iota(I32, (4 * w, tq), 0)
    qpos = i * tq + lax.broadcasted_iota(I32, (4 * w, tq), 1)
    valid = (jnp.abs(qpos - kpos) <= SWA_WINDOW) & (kpos >= 0) & (kpos < seq)
    qs = [q_ref[h * dh:(h + 1) * dh, :] for h in range(g)]
    s_loc = [jnp.where(valid, jnp.dot(k_loc, qs[h], preferred_element_type=F32), NEG_INF) for h in range(g)]
    s_ctx = [jnp.dot(kc_ref[0, 0], qs[h], preferred_element_type=F32) for h in range(g)]
    outs = []
    for h in range(g):
        sink = sink_ref[0, h:h + 1, :]
        m = jnp.maximum(jnp.maximum(jnp.max(s_loc[h], axis=0, keepdims=True),
                                    jnp.max(s_ctx[h], axis=0, keepdims=True)), sink)
        p_loc = jnp.exp2(s_loc[h] - m)
        p_ctx = jnp.exp2(s_ctx[h] - m)
        den = jnp.sum(p_loc, axis=0, keepdims=True) + jnp.sum(p_ctx, axis=0, keepdims=True) + jnp.exp2(sink - m)
        o = (jnp.dot(v_loc, p_loc.astype(BF16), preferred_element_type=F32)
             + jnp.dot(vc_ref[0, 0], p_ctx.astype(BF16), preferred_element_type=F32))
        outs.append(o / den)
    o_ref[...] = jnp.concatenate(outs, axis=0).T.astype(o_ref.dtype)


def _swa(qt, k, vt, k_ctx, vt_ctx, sink2, b, s):
    hkv, dh = k.shape[1], k.shape[3]
    nh = qt.shape[0] // dh
    g = nh // hkv
    w = SWA_BLOCK
    tq = 2 * w
    nb, nq = s // w, s // tq
    lc = k_ctx.shape[2]
    clampi = lambda j: jnp.clip(j, 0, nb - 1)
    kspec = lambda o: pl.BlockSpec((1, 1, w, dh), lambda bi, hi, i: (bi, hi, clampi(2 * i + o), 0))
    vspec = lambda o: pl.BlockSpec((1, 1, dh, w), lambda bi, hi, i: (bi, hi, 0, clampi(2 * i + o)))
    return pl.pallas_call(
        functools.partial(_swa_kernel, seq=s),
        out_shape=jax.ShapeDtypeStruct((b * s, nh * dh), BF16),
        grid=(b, hkv, nq),
        in_specs=[pl.BlockSpec((g * dh, tq), lambda bi, hi, i: (hi, bi * nq + i)),
                  kspec(-1), kspec(0), kspec(1), kspec(2), vspec(-1), vspec(0), vspec(1), vspec(2),
                  pl.BlockSpec((1, 1, lc, dh), lambda bi, hi, i: (bi, hi, 0, 0)),
                  pl.BlockSpec((1, 1, dh, lc), lambda bi, hi, i: (bi, hi, 0, 0)),
                  pl.BlockSpec((1, g, tq), lambda bi, hi, i: (hi, 0, 0))],
        out_specs=pl.BlockSpec((tq, g * dh), lambda bi, hi, i: (bi * nq + i, hi)),
        compiler_params=_params("parallel", "parallel", "arbitrary"),
        name="window_attention",
    )(qt, k, k, k, k, vt, vt, vt, vt, k_ctx, vt_ctx, sink2)


def _ctx_attn_kernel(q_ref, k_ref, v_ref, sink_ref, o_ref, *, use_sink):
    g, lq, dh = q_ref.shape[1:]
    q = q_ref[0].reshape(g * lq, dh)
    s = lax.dot_general(q, k_ref[0, 0], (((1,), (1,)), ((), ())), preferred_element_type=F32) * ATTN_SCALE
    m = jnp.max(s, axis=1, keepdims=True)
    if use_sink:
        m = jnp.maximum(m, sink_ref[0])
    p = jnp.exp(s - m)
    den = jnp.sum(p, axis=1, keepdims=True)
    if use_sink:
        den = den + jnp.exp(sink_ref[0] - m)
    o = jnp.dot(p.astype(BF16), v_ref[0, 0], preferred_element_type=F32) / den
    o_ref[0] = o.reshape(g, lq, dh).astype(o_ref.dtype)


def _ctx_attn(q, k, v, sink_col, use_sink):
    b, nh, lq, dh = q.shape
    hkv = k.shape[1]
    g = nh // hkv
    qblk = pl.BlockSpec((1, g, lq, dh), lambda bi, hi: (bi, hi, 0, 0))
    kblk = pl.BlockSpec((1, 1, k.shape[2], dh), lambda bi, hi: (bi, hi, 0, 0))
    return pl.pallas_call(
        functools.partial(_ctx_attn_kernel, use_sink=use_sink),
        out_shape=jax.ShapeDtypeStruct(q.shape, BF16),
        grid=(b, hkv),
        in_specs=[qblk, kblk, kblk, pl.BlockSpec((1, g * lq, 1), lambda bi, hi: (hi, 0, 0))],
        out_specs=qblk,
        compiler_params=_params("parallel", "parallel"),
        name="context_attention",
    )(q, k, v, sink_col)


def _flash_kernel(q_ref, k_ref, v_ref, o_ref, m_scr, l_scr, acc_scr, sa_scr, sb_scr, *, tk):
    dh = HEAD_DIM
    g = q_ref.shape[0] // dh
    n = k_ref.shape[2] // tk
    m_scr[...] = jnp.full_like(m_scr, NEG_INF)
    l_scr[...] = jnp.zeros_like(l_scr)
    acc_scr[...] = jnp.zeros_like(acc_scr)

    def scores(j, dst):
        kk = k_ref[0, 0, pl.ds(pl.multiple_of(j * tk, tk), tk), :]
        for h in range(g):
            dst[h] = jnp.dot(kk, q_ref[h * dh:(h + 1) * dh, :], preferred_element_type=F32)

    def update(j, src):
        vv = v_ref[0, 0, :, pl.ds(pl.multiple_of(j * tk, tk), tk)]
        ss = [src[h] for h in range(g)]
        m_olds = [m_scr[h] for h in range(g)]
        m_news = [jnp.maximum(m_olds[h], jnp.max(ss[h], axis=0, keepdims=True)) for h in range(g)]
        ps = [jnp.exp2(ss[h] - m_news[h]) for h in range(g)]
        pvs = [jnp.dot(vv, ps[h].astype(BF16), preferred_element_type=F32) for h in range(g)]
        for h in range(g):
            alpha = jnp.exp2(m_olds[h] - m_news[h])
            l_scr[h] = alpha * l_scr[h] + jnp.sum(ps[h], axis=0, keepdims=True)
            acc_scr[h] = alpha * acc_scr[h] + pvs[h]
            m_scr[h] = m_news[h]

    scores(0, sa_scr)

    def body(i, carry):
        scores(2 * i + 1, sb_scr)
        update(2 * i, sa_scr)
        scores(2 * i + 2, sa_scr)
        update(2 * i + 1, sb_scr)
        return carry

    lax.fori_loop(0, (n - 1) // 2, body, 0)
    if n % 2 == 1:
        update(n - 1, sa_scr)
    else:
        scores(n - 1, sb_scr)
        update(n - 2, sa_scr)
        update(n - 1, sb_scr)
    out = (acc_scr[...] / l_scr[...]).reshape(g * dh, -1)
    o_ref[...] = out.T.astype(o_ref.dtype)


def _flash(qt, k, vt, b, s, tq=None, tk=None):
    w = qt.shape[0]
    hkv, t, dh = k.shape[1], k.shape[2], k.shape[3]
    g = w // dh // hkv
    tq = tq or _tile(s, (256, 128))
    tk = tk or _tile(t, (256, 128))
    nq = s // tq
    return pl.pallas_call(
        functools.partial(_flash_kernel, tk=tk),
        out_shape=jax.ShapeDtypeStruct((b * s, w), BF16),
        grid=(b, hkv, nq),
        in_specs=[pl.BlockSpec((g * dh, tq), lambda bi, hi, i: (hi, bi * nq + i)),
                  pl.BlockSpec((1, 1, t, dh), lambda bi, hi, i: (bi, hi, 0, 0)),
                  pl.BlockSpec((1, 1, dh, t), lambda bi, hi, i: (bi, hi, 0, 0))],
        out_specs=pl.BlockSpec((tq, g * dh), lambda bi, hi, i: (bi * nq + i, hi)),
        scratch_shapes=[pltpu.VMEM((g, 1, tq), F32), pltpu.VMEM((g, 1, tq), F32), pltpu.VMEM((g, dh, tq), F32),
                        pltpu.VMEM((g, tk, tq), F32), pltpu.VMEM((g, tk, tq), F32)],
        compiler_params=_params("parallel", "parallel", "arbitrary"),
        name="dense_attention",
    )(qt, k, vt)


def _bmm_kernel(a_ref, b_ref, o_ref):
    o_ref[0] = jnp.dot(a_ref[0], b_ref[0], preferred_element_type=F32).astype(o_ref.dtype)


def _group_matmul(a, bmat, name):
    g, m, k = a.shape
    n = bmat.shape[2]
    tm = _tile(m, (1024, 512, 256, 128, 64, 32, 16, 8))
    return pl.pallas_call(
        _bmm_kernel,
        out_shape=jax.ShapeDtypeStruct((g, m, n), F32),
        grid=(g, m // tm),
        in_specs=[pl.BlockSpec((1, tm, k), lambda gi, i: (gi, i, 0)),
                  pl.BlockSpec((1, k, n), lambda gi, i: (gi, 0, 0))],
        out_specs=pl.BlockSpec((1, tm, n), lambda gi, i: (gi, i, 0)),
        compiler_params=_params("parallel", "parallel"),
        name=name,
    )(a, bmat)


def _s5_scan_kernel(sre_ref, sim_ref, are_ref, aim_ref, zre_ref, zim_ref):
    nd, nsteps = sre_ref.shape[0], sre_ref.shape[1]
    for d in range(nd):
        a_re, a_im = are_ref[d], aim_ref[d]

        def body(i, carry, d=d, a_re=a_re, a_im=a_im):
            z_re, z_im = carry
            zre_ref[d, i] = z_re
            zim_ref[d, i] = z_im
            return (a_re * z_re - a_im * z_im + sre_ref[d, i], a_re * z_im + a_im * z_re + sim_ref[d, i])

        zero = jnp.zeros(sre_ref.shape[2:], F32)
        lax.fori_loop(0, nsteps, body, (zero, zero))


def _s5_scan(s_re, s_im, a_re, a_im):
    full = lambda arr: pl.BlockSpec(arr.shape, lambda i: (0,) * arr.ndim)
    return pl.pallas_call(
        _s5_scan_kernel,
        out_shape=[jax.ShapeDtypeStruct(s_re.shape, F32)] * 2,
        grid=(1,),
        in_specs=[full(s_re), full(s_im), full(a_re), full(a_im)],
        out_specs=[full(s_re), full(s_im)],
        compiler_params=_params("arbitrary"),
        name="s5_chunk_scan",
    )(s_re, s_im, a_re, a_im)


def _s5_glu_kernel(y_ref, u_ref, d_ref, w_ref, b_ref, o_ref):
    y = y_ref[...] + d_ref[...] * u_ref[...]
    y = jax.nn.gelu(y)
    gate = jnp.dot(y.astype(BF16), w_ref[...], preferred_element_type=F32) + b_ref[...]
    o_ref[...] = (y * jax.nn.sigmoid(gate)).astype(o_ref.dtype)


def _s5_glu(y, u, d_skip, w_glu, b_glu, tm):
    r, c = y.shape
    blk = pl.BlockSpec((tm, c), lambda i: (i, 0))
    vec = pl.BlockSpec((1, c), lambda i: (0, 0))
    return pl.pallas_call(
        _s5_glu_kernel,
        out_shape=jax.ShapeDtypeStruct((r, c), BF16),
        grid=(r // tm,),
        in_specs=[blk, blk, vec, pl.BlockSpec((c, c), lambda i: (0, 0)), vec],
        out_specs=blk,
        compiler_params=_params("parallel"),
        name="s5_readout_glu",
    )(y, u, d_skip.reshape(1, c), w_glu.astype(BF16), b_glu.reshape(1, c))


def _s5_tables(lam_re, lam_im, log_dt, b_re, b_im, c_re, c_im):
    ln, p, gc = S5_CHUNK, S5_STATE, S5_GROUP
    lam = lax.complex(lam_re.astype(F32), lam_im.astype(F32))
    dt = jnp.exp(log_dt.astype(F32))[..., None]
    a_bar = jnp.exp(lam * dt)
    b_scale = (a_bar - 1.0) / lam
    b_mat = lax.complex(b_re.astype(F32), b_im.astype(F32))
    c_mat = lax.complex(c_re.astype(F32), c_im.astype(F32))
    tau = jnp.arange(ln + 1, dtype=F32)
    apow = jnp.exp((lam * dt)[:, :, None, :] * tau[None, None, :, None])
    drive = b_scale[..., None] * b_mat[None]
    kern = jnp.real(jnp.einsum('gcp,dgtp,dgpe->dgtce', c_mat, apow[:, :, :ln], drive))
    kc = jnp.concatenate([kern[1, :, :0:-1], (kern[0, :, :1] + kern[1, :, :1]), kern[0, :, 1:]], axis=1)
    kc_e = kc.transpose(0, 3, 1, 2).reshape(S5_GROUPS, gc, (2 * ln - 1) * gc)
    toep = jnp.stack([kc_e[:, :, (ln - 1 - s) * gc:(2 * ln - 1 - s) * gc] for s in range(ln)], axis=1)
    toep = toep.reshape(S5_GROUPS, ln * gc, ln * gc)
    w_f = apow[0, :, ln - 1::-1][:, :ln, :, None] * drive[0][:, None]
    w_b = apow[1, :, :ln, :, None] * drive[1][:, None]
    def m_in(wc):
        wt = wc.transpose(0, 1, 3, 2).reshape(S5_GROUPS, ln * gc, p)
        return jnp.concatenate([jnp.real(wt), jnp.imag(wt)], axis=-1)
    min_all = jnp.concatenate([m_in(w_f), m_in(w_b)], axis=-1)
    o_f = c_mat[:, None] * apow[0, :, 1:ln + 1][:, :, None, :]
    o_b = c_mat[:, None] * apow[1, :, ln:0:-1][:, :, None, :]
    def m_out(oc):
        ot = oc.transpose(0, 3, 1, 2).reshape(S5_GROUPS, p, ln * gc)
        return jnp.concatenate([jnp.real(ot), -jnp.imag(ot)], axis=1)
    rhs = jnp.concatenate([toep, m_out(o_f), m_out(o_b)], axis=1)
    a_l = apow[:, :, ln].reshape(2, 1, S5_GROUPS * p)
    return min_all.astype(BF16), rhs.astype(BF16), jnp.real(a_l), jnp.imag(a_l)


def _s5_mixer(u_x, u_c, tables, d_skip, w_glu, b_glu):
    min_all, rhs, a_re, a_im = tables
    b, s, _ = u_x.shape
    ln, p, gc, ng = S5_CHUNK, S5_STATE, S5_GROUP, S5_GROUPS

    def chunks(u):
        nc = u.shape[1] // ln
        return u.reshape(b, nc, ln, ng, gc).transpose(3, 0, 1, 2, 4).reshape(ng, b * nc, ln * gc).astype(BF16), nc

    ux, ncx = chunks(u_x)
    uc, ncc = chunks(u_c)
    sx = _group_matmul(ux, min_all, "s5_local_state").reshape(ng, b, ncx, 4, p)
    sc = _group_matmul(uc, min_all, "s5_local_state_ctx").reshape(ng, b, ncc, 4, p)

    def scan_order(part):
        f = jnp.concatenate([sc[:, :, :, part], sx[:, :, :, part]], axis=2)
        r = jnp.concatenate([sc[:, :, ::-1, part + 2], sx[:, :, ::-1, part + 2]], axis=2)
        return jnp.stack([f, r]).transpose(0, 3, 2, 1, 4).reshape(2, ncc + ncx, b, ng * p)

    z_re, z_im = _s5_scan(scan_order(0), scan_order(1), a_re, a_im)

    def latent(z, d):
        zl = z[d, ncc:]
        if d == 1:
            zl = zl[::-1]
        return zl.reshape(ncx, b, ng, p).transpose(2, 1, 0, 3).reshape(ng, b * ncx, p)

    lhs = jnp.concatenate([ux, latent(z_re, 0).astype(BF16), latent(z_im, 0).astype(BF16),
                           latent(z_re, 1).astype(BF16), latent(z_im, 1).astype(BF16)], axis=-1)
    y = _group_matmul(lhs, rhs, "s5_outputs")
    y = y.reshape(ng, b, ncx, ln, gc).transpose(1, 2, 3, 0, 4).reshape(b * s, S5_CHANNELS)
    return _s5_glu(y, u_x.reshape(b * s, S5_CHANNELS), d_skip, w_glu, b_glu, _tile(b * s, (1024, 512, 256)))


def _out_proj_kernel(x_ref, a_ref, b_ref, wa_ref, wb_ref, gt_ref, g_ref, sc_ref, sh_ref, wr_ref, br_ref,
                     xo_ref, xn_ref, lg_ref):
    y = (jnp.dot(a_ref[...], wa_ref[...], preferred_element_type=F32)
         + jnp.dot(b_ref[...], wb_ref[...], preferred_element_type=F32))
    x = x_ref[...] + gt_ref[0] * y
    xo_ref[...] = x
    xn = x * lax.rsqrt(jnp.mean(x * x, axis=-1, keepdims=True) + EPS) * g_ref[...]
    xn = xn * (1.0 + sc_ref[0]) + sh_ref[0]
    hi = xn.astype(BF16)
    xn_ref[...] = hi
    lo = (xn - hi.astype(F32)).astype(BF16)
    part = jnp.dot(hi, wr_ref[...], preferred_element_type=F32)
    part = part[:, :LANES] + part[:, LANES:] + jnp.dot(lo, wr_ref[:, :LANES], preferred_element_type=F32)
    lg_ref[...] = part[:, :lg_ref.shape[1]] + br_ref[...]


def _out_proj(x, row0, a, bmix, wa, wb, gate, g_ffn, scale, shift, mod_map, w_router, b_router, tm):
    r = a.shape[0]
    d = x.shape[1]
    t0 = row0 // tm
    ne = w_router.shape[1]
    w_hi = w_router.astype(BF16)
    w_lo = (w_router - w_hi.astype(F32)).astype(BF16)
    w_router = jnp.concatenate([_pad_cols(w_hi), _pad_cols(w_lo)], axis=1)
    row = lambda w: pl.BlockSpec((tm, w), lambda i: (i, 0))
    mod = pl.BlockSpec((1, 1, d), lambda i: (mod_map(i), 0, 0))
    full = lambda arr: pl.BlockSpec(arr.shape, lambda i: (0, 0))
    return pl.pallas_call(
        _out_proj_kernel,
        out_shape=[jax.ShapeDtypeStruct((r, d), F32), jax.ShapeDtypeStruct((r, d), BF16),
                   jax.ShapeDtypeStruct((r, ne), F32)],
        grid=(r // tm,),
        in_specs=[pl.BlockSpec((tm, d), lambda i: (i + t0, 0)), row(a.shape[1]), row(bmix.shape[1]),
                  full(wa), full(wb), mod, pl.BlockSpec((1, d), lambda i: (0, 0)), mod, mod,
                  full(w_router), pl.BlockSpec((1, ne), lambda i: (0, 0))],
        out_specs=[row(d), row(d), row(ne)],
        compiler_params=_params("parallel"),
        name="out_proj",
    )(x, a, bmix, wa, wb, gate, g_ffn.reshape(1, d), scale, shift, w_router, b_router.reshape(1, ne))


def _route_kernel(lg_ref, idx_ref, gate_ref, pos_ref, cnt_ref, carry):
    tm, ne = lg_ref.shape

    @pl.when(pl.program_id(0) == 0)
    def _():
        carry[...] = jnp.zeros_like(carry)

    work = lg_ref[...]
    lane = lax.broadcasted_iota(I32, (tm, ne), 1).astype(F32)
    out_lane = lax.broadcasted_iota(I32, (tm, LANES), 1)
    vals, hots, idx_out = [], [], jnp.zeros((tm, LANES), I32)
    for kk in range(TOP_K):
        mx = jnp.max(work, axis=1, keepdims=True)
        idx = jnp.min(jnp.where(work == mx, lane, float(ne)), axis=1, keepdims=True)
        hot = lane == idx
        work = jnp.where(hot, -jnp.inf, work)
        vals.append(mx)
        hots.append(hot)
        idx_out = jnp.where(out_lane == kk, idx.astype(I32), idx_out)
    exps = [jnp.exp(vv - vals[0]) for vv in vals]
    tot = exps[0] + exps[1] + exps[2] + exps[3]
    multi = (hots[0] | hots[1] | hots[2] | hots[3]).astype(BF16)
    r_i = lax.broadcasted_iota(I32, (tm, tm), 0)
    c_i = lax.broadcasted_iota(I32, (tm, tm), 1)
    before = (r_i > c_i).astype(BF16)
    prefix = jnp.dot(before, multi, preferred_element_type=F32) + carry[...]
    gate_out = jnp.zeros((tm, LANES), F32)
    pos_out = jnp.zeros((tm, LANES), I32)
    for kk in range(TOP_K):
        gate_out = jnp.where(out_lane == kk, exps[kk] / tot, gate_out)
        pos = jnp.sum(jnp.where(hots[kk], prefix, 0.0), axis=1, keepdims=True).astype(I32)
        pos_out = jnp.where(out_lane == kk, pos, pos_out)
    idx_ref[...] = idx_out
    gate_ref[...] = gate_out
    pos_ref[...] = pos_out
    carry[...] = carry[...] + jnp.sum(multi.astype(F32), axis=0, keepdims=True)
    cnt_ref[...] = carry[...]


def _route(logits, tm, row0, n):
    ne = logits.shape[1]
    t0 = row0 // tm
    wide = pl.BlockSpec((tm, LANES), lambda i: (i, 0))
    return pl.pallas_call(
        _route_kernel,
        out_shape=[jax.ShapeDtypeStruct((n, LANES), I32), jax.ShapeDtypeStruct((n, LANES), F32),
                   jax.ShapeDtypeStruct((n, LANES), I32), jax.ShapeDtypeStruct((1, ne), F32)],
        grid=(n // tm,),
        in_specs=[pl.BlockSpec((tm, ne), lambda i: (i + t0, 0))],
        out_specs=[wide, wide, wide, pl.BlockSpec((1, ne), lambda i: (0, 0))],
        scratch_shapes=[pltpu.VMEM((1, ne), F32)],
        compiler_params=_params("arbitrary"),
        name="moe_route",
    )(logits)


def _expert_kernel(be_ref, nb_ref, x_ref, wgu_ref, bgu_ref, wd_ref, bd_ref, perm_ref, o_ref, wgu_bf, wd_bf):
    i = pl.program_id(0)
    fresh = jnp.logical_or(i == 0, be_ref[i] != be_ref[jnp.maximum(i - 1, 0)])

    @pl.when(jnp.logical_and(i < nb_ref[0], fresh))
    def _():
        tile = perm_ref.shape[0]
        for t in range(wgu_ref.shape[2] // tile):
            sl = slice(t * tile, (t + 1) * tile)
            wgu_bf[:, sl] = jnp.dot(wgu_ref[0, :, sl].astype(BF16), perm_ref[...],
                                    preferred_element_type=F32).astype(BF16)
        wd_bf[...] = wd_ref[0].astype(BF16)

    @pl.when(i < nb_ref[0])
    def _():
        h = jnp.dot(x_ref[...], wgu_bf[...], preferred_element_type=F32) + bgu_ref[0]
        acts = []
        for t in range(h.shape[1] // (2 * LANES)):
            glu = jnp.minimum(h[:, 2 * LANES * t:2 * LANES * t + LANES], SWIGLU_LIMIT)
            lin = jnp.clip(h[:, 2 * LANES * t + LANES:2 * LANES * (t + 1)], -SWIGLU_LIMIT, SWIGLU_LIMIT)
            acts.append((glu * jax.nn.sigmoid(SWIGLU_ALPHA * glu) * (lin + 1.0)).astype(BF16))
        act = jnp.concatenate(acts, axis=1)
        o_ref[...] = (jnp.dot(act, wd_bf[...], preferred_element_type=F32) + bd_ref[0]).astype(o_ref.dtype)

    @pl.when(i >= nb_ref[0])
    def _():
        o_ref[...] = jnp.zeros_like(o_ref)


def _experts(xs, blk_e, n_used, wgu, bgu, wd, bd):
    n, d = xs.shape
    ff2 = wgu.shape[2]
    nblk = n // MOE_ROWS
    tile = 2 * LANES
    j = np.arange(tile)
    col = np.where(j < LANES, 2 * j, 2 * (j - LANES) + 1)
    perm = jnp.asarray(np.arange(tile)[:, None] == col[None, :], BF16)
    wspec = lambda shape: pl.BlockSpec((1,) + shape, lambda i, be, nb: (be[i], 0, 0))
    return pl.pallas_call(
        _expert_kernel,
        out_shape=jax.ShapeDtypeStruct((n, d), BF16),
        grid_spec=pltpu.PrefetchScalarGridSpec(
            num_scalar_prefetch=2,
            grid=(nblk,),
            in_specs=[pl.BlockSpec((MOE_ROWS, d), lambda i, be, nb: (i, 0)),
                      wspec((d, ff2)), wspec((1, ff2)), wspec((ff2 // 2, d)), wspec((1, d)),
                      pl.BlockSpec((tile, tile), lambda i, be, nb: (0, 0))],
            out_specs=pl.BlockSpec((MOE_ROWS, d), lambda i, be, nb: (i, 0)),
            scratch_shapes=[pltpu.VMEM((d, ff2), BF16), pltpu.VMEM((ff2 // 2, d), BF16)]),
        compiler_params=_params("arbitrary"),
        name="moe_experts",
    )(blk_e, n_used, xs, wgu, bgu, wd, bd, perm)


def _combine_kernel(x_ref, y_ref, gate_ref, gt_ref, g_ref, o_ref, *, final_norm):
    gates = gate_ref[...]
    y = gates[:, 0:1] * y_ref[0].astype(F32)
    for kk in range(1, TOP_K):
        y = y + gates[:, kk:kk + 1] * y_ref[kk].astype(F32)
    x = x_ref[...] + gt_ref[0] * y
    if final_norm:
        x = x * lax.rsqrt(jnp.mean(x * x, axis=-1, keepdims=True) + EPS) * g_ref[...]
    o_ref[...] = x


def _combine_alias_kernel(x_ref, y_ref, gate_ref, gt_ref, g_ref, prev_ref, o_ref, *, final_norm):
    del prev_ref
    _combine_kernel(x_ref, y_ref, gate_ref, gt_ref, g_ref, o_ref, final_norm=final_norm)


def _combine(x, y4, gates, gate_mod, mod_map, g_final, final_norm, tm, row0, prev):
    r, d = x.shape
    n = y4.shape[1]
    t0 = row0 // tm
    in_specs = [pl.BlockSpec((tm, d), lambda i: (i + t0, 0)),
                pl.BlockSpec((TOP_K, tm, d), lambda i: (0, i, 0)),
                pl.BlockSpec((tm, LANES), lambda i: (i, 0)),
                pl.BlockSpec((1, 1, d), lambda i: (mod_map(i + t0), 0, 0)),
                pl.BlockSpec((1, d), lambda i: (0, 0))]
    args = [x, y4, gates, gate_mod, g_final.reshape(1, d)]
    body, aliases = _combine_kernel, {}
    if prev is not None:
        in_specs.append(pl.BlockSpec(memory_space=pl.ANY))
        args.append(prev)
        body, aliases = _combine_alias_kernel, {len(args) - 1: 0}
    return pl.pallas_call(
        functools.partial(body, final_norm=final_norm),
        out_shape=jax.ShapeDtypeStruct((r, d), F32),
        grid=(n // tm,),
        in_specs=in_specs,
        out_specs=pl.BlockSpec((tm, d), lambda i: (i + t0, 0)),
        input_output_aliases=aliases,
        compiler_params=_params("parallel"),
        name="moe_combine",
    )(*args)


def _moe(x, xn, logits, gate_mod, mod_map, weights, g_final, final_norm, tm):
    wgu, bgu, wd, bd = weights
    r, d = xn.shape
    parts = MOE_PARTS if (r // tm) % MOE_PARTS == 0 else 1
    n = r // parts
    nblk = n * TOP_K // MOE_ROWS + N_EXPERTS
    blk_row0 = jnp.arange(nblk, dtype=I32) * MOE_ROWS
    routed = []
    for p in range(parts):
        idx_w, gates_w, pos_w, counts = _route(logits, tm, p * n, n)
        idx, pos = idx_w[:, :TOP_K], pos_w[:, :TOP_K]
        counts = counts[0].astype(I32)
        padded = (counts + MOE_ROWS - 1) // MOE_ROWS * MOE_ROWS
        pad_end = jnp.cumsum(padded)
        pad_start = pad_end - padded
        dest = pad_start[idx] + pos
        blk_e = jnp.minimum(jnp.sum((pad_end[None, :] <= blk_row0[:, None]).astype(I32), axis=1), N_EXPERTS - 1)
        n_used = (pad_end[-1:] // MOE_ROWS).astype(I32)
        key = idx * n + jnp.arange(n, dtype=I32)[:, None]
        tok_sorted = jnp.sort(key.reshape(-1)) % n
        grp_start = jnp.cumsum(counts) - counts
        slot = blk_row0[:, None] + jnp.arange(MOE_ROWS, dtype=I32)[None, :]
        compact = jnp.minimum(slot + (grp_start - pad_start)[blk_e][:, None], (grp_start + counts - 1)[blk_e][:, None])
        src = tok_sorted.at[jnp.clip(compact, 0, n * TOP_K - 1).reshape(-1)].get(mode="promise_in_bounds")
        xs = xn.at[src + p * n].get(mode="promise_in_bounds")
        routed.append((xs, blk_e, n_used, dest, gates_w))
    ys = [_experts(xs, blk_e, n_used, wgu, bgu, wd, bd) for xs, blk_e, n_used, _, _ in routed]
    out = None
    for p, (_, _, _, dest, gates_w) in enumerate(routed):
        y4 = ys[p].at[dest.T.reshape(-1)].get(mode="promise_in_bounds").reshape(TOP_K, n, d)
        out = _combine(x, y4, gates_w, gate_mod, mod_map, g_final, final_norm, tm, p * n, out)
    return out


def _moe_weights(w_gate_up, b_gate_up, w_down, b_down):
    ne, d, ff2 = w_gate_up.shape
    b_tiled = b_gate_up.reshape(ne, ff2 // (2 * LANES), LANES, 2).swapaxes(-1, -2).reshape(ne, 1, ff2)
    return w_gate_up, b_tiled, w_down, b_down[:, None, :]


def _rope_tables(n_tokens):
    rows = n_tokens // GRID_W
    row = jnp.repeat(jnp.arange(rows, dtype=I32), GRID_W).astype(F32)
    col = jnp.tile(jnp.arange(GRID_W, dtype=I32), rows).astype(F32)
    inv = ROPE_BASE ** (-jnp.arange(0, ROPE_AXIS_DIM, 2, dtype=F32) / ROPE_AXIS_DIM)
    ang_r, ang_c = row[:, None] * inv, col[:, None] * inv
    cos = jnp.concatenate([jnp.cos(ang_r)] * 2 + [jnp.cos(ang_c)] * 2, axis=-1)
    sin = jnp.concatenate([jnp.sin(ang_r)] * 2 + [jnp.sin(ang_c)] * 2, axis=-1)
    return cos, sin


def _rot_perm():
    q = ROPE_AXIS_DIM // 2
    d = np.arange(HEAD_DIM)
    first = (d % ROPE_AXIS_DIM) < q
    perm = np.where(first, d + q, d - q)
    sign = np.where(first, -1.0, 1.0).astype(np.float32)
    return perm, sign


def _rot_cols(w, n_heads):
    perm, sign = _rot_perm()
    k = w.shape[0]
    wh = w.reshape(k, n_heads, HEAD_DIM)
    return (wh[:, :, perm] * sign).reshape(k, n_heads * HEAD_DIM)


def _to_heads(t, b, n_heads):
    return t.reshape(b, -1, n_heads, HEAD_DIM).transpose(0, 2, 1, 3)


def _from_heads(t):
    b, h, tt, dh = t.shape
    return t.transpose(0, 2, 1, 3).reshape(b * tt, h * dh)


def _layer_even(xa, mod, b, s, lc, g_mix, g_ffn, w_in, w_out, gate_bias, mlstm_norm, sink, router, moe_w):
    d = xa.shape[1]
    nctx = b * lc
    tm = _tile(math.gcd(nctx, s), (512, 256, 128))
    mod_map = lambda i: jnp.where(i * tm < nctx, b, (i * tm - nctx) // s)
    sh1, sc1, gt1, sh2, sc2, gt2 = [mod[:, j][:, None, :] for j in range(6)]

    hm, hs, hk = MLSTM_HEADS * HEAD_DIM, SWA_HEADS * HEAD_DIM, SWA_KV_HEADS * HEAD_DIM
    o0 = np.cumsum([0, hm, hm, hm, hm, 4 * MLSTM_HEADS, hs, hk, hk])
    seg = lambda j: w_in[:, o0[j]:o0[j + 1]]
    w_cat = jnp.concatenate([seg(0), seg(1) * ATTN_SCALE, seg(2), seg(3),
                             seg(6), _rot_cols(seg(6), SWA_KV_HEADS), seg(7), _pad_cols(seg(4))], axis=1).astype(BF16)
    w_q = jnp.concatenate([seg(5), _rot_cols(seg(5), SWA_HEADS)], axis=1).T.astype(BF16)
    widths = [hm, hm, hm, hm, hk, hk, hk, LANES]
    dts = [BF16] * 7 + [F32]
    qa, ka, va, oa, kb, kb_r, vb, gts, qt, qt_r = _in_proj(xa, g_mix, sc1, sh1, mod_map, w_cat, widths, dts, tm,
                                                           wt=w_q, t_widths=[hs, hs], t_dtypes=[BF16, BF16])

    hf, hb = _mlstm(qa, ka, va, gts, _pad_cols(gate_bias.astype(F32).reshape(1, -1)), b, lc, s)
    mix_a = _mlstm_out(hf, hb, oa, mlstm_norm.reshape(-1), tm)

    cos, sin = _rope_tables(s)
    ones = jnp.ones((HEAD_DIM,), F32)
    g = SWA_HEADS // SWA_KV_HEADS
    heads = lambda t, n: _to_heads(t, b, n)
    log2e = math.log2(math.e)
    q_x = _q_prep_t(qt, qt_r, nctx, b * s, cos.T, sin.T, ones, ones, ATTN_SCALE * log2e, norm=False)
    k_x = _qk_prep(heads(kb[nctx:], SWA_KV_HEADS), heads(kb_r[nctx:], SWA_KV_HEADS), cos, sin, ones, ones,
                   norm=False, rope=True, scale=1.0)
    k_c, v_c, v_x = heads(kb[:nctx], SWA_KV_HEADS), heads(vb[:nctx], SWA_KV_HEADS), heads(vb[nctx:], SWA_KV_HEADS)
    q_c = heads(qt[:, :nctx].T, SWA_HEADS)
    sink_h = sink.astype(F32).reshape(SWA_KV_HEADS, g, 1)
    att_x = _swa(q_x, k_x, v_x.swapaxes(-1, -2), k_c, v_c.swapaxes(-1, -2),
                 jnp.broadcast_to(sink_h * log2e, (SWA_KV_HEADS, g, 2 * SWA_BLOCK)), b, s)
    att_c = _ctx_attn(q_c, k_c, v_c, jnp.broadcast_to(sink_h[..., None], (SWA_KV_HEADS, g, lc, 1))
                      .reshape(SWA_KV_HEADS, g * lc, 1), True)
    mix_b = jnp.concatenate([_from_heads(att_c), att_x], axis=0)

    w_router, b_router = router
    xa, xn2, logits = _out_proj(xa, 0, mix_a, mix_b, w_out[:hm].astype(BF16), w_out[hm:].astype(BF16), gt1, g_ffn,
                                sc2, sh2, mod_map, w_router.astype(F32), b_router.astype(F32), tm)
    return _moe(xa, xn2, logits, gt2, mod_map, moe_w, g_ffn, False, tm)


def _layer_odd_last(xa, mod, b, s, lc, g_mix, g_ffn, w_in, w_out, s5_params, d_skip, w_glu, b_glu,
                    q_norm, k_norm, router, moe_w, g_final):
    nctx = b * lc
    tm = _tile(math.gcd(nctx, s), (512, 256, 128))
    mod_map = lambda i: jnp.where(i * tm < nctx, b, (i * tm - nctx) // s)
    lat_map = lambda i: i * tm // s
    sh1, sc1, gt1, sh2, sc2, gt2 = [mod[:, j][:, None, :] for j in range(6)]

    hq, hk = ATT_HEADS * HEAD_DIM, ATT_KV_HEADS * HEAD_DIM
    o1 = np.cumsum([0, S5_CHANNELS, hq, hk, hk])
    seg = lambda j: w_in[:, o1[j]:o1[j + 1]]
    kpad = lambda w: _pad_cols(w, 2 * LANES)
    w_cat = jnp.concatenate([seg(0), kpad(seg(2)), kpad(_rot_cols(seg(2), ATT_KV_HEADS)), kpad(seg(3))],
                            axis=1).astype(BF16)
    w_q = jnp.concatenate([seg(1), _rot_cols(seg(1), ATT_HEADS)], axis=1).T.astype(BF16)
    widths = [S5_CHANNELS, 2 * LANES, 2 * LANES, 2 * LANES]
    dts = [F32, BF16, BF16, BF16]
    u, k, k_r, v, qt, qt_r = _in_proj(xa, g_mix, sc1, sh1, mod_map, w_cat, widths, dts, tm,
                                      wt=w_q, t_widths=[hq, hq], t_dtypes=[BF16, BF16])

    mix_a = _s5_mixer(u[nctx:].reshape(b, s, S5_CHANNELS), u[:nctx].reshape(b, lc, S5_CHANNELS),
                      _s5_tables(*s5_params), d_skip, w_glu, b_glu)

    cos, sin = _rope_tables(s)
    perm, _ = _rot_perm()
    qn, kn = q_norm.astype(F32), k_norm.astype(F32)
    heads = lambda t, n: _to_heads(t[:, :n * HEAD_DIM], b, n)
    q_x = _q_prep_t(qt, qt_r, nctx, b * s, cos.T, sin.T, qn, qn[perm], ATTN_SCALE * math.log2(math.e))
    k_x = _qk_prep(heads(k[nctx:], ATT_KV_HEADS), heads(k_r[nctx:], ATT_KV_HEADS), cos, sin, kn, kn[perm],
                   norm=True, rope=True, scale=1.0)
    kc_raw = heads(k[:nctx], ATT_KV_HEADS)
    k_c = _qk_prep(kc_raw, kc_raw, cos[:lc], sin[:lc], kn, kn, norm=True, rope=False, scale=1.0)
    k_all = jnp.concatenate([k_c, k_x], axis=2)
    v_all = jnp.concatenate([heads(v[:nctx], ATT_KV_HEADS), heads(v[nctx:], ATT_KV_HEADS)], axis=2)
    mix_b = _flash(q_x, k_all, v_all.swapaxes(-1, -2), b, s)

    w_router, b_router = router
    hs = S5_CHANNELS
    x, xn2, logits = _out_proj(xa, nctx, mix_a, mix_b, w_out[:hs].astype(BF16), w_out[hs:].astype(BF16), gt1, g_ffn,
                               sc2, sh2, lat_map, w_router.astype(F32), b_router.astype(F32), tm)
    return _moe(x, xn2, logits, gt2, lat_map, moe_w, g_final, True, tm)


def kernel(x, c, ctx, c_ctx, l0_w_mod, l0_b_mod, l0_g_mix, l0_g_ffn, l0_w_in, l0_w_out, l0_gate_bias, l0_mlstm_norm, l0_sink, l0_w_router, l0_b_router, l0_w_gate_up, l0_b_gate_up, l0_w_down, l0_b_down, l1_w_mod, l1_b_mod, l1_g_mix, l1_g_ffn, l1_w_in, l1_w_out, l1_lam_re, l1_lam_im, l1_log_dt, l1_b_re, l1_b_im, l1_c_re, l1_c_im, l1_d_skip, l1_w_glu, l1_b_glu, l1_q_norm, l1_k_norm, l1_w_router, l1_b_router, l1_w_gate_up, l1_b_gate_up, l1_w_down, l1_b_down, g_final):
    b, s, d = x.shape
    lc = ctx.shape[1]
    cond = jnp.concatenate([c, c_ctx[None, :]], axis=0)
    cond = jnp.pad(cond, ((0, (-(b + 1)) % 8), (0, 0)))
    mod0 = _silu_linear(cond, l0_w_mod, l0_b_mod)[:b + 1].reshape(b + 1, 6, d)
    mod1 = _silu_linear(cond, l1_w_mod, l1_b_mod)[:b + 1].reshape(b + 1, 6, d)

    xa = jnp.concatenate([ctx.reshape(b * lc, d), x.reshape(b * s, d)], axis=0)
    xa = _layer_even(xa, mod0, b, s, lc, l0_g_mix, l0_g_ffn, l0_w_in, l0_w_out, l0_gate_bias, l0_mlstm_norm,
                     l0_sink, (l0_w_router, l0_b_router),
                     _moe_weights(l0_w_gate_up, l0_b_gate_up, l0_w_down, l0_b_down))
    out = _layer_odd_last(xa, mod1, b, s, lc, l1_g_mix, l1_g_ffn, l1_w_in, l1_w_out,
                          (l1_lam_re, l1_lam_im, l1_log_dt, l1_b_re, l1_b_im, l1_c_re, l1_c_im),
                          l1_d_skip, l1_w_glu, l1_b_glu, l1_q_norm, l1_k_norm, (l1_w_router, l1_b_router),
                          _moe_weights(l1_w_gate_up, l1_b_gate_up, l1_w_down, l1_b_down), g_final)
    return out.reshape(b, s, d)
```

```python
import functools
import math

import jax
import jax.numpy as jnp
import numpy as np
from jax import lax
from jax.experimental import pallas as pl
from jax.experimental.pallas import tpu as pltpu

F32 = jnp.float32
BF16 = jnp.bfloat16
I32 = jnp.int32

GRID_W = 64
HEAD_DIM = 64
ATTN_SCALE = HEAD_DIM ** -0.5
ROPE_AXIS_DIM = HEAD_DIM // 2
ROPE_BASE = 10000.0
EPS = 1e-6
NEG_INF = -1e30

MLSTM_HEADS = 8
MLSTM_CHUNKS = (256, 128, 64)
SWA_HEADS = 8
SWA_KV_HEADS = 2
SWA_WINDOW = 128
SWA_BLOCK = 128
S5_CHANNELS = 256
S5_GROUP = 16
S5_GROUPS = S5_CHANNELS // S5_GROUP
S5_STATE = 64
S5_CHUNK = 64
ATT_HEADS = 12
ATT_KV_HEADS = 3
N_EXPERTS = 32
TOP_K = 4
SWIGLU_LIMIT = 7.0
SWIGLU_ALPHA = 1.702

LANES = 128
VMEM_LIMIT = 56 * 1024 * 1024
MOE_ROWS = 512
MOE_PARTS = 1
BATCH_GROUPS = 2
HIGHEST = lax.Precision.HIGHEST


def _params(*sem):
    return pltpu.CompilerParams(dimension_semantics=sem, vmem_limit_bytes=VMEM_LIMIT)


def _tile(n, prefs):
    for t in prefs:
        if n % t == 0:
            return t
    return n


def _pad_cols(w, mult=LANES):
    pad = (-w.shape[-1]) % mult
    if pad:
        w = jnp.pad(w, [(0, 0)] * (w.ndim - 1) + [(0, pad)])
    return w


def _linear_kernel(x_ref, w_ref, b_ref, o_ref):
    x = x_ref[...]
    x = x * jax.nn.sigmoid(x)
    o_ref[...] = jnp.dot(x, w_ref[...], precision=HIGHEST, preferred_element_type=F32) + b_ref[...]


def _silu_linear(x, w, b):
    m, k = x.shape
    n = w.shape[1]
    tn = _tile(n, (1024, 512, 256, 128))
    return pl.pallas_call(
        _linear_kernel,
        out_shape=jax.ShapeDtypeStruct((m, n), F32),
        grid=(n // tn,),
        in_specs=[pl.BlockSpec((m, k), lambda j: (0, 0)),
                  pl.BlockSpec((k, tn), lambda j: (0, j)),
                  pl.BlockSpec((1, tn), lambda j: (0, j))],
        out_specs=pl.BlockSpec((m, tn), lambda j: (0, j)),
        compiler_params=_params("arbitrary"),
        name="adaln_linear",
    )(x, w, b.reshape(1, n))


def _in_proj_kernel(x_ref, g_ref, sc_ref, sh_ref, w_ref, wt_ref, *out_refs, widths, t_widths):
    x = x_ref[...]
    xn = x * lax.rsqrt(jnp.mean(x * x, axis=-1, keepdims=True) + EPS) * g_ref[...]
    xb = (xn * (1.0 + sc_ref[0]) + sh_ref[0]).astype(BF16)
    off = 0
    for o_ref, w in zip(out_refs, widths):
        o_ref[...] = jnp.dot(xb, w_ref[:, off:off + w], preferred_element_type=F32).astype(o_ref.dtype)
        off += w
    off = 0
    for o_ref, w in zip(out_refs[len(widths):], t_widths):
        o_ref[...] = lax.dot_general(wt_ref[off:off + w, :], xb, (((1,), (1,)), ((), ())),
                                     preferred_element_type=F32).astype(o_ref.dtype)
        off += w


def _in_proj(x, g, scale, shift, mod_map, w, widths, dtypes, tm, wt=None, t_widths=(), t_dtypes=()):
    r, d = x.shape
    if wt is None:
        wt = jnp.zeros((8, d), BF16)
    return pl.pallas_call(
        functools.partial(_in_proj_kernel, widths=tuple(widths), t_widths=tuple(t_widths)),
        out_shape=([jax.ShapeDtypeStruct((r, wd), dt) for wd, dt in zip(widths, dtypes)]
                   + [jax.ShapeDtypeStruct((wd, r), dt) for wd, dt in zip(t_widths, t_dtypes)]),
        grid=(r // tm,),
        in_specs=[pl.BlockSpec((tm, d), lambda i: (i, 0)),
                  pl.BlockSpec((1, d), lambda i: (0, 0)),
                  pl.BlockSpec((1, 1, d), lambda i: (mod_map(i), 0, 0)),
                  pl.BlockSpec((1, 1, d), lambda i: (mod_map(i), 0, 0)),
                  pl.BlockSpec(w.shape, lambda i: (0, 0)),
                  pl.BlockSpec(wt.shape, lambda i: (0, 0))],
        out_specs=([pl.BlockSpec((tm, wd), lambda i: (i, 0)) for wd in widths]
                   + [pl.BlockSpec((wd, tm), lambda i: (0, i)) for wd in t_widths]),
        compiler_params=_params("parallel"),
        name="in_proj",
    )(x, g.reshape(1, d), scale, shift, w, wt)


def _qk_prep_kernel(x_ref, xr_ref, cos_ref, sin_ref, g_ref, gr_ref, o_ref, *, norm, rope, scale):
    x = x_ref[0].astype(F32)
    if norm:
        s = lax.rsqrt(jnp.mean(x * x, axis=-1, keepdims=True) + EPS)
        x = x * s * g_ref[...]
    if rope:
        xr = xr_ref[0].astype(F32)
        if norm:
            xr = xr * s * gr_ref[...]
        x = x * cos_ref[...] + xr * sin_ref[...]
    o_ref[0] = (x * scale).astype(o_ref.dtype)


def _qk_prep(x, x_rot, cos, sin, gain, gain_rot, *, norm, rope, scale):
    b, h, t, dh = x.shape
    tt = _tile(t, (512, 256, 128))
    blk = pl.BlockSpec((1, h, tt, dh), lambda i, j: (i, 0, j, 0))
    tab = pl.BlockSpec((tt, dh), lambda i, j: (j, 0))
    vec = pl.BlockSpec((1, dh), lambda i, j: (0, 0))
    return pl.pallas_call(
        functools.partial(_qk_prep_kernel, norm=norm, rope=rope, scale=scale),
        out_shape=jax.ShapeDtypeStruct(x.shape, BF16),
        grid=(b, t // tt),
        in_specs=[blk, blk, tab, tab, vec, vec],
        out_specs=blk,
        compiler_params=_params("parallel", "parallel"),
        name="qk_prep",
    )(x, x_rot, cos, sin, gain.reshape(1, dh), gain_rot.reshape(1, dh))


def _q_prep_t_kernel(x_ref, xr_ref, cos_ref, sin_ref, g_ref, gr_ref, o_ref, *, norm, scale):
    w, tt = x_ref.shape
    nh = w // HEAD_DIM
    x = x_ref[...].astype(F32).reshape(nh, HEAD_DIM, tt)
    xr = xr_ref[...].astype(F32).reshape(nh, HEAD_DIM, tt)
    if norm:
        s = lax.rsqrt(jnp.mean(x * x, axis=1, keepdims=True) + EPS)
        x, xr = x * s * g_ref[...], xr * s * gr_ref[...]
    y = x * cos_ref[...] + xr * sin_ref[...]
    o_ref[...] = (y * scale).reshape(w, tt).astype(o_ref.dtype)


def _q_prep_t(xt, xt_rot, col0, n_cols, cos_t, sin_t, gain, gain_rot, scale, norm=True):
    w = xt.shape[0]
    s = cos_t.shape[1]
    tt = _tile(math.gcd(col0, s), (512, 256, 128))
    blk = pl.BlockSpec((w, tt), lambda j: (0, j + col0 // tt))
    tab = pl.BlockSpec((HEAD_DIM, tt), lambda j: (0, j % (s // tt)))
    vec = pl.BlockSpec((HEAD_DIM, 1), lambda j: (0, 0))
    return pl.pallas_call(
        functools.partial(_q_prep_t_kernel, norm=norm, scale=scale),
        out_shape=jax.ShapeDtypeStruct((w, n_cols), BF16),
        grid=(n_cols // tt,),
        in_specs=[blk, blk, tab, tab, vec, vec],
        out_specs=pl.BlockSpec((w, tt), lambda j: (0, j)),
        compiler_params=_params("parallel"),
        name="q_prep_t",
    )(xt, xt_rot, cos_t, sin_t, gain.reshape(HEAD_DIM, 1), gain_rot.reshape(HEAD_DIM, 1))


def _log_sigmoid(x):
    return jnp.minimum(x, 0.0) - jnp.log(1.0 + jnp.exp(-jnp.abs(x)))


def _mlstm_kernel(qf, kf, vf, gf, qb, kb, vb, gb, bias_ref, hf_ref, hb_ref, c_scr, n_scr, m_scr):
    ch, nh, dh = qf.shape[0], MLSTM_HEADS, HEAD_DIM

    @pl.when(pl.program_id(1) == 0)
    def _():
        c_scr[...] = jnp.zeros_like(c_scr)
        n_scr[...] = jnp.zeros_like(n_scr)
        m_scr[...] = jnp.full_like(m_scr, NEG_INF)

    row = lax.broadcasted_iota(I32, (ch, ch), 0)
    col = lax.broadcasted_iota(I32, (ch, ch), 1)
    lane_lo = lax.broadcasted_iota(I32, (ch, 2 * dh), 1) < dh
    row_lo = lax.broadcasted_iota(I32, (2 * dh, 2 * dh), 0) < dh
    col_lo = lax.broadcasted_iota(I32, (2 * dh, 2 * dh), 1) < dh
    vec_lo = lax.broadcasted_iota(I32, (1, 2 * dh), 1) < dh
    pair = lambda a, b: jnp.where(lane_lo, a, b)
    refs = ((qf, kf, vf, gf, hf_ref), (qb, kb, vb, gb, hb_ref))
    jobs = [(d, p) for d in range(2) for p in range(nh // 2)]
    sl = lambda p: slice(2 * dh * p, 2 * dh * (p + 1))

    qk = {}
    for d, p in jobs:
        q2, k2 = refs[d][0][:, sl(p)], refs[d][1][:, sl(p)]
        zero = jnp.zeros_like(q2)
        q_st = jnp.concatenate([jnp.where(lane_lo, q2, zero), jnp.where(lane_lo, zero, q2)], axis=0)
        qk[d, p] = lax.dot_general(q_st, k2, (((1,), (1,)), ((), ())), preferred_element_type=F32)

    st = {}
    for d in range(2):
        seen = (col <= row) if d == 0 else (col >= row)
        tri = seen.astype(F32)
        tri_t = ((row <= col) if d == 0 else (row >= col)).astype(F32)
        last = ch - 1 if d == 0 else 0
        g = refs[d][3][...] + bias_ref[...]
        g_t = g.T
        lo = 2 * nh * d
        li_col, lf_col = g[:, lo:lo + nh], _log_sigmoid(g[:, lo + nh:lo + 2 * nh])
        li_row, lf_row = g_t[lo:lo + nh, :], _log_sigmoid(g_t[lo + nh:lo + 2 * nh, :])
        b_col = jnp.dot(tri, lf_col, precision=HIGHEST, preferred_element_type=F32)
        b_row = jnp.dot(lf_row, tri_t, precision=HIGHEST, preferred_element_type=F32)
        for h in range(nh):
            bc, br = b_col[:, h:h + 1], b_row[h:h + 1, :]
            lir, lic = li_row[h:h + 1, :], li_col[:, h:h + 1]
            b_last = br[:, last:last + 1]
            m_prev = m_scr[d, h // 2][:, (h % 2) * dh:(h % 2) * dh + 1]
            d_log = jnp.where(seen, bc - br + lir, NEG_INF)
            inter_log = bc + m_prev
            m_t = jnp.maximum(inter_log, jnp.max(d_log, axis=1, keepdims=True))
            w_log = b_last - bc + lic
            m_new = jnp.maximum(b_last + m_prev, jnp.max(w_log, axis=0, keepdims=True))
            st[d, h] = dict(dmat=jnp.exp(d_log - m_t), inter=jnp.exp(inter_log - m_t), floor=jnp.exp(-m_t),
                            m_new=m_new, decay=jnp.exp(b_last + m_prev - m_new), wn=jnp.exp(w_log - m_new))

    mm = {}
    for d, p in jobs:
        a, b = st[d, 2 * p], st[d, 2 * p + 1]
        q2, k2, v2 = refs[d][0][:, sl(p)], refs[d][1][:, sl(p)], refs[d][2][:, sl(p)]
        s = qk[d, p] * jnp.concatenate([a["dmat"], b["dmat"]], axis=0)
        kw = k2.astype(F32) * pair(a["wn"], b["wn"])
        mm[d, p] = dict(
            s_sum=jnp.sum(s, axis=1, keepdims=True), kw_sum=jnp.sum(kw, axis=0, keepdims=True),
            sv=jnp.dot(s.astype(BF16), v2, preferred_element_type=F32),
            q_c=jnp.dot(q2, c_scr[d, p].astype(BF16), preferred_element_type=F32),
            kv=lax.dot_general(kw.astype(BF16), v2, (((0,), (0,)), ((), ())), preferred_element_type=F32))

    for d, p in jobs:
        a, b, r = st[d, 2 * p], st[d, 2 * p + 1], mm[d, p]
        q2 = refs[d][0][:, sl(p)]
        n_prev = n_scr[d, p]
        num = pair(a["inter"], b["inter"]) * r["q_c"] + jnp.where(lane_lo, r["sv"][:ch], r["sv"][ch:])
        qn = q2.astype(F32) * n_prev
        qn_a = jnp.sum(jnp.where(lane_lo, qn, 0.0), axis=1, keepdims=True)
        qn_b = jnp.sum(jnp.where(lane_lo, 0.0, qn), axis=1, keepdims=True)
        den_a = jnp.maximum(jnp.abs(a["inter"] * qn_a + r["s_sum"][:ch]), a["floor"])
        den_b = jnp.maximum(jnp.abs(b["inter"] * qn_b + r["s_sum"][ch:]), b["floor"])
        refs[d][4][:, sl(p)] = num / pair(den_a, den_b)
        c_scr[d, p] = (jnp.where(row_lo, a["decay"], b["decay"]) * c_scr[d, p]
                       + jnp.where(row_lo == col_lo, r["kv"], 0.0))
        n_scr[d, p] = jnp.where(vec_lo, a["decay"], b["decay"]) * n_prev + r["kw_sum"]
        m_scr[d, p] = jnp.where(vec_lo, a["m_new"], b["m_new"])


def _mlstm(q, k, v, gates, bias, b, lc, s):
    r, w = q.shape
    ch = _tile(math.gcd(lc, s), MLSTM_CHUNKS)
    ncc, ncx = lc // ch, s // ch
    base = b * ncc

    def fwd(i, c):
        return jnp.where(c < ncc, i * ncc + c, base + i * ncx + (c - ncc)), 0

    def bwd(i, c):
        return jnp.where(c < ncc, i * ncc + (ncc - 1 - c), base + i * ncx + (ncx - 1 - (c - ncc))), 0

    spec = lambda width, m: pl.BlockSpec((ch, width), m)
    npair = MLSTM_HEADS // 2
    return pl.pallas_call(
        _mlstm_kernel,
        out_shape=[jax.ShapeDtypeStruct((r, w), F32)] * 2,
        grid=(b, ncc + ncx),
        in_specs=[spec(w, fwd), spec(w, fwd), spec(w, fwd), spec(LANES, fwd),
                  spec(w, bwd), spec(w, bwd), spec(w, bwd), spec(LANES, bwd),
                  pl.BlockSpec((1, LANES), lambda i, c: (0, 0))],
        out_specs=[spec(w, fwd), spec(w, bwd)],
        scratch_shapes=[pltpu.VMEM((2, npair, 2 * HEAD_DIM, 2 * HEAD_DIM), F32),
                        pltpu.VMEM((2, npair, 1, 2 * HEAD_DIM), F32),
                        pltpu.VMEM((2, npair, 1, 2 * HEAD_DIM), F32)],
        compiler_params=_params("parallel", "arbitrary"),
        name="mlstm_scan",
    )(q, k, v, gates, q, k, v, gates, bias)


def _mlstm_out_kernel(hf_ref, hb_ref, o_ref, nrm_ref, ones_ref, y_ref):
    h = hf_ref[...] + hb_ref[...]
    ms = jnp.dot((h * h).astype(BF16), ones_ref[...], preferred_element_type=F32) * (1.0 / HEAD_DIM)
    hn = h * lax.rsqrt(ms + EPS) * nrm_ref[...]
    y_ref[...] = (jax.nn.sigmoid(o_ref[...].astype(F32)) * hn).astype(y_ref.dtype)


def _head_ones(width):
    idx = np.arange(width) // HEAD_DIM
    return jnp.asarray(idx[:, None] == idx[None, :], BF16)


def _mlstm_out(hf, hb, o, norm, tm):
    r, w = hf.shape
    blk = pl.BlockSpec((tm, w), lambda i: (i, 0))
    return pl.pallas_call(
        _mlstm_out_kernel,
        out_shape=jax.ShapeDtypeStruct((r, w), BF16),
        grid=(r // tm,),
        in_specs=[blk, blk, blk,
                  pl.BlockSpec((1, w), lambda i: (0, 0)),
                  pl.BlockSpec((w, w), lambda i: (0, 0))],
        out_specs=blk,
        compiler_params=_params("parallel"),
        name="mlstm_out",
    )(hf, hb, o, norm.reshape(1, w), _head_ones(w))


def _swa_kernel(q_ref, k0, k1, k2, k3, v0, v1, v2, v3, kc_ref, vc_ref, sink_ref, o_ref, *, seq):
    w, dh = SWA_BLOCK, HEAD_DIM
    tq = q_ref.shape[1]
    g = q_ref.shape[0] // dh
    i = pl.program_id(2)
    k_loc = jnp.concatenate([k0[0, 0], k1[0, 0], k2[0, 0], k3[0, 0]], axis=0)
    v_loc = jnp.concatenate([v0[0, 0], v1[0, 0], v2[0, 0], v3[0, 0]], axis=1)
    kpos = (2 * i - 1) * w + lax.broadcasted_iota(I32, (4 * w, tq), 0)
    qpos = i * tq + lax.broadcasted_iota(I32, (4 * w, tq), 1)
    valid = (jnp.abs(qpos - kpos) <= SWA_WINDOW) & (kpos >= 0) & (kpos < seq)
    qs = [q_ref[h * dh:(h + 1) * dh, :] for h in range(g)]
    s_loc = [jnp.where(valid, jnp.dot(k_loc, qs[h], preferred_element_type=F32), NEG_INF) for h in range(g)]
    s_ctx = [jnp.dot(kc_ref[0, 0], qs[h], preferred_element_type=F32) for h in range(g)]
    outs = []
    for h in range(g):
        sink = sink_ref[0, h:h + 1, :]
        m = jnp.maximum(jnp.maximum(jnp.max(s_loc[h], axis=0, keepdims=True),
                                    jnp.max(s_ctx[h], axis=0, keepdims=True)), sink)
        p_loc = jnp.exp2(s_loc[h] - m)
        p_ctx = jnp.exp2(s_ctx[h] - m)
        den = jnp.sum(p_loc, axis=0, keepdims=True) + jnp.sum(p_ctx, axis=0, keepdims=True) + jnp.exp2(sink - m)
        o = (jnp.dot(v_loc, p_loc.astype(BF16), preferred_element_type=F32)
             + jnp.dot(vc_ref[0, 0], p_ctx.astype(BF16), preferred_element_type=F32))
        outs.append(o / den)
    o_ref[...] = jnp.concatenate(outs, axis=0).T.astype(o_ref.dtype)


def _swa(qt, k, vt, k_ctx, vt_ctx, sink2, b, s):
    hkv, dh = k.shape[1], k.shape[3]
    nh = qt.shape[0] // dh
    g = nh // hkv
    w = SWA_BLOCK
    tq = 2 * w
    nb, nq = s // w, s // tq
    lc = k_ctx.shape[2]
    clampi = lambda j: jnp.clip(j, 0, nb - 1)
    kspec = lambda o: pl.BlockSpec((1, 1, w, dh), lambda bi, hi, i: (bi, hi, clampi(2 * i + o), 0))
    vspec = lambda o: pl.BlockSpec((1, 1, dh, w), lambda bi, hi, i: (bi, hi, 0, clampi(2 * i + o)))
    return pl.pallas_call(
        functools.partial(_swa_kernel, seq=s),
        out_shape=jax.ShapeDtypeStruct((b * s, nh * dh), BF16),
        grid=(b, hkv, nq),
        in_specs=[pl.BlockSpec((g * dh, tq), lambda bi, hi, i: (hi, bi * nq + i)),
                  kspec(-1), kspec(0), kspec(1), kspec(2), vspec(-1), vspec(0), vspec(1), vspec(2),
                  pl.BlockSpec((1, 1, lc, dh), lambda bi, hi, i: (bi, hi, 0, 0)),
                  pl.BlockSpec((1, 1, dh, lc), lambda bi, hi, i: (bi, hi, 0, 0)),
                  pl.BlockSpec((1, g, tq), lambda bi, hi, i: (hi, 0, 0))],
        out_specs=pl.BlockSpec((tq, g * dh), lambda bi, hi, i: (bi * nq + i, hi)),
        compiler_params=_params("parallel", "parallel", "arbitrary"),
        name="window_attention",
    )(qt, k, k, k, k, vt, vt, vt, vt, k_ctx, vt_ctx, sink2)


def _ctx_attn_kernel(q_ref, k_ref, v_ref, sink_ref, o_ref, *, use_sink):
    g, lq, dh = q_ref.shape[1:]
    q = q_ref[0].reshape(g * lq, dh)
    s = lax.dot_general(q, k_ref[0, 0], (((1,), (1,)), ((), ())), preferred_element_type=F32) * ATTN_SCALE
    m = jnp.max(s, axis=1, keepdims=True)
    if use_sink:
        m = jnp.maximum(m, sink_ref[0])
    p = jnp.exp(s - m)
    den = jnp.sum(p, axis=1, keepdims=True)
    if use_sink:
        den = den + jnp.exp(sink_ref[0] - m)
    o = jnp.dot(p.astype(BF16), v_ref[0, 0], preferred_element_type=F32) / den
    o_ref[0] = o.reshape(g, lq, dh).astype(o_ref.dtype)


def _ctx_attn(q, k, v, sink_col, use_sink):
    b, nh, lq, dh = q.shape
    hkv = k.shape[1]
    g = nh // hkv
    qblk = pl.BlockSpec((1, g, lq, dh), lambda bi, hi: (bi, hi, 0, 0))
    kblk = pl.BlockSpec((1, 1, k.shape[2], dh), lambda bi, hi: (bi, hi, 0, 0))
    return pl.pallas_call(
        functools.partial(_ctx_attn_kernel, use_sink=use_sink),
        out_shape=jax.ShapeDtypeStruct(q.shape, BF16),
        grid=(b, hkv),
        in_specs=[qblk, kblk, kblk, pl.BlockSpec((1, g * lq, 1), lambda bi, hi: (hi, 0, 0))],
        out_specs=qblk,
        compiler_params=_params("parallel", "parallel"),
        name="context_attention",
    )(q, k, v, sink_col)


def _flash_kernel(q_ref, k_ref, v_ref, o_ref, m_scr, l_scr, acc_scr, sa_scr, sb_scr, *, tk):
    dh = HEAD_DIM
    g = q_ref.shape[0] // dh
    n = k_ref.shape[2] // tk
    m_scr[...] = jnp.full_like(m_scr, NEG_INF)
    l_scr[...] = jnp.zeros_like(l_scr)
    acc_scr[...] = jnp.zeros_like(acc_scr)

    def scores(j, dst):
        kk = k_ref[0, 0, pl.ds(pl.multiple_of(j * tk, tk), tk), :]
        for h in range(g):
            dst[h] = jnp.dot(kk, q_ref[h * dh:(h + 1) * dh, :], preferred_element_type=F32)

    def update(j, src):
        vv = v_ref[0, 0, :, pl.ds(pl.multiple_of(j * tk, tk), tk)]
        ss = [src[h] for h in range(g)]
        m_olds = [m_scr[h] for h in range(g)]
        m_news = [jnp.maximum(m_olds[h], jnp.max(ss[h], axis=0, keepdims=True)) for h in range(g)]
        ps = [jnp.exp2(ss[h] - m_news[h]) for h in range(g)]
        pvs = [jnp.dot(vv, ps[h].astype(BF16), preferred_element_type=F32) for h in range(g)]
        for h in range(g):
            alpha = jnp.exp2(m_olds[h] - m_news[h])
            l_scr[h] = alpha * l_scr[h] + jnp.sum(ps[h], axis=0, keepdims=True)
            acc_scr[h] = alpha * acc_scr[h] + pvs[h]
            m_scr[h] = m_news[h]

    scores(0, sa_scr)

    def body(i, carry):
        scores(2 * i + 1, sb_scr)
        update(2 * i, sa_scr)
        scores(2 * i + 2, sa_scr)
        update(2 * i + 1, sb_scr)
        return carry

    lax.fori_loop(0, (n - 1) // 2, body, 0)
    if n % 2 == 1:
        update(n - 1, sa_scr)
    else:
        scores(n - 1, sb_scr)
        update(n - 2, sa_scr)
        update(n - 1, sb_scr)
    out = (acc_scr[...] / l_scr[...]).reshape(g * dh, -1)
    o_ref[...] = out.T.astype(o_ref.dtype)


def _flash(qt, k, vt, b, s, tq=None, tk=None):
    w = qt.shape[0]
    hkv, t, dh = k.shape[1], k.shape[2], k.shape[3]
    g = w // dh // hkv
    tq = tq or _tile(s, (256, 128))
    tk = tk or _tile(t, (256, 128))
    nq = s // tq
    return pl.pallas_call(
        functools.partial(_flash_kernel, tk=tk),
        out_shape=jax.ShapeDtypeStruct((b * s, w), BF16),
        grid=(b, hkv, nq),
        in_specs=[pl.BlockSpec((g * dh, tq), lambda bi, hi, i: (hi, bi * nq + i)),
                  pl.BlockSpec((1, 1, t, dh), lambda bi, hi, i: (bi, hi, 0, 0)),
                  pl.BlockSpec((1, 1, dh, t), lambda bi, hi, i: (bi, hi, 0, 0))],
        out_specs=pl.BlockSpec((tq, g * dh), lambda bi, hi, i: (bi * nq + i, hi)),
        scratch_shapes=[pltpu.VMEM((g, 1, tq), F32), pltpu.VMEM((g, 1, tq), F32), pltpu.VMEM((g, dh, tq), F32),
                        pltpu.VMEM((g, tk, tq), F32), pltpu.VMEM((g, tk, tq), F32)],
        compiler_params=_params("parallel", "parallel", "arbitrary"),
        name="dense_attention",
    )(qt, k, vt)


def _bmm_kernel(a_ref, b_ref, o_ref):
    o_ref[0] = jnp.dot(a_ref[0], b_ref[0], preferred_element_type=F32).astype(o_ref.dtype)


def _group_matmul(a, bmat, name):
    g, m, k = a.shape
    n = bmat.shape[2]
    tm = _tile(m, (1024, 512, 256, 128, 64, 32, 16, 8))
    return pl.pallas_call(
        _bmm_kernel,
        out_shape=jax.ShapeDtypeStruct((g, m, n), F32),
        grid=(g, m // tm),
        in_specs=[pl.BlockSpec((1, tm, k), lambda gi, i: (gi, i, 0)),
                  pl.BlockSpec((1, k, n), lambda gi, i: (gi, 0, 0))],
        out_specs=pl.BlockSpec((1, tm, n), lambda gi, i: (gi, i, 0)),
        compiler_params=_params("parallel", "parallel"),
        name=name,
    )(a, bmat)


def _s5_scan_kernel(sre_ref, sim_ref, are_ref, aim_ref, zre_ref, zim_ref):
    nd, nsteps = sre_ref.shape[0], sre_ref.shape[1]
    for d in range(nd):
        a_re, a_im = are_ref[d], aim_ref[d]

        def body(i, carry, d=d, a_re=a_re, a_im=a_im):
            z_re, z_im = carry
            zre_ref[d, i] = z_re
            zim_ref[d, i] = z_im
            return (a_re * z_re - a_im * z_im + sre_ref[d, i], a_re * z_im + a_im * z_re + sim_ref[d, i])

        zero = jnp.zeros(sre_ref.shape[2:], F32)
        lax.fori_loop(0, nsteps, body, (zero, zero))


def _s5_scan(s_re, s_im, a_re, a_im):
    full = lambda arr: pl.BlockSpec(arr.shape, lambda i: (0,) * arr.ndim)
    return pl.pallas_call(
        _s5_scan_kernel,
        out_shape=[jax.ShapeDtypeStruct(s_re.shape, F32)] * 2,
        grid=(1,),
        in_specs=[full(s_re), full(s_im), full(a_re), full(a_im)],
        out_specs=[full(s_re), full(s_im)],
        compiler_params=_params("arbitrary"),
        name="s5_chunk_scan",
    )(s_re, s_im, a_re, a_im)


def _s5_glu_kernel(y_ref, u_ref, d_ref, w_ref, b_ref, o_ref):
    y = y_ref[...] + d_ref[...] * u_ref[...]
    y = jax.nn.gelu(y)
    gate = jnp.dot(y.astype(BF16), w_ref[...], preferred_element_type=F32) + b_ref[...]
    o_ref[...] = (y * jax.nn.sigmoid(gate)).astype(o_ref.dtype)


def _s5_glu(y, u, d_skip, w_glu, b_glu, tm):
    r, c = y.shape
    blk = pl.BlockSpec((tm, c), lambda i: (i, 0))
    vec = pl.BlockSpec((1, c), lambda i: (0, 0))
    return pl.pallas_call(
        _s5_glu_kernel,
        out_shape=jax.ShapeDtypeStruct((r, c), BF16),
        grid=(r // tm,),
        in_specs=[blk, blk, vec, pl.BlockSpec((c, c), lambda i: (0, 0)), vec],
        out_specs=blk,
        compiler_params=_params("parallel"),
        name="s5_readout_glu",
    )(y, u, d_skip.reshape(1, c), w_glu.astype(BF16), b_glu.reshape(1, c))


def _s5_tables(lam_re, lam_im, log_dt, b_re, b_im, c_re, c_im):
    ln, p, gc = S5_CHUNK, S5_STATE, S5_GROUP
    lam = lax.complex(lam_re.astype(F32), lam_im.astype(F32))
    dt = jnp.exp(log_dt.astype(F32))[..., None]
    a_bar = jnp.exp(lam * dt)
    b_scale = (a_bar - 1.0) / lam
    b_mat = lax.complex(b_re.astype(F32), b_im.astype(F32))
    c_mat = lax.complex(c_re.astype(F32), c_im.astype(F32))
    tau = jnp.arange(ln + 1, dtype=F32)
    apow = jnp.exp((lam * dt)[:, :, None, :] * tau[None, None, :, None])
    drive = b_scale[..., None] * b_mat[None]
    kern = jnp.real(jnp.einsum('gcp,dgtp,dgpe->dgtce', c_mat, apow[:, :, :ln], drive))
    kc = jnp.concatenate([kern[1, :, :0:-1], (kern[0, :, :1] + kern[1, :, :1]), kern[0, :, 1:]], axis=1)
    kc_e = kc.transpose(0, 3, 1, 2).reshape(S5_GROUPS, gc, (2 * ln - 1) * gc)
    toep = jnp.stack([kc_e[:, :, (ln - 1 - s) * gc:(2 * ln - 1 - s) * gc] for s in range(ln)], axis=1)
    toep = toep.reshape(S5_GROUPS, ln * gc, ln * gc)
    w_f = apow[0, :, ln - 1::-1][:, :ln, :, None] * drive[0][:, None]
    w_b = apow[1, :, :ln, :, None] * drive[1][:, None]
    def m_in(wc):
        wt = wc.transpose(0, 1, 3, 2).reshape(S5_GROUPS, ln * gc, p)
        return jnp.concatenate([jnp.real(wt), jnp.imag(wt)], axis=-1)
    min_all = jnp.concatenate([m_in(w_f), m_in(w_b)], axis=-1)
    o_f = c_mat[:, None] * apow[0, :, 1:ln + 1][:, :, None, :]
    o_b = c_mat[:, None] * apow[1, :, ln:0:-1][:, :, None, :]
    def m_out(oc):
        ot = oc.transpose(0, 3, 1, 2).reshape(S5_GROUPS, p, ln * gc)
        return jnp.concatenate([jnp.real(ot), -jnp.imag(ot)], axis=1)
    rhs = jnp.concatenate([toep, m_out(o_f), m_out(o_b)], axis=1)
    a_l = apow[:, :, ln].reshape(2, 1, S5_GROUPS * p)
    return min_all.astype(BF16), rhs.astype(BF16), jnp.real(a_l), jnp.imag(a_l)


def _s5_mixer(u_x, u_c, tables, d_skip, w_glu, b_glu):
    min_all, rhs, a_re, a_im = tables
    b, s, _ = u_x.shape
    ln, p, gc, ng = S5_CHUNK, S5_STATE, S5_GROUP, S5_GROUPS

    def chunks(u):
        nc = u.shape[1] // ln
        return u.reshape(b, nc, ln, ng, gc).transpose(3, 0, 1, 2, 4).reshape(ng, b * nc, ln * gc).astype(BF16), nc

    ux, ncx = chunks(u_x)
    uc, ncc = chunks(u_c)
    sx = _group_matmul(ux, min_all, "s5_local_state").reshape(ng, b, ncx, 4, p)
    sc = _group_matmul(uc, min_all, "s5_local_state_ctx").reshape(ng, b, ncc, 4, p)

    def scan_order(part):
        f = jnp.concatenate([sc[:, :, :, part], sx[:, :, :, part]], axis=2)
        r = jnp.concatenate([sc[:, :, ::-1, part + 2], sx[:, :, ::-1, part + 2]], axis=2)
        return jnp.stack([f, r]).transpose(0, 3, 2, 1, 4).reshape(2, ncc + ncx, b, ng * p)

    z_re, z_im = _s5_scan(scan_order(0), scan_order(1), a_re, a_im)

    def latent(z, d):
        zl = z[d, ncc:]
        if d == 1:
            zl = zl[::-1]
        return zl.reshape(ncx, b, ng, p).transpose(2, 1, 0, 3).reshape(ng, b * ncx, p)

    lhs = jnp.concatenate([ux, latent(z_re, 0).astype(BF16), latent(z_im, 0).astype(BF16),
                           latent(z_re, 1).astype(BF16), latent(z_im, 1).astype(BF16)], axis=-1)
    y = _group_matmul(lhs, rhs, "s5_outputs")
    y = y.reshape(ng, b, ncx, ln, gc).transpose(1, 2, 3, 0, 4).reshape(b * s, S5_CHANNELS)
    return _s5_glu(y, u_x.reshape(b * s, S5_CHANNELS), d_skip, w_glu, b_glu, _tile(b * s, (1024, 512, 256)))


def _out_proj_kernel(x_ref, a_ref, b_ref, wa_ref, wb_ref, gt_ref, g_ref, sc_ref, sh_ref, wr_ref, br_ref,
                     xo_ref, xn_ref, lg_ref):
    y = (jnp.dot(a_ref[...], wa_ref[...], preferred_element_type=F32)
         + jnp.dot(b_ref[...], wb_ref[...], preferred_element_type=F32))
    x = x_ref[...] + gt_ref[0] * y
    xo_ref[...] = x
    xn = x * lax.rsqrt(jnp.mean(x * x, axis=-1, keepdims=True) + EPS) * g_ref[...]
    xn = xn * (1.0 + sc_ref[0]) + sh_ref[0]
    hi = xn.astype(BF16)
    xn_ref[...] = hi
    lo = (xn - hi.astype(F32)).astype(BF16)
    part = jnp.dot(hi, wr_ref[...], preferred_element_type=F32)
    part = part[:, :LANES] + part[:, LANES:] + jnp.dot(lo, wr_ref[:, :LANES], preferred_element_type=F32)
    lg_ref[...] = part[:, :lg_ref.shape[1]] + br_ref[...]


def _out_proj(x, row0, a, bmix, wa, wb, gate, g_ffn, scale, shift, mod_map, w_router, b_router, tm):
    r = a.shape[0]
    d = x.shape[1]
    t0 = row0 // tm
    ne = w_router.shape[1]
    w_hi = w_router.astype(BF16)
    w_lo = (w_router - w_hi.astype(F32)).astype(BF16)
    w_router = jnp.concatenate([_pad_cols(w_hi), _pad_cols(w_lo)], axis=1)
    row = lambda w: pl.BlockSpec((tm, w), lambda i: (i, 0))
    mod = pl.BlockSpec((1, 1, d), lambda i: (mod_map(i), 0, 0))
    full = lambda arr: pl.BlockSpec(arr.shape, lambda i: (0, 0))
    return pl.pallas_call(
        _out_proj_kernel,
        out_shape=[jax.ShapeDtypeStruct((r, d), F32), jax.ShapeDtypeStruct((r, d), BF16),
                   jax.ShapeDtypeStruct((r, ne), F32)],
        grid=(r // tm,),
        in_specs=[pl.BlockSpec((tm, d), lambda i: (i + t0, 0)), row(a.shape[1]), row(bmix.shape[1]),
                  full(wa), full(wb), mod, pl.BlockSpec((1, d), lambda i: (0, 0)), mod, mod,
                  full(w_router), pl.BlockSpec((1, ne), lambda i: (0, 0))],
        out_specs=[row(d), row(d), row(ne)],
        compiler_params=_params("parallel"),
        name="out_proj",
    )(x, a, bmix, wa, wb, gate, g_ffn.reshape(1, d), scale, shift, w_router, b_router.reshape(1, ne))


def _route_kernel(lg_ref, idx_ref, gate_ref, pos_ref, cnt_ref, carry):
    tm, ne = lg_ref.shape

    @pl.when(pl.program_id(0) == 0)
    def _():
        carry[...] = jnp.zeros_like(carry)

    work = lg_ref[...]
    lane = lax.broadcasted_iota(I32, (tm, ne), 1).astype(F32)
    out_lane = lax.broadcasted_iota(I32, (tm, LANES), 1)
    vals, hots, idx_out = [], [], jnp.zeros((tm, LANES), I32)
    for kk in range(TOP_K):
        mx = jnp.max(work, axis=1, keepdims=True)
        idx = jnp.min(jnp.where(work == mx, lane, float(ne)), axis=1, keepdims=True)
        hot = lane == idx
        work = jnp.where(hot, -jnp.inf, work)
        vals.append(mx)
        hots.append(hot)
        idx_out = jnp.where(out_lane == kk, idx.astype(I32), idx_out)
    exps = [jnp.exp(vv - vals[0]) for vv in vals]
    tot = exps[0] + exps[1] + exps[2] + exps[3]
    multi = (hots[0] | hots[1] | hots[2] | hots[3]).astype(BF16)
    r_i = lax.broadcasted_iota(I32, (tm, tm), 0)
    c_i = lax.broadcasted_iota(I32, (tm, tm), 1)
    before = (r_i > c_i).astype(BF16)
    prefix = jnp.dot(before, multi, preferred_element_type=F32) + carry[...]
    gate_out = jnp.zeros((tm, LANES), F32)
    pos_out = jnp.zeros((tm, LANES), I32)
    for kk in range(TOP_K):
        gate_out = jnp.where(out_lane == kk, exps[kk] / tot, gate_out)
        pos = jnp.sum(jnp.where(hots[kk], prefix, 0.0), axis=1, keepdims=True).astype(I32)
        pos_out = jnp.where(out_lane == kk, pos, pos_out)
    idx_ref[...] = idx_out
    gate_ref[...] = gate_out
    pos_ref[...] = pos_out
    carry[...] = carry[...] + jnp.sum(multi.astype(F32), axis=0, keepdims=True)
    cnt_ref[...] = carry[...]


def _route(logits, tm, row0, n):
    ne = logits.shape[1]
    t0 = row0 // tm
    wide = pl.BlockSpec((tm, LANES), lambda i: (i, 0))
    return pl.pallas_call(
        _route_kernel,
        out_shape=[jax.ShapeDtypeStruct((n, LANES), I32), jax.ShapeDtypeStruct((n, LANES), F32),
                   jax.ShapeDtypeStruct((n, LANES), I32), jax.ShapeDtypeStruct((1, ne), F32)],
        grid=(n // tm,),
        in_specs=[pl.BlockSpec((tm, ne), lambda i: (i + t0, 0))],
        out_specs=[wide, wide, wide, pl.BlockSpec((1, ne), lambda i: (0, 0))],
        scratch_shapes=[pltpu.VMEM((1, ne), F32)],
        compiler_params=_params("arbitrary"),
        name="moe_route",
    )(logits)


def _expert_kernel(be_ref, nb_ref, x_ref, wgu_ref, bgu_ref, wd_ref, bd_ref, perm_ref, o_ref, wgu_bf, wd_bf):
    i = pl.program_id(0)
    fresh = jnp.logical_or(i == 0, be_ref[i] != be_ref[jnp.maximum(i - 1, 0)])

    @pl.when(jnp.logical_and(i < nb_ref[0], fresh))
    def _():
        tile = perm_ref.shape[0]
        for t in range(wgu_ref.shape[2] // tile):
            sl = slice(t * tile, (t + 1) * tile)
            wgu_bf[:, sl] = jnp.dot(wgu_ref[0, :, sl].astype(BF16), perm_ref[...],
                                    preferred_element_type=F32).astype(BF16)
        wd_bf[...] = wd_ref[0].astype(BF16)

    @pl.when(i < nb_ref[0])
    def _():
        h = jnp.dot(x_ref[...], wgu_bf[...], preferred_element_type=F32) + bgu_ref[0]
        acts = []
        for t in range(h.shape[1] // (2 * LANES)):
            glu = jnp.minimum(h[:, 2 * LANES * t:2 * LANES * t + LANES], SWIGLU_LIMIT)
            lin = jnp.clip(h[:, 2 * LANES * t + LANES:2 * LANES * (t + 1)], -SWIGLU_LIMIT, SWIGLU_LIMIT)
            acts.append((glu * jax.nn.sigmoid(SWIGLU_ALPHA * glu) * (lin + 1.0)).astype(BF16))
        act = jnp.concatenate(acts, axis=1)
        o_ref[...] = (jnp.dot(act, wd_bf[...], preferred_element_type=F32) + bd_ref[0]).astype(o_ref.dtype)

    @pl.when(i >= nb_ref[0])
    def _():
        o_ref[...] = jnp.zeros_like(o_ref)


def _experts(xs, blk_e, n_used, wgu, bgu, wd, bd):
    n, d = xs.shape
    ff2 = wgu.shape[2]
    nblk = n // MOE_ROWS
    tile = 2 * LANES
    j = np.arange(tile)
    col = np.where(j < LANES, 2 * j, 2 * (j - LANES) + 1)
    perm = jnp.asarray(np.arange(tile)[:, None] == col[None, :], BF16)
    wspec = lambda shape: pl.BlockSpec((1,) + shape, lambda i, be, nb: (be[i], 0, 0))
    return pl.pallas_call(
        _expert_kernel,
        out_shape=jax.ShapeDtypeStruct((n, d), BF16),
        grid_spec=pltpu.PrefetchScalarGridSpec(
            num_scalar_prefetch=2,
            grid=(nblk,),
            in_specs=[pl.BlockSpec((MOE_ROWS, d), lambda i, be, nb: (i, 0)),
                      wspec((d, ff2)), wspec((1, ff2)), wspec((ff2 // 2, d)), wspec((1, d)),
                      pl.BlockSpec((tile, tile), lambda i, be, nb: (0, 0))],
            out_specs=pl.BlockSpec((MOE_ROWS, d), lambda i, be, nb: (i, 0)),
            scratch_shapes=[pltpu.VMEM((d, ff2), BF16), pltpu.VMEM((ff2 // 2, d), BF16)]),
        compiler_params=_params("arbitrary"),
        name="moe_experts",
    )(blk_e, n_used, xs, wgu, bgu, wd, bd, perm)


def _combine_kernel(x_ref, y_ref, gate_ref, gt_ref, g_ref, o_ref, *, final_norm):
    gates = gate_ref[...]
    y = gates[:, 0:1] * y_ref[0].astype(F32)
    for kk in range(1, TOP_K):
        y = y + gates[:, kk:kk + 1] * y_ref[kk].astype(F32)
    x = x_ref[...] + gt_ref[0] * y
    if final_norm:
        x = x * lax.rsqrt(jnp.mean(x * x, axis=-1, keepdims=True) + EPS) * g_ref[...]
    o_ref[...] = x


def _combine_alias_kernel(x_ref, y_ref, gate_ref, gt_ref, g_ref, prev_ref, o_ref, *, final_norm):
    del prev_ref
    _combine_kernel(x_ref, y_ref, gate_ref, gt_ref, g_ref, o_ref, final_norm=final_norm)


def _combine(x, y4, gates, gate_mod, mod_map, g_final, final_norm, tm, row0, prev):
    r, d = x.shape
    n = y4.shape[1]
    t0 = row0 // tm
    in_specs = [pl.BlockSpec((tm, d), lambda i: (i + t0, 0)),
                pl.BlockSpec((TOP_K, tm, d), lambda i: (0, i, 0)),
                pl.BlockSpec((tm, LANES), lambda i: (i, 0)),
                pl.BlockSpec((1, 1, d), lambda i: (mod_map(i + t0), 0, 0)),
                pl.BlockSpec((1, d), lambda i: (0, 0))]
    args = [x, y4, gates, gate_mod, g_final.reshape(1, d)]
    body, aliases = _combine_kernel, {}
    if prev is not None:
        in_specs.append(pl.BlockSpec(memory_space=pl.ANY))
        args.append(prev)
        body, aliases = _combine_alias_kernel, {len(args) - 1: 0}
    return pl.pallas_call(
        functools.partial(body, final_norm=final_norm),
        out_shape=jax.ShapeDtypeStruct((r, d), F32),
        grid=(n // tm,),
        in_specs=in_specs,
        out_specs=pl.BlockSpec((tm, d), lambda i: (i + t0, 0)),
        input_output_aliases=aliases,
        compiler_params=_params("parallel"),
        name="moe_combine",
    )(*args)


def _moe(x, xn, logits, gate_mod, mod_map, weights, g_final, final_norm, tm):
    wgu, bgu, wd, bd = weights
    r, d = xn.shape
    parts = MOE_PARTS if (r // tm) % MOE_PARTS == 0 else 1
    n = r // parts
    nblk = n * TOP_K // MOE_ROWS + N_EXPERTS
    blk_row0 = jnp.arange(nblk, dtype=I32) * MOE_ROWS
    routed = []
    for p in range(parts):
        idx_w, gates_w, pos_w, counts = _route(logits, tm, p * n, n)
        idx, pos = idx_w[:, :TOP_K], pos_w[:, :TOP_K]
        counts = counts[0].astype(I32)
        padded = (counts + MOE_ROWS - 1) // MOE_ROWS * MOE_ROWS
        pad_end = jnp.cumsum(padded)
        pad_start = pad_end - padded
        dest = pad_start[idx] + pos
        blk_e = jnp.minimum(jnp.sum((pad_end[None, :] <= blk_row0[:, None]).astype(I32), axis=1), N_EXPERTS - 1)
        n_used = (pad_end[-1:] // MOE_ROWS).astype(I32)
        key = idx * n + jnp.arange(n, dtype=I32)[:, None]
        tok_sorted = jnp.sort(key.reshape(-1)) % n
        grp_start = jnp.cumsum(counts) - counts
        slot = blk_row0[:, None] + jnp.arange(MOE_ROWS, dtype=I32)[None, :]
        compact = jnp.minimum(slot + (grp_start - pad_start)[blk_e][:, None], (grp_start + counts - 1)[blk_e][:, None])
        src = tok_sorted.at[jnp.clip(compact, 0, n * TOP_K - 1).reshape(-1)].get(mode="promise_in_bounds")
        xs = xn.at[src + p * n].get(mode="promise_in_bounds")
        routed.append((xs, blk_e, n_used, dest, gates_w))
    ys = [_experts(xs, blk_e, n_used, wgu, bgu, wd, bd) for xs, blk_e, n_used, _, _ in routed]
    out = None
    for p, (_, _, _, dest, gates_w) in enumerate(routed):
        y4 = ys[p].at[dest.T.reshape(-1)].get(mode="promise_in_bounds").reshape(TOP_K, n, d)
        out = _combine(x, y4, gates_w, gate_mod, mod_map, g_final, final_norm, tm, p * n, out)
    return out


def _moe_weights(w_gate_up, b_gate_up, w_down, b_down):
    ne, d, ff2 = w_gate_up.shape
    b_tiled = b_gate_up.reshape(ne, ff2 // (2 * LANES), LANES, 2).swapaxes(-1, -2).reshape(ne, 1, ff2)
    return w_gate_up, b_tiled, w_down, b_down[:, None, :]


def _rope_tables(n_tokens):
    rows = n_tokens // GRID_W
    row = jnp.repeat(jnp.arange(rows, dtype=I32), GRID_W).astype(F32)
    col = jnp.tile(jnp.arange(GRID_W, dtype=I32), rows).astype(F32)
    inv = ROPE_BASE ** (-jnp.arange(0, ROPE_AXIS_DIM, 2, dtype=F32) / ROPE_AXIS_DIM)
    ang_r, ang_c = row[:, None] * inv, col[:, None] * inv
    cos = jnp.concatenate([jnp.cos(ang_r)] * 2 + [jnp.cos(ang_c)] * 2, axis=-1)
    sin = jnp.concatenate([jnp.sin(ang_r)] * 2 + [jnp.sin(ang_c)] * 2, axis=-1)
    return cos, sin


def _rot_perm():
    q = ROPE_AXIS_DIM // 2
    d = np.arange(HEAD_DIM)
    first = (d % ROPE_AXIS_DIM) < q
    perm = np.where(first, d + q, d - q)
    sign = np.where(first, -1.0, 1.0).astype(np.float32)
    return perm, sign


def _rot_cols(w, n_heads):
    perm, sign = _rot_perm()
    k = w.shape[0]
    wh = w.reshape(k, n_heads, HEAD_DIM)
    return (wh[:, :, perm] * sign).reshape(k, n_heads * HEAD_DIM)


def _to_heads(t, b, n_heads):
    return t.reshape(b, -1, n_heads, HEAD_DIM).transpose(0, 2, 1, 3)


def _from_heads(t):
    b, h, tt, dh = t.shape
    return t.transpose(0, 2, 1, 3).reshape(b * tt, h * dh)


def _layer_even(xa, mod, b, s, lc, g_mix, g_ffn, w_in, w_out, gate_bias, mlstm_norm, sink, router, moe_w):
    d = xa.shape[1]
    nctx = b * lc
    tm = _tile(math.gcd(nctx, s), (512, 256, 128))
    mod_map = lambda i: jnp.where(i * tm < nctx, b, (i * tm - nctx) // s)
    sh1, sc1, gt1, sh2, sc2, gt2 = [mod[:, j][:, None, :] for j in range(6)]

    hm, hs, hk = MLSTM_HEADS * HEAD_DIM, SWA_HEADS * HEAD_DIM, SWA_KV_HEADS * HEAD_DIM
    o0 = np.cumsum([0, hm, hm, hm, hm, 4 * MLSTM_HEADS, hs, hk, hk])
    seg = lambda j: w_in[:, o0[j]:o0[j + 1]]
    w_cat = jnp.concatenate([seg(0), seg(1) * ATTN_SCALE, seg(2), seg(3),
                             seg(6), _rot_cols(seg(6), SWA_KV_HEADS), seg(7), _pad_cols(seg(4))], axis=1).astype(BF16)
    w_q = jnp.concatenate([seg(5), _rot_cols(seg(5), SWA_HEADS)], axis=1).T.astype(BF16)
    widths = [hm, hm, hm, hm, hk, hk, hk, LANES]
    dts = [BF16] * 7 + [F32]
    qa, ka, va, oa, kb, kb_r, vb, gts, qt, qt_r = _in_proj(xa, g_mix, sc1, sh1, mod_map, w_cat, widths, dts, tm,
                                                           wt=w_q, t_widths=[hs, hs], t_dtypes=[BF16, BF16])

    hf, hb = _mlstm(qa, ka, va, gts, _pad_cols(gate_bias.astype(F32).reshape(1, -1)), b, lc, s)
    mix_a = _mlstm_out(hf, hb, oa, mlstm_norm.reshape(-1), tm)

    cos, sin = _rope_tables(s)
    ones = jnp.ones((HEAD_DIM,), F32)
    g = SWA_HEADS // SWA_KV_HEADS
    heads = lambda t, n: _to_heads(t, b, n)
    log2e = math.log2(math.e)
    q_x = _q_prep_t(qt, qt_r, nctx, b * s, cos.T, sin.T, ones, ones, ATTN_SCALE * log2e, norm=False)
    k_x = _qk_prep(heads(kb[nctx:], SWA_KV_HEADS), heads(kb_r[nctx:], SWA_KV_HEADS), cos, sin, ones, ones,
                   norm=False, rope=True, scale=1.0)
    k_c, v_c, v_x = heads(kb[:nctx], SWA_KV_HEADS), heads(vb[:nctx], SWA_KV_HEADS), heads(vb[nctx:], SWA_KV_HEADS)
    q_c = heads(qt[:, :nctx].T, SWA_HEADS)
    sink_h = sink.astype(F32).reshape(SWA_KV_HEADS, g, 1)
    att_x = _swa(q_x, k_x, v_x.swapaxes(-1, -2), k_c, v_c.swapaxes(-1, -2),
                 jnp.broadcast_to(sink_h * log2e, (SWA_KV_HEADS, g, 2 * SWA_BLOCK)), b, s)
    att_c = _ctx_attn(q_c, k_c, v_c, jnp.broadcast_to(sink_h[..., None], (SWA_KV_HEADS, g, lc, 1))
                      .reshape(SWA_KV_HEADS, g * lc, 1), True)
    mix_b = jnp.concatenate([_from_heads(att_c), att_x], axis=0)

    w_router, b_router = router
    xa, xn2, logits = _out_proj(xa, 0, mix_a, mix_b, w_out[:hm].astype(BF16), w_out[hm:].astype(BF16), gt1, g_ffn,
                                sc2, sh2, mod_map, w_router.astype(F32), b_router.astype(F32), tm)
    return _moe(xa, xn2, logits, gt2, mod_map, moe_w, g_ffn, False, tm)


def _layer_odd_last(xa, mod, b, s, lc, g_mix, g_ffn, w_in, w_out, s5_params, d_skip, w_glu, b_glu,
                    q_norm, k_norm, router, moe_w, g_final):
    nctx = b * lc
    tm = _tile(math.gcd(nctx, s), (512, 256, 128))
    mod_map = lambda i: jnp.where(i * tm < nctx, b, (i * tm - nctx) // s)
    lat_map = lambda i: i * tm // s
    sh1, sc1, gt1, sh2, sc2, gt2 = [mod[:, j][:, None, :] for j in range(6)]

    hq, hk = ATT_HEADS * HEAD_DIM, ATT_KV_HEADS * HEAD_DIM
    o1 = np.cumsum([0, S5_CHANNELS, hq, hk, hk])
    seg = lambda j: w_in[:, o1[j]:o1[j + 1]]
    kpad = lambda w: _pad_cols(w, 2 * LANES)
    w_cat = jnp.concatenate([seg(0), kpad(seg(2)), kpad(_rot_cols(seg(2), ATT_KV_HEADS)), kpad(seg(3))],
                            axis=1).astype(BF16)
    w_q = jnp.concatenate([seg(1), _rot_cols(seg(1), ATT_HEADS)], axis=1).T.astype(BF16)
    widths = [S5_CHANNELS, 2 * LANES, 2 * LANES, 2 * LANES]
    dts = [F32, BF16, BF16, BF16]
    u, k, k_r, v, qt, qt_r = _in_proj(xa, g_mix, sc1, sh1, mod_map, w_cat, widths, dts, tm,
                                      wt=w_q, t_widths=[hq, hq], t_dtypes=[BF16, BF16])

    mix_a = _s5_mixer(u[nctx:].reshape(b, s, S5_CHANNELS), u[:nctx].reshape(b, lc, S5_CHANNELS),
                      _s5_tables(*s5_params), d_skip, w_glu, b_glu)

    cos, sin = _rope_tables(s)
    perm, _ = _rot_perm()
    qn, kn = q_norm.astype(F32), k_norm.astype(F32)
    heads = lambda t, n: _to_heads(t[:, :n * HEAD_DIM], b, n)
    q_x = _q_prep_t(qt, qt_r, nctx, b * s, cos.T, sin.T, qn, qn[perm], ATTN_SCALE * math.log2(math.e))
    k_x = _qk_prep(heads(k[nctx:], ATT_KV_HEADS), heads(k_r[nctx:], ATT_KV_HEADS), cos, sin, kn, kn[perm],
                   norm=True, rope=True, scale=1.0)
    kc_raw = heads(k[:nctx], ATT_KV_HEADS)
    k_c = _qk_prep(kc_raw, kc_raw, cos[:lc], sin[:lc], kn, kn, norm=True, rope=False, scale=1.0)
    k_all = jnp.concatenate([k_c, k_x], axis=2)
    v_all = jnp.concatenate([heads(v[:nctx], ATT_KV_HEADS), heads(v[nctx:], ATT_KV_HEADS)], axis=2)
    mix_b = _flash(q_x, k_all, v_all.swapaxes(-1, -2), b, s)

    w_router, b_router = router
    hs = S5_CHANNELS
    x, xn2, logits = _out_proj(xa, nctx, mix_a, mix_b, w_out[:hs].astype(BF16), w_out[hs:].astype(BF16), gt1, g_ffn,
                               sc2, sh2, lat_map, w_router.astype(F32), b_router.astype(F32), tm)
    return _moe(x, xn2, logits, gt2, lat_map, moe_w, g_final, True, tm)


def kernel(x, c, ctx, c_ctx, l0_w_mod, l0_b_mod, l0_g_mix, l0_g_ffn, l0_w_in, l0_w_out, l0_gate_bias, l0_mlstm_norm, l0_sink, l0_w_router, l0_b_router, l0_w_gate_up, l0_b_gate_up, l0_w_down, l0_b_down, l1_w_mod, l1_b_mod, l1_g_mix, l1_g_ffn, l1_w_in, l1_w_out, l1_lam_re, l1_lam_im, l1_log_dt, l1_b_re, l1_b_im, l1_c_re, l1_c_im, l1_d_skip, l1_w_glu, l1_b_glu, l1_q_norm, l1_k_norm, l1_w_router, l1_b_router, l1_w_gate_up, l1_b_gate_up, l1_w_down, l1_b_down, g_final):
    b, s, d = x.shape
    lc = ctx.shape[1]
    cond = jnp.concatenate([c, c_ctx[None, :]], axis=0)
    cond = jnp.pad(cond, ((0, (-(b + 1)) % 8), (0, 0)))
    mod0 = _silu_linear(cond, l0_w_mod, l0_b_mod)[:b + 1].reshape(b + 1, 6, d)
    mod1 = _silu_linear(cond, l1_w_mod, l1_b_mod)[:b + 1].reshape(b + 1, 6, d)

    moe0 = _moe_weights(l0_w_gate_up, l0_b_gate_up, l0_w_down, l0_b_down)
    moe1 = _moe_weights(l1_w_gate_up, l1_b_gate_up, l1_w_down, l1_b_down)
    groups = BATCH_GROUPS if b % BATCH_GROUPS == 0 else 1
    bg = b // groups
    outs = []
    for gi in range(groups):
        sl = slice(gi * bg, (gi + 1) * bg)
        mod0_g = jnp.concatenate([mod0[sl], mod0[b:b + 1]], axis=0)
        mod1_g = jnp.concatenate([mod1[sl], mod1[b:b + 1]], axis=0)
        xa = jnp.concatenate([ctx[sl].reshape(bg * lc, d), x[sl].reshape(bg * s, d)], axis=0)
        xa = _layer_even(xa, mod0_g, bg, s, lc, l0_g_mix, l0_g_ffn, l0_w_in, l0_w_out, l0_gate_bias,
                         l0_mlstm_norm, l0_sink, (l0_w_router, l0_b_router), moe0)
        out = _layer_odd_last(xa, mod1_g, bg, s, lc, l1_g_mix, l1_g_ffn, l1_w_in, l1_w_out,
                              (l1_lam_re, l1_lam_im, l1_log_dt, l1_b_re, l1_b_im, l1_c_re, l1_c_im),
                              l1_d_skip, l1_w_glu, l1_b_glu, l1_q_norm, l1_k_norm, (l1_w_router, l1_b_router),
                              moe1, g_final)
        outs.append(out.reshape(bg, s, d))
    return jnp.concatenate(outs, axis=0)
```

```python
import functools
import math

import jax
import jax.numpy as jnp
import numpy as np
from jax import lax
from jax.experimental import pallas as pl
from jax.experimental.pallas import tpu as pltpu

F32 = jnp.float32
BF16 = jnp.bfloat16
I32 = jnp.int32

GRID_W = 64
HEAD_DIM = 64
ATTN_SCALE = HEAD_DIM ** -0.5
ROPE_AXIS_DIM = HEAD_DIM // 2
ROPE_BASE = 10000.0
EPS = 1e-6
NEG_INF = -1e30

MLSTM_HEADS = 8
MLSTM_CHUNKS = (256, 128, 64)
SWA_HEADS = 8
SWA_KV_HEADS = 2
SWA_WINDOW = 128
SWA_BLOCK = 128
S5_CHANNELS = 256
S5_GROUP = 16
S5_GROUPS = S5_CHANNELS // S5_GROUP
S5_STATE = 64
S5_CHUNK = 64
ATT_HEADS = 12
ATT_KV_HEADS = 3
N_EXPERTS = 32
TOP_K = 4
SWIGLU_LIMIT = 7.0
SWIGLU_ALPHA = 1.702

LANES = 128
VMEM_LIMIT = 56 * 1024 * 1024
MOE_ROWS = 512
BATCH_GROUPS = 2
HIGHEST = lax.Precision.HIGHEST


def _params(*sem):
    return pltpu.CompilerParams(dimension_semantics=sem, vmem_limit_bytes=VMEM_LIMIT)


def _tile(n, prefs):
    for t in prefs:
        if n % t == 0:
            return t
    return n


def _pad_cols(w, mult=LANES):
    pad = (-w.shape[-1]) % mult
    if pad:
        w = jnp.pad(w, [(0, 0)] * (w.ndim - 1) + [(0, pad)])
    return w


def _linear_kernel(x_ref, w_ref, b_ref, o_ref):
    x = x_ref[...]
    x = x * jax.nn.sigmoid(x)
    o_ref[...] = jnp.dot(x, w_ref[...], precision=HIGHEST, preferred_element_type=F32) + b_ref[...]


def _silu_linear(x, w, b):
    m, k = x.shape
    n = w.shape[1]
    tn = _tile(n, (1024, 512, 256, 128))
    return pl.pallas_call(
        _linear_kernel,
        out_shape=jax.ShapeDtypeStruct((m, n), F32),
        grid=(n // tn,),
        in_specs=[pl.BlockSpec((m, k), lambda j: (0, 0)),
                  pl.BlockSpec((k, tn), lambda j: (0, j)),
                  pl.BlockSpec((1, tn), lambda j: (0, j))],
        out_specs=pl.BlockSpec((m, tn), lambda j: (0, j)),
        compiler_params=_params("arbitrary"),
        name="adaln_linear",
    )(x, w, b.reshape(1, n))


def _in_proj_kernel(x_ref, g_ref, sc_ref, sh_ref, w_ref, wt_ref, *out_refs, widths, t_widths):
    x = x_ref[...]
    xn = x * lax.rsqrt(jnp.mean(x * x, axis=-1, keepdims=True) + EPS) * g_ref[...]
    xb = (xn * (1.0 + sc_ref[0]) + sh_ref[0]).astype(BF16)
    off = 0
    for o_ref, w in zip(out_refs, widths):
        o_ref[...] = jnp.dot(xb, w_ref[:, off:off + w], preferred_element_type=F32).astype(o_ref.dtype)
        off += w
    off = 0
    for o_ref, w in zip(out_refs[len(widths):], t_widths):
        o_ref[...] = lax.dot_general(wt_ref[off:off + w, :], xb, (((1,), (1,)), ((), ())),
                                     preferred_element_type=F32).astype(o_ref.dtype)
        off += w


def _in_proj(x, g, scale, shift, mod_map, w, widths, dtypes, tm, wt=None, t_widths=(), t_dtypes=()):
    r, d = x.shape
    if wt is None:
        wt = jnp.zeros((8, d), BF16)
    return pl.pallas_call(
        functools.partial(_in_proj_kernel, widths=tuple(widths), t_widths=tuple(t_widths)),
        out_shape=([jax.ShapeDtypeStruct((r, wd), dt) for wd, dt in zip(widths, dtypes)]
                   + [jax.ShapeDtypeStruct((wd, r), dt) for wd, dt in zip(t_widths, t_dtypes)]),
        grid=(r // tm,),
        in_specs=[pl.BlockSpec((tm, d), lambda i: (i, 0)),
                  pl.BlockSpec((1, d), lambda i: (0, 0)),
                  pl.BlockSpec((1, 1, d), lambda i: (mod_map(i), 0, 0)),
                  pl.BlockSpec((1, 1, d), lambda i: (mod_map(i), 0, 0)),
                  pl.BlockSpec(w.shape, lambda i: (0, 0)),
                  pl.BlockSpec(wt.shape, lambda i: (0, 0))],
        out_specs=([pl.BlockSpec((tm, wd), lambda i: (i, 0)) for wd in widths]
                   + [pl.BlockSpec((wd, tm), lambda i: (0, i)) for wd in t_widths]),
        compiler_params=_params("parallel"),
        name="in_proj",
    )(x, g.reshape(1, d), scale, shift, w, wt)


def _qk_prep_kernel(x_ref, xr_ref, cos_ref, sin_ref, g_ref, gr_ref, o_ref, *, norm, rope, scale):
    x = x_ref[0].astype(F32)
    if norm:
        s = lax.rsqrt(jnp.mean(x * x, axis=-1, keepdims=True) + EPS)
        x = x * s * g_ref[...]
    if rope:
        xr = xr_ref[0].astype(F32)
        if norm:
            xr = xr * s * gr_ref[...]
        x = x * cos_ref[...] + xr * sin_ref[...]
    o_ref[0] = (x * scale).astype(o_ref.dtype)


def _qk_prep(x, x_rot, cos, sin, gain, gain_rot, *, norm, rope, scale):
    b, h, t, dh = x.shape
    tt = _tile(t, (512, 256, 128))
    blk = pl.BlockSpec((1, h, tt, dh), lambda i, j: (i, 0, j, 0))
    tab = pl.BlockSpec((tt, dh), lambda i, j: (j, 0))
    vec = pl.BlockSpec((1, dh), lambda i, j: (0, 0))
    return pl.pallas_call(
        functools.partial(_qk_prep_kernel, norm=norm, rope=rope, scale=scale),
        out_shape=jax.ShapeDtypeStruct(x.shape, BF16),
        grid=(b, t // tt),
        in_specs=[blk, blk, tab, tab, vec, vec],
        out_specs=blk,
        compiler_params=_params("parallel", "parallel"),
        name="qk_prep",
    )(x, x_rot, cos, sin, gain.reshape(1, dh), gain_rot.reshape(1, dh))


def _q_prep_t_kernel(x_ref, xr_ref, cos_ref, sin_ref, g_ref, gr_ref, o_ref, *, norm, scale):
    w, tt = x_ref.shape
    nh = w // HEAD_DIM
    x = x_ref[...].astype(F32).reshape(nh, HEAD_DIM, tt)
    xr = xr_ref[...].astype(F32).reshape(nh, HEAD_DIM, tt)
    if norm:
        s = lax.rsqrt(jnp.mean(x * x, axis=1, keepdims=True) + EPS)
        x, xr = x * s * g_ref[...], xr * s * gr_ref[...]
    y = x * cos_ref[...] + xr * sin_ref[...]
    o_ref[...] = (y * scale).reshape(w, tt).astype(o_ref.dtype)


def _q_prep_t(xt, xt_rot, col0, n_cols, cos_t, sin_t, gain, gain_rot, scale, norm=True):
    w = xt.shape[0]
    s = cos_t.shape[1]
    tt = _tile(math.gcd(col0, s), (512, 256, 128))
    blk = pl.BlockSpec((w, tt), lambda j: (0, j + col0 // tt))
    tab = pl.BlockSpec((HEAD_DIM, tt), lambda j: (0, j % (s // tt)))
    vec = pl.BlockSpec((HEAD_DIM, 1), lambda j: (0, 0))
    return pl.pallas_call(
        functools.partial(_q_prep_t_kernel, norm=norm, scale=scale),
        out_shape=jax.ShapeDtypeStruct((w, n_cols), BF16),
        grid=(n_cols // tt,),
        in_specs=[blk, blk, tab, tab, vec, vec],
        out_specs=pl.BlockSpec((w, tt), lambda j: (0, j)),
        compiler_params=_params("parallel"),
        name="q_prep_t",
    )(xt, xt_rot, cos_t, sin_t, gain.reshape(HEAD_DIM, 1), gain_rot.reshape(HEAD_DIM, 1))


def _log_sigmoid(x):
    return jnp.minimum(x, 0.0) - jnp.log(1.0 + jnp.exp(-jnp.abs(x)))


def _mlstm_kernel(qf, kf, vf, gf, qb, kb, vb, gb, bias_ref, hf_ref, hb_ref, c_scr, n_scr, m_scr):
    ch, nh, dh = qf.shape[0], MLSTM_HEADS, HEAD_DIM

    @pl.when(pl.program_id(1) == 0)
    def _():
        c_scr[...] = jnp.zeros_like(c_scr)
        n_scr[...] = jnp.zeros_like(n_scr)
        m_scr[...] = jnp.full_like(m_scr, NEG_INF)

    row = lax.broadcasted_iota(I32, (ch, ch), 0)
    col = lax.broadcasted_iota(I32, (ch, ch), 1)
    lane_lo = lax.broadcasted_iota(I32, (ch, 2 * dh), 1) < dh
    row_lo = lax.broadcasted_iota(I32, (2 * dh, 2 * dh), 0) < dh
    col_lo = lax.broadcasted_iota(I32, (2 * dh, 2 * dh), 1) < dh
    vec_lo = lax.broadcasted_iota(I32, (1, 2 * dh), 1) < dh
    pair = lambda a, b: jnp.where(lane_lo, a, b)
    refs = ((qf, kf, vf, gf, hf_ref), (qb, kb, vb, gb, hb_ref))
    jobs = [(d, p) for d in range(2) for p in range(nh // 2)]
    sl = lambda p: slice(2 * dh * p, 2 * dh * (p + 1))

    qk = {}
    for d, p in jobs:
        q2, k2 = refs[d][0][:, sl(p)], refs[d][1][:, sl(p)]
        zero = jnp.zeros_like(q2)
        q_st = jnp.concatenate([jnp.where(lane_lo, q2, zero), jnp.where(lane_lo, zero, q2)], axis=0)
        qk[d, p] = lax.dot_general(q_st, k2, (((1,), (1,)), ((), ())), preferred_element_type=F32)

    st = {}
    for d in range(2):
        seen = (col <= row) if d == 0 else (col >= row)
        tri = seen.astype(F32)
        tri_t = ((row <= col) if d == 0 else (row >= col)).astype(F32)
        last = ch - 1 if d == 0 else 0
        g = refs[d][3][...] + bias_ref[...]
        g_t = g.T
        lo = 2 * nh * d
        li_col, lf_col = g[:, lo:lo + nh], _log_sigmoid(g[:, lo + nh:lo + 2 * nh])
        li_row, lf_row = g_t[lo:lo + nh, :], _log_sigmoid(g_t[lo + nh:lo + 2 * nh, :])
        b_col = jnp.dot(tri, lf_col, precision=HIGHEST, preferred_element_type=F32)
        b_row = jnp.dot(lf_row, tri_t, precision=HIGHEST, preferred_element_type=F32)
        for h in range(nh):
            bc, br = b_col[:, h:h + 1], b_row[h:h + 1, :]
            lir, lic = li_row[h:h + 1, :], li_col[:, h:h + 1]
            b_last = br[:, last:last + 1]
            m_prev = m_scr[d, h // 2][:, (h % 2) * dh:(h % 2) * dh + 1]
            d_log = jnp.where(seen, bc - br + lir, NEG_INF)
            inter_log = bc + m_prev
            m_t = jnp.maximum(inter_log, jnp.max(d_log, axis=1, keepdims=True))
            w_log = b_last - bc + lic
            m_new = jnp.maximum(b_last + m_prev, jnp.max(w_log, axis=0, keepdims=True))
            st[d, h] = dict(dmat=jnp.exp(d_log - m_t), inter=jnp.exp(inter_log - m_t), floor=jnp.exp(-m_t),
                            m_new=m_new, decay=jnp.exp(b_last + m_prev - m_new), wn=jnp.exp(w_log - m_new))

    mm = {}
    for d, p in jobs:
        a, b = st[d, 2 * p], st[d, 2 * p + 1]
        q2, k2, v2 = refs[d][0][:, sl(p)], refs[d][1][:, sl(p)], refs[d][2][:, sl(p)]
        s = qk[d, p] * jnp.concatenate([a["dmat"], b["dmat"]], axis=0)
        kw = k2.astype(F32) * pair(a["wn"], b["wn"])
        mm[d, p] = dict(
            s_sum=jnp.sum(s, axis=1, keepdims=True), kw_sum=jnp.sum(kw, axis=0, keepdims=True),
            sv=jnp.dot(s.astype(BF16), v2, preferred_element_type=F32),
            q_c=jnp.dot(q2, c_scr[d, p].astype(BF16), preferred_element_type=F32),
            kv=lax.dot_general(kw.astype(BF16), v2, (((0,), (0,)), ((), ())), preferred_element_type=F32))

    for d, p in jobs:
        a, b, r = st[d, 2 * p], st[d, 2 * p + 1], mm[d, p]
        q2 = refs[d][0][:, sl(p)]
        n_prev = n_scr[d, p]
        num = pair(a["inter"], b["inter"]) * r["q_c"] + jnp.where(lane_lo, r["sv"][:ch], r["sv"][ch:])
        qn = q2.astype(F32) * n_prev
        qn_a = jnp.sum(jnp.where(lane_lo, qn, 0.0), axis=1, keepdims=True)
        qn_b = jnp.sum(jnp.where(lane_lo, 0.0, qn), axis=1, keepdims=True)
        den_a = jnp.maximum(jnp.abs(a["inter"] * qn_a + r["s_sum"][:ch]), a["floor"])
        den_b = jnp.maximum(jnp.abs(b["inter"] * qn_b + r["s_sum"][ch:]), b["floor"])
        refs[d][4][:, sl(p)] = num / pair(den_a, den_b)
        c_scr[d, p] = (jnp.where(row_lo, a["decay"], b["decay"]) * c_scr[d, p]
                       + jnp.where(row_lo == col_lo, r["kv"], 0.0))
        n_scr[d, p] = jnp.where(vec_lo, a["decay"], b["decay"]) * n_prev + r["kw_sum"]
        m_scr[d, p] = jnp.where(vec_lo, a["m_new"], b["m_new"])


def _mlstm(q, k, v, gates, bias, b, lc, s):
    r, w = q.shape
    ch = _tile(math.gcd(lc, s), MLSTM_CHUNKS)
    ncc, ncx = lc // ch, s // ch
    base = b * ncc

    def fwd(i, c):
        return jnp.where(c < ncc, i * ncc + c, base + i * ncx + (c - ncc)), 0

    def bwd(i, c):
        return jnp.where(c < ncc, i * ncc + (ncc - 1 - c), base + i * ncx + (ncx - 1 - (c - ncc))), 0

    spec = lambda width, m: pl.BlockSpec((ch, width), m)
    npair = MLSTM_HEADS // 2
    return pl.pallas_call(
        _mlstm_kernel,
        out_shape=[jax.ShapeDtypeStruct((r, w), F32)] * 2,
        grid=(b, ncc + ncx),
        in_specs=[spec(w, fwd), spec(w, fwd), spec(w, fwd), spec(LANES, fwd),
                  spec(w, bwd), spec(w, bwd), spec(w, bwd), spec(LANES, bwd),
                  pl.BlockSpec((1, LANES), lambda i, c: (0, 0))],
        out_specs=[spec(w, fwd), spec(w, bwd)],
        scratch_shapes=[pltpu.VMEM((2, npair, 2 * HEAD_DIM, 2 * HEAD_DIM), F32),
                        pltpu.VMEM((2, npair, 1, 2 * HEAD_DIM), F32),
                        pltpu.VMEM((2, npair, 1, 2 * HEAD_DIM), F32)],
        compiler_params=_params("parallel", "arbitrary"),
        name="mlstm_scan",
    )(q, k, v, gates, q, k, v, gates, bias)


def _mlstm_out_kernel(hf_ref, hb_ref, o_ref, nrm_ref, ones_ref, y_ref):
    h = hf_ref[...] + hb_ref[...]
    ms = jnp.dot((h * h).astype(BF16), ones_ref[...], preferred_element_type=F32) * (1.0 / HEAD_DIM)
    hn = h * lax.rsqrt(ms + EPS) * nrm_ref[...]
    y_ref[...] = (jax.nn.sigmoid(o_ref[...].astype(F32)) * hn).astype(y_ref.dtype)


def _head_ones(width):
    idx = np.arange(width) // HEAD_DIM
    return jnp.asarray(idx[:, None] == idx[None, :], BF16)


def _mlstm_out(hf, hb, o, norm, tm):
    r, w = hf.shape
    blk = pl.BlockSpec((tm, w), lambda i: (i, 0))
    return pl.pallas_call(
        _mlstm_out_kernel,
        out_shape=jax.ShapeDtypeStruct((r, w), BF16),
        grid=(r // tm,),
        in_specs=[blk, blk, blk,
                  pl.BlockSpec((1, w), lambda i: (0, 0)),
                  pl.BlockSpec((w, w), lambda i: (0, 0))],
        out_specs=blk,
        compiler_params=_params("parallel"),
        name="mlstm_out",
    )(hf, hb, o, norm.reshape(1, w), _head_ones(w))


def _swa_kernel(q_ref, k0, k1, k2, k3, v0, v1, v2, v3, kc_ref, vc_ref, sink_ref, o_ref, *, seq):
    w, dh = SWA_BLOCK, HEAD_DIM
    tq = q_ref.shape[1]
    g = q_ref.shape[0] // dh
    i = pl.program_id(2)
    k_loc = jnp.concatenate([k0[0, 0], k1[0, 0], k2[0, 0], k3[0, 0]], axis=0)
    v_loc = jnp.concatenate([v0[0, 0], v1[0, 0], v2[0, 0], v3[0, 0]], axis=1)
    kpos = (2 * i - 1) * w + lax.broadcasted_iota(I32, (4 * w, tq), 0)
    qpos = i * tq + lax.broadcasted_iota(I32, (4 * w, tq), 1)
    valid = (jnp.abs(qpos - kpos) <= SWA_WINDOW) & (kpos >= 0) & (kpos < seq)
    qs = [q_ref[h * dh:(h + 1) * dh, :] for h in range(g)]
    s_loc = [jnp.where(valid, jnp.dot(k_loc, qs[h], preferred_element_type=F32), NEG_INF) for h in range(g)]
    s_ctx = [jnp.dot(kc_ref[0, 0], qs[h], preferred_element_type=F32) for h in range(g)]
    outs = []
    for h in range(g):
        sink = sink_ref[0, h:h + 1, :]
        m = jnp.maximum(jnp.maximum(jnp.max(s_loc[h], axis=0, keepdims=True),
                                    jnp.max(s_ctx[h], axis=0, keepdims=True)), sink)
        p_loc = jnp.exp2(s_loc[h] - m)
        p_ctx = jnp.exp2(s_ctx[h] - m)
        den = jnp.sum(p_loc, axis=0, keepdims=True) + jnp.sum(p_ctx, axis=0, keepdims=True) + jnp.exp2(sink - m)
        o = (jnp.dot(v_loc, p_loc.astype(BF16), preferred_element_type=F32)
             + jnp.dot(vc_ref[0, 0], p_ctx.astype(BF16), preferred_element_type=F32))
        outs.append(o / den)
    o_ref[...] = jnp.concatenate(outs, axis=0).T.astype(o_ref.dtype)


def _swa(qt, k, vt, k_ctx, vt_ctx, sink2, b, s):
    hkv, dh = k.shape[1], k.shape[3]
    nh = qt.shape[0] // dh
    g = nh // hkv
    w = SWA_BLOCK
    tq = 2 * w
    nb, nq = s // w, s // tq
    lc = k_ctx.shape[2]
    clampi = lambda j: jnp.clip(j, 0, nb - 1)
    kspec = lambda o: pl.BlockSpec((1, 1, w, dh), lambda bi, hi, i: (bi, hi, clampi(2 * i + o), 0))
    vspec = lambda o: pl.BlockSpec((1, 1, dh, w), lambda bi, hi, i: (bi, hi, 0, clampi(2 * i + o)))
    return pl.pallas_call(
        functools.partial(_swa_kernel, seq=s),
        out_shape=jax.ShapeDtypeStruct((b * s, nh * dh), BF16),
        grid=(b, hkv, nq),
        in_specs=[pl.BlockSpec((g * dh, tq), lambda bi, hi, i: (hi, bi * nq + i)),
                  kspec(-1), kspec(0), kspec(1), kspec(2), vspec(-1), vspec(0), vspec(1), vspec(2),
                  pl.BlockSpec((1, 1, lc, dh), lambda bi, hi, i: (bi, hi, 0, 0)),
                  pl.BlockSpec((1, 1, dh, lc), lambda bi, hi, i: (bi, hi, 0, 0)),
                  pl.BlockSpec((1, g, tq), lambda bi, hi, i: (hi, 0, 0))],
        out_specs=pl.BlockSpec((tq, g * dh), lambda bi, hi, i: (bi * nq + i, hi)),
        compiler_params=_params("parallel", "parallel", "arbitrary"),
        name="window_attention",
    )(qt, k, k, k, k, vt, vt, vt, vt, k_ctx, vt_ctx, sink2)


def _ctx_attn_kernel(q_ref, k_ref, v_ref, sink_ref, o_ref, *, use_sink):
    g, lq, dh = q_ref.shape[1:]
    q = q_ref[0].reshape(g * lq, dh)
    s = lax.dot_general(q, k_ref[0, 0], (((1,), (1,)), ((), ())), preferred_element_type=F32) * ATTN_SCALE
    m = jnp.max(s, axis=1, keepdims=True)
    if use_sink:
        m = jnp.maximum(m, sink_ref[0])
    p = jnp.exp(s - m)
    den = jnp.sum(p, axis=1, keepdims=True)
    if use_sink:
        den = den + jnp.exp(sink_ref[0] - m)
    o = jnp.dot(p.astype(BF16), v_ref[0, 0], preferred_element_type=F32) / den
    o_ref[0] = o.reshape(g, lq, dh).astype(o_ref.dtype)


def _ctx_attn(q, k, v, sink_col, use_sink):
    b, nh, lq, dh = q.shape
    hkv = k.shape[1]
    g = nh // hkv
    qblk = pl.BlockSpec((1, g, lq, dh), lambda bi, hi: (bi, hi, 0, 0))
    kblk = pl.BlockSpec((1, 1, k.shape[2], dh), lambda bi, hi: (bi, hi, 0, 0))
    return pl.pallas_call(
        functools.partial(_ctx_attn_kernel, use_sink=use_sink),
        out_shape=jax.ShapeDtypeStruct(q.shape, BF16),
        grid=(b, hkv),
        in_specs=[qblk, kblk, kblk, pl.BlockSpec((1, g * lq, 1), lambda bi, hi: (hi, 0, 0))],
        out_specs=qblk,
        compiler_params=_params("parallel", "parallel"),
        name="context_attention",
    )(q, k, v, sink_col)


def _flash_kernel(q_ref, k_ref, v_ref, o_ref, m_scr, l_scr, acc_scr, sa_scr, sb_scr, *, tk):
    dh = HEAD_DIM
    g = q_ref.shape[0] // dh
    n = k_ref.shape[2] // tk
    m_scr[...] = jnp.full_like(m_scr, NEG_INF)
    l_scr[...] = jnp.zeros_like(l_scr)
    acc_scr[...] = jnp.zeros_like(acc_scr)

    def scores(j, dst):
        kk = k_ref[0, 0, pl.ds(pl.multiple_of(j * tk, tk), tk), :]
        for h in range(g):
            dst[h] = jnp.dot(kk, q_ref[h * dh:(h + 1) * dh, :], preferred_element_type=F32)

    def update(j, src):
        vv = v_ref[0, 0, :, pl.ds(pl.multiple_of(j * tk, tk), tk)]
        ss = [src[h] for h in range(g)]
        m_olds = [m_scr[h] for h in range(g)]
        m_news = [jnp.maximum(m_olds[h], jnp.max(ss[h], axis=0, keepdims=True)) for h in range(g)]
        ps = [jnp.exp2(ss[h] - m_news[h]) for h in range(g)]
        pvs = [jnp.dot(vv, ps[h].astype(BF16), preferred_element_type=F32) for h in range(g)]
        for h in range(g):
            alpha = jnp.exp2(m_olds[h] - m_news[h])
            l_scr[h] = alpha * l_scr[h] + jnp.sum(ps[h], axis=0, keepdims=True)
            acc_scr[h] = alpha * acc_scr[h] + pvs[h]
            m_scr[h] = m_news[h]

    scores(0, sa_scr)

    def body(i, carry):
        scores(2 * i + 1, sb_scr)
        update(2 * i, sa_scr)
        scores(2 * i + 2, sa_scr)
        update(2 * i + 1, sb_scr)
        return carry

    lax.fori_loop(0, (n - 1) // 2, body, 0)
    if n % 2 == 1:
        update(n - 1, sa_scr)
    else:
        scores(n - 1, sb_scr)
        update(n - 2, sa_scr)
        update(n - 1, sb_scr)
    out = (acc_scr[...] / l_scr[...]).reshape(g * dh, -1)
    o_ref[...] = out.T.astype(o_ref.dtype)


def _flash(qt, k, vt, b, s, tq=None, tk=None):
    w = qt.shape[0]
    hkv, t, dh = k.shape[1], k.shape[2], k.shape[3]
    g = w // dh // hkv
    tq = tq or _tile(s, (256, 128))
    tk = tk or _tile(t, (256, 128))
    nq = s // tq
    return pl.pallas_call(
        functools.partial(_flash_kernel, tk=tk),
        out_shape=jax.ShapeDtypeStruct((b * s, w), BF16),
        grid=(b, hkv, nq),
        in_specs=[pl.BlockSpec((g * dh, tq), lambda bi, hi, i: (hi, bi * nq + i)),
                  pl.BlockSpec((1, 1, t, dh), lambda bi, hi, i: (bi, hi, 0, 0)),
                  pl.BlockSpec((1, 1, dh, t), lambda bi, hi, i: (bi, hi, 0, 0))],
        out_specs=pl.BlockSpec((tq, g * dh), lambda bi, hi, i: (bi * nq + i, hi)),
        scratch_shapes=[pltpu.VMEM((g, 1, tq), F32), pltpu.VMEM((g, 1, tq), F32), pltpu.VMEM((g, dh, tq), F32),
                        pltpu.VMEM((g, tk, tq), F32), pltpu.VMEM((g, tk, tq), F32)],
        compiler_params=_params("parallel", "parallel", "arbitrary"),
        name="dense_attention",
    )(qt, k, vt)


def _bmm_kernel(a_ref, b_ref, o_ref):
    o_ref[0] = jnp.dot(a_ref[0], b_ref[0], preferred_element_type=F32).astype(o_ref.dtype)


def _group_matmul(a, bmat, name):
    g, m, k = a.shape
    n = bmat.shape[2]
    tm = _tile(m, (1024, 512, 256, 128, 64, 32, 16, 8))
    return pl.pallas_call(
        _bmm_kernel,
        out_shape=jax.ShapeDtypeStruct((g, m, n), F32),
        grid=(g, m // tm),
        in_specs=[pl.BlockSpec((1, tm, k), lambda gi, i: (gi, i, 0)),
                  pl.BlockSpec((1, k, n), lambda gi, i: (gi, 0, 0))],
        out_specs=pl.BlockSpec((1, tm, n), lambda gi, i: (gi, i, 0)),
        compiler_params=_params("parallel", "parallel"),
        name=name,
    )(a, bmat)


def _s5_scan_kernel(sre_ref, sim_ref, are_ref, aim_ref, zre_ref, zim_ref):
    nd, nsteps = sre_ref.shape[0], sre_ref.shape[1]
    for d in range(nd):
        a_re, a_im = are_ref[d], aim_ref[d]

        def body(i, carry, d=d, a_re=a_re, a_im=a_im):
            z_re, z_im = carry
            zre_ref[d, i] = z_re
            zim_ref[d, i] = z_im
            return (a_re * z_re - a_im * z_im + sre_ref[d, i], a_re * z_im + a_im * z_re + sim_ref[d, i])

        zero = jnp.zeros(sre_ref.shape[2:], F32)
        lax.fori_loop(0, nsteps, body, (zero, zero))


def _s5_scan(s_re, s_im, a_re, a_im):
    full = lambda arr: pl.BlockSpec(arr.shape, lambda i: (0,) * arr.ndim)
    return pl.pallas_call(
        _s5_scan_kernel,
        out_shape=[jax.ShapeDtypeStruct(s_re.shape, F32)] * 2,
        grid=(1,),
        in_specs=[full(s_re), full(s_im), full(a_re), full(a_im)],
        out_specs=[full(s_re), full(s_im)],
        compiler_params=_params("arbitrary"),
        name="s5_chunk_scan",
    )(s_re, s_im, a_re, a_im)


def _s5_glu_kernel(y_ref, u_ref, d_ref, w_ref, b_ref, o_ref):
    y = y_ref[...] + d_ref[...] * u_ref[...]
    y = jax.nn.gelu(y)
    gate = jnp.dot(y.astype(BF16), w_ref[...], preferred_element_type=F32) + b_ref[...]
    o_ref[...] = (y * jax.nn.sigmoid(gate)).astype(o_ref.dtype)


def _s5_glu(y, u, d_skip, w_glu, b_glu, tm):
    r, c = y.shape
    blk = pl.BlockSpec((tm, c), lambda i: (i, 0))
    vec = pl.BlockSpec((1, c), lambda i: (0, 0))
    return pl.pallas_call(
        _s5_glu_kernel,
        out_shape=jax.ShapeDtypeStruct((r, c), BF16),
        grid=(r // tm,),
        in_specs=[blk, blk, vec, pl.BlockSpec((c, c), lambda i: (0, 0)), vec],
        out_specs=blk,
        compiler_params=_params("parallel"),
        name="s5_readout_glu",
    )(y, u, d_skip.reshape(1, c), w_glu.astype(BF16), b_glu.reshape(1, c))


def _s5_tables(lam_re, lam_im, log_dt, b_re, b_im, c_re, c_im):
    ln, p, gc = S5_CHUNK, S5_STATE, S5_GROUP
    lam = lax.complex(lam_re.astype(F32), lam_im.astype(F32))
    dt = jnp.exp(log_dt.astype(F32))[..., None]
    a_bar = jnp.exp(lam * dt)
    b_scale = (a_bar - 1.0) / lam
    b_mat = lax.complex(b_re.astype(F32), b_im.astype(F32))
    c_mat = lax.complex(c_re.astype(F32), c_im.astype(F32))
    tau = jnp.arange(ln + 1, dtype=F32)
    apow = jnp.exp((lam * dt)[:, :, None, :] * tau[None, None, :, None])
    drive = b_scale[..., None] * b_mat[None]
    kern = jnp.real(jnp.einsum('gcp,dgtp,dgpe->dgtce', c_mat, apow[:, :, :ln], drive))
    kc = jnp.concatenate([kern[1, :, :0:-1], (kern[0, :, :1] + kern[1, :, :1]), kern[0, :, 1:]], axis=1)
    kc_e = kc.transpose(0, 3, 1, 2).reshape(S5_GROUPS, gc, (2 * ln - 1) * gc)
    toep = jnp.stack([kc_e[:, :, (ln - 1 - s) * gc:(2 * ln - 1 - s) * gc] for s in range(ln)], axis=1)
    toep = toep.reshape(S5_GROUPS, ln * gc, ln * gc)
    w_f = apow[0, :, ln - 1::-1][:, :ln, :, None] * drive[0][:, None]
    w_b = apow[1, :, :ln, :, None] * drive[1][:, None]
    def m_in(wc):
        wt = wc.transpose(0, 1, 3, 2).reshape(S5_GROUPS, ln * gc, p)
        return jnp.concatenate([jnp.real(wt), jnp.imag(wt)], axis=-1)
    min_all = jnp.concatenate([m_in(w_f), m_in(w_b)], axis=-1)
    o_f = c_mat[:, None] * apow[0, :, 1:ln + 1][:, :, None, :]
    o_b = c_mat[:, None] * apow[1, :, ln:0:-1][:, :, None, :]
    def m_out(oc):
        ot = oc.transpose(0, 3, 1, 2).reshape(S5_GROUPS, p, ln * gc)
        return jnp.concatenate([jnp.real(ot), -jnp.imag(ot)], axis=1)
    rhs = jnp.concatenate([toep, m_out(o_f), m_out(o_b)], axis=1)
    a_l = apow[:, :, ln].reshape(2, 1, S5_GROUPS * p)
    return min_all.astype(BF16), rhs.astype(BF16), jnp.real(a_l), jnp.imag(a_l)


def _s5_mixer(u_x, u_c, tables, d_skip, w_glu, b_glu):
    min_all, rhs, a_re, a_im = tables
    b, s, _ = u_x.shape
    ln, p, gc, ng = S5_CHUNK, S5_STATE, S5_GROUP, S5_GROUPS

    def chunks(u):
        nc = u.shape[1] // ln
        return u.reshape(b, nc, ln, ng, gc).transpose(3, 0, 1, 2, 4).reshape(ng, b * nc, ln * gc).astype(BF16), nc

    ux, ncx = chunks(u_x)
    uc, ncc = chunks(u_c)
    sx = _group_matmul(ux, min_all, "s5_local_state").reshape(ng, b, ncx, 4, p)
    sc = _group_matmul(uc, min_all, "s5_local_state_ctx").reshape(ng, b, ncc, 4, p)

    def scan_order(part):
        f = jnp.concatenate([sc[:, :, :, part], sx[:, :, :, part]], axis=2)
        r = jnp.concatenate([sc[:, :, ::-1, part + 2], sx[:, :, ::-1, part + 2]], axis=2)
        return jnp.stack([f, r]).transpose(0, 3, 2, 1, 4).reshape(2, ncc + ncx, b, ng * p)

    z_re, z_im = _s5_scan(scan_order(0), scan_order(1), a_re, a_im)

    def latent(z, d):
        zl = z[d, ncc:]
        if d == 1:
            zl = zl[::-1]
        return zl.reshape(ncx, b, ng, p).transpose(2, 1, 0, 3).reshape(ng, b * ncx, p)

    lhs = jnp.concatenate([ux, latent(z_re, 0).astype(BF16), latent(z_im, 0).astype(BF16),
                           latent(z_re, 1).astype(BF16), latent(z_im, 1).astype(BF16)], axis=-1)
    y = _group_matmul(lhs, rhs, "s5_outputs")
    y = y.reshape(ng, b, ncx, ln, gc).transpose(1, 2, 3, 0, 4).reshape(b * s, S5_CHANNELS)
    return _s5_glu(y, u_x.reshape(b * s, S5_CHANNELS), d_skip, w_glu, b_glu, _tile(b * s, (1024, 512, 256)))


def _out_proj_kernel(x_ref, a_ref, b_ref, wa_ref, wb_ref, gt_ref, g_ref, sc_ref, sh_ref, wr_ref, br_ref,
                     xo_ref, xn_ref, lg_ref):
    y = (jnp.dot(a_ref[...], wa_ref[...], preferred_element_type=F32)
         + jnp.dot(b_ref[...], wb_ref[...], preferred_element_type=F32))
    x = x_ref[...] + gt_ref[0] * y
    xo_ref[...] = x
    xn = x * lax.rsqrt(jnp.mean(x * x, axis=-1, keepdims=True) + EPS) * g_ref[...]
    xn = xn * (1.0 + sc_ref[0]) + sh_ref[0]
    hi = xn.astype(BF16)
    xn_ref[...] = hi
    lo = (xn - hi.astype(F32)).astype(BF16)
    part = jnp.dot(hi, wr_ref[...], preferred_element_type=F32)
    part = part[:, :LANES] + part[:, LANES:] + jnp.dot(lo, wr_ref[:, :LANES], preferred_element_type=F32)
    lg_ref[...] = part[:, :lg_ref.shape[1]] + br_ref[...]


def _out_proj(x, row0, a, bmix, wa, wb, gate, g_ffn, scale, shift, mod_map, w_router, b_router, tm):
    r = a.shape[0]
    d = x.shape[1]
    t0 = row0 // tm
    ne = w_router.shape[1]
    w_hi = w_router.astype(BF16)
    w_lo = (w_router - w_hi.astype(F32)).astype(BF16)
    w_router = jnp.concatenate([_pad_cols(w_hi), _pad_cols(w_lo)], axis=1)
    row = lambda w: pl.BlockSpec((tm, w), lambda i: (i, 0))
    mod = pl.BlockSpec((1, 1, d), lambda i: (mod_map(i), 0, 0))
    full = lambda arr: pl.BlockSpec(arr.shape, lambda i: (0, 0))
    return pl.pallas_call(
        _out_proj_kernel,
        out_shape=[jax.ShapeDtypeStruct((r, d), F32), jax.ShapeDtypeStruct((r, d), BF16),
                   jax.ShapeDtypeStruct((r, ne), F32)],
        grid=(r // tm,),
        in_specs=[pl.BlockSpec((tm, d), lambda i: (i + t0, 0)), row(a.shape[1]), row(bmix.shape[1]),
                  full(wa), full(wb), mod, pl.BlockSpec((1, d), lambda i: (0, 0)), mod, mod,
                  full(w_router), pl.BlockSpec((1, ne), lambda i: (0, 0))],
        out_specs=[row(d), row(d), row(ne)],
        compiler_params=_params("parallel"),
        name="out_proj",
    )(x, a, bmix, wa, wb, gate, g_ffn.reshape(1, d), scale, shift, w_router, b_router.reshape(1, ne))


def _route_kernel(lg_ref, idx_ref, gate_ref, pos_ref, cnt_ref, carry):
    tm, ne = lg_ref.shape

    @pl.when(pl.program_id(0) == 0)
    def _():
        carry[...] = jnp.zeros_like(carry)

    work = lg_ref[...]
    lane = lax.broadcasted_iota(I32, (tm, ne), 1).astype(F32)
    out_lane = lax.broadcasted_iota(I32, (tm, LANES), 1)
    vals, hots, idx_out = [], [], jnp.zeros((tm, LANES), I32)
    for kk in range(TOP_K):
        mx = jnp.max(work, axis=1, keepdims=True)
        idx = jnp.min(jnp.where(work == mx, lane, float(ne)), axis=1, keepdims=True)
        hot = lane == idx
        work = jnp.where(hot, -jnp.inf, work)
        vals.append(mx)
        hots.append(hot)
        idx_out = jnp.where(out_lane == kk, idx.astype(I32), idx_out)
    exps = [jnp.exp(vv - vals[0]) for vv in vals]
    tot = exps[0] + exps[1] + exps[2] + exps[3]
    multi = (hots[0] | hots[1] | hots[2] | hots[3]).astype(BF16)
    r_i = lax.broadcasted_iota(I32, (tm, tm), 0)
    c_i = lax.broadcasted_iota(I32, (tm, tm), 1)
    before = (r_i > c_i).astype(BF16)
    prefix = jnp.dot(before, multi, preferred_element_type=F32) + carry[...]
    gate_out = jnp.zeros((tm, LANES), F32)
    pos_out = jnp.zeros((tm, LANES), I32)
    for kk in range(TOP_K):
        gate_out = jnp.where(out_lane == kk, exps[kk] / tot, gate_out)
        pos = jnp.sum(jnp.where(hots[kk], prefix, 0.0), axis=1, keepdims=True).astype(I32)
        pos_out = jnp.where(out_lane == kk, pos, pos_out)
    idx_ref[...] = idx_out
    gate_ref[...] = gate_out
    pos_ref[...] = pos_out
    carry[...] = carry[...] + jnp.sum(multi.astype(F32), axis=0, keepdims=True)
    cnt_ref[...] = carry[...]


def _route(logits, tm, row0, n):
    ne = logits.shape[1]
    t0 = row0 // tm
    wide = pl.BlockSpec((tm, LANES), lambda i: (i, 0))
    return pl.pallas_call(
        _route_kernel,
        out_shape=[jax.ShapeDtypeStruct((n, LANES), I32), jax.ShapeDtypeStruct((n, LANES), F32),
                   jax.ShapeDtypeStruct((n, LANES), I32), jax.ShapeDtypeStruct((1, ne), F32)],
        grid=(n // tm,),
        in_specs=[pl.BlockSpec((tm, ne), lambda i: (i + t0, 0))],
        out_specs=[wide, wide, wide, pl.BlockSpec((1, ne), lambda i: (0, 0))],
        scratch_shapes=[pltpu.VMEM((1, ne), F32)],
        compiler_params=_params("arbitrary"),
        name="moe_route",
    )(logits)


def _expert_kernel(be_ref, nb_ref, x_ref, wgu_ref, bgu_ref, wd_ref, bd_ref, perm_ref, o_ref, wgu_bf, wd_bf):
    i = pl.program_id(0)
    fresh = jnp.logical_or(i == 0, be_ref[i] != be_ref[jnp.maximum(i - 1, 0)])

    @pl.when(jnp.logical_and(i < nb_ref[0], fresh))
    def _():
        tile = perm_ref.shape[0]
        for t in range(wgu_ref.shape[2] // tile):
            sl = slice(t * tile, (t + 1) * tile)
            wgu_bf[:, sl] = jnp.dot(wgu_ref[0, :, sl].astype(BF16), perm_ref[...],
                                    preferred_element_type=F32).astype(BF16)
        wd_bf[...] = wd_ref[0].astype(BF16)

    @pl.when(i < nb_ref[0])
    def _():
        h = jnp.dot(x_ref[...], wgu_bf[...], preferred_element_type=F32) + bgu_ref[0]
        acts = []
        for t in range(h.shape[1] // (2 * LANES)):
            glu = jnp.minimum(h[:, 2 * LANES * t:2 * LANES * t + LANES], SWIGLU_LIMIT)
            lin = jnp.clip(h[:, 2 * LANES * t + LANES:2 * LANES * (t + 1)], -SWIGLU_LIMIT, SWIGLU_LIMIT)
            acts.append((glu * jax.nn.sigmoid(SWIGLU_ALPHA * glu) * (lin + 1.0)).astype(BF16))
        act = jnp.concatenate(acts, axis=1)
        o_ref[...] = (jnp.dot(act, wd_bf[...], preferred_element_type=F32) + bd_ref[0]).astype(o_ref.dtype)

    @pl.when(i >= nb_ref[0])
    def _():
        o_ref[...] = jnp.zeros_like(o_ref)


def _experts(xs, blk_e, n_used, wgu, bgu, wd, bd):
    n, d = xs.shape
    ff2 = wgu.shape[2]
    nblk = n // MOE_ROWS
    tile = 2 * LANES
    j = np.arange(tile)
    col = np.where(j < LANES, 2 * j, 2 * (j - LANES) + 1)
    perm = jnp.asarray(np.arange(tile)[:, None] == col[None, :], BF16)
    wspec = lambda shape: pl.BlockSpec((1,) + shape, lambda i, be, nb: (be[i], 0, 0))
    return pl.pallas_call(
        _expert_kernel,
        out_shape=jax.ShapeDtypeStruct((n, d), BF16),
        grid_spec=pltpu.PrefetchScalarGridSpec(
            num_scalar_prefetch=2,
            grid=(nblk,),
            in_specs=[pl.BlockSpec((MOE_ROWS, d), lambda i, be, nb: (i, 0)),
                      wspec((d, ff2)), wspec((1, ff2)), wspec((ff2 // 2, d)), wspec((1, d)),
                      pl.BlockSpec((tile, tile), lambda i, be, nb: (0, 0))],
            out_specs=pl.BlockSpec((MOE_ROWS, d), lambda i, be, nb: (i, 0)),
            scratch_shapes=[pltpu.VMEM((d, ff2), BF16), pltpu.VMEM((ff2 // 2, d), BF16)]),
        compiler_params=_params("arbitrary"),
        name="moe_experts",
    )(blk_e, n_used, xs, wgu, bgu, wd, bd, perm)


def _combine_kernel(x_ref, y_ref, gate_ref, gt_ref, g_ref, o_ref, *, final_norm):
    gates = gate_ref[...]
    y = gates[:, 0:1] * y_ref[0].astype(F32)
    for kk in range(1, TOP_K):
        y = y + gates[:, kk:kk + 1] * y_ref[kk].astype(F32)
    x = x_ref[...] + gt_ref[0] * y
    if final_norm:
        x = x * lax.rsqrt(jnp.mean(x * x, axis=-1, keepdims=True) + EPS) * g_ref[...]
    o_ref[...] = x


def _combine_alias_kernel(x_ref, y_ref, gate_ref, gt_ref, g_ref, prev_ref, o_ref, *, final_norm):
    del prev_ref
    _combine_kernel(x_ref, y_ref, gate_ref, gt_ref, g_ref, o_ref, final_norm=final_norm)


def _combine(x, y4, gates, gate_mod, mod_map, g_final, final_norm, tm, row0, prev):
    r, d = x.shape
    n = y4.shape[1]
    t0 = row0 // tm
    in_specs = [pl.BlockSpec((tm, d), lambda i: (i + t0, 0)),
                pl.BlockSpec((TOP_K, tm, d), lambda i: (0, i, 0)),
                pl.BlockSpec((tm, LANES), lambda i: (i, 0)),
                pl.BlockSpec((1, 1, d), lambda i: (mod_map(i + t0), 0, 0)),
                pl.BlockSpec((1, d), lambda i: (0, 0))]
    args = [x, y4, gates, gate_mod, g_final.reshape(1, d)]
    body, aliases = _combine_kernel, {}
    if prev is not None:
        in_specs.append(pl.BlockSpec(memory_space=pl.ANY))
        args.append(prev)
        body, aliases = _combine_alias_kernel, {len(args) - 1: 0}
    return pl.pallas_call(
        functools.partial(body, final_norm=final_norm),
        out_shape=jax.ShapeDtypeStruct((r, d), F32),
        grid=(n // tm,),
        in_specs=in_specs,
        out_specs=pl.BlockSpec((tm, d), lambda i: (i + t0, 0)),
        input_output_aliases=aliases,
        compiler_params=_params("parallel"),
        name="moe_combine",
    )(*args)


def _after(value, token):
    if token is None:
        return value
    return lax.optimization_barrier((value, token))[0]


def _moe(x, xn, logits, gate_mod, mod_map, weights, g_final, final_norm, tm):
    wgu, bgu, wd, bd = weights
    n, d = xn.shape
    nblk = n * TOP_K // MOE_ROWS + N_EXPERTS
    blk_row0 = jnp.arange(nblk, dtype=I32) * MOE_ROWS
    idx_w, gates_w, pos_w, counts = _route(logits, tm, 0, n)
    idx, pos = idx_w[:, :TOP_K], pos_w[:, :TOP_K]
    counts = counts[0].astype(I32)
    padded = (counts + MOE_ROWS - 1) // MOE_ROWS * MOE_ROWS
    pad_end = jnp.cumsum(padded)
    pad_start = pad_end - padded
    dest = pad_start[idx] + pos
    blk_e = jnp.minimum(jnp.sum((pad_end[None, :] <= blk_row0[:, None]).astype(I32), axis=1), N_EXPERTS - 1)
    n_used = (pad_end[-1:] // MOE_ROWS).astype(I32)
    key = idx * n + jnp.arange(n, dtype=I32)[:, None]
    tok_sorted = jnp.sort(key.reshape(-1)) % n
    grp_start = jnp.cumsum(counts) - counts
    slot = blk_row0[:, None] + jnp.arange(MOE_ROWS, dtype=I32)[None, :]
    compact = jnp.minimum(slot + (grp_start - pad_start)[blk_e][:, None], (grp_start + counts - 1)[blk_e][:, None])
    src = tok_sorted.at[jnp.clip(compact, 0, n * TOP_K - 1).reshape(-1)].get(mode="promise_in_bounds")
    wait = yield src
    xs = _after(xn.at[src].get(mode="promise_in_bounds"), wait)
    ys = _experts(xs, blk_e, n_used, wgu, bgu, wd, bd)
    wait = yield ys
    y4 = _after(ys.at[dest.T.reshape(-1)].get(mode="promise_in_bounds"), wait).reshape(TOP_K, n, d)
    return _combine(x, y4, gates_w, gate_mod, mod_map, g_final, final_norm, tm, 0, None)


def _moe_weights(w_gate_up, b_gate_up, w_down, b_down):
    ne, d, ff2 = w_gate_up.shape
    b_tiled = b_gate_up.reshape(ne, ff2 // (2 * LANES), LANES, 2).swapaxes(-1, -2).reshape(ne, 1, ff2)
    return w_gate_up, b_tiled, w_down, b_down[:, None, :]


def _rope_tables(n_tokens):
    rows = n_tokens // GRID_W
    row = jnp.repeat(jnp.arange(rows, dtype=I32), GRID_W).astype(F32)
    col = jnp.tile(jnp.arange(GRID_W, dtype=I32), rows).astype(F32)
    inv = ROPE_BASE ** (-jnp.arange(0, ROPE_AXIS_DIM, 2, dtype=F32) / ROPE_AXIS_DIM)
    ang_r, ang_c = row[:, None] * inv, col[:, None] * inv
    cos = jnp.concatenate([jnp.cos(ang_r)] * 2 + [jnp.cos(ang_c)] * 2, axis=-1)
    sin = jnp.concatenate([jnp.sin(ang_r)] * 2 + [jnp.sin(ang_c)] * 2, axis=-1)
    return cos, sin


def _rot_perm():
    q = ROPE_AXIS_DIM // 2
    d = np.arange(HEAD_DIM)
    first = (d % ROPE_AXIS_DIM) < q
    perm = np.where(first, d + q, d - q)
    sign = np.where(first, -1.0, 1.0).astype(np.float32)
    return perm, sign


def _rot_cols(w, n_heads):
    perm, sign = _rot_perm()
    k = w.shape[0]
    wh = w.reshape(k, n_heads, HEAD_DIM)
    return (wh[:, :, perm] * sign).reshape(k, n_heads * HEAD_DIM)


def _to_heads(t, b, n_heads):
    return t.reshape(b, -1, n_heads, HEAD_DIM).transpose(0, 2, 1, 3)


def _from_heads(t):
    b, h, tt, dh = t.shape
    return t.transpose(0, 2, 1, 3).reshape(b * tt, h * dh)


def _layer_even(xa, mod, b, s, lc, g_mix, g_ffn, w_in, w_out, gate_bias, mlstm_norm, sink, router, moe_w):
    d = xa.shape[1]
    nctx = b * lc
    tm = _tile(math.gcd(nctx, s), (512, 256, 128))
    mod_map = lambda i: jnp.where(i * tm < nctx, b, (i * tm - nctx) // s)
    sh1, sc1, gt1, sh2, sc2, gt2 = [mod[:, j][:, None, :] for j in range(6)]

    hm, hs, hk = MLSTM_HEADS * HEAD_DIM, SWA_HEADS * HEAD_DIM, SWA_KV_HEADS * HEAD_DIM
    o0 = np.cumsum([0, hm, hm, hm, hm, 4 * MLSTM_HEADS, hs, hk, hk])
    seg = lambda j: w_in[:, o0[j]:o0[j + 1]]
    w_cat = jnp.concatenate([seg(0), seg(1) * ATTN_SCALE, seg(2), seg(3),
                             seg(6), _rot_cols(seg(6), SWA_KV_HEADS), seg(7), _pad_cols(seg(4))], axis=1).astype(BF16)
    w_q = jnp.concatenate([seg(5), _rot_cols(seg(5), SWA_HEADS)], axis=1).T.astype(BF16)
    widths = [hm, hm, hm, hm, hk, hk, hk, LANES]
    dts = [BF16] * 7 + [F32]
    qa, ka, va, oa, kb, kb_r, vb, gts, qt, qt_r = _in_proj(xa, g_mix, sc1, sh1, mod_map, w_cat, widths, dts, tm,
                                                           wt=w_q, t_widths=[hs, hs], t_dtypes=[BF16, BF16])

    hf, hb = _mlstm(qa, ka, va, gts, _pad_cols(gate_bias.astype(F32).reshape(1, -1)), b, lc, s)
    mix_a = _mlstm_out(hf, hb, oa, mlstm_norm.reshape(-1), tm)

    cos, sin = _rope_tables(s)
    ones = jnp.ones((HEAD_DIM,), F32)
    g = SWA_HEADS // SWA_KV_HEADS
    heads = lambda t, n: _to_heads(t, b, n)
    log2e = math.log2(math.e)
    q_x = _q_prep_t(qt, qt_r, nctx, b * s, cos.T, sin.T, ones, ones, ATTN_SCALE * log2e, norm=False)
    k_x = _qk_prep(heads(kb[nctx:], SWA_KV_HEADS), heads(kb_r[nctx:], SWA_KV_HEADS), cos, sin, ones, ones,
                   norm=False, rope=True, scale=1.0)
    k_c, v_c, v_x = heads(kb[:nctx], SWA_KV_HEADS), heads(vb[:nctx], SWA_KV_HEADS), heads(vb[nctx:], SWA_KV_HEADS)
    q_c = heads(qt[:, :nctx].T, SWA_HEADS)
    sink_h = sink.astype(F32).reshape(SWA_KV_HEADS, g, 1)
    att_x = _swa(q_x, k_x, v_x.swapaxes(-1, -2), k_c, v_c.swapaxes(-1, -2),
                 jnp.broadcast_to(sink_h * log2e, (SWA_KV_HEADS, g, 2 * SWA_BLOCK)), b, s)
    att_c = _ctx_attn(q_c, k_c, v_c, jnp.broadcast_to(sink_h[..., None], (SWA_KV_HEADS, g, lc, 1))
                      .reshape(SWA_KV_HEADS, g * lc, 1), True)
    mix_b = jnp.concatenate([_from_heads(att_c), att_x], axis=0)

    w_router, b_router = router
    xa, xn2, logits = _out_proj(xa, 0, mix_a, mix_b, w_out[:hm].astype(BF16), w_out[hm:].astype(BF16), gt1, g_ffn,
                                sc2, sh2, mod_map, w_router.astype(F32), b_router.astype(F32), tm)
    return (yield from _moe(xa, xn2, logits, gt2, mod_map, moe_w, g_ffn, False, tm))


def _layer_odd_last(xa, mod, b, s, lc, g_mix, g_ffn, w_in, w_out, s5_params, d_skip, w_glu, b_glu,
                    q_norm, k_norm, router, moe_w, g_final):
    nctx = b * lc
    tm = _tile(math.gcd(nctx, s), (512, 256, 128))
    mod_map = lambda i: jnp.where(i * tm < nctx, b, (i * tm - nctx) // s)
    lat_map = lambda i: i * tm // s
    sh1, sc1, gt1, sh2, sc2, gt2 = [mod[:, j][:, None, :] for j in range(6)]

    hq, hk = ATT_HEADS * HEAD_DIM, ATT_KV_HEADS * HEAD_DIM
    o1 = np.cumsum([0, S5_CHANNELS, hq, hk, hk])
    seg = lambda j: w_in[:, o1[j]:o1[j + 1]]
    kpad = lambda w: _pad_cols(w, 2 * LANES)
    w_cat = jnp.concatenate([seg(0), kpad(seg(2)), kpad(_rot_cols(seg(2), ATT_KV_HEADS)), kpad(seg(3))],
                            axis=1).astype(BF16)
    w_q = jnp.concatenate([seg(1), _rot_cols(seg(1), ATT_HEADS)], axis=1).T.astype(BF16)
    widths = [S5_CHANNELS, 2 * LANES, 2 * LANES, 2 * LANES]
    dts = [F32, BF16, BF16, BF16]
    u, k, k_r, v, qt, qt_r = _in_proj(xa, g_mix, sc1, sh1, mod_map, w_cat, widths, dts, tm,
                                      wt=w_q, t_widths=[hq, hq], t_dtypes=[BF16, BF16])

    mix_a = _s5_mixer(u[nctx:].reshape(b, s, S5_CHANNELS), u[:nctx].reshape(b, lc, S5_CHANNELS),
                      _s5_tables(*s5_params), d_skip, w_glu, b_glu)

    cos, sin = _rope_tables(s)
    perm, _ = _rot_perm()
    qn, kn = q_norm.astype(F32), k_norm.astype(F32)
    heads = lambda t, n: _to_heads(t[:, :n * HEAD_DIM], b, n)
    q_x = _q_prep_t(qt, qt_r, nctx, b * s, cos.T, sin.T, qn, qn[perm], ATTN_SCALE * math.log2(math.e))
    k_x = _qk_prep(heads(k[nctx:], ATT_KV_HEADS), heads(k_r[nctx:], ATT_KV_HEADS), cos, sin, kn, kn[perm],
                   norm=True, rope=True, scale=1.0)
    kc_raw = heads(k[:nctx], ATT_KV_HEADS)
    k_c = _qk_prep(kc_raw, kc_raw, cos[:lc], sin[:lc], kn, kn, norm=True, rope=False, scale=1.0)
    k_all = jnp.concatenate([k_c, k_x], axis=2)
    v_all = jnp.concatenate([heads(v[:nctx], ATT_KV_HEADS), heads(v[nctx:], ATT_KV_HEADS)], axis=2)
    mix_b = _flash(q_x, k_all, v_all.swapaxes(-1, -2), b, s)

    w_router, b_router = router
    hs = S5_CHANNELS
    x, xn2, logits = _out_proj(xa, nctx, mix_a, mix_b, w_out[:hs].astype(BF16), w_out[hs:].astype(BF16), gt1, g_ffn,
                               sc2, sh2, lat_map, w_router.astype(F32), b_router.astype(F32), tm)
    return (yield from _moe(x, xn2, logits, gt2, lat_map, moe_w, g_final, True, tm))


def kernel(x, c, ctx, c_ctx, l0_w_mod, l0_b_mod, l0_g_mix, l0_g_ffn, l0_w_in, l0_w_out, l0_gate_bias, l0_mlstm_norm, l0_sink, l0_w_router, l0_b_router, l0_w_gate_up, l0_b_gate_up, l0_w_down, l0_b_down, l1_w_mod, l1_b_mod, l1_g_mix, l1_g_ffn, l1_w_in, l1_w_out, l1_lam_re, l1_lam_im, l1_log_dt, l1_b_re, l1_b_im, l1_c_re, l1_c_im, l1_d_skip, l1_w_glu, l1_b_glu, l1_q_norm, l1_k_norm, l1_w_router, l1_b_router, l1_w_gate_up, l1_b_gate_up, l1_w_down, l1_b_down, g_final):
    b, s, d = x.shape
    lc = ctx.shape[1]
    cond = jnp.concatenate([c, c_ctx[None, :]], axis=0)
    cond = jnp.pad(cond, ((0, (-(b + 1)) % 8), (0, 0)))
    mod0 = _silu_linear(cond, l0_w_mod, l0_b_mod)[:b + 1].reshape(b + 1, 6, d)
    mod1 = _silu_linear(cond, l1_w_mod, l1_b_mod)[:b + 1].reshape(b + 1, 6, d)

    moe0 = _moe_weights(l0_w_gate_up, l0_b_gate_up, l0_w_down, l0_b_down)
    moe1 = _moe_weights(l1_w_gate_up, l1_b_gate_up, l1_w_down, l1_b_down)
    groups = BATCH_GROUPS if b % BATCH_GROUPS == 0 else 1
    bg = b // groups

    def trunk(gi, wait):
        sl = slice(gi * bg, (gi + 1) * bg)
        mod0_g = jnp.concatenate([mod0[sl], mod0[b:b + 1]], axis=0)
        mod1_g = jnp.concatenate([mod1[sl], mod1[b:b + 1]], axis=0)
        xa = jnp.concatenate([ctx[sl].reshape(bg * lc, d), x[sl].reshape(bg * s, d)], axis=0)
        xa = yield from _layer_even(_after(xa, wait), mod0_g, bg, s, lc, l0_g_mix, l0_g_ffn, l0_w_in, l0_w_out,
                                    l0_gate_bias, l0_mlstm_norm, l0_sink, (l0_w_router, l0_b_router), moe0)
        out = yield from _layer_odd_last(xa, mod1_g, bg, s, lc, l1_g_mix, l1_g_ffn, l1_w_in, l1_w_out,
                                         (l1_lam_re, l1_lam_im, l1_log_dt, l1_b_re, l1_b_im, l1_c_re, l1_c_im),
                                         l1_d_skip, l1_w_glu, l1_b_glu, l1_q_norm, l1_k_norm,
                                         (l1_w_router, l1_b_router), moe1, g_final)
        return out.reshape(bg, s, d)

    gens, outs, token = [], [None] * groups, None
    for gi in range(groups):
        gen = trunk(gi, token)
        token = next(gen)
        gens.append(gen)
    live = list(range(groups))
    while live:
        for gi in list(live):
            try:
                token = gens[gi].send(token)
            except StopIteration as done:
                outs[gi] = token = done.value
                live.remove(gi)
    return outs[0] if groups == 1 else jnp.concatenate(outs, axis=0)
```

```python
import functools
import math

import jax
import jax.numpy as jnp
import numpy as np
from jax import lax
from jax.experimental import pallas as pl
from jax.experimental.pallas import tpu as pltpu

F32 = jnp.float32
BF16 = jnp.bfloat16
I32 = jnp.int32

GRID_W = 64
HEAD_DIM = 64
ATTN_SCALE = HEAD_DIM ** -0.5
ROPE_AXIS_DIM = HEAD_DIM // 2
ROPE_BASE = 10000.0
EPS = 1e-6
NEG_INF = -1e30

MLSTM_HEADS = 8
MLSTM_CHUNKS = (256, 128, 64)
SWA_HEADS = 8
SWA_KV_HEADS = 2
SWA_WINDOW = 128
SWA_BLOCK = 128
S5_CHANNELS = 256
S5_GROUP = 16
S5_GROUPS = S5_CHANNELS // S5_GROUP
S5_STATE = 64
S5_CHUNK = 64
ATT_HEADS = 12
ATT_KV_HEADS = 3
N_EXPERTS = 32
TOP_K = 4
SWIGLU_LIMIT = 7.0
SWIGLU_ALPHA = 1.702

LANES = 128
VMEM_LIMIT = 56 * 1024 * 1024
MOE_ROWS = 512
BATCH_GROUPS = 2
HIGHEST = lax.Precision.HIGHEST


def _params(*sem):
    return pltpu.CompilerParams(dimension_semantics=sem, vmem_limit_bytes=VMEM_LIMIT)


def _tile(n, prefs):
    for t in prefs:
        if n % t == 0:
            return t
    return n


def _pad_cols(w, mult=LANES):
    pad = (-w.shape[-1]) % mult
    if pad:
        w = jnp.pad(w, [(0, 0)] * (w.ndim - 1) + [(0, pad)])
    return w


def _linear_kernel(x_ref, w_ref, b_ref, o_ref):
    x = x_ref[...]
    x = x * jax.nn.sigmoid(x)
    o_ref[...] = jnp.dot(x, w_ref[...], precision=HIGHEST, preferred_element_type=F32) + b_ref[...]


def _silu_linear(x, w, b):
    m, k = x.shape
    n = w.shape[1]
    tn = _tile(n, (1024, 512, 256, 128))
    return pl.pallas_call(
        _linear_kernel,
        out_shape=jax.ShapeDtypeStruct((m, n), F32),
        grid=(n // tn,),
        in_specs=[pl.BlockSpec((m, k), lambda j: (0, 0)),
                  pl.BlockSpec((k, tn), lambda j: (0, j)),
                  pl.BlockSpec((1, tn), lambda j: (0, j))],
        out_specs=pl.BlockSpec((m, tn), lambda j: (0, j)),
        compiler_params=_params("arbitrary"),
        name="adaln_linear",
    )(x, w, b.reshape(1, n))


def _in_proj_kernel(x_ref, g_ref, sc_ref, sh_ref, w_ref, wt_ref, *out_refs, widths, t_widths):
    x = x_ref[...]
    xn = x * lax.rsqrt(jnp.mean(x * x, axis=-1, keepdims=True) + EPS) * g_ref[...]
    xb = (xn * (1.0 + sc_ref[0]) + sh_ref[0]).astype(BF16)
    off = 0
    for o_ref, w in zip(out_refs, widths):
        o_ref[...] = jnp.dot(xb, w_ref[:, off:off + w], preferred_element_type=F32).astype(o_ref.dtype)
        off += w
    off = 0
    for o_ref, w in zip(out_refs[len(widths):], t_widths):
        o_ref[...] = lax.dot_general(wt_ref[off:off + w, :], xb, (((1,), (1,)), ((), ())),
                                     preferred_element_type=F32).astype(o_ref.dtype)
        off += w


def _in_proj(x, g, scale, shift, mod_map, w, widths, dtypes, tm, wt=None, t_widths=(), t_dtypes=()):
    r, d = x.shape
    if wt is None:
        wt = jnp.zeros((8, d), BF16)
    return pl.pallas_call(
        functools.partial(_in_proj_kernel, widths=tuple(widths), t_widths=tuple(t_widths)),
        out_shape=([jax.ShapeDtypeStruct((r, wd), dt) for wd, dt in zip(widths, dtypes)]
                   + [jax.ShapeDtypeStruct((wd, r), dt) for wd, dt in zip(t_widths, t_dtypes)]),
        grid=(r // tm,),
        in_specs=[pl.BlockSpec((tm, d), lambda i: (i, 0)),
                  pl.BlockSpec((1, d), lambda i: (0, 0)),
                  pl.BlockSpec((1, 1, d), lambda i: (mod_map(i), 0, 0)),
                  pl.BlockSpec((1, 1, d), lambda i: (mod_map(i), 0, 0)),
                  pl.BlockSpec(w.shape, lambda i: (0, 0)),
                  pl.BlockSpec(wt.shape, lambda i: (0, 0))],
        out_specs=([pl.BlockSpec((tm, wd), lambda i: (i, 0)) for wd in widths]
                   + [pl.BlockSpec((wd, tm), lambda i: (0, i)) for wd in t_widths]),
        compiler_params=_params("parallel"),
        name="in_proj",
    )(x, g.reshape(1, d), scale, shift, w, wt)


def _qk_prep_kernel(x_ref, xr_ref, cos_ref, sin_ref, g_ref, gr_ref, o_ref, *, norm, rope, scale):
    x = x_ref[0].astype(F32)
    if norm:
        s = lax.rsqrt(jnp.mean(x * x, axis=-1, keepdims=True) + EPS)
        x = x * s * g_ref[...]
    if rope:
        xr = xr_ref[0].astype(F32)
        if norm:
            xr = xr * s * gr_ref[...]
        x = x * cos_ref[...] + xr * sin_ref[...]
    o_ref[0] = (x * scale).astype(o_ref.dtype)


def _qk_prep(x, x_rot, cos, sin, gain, gain_rot, *, norm, rope, scale):
    b, h, t, dh = x.shape
    tt = _tile(t, (512, 256, 128))
    blk = pl.BlockSpec((1, h, tt, dh), lambda i, j: (i, 0, j, 0))
    tab = pl.BlockSpec((tt, dh), lambda i, j: (j, 0))
    vec = pl.BlockSpec((1, dh), lambda i, j: (0, 0))
    return pl.pallas_call(
        functools.partial(_qk_prep_kernel, norm=norm, rope=rope, scale=scale),
        out_shape=jax.ShapeDtypeStruct(x.shape, BF16),
        grid=(b, t // tt),
        in_specs=[blk, blk, tab, tab, vec, vec],
        out_specs=blk,
        compiler_params=_params("parallel", "parallel"),
        name="qk_prep",
    )(x, x_rot, cos, sin, gain.reshape(1, dh), gain_rot.reshape(1, dh))


def _q_prep_t_kernel(x_ref, xr_ref, cos_ref, sin_ref, g_ref, gr_ref, o_ref, *, norm, scale):
    w, tt = x_ref.shape
    nh = w // HEAD_DIM
    x = x_ref[...].astype(F32).reshape(nh, HEAD_DIM, tt)
    xr = xr_ref[...].astype(F32).reshape(nh, HEAD_DIM, tt)
    if norm:
        s = lax.rsqrt(jnp.mean(x * x, axis=1, keepdims=True) + EPS)
        x, xr = x * s * g_ref[...], xr * s * gr_ref[...]
    y = x * cos_ref[...] + xr * sin_ref[...]
    o_ref[...] = (y * scale).reshape(w, tt).astype(o_ref.dtype)


def _q_prep_t(xt, xt_rot, col0, n_cols, cos_t, sin_t, gain, gain_rot, scale, norm=True):
    w = xt.shape[0]
    s = cos_t.shape[1]
    tt = _tile(math.gcd(col0, s), (512, 256, 128))
    blk = pl.BlockSpec((w, tt), lambda j: (0, j + col0 // tt))
    tab = pl.BlockSpec((HEAD_DIM, tt), lambda j: (0, j % (s // tt)))
    vec = pl.BlockSpec((HEAD_DIM, 1), lambda j: (0, 0))
    return pl.pallas_call(
        functools.partial(_q_prep_t_kernel, norm=norm, scale=scale),
        out_shape=jax.ShapeDtypeStruct((w, n_cols), BF16),
        grid=(n_cols // tt,),
        in_specs=[blk, blk, tab, tab, vec, vec],
        out_specs=pl.BlockSpec((w, tt), lambda j: (0, j)),
        compiler_params=_params("parallel"),
        name="q_prep_t",
    )(xt, xt_rot, cos_t, sin_t, gain.reshape(HEAD_DIM, 1), gain_rot.reshape(HEAD_DIM, 1))


def _log_sigmoid(x):
    return jnp.minimum(x, 0.0) - jnp.log(1.0 + jnp.exp(-jnp.abs(x)))


def _mlstm_kernel(qf, kf, vf, gf, qb, kb, vb, gb, bias_ref, hf_ref, hb_ref, c_scr, n_scr, m_scr):
    ch, nh, dh = qf.shape[0], MLSTM_HEADS, HEAD_DIM

    @pl.when(pl.program_id(1) == 0)
    def _():
        c_scr[...] = jnp.zeros_like(c_scr)
        n_scr[...] = jnp.zeros_like(n_scr)
        m_scr[...] = jnp.full_like(m_scr, NEG_INF)

    row = lax.broadcasted_iota(I32, (ch, ch), 0)
    col = lax.broadcasted_iota(I32, (ch, ch), 1)
    lane_lo = lax.broadcasted_iota(I32, (ch, 2 * dh), 1) < dh
    row_lo = lax.broadcasted_iota(I32, (2 * dh, 2 * dh), 0) < dh
    col_lo = lax.broadcasted_iota(I32, (2 * dh, 2 * dh), 1) < dh
    vec_lo = lax.broadcasted_iota(I32, (1, 2 * dh), 1) < dh
    pair = lambda a, b: jnp.where(lane_lo, a, b)
    refs = ((qf, kf, vf, gf, hf_ref), (qb, kb, vb, gb, hb_ref))
    jobs = [(d, p) for d in range(2) for p in range(nh // 2)]
    sl = lambda p: slice(2 * dh * p, 2 * dh * (p + 1))

    qk = {}
    for d, p in jobs:
        q2, k2 = refs[d][0][:, sl(p)], refs[d][1][:, sl(p)]
        zero = jnp.zeros_like(q2)
        q_st = jnp.concatenate([jnp.where(lane_lo, q2, zero), jnp.where(lane_lo, zero, q2)], axis=0)
        qk[d, p] = lax.dot_general(q_st, k2, (((1,), (1,)), ((), ())), preferred_element_type=F32)

    st = {}
    for d in range(2):
        seen = (col <= row) if d == 0 else (col >= row)
        tri = seen.astype(F32)
        tri_t = ((row <= col) if d == 0 else (row >= col)).astype(F32)
        last = ch - 1 if d == 0 else 0
        g = refs[d][3][...] + bias_ref[...]
        g_t = g.T
        lo = 2 * nh * d
        li_col, lf_col = g[:, lo:lo + nh], _log_sigmoid(g[:, lo + nh:lo + 2 * nh])
        li_row, lf_row = g_t[lo:lo + nh, :], _log_sigmoid(g_t[lo + nh:lo + 2 * nh, :])
        b_col = jnp.dot(tri, lf_col, precision=HIGHEST, preferred_element_type=F32)
        b_row = jnp.dot(lf_row, tri_t, precision=HIGHEST, preferred_element_type=F32)
        for h in range(nh):
            bc, br = b_col[:, h:h + 1], b_row[h:h + 1, :]
            lir, lic = li_row[h:h + 1, :], li_col[:, h:h + 1]
            b_last = br[:, last:last + 1]
            m_prev = m_scr[d, h // 2][:, (h % 2) * dh:(h % 2) * dh + 1]
            d_log = jnp.where(seen, bc - br + lir, NEG_INF)
            inter_log = bc + m_prev
            m_t = jnp.maximum(inter_log, jnp.max(d_log, axis=1, keepdims=True))
            w_log = b_last - bc + lic
            m_new = jnp.maximum(b_last + m_prev, jnp.max(w_log, axis=0, keepdims=True))
            st[d, h] = dict(dmat=jnp.exp(d_log - m_t), inter=jnp.exp(inter_log - m_t), floor=jnp.exp(-m_t),
                            m_new=m_new, decay=jnp.exp(b_last + m_prev - m_new), wn=jnp.exp(w_log - m_new))

    mm = {}
    for d, p in jobs:
        a, b = st[d, 2 * p], st[d, 2 * p + 1]
        q2, k2, v2 = refs[d][0][:, sl(p)], refs[d][1][:, sl(p)], refs[d][2][:, sl(p)]
        s = qk[d, p] * jnp.concatenate([a["dmat"], b["dmat"]], axis=0)
        kw = k2.astype(F32) * pair(a["wn"], b["wn"])
        mm[d, p] = dict(
            s_sum=jnp.sum(s, axis=1, keepdims=True), kw_sum=jnp.sum(kw, axis=0, keepdims=True),
            sv=jnp.dot(s.astype(BF16), v2, preferred_element_type=F32),
            q_c=jnp.dot(q2, c_scr[d, p].astype(BF16), preferred_element_type=F32),
            kv=lax.dot_general(kw.astype(BF16), v2, (((0,), (0,)), ((), ())), preferred_element_type=F32))

    for d, p in jobs:
        a, b, r = st[d, 2 * p], st[d, 2 * p + 1], mm[d, p]
        q2 = refs[d][0][:, sl(p)]
        n_prev = n_scr[d, p]
        num = pair(a["inter"], b["inter"]) * r["q_c"] + jnp.where(lane_lo, r["sv"][:ch], r["sv"][ch:])
        qn = q2.astype(F32) * n_prev
        qn_a = jnp.sum(jnp.where(lane_lo, qn, 0.0), axis=1, keepdims=True)
        qn_b = jnp.sum(jnp.where(lane_lo, 0.0, qn), axis=1, keepdims=True)
        den_a = jnp.maximum(jnp.abs(a["inter"] * qn_a + r["s_sum"][:ch]), a["floor"])
        den_b = jnp.maximum(jnp.abs(b["inter"] * qn_b + r["s_sum"][ch:]), b["floor"])
        refs[d][4][:, sl(p)] = num / pair(den_a, den_b)
        c_scr[d, p] = (jnp.where(row_lo, a["decay"], b["decay"]) * c_scr[d, p]
                       + jnp.where(row_lo == col_lo, r["kv"], 0.0))
        n_scr[d, p] = jnp.where(vec_lo, a["decay"], b["decay"]) * n_prev + r["kw_sum"]
        m_scr[d, p] = jnp.where(vec_lo, a["m_new"], b["m_new"])


def _mlstm(q, k, v, gates, bias, b, lc, s):
    r, w = q.shape
    ch = _tile(math.gcd(lc, s), MLSTM_CHUNKS)
    ncc, ncx = lc // ch, s // ch
    base = b * ncc

    def fwd(i, c):
        return jnp.where(c < ncc, i * ncc + c, base + i * ncx + (c - ncc)), 0

    def bwd(i, c):
        return jnp.where(c < ncc, i * ncc + (ncc - 1 - c), base + i * ncx + (ncx - 1 - (c - ncc))), 0

    spec = lambda width, m: pl.BlockSpec((ch, width), m)
    npair = MLSTM_HEADS // 2
    return pl.pallas_call(
        _mlstm_kernel,
        out_shape=[jax.ShapeDtypeStruct((r, w), F32)] * 2,
        grid=(b, ncc + ncx),
        in_specs=[spec(w, fwd), spec(w, fwd), spec(w, fwd), spec(LANES, fwd),
                  spec(w, bwd), spec(w, bwd), spec(w, bwd), spec(LANES, bwd),
                  pl.BlockSpec((1, LANES), lambda i, c: (0, 0))],
        out_specs=[spec(w, fwd), spec(w, bwd)],
        scratch_shapes=[pltpu.VMEM((2, npair, 2 * HEAD_DIM, 2 * HEAD_DIM), F32),
                        pltpu.VMEM((2, npair, 1, 2 * HEAD_DIM), F32),
                        pltpu.VMEM((2, npair, 1, 2 * HEAD_DIM), F32)],
        compiler_params=_params("parallel", "arbitrary"),
        name="mlstm_scan",
    )(q, k, v, gates, q, k, v, gates, bias)


def _mlstm_out_kernel(hf_ref, hb_ref, o_ref, nrm_ref, ones_ref, y_ref):
    h = hf_ref[...] + hb_ref[...]
    ms = jnp.dot((h * h).astype(BF16), ones_ref[...], preferred_element_type=F32) * (1.0 / HEAD_DIM)
    hn = h * lax.rsqrt(ms + EPS) * nrm_ref[...]
    y_ref[...] = (jax.nn.sigmoid(o_ref[...].astype(F32)) * hn).astype(y_ref.dtype)


def _head_ones(width):
    idx = np.arange(width) // HEAD_DIM
    return jnp.asarray(idx[:, None] == idx[None, :], BF16)


def _mlstm_out(hf, hb, o, norm, tm):
    r, w = hf.shape
    blk = pl.BlockSpec((tm, w), lambda i: (i, 0))
    return pl.pallas_call(
        _mlstm_out_kernel,
        out_shape=jax.ShapeDtypeStruct((r, w), BF16),
        grid=(r // tm,),
        in_specs=[blk, blk, blk,
                  pl.BlockSpec((1, w), lambda i: (0, 0)),
                  pl.BlockSpec((w, w), lambda i: (0, 0))],
        out_specs=blk,
        compiler_params=_params("parallel"),
        name="mlstm_out",
    )(hf, hb, o, norm.reshape(1, w), _head_ones(w))


def _swa_kernel(q_ref, k0, k1, k2, k3, v0, v1, v2, v3, kc_ref, vc_ref, sink_ref, o_ref, *, seq):
    w, dh = SWA_BLOCK, HEAD_DIM
    tq = q_ref.shape[1]
    g = q_ref.shape[0] // dh
    i = pl.program_id(2)
    k_loc = jnp.concatenate([k0[0, 0], k1[0, 0], k2[0, 0], k3[0, 0]], axis=0)
    v_loc = jnp.concatenate([v0[0, 0], v1[0, 0], v2[0, 0], v3[0, 0]], axis=1)
    kpos = (2 * i - 1) * w + lax.broadcasted_iota(I32, (4 * w, tq), 0)
    qpos = i * tq + lax.broadcasted_iota(I32, (4 * w, tq), 1)
    valid = (jnp.abs(qpos - kpos) <= SWA_WINDOW) & (kpos >= 0) & (kpos < seq)
    qs = [q_ref[h * dh:(h + 1) * dh, :] for h in range(g)]
    s_loc = [jnp.where(valid, jnp.dot(k_loc, qs[h], preferred_element_type=F32), NEG_INF) for h in range(g)]
    s_ctx = [jnp.dot(kc_ref[0, 0], qs[h], preferred_element_type=F32) for h in range(g)]
    outs = []
    for h in range(g):
        sink = sink_ref[0, h:h + 1, :]
        m = jnp.maximum(jnp.maximum(jnp.max(s_loc[h], axis=0, keepdims=True),
                                    jnp.max(s_ctx[h], axis=0, keepdims=True)), sink)
        p_loc = jnp.exp2(s_loc[h] - m)
        p_ctx = jnp.exp2(s_ctx[h] - m)
        den = jnp.sum(p_loc, axis=0, keepdims=True) + jnp.sum(p_ctx, axis=0, keepdims=True) + jnp.exp2(sink - m)
        o = (jnp.dot(v_loc, p_loc.astype(BF16), preferred_element_type=F32)
             + jnp.dot(vc_ref[0, 0], p_ctx.astype(BF16), preferred_element_type=F32))
        outs.append(o / den)
    o_ref[...] = jnp.concatenate(outs, axis=0).T.astype(o_ref.dtype)


def _swa(qt, k, vt, k_ctx, vt_ctx, sink2, b, s):
    hkv, dh = k.shape[1], k.shape[3]
    nh = qt.shape[0] // dh
    g = nh // hkv
    w = SWA_BLOCK
    tq = 2 * w
    nb, nq = s // w, s // tq
    lc = k_ctx.shape[2]
    clampi = lambda j: jnp.clip(j, 0, nb - 1)
    kspec = lambda o: pl.BlockSpec((1, 1, w, dh), lambda bi, hi, i: (bi, hi, clampi(2 * i + o), 0))
    vspec = lambda o: pl.BlockSpec((1, 1, dh, w), lambda bi, hi, i: (bi, hi, 0, clampi(2 * i + o)))
    return pl.pallas_call(
        functools.partial(_swa_kernel, seq=s),
        out_shape=jax.ShapeDtypeStruct((b * s, nh * dh), BF16),
        grid=(b, hkv, nq),
        in_specs=[pl.BlockSpec((g * dh, tq), lambda bi, hi, i: (hi, bi * nq + i)),
                  kspec(-1), kspec(0), kspec(1), kspec(2), vspec(-1), vspec(0), vspec(1), vspec(2),
                  pl.BlockSpec((1, 1, lc, dh), lambda bi, hi, i: (bi, hi, 0, 0)),
                  pl.BlockSpec((1, 1, dh, lc), lambda bi, hi, i: (bi, hi, 0, 0)),
                  pl.BlockSpec((1, g, tq), lambda bi, hi, i: (hi, 0, 0))],
        out_specs=pl.BlockSpec((tq, g * dh), lambda bi, hi, i: (bi * nq + i, hi)),
        compiler_params=_params("parallel", "parallel", "arbitrary"),
        name="window_attention",
    )(qt, k, k, k, k, vt, vt, vt, vt, k_ctx, vt_ctx, sink2)


def _ctx_attn_kernel(q_ref, k_ref, v_ref, sink_ref, o_ref, *, use_sink):
    g, lq, dh = q_ref.shape[1:]
    q = q_ref[0].reshape(g * lq, dh)
    s = lax.dot_general(q, k_ref[0, 0], (((1,), (1,)), ((), ())), preferred_element_type=F32) * ATTN_SCALE
    m = jnp.max(s, axis=1, keepdims=True)
    if use_sink:
        m = jnp.maximum(m, sink_ref[0])
    p = jnp.exp(s - m)
    den = jnp.sum(p, axis=1, keepdims=True)
    if use_sink:
        den = den + jnp.exp(sink_ref[0] - m)
    o = jnp.dot(p.astype(BF16), v_ref[0, 0], preferred_element_type=F32) / den
    o_ref[0] = o.reshape(g, lq, dh).astype(o_ref.dtype)


def _ctx_attn(q, k, v, sink_col, use_sink):
    b, nh, lq, dh = q.shape
    hkv = k.shape[1]
    g = nh // hkv
    qblk = pl.BlockSpec((1, g, lq, dh), lambda bi, hi: (bi, hi, 0, 0))
    kblk = pl.BlockSpec((1, 1, k.shape[2], dh), lambda bi, hi: (bi, hi, 0, 0))
    return pl.pallas_call(
        functools.partial(_ctx_attn_kernel, use_sink=use_sink),
        out_shape=jax.ShapeDtypeStruct(q.shape, BF16),
        grid=(b, hkv),
        in_specs=[qblk, kblk, kblk, pl.BlockSpec((1, g * lq, 1), lambda bi, hi: (hi, 0, 0))],
        out_specs=qblk,
        compiler_params=_params("parallel", "parallel"),
        name="context_attention",
    )(q, k, v, sink_col)


def _flash_kernel(q_ref, k_ref, v_ref, o_ref, m_scr, l_scr, acc_scr, sa_scr, sb_scr, *, tk):
    dh = HEAD_DIM
    g = q_ref.shape[0] // dh
    n = k_ref.shape[2] // tk
    m_scr[...] = jnp.full_like(m_scr, NEG_INF)
    l_scr[...] = jnp.zeros_like(l_scr)
    acc_scr[...] = jnp.zeros_like(acc_scr)

    def scores(j, dst):
        kk = k_ref[0, 0, pl.ds(pl.multiple_of(j * tk, tk), tk), :]
        for h in range(g):
            dst[h] = jnp.dot(kk, q_ref[h * dh:(h + 1) * dh, :], preferred_element_type=F32)

    def update(j, src):
        vv = v_ref[0, 0, :, pl.ds(pl.multiple_of(j * tk, tk), tk)]
        ss = [src[h] for h in range(g)]
        m_olds = [m_scr[h] for h in range(g)]
        m_news = [jnp.maximum(m_olds[h], jnp.max(ss[h], axis=0, keepdims=True)) for h in range(g)]
        ps = [jnp.exp2(ss[h] - m_news[h]) for h in range(g)]
        pvs = [jnp.dot(vv, ps[h].astype(BF16), preferred_element_type=F32) for h in range(g)]
        for h in range(g):
            alpha = jnp.exp2(m_olds[h] - m_news[h])
            l_scr[h] = alpha * l_scr[h] + jnp.sum(ps[h], axis=0, keepdims=True)
            acc_scr[h] = alpha * acc_scr[h] + pvs[h]
            m_scr[h] = m_news[h]

    scores(0, sa_scr)

    def body(i, carry):
        scores(2 * i + 1, sb_scr)
        update(2 * i, sa_scr)
        scores(2 * i + 2, sa_scr)
        update(2 * i + 1, sb_scr)
        return carry

    lax.fori_loop(0, (n - 1) // 2, body, 0)
    if n % 2 == 1:
        update(n - 1, sa_scr)
    else:
        scores(n - 1, sb_scr)
        update(n - 2, sa_scr)
        update(n - 1, sb_scr)
    out = (acc_scr[...] / l_scr[...]).reshape(g * dh, -1)
    o_ref[...] = out.T.astype(o_ref.dtype)


def _flash(qt, k, vt, b, s, tq=None, tk=None):
    w = qt.shape[0]
    hkv, t, dh = k.shape[1], k.shape[2], k.shape[3]
    g = w // dh // hkv
    tq = tq or _tile(s, (256, 128))
    tk = tk or _tile(t, (256, 128))
    nq = s // tq
    return pl.pallas_call(
        functools.partial(_flash_kernel, tk=tk),
        out_shape=jax.ShapeDtypeStruct((b * s, w), BF16),
        grid=(b, hkv, nq),
        in_specs=[pl.BlockSpec((g * dh, tq), lambda bi, hi, i: (hi, bi * nq + i)),
                  pl.BlockSpec((1, 1, t, dh), lambda bi, hi, i: (bi, hi, 0, 0)),
                  pl.BlockSpec((1, 1, dh, t), lambda bi, hi, i: (bi, hi, 0, 0))],
        out_specs=pl.BlockSpec((tq, g * dh), lambda bi, hi, i: (bi * nq + i, hi)),
        scratch_shapes=[pltpu.VMEM((g, 1, tq), F32), pltpu.VMEM((g, 1, tq), F32), pltpu.VMEM((g, dh, tq), F32),
                        pltpu.VMEM((g, tk, tq), F32), pltpu.VMEM((g, tk, tq), F32)],
        compiler_params=_params("parallel", "parallel", "arbitrary"),
        name="dense_attention",
    )(qt, k, vt)


def _bmm_kernel(a_ref, b_ref, o_ref):
    o_ref[0] = jnp.dot(a_ref[0], b_ref[0], preferred_element_type=F32).astype(o_ref.dtype)


def _group_matmul(a, bmat, name):
    g, m, k = a.shape
    n = bmat.shape[2]
    tm = _tile(m, (1024, 512, 256, 128, 64, 32, 16, 8))
    return pl.pallas_call(
        _bmm_kernel,
        out_shape=jax.ShapeDtypeStruct((g, m, n), F32),
        grid=(g, m // tm),
        in_specs=[pl.BlockSpec((1, tm, k), lambda gi, i: (gi, i, 0)),
                  pl.BlockSpec((1, k, n), lambda gi, i: (gi, 0, 0))],
        out_specs=pl.BlockSpec((1, tm, n), lambda gi, i: (gi, i, 0)),
        compiler_params=_params("parallel", "parallel"),
        name=name,
    )(a, bmat)


def _s5_scan_kernel(sre_ref, sim_ref, are_ref, aim_ref, zre_ref, zim_ref):
    nd, nsteps = sre_ref.shape[0], sre_ref.shape[1]
    for d in range(nd):
        a_re, a_im = are_ref[d], aim_ref[d]

        def body(i, carry, d=d, a_re=a_re, a_im=a_im):
            z_re, z_im = carry
            zre_ref[d, i] = z_re
            zim_ref[d, i] = z_im
            return (a_re * z_re - a_im * z_im + sre_ref[d, i], a_re * z_im + a_im * z_re + sim_ref[d, i])

        zero = jnp.zeros(sre_ref.shape[2:], F32)
        lax.fori_loop(0, nsteps, body, (zero, zero))


def _s5_scan(s_re, s_im, a_re, a_im):
    full = lambda arr: pl.BlockSpec(arr.shape, lambda i: (0,) * arr.ndim)
    return pl.pallas_call(
        _s5_scan_kernel,
        out_shape=[jax.ShapeDtypeStruct(s_re.shape, F32)] * 2,
        grid=(1,),
        in_specs=[full(s_re), full(s_im), full(a_re), full(a_im)],
        out_specs=[full(s_re), full(s_im)],
        compiler_params=_params("arbitrary"),
        name="s5_chunk_scan",
    )(s_re, s_im, a_re, a_im)


def _s5_glu_kernel(y_ref, u_ref, d_ref, w_ref, b_ref, o_ref):
    y = y_ref[...] + d_ref[...] * u_ref[...]
    y = jax.nn.gelu(y)
    gate = jnp.dot(y.astype(BF16), w_ref[...], preferred_element_type=F32) + b_ref[...]
    o_ref[...] = (y * jax.nn.sigmoid(gate)).astype(o_ref.dtype)


def _s5_glu(y, u, d_skip, w_glu, b_glu, tm):
    r, c = y.shape
    blk = pl.BlockSpec((tm, c), lambda i: (i, 0))
    vec = pl.BlockSpec((1, c), lambda i: (0, 0))
    return pl.pallas_call(
        _s5_glu_kernel,
        out_shape=jax.ShapeDtypeStruct((r, c), BF16),
        grid=(r // tm,),
        in_specs=[blk, blk, vec, pl.BlockSpec((c, c), lambda i: (0, 0)), vec],
        out_specs=blk,
        compiler_params=_params("parallel"),
        name="s5_readout_glu",
    )(y, u, d_skip.reshape(1, c), w_glu.astype(BF16), b_glu.reshape(1, c))


def _s5_tables(lam_re, lam_im, log_dt, b_re, b_im, c_re, c_im):
    ln, p, gc = S5_CHUNK, S5_STATE, S5_GROUP
    lam = lax.complex(lam_re.astype(F32), lam_im.astype(F32))
    dt = jnp.exp(log_dt.astype(F32))[..., None]
    a_bar = jnp.exp(lam * dt)
    b_scale = (a_bar - 1.0) / lam
    b_mat = lax.complex(b_re.astype(F32), b_im.astype(F32))
    c_mat = lax.complex(c_re.astype(F32), c_im.astype(F32))
    tau = jnp.arange(ln + 1, dtype=F32)
    apow = jnp.exp((lam * dt)[:, :, None, :] * tau[None, None, :, None])
    drive = b_scale[..., None] * b_mat[None]
    kern = jnp.real(jnp.einsum('gcp,dgtp,dgpe->dgtce', c_mat, apow[:, :, :ln], drive))
    kc = jnp.concatenate([kern[1, :, :0:-1], (kern[0, :, :1] + kern[1, :, :1]), kern[0, :, 1:]], axis=1)
    kc_e = kc.transpose(0, 3, 1, 2).reshape(S5_GROUPS, gc, (2 * ln - 1) * gc)
    toep = jnp.stack([kc_e[:, :, (ln - 1 - s) * gc:(2 * ln - 1 - s) * gc] for s in range(ln)], axis=1)
    toep = toep.reshape(S5_GROUPS, ln * gc, ln * gc)
    w_f = apow[0, :, ln - 1::-1][:, :ln, :, None] * drive[0][:, None]
    w_b = apow[1, :, :ln, :, None] * drive[1][:, None]
    def m_in(wc):
        wt = wc.transpose(0, 1, 3, 2).reshape(S5_GROUPS, ln * gc, p)
        return jnp.concatenate([jnp.real(wt), jnp.imag(wt)], axis=-1)
    min_all = jnp.concatenate([m_in(w_f), m_in(w_b)], axis=-1)
    o_f = c_mat[:, None] * apow[0, :, 1:ln + 1][:, :, None, :]
    o_b = c_mat[:, None] * apow[1, :, ln:0:-1][:, :, None, :]
    def m_out(oc):
        ot = oc.transpose(0, 3, 1, 2).reshape(S5_GROUPS, p, ln * gc)
        return jnp.concatenate([jnp.real(ot), -jnp.imag(ot)], axis=1)
    rhs = jnp.concatenate([toep, m_out(o_f), m_out(o_b)], axis=1)
    a_l = apow[:, :, ln].reshape(2, 1, S5_GROUPS * p)
    return min_all.astype(BF16), rhs.astype(BF16), jnp.real(a_l), jnp.imag(a_l)


def _s5_mixer(u_x, u_c, tables, d_skip, w_glu, b_glu):
    min_all, rhs, a_re, a_im = tables
    b, s, _ = u_x.shape
    ln, p, gc, ng = S5_CHUNK, S5_STATE, S5_GROUP, S5_GROUPS

    def chunks(u):
        nc = u.shape[1] // ln
        return u.reshape(b, nc, ln, ng, gc).transpose(3, 0, 1, 2, 4).reshape(ng, b * nc, ln * gc).astype(BF16), nc

    ux, ncx = chunks(u_x)
    uc, ncc = chunks(u_c)
    sx = _group_matmul(ux, min_all, "s5_local_state").reshape(ng, b, ncx, 4, p)
    sc = _group_matmul(uc, min_all, "s5_local_state_ctx").reshape(ng, b, ncc, 4, p)

    def scan_order(part):
        f = jnp.concatenate([sc[:, :, :, part], sx[:, :, :, part]], axis=2)
        r = jnp.concatenate([sc[:, :, ::-1, part + 2], sx[:, :, ::-1, part + 2]], axis=2)
        return jnp.stack([f, r]).transpose(0, 3, 2, 1, 4).reshape(2, ncc + ncx, b, ng * p)

    z_re, z_im = _s5_scan(scan_order(0), scan_order(1), a_re, a_im)

    def latent(z, d):
        zl = z[d, ncc:]
        if d == 1:
            zl = zl[::-1]
        return zl.reshape(ncx, b, ng, p).transpose(2, 1, 0, 3).reshape(ng, b * ncx, p)

    lhs = jnp.concatenate([ux, latent(z_re, 0).astype(BF16), latent(z_im, 0).astype(BF16),
                           latent(z_re, 1).astype(BF16), latent(z_im, 1).astype(BF16)], axis=-1)
    y = _group_matmul(lhs, rhs, "s5_outputs")
    y = y.reshape(ng, b, ncx, ln, gc).transpose(1, 2, 3, 0, 4).reshape(b * s, S5_CHANNELS)
    return _s5_glu(y, u_x.reshape(b * s, S5_CHANNELS), d_skip, w_glu, b_glu, _tile(b * s, (1024, 512, 256)))


def _out_proj_kernel(x_ref, a_ref, b_ref, wa_ref, wb_ref, gt_ref, g_ref, sc_ref, sh_ref, wr_ref, br_ref,
                     xo_ref, xn_ref, lg_ref):
    y = (jnp.dot(a_ref[...], wa_ref[...], preferred_element_type=F32)
         + jnp.dot(b_ref[...], wb_ref[...], preferred_element_type=F32))
    x = x_ref[...] + gt_ref[0] * y
    xo_ref[...] = x
    xn = x * lax.rsqrt(jnp.mean(x * x, axis=-1, keepdims=True) + EPS) * g_ref[...]
    xn = xn * (1.0 + sc_ref[0]) + sh_ref[0]
    hi = xn.astype(BF16)
    xn_ref[...] = hi
    lo = (xn - hi.astype(F32)).astype(BF16)
    part = jnp.dot(hi, wr_ref[...], preferred_element_type=F32)
    part = part[:, :LANES] + part[:, LANES:] + jnp.dot(lo, wr_ref[:, :LANES], preferred_element_type=F32)
    lg_ref[...] = part[:, :lg_ref.shape[1]] + br_ref[...]


def _out_proj(x, row0, a, bmix, wa, wb, gate, g_ffn, scale, shift, mod_map, w_router, b_router, tm):
    r = a.shape[0]
    d = x.shape[1]
    t0 = row0 // tm
    ne = w_router.shape[1]
    w_hi = w_router.astype(BF16)
    w_lo = (w_router - w_hi.astype(F32)).astype(BF16)
    w_router = jnp.concatenate([_pad_cols(w_hi), _pad_cols(w_lo)], axis=1)
    row = lambda w: pl.BlockSpec((tm, w), lambda i: (i, 0))
    mod = pl.BlockSpec((1, 1, d), lambda i: (mod_map(i), 0, 0))
    full = lambda arr: pl.BlockSpec(arr.shape, lambda i: (0, 0))
    return pl.pallas_call(
        _out_proj_kernel,
        out_shape=[jax.ShapeDtypeStruct((r, d), F32), jax.ShapeDtypeStruct((r, d), BF16),
                   jax.ShapeDtypeStruct((r, ne), F32)],
        grid=(r // tm,),
        in_specs=[pl.BlockSpec((tm, d), lambda i: (i + t0, 0)), row(a.shape[1]), row(bmix.shape[1]),
                  full(wa), full(wb), mod, pl.BlockSpec((1, d), lambda i: (0, 0)), mod, mod,
                  full(w_router), pl.BlockSpec((1, ne), lambda i: (0, 0))],
        out_specs=[row(d), row(d), row(ne)],
        compiler_params=_params("parallel"),
        name="out_proj",
    )(x, a, bmix, wa, wb, gate, g_ffn.reshape(1, d), scale, shift, w_router, b_router.reshape(1, ne))


def _route_kernel(lg_ref, idx_ref, gate_ref, pos_ref, cnt_ref, carry):
    tm, ne = lg_ref.shape

    @pl.when(pl.program_id(0) == 0)
    def _():
        carry[...] = jnp.zeros_like(carry)

    work = lg_ref[...]
    lane = lax.broadcasted_iota(I32, (tm, ne), 1).astype(F32)
    out_lane = lax.broadcasted_iota(I32, (tm, LANES), 1)
    vals, hots, idx_out = [], [], jnp.zeros((tm, LANES), I32)
    for kk in range(TOP_K):
        mx = jnp.max(work, axis=1, keepdims=True)
        idx = jnp.min(jnp.where(work == mx, lane, float(ne)), axis=1, keepdims=True)
        hot = lane == idx
        work = jnp.where(hot, -jnp.inf, work)
        vals.append(mx)
        hots.append(hot)
        idx_out = jnp.where(out_lane == kk, idx.astype(I32), idx_out)
    exps = [jnp.exp(vv - vals[0]) for vv in vals]
    tot = exps[0] + exps[1] + exps[2] + exps[3]
    multi = (hots[0] | hots[1] | hots[2] | hots[3]).astype(BF16)
    r_i = lax.broadcasted_iota(I32, (tm, tm), 0)
    c_i = lax.broadcasted_iota(I32, (tm, tm), 1)
    before = (r_i > c_i).astype(BF16)
    prefix = jnp.dot(before, multi, preferred_element_type=F32) + carry[...]
    gate_out = jnp.zeros((tm, LANES), F32)
    pos_out = jnp.zeros((tm, LANES), I32)
    for kk in range(TOP_K):
        gate_out = jnp.where(out_lane == kk, exps[kk] / tot, gate_out)
        pos = jnp.sum(jnp.where(hots[kk], prefix, 0.0), axis=1, keepdims=True).astype(I32)
        pos_out = jnp.where(out_lane == kk, pos, pos_out)
    idx_ref[...] = idx_out
    gate_ref[...] = gate_out
    pos_ref[...] = pos_out
    carry[...] = carry[...] + jnp.sum(multi.astype(F32), axis=0, keepdims=True)
    cnt_ref[...] = carry[...]


def _route(logits, tm, row0, n):
    ne = logits.shape[1]
    t0 = row0 // tm
    wide = pl.BlockSpec((tm, LANES), lambda i: (i, 0))
    return pl.pallas_call(
        _route_kernel,
        out_shape=[jax.ShapeDtypeStruct((n, LANES), I32), jax.ShapeDtypeStruct((n, LANES), F32),
                   jax.ShapeDtypeStruct((n, LANES), I32), jax.ShapeDtypeStruct((1, ne), F32)],
        grid=(n // tm,),
        in_specs=[pl.BlockSpec((tm, ne), lambda i: (i + t0, 0))],
        out_specs=[wide, wide, wide, pl.BlockSpec((1, ne), lambda i: (0, 0))],
        scratch_shapes=[pltpu.VMEM((1, ne), F32)],
        compiler_params=_params("arbitrary"),
        name="moe_route",
    )(logits)


def _expert_kernel(be_ref, nb_ref, x_ref, wgu_ref, bgu_ref, wd_ref, bd_ref, perm_ref, o_ref, wgu_bf, wd_bf):
    i = pl.program_id(0)
    fresh = jnp.logical_or(i == 0, be_ref[i] != be_ref[jnp.maximum(i - 1, 0)])

    @pl.when(jnp.logical_and(i < nb_ref[0], fresh))
    def _():
        tile = perm_ref.shape[0]
        for t in range(wgu_ref.shape[2] // tile):
            sl = slice(t * tile, (t + 1) * tile)
            wgu_bf[:, sl] = jnp.dot(wgu_ref[0, :, sl].astype(BF16), perm_ref[...],
                                    preferred_element_type=F32).astype(BF16)
        wd_bf[...] = wd_ref[0].astype(BF16)

    @pl.when(i < nb_ref[0])
    def _():
        h = jnp.dot(x_ref[...], wgu_bf[...], preferred_element_type=F32) + bgu_ref[0]
        acts = []
        for t in range(h.shape[1] // (2 * LANES)):
            glu = jnp.minimum(h[:, 2 * LANES * t:2 * LANES * t + LANES], SWIGLU_LIMIT)
            lin = jnp.clip(h[:, 2 * LANES * t + LANES:2 * LANES * (t + 1)], -SWIGLU_LIMIT, SWIGLU_LIMIT)
            acts.append((glu * jax.nn.sigmoid(SWIGLU_ALPHA * glu) * (lin + 1.0)).astype(BF16))
        act = jnp.concatenate(acts, axis=1)
        o_ref[...] = (jnp.dot(act, wd_bf[...], preferred_element_type=F32) + bd_ref[0]).astype(o_ref.dtype)

    @pl.when(i >= nb_ref[0])
    def _():
        o_ref[...] = jnp.zeros_like(o_ref)


def _experts(xs, blk_e, n_used, wgu, bgu, wd, bd):
    n, d = xs.shape
    ff2 = wgu.shape[2]
    nblk = n // MOE_ROWS
    tile = 2 * LANES
    j = np.arange(tile)
    col = np.where(j < LANES, 2 * j, 2 * (j - LANES) + 1)
    perm = jnp.asarray(np.arange(tile)[:, None] == col[None, :], BF16)
    wspec = lambda shape: pl.BlockSpec((1,) + shape, lambda i, be, nb: (be[i], 0, 0))
    return pl.pallas_call(
        _expert_kernel,
        out_shape=jax.ShapeDtypeStruct((n, d), BF16),
        grid_spec=pltpu.PrefetchScalarGridSpec(
            num_scalar_prefetch=2,
            grid=(nblk,),
            in_specs=[pl.BlockSpec((MOE_ROWS, d), lambda i, be, nb: (i, 0)),
                      wspec((d, ff2)), wspec((1, ff2)), wspec((ff2 // 2, d)), wspec((1, d)),
                      pl.BlockSpec((tile, tile), lambda i, be, nb: (0, 0))],
            out_specs=pl.BlockSpec((MOE_ROWS, d), lambda i, be, nb: (i, 0)),
            scratch_shapes=[pltpu.VMEM((d, ff2), BF16), pltpu.VMEM((ff2 // 2, d), BF16)]),
        compiler_params=_params("arbitrary"),
        name="moe_experts",
    )(blk_e, n_used, xs, wgu, bgu, wd, bd, perm)


def _combine_kernel(x_ref, y_ref, gate_ref, gt_ref, g_ref, o_ref, *, final_norm):
    gates = gate_ref[...]
    y = gates[:, 0:1] * y_ref[0].astype(F32)
    for kk in range(1, TOP_K):
        y = y + gates[:, kk:kk + 1] * y_ref[kk].astype(F32)
    x = x_ref[...] + gt_ref[0] * y
    if final_norm:
        x = x * lax.rsqrt(jnp.mean(x * x, axis=-1, keepdims=True) + EPS) * g_ref[...]
    o_ref[...] = x


def _combine_alias_kernel(x_ref, y_ref, gate_ref, gt_ref, g_ref, prev_ref, o_ref, *, final_norm):
    del prev_ref
    _combine_kernel(x_ref, y_ref, gate_ref, gt_ref, g_ref, o_ref, final_norm=final_norm)


def _combine(x, y4, gates, gate_mod, mod_map, g_final, final_norm, tm, row0, prev):
    r, d = x.shape
    n = y4.shape[1]
    t0 = row0 // tm
    in_specs = [pl.BlockSpec((tm, d), lambda i: (i + t0, 0)),
                pl.BlockSpec((TOP_K, tm, d), lambda i: (0, i, 0)),
                pl.BlockSpec((tm, LANES), lambda i: (i, 0)),
                pl.BlockSpec((1, 1, d), lambda i: (mod_map(i + t0), 0, 0)),
                pl.BlockSpec((1, d), lambda i: (0, 0))]
    args = [x, y4, gates, gate_mod, g_final.reshape(1, d)]
    body, aliases = _combine_kernel, {}
    if prev is not None:
        in_specs.append(pl.BlockSpec(memory_space=pl.ANY))
        args.append(prev)
        body, aliases = _combine_alias_kernel, {len(args) - 1: 0}
    return pl.pallas_call(
        functools.partial(body, final_norm=final_norm),
        out_shape=jax.ShapeDtypeStruct((r, d), F32),
        grid=(n // tm,),
        in_specs=in_specs,
        out_specs=pl.BlockSpec((tm, d), lambda i: (i + t0, 0)),
        input_output_aliases=aliases,
        compiler_params=_params("parallel"),
        name="moe_combine",
    )(*args)


def _after(value, token):
    if token is None:
        return value
    return lax.optimization_barrier((value, token))[0]


def _moe(x, xn, logits, gate_mod, mod_map, weights, g_final, final_norm, tm):
    wgu, bgu, wd, bd = weights
    n, d = xn.shape
    nblk = n * TOP_K // MOE_ROWS + N_EXPERTS
    blk_row0 = jnp.arange(nblk, dtype=I32) * MOE_ROWS
    idx_w, gates_w, pos_w, counts = _route(logits, tm, 0, n)
    idx, pos = idx_w[:, :TOP_K], pos_w[:, :TOP_K]
    counts = counts[0].astype(I32)
    padded = (counts + MOE_ROWS - 1) // MOE_ROWS * MOE_ROWS
    pad_end = jnp.cumsum(padded)
    pad_start = pad_end - padded
    dest = pad_start[idx] + pos
    blk_e = jnp.minimum(jnp.sum((pad_end[None, :] <= blk_row0[:, None]).astype(I32), axis=1), N_EXPERTS - 1)
    n_used = (pad_end[-1:] // MOE_ROWS).astype(I32)
    key = idx * n + jnp.arange(n, dtype=I32)[:, None]
    tok_sorted = jnp.sort(key.reshape(-1)) % n
    grp_start = jnp.cumsum(counts) - counts
    slot = blk_row0[:, None] + jnp.arange(MOE_ROWS, dtype=I32)[None, :]
    compact = jnp.minimum(slot + (grp_start - pad_start)[blk_e][:, None], (grp_start + counts - 1)[blk_e][:, None])
    src = tok_sorted.at[jnp.clip(compact, 0, n * TOP_K - 1).reshape(-1)].get(mode="promise_in_bounds")
    xs = xn.at[src].get(mode="promise_in_bounds")
    wait = yield src
    ys = _experts(_after(xs, wait), blk_e, n_used, wgu, bgu, wd, bd)
    y4 = ys.at[dest.T.reshape(-1)].get(mode="promise_in_bounds")
    wait = yield ys
    y4 = _after(y4, wait).reshape(TOP_K, n, d)
    return _combine(x, y4, gates_w, gate_mod, mod_map, g_final, final_norm, tm, 0, None)


def _moe_weights(w_gate_up, b_gate_up, w_down, b_down):
    ne, d, ff2 = w_gate_up.shape
    b_tiled = b_gate_up.reshape(ne, ff2 // (2 * LANES), LANES, 2).swapaxes(-1, -2).reshape(ne, 1, ff2)
    return w_gate_up, b_tiled, w_down, b_down[:, None, :]


def _rope_tables(n_tokens):
    rows = n_tokens // GRID_W
    row = jnp.repeat(jnp.arange(rows, dtype=I32), GRID_W).astype(F32)
    col = jnp.tile(jnp.arange(GRID_W, dtype=I32), rows).astype(F32)
    inv = ROPE_BASE ** (-jnp.arange(0, ROPE_AXIS_DIM, 2, dtype=F32) / ROPE_AXIS_DIM)
    ang_r, ang_c = row[:, None] * inv, col[:, None] * inv
    cos = jnp.concatenate([jnp.cos(ang_r)] * 2 + [jnp.cos(ang_c)] * 2, axis=-1)
    sin = jnp.concatenate([jnp.sin(ang_r)] * 2 + [jnp.sin(ang_c)] * 2, axis=-1)
    return cos, sin


def _rot_perm():
    q = ROPE_AXIS_DIM // 2
    d = np.arange(HEAD_DIM)
    first = (d % ROPE_AXIS_DIM) < q
    perm = np.where(first, d + q, d - q)
    sign = np.where(first, -1.0, 1.0).astype(np.float32)
    return perm, sign


def _rot_cols(w, n_heads):
    perm, sign = _rot_perm()
    k = w.shape[0]
    wh = w.reshape(k, n_heads, HEAD_DIM)
    return (wh[:, :, perm] * sign).reshape(k, n_heads * HEAD_DIM)


def _to_heads(t, b, n_heads):
    return t.reshape(b, -1, n_heads, HEAD_DIM).transpose(0, 2, 1, 3)


def _from_heads(t):
    b, h, tt, dh = t.shape
    return t.transpose(0, 2, 1, 3).reshape(b * tt, h * dh)


def _layer_even(xa, mod, b, s, lc, g_mix, g_ffn, w_in, w_out, gate_bias, mlstm_norm, sink, router, moe_w):
    d = xa.shape[1]
    nctx = b * lc
    tm = _tile(math.gcd(nctx, s), (512, 256, 128))
    mod_map = lambda i: jnp.where(i * tm < nctx, b, (i * tm - nctx) // s)
    sh1, sc1, gt1, sh2, sc2, gt2 = [mod[:, j][:, None, :] for j in range(6)]

    hm, hs, hk = MLSTM_HEADS * HEAD_DIM, SWA_HEADS * HEAD_DIM, SWA_KV_HEADS * HEAD_DIM
    o0 = np.cumsum([0, hm, hm, hm, hm, 4 * MLSTM_HEADS, hs, hk, hk])
    seg = lambda j: w_in[:, o0[j]:o0[j + 1]]
    w_cat = jnp.concatenate([seg(0), seg(1) * ATTN_SCALE, seg(2), seg(3),
                             seg(6), _rot_cols(seg(6), SWA_KV_HEADS), seg(7), _pad_cols(seg(4))], axis=1).astype(BF16)
    w_q = jnp.concatenate([seg(5), _rot_cols(seg(5), SWA_HEADS)], axis=1).T.astype(BF16)
    widths = [hm, hm, hm, hm, hk, hk, hk, LANES]
    dts = [BF16] * 7 + [F32]
    qa, ka, va, oa, kb, kb_r, vb, gts, qt, qt_r = _in_proj(xa, g_mix, sc1, sh1, mod_map, w_cat, widths, dts, tm,
                                                           wt=w_q, t_widths=[hs, hs], t_dtypes=[BF16, BF16])

    hf, hb = _mlstm(qa, ka, va, gts, _pad_cols(gate_bias.astype(F32).reshape(1, -1)), b, lc, s)
    mix_a = _mlstm_out(hf, hb, oa, mlstm_norm.reshape(-1), tm)

    cos, sin = _rope_tables(s)
    ones = jnp.ones((HEAD_DIM,), F32)
    g = SWA_HEADS // SWA_KV_HEADS
    heads = lambda t, n: _to_heads(t, b, n)
    log2e = math.log2(math.e)
    q_x = _q_prep_t(qt, qt_r, nctx, b * s, cos.T, sin.T, ones, ones, ATTN_SCALE * log2e, norm=False)
    k_x = _qk_prep(heads(kb[nctx:], SWA_KV_HEADS), heads(kb_r[nctx:], SWA_KV_HEADS), cos, sin, ones, ones,
                   norm=False, rope=True, scale=1.0)
    k_c, v_c, v_x = heads(kb[:nctx], SWA_KV_HEADS), heads(vb[:nctx], SWA_KV_HEADS), heads(vb[nctx:], SWA_KV_HEADS)
    q_c = heads(qt[:, :nctx].T, SWA_HEADS)
    sink_h = sink.astype(F32).reshape(SWA_KV_HEADS, g, 1)
    att_x = _swa(q_x, k_x, v_x.swapaxes(-1, -2), k_c, v_c.swapaxes(-1, -2),
                 jnp.broadcast_to(sink_h * log2e, (SWA_KV_HEADS, g, 2 * SWA_BLOCK)), b, s)
    att_c = _ctx_attn(q_c, k_c, v_c, jnp.broadcast_to(sink_h[..., None], (SWA_KV_HEADS, g, lc, 1))
                      .reshape(SWA_KV_HEADS, g * lc, 1), True)
    mix_b = jnp.concatenate([_from_heads(att_c), att_x], axis=0)

    w_router, b_router = router
    xa, xn2, logits = _out_proj(xa, 0, mix_a, mix_b, w_out[:hm].astype(BF16), w_out[hm:].astype(BF16), gt1, g_ffn,
                                sc2, sh2, mod_map, w_router.astype(F32), b_router.astype(F32), tm)
    return (yield from _moe(xa, xn2, logits, gt2, mod_map, moe_w, g_ffn, False, tm))


def _layer_odd_last(xa, mod, b, s, lc, g_mix, g_ffn, w_in, w_out, s5_params, d_skip, w_glu, b_glu,
                    q_norm, k_norm, router, moe_w, g_final):
    nctx = b * lc
    tm = _tile(math.gcd(nctx, s), (512, 256, 128))
    mod_map = lambda i: jnp.where(i * tm < nctx, b, (i * tm - nctx) // s)
    lat_map = lambda i: i * tm // s
    sh1, sc1, gt1, sh2, sc2, gt2 = [mod[:, j][:, None, :] for j in range(6)]

    hq, hk = ATT_HEADS * HEAD_DIM, ATT_KV_HEADS * HEAD_DIM
    o1 = np.cumsum([0, S5_CHANNELS, hq, hk, hk])
    seg = lambda j: w_in[:, o1[j]:o1[j + 1]]
    kpad = lambda w: _pad_cols(w, 2 * LANES)
    w_cat = jnp.concatenate([seg(0), kpad(seg(2)), kpad(_rot_cols(seg(2), ATT_KV_HEADS)), kpad(seg(3))],
                            axis=1).astype(BF16)
    w_q = jnp.concatenate([seg(1), _rot_cols(seg(1), ATT_HEADS)], axis=1).T.astype(BF16)
    widths = [S5_CHANNELS, 2 * LANES, 2 * LANES, 2 * LANES]
    dts = [F32, BF16, BF16, BF16]
    u, k, k_r, v, qt, qt_r = _in_proj(xa, g_mix, sc1, sh1, mod_map, w_cat, widths, dts, tm,
                                      wt=w_q, t_widths=[hq, hq], t_dtypes=[BF16, BF16])

    mix_a = _s5_mixer(u[nctx:].reshape(b, s, S5_CHANNELS), u[:nctx].reshape(b, lc, S5_CHANNELS),
                      _s5_tables(*s5_params), d_skip, w_glu, b_glu)

    cos, sin = _rope_tables(s)
    perm, _ = _rot_perm()
    qn, kn = q_norm.astype(F32), k_norm.astype(F32)
    heads = lambda t, n: _to_heads(t[:, :n * HEAD_DIM], b, n)
    q_x = _q_prep_t(qt, qt_r, nctx, b * s, cos.T, sin.T, qn, qn[perm], ATTN_SCALE * math.log2(math.e))
    k_x = _qk_prep(heads(k[nctx:], ATT_KV_HEADS), heads(k_r[nctx:], ATT_KV_HEADS), cos, sin, kn, kn[perm],
                   norm=True, rope=True, scale=1.0)
    kc_raw = heads(k[:nctx], ATT_KV_HEADS)
    k_c = _qk_prep(kc_raw, kc_raw, cos[:lc], sin[:lc], kn, kn, norm=True, rope=False, scale=1.0)
    k_all = jnp.concatenate([k_c, k_x], axis=2)
    v_all = jnp.concatenate([heads(v[:nctx], ATT_KV_HEADS), heads(v[nctx:], ATT_KV_HEADS)], axis=2)
    mix_b = _flash(q_x, k_all, v_all.swapaxes(-1, -2), b, s)

    w_router, b_router = router
    hs = S5_CHANNELS
    x, xn2, logits = _out_proj(xa, nctx, mix_a, mix_b, w_out[:hs].astype(BF16), w_out[hs:].astype(BF16), gt1, g_ffn,
                               sc2, sh2, lat_map, w_router.astype(F32), b_router.astype(F32), tm)
    return (yield from _moe(x, xn2, logits, gt2, lat_map, moe_w, g_final, True, tm))


def kernel(x, c, ctx, c_ctx, l0_w_mod, l0_b_mod, l0_g_mix, l0_g_ffn, l0_w_in, l0_w_out, l0_gate_bias, l0_mlstm_norm, l0_sink, l0_w_router, l0_b_router, l0_w_gate_up, l0_b_gate_up, l0_w_down, l0_b_down, l1_w_mod, l1_b_mod, l1_g_mix, l1_g_ffn, l1_w_in, l1_w_out, l1_lam_re, l1_lam_im, l1_log_dt, l1_b_re, l1_b_im, l1_c_re, l1_c_im, l1_d_skip, l1_w_glu, l1_b_glu, l1_q_norm, l1_k_norm, l1_w_router, l1_b_router, l1_w_gate_up, l1_b_gate_up, l1_w_down, l1_b_down, g_final):
    b, s, d = x.shape
    lc = ctx.shape[1]
    cond = jnp.concatenate([c, c_ctx[None, :]], axis=0)
    cond = jnp.pad(cond, ((0, (-(b + 1)) % 8), (0, 0)))
    mod0 = _silu_linear(cond, l0_w_mod, l0_b_mod)[:b + 1].reshape(b + 1, 6, d)
    mod1 = _silu_linear(cond, l1_w_mod, l1_b_mod)[:b + 1].reshape(b + 1, 6, d)

    moe0 = _moe_weights(l0_w_gate_up, l0_b_gate_up, l0_w_down, l0_b_down)
    moe1 = _moe_weights(l1_w_gate_up, l1_b_gate_up, l1_w_down, l1_b_down)
    groups = BATCH_GROUPS if b % BATCH_GROUPS == 0 else 1
    bg = b // groups

    def trunk(gi, wait):
        sl = slice(gi * bg, (gi + 1) * bg)
        mod0_g = jnp.concatenate([mod0[sl], mod0[b:b + 1]], axis=0)
        mod1_g = jnp.concatenate([mod1[sl], mod1[b:b + 1]], axis=0)
        xa = jnp.concatenate([ctx[sl].reshape(bg * lc, d), x[sl].reshape(bg * s, d)], axis=0)
        xa = yield from _layer_even(_after(xa, wait), mod0_g, bg, s, lc, l0_g_mix, l0_g_ffn, l0_w_in, l0_w_out,
                                    l0_gate_bias, l0_mlstm_norm, l0_sink, (l0_w_router, l0_b_router), moe0)
        out = yield from _layer_odd_last(xa, mod1_g, bg, s, lc, l1_g_mix, l1_g_ffn, l1_w_in, l1_w_out,
                                         (l1_lam_re, l1_lam_im, l1_log_dt, l1_b_re, l1_b_im, l1_c_re, l1_c_im),
                                         l1_d_skip, l1_w_glu, l1_b_glu, l1_q_norm, l1_k_norm,
                                         (l1_w_router, l1_b_router), moe1, g_final)
        return out.reshape(bg, s, d)

    gens, outs, token = [], [None] * groups, None
    for gi in range(groups):
        gen = trunk(gi, token)
        token = next(gen)
        gens.append(gen)
    live = list(range(groups))
    while live:
        for gi in list(live):
            try:
                token = gens[gi].send(token)
            except StopIteration as done:
                outs[gi] = token = done.value
                live.remove(gi)
    return outs[0] if groups == 1 else jnp.concatenate(outs, axis=0)
```

```python
import functools
import math

import jax
import jax.numpy as jnp
import numpy as np
from jax import lax
from jax.experimental import pallas as pl
from jax.experimental.pallas import tpu as pltpu

F32 = jnp.float32
BF16 = jnp.bfloat16
I32 = jnp.int32

GRID_W = 64
HEAD_DIM = 64
ATTN_SCALE = HEAD_DIM ** -0.5
ROPE_AXIS_DIM = HEAD_DIM // 2
ROPE_BASE = 10000.0
EPS = 1e-6
NEG_INF = -1e30

MLSTM_HEADS = 8
MLSTM_CHUNKS = (256, 128, 64)
SWA_HEADS = 8
SWA_KV_HEADS = 2
SWA_WINDOW = 128
SWA_BLOCK = 128
S5_CHANNELS = 256
S5_GROUP = 16
S5_GROUPS = S5_CHANNELS // S5_GROUP
S5_STATE = 64
S5_CHUNK = 64
ATT_HEADS = 12
ATT_KV_HEADS = 3
N_EXPERTS = 32
TOP_K = 4
SWIGLU_LIMIT = 7.0
SWIGLU_ALPHA = 1.702

LANES = 128
VMEM_LIMIT = 56 * 1024 * 1024
MOE_ROWS = 512
MOE_SPLIT = (1, 3, 4)
HIGHEST = lax.Precision.HIGHEST


def _params(*sem):
    return pltpu.CompilerParams(dimension_semantics=sem, vmem_limit_bytes=VMEM_LIMIT)


def _tile(n, prefs):
    for t in prefs:
        if n % t == 0:
            return t
    return n


def _pad_cols(w, mult=LANES):
    pad = (-w.shape[-1]) % mult
    if pad:
        w = jnp.pad(w, [(0, 0)] * (w.ndim - 1) + [(0, pad)])
    return w


def _linear_kernel(x_ref, w_ref, b_ref, o_ref):
    x = x_ref[...]
    x = x * jax.nn.sigmoid(x)
    o_ref[...] = jnp.dot(x, w_ref[...], precision=HIGHEST, preferred_element_type=F32) + b_ref[...]


def _silu_linear(x, w, b):
    m, k = x.shape
    n = w.shape[1]
    tn = _tile(n, (1024, 512, 256, 128))
    return pl.pallas_call(
        _linear_kernel,
        out_shape=jax.ShapeDtypeStruct((m, n), F32),
        grid=(n // tn,),
        in_specs=[pl.BlockSpec((m, k), lambda j: (0, 0)),
                  pl.BlockSpec((k, tn), lambda j: (0, j)),
                  pl.BlockSpec((1, tn), lambda j: (0, j))],
        out_specs=pl.BlockSpec((m, tn), lambda j: (0, j)),
        compiler_params=_params("arbitrary"),
        name="adaln_linear",
    )(x, w, b.reshape(1, n))


def _in_proj_kernel(x_ref, g_ref, sc_ref, sh_ref, w_ref, wt_ref, *out_refs, widths, t_widths):
    x = x_ref[...]
    xn = x * lax.rsqrt(jnp.mean(x * x, axis=-1, keepdims=True) + EPS) * g_ref[...]
    xb = (xn * (1.0 + sc_ref[0]) + sh_ref[0]).astype(BF16)
    off = 0
    for o_ref, w in zip(out_refs, widths):
        o_ref[...] = jnp.dot(xb, w_ref[:, off:off + w], preferred_element_type=F32).astype(o_ref.dtype)
        off += w
    off = 0
    for o_ref, w in zip(out_refs[len(widths):], t_widths):
        o_ref[...] = lax.dot_general(wt_ref[off:off + w, :], xb, (((1,), (1,)), ((), ())),
                                     preferred_element_type=F32).astype(o_ref.dtype)
        off += w


def _in_proj(x, g, scale, shift, mod_map, w, widths, dtypes, tm, wt=None, t_widths=(), t_dtypes=()):
    r, d = x.shape
    if wt is None:
        wt = jnp.zeros((8, d), BF16)
    return pl.pallas_call(
        functools.partial(_in_proj_kernel, widths=tuple(widths), t_widths=tuple(t_widths)),
        out_shape=([jax.ShapeDtypeStruct((r, wd), dt) for wd, dt in zip(widths, dtypes)]
                   + [jax.ShapeDtypeStruct((wd, r), dt) for wd, dt in zip(t_widths, t_dtypes)]),
        grid=(r // tm,),
        in_specs=[pl.BlockSpec((tm, d), lambda i: (i, 0)),
                  pl.BlockSpec((1, d), lambda i: (0, 0)),
                  pl.BlockSpec((1, 1, d), lambda i: (mod_map(i), 0, 0)),
                  pl.BlockSpec((1, 1, d), lambda i: (mod_map(i), 0, 0)),
                  pl.BlockSpec(w.shape, lambda i: (0, 0)),
                  pl.BlockSpec(wt.shape, lambda i: (0, 0))],
        out_specs=([pl.BlockSpec((tm, wd), lambda i: (i, 0)) for wd in widths]
                   + [pl.BlockSpec((wd, tm), lambda i: (0, i)) for wd in t_widths]),
        compiler_params=_params("parallel"),
        name="in_proj",
    )(x, g.reshape(1, d), scale, shift, w, wt)


def _qk_prep_kernel(x_ref, xr_ref, cos_ref, sin_ref, g_ref, gr_ref, o_ref, *, norm, rope, scale):
    x = x_ref[0].astype(F32)
    if norm:
        s = lax.rsqrt(jnp.mean(x * x, axis=-1, keepdims=True) + EPS)
        x = x * s * g_ref[...]
    if rope:
        xr = xr_ref[0].astype(F32)
        if norm:
            xr = xr * s * gr_ref[...]
        x = x * cos_ref[...] + xr * sin_ref[...]
    o_ref[0] = (x * scale).astype(o_ref.dtype)


def _qk_prep(x, x_rot, cos, sin, gain, gain_rot, *, norm, rope, scale):
    b, h, t, dh = x.shape
    tt = _tile(t, (512, 256, 128))
    blk = pl.BlockSpec((1, h, tt, dh), lambda i, j: (i, 0, j, 0))
    tab = pl.BlockSpec((tt, dh), lambda i, j: (j, 0))
    vec = pl.BlockSpec((1, dh), lambda i, j: (0, 0))
    return pl.pallas_call(
        functools.partial(_qk_prep_kernel, norm=norm, rope=rope, scale=scale),
        out_shape=jax.ShapeDtypeStruct(x.shape, BF16),
        grid=(b, t // tt),
        in_specs=[blk, blk, tab, tab, vec, vec],
        out_specs=blk,
        compiler_params=_params("parallel", "parallel"),
        name="qk_prep",
    )(x, x_rot, cos, sin, gain.reshape(1, dh), gain_rot.reshape(1, dh))


def _q_prep_t_kernel(x_ref, xr_ref, cos_ref, sin_ref, g_ref, gr_ref, o_ref, *, norm, scale):
    w, tt = x_ref.shape
    nh = w // HEAD_DIM
    x = x_ref[...].astype(F32).reshape(nh, HEAD_DIM, tt)
    xr = xr_ref[...].astype(F32).reshape(nh, HEAD_DIM, tt)
    if norm:
        s = lax.rsqrt(jnp.mean(x * x, axis=1, keepdims=True) + EPS)
        x, xr = x * s * g_ref[...], xr * s * gr_ref[...]
    y = x * cos_ref[...] + xr * sin_ref[...]
    o_ref[...] = (y * scale).reshape(w, tt).astype(o_ref.dtype)


def _q_prep_t(xt, xt_rot, col0, n_cols, cos_t, sin_t, gain, gain_rot, scale, norm=True):
    w = xt.shape[0]
    s = cos_t.shape[1]
    tt = _tile(math.gcd(col0, s), (512, 256, 128))
    blk = pl.BlockSpec((w, tt), lambda j: (0, j + col0 // tt))
    tab = pl.BlockSpec((HEAD_DIM, tt), lambda j: (0, j % (s // tt)))
    vec = pl.BlockSpec((HEAD_DIM, 1), lambda j: (0, 0))
    return pl.pallas_call(
        functools.partial(_q_prep_t_kernel, norm=norm, scale=scale),
        out_shape=jax.ShapeDtypeStruct((w, n_cols), BF16),
        grid=(n_cols // tt,),
        in_specs=[blk, blk, tab, tab, vec, vec],
        out_specs=pl.BlockSpec((w, tt), lambda j: (0, j)),
        compiler_params=_params("parallel"),
        name="q_prep_t",
    )(xt, xt_rot, cos_t, sin_t, gain.reshape(HEAD_DIM, 1), gain_rot.reshape(HEAD_DIM, 1))


def _log_sigmoid(x):
    return jnp.minimum(x, 0.0) - jnp.log(1.0 + jnp.exp(-jnp.abs(x)))


def _mlstm_kernel(qf, kf, vf, gf, qb, kb, vb, gb, bias_ref, hf_ref, hb_ref, c_scr, n_scr, m_scr):
    ch, nh, dh = qf.shape[0], MLSTM_HEADS, HEAD_DIM

    @pl.when(pl.program_id(1) == 0)
    def _():
        c_scr[...] = jnp.zeros_like(c_scr)
        n_scr[...] = jnp.zeros_like(n_scr)
        m_scr[...] = jnp.full_like(m_scr, NEG_INF)

    row = lax.broadcasted_iota(I32, (ch, ch), 0)
    col = lax.broadcasted_iota(I32, (ch, ch), 1)
    lane_lo = lax.broadcasted_iota(I32, (ch, 2 * dh), 1) < dh
    row_lo = lax.broadcasted_iota(I32, (2 * dh, 2 * dh), 0) < dh
    col_lo = lax.broadcasted_iota(I32, (2 * dh, 2 * dh), 1) < dh
    vec_lo = lax.broadcasted_iota(I32, (1, 2 * dh), 1) < dh
    pair = lambda a, b: jnp.where(lane_lo, a, b)
    refs = ((qf, kf, vf, gf, hf_ref), (qb, kb, vb, gb, hb_ref))
    jobs = [(d, p) for d in range(2) for p in range(nh // 2)]
    sl = lambda p: slice(2 * dh * p, 2 * dh * (p + 1))

    qk = {}
    for d, p in jobs:
        q2, k2 = refs[d][0][:, sl(p)], refs[d][1][:, sl(p)]
        zero = jnp.zeros_like(q2)
        q_st = jnp.concatenate([jnp.where(lane_lo, q2, zero), jnp.where(lane_lo, zero, q2)], axis=0)
        qk[d, p] = lax.dot_general(q_st, k2, (((1,), (1,)), ((), ())), preferred_element_type=F32)

    st = {}
    for d in range(2):
        seen = (col <= row) if d == 0 else (col >= row)
        tri = seen.astype(F32)
        tri_t = ((row <= col) if d == 0 else (row >= col)).astype(F32)
        last = ch - 1 if d == 0 else 0
        g = refs[d][3][...] + bias_ref[...]
        g_t = g.T
        lo = 2 * nh * d
        li_col, lf_col = g[:, lo:lo + nh], _log_sigmoid(g[:, lo + nh:lo + 2 * nh])
        li_row, lf_row = g_t[lo:lo + nh, :], _log_sigmoid(g_t[lo + nh:lo + 2 * nh, :])
        b_col = jnp.dot(tri, lf_col, precision=HIGHEST, preferred_element_type=F32)
        b_row = jnp.dot(lf_row, tri_t, precision=HIGHEST, preferred_element_type=F32)
        for h in range(nh):
            bc, br = b_col[:, h:h + 1], b_row[h:h + 1, :]
            lir, lic = li_row[h:h + 1, :], li_col[:, h:h + 1]
            b_last = br[:, last:last + 1]
            m_prev = m_scr[d, h // 2][:, (h % 2) * dh:(h % 2) * dh + 1]
            d_log = jnp.where(seen, bc - br + lir, NEG_INF)
            inter_log = bc + m_prev
            m_t = jnp.maximum(inter_log, jnp.max(d_log, axis=1, keepdims=True))
            w_log = b_last - bc + lic
            m_new = jnp.maximum(b_last + m_prev, jnp.max(w_log, axis=0, keepdims=True))
            st[d, h] = dict(dmat=jnp.exp(d_log - m_t), inter=jnp.exp(inter_log - m_t), floor=jnp.exp(-m_t),
                            m_new=m_new, decay=jnp.exp(b_last + m_prev - m_new), wn=jnp.exp(w_log - m_new))

    mm = {}
    for d, p in jobs:
        a, b = st[d, 2 * p], st[d, 2 * p + 1]
        q2, k2, v2 = refs[d][0][:, sl(p)], refs[d][1][:, sl(p)], refs[d][2][:, sl(p)]
        s = qk[d, p] * jnp.concatenate([a["dmat"], b["dmat"]], axis=0)
        kw = k2.astype(F32) * pair(a["wn"], b["wn"])
        mm[d, p] = dict(
            s_sum=jnp.sum(s, axis=1, keepdims=True), kw_sum=jnp.sum(kw, axis=0, keepdims=True),
            sv=jnp.dot(s.astype(BF16), v2, preferred_element_type=F32),
            q_c=jnp.dot(q2, c_scr[d, p].astype(BF16), preferred_element_type=F32),
            kv=lax.dot_general(kw.astype(BF16), v2, (((0,), (0,)), ((), ())), preferred_element_type=F32))

    for d, p in jobs:
        a, b, r = st[d, 2 * p], st[d, 2 * p + 1], mm[d, p]
        q2 = refs[d][0][:, sl(p)]
        n_prev = n_scr[d, p]
        num = pair(a["inter"], b["inter"]) * r["q_c"] + jnp.where(lane_lo, r["sv"][:ch], r["sv"][ch:])
        qn = q2.astype(F32) * n_prev
        qn_a = jnp.sum(jnp.where(lane_lo, qn, 0.0), axis=1, keepdims=True)
        qn_b = jnp.sum(jnp.where(lane_lo, 0.0, qn), axis=1, keepdims=True)
        den_a = jnp.maximum(jnp.abs(a["inter"] * qn_a + r["s_sum"][:ch]), a["floor"])
        den_b = jnp.maximum(jnp.abs(b["inter"] * qn_b + r["s_sum"][ch:]), b["floor"])
        refs[d][4][:, sl(p)] = num / pair(den_a, den_b)
        c_scr[d, p] = (jnp.where(row_lo, a["decay"], b["decay"]) * c_scr[d, p]
                       + jnp.where(row_lo == col_lo, r["kv"], 0.0))
        n_scr[d, p] = jnp.where(vec_lo, a["decay"], b["decay"]) * n_prev + r["kw_sum"]
        m_scr[d, p] = jnp.where(vec_lo, a["m_new"], b["m_new"])


def _mlstm(q, k, v, gates, bias, b, lc, s):
    r, w = q.shape
    ch = _tile(math.gcd(lc, s), MLSTM_CHUNKS)
    ncc, ncx = lc // ch, s // ch
    base = b * ncc

    def fwd(i, c):
        return jnp.where(c < ncc, i * ncc + c, base + i * ncx + (c - ncc)), 0

    def bwd(i, c):
        return jnp.where(c < ncc, i * ncc + (ncc - 1 - c), base + i * ncx + (ncx - 1 - (c - ncc))), 0

    spec = lambda width, m: pl.BlockSpec((ch, width), m)
    npair = MLSTM_HEADS // 2
    return pl.pallas_call(
        _mlstm_kernel,
        out_shape=[jax.ShapeDtypeStruct((r, w), F32)] * 2,
        grid=(b, ncc + ncx),
        in_specs=[spec(w, fwd), spec(w, fwd), spec(w, fwd), spec(LANES, fwd),
                  spec(w, bwd), spec(w, bwd), spec(w, bwd), spec(LANES, bwd),
                  pl.BlockSpec((1, LANES), lambda i, c: (0, 0))],
        out_specs=[spec(w, fwd), spec(w, bwd)],
        scratch_shapes=[pltpu.VMEM((2, npair, 2 * HEAD_DIM, 2 * HEAD_DIM), F32),
                        pltpu.VMEM((2, npair, 1, 2 * HEAD_DIM), F32),
                        pltpu.VMEM((2, npair, 1, 2 * HEAD_DIM), F32)],
        compiler_params=_params("parallel", "arbitrary"),
        name="mlstm_scan",
    )(q, k, v, gates, q, k, v, gates, bias)


def _mlstm_out_kernel(hf_ref, hb_ref, o_ref, nrm_ref, ones_ref, y_ref):
    h = hf_ref[...] + hb_ref[...]
    ms = jnp.dot((h * h).astype(BF16), ones_ref[...], preferred_element_type=F32) * (1.0 / HEAD_DIM)
    hn = h * lax.rsqrt(ms + EPS) * nrm_ref[...]
    y_ref[...] = (jax.nn.sigmoid(o_ref[...].astype(F32)) * hn).astype(y_ref.dtype)


def _head_ones(width):
    idx = np.arange(width) // HEAD_DIM
    return jnp.asarray(idx[:, None] == idx[None, :], BF16)


def _mlstm_out(hf, hb, o, norm, tm):
    r, w = hf.shape
    blk = pl.BlockSpec((tm, w), lambda i: (i, 0))
    return pl.pallas_call(
        _mlstm_out_kernel,
        out_shape=jax.ShapeDtypeStruct((r, w), BF16),
        grid=(r // tm,),
        in_specs=[blk, blk, blk,
                  pl.BlockSpec((1, w), lambda i: (0, 0)),
                  pl.BlockSpec((w, w), lambda i: (0, 0))],
        out_specs=blk,
        compiler_params=_params("parallel"),
        name="mlstm_out",
    )(hf, hb, o, norm.reshape(1, w), _head_ones(w))


def _swa_kernel(q_ref, k0, k1, k2, k3, v0, v1, v2, v3, kc_ref, vc_ref, sink_ref, o_ref, *, seq):
    w, dh = SWA_BLOCK, HEAD_DIM
    tq = q_ref.shape[1]
    g = q_ref.shape[0] // dh
    i = pl.program_id(2)
    k_loc = jnp.concatenate([k0[0, 0], k1[0, 0], k2[0, 0], k3[0, 0]], axis=0)
    v_loc = jnp.concatenate([v0[0, 0], v1[0, 0], v2[0, 0], v3[0, 0]], axis=1)
    kpos = (2 * i - 1) * w + lax.broadcasted_iota(I32, (4 * w, tq), 0)
    qpos = i * tq + lax.broadcasted_iota(I32, (4 * w, tq), 1)
    valid = (jnp.abs(qpos - kpos) <= SWA_WINDOW) & (kpos >= 0) & (kpos < seq)
    qs = [q_ref[h * dh:(h + 1) * dh, :] for h in range(g)]
    s_loc = [jnp.where(valid, jnp.dot(k_loc, qs[h], preferred_element_type=F32), NEG_INF) for h in range(g)]
    s_ctx = [jnp.dot(kc_ref[0, 0], qs[h], preferred_element_type=F32) for h in range(g)]
    outs = []
    for h in range(g):
        sink = sink_ref[0, h:h + 1, :]
        m = jnp.maximum(jnp.maximum(jnp.max(s_loc[h], axis=0, keepdims=True),
                                    jnp.max(s_ctx[h], axis=0, keepdims=True)), sink)
        p_loc = jnp.exp2(s_loc[h] - m)
        p_ctx = jnp.exp2(s_ctx[h] - m)
        den = jnp.sum(p_loc, axis=0, keepdims=True) + jnp.sum(p_ctx, axis=0, keepdims=True) + jnp.exp2(sink - m)
        o = (jnp.dot(v_loc, p_loc.astype(BF16), preferred_element_type=F32)
             + jnp.dot(vc_ref[0, 0], p_ctx.astype(BF16), preferred_element_type=F32))
        outs.append(o / den)
    o_ref[...] = jnp.concatenate(outs, axis=0).T.astype(o_ref.dtype)


def _swa(qt, k, vt, k_ctx, vt_ctx, sink2, b, s):
    hkv, dh = k.shape[1], k.shape[3]
    nh = qt.shape[0] // dh
    g = nh // hkv
    w = SWA_BLOCK
    tq = 2 * w
    nb, nq = s // w, s // tq
    lc = k_ctx.shape[2]
    clampi = lambda j: jnp.clip(j, 0, nb - 1)
    kspec = lambda o: pl.BlockSpec((1, 1, w, dh), lambda bi, hi, i: (bi, hi, clampi(2 * i + o), 0))
    vspec = lambda o: pl.BlockSpec((1, 1, dh, w), lambda bi, hi, i: (bi, hi, 0, clampi(2 * i + o)))
    return pl.pallas_call(
        functools.partial(_swa_kernel, seq=s),
        out_shape=jax.ShapeDtypeStruct((b * s, nh * dh), BF16),
        grid=(b, hkv, nq),
        in_specs=[pl.BlockSpec((g * dh, tq), lambda bi, hi, i: (hi, bi * nq + i)),
                  kspec(-1), kspec(0), kspec(1), kspec(2), vspec(-1), vspec(0), vspec(1), vspec(2),
                  pl.BlockSpec((1, 1, lc, dh), lambda bi, hi, i: (bi, hi, 0, 0)),
                  pl.BlockSpec((1, 1, dh, lc), lambda bi, hi, i: (bi, hi, 0, 0)),
                  pl.BlockSpec((1, g, tq), lambda bi, hi, i: (hi, 0, 0))],
        out_specs=pl.BlockSpec((tq, g * dh), lambda bi, hi, i: (bi * nq + i, hi)),
        compiler_params=_params("parallel", "parallel", "arbitrary"),
        name="window_attention",
    )(qt, k, k, k, k, vt, vt, vt, vt, k_ctx, vt_ctx, sink2)


def _ctx_attn_kernel(q_ref, k_ref, v_ref, sink_ref, o_ref, *, use_sink):
    g, lq, dh = q_ref.shape[1:]
    q = q_ref[0].reshape(g * lq, dh)
    s = lax.dot_general(q, k_ref[0, 0], (((1,), (1,)), ((), ())), preferred_element_type=F32) * ATTN_SCALE
    m = jnp.max(s, axis=1, keepdims=True)
    if use_sink:
        m = jnp.maximum(m, sink_ref[0])
    p = jnp.exp(s - m)
    den = jnp.sum(p, axis=1, keepdims=True)
    if use_sink:
        den = den + jnp.exp(sink_ref[0] - m)
    o = jnp.dot(p.astype(BF16), v_ref[0, 0], preferred_element_type=F32) / den
    o_ref[0] = o.reshape(g, lq, dh).astype(o_ref.dtype)


def _ctx_attn(q, k, v, sink_col, use_sink):
    b, nh, lq, dh = q.shape
    hkv = k.shape[1]
    g = nh // hkv
    qblk = pl.BlockSpec((1, g, lq, dh), lambda bi, hi: (bi, hi, 0, 0))
    kblk = pl.BlockSpec((1, 1, k.shape[2], dh), lambda bi, hi: (bi, hi, 0, 0))
    return pl.pallas_call(
        functools.partial(_ctx_attn_kernel, use_sink=use_sink),
        out_shape=jax.ShapeDtypeStruct(q.shape, BF16),
        grid=(b, hkv),
        in_specs=[qblk, kblk, kblk, pl.BlockSpec((1, g * lq, 1), lambda bi, hi: (hi, 0, 0))],
        out_specs=qblk,
        compiler_params=_params("parallel", "parallel"),
        name="context_attention",
    )(q, k, v, sink_col)


def _flash_kernel(q_ref, k_ref, v_ref, o_ref, m_scr, l_scr, acc_scr, sa_scr, sb_scr, *, tk):
    dh = HEAD_DIM
    g = q_ref.shape[0] // dh
    n = k_ref.shape[2] // tk
    m_scr[...] = jnp.full_like(m_scr, NEG_INF)
    l_scr[...] = jnp.zeros_like(l_scr)
    acc_scr[...] = jnp.zeros_like(acc_scr)

    def scores(j, dst):
        kk = k_ref[0, 0, pl.ds(pl.multiple_of(j * tk, tk), tk), :]
        for h in range(g):
            dst[h] = jnp.dot(kk, q_ref[h * dh:(h + 1) * dh, :], preferred_element_type=F32)

    def update(j, src):
        vv = v_ref[0, 0, :, pl.ds(pl.multiple_of(j * tk, tk), tk)]
        ss = [src[h] for h in range(g)]
        m_olds = [m_scr[h] for h in range(g)]
        m_news = [jnp.maximum(m_olds[h], jnp.max(ss[h], axis=0, keepdims=True)) for h in range(g)]
        ps = [jnp.exp2(ss[h] - m_news[h]) for h in range(g)]
        pvs = [jnp.dot(vv, ps[h].astype(BF16), preferred_element_type=F32) for h in range(g)]
        for h in range(g):
            alpha = jnp.exp2(m_olds[h] - m_news[h])
            l_scr[h] = alpha * l_scr[h] + jnp.sum(ps[h], axis=0, keepdims=True)
            acc_scr[h] = alpha * acc_scr[h] + pvs[h]
            m_scr[h] = m_news[h]

    scores(0, sa_scr)

    def body(i, carry):
        scores(2 * i + 1, sb_scr)
        update(2 * i, sa_scr)
        scores(2 * i + 2, sa_scr)
        update(2 * i + 1, sb_scr)
        return carry

    lax.fori_loop(0, (n - 1) // 2, body, 0)
    if n % 2 == 1:
        update(n - 1, sa_scr)
    else:
        scores(n - 1, sb_scr)
        update(n - 2, sa_scr)
        update(n - 1, sb_scr)
    out = (acc_scr[...] / l_scr[...]).reshape(g * dh, -1)
    o_ref[...] = out.T.astype(o_ref.dtype)


def _flash(qt, k, vt, b, s, tq=None, tk=None):
    w = qt.shape[0]
    hkv, t, dh = k.shape[1], k.shape[2], k.shape[3]
    g = w // dh // hkv
    tq = tq or _tile(s, (256, 128))
    tk = tk or _tile(t, (256, 128))
    nq = s // tq
    return pl.pallas_call(
        functools.partial(_flash_kernel, tk=tk),
        out_shape=jax.ShapeDtypeStruct((b * s, w), BF16),
        grid=(b, hkv, nq),
        in_specs=[pl.BlockSpec((g * dh, tq), lambda bi, hi, i: (hi, bi * nq + i)),
                  pl.BlockSpec((1, 1, t, dh), lambda bi, hi, i: (bi, hi, 0, 0)),
                  pl.BlockSpec((1, 1, dh, t), lambda bi, hi, i: (bi, hi, 0, 0))],
        out_specs=pl.BlockSpec((tq, g * dh), lambda bi, hi, i: (bi * nq + i, hi)),
        scratch_shapes=[pltpu.VMEM((g, 1, tq), F32), pltpu.VMEM((g, 1, tq), F32), pltpu.VMEM((g, dh, tq), F32),
                        pltpu.VMEM((g, tk, tq), F32), pltpu.VMEM((g, tk, tq), F32)],
        compiler_params=_params("parallel", "parallel", "arbitrary"),
        name="dense_attention",
    )(qt, k, vt)


def _bmm_kernel(a_ref, b_ref, o_ref):
    o_ref[0] = jnp.dot(a_ref[0], b_ref[0], preferred_element_type=F32).astype(o_ref.dtype)


def _group_matmul(a, bmat, name):
    g, m, k = a.shape
    n = bmat.shape[2]
    tm = _tile(m, (1024, 512, 256, 128, 64, 32, 16, 8))
    return pl.pallas_call(
        _bmm_kernel,
        out_shape=jax.ShapeDtypeStruct((g, m, n), F32),
        grid=(g, m // tm),
        in_specs=[pl.BlockSpec((1, tm, k), lambda gi, i: (gi, i, 0)),
                  pl.BlockSpec((1, k, n), lambda gi, i: (gi, 0, 0))],
        out_specs=pl.BlockSpec((1, tm, n), lambda gi, i: (gi, i, 0)),
        compiler_params=_params("parallel", "parallel"),
        name=name,
    )(a, bmat)


def _s5_scan_kernel(sre_ref, sim_ref, are_ref, aim_ref, zre_ref, zim_ref):
    nd, nsteps = sre_ref.shape[0], sre_ref.shape[1]
    for d in range(nd):
        a_re, a_im = are_ref[d], aim_ref[d]

        def body(i, carry, d=d, a_re=a_re, a_im=a_im):
            z_re, z_im = carry
            zre_ref[d, i] = z_re
            zim_ref[d, i] = z_im
            return (a_re * z_re - a_im * z_im + sre_ref[d, i], a_re * z_im + a_im * z_re + sim_ref[d, i])

        zero = jnp.zeros(sre_ref.shape[2:], F32)
        lax.fori_loop(0, nsteps, body, (zero, zero))


def _s5_scan(s_re, s_im, a_re, a_im):
    full = lambda arr: pl.BlockSpec(arr.shape, lambda i: (0,) * arr.ndim)
    return pl.pallas_call(
        _s5_scan_kernel,
        out_shape=[jax.ShapeDtypeStruct(s_re.shape, F32)] * 2,
        grid=(1,),
        in_specs=[full(s_re), full(s_im), full(a_re), full(a_im)],
        out_specs=[full(s_re), full(s_im)],
        compiler_params=_params("arbitrary"),
        name="s5_chunk_scan",
    )(s_re, s_im, a_re, a_im)


def _s5_glu_kernel(y_ref, u_ref, d_ref, w_ref, b_ref, o_ref):
    y = y_ref[...] + d_ref[...] * u_ref[...]
    y = jax.nn.gelu(y)
    gate = jnp.dot(y.astype(BF16), w_ref[...], preferred_element_type=F32) + b_ref[...]
    o_ref[...] = (y * jax.nn.sigmoid(gate)).astype(o_ref.dtype)


def _s5_glu(y, u, d_skip, w_glu, b_glu, tm):
    r, c = y.shape
    blk = pl.BlockSpec((tm, c), lambda i: (i, 0))
    vec = pl.BlockSpec((1, c), lambda i: (0, 0))
    return pl.pallas_call(
        _s5_glu_kernel,
        out_shape=jax.ShapeDtypeStruct((r, c), BF16),
        grid=(r // tm,),
        in_specs=[blk, blk, vec, pl.BlockSpec((c, c), lambda i: (0, 0)), vec],
        out_specs=blk,
        compiler_params=_params("parallel"),
        name="s5_readout_glu",
    )(y, u, d_skip.reshape(1, c), w_glu.astype(BF16), b_glu.reshape(1, c))


def _s5_tables(lam_re, lam_im, log_dt, b_re, b_im, c_re, c_im):
    ln, p, gc = S5_CHUNK, S5_STATE, S5_GROUP
    lam = lax.complex(lam_re.astype(F32), lam_im.astype(F32))
    dt = jnp.exp(log_dt.astype(F32))[..., None]
    a_bar = jnp.exp(lam * dt)
    b_scale = (a_bar - 1.0) / lam
    b_mat = lax.complex(b_re.astype(F32), b_im.astype(F32))
    c_mat = lax.complex(c_re.astype(F32), c_im.astype(F32))
    tau = jnp.arange(ln + 1, dtype=F32)
    apow = jnp.exp((lam * dt)[:, :, None, :] * tau[None, None, :, None])
    drive = b_scale[..., None] * b_mat[None]
    kern = jnp.real(jnp.einsum('gcp,dgtp,dgpe->dgtce', c_mat, apow[:, :, :ln], drive))
    kc = jnp.concatenate([kern[1, :, :0:-1], (kern[0, :, :1] + kern[1, :, :1]), kern[0, :, 1:]], axis=1)
    kc_e = kc.transpose(0, 3, 1, 2).reshape(S5_GROUPS, gc, (2 * ln - 1) * gc)
    toep = jnp.stack([kc_e[:, :, (ln - 1 - s) * gc:(2 * ln - 1 - s) * gc] for s in range(ln)], axis=1)
    toep = toep.reshape(S5_GROUPS, ln * gc, ln * gc)
    w_f = apow[0, :, ln - 1::-1][:, :ln, :, None] * drive[0][:, None]
    w_b = apow[1, :, :ln, :, None] * drive[1][:, None]
    def m_in(wc):
        wt = wc.transpose(0, 1, 3, 2).reshape(S5_GROUPS, ln * gc, p)
        return jnp.concatenate([jnp.real(wt), jnp.imag(wt)], axis=-1)
    min_all = jnp.concatenate([m_in(w_f), m_in(w_b)], axis=-1)
    o_f = c_mat[:, None] * apow[0, :, 1:ln + 1][:, :, None, :]
    o_b = c_mat[:, None] * apow[1, :, ln:0:-1][:, :, None, :]
    def m_out(oc):
        ot = oc.transpose(0, 3, 1, 2).reshape(S5_GROUPS, p, ln * gc)
        return jnp.concatenate([jnp.real(ot), -jnp.imag(ot)], axis=1)
    rhs = jnp.concatenate([toep, m_out(o_f), m_out(o_b)], axis=1)
    a_l = apow[:, :, ln].reshape(2, 1, S5_GROUPS * p)
    return min_all.astype(BF16), rhs.astype(BF16), jnp.real(a_l), jnp.imag(a_l)


def _s5_mixer(u_x, u_c, tables, d_skip, w_glu, b_glu):
    min_all, rhs, a_re, a_im = tables
    b, s, _ = u_x.shape
    ln, p, gc, ng = S5_CHUNK, S5_STATE, S5_GROUP, S5_GROUPS

    def chunks(u):
        nc = u.shape[1] // ln
        return u.reshape(b, nc, ln, ng, gc).transpose(3, 0, 1, 2, 4).reshape(ng, b * nc, ln * gc).astype(BF16), nc

    ux, ncx = chunks(u_x)
    uc, ncc = chunks(u_c)
    sx = _group_matmul(ux, min_all, "s5_local_state").reshape(ng, b, ncx, 4, p)
    sc = _group_matmul(uc, min_all, "s5_local_state_ctx").reshape(ng, b, ncc, 4, p)

    def scan_order(part):
        f = jnp.concatenate([sc[:, :, :, part], sx[:, :, :, part]], axis=2)
        r = jnp.concatenate([sc[:, :, ::-1, part + 2], sx[:, :, ::-1, part + 2]], axis=2)
        return jnp.stack([f, r]).transpose(0, 3, 2, 1, 4).reshape(2, ncc + ncx, b, ng * p)

    z_re, z_im = _s5_scan(scan_order(0), scan_order(1), a_re, a_im)

    def latent(z, d):
        zl = z[d, ncc:]
        if d == 1:
            zl = zl[::-1]
        return zl.reshape(ncx, b, ng, p).transpose(2, 1, 0, 3).reshape(ng, b * ncx, p)

    lhs = jnp.concatenate([ux, latent(z_re, 0).astype(BF16), latent(z_im, 0).astype(BF16),
                           latent(z_re, 1).astype(BF16), latent(z_im, 1).astype(BF16)], axis=-1)
    y = _group_matmul(lhs, rhs, "s5_outputs")
    y = y.reshape(ng, b, ncx, ln, gc).transpose(1, 2, 3, 0, 4).reshape(b * s, S5_CHANNELS)
    return _s5_glu(y, u_x.reshape(b * s, S5_CHANNELS), d_skip, w_glu, b_glu, _tile(b * s, (1024, 512, 256)))


def _out_proj_kernel(x_ref, a_ref, b_ref, wa_ref, wb_ref, gt_ref, g_ref, sc_ref, sh_ref, wr_ref, br_ref,
                     xo_ref, xn_ref, lg_ref):
    y = (jnp.dot(a_ref[...], wa_ref[...], preferred_element_type=F32)
         + jnp.dot(b_ref[...], wb_ref[...], preferred_element_type=F32))
    x = x_ref[...] + gt_ref[0] * y
    xo_ref[...] = x
    xn = x * lax.rsqrt(jnp.mean(x * x, axis=-1, keepdims=True) + EPS) * g_ref[...]
    xn = xn * (1.0 + sc_ref[0]) + sh_ref[0]
    hi = xn.astype(BF16)
    xn_ref[...] = hi
    lo = (xn - hi.astype(F32)).astype(BF16)
    part = jnp.dot(hi, wr_ref[...], preferred_element_type=F32)
    part = part[:, :LANES] + part[:, LANES:] + jnp.dot(lo, wr_ref[:, :LANES], preferred_element_type=F32)
    lg_ref[...] = part[:, :lg_ref.shape[1]] + br_ref[...]


def _out_proj(x, row0, a, bmix, wa, wb, gate, g_ffn, scale, shift, mod_map, w_router, b_router, tm):
    r = a.shape[0]
    d = x.shape[1]
    t0 = row0 // tm
    ne = w_router.shape[1]
    w_hi = w_router.astype(BF16)
    w_lo = (w_router - w_hi.astype(F32)).astype(BF16)
    w_router = jnp.concatenate([_pad_cols(w_hi), _pad_cols(w_lo)], axis=1)
    row = lambda w: pl.BlockSpec((tm, w), lambda i: (i, 0))
    mod = pl.BlockSpec((1, 1, d), lambda i: (mod_map(i), 0, 0))
    full = lambda arr: pl.BlockSpec(arr.shape, lambda i: (0, 0))
    return pl.pallas_call(
        _out_proj_kernel,
        out_shape=[jax.ShapeDtypeStruct((r, d), F32), jax.ShapeDtypeStruct((r, d), BF16),
                   jax.ShapeDtypeStruct((r, ne), F32)],
        grid=(r // tm,),
        in_specs=[pl.BlockSpec((tm, d), lambda i: (i + t0, 0)), row(a.shape[1]), row(bmix.shape[1]),
                  full(wa), full(wb), mod, pl.BlockSpec((1, d), lambda i: (0, 0)), mod, mod,
                  full(w_router), pl.BlockSpec((1, ne), lambda i: (0, 0))],
        out_specs=[row(d), row(d), row(ne)],
        compiler_params=_params("parallel"),
        name="out_proj",
    )(x, a, bmix, wa, wb, gate, g_ffn.reshape(1, d), scale, shift, w_router, b_router.reshape(1, ne))


def _route_kernel(lg_ref, idx_ref, gate_ref, pos_ref, cnt_ref, carry):
    tm, ne = lg_ref.shape

    @pl.when(pl.program_id(0) == 0)
    def _():
        carry[...] = jnp.zeros_like(carry)

    work = lg_ref[...]
    lane = lax.broadcasted_iota(I32, (tm, ne), 1).astype(F32)
    out_lane = lax.broadcasted_iota(I32, (tm, LANES), 1)
    vals, hots, idx_out = [], [], jnp.zeros((tm, LANES), I32)
    for kk in range(TOP_K):
        mx = jnp.max(work, axis=1, keepdims=True)
        idx = jnp.min(jnp.where(work == mx, lane, float(ne)), axis=1, keepdims=True)
        hot = lane == idx
        work = jnp.where(hot, -jnp.inf, work)
        vals.append(mx)
        hots.append(hot)
        idx_out = jnp.where(out_lane == kk, idx.astype(I32), idx_out)
    exps = [jnp.exp(vv - vals[0]) for vv in vals]
    tot = exps[0] + exps[1] + exps[2] + exps[3]
    multi = (hots[0] | hots[1] | hots[2] | hots[3]).astype(BF16)
    r_i = lax.broadcasted_iota(I32, (tm, tm), 0)
    c_i = lax.broadcasted_iota(I32, (tm, tm), 1)
    before = (r_i > c_i).astype(BF16)
    prefix = jnp.dot(before, multi, preferred_element_type=F32) + carry[...]
    gate_out = jnp.zeros((tm, LANES), F32)
    pos_out = jnp.zeros((tm, LANES), I32)
    for kk in range(TOP_K):
        gate_out = jnp.where(out_lane == kk, exps[kk] / tot, gate_out)
        pos = jnp.sum(jnp.where(hots[kk], prefix, 0.0), axis=1, keepdims=True).astype(I32)
        pos_out = jnp.where(out_lane == kk, pos, pos_out)
    idx_ref[...] = idx_out
    gate_ref[...] = gate_out
    pos_ref[...] = pos_out
    carry[...] = carry[...] + jnp.sum(multi.astype(F32), axis=0, keepdims=True)
    cnt_ref[...] = carry[...]


def _route(logits, tm, row0, n):
    ne = logits.shape[1]
    t0 = row0 // tm
    wide = pl.BlockSpec((tm, LANES), lambda i: (i, 0))
    return pl.pallas_call(
        _route_kernel,
        out_shape=[jax.ShapeDtypeStruct((n, LANES), I32), jax.ShapeDtypeStruct((n, LANES), F32),
                   jax.ShapeDtypeStruct((n, LANES), I32), jax.ShapeDtypeStruct((1, ne), F32)],
        grid=(n // tm,),
        in_specs=[pl.BlockSpec((tm, ne), lambda i: (i + t0, 0))],
        out_specs=[wide, wide, wide, pl.BlockSpec((1, ne), lambda i: (0, 0))],
        scratch_shapes=[pltpu.VMEM((1, ne), F32)],
        compiler_params=_params("arbitrary"),
        name="moe_route",
    )(logits)


def _expert_kernel(be_ref, nb_ref, x_ref, wgu_ref, bgu_ref, wd_ref, bd_ref, perm_ref, o_ref, wgu_bf, wd_bf):
    i = pl.program_id(0)
    fresh = jnp.logical_or(i == 0, be_ref[i] != be_ref[jnp.maximum(i - 1, 0)])

    @pl.when(jnp.logical_and(i < nb_ref[0], fresh))
    def _():
        tile = perm_ref.shape[0]
        for t in range(wgu_ref.shape[2] // tile):
            sl = slice(t * tile, (t + 1) * tile)
            wgu_bf[:, sl] = jnp.dot(wgu_ref[0, :, sl].astype(BF16), perm_ref[...],
                                    preferred_element_type=F32).astype(BF16)
        wd_bf[...] = wd_ref[0].astype(BF16)

    @pl.when(i < nb_ref[0])
    def _():
        h = jnp.dot(x_ref[...], wgu_bf[...], preferred_element_type=F32) + bgu_ref[0]
        acts = []
        for t in range(h.shape[1] // (2 * LANES)):
            glu = jnp.minimum(h[:, 2 * LANES * t:2 * LANES * t + LANES], SWIGLU_LIMIT)
            lin = jnp.clip(h[:, 2 * LANES * t + LANES:2 * LANES * (t + 1)], -SWIGLU_LIMIT, SWIGLU_LIMIT)
            acts.append((glu * jax.nn.sigmoid(SWIGLU_ALPHA * glu) * (lin + 1.0)).astype(BF16))
        act = jnp.concatenate(acts, axis=1)
        o_ref[...] = (jnp.dot(act, wd_bf[...], preferred_element_type=F32) + bd_ref[0]).astype(o_ref.dtype)

    @pl.when(i >= nb_ref[0])
    def _():
        o_ref[...] = jnp.zeros_like(o_ref)


def _experts(xs, blk_e, n_used, wgu, bgu, wd, bd):
    n, d = xs.shape
    ff2 = wgu.shape[2]
    nblk = n // MOE_ROWS
    tile = 2 * LANES
    j = np.arange(tile)
    col = np.where(j < LANES, 2 * j, 2 * (j - LANES) + 1)
    perm = jnp.asarray(np.arange(tile)[:, None] == col[None, :], BF16)
    wspec = lambda shape: pl.BlockSpec((1,) + shape, lambda i, be, nb: (be[i], 0, 0))
    return pl.pallas_call(
        _expert_kernel,
        out_shape=jax.ShapeDtypeStruct((n, d), BF16),
        grid_spec=pltpu.PrefetchScalarGridSpec(
            num_scalar_prefetch=2,
            grid=(nblk,),
            in_specs=[pl.BlockSpec((MOE_ROWS, d), lambda i, be, nb: (i, 0)),
                      wspec((d, ff2)), wspec((1, ff2)), wspec((ff2 // 2, d)), wspec((1, d)),
                      pl.BlockSpec((tile, tile), lambda i, be, nb: (0, 0))],
            out_specs=pl.BlockSpec((MOE_ROWS, d), lambda i, be, nb: (i, 0)),
            scratch_shapes=[pltpu.VMEM((d, ff2), BF16), pltpu.VMEM((ff2 // 2, d), BF16)]),
        compiler_params=_params("arbitrary"),
        name="moe_experts",
    )(blk_e, n_used, xs, wgu, bgu, wd, bd, perm)


def _combine_kernel(x_ref, y_ref, gate_ref, gt_ref, g_ref, o_ref, *, final_norm):
    gates = gate_ref[...]
    y = gates[:, 0:1] * y_ref[0].astype(F32)
    for kk in range(1, TOP_K):
        y = y + gates[:, kk:kk + 1] * y_ref[kk].astype(F32)
    x = x_ref[...] + gt_ref[0] * y
    if final_norm:
        x = x * lax.rsqrt(jnp.mean(x * x, axis=-1, keepdims=True) + EPS) * g_ref[...]
    o_ref[...] = x


def _combine_alias_kernel(x_ref, y_ref, gate_ref, gt_ref, g_ref, prev_ref, o_ref, *, final_norm):
    del prev_ref
    _combine_kernel(x_ref, y_ref, gate_ref, gt_ref, g_ref, o_ref, final_norm=final_norm)


def _combine(x, y4, gates, gate_mod, mod_map, g_final, final_norm, tm, row0, prev):
    r, d = x.shape
    n = y4.shape[1]
    t0 = row0 // tm
    in_specs = [pl.BlockSpec((tm, d), lambda i: (i + t0, 0)),
                pl.BlockSpec((TOP_K, tm, d), lambda i: (0, i, 0)),
                pl.BlockSpec((tm, LANES), lambda i: (i, 0)),
                pl.BlockSpec((1, 1, d), lambda i: (mod_map(i + t0), 0, 0)),
                pl.BlockSpec((1, d), lambda i: (0, 0))]
    args = [x, y4, gates, gate_mod, g_final.reshape(1, d)]
    body, aliases = _combine_kernel, {}
    if prev is not None:
        in_specs.append(pl.BlockSpec(memory_space=pl.ANY))
        args.append(prev)
        body, aliases = _combine_alias_kernel, {len(args) - 1: 0}
    return pl.pallas_call(
        functools.partial(body, final_norm=final_norm),
        out_shape=jax.ShapeDtypeStruct((r, d), F32),
        grid=(n // tm,),
        in_specs=in_specs,
        out_specs=pl.BlockSpec((tm, d), lambda i: (i + t0, 0)),
        input_output_aliases=aliases,
        compiler_params=_params("parallel"),
        name="moe_combine",
    )(*args)


def _moe_parts(tiles):
    total = sum(MOE_SPLIT)
    sizes = [tiles * w // total for w in MOE_SPLIT[:-1]]
    sizes.append(tiles - sum(sizes))
    return [sz for sz in sizes if sz > 0]


def _moe(x, xn, logits, gate_mod, mod_map, weights, g_final, final_norm, tm):
    wgu, bgu, wd, bd = weights
    r, d = xn.shape
    routed, row0 = [], 0
    for tiles in _moe_parts(r // tm):
        n = tiles * tm
        nblk = n * TOP_K // MOE_ROWS + N_EXPERTS
        blk_row0 = jnp.arange(nblk, dtype=I32) * MOE_ROWS
        idx_w, gates_w, pos_w, counts = _route(logits, tm, row0, n)
        idx, pos = idx_w[:, :TOP_K], pos_w[:, :TOP_K]
        counts = counts[0].astype(I32)
        padded = (counts + MOE_ROWS - 1) // MOE_ROWS * MOE_ROWS
        pad_end = jnp.cumsum(padded)
        pad_start = pad_end - padded
        dest = pad_start[idx] + pos
        blk_e = jnp.minimum(jnp.sum((pad_end[None, :] <= blk_row0[:, None]).astype(I32), axis=1), N_EXPERTS - 1)
        n_used = (pad_end[-1:] // MOE_ROWS).astype(I32)
        key = idx * n + jnp.arange(n, dtype=I32)[:, None]
        tok_sorted = jnp.sort(key.reshape(-1)) % n
        grp_start = jnp.cumsum(counts) - counts
        slot = blk_row0[:, None] + jnp.arange(MOE_ROWS, dtype=I32)[None, :]
        compact = jnp.minimum(slot + (grp_start - pad_start)[blk_e][:, None],
                              (grp_start + counts - 1)[blk_e][:, None])
        src = tok_sorted.at[jnp.clip(compact, 0, n * TOP_K - 1).reshape(-1)].get(mode="promise_in_bounds")
        xs = xn.at[src + row0].get(mode="promise_in_bounds")
        routed.append((xs, blk_e, n_used, dest, gates_w, row0, n))
        row0 += n
    ys = [_experts(xs, blk_e, n_used, wgu, bgu, wd, bd) for xs, blk_e, n_used, _, _, _, _ in routed]
    out = None
    for p, (_, _, _, dest, gates_w, row0, n) in enumerate(routed):
        y4 = ys[p].at[dest.T.reshape(-1)].get(mode="promise_in_bounds").reshape(TOP_K, n, d)
        out = _combine(x, y4, gates_w, gate_mod, mod_map, g_final, final_norm, tm, row0, out)
    return out


def _moe_weights(w_gate_up, b_gate_up, w_down, b_down):
    ne, d, ff2 = w_gate_up.shape
    b_tiled = b_gate_up.reshape(ne, ff2 // (2 * LANES), LANES, 2).swapaxes(-1, -2).reshape(ne, 1, ff2)
    return w_gate_up, b_tiled, w_down, b_down[:, None, :]


def _rope_tables(n_tokens):
    rows = n_tokens // GRID_W
    row = jnp.repeat(jnp.arange(rows, dtype=I32), GRID_W).astype(F32)
    col = jnp.tile(jnp.arange(GRID_W, dtype=I32), rows).astype(F32)
    inv = ROPE_BASE ** (-jnp.arange(0, ROPE_AXIS_DIM, 2, dtype=F32) / ROPE_AXIS_DIM)
    ang_r, ang_c = row[:, None] * inv, col[:, None] * inv
    cos = jnp.concatenate([jnp.cos(ang_r)] * 2 + [jnp.cos(ang_c)] * 2, axis=-1)
    sin = jnp.concatenate([jnp.sin(ang_r)] * 2 + [jnp.sin(ang_c)] * 2, axis=-1)
    return cos, sin


def _rot_perm():
    q = ROPE_AXIS_DIM // 2
    d = np.arange(HEAD_DIM)
    first = (d % ROPE_AXIS_DIM) < q
    perm = np.where(first, d + q, d - q)
    sign = np.where(first, -1.0, 1.0).astype(np.float32)
    return perm, sign


def _rot_cols(w, n_heads):
    perm, sign = _rot_perm()
    k = w.shape[0]
    wh = w.reshape(k, n_heads, HEAD_DIM)
    return (wh[:, :, perm] * sign).reshape(k, n_heads * HEAD_DIM)


def _to_heads(t, b, n_heads):
    return t.reshape(b, -1, n_heads, HEAD_DIM).transpose(0, 2, 1, 3)


def _from_heads(t):
    b, h, tt, dh = t.shape
    return t.transpose(0, 2, 1, 3).reshape(b * tt, h * dh)


def _layer_even(xa, mod, b, s, lc, g_mix, g_ffn, w_in, w_out, gate_bias, mlstm_norm, sink, router, moe_w):
    d = xa.shape[1]
    nctx = b * lc
    tm = _tile(math.gcd(nctx, s), (512, 256, 128))
    mod_map = lambda i: jnp.where(i * tm < nctx, b, (i * tm - nctx) // s)
    sh1, sc1, gt1, sh2, sc2, gt2 = [mod[:, j][:, None, :] for j in range(6)]

    hm, hs, hk = MLSTM_HEADS * HEAD_DIM, SWA_HEADS * HEAD_DIM, SWA_KV_HEADS * HEAD_DIM
    o0 = np.cumsum([0, hm, hm, hm, hm, 4 * MLSTM_HEADS, hs, hk, hk])
    seg = lambda j: w_in[:, o0[j]:o0[j + 1]]
    w_cat = jnp.concatenate([seg(0), seg(1) * ATTN_SCALE, seg(2), seg(3),
                             seg(6), _rot_cols(seg(6), SWA_KV_HEADS), seg(7), _pad_cols(seg(4))], axis=1).astype(BF16)
    w_q = jnp.concatenate([seg(5), _rot_cols(seg(5), SWA_HEADS)], axis=1).T.astype(BF16)
    widths = [hm, hm, hm, hm, hk, hk, hk, LANES]
    dts = [BF16] * 7 + [F32]
    qa, ka, va, oa, kb, kb_r, vb, gts, qt, qt_r = _in_proj(xa, g_mix, sc1, sh1, mod_map, w_cat, widths, dts, tm,
                                                           wt=w_q, t_widths=[hs, hs], t_dtypes=[BF16, BF16])

    hf, hb = _mlstm(qa, ka, va, gts, _pad_cols(gate_bias.astype(F32).reshape(1, -1)), b, lc, s)
    mix_a = _mlstm_out(hf, hb, oa, mlstm_norm.reshape(-1), tm)

    cos, sin = _rope_tables(s)
    ones = jnp.ones((HEAD_DIM,), F32)
    g = SWA_HEADS // SWA_KV_HEADS
    heads = lambda t, n: _to_heads(t, b, n)
    log2e = math.log2(math.e)
    q_x = _q_prep_t(qt, qt_r, nctx, b * s, cos.T, sin.T, ones, ones, ATTN_SCALE * log2e, norm=False)
    k_x = _qk_prep(heads(kb[nctx:], SWA_KV_HEADS), heads(kb_r[nctx:], SWA_KV_HEADS), cos, sin, ones, ones,
                   norm=False, rope=True, scale=1.0)
    k_c, v_c, v_x = heads(kb[:nctx], SWA_KV_HEADS), heads(vb[:nctx], SWA_KV_HEADS), heads(vb[nctx:], SWA_KV_HEADS)
    q_c = heads(qt[:, :nctx].T, SWA_HEADS)
    sink_h = sink.astype(F32).reshape(SWA_KV_HEADS, g, 1)
    att_x = _swa(q_x, k_x, v_x.swapaxes(-1, -2), k_c, v_c.swapaxes(-1, -2),
                 jnp.broadcast_to(sink_h * log2e, (SWA_KV_HEADS, g, 2 * SWA_BLOCK)), b, s)
    att_c = _ctx_attn(q_c, k_c, v_c, jnp.broadcast_to(sink_h[..., None], (SWA_KV_HEADS, g, lc, 1))
                      .reshape(SWA_KV_HEADS, g * lc, 1), True)
    mix_b = jnp.concatenate([_from_heads(att_c), att_x], axis=0)

    w_router, b_router = router
    xa, xn2, logits = _out_proj(xa, 0, mix_a, mix_b, w_out[:hm].astype(BF16), w_out[hm:].astype(BF16), gt1, g_ffn,
                                sc2, sh2, mod_map, w_router.astype(F32), b_router.astype(F32), tm)
    return _moe(xa, xn2, logits, gt2, mod_map, moe_w, g_ffn, False, tm)


def _layer_odd_last(xa, mod, b, s, lc, g_mix, g_ffn, w_in, w_out, s5_params, d_skip, w_glu, b_glu,
                    q_norm, k_norm, router, moe_w, g_final):
    nctx = b * lc
    tm = _tile(math.gcd(nctx, s), (512, 256, 128))
    mod_map = lambda i: jnp.where(i * tm < nctx, b, (i * tm - nctx) // s)
    lat_map = lambda i: i * tm // s
    sh1, sc1, gt1, sh2, sc2, gt2 = [mod[:, j][:, None, :] for j in range(6)]

    hq, hk = ATT_HEADS * HEAD_DIM, ATT_KV_HEADS * HEAD_DIM
    o1 = np.cumsum([0, S5_CHANNELS, hq, hk, hk])
    seg = lambda j: w_in[:, o1[j]:o1[j + 1]]
    kpad = lambda w: _pad_cols(w, 2 * LANES)
    w_cat = jnp.concatenate([seg(0), kpad(seg(2)), kpad(_rot_cols(seg(2), ATT_KV_HEADS)), kpad(seg(3))],
                            axis=1).astype(BF16)
    w_q = jnp.concatenate([seg(1), _rot_cols(seg(1), ATT_HEADS)], axis=1).T.astype(BF16)
    widths = [S5_CHANNELS, 2 * LANES, 2 * LANES, 2 * LANES]
    dts = [F32, BF16, BF16, BF16]
    u, k, k_r, v, qt, qt_r = _in_proj(xa, g_mix, sc1, sh1, mod_map, w_cat, widths, dts, tm,
                                      wt=w_q, t_widths=[hq, hq], t_dtypes=[BF16, BF16])

    mix_a = _s5_mixer(u[nctx:].reshape(b, s, S5_CHANNELS), u[:nctx].reshape(b, lc, S5_CHANNELS),
                      _s5_tables(*s5_params), d_skip, w_glu, b_glu)

    cos, sin = _rope_tables(s)
    perm, _ = _rot_perm()
    qn, kn = q_norm.astype(F32), k_norm.astype(F32)
    heads = lambda t, n: _to_heads(t[:, :n * HEAD_DIM], b, n)
    q_x = _q_prep_t(qt, qt_r, nctx, b * s, cos.T, sin.T, qn, qn[perm], ATTN_SCALE * math.log2(math.e))
    k_x = _qk_prep(heads(k[nctx:], ATT_KV_HEADS), heads(k_r[nctx:], ATT_KV_HEADS), cos, sin, kn, kn[perm],
                   norm=True, rope=True, scale=1.0)
    kc_raw = heads(k[:nctx], ATT_KV_HEADS)
    k_c = _qk_prep(kc_raw, kc_raw, cos[:lc], sin[:lc], kn, kn, norm=True, rope=False, scale=1.0)
    k_all = jnp.concatenate([k_c, k_x], axis=2)
    v_all = jnp.concatenate([heads(v[:nctx], ATT_KV_HEADS), heads(v[nctx:], ATT_KV_HEADS)], axis=2)
    mix_b = _flash(q_x, k_all, v_all.swapaxes(-1, -2), b, s)

    w_router, b_router = router
    hs = S5_CHANNELS
    x, xn2, logits = _out_proj(xa, nctx, mix_a, mix_b, w_out[:hs].astype(BF16), w_out[hs:].astype(BF16), gt1, g_ffn,
                               sc2, sh2, lat_map, w_router.astype(F32), b_router.astype(F32), tm)
    return _moe(x, xn2, logits, gt2, lat_map, moe_w, g_final, True, tm)


def kernel(x, c, ctx, c_ctx, l0_w_mod, l0_b_mod, l0_g_mix, l0_g_ffn, l0_w_in, l0_w_out, l0_gate_bias, l0_mlstm_norm, l0_sink, l0_w_router, l0_b_router, l0_w_gate_up, l0_b_gate_up, l0_w_down, l0_b_down, l1_w_mod, l1_b_mod, l1_g_mix, l1_g_ffn, l1_w_in, l1_w_out, l1_lam_re, l1_lam_im, l1_log_dt, l1_b_re, l1_b_im, l1_c_re, l1_c_im, l1_d_skip, l1_w_glu, l1_b_glu, l1_q_norm, l1_k_norm, l1_w_router, l1_b_router, l1_w_gate_up, l1_b_gate_up, l1_w_down, l1_b_down, g_final):
    b, s, d = x.shape
    lc = ctx.shape[1]
    cond = jnp.concatenate([c, c_ctx[None, :]], axis=0)
    cond = jnp.pad(cond, ((0, (-(b + 1)) % 8), (0, 0)))
    mod0 = _silu_linear(cond, l0_w_mod, l0_b_mod)[:b + 1].reshape(b + 1, 6, d)
    mod1 = _silu_linear(cond, l1_w_mod, l1_b_mod)[:b + 1].reshape(b + 1, 6, d)

    moe0 = _moe_weights(l0_w_gate_up, l0_b_gate_up, l0_w_down, l0_b_down)
    moe1 = _moe_weights(l1_w_gate_up, l1_b_gate_up, l1_w_down, l1_b_down)
    xa = jnp.concatenate([ctx.reshape(b * lc, d), x.reshape(b * s, d)], axis=0)
    xa = _layer_even(xa, mod0, b, s, lc, l0_g_mix, l0_g_ffn, l0_w_in, l0_w_out, l0_gate_bias, l0_mlstm_norm,
                     l0_sink, (l0_w_router, l0_b_router), moe0)
    out = _layer_odd_last(xa, mod1, b, s, lc, l1_g_mix, l1_g_ffn, l1_w_in, l1_w_out,
                          (l1_lam_re, l1_lam_im, l1_log_dt, l1_b_re, l1_b_im, l1_c_re, l1_c_im),
                          l1_d_skip, l1_w_glu, l1_b_glu, l1_q_norm, l1_k_norm, (l1_w_router, l1_b_router),
                          moe1, g_final)
    return out.reshape(b, s, d)
```

```python
import functools
import math

import jax
import jax.numpy as jnp
import numpy as np
from jax import lax
from jax.experimental import pallas as pl
from jax.experimental.pallas import tpu as pltpu

F32 = jnp.float32
BF16 = jnp.bfloat16
I32 = jnp.int32

GRID_W = 64
HEAD_DIM = 64
ATTN_SCALE = HEAD_DIM ** -0.5
ROPE_AXIS_DIM = HEAD_DIM // 2
ROPE_BASE = 10000.0
EPS = 1e-6
NEG_INF = -1e30

MLSTM_HEADS = 8
MLSTM_CHUNKS = (256, 128, 64)
SWA_HEADS = 8
SWA_KV_HEADS = 2
SWA_WINDOW = 128
SWA_BLOCK = 128
S5_CHANNELS = 256
S5_GROUP = 16
S5_GROUPS = S5_CHANNELS // S5_GROUP
S5_STATE = 64
S5_CHUNK = 64
ATT_HEADS = 12
ATT_KV_HEADS = 3
N_EXPERTS = 32
TOP_K = 4
SWIGLU_LIMIT = 7.0
SWIGLU_ALPHA = 1.702

LANES = 128
VMEM_LIMIT = 56 * 1024 * 1024
MOE_ROWS = 512
MOE_SPLIT = (1, 1)
HIGHEST = lax.Precision.HIGHEST


def _params(*sem):
    return pltpu.CompilerParams(dimension_semantics=sem, vmem_limit_bytes=VMEM_LIMIT)


def _tile(n, prefs):
    for t in prefs:
        if n % t == 0:
            return t
    return n


def _pad_cols(w, mult=LANES):
    pad = (-w.shape[-1]) % mult
    if pad:
        w = jnp.pad(w, [(0, 0)] * (w.ndim - 1) + [(0, pad)])
    return w


def _linear_kernel(x_ref, w_ref, b_ref, o_ref):
    x = x_ref[...]
    x = x * jax.nn.sigmoid(x)
    o_ref[...] = jnp.dot(x, w_ref[...], precision=HIGHEST, preferred_element_type=F32) + b_ref[...]


def _silu_linear(x, w, b):
    m, k = x.shape
    n = w.shape[1]
    tn = _tile(n, (1024, 512, 256, 128))
    return pl.pallas_call(
        _linear_kernel,
        out_shape=jax.ShapeDtypeStruct((m, n), F32),
        grid=(n // tn,),
        in_specs=[pl.BlockSpec((m, k), lambda j: (0, 0)),
                  pl.BlockSpec((k, tn), lambda j: (0, j)),
                  pl.BlockSpec((1, tn), lambda j: (0, j))],
        out_specs=pl.BlockSpec((m, tn), lambda j: (0, j)),
        compiler_params=_params("arbitrary"),
        name="adaln_linear",
    )(x, w, b.reshape(1, n))


def _in_proj_kernel(x_ref, g_ref, sc_ref, sh_ref, w_ref, wt_ref, *out_refs, widths, t_widths):
    x = x_ref[...]
    xn = x * lax.rsqrt(jnp.mean(x * x, axis=-1, keepdims=True) + EPS) * g_ref[...]
    xb = (xn * (1.0 + sc_ref[0]) + sh_ref[0]).astype(BF16)
    off = 0
    for o_ref, w in zip(out_refs, widths):
        o_ref[...] = jnp.dot(xb, w_ref[:, off:off + w], preferred_element_type=F32).astype(o_ref.dtype)
        off += w
    off = 0
    for o_ref, w in zip(out_refs[len(widths):], t_widths):
        o_ref[...] = lax.dot_general(wt_ref[off:off + w, :], xb, (((1,), (1,)), ((), ())),
                                     preferred_element_type=F32).astype(o_ref.dtype)
        off += w


def _in_proj(x, g, scale, shift, mod_map, w, widths, dtypes, tm, wt=None, t_widths=(), t_dtypes=()):
    r, d = x.shape
    if wt is None:
        wt = jnp.zeros((8, d), BF16)
    return pl.pallas_call(
        functools.partial(_in_proj_kernel, widths=tuple(widths), t_widths=tuple(t_widths)),
        out_shape=([jax.ShapeDtypeStruct((r, wd), dt) for wd, dt in zip(widths, dtypes)]
                   + [jax.ShapeDtypeStruct((wd, r), dt) for wd, dt in zip(t_widths, t_dtypes)]),
        grid=(r // tm,),
        in_specs=[pl.BlockSpec((tm, d), lambda i: (i, 0)),
                  pl.BlockSpec((1, d), lambda i: (0, 0)),
                  pl.BlockSpec((1, 1, d), lambda i: (mod_map(i), 0, 0)),
                  pl.BlockSpec((1, 1, d), lambda i: (mod_map(i), 0, 0)),
                  pl.BlockSpec(w.shape, lambda i: (0, 0)),
                  pl.BlockSpec(wt.shape, lambda i: (0, 0))],
        out_specs=([pl.BlockSpec((tm, wd), lambda i: (i, 0)) for wd in widths]
                   + [pl.BlockSpec((wd, tm), lambda i: (0, i)) for wd in t_widths]),
        compiler_params=_params("parallel"),
        name="in_proj",
    )(x, g.reshape(1, d), scale, shift, w, wt)


def _qk_prep_kernel(x_ref, xr_ref, cos_ref, sin_ref, g_ref, gr_ref, o_ref, *, norm, rope, scale):
    x = x_ref[0].astype(F32)
    if norm:
        s = lax.rsqrt(jnp.mean(x * x, axis=-1, keepdims=True) + EPS)
        x = x * s * g_ref[...]
    if rope:
        xr = xr_ref[0].astype(F32)
        if norm:
            xr = xr * s * gr_ref[...]
        x = x * cos_ref[...] + xr * sin_ref[...]
    o_ref[0] = (x * scale).astype(o_ref.dtype)


def _qk_prep(x, x_rot, cos, sin, gain, gain_rot, *, norm, rope, scale):
    b, h, t, dh = x.shape
    tt = _tile(t, (512, 256, 128))
    blk = pl.BlockSpec((1, h, tt, dh), lambda i, j: (i, 0, j, 0))
    tab = pl.BlockSpec((tt, dh), lambda i, j: (j, 0))
    vec = pl.BlockSpec((1, dh), lambda i, j: (0, 0))
    return pl.pallas_call(
        functools.partial(_qk_prep_kernel, norm=norm, rope=rope, scale=scale),
        out_shape=jax.ShapeDtypeStruct(x.shape, BF16),
        grid=(b, t // tt),
        in_specs=[blk, blk, tab, tab, vec, vec],
        out_specs=blk,
        compiler_params=_params("parallel", "parallel"),
        name="qk_prep",
    )(x, x_rot, cos, sin, gain.reshape(1, dh), gain_rot.reshape(1, dh))


def _q_prep_t_kernel(x_ref, xr_ref, cos_ref, sin_ref, g_ref, gr_ref, o_ref, *, norm, scale):
    w, tt = x_ref.shape
    nh = w // HEAD_DIM
    x = x_ref[...].astype(F32).reshape(nh, HEAD_DIM, tt)
    xr = xr_ref[...].astype(F32).reshape(nh, HEAD_DIM, tt)
    if norm:
        s = lax.rsqrt(jnp.mean(x * x, axis=1, keepdims=True) + EPS)
        x, xr = x * s * g_ref[...], xr * s * gr_ref[...]
    y = x * cos_ref[...] + xr * sin_ref[...]
    o_ref[...] = (y * scale).reshape(w, tt).astype(o_ref.dtype)


def _q_prep_t(xt, xt_rot, col0, n_cols, cos_t, sin_t, gain, gain_rot, scale, norm=True):
    w = xt.shape[0]
    s = cos_t.shape[1]
    tt = _tile(math.gcd(col0, s), (512, 256, 128))
    blk = pl.BlockSpec((w, tt), lambda j: (0, j + col0 // tt))
    tab = pl.BlockSpec((HEAD_DIM, tt), lambda j: (0, j % (s // tt)))
    vec = pl.BlockSpec((HEAD_DIM, 1), lambda j: (0, 0))
    return pl.pallas_call(
        functools.partial(_q_prep_t_kernel, norm=norm, scale=scale),
        out_shape=jax.ShapeDtypeStruct((w, n_cols), BF16),
        grid=(n_cols // tt,),
        in_specs=[blk, blk, tab, tab, vec, vec],
        out_specs=pl.BlockSpec((w, tt), lambda j: (0, j)),
        compiler_params=_params("parallel"),
        name="q_prep_t",
    )(xt, xt_rot, cos_t, sin_t, gain.reshape(HEAD_DIM, 1), gain_rot.reshape(HEAD_DIM, 1))


def _log_sigmoid(x):
    return jnp.minimum(x, 0.0) - jnp.log(1.0 + jnp.exp(-jnp.abs(x)))


def _mlstm_kernel(qf, kf, vf, gf, qb, kb, vb, gb, bias_ref, hf_ref, hb_ref, c_scr, n_scr, m_scr):
    ch, nh, dh = qf.shape[0], MLSTM_HEADS, HEAD_DIM

    @pl.when(pl.program_id(1) == 0)
    def _():
        c_scr[...] = jnp.zeros_like(c_scr)
        n_scr[...] = jnp.zeros_like(n_scr)
        m_scr[...] = jnp.full_like(m_scr, NEG_INF)

    row = lax.broadcasted_iota(I32, (ch, ch), 0)
    col = lax.broadcasted_iota(I32, (ch, ch), 1)
    lane_lo = lax.broadcasted_iota(I32, (ch, 2 * dh), 1) < dh
    row_lo = lax.broadcasted_iota(I32, (2 * dh, 2 * dh), 0) < dh
    col_lo = lax.broadcasted_iota(I32, (2 * dh, 2 * dh), 1) < dh
    vec_lo = lax.broadcasted_iota(I32, (1, 2 * dh), 1) < dh
    pair = lambda a, b: jnp.where(lane_lo, a, b)
    refs = ((qf, kf, vf, gf, hf_ref), (qb, kb, vb, gb, hb_ref))
    jobs = [(d, p) for d in range(2) for p in range(nh // 2)]
    sl = lambda p: slice(2 * dh * p, 2 * dh * (p + 1))

    qk = {}
    for d, p in jobs:
        q2, k2 = refs[d][0][:, sl(p)], refs[d][1][:, sl(p)]
        zero = jnp.zeros_like(q2)
        q_st = jnp.concatenate([jnp.where(lane_lo, q2, zero), jnp.where(lane_lo, zero, q2)], axis=0)
        qk[d, p] = lax.dot_general(q_st, k2, (((1,), (1,)), ((), ())), preferred_element_type=F32)

    st = {}
    for d in range(2):
        seen = (col <= row) if d == 0 else (col >= row)
        tri = seen.astype(F32)
        tri_t = ((row <= col) if d == 0 else (row >= col)).astype(F32)
        last = ch - 1 if d == 0 else 0
        g = refs[d][3][...] + bias_ref[...]
        g_t = g.T
        lo = 2 * nh * d
        log2e = math.log2(math.e)
        li_col, lf_col = g[:, lo:lo + nh] * log2e, _log_sigmoid(g[:, lo + nh:lo + 2 * nh]) * log2e
        li_row, lf_row = g_t[lo:lo + nh, :] * log2e, _log_sigmoid(g_t[lo + nh:lo + 2 * nh, :]) * log2e
        b_col = jnp.dot(tri, lf_col, precision=HIGHEST, preferred_element_type=F32)
        b_row = jnp.dot(lf_row, tri_t, precision=HIGHEST, preferred_element_type=F32)
        src_row = li_row - b_row
        for h in range(nh):
            bc, br = b_col[:, h:h + 1], b_row[h:h + 1, :]
            lic = li_col[:, h:h + 1]
            b_last = br[:, last:last + 1]
            m_prev = m_scr[d, h // 2][:, (h % 2) * dh:(h % 2) * dh + 1]
            d_log = jnp.where(seen, bc + src_row[h:h + 1, :], NEG_INF)
            inter_log = bc + m_prev
            m_t = jnp.maximum(inter_log, jnp.max(d_log, axis=1, keepdims=True))
            w_log = b_last - bc + lic
            m_new = jnp.maximum(b_last + m_prev, jnp.max(w_log, axis=0, keepdims=True))
            st[d, h] = dict(dmat=jnp.exp2(d_log - m_t), inter=jnp.exp2(inter_log - m_t), floor=jnp.exp2(-m_t),
                            m_new=m_new, decay=jnp.exp2(b_last + m_prev - m_new), wn=jnp.exp2(w_log - m_new))

    mm = {}
    for d, p in jobs:
        a, b = st[d, 2 * p], st[d, 2 * p + 1]
        q2, k2, v2 = refs[d][0][:, sl(p)], refs[d][1][:, sl(p)], refs[d][2][:, sl(p)]
        s = qk[d, p] * jnp.concatenate([a["dmat"], b["dmat"]], axis=0)
        kw = k2.astype(F32) * pair(a["wn"], b["wn"])
        mm[d, p] = dict(
            s_sum=jnp.sum(s, axis=1, keepdims=True), kw_sum=jnp.sum(kw, axis=0, keepdims=True),
            sv=jnp.dot(s.astype(BF16), v2, preferred_element_type=F32),
            q_c=jnp.dot(q2, c_scr[d, p].astype(BF16), preferred_element_type=F32),
            kv=lax.dot_general(kw.astype(BF16), v2, (((0,), (0,)), ((), ())), preferred_element_type=F32))

    for d, p in jobs:
        a, b, r = st[d, 2 * p], st[d, 2 * p + 1], mm[d, p]
        q2 = refs[d][0][:, sl(p)]
        n_prev = n_scr[d, p]
        num = pair(a["inter"], b["inter"]) * r["q_c"] + jnp.where(lane_lo, r["sv"][:ch], r["sv"][ch:])
        qn = q2.astype(F32) * n_prev
        qn_a = jnp.sum(jnp.where(lane_lo, qn, 0.0), axis=1, keepdims=True)
        qn_b = jnp.sum(jnp.where(lane_lo, 0.0, qn), axis=1, keepdims=True)
        den_a = jnp.maximum(jnp.abs(a["inter"] * qn_a + r["s_sum"][:ch]), a["floor"])
        den_b = jnp.maximum(jnp.abs(b["inter"] * qn_b + r["s_sum"][ch:]), b["floor"])
        refs[d][4][:, sl(p)] = num / pair(den_a, den_b)
        c_scr[d, p] = (jnp.where(row_lo, a["decay"], b["decay"]) * c_scr[d, p]
                       + jnp.where(row_lo == col_lo, r["kv"], 0.0))
        n_scr[d, p] = jnp.where(vec_lo, a["decay"], b["decay"]) * n_prev + r["kw_sum"]
        m_scr[d, p] = jnp.where(vec_lo, a["m_new"], b["m_new"])


def _mlstm(q, k, v, gates, bias, b, lc, s):
    r, w = q.shape
    ch = _tile(math.gcd(lc, s), MLSTM_CHUNKS)
    ncc, ncx = lc // ch, s // ch
    base = b * ncc

    def fwd(i, c):
        return jnp.where(c < ncc, i * ncc + c, base + i * ncx + (c - ncc)), 0

    def bwd(i, c):
        return jnp.where(c < ncc, i * ncc + (ncc - 1 - c), base + i * ncx + (ncx - 1 - (c - ncc))), 0

    spec = lambda width, m: pl.BlockSpec((ch, width), m)
    npair = MLSTM_HEADS // 2
    return pl.pallas_call(
        _mlstm_kernel,
        out_shape=[jax.ShapeDtypeStruct((r, w), F32)] * 2,
        grid=(b, ncc + ncx),
        in_specs=[spec(w, fwd), spec(w, fwd), spec(w, fwd), spec(LANES, fwd),
                  spec(w, bwd), spec(w, bwd), spec(w, bwd), spec(LANES, bwd),
                  pl.BlockSpec((1, LANES), lambda i, c: (0, 0))],
        out_specs=[spec(w, fwd), spec(w, bwd)],
        scratch_shapes=[pltpu.VMEM((2, npair, 2 * HEAD_DIM, 2 * HEAD_DIM), F32),
                        pltpu.VMEM((2, npair, 1, 2 * HEAD_DIM), F32),
                        pltpu.VMEM((2, npair, 1, 2 * HEAD_DIM), F32)],
        compiler_params=_params("parallel", "arbitrary"),
        name="mlstm_scan",
    )(q, k, v, gates, q, k, v, gates, bias)


def _mlstm_out_kernel(hf_ref, hb_ref, o_ref, nrm_ref, ones_ref, y_ref):
    h = hf_ref[...] + hb_ref[...]
    ms = jnp.dot((h * h).astype(BF16), ones_ref[...], preferred_element_type=F32) * (1.0 / HEAD_DIM)
    hn = h * lax.rsqrt(ms + EPS) * nrm_ref[...]
    y_ref[...] = (jax.nn.sigmoid(o_ref[...].astype(F32)) * hn).astype(y_ref.dtype)


def _head_ones(width):
    idx = np.arange(width) // HEAD_DIM
    return jnp.asarray(idx[:, None] == idx[None, :], BF16)


def _mlstm_out(hf, hb, o, norm, tm):
    r, w = hf.shape
    blk = pl.BlockSpec((tm, w), lambda i: (i, 0))
    return pl.pallas_call(
        _mlstm_out_kernel,
        out_shape=jax.ShapeDtypeStruct((r, w), BF16),
        grid=(r // tm,),
        in_specs=[blk, blk, blk,
                  pl.BlockSpec((1, w), lambda i: (0, 0)),
                  pl.BlockSpec((w, w), lambda i: (0, 0))],
        out_specs=blk,
        compiler_params=_params("parallel"),
        name="mlstm_out",
    )(hf, hb, o, norm.reshape(1, w), _head_ones(w))


def _swa_kernel(q_ref, k0, k1, k2, k3, v0, v1, v2, v3, kc_ref, vc_ref, sink_ref, o_ref, *, seq):
    w, dh = SWA_BLOCK, HEAD_DIM
    tq = q_ref.shape[1]
    g = q_ref.shape[0] // dh
    i = pl.program_id(2)
    k_loc = jnp.concatenate([k0[...], k1[...], k2[...], k3[...]], axis=1)
    v_loc = jnp.concatenate([v0[...], v1[...], v2[...], v3[...]], axis=1)
    kpos = (2 * i - 1) * w + lax.broadcasted_iota(I32, (4 * w, tq), 0)
    qpos = i * tq + lax.broadcasted_iota(I32, (4 * w, tq), 1)
    valid = (jnp.abs(qpos - kpos) <= SWA_WINDOW) & (kpos >= 0) & (kpos < seq)
    t_dot = lambda kt, q: lax.dot_general(kt, q, (((0,), (0,)), ((), ())), preferred_element_type=F32)
    qs = [q_ref[h * dh:(h + 1) * dh, :] for h in range(g)]
    s_loc = [jnp.where(valid, t_dot(k_loc, qs[h]), NEG_INF) for h in range(g)]
    s_ctx = [t_dot(kc_ref[...], qs[h]) for h in range(g)]
    outs = []
    for h in range(g):
        sink = sink_ref[0, h:h + 1, :]
        m = jnp.maximum(jnp.maximum(jnp.max(s_loc[h], axis=0, keepdims=True),
                                    jnp.max(s_ctx[h], axis=0, keepdims=True)), sink)
        p_loc = jnp.exp2(s_loc[h] - m)
        p_ctx = jnp.exp2(s_ctx[h] - m)
        den = jnp.sum(p_loc, axis=0, keepdims=True) + jnp.sum(p_ctx, axis=0, keepdims=True) + jnp.exp2(sink - m)
        o = (jnp.dot(v_loc, p_loc.astype(BF16), preferred_element_type=F32)
             + jnp.dot(vc_ref[...], p_ctx.astype(BF16), preferred_element_type=F32))
        outs.append(o / den)
    o_ref[...] = jnp.concatenate(outs, axis=0).T.astype(o_ref.dtype)


def _swa(qt, kt, kvt, sink2, b, s, lc):
    dh = HEAD_DIM
    nh, hkv = qt.shape[0] // dh, kt.shape[0] // dh
    g = nh // hkv
    w = SWA_BLOCK
    tq = 2 * w
    nb, nq = s // w, s // tq
    c0 = b * lc // w
    clampi = lambda j: jnp.clip(j, 0, nb - 1)
    kspec = lambda o: pl.BlockSpec((dh, w), lambda bi, hi, i: (hi, bi * nb + clampi(2 * i + o)))
    vspec = lambda o: pl.BlockSpec((dh, w), lambda bi, hi, i: (hi, c0 + bi * nb + clampi(2 * i + o)))
    cspec = pl.BlockSpec((dh, lc), lambda bi, hi, i: (hi, bi))
    k_all, v_all = kvt
    return pl.pallas_call(
        functools.partial(_swa_kernel, seq=s),
        out_shape=jax.ShapeDtypeStruct((b * s, nh * dh), BF16),
        grid=(b, hkv, nq),
        in_specs=[pl.BlockSpec((g * dh, tq), lambda bi, hi, i: (hi, bi * nq + i)),
                  kspec(-1), kspec(0), kspec(1), kspec(2), vspec(-1), vspec(0), vspec(1), vspec(2),
                  cspec, cspec, pl.BlockSpec((1, g, tq), lambda bi, hi, i: (hi, 0, 0))],
        out_specs=pl.BlockSpec((tq, g * dh), lambda bi, hi, i: (bi * nq + i, hi)),
        compiler_params=_params("parallel", "parallel", "arbitrary"),
        name="window_attention",
    )(qt, kt, kt, kt, kt, v_all, v_all, v_all, v_all, k_all, v_all, sink2)


def _ctx_attn_kernel(q_ref, k_ref, v_ref, sink_ref, o_ref, *, use_sink):
    g, lq, dh = q_ref.shape[1:]
    q = q_ref[0].reshape(g * lq, dh)
    s = lax.dot_general(q, k_ref[0, 0], (((1,), (1,)), ((), ())), preferred_element_type=F32) * ATTN_SCALE
    m = jnp.max(s, axis=1, keepdims=True)
    if use_sink:
        m = jnp.maximum(m, sink_ref[0])
    p = jnp.exp(s - m)
    den = jnp.sum(p, axis=1, keepdims=True)
    if use_sink:
        den = den + jnp.exp(sink_ref[0] - m)
    o = jnp.dot(p.astype(BF16), v_ref[0, 0], preferred_element_type=F32) / den
    o_ref[0] = o.reshape(g, lq, dh).astype(o_ref.dtype)


def _ctx_attn(q, k, v, sink_col, use_sink):
    b, nh, lq, dh = q.shape
    hkv = k.shape[1]
    g = nh // hkv
    qblk = pl.BlockSpec((1, g, lq, dh), lambda bi, hi: (bi, hi, 0, 0))
    kblk = pl.BlockSpec((1, 1, k.shape[2], dh), lambda bi, hi: (bi, hi, 0, 0))
    return pl.pallas_call(
        functools.partial(_ctx_attn_kernel, use_sink=use_sink),
        out_shape=jax.ShapeDtypeStruct(q.shape, BF16),
        grid=(b, hkv),
        in_specs=[qblk, kblk, kblk, pl.BlockSpec((1, g * lq, 1), lambda bi, hi: (hi, 0, 0))],
        out_specs=qblk,
        compiler_params=_params("parallel", "parallel"),
        name="context_attention",
    )(q, k, v, sink_col)


def _flash_kernel(q_ref, k_ref, v_ref, o_ref, m_scr, l_scr, acc_scr, sa_scr, sb_scr, *, tk):
    dh = HEAD_DIM
    g = q_ref.shape[0] // dh
    n = k_ref.shape[2] // tk
    m_scr[...] = jnp.full_like(m_scr, NEG_INF)
    l_scr[...] = jnp.zeros_like(l_scr)
    acc_scr[...] = jnp.zeros_like(acc_scr)

    def scores(j, dst):
        kk = k_ref[0, 0, pl.ds(pl.multiple_of(j * tk, tk), tk), :]
        for h in range(g):
            dst[h] = jnp.dot(kk, q_ref[h * dh:(h + 1) * dh, :], preferred_element_type=F32)

    def update(j, src):
        vv = v_ref[0, 0, :, pl.ds(pl.multiple_of(j * tk, tk), tk)]
        ss = [src[h] for h in range(g)]
        m_olds = [m_scr[h] for h in range(g)]
        m_news = [jnp.maximum(m_olds[h], jnp.max(ss[h], axis=0, keepdims=True)) for h in range(g)]
        ps = [jnp.exp2(ss[h] - m_news[h]) for h in range(g)]
        pvs = [jnp.dot(vv, ps[h].astype(BF16), preferred_element_type=F32) for h in range(g)]
        for h in range(g):
            alpha = jnp.exp2(m_olds[h] - m_news[h])
            l_scr[h] = alpha * l_scr[h] + jnp.sum(ps[h], axis=0, keepdims=True)
            acc_scr[h] = alpha * acc_scr[h] + pvs[h]
            m_scr[h] = m_news[h]

    scores(0, sa_scr)

    def body(i, carry):
        scores(2 * i + 1, sb_scr)
        update(2 * i, sa_scr)
        scores(2 * i + 2, sa_scr)
        update(2 * i + 1, sb_scr)
        return carry

    lax.fori_loop(0, (n - 1) // 2, body, 0)
    if n % 2 == 1:
        update(n - 1, sa_scr)
    else:
        scores(n - 1, sb_scr)
        update(n - 2, sa_scr)
        update(n - 1, sb_scr)
    out = (acc_scr[...] / l_scr[...]).reshape(g * dh, -1)
    o_ref[...] = out.T.astype(o_ref.dtype)


def _flash(qt, k, vt, b, s, tq=None, tk=None):
    w = qt.shape[0]
    hkv, t, dh = k.shape[1], k.shape[2], k.shape[3]
    g = w // dh // hkv
    tq = tq or _tile(s, (256, 128))
    tk = tk or _tile(t, (256, 128))
    nq = s // tq
    return pl.pallas_call(
        functools.partial(_flash_kernel, tk=tk),
        out_shape=jax.ShapeDtypeStruct((b * s, w), BF16),
        grid=(b, hkv, nq),
        in_specs=[pl.BlockSpec((g * dh, tq), lambda bi, hi, i: (hi, bi * nq + i)),
                  pl.BlockSpec((1, 1, t, dh), lambda bi, hi, i: (bi, hi, 0, 0)),
                  pl.BlockSpec((1, 1, dh, t), lambda bi, hi, i: (bi, hi, 0, 0))],
        out_specs=pl.BlockSpec((tq, g * dh), lambda bi, hi, i: (bi * nq + i, hi)),
        scratch_shapes=[pltpu.VMEM((g, 1, tq), F32), pltpu.VMEM((g, 1, tq), F32), pltpu.VMEM((g, dh, tq), F32),
                        pltpu.VMEM((g, tk, tq), F32), pltpu.VMEM((g, tk, tq), F32)],
        compiler_params=_params("parallel", "parallel", "arbitrary"),
        name="dense_attention",
    )(qt, k, vt)


def _bmm_kernel(a_ref, b_ref, o_ref):
    o_ref[0] = jnp.dot(a_ref[0], b_ref[0], preferred_element_type=F32).astype(o_ref.dtype)


def _group_matmul(a, bmat, name):
    g, m, k = a.shape
    n = bmat.shape[2]
    tm = _tile(m, (1024, 512, 256, 128, 64, 32, 16, 8))
    return pl.pallas_call(
        _bmm_kernel,
        out_shape=jax.ShapeDtypeStruct((g, m, n), F32),
        grid=(g, m // tm),
        in_specs=[pl.BlockSpec((1, tm, k), lambda gi, i: (gi, i, 0)),
                  pl.BlockSpec((1, k, n), lambda gi, i: (gi, 0, 0))],
        out_specs=pl.BlockSpec((1, tm, n), lambda gi, i: (gi, i, 0)),
        compiler_params=_params("parallel", "parallel"),
        name=name,
    )(a, bmat)


def _s5_scan_kernel(sre_ref, sim_ref, are_ref, aim_ref, zre_ref, zim_ref):
    nd, nsteps = sre_ref.shape[0], sre_ref.shape[1]
    for d in range(nd):
        a_re, a_im = are_ref[d], aim_ref[d]

        def body(i, carry, d=d, a_re=a_re, a_im=a_im):
            z_re, z_im = carry
            zre_ref[d, i] = z_re
            zim_ref[d, i] = z_im
            return (a_re * z_re - a_im * z_im + sre_ref[d, i], a_re * z_im + a_im * z_re + sim_ref[d, i])

        zero = jnp.zeros(sre_ref.shape[2:], F32)
        lax.fori_loop(0, nsteps, body, (zero, zero))


def _s5_scan(s_re, s_im, a_re, a_im):
    full = lambda arr: pl.BlockSpec(arr.shape, lambda i: (0,) * arr.ndim)
    return pl.pallas_call(
        _s5_scan_kernel,
        out_shape=[jax.ShapeDtypeStruct(s_re.shape, F32)] * 2,
        grid=(1,),
        in_specs=[full(s_re), full(s_im), full(a_re), full(a_im)],
        out_specs=[full(s_re), full(s_im)],
        compiler_params=_params("arbitrary"),
        name="s5_chunk_scan",
    )(s_re, s_im, a_re, a_im)


def _s5_glu_kernel(y_ref, u_ref, d_ref, w_ref, b_ref, o_ref):
    y = y_ref[...] + d_ref[...] * u_ref[...]
    y = jax.nn.gelu(y)
    gate = jnp.dot(y.astype(BF16), w_ref[...], preferred_element_type=F32) + b_ref[...]
    o_ref[...] = (y * jax.nn.sigmoid(gate)).astype(o_ref.dtype)


def _s5_glu(y, u, d_skip, w_glu, b_glu, tm):
    r, c = y.shape
    blk = pl.BlockSpec((tm, c), lambda i: (i, 0))
    vec = pl.BlockSpec((1, c), lambda i: (0, 0))
    return pl.pallas_call(
        _s5_glu_kernel,
        out_shape=jax.ShapeDtypeStruct((r, c), BF16),
        grid=(r // tm,),
        in_specs=[blk, blk, vec, pl.BlockSpec((c, c), lambda i: (0, 0)), vec],
        out_specs=blk,
        compiler_params=_params("parallel"),
        name="s5_readout_glu",
    )(y, u, d_skip.reshape(1, c), w_glu.astype(BF16), b_glu.reshape(1, c))


def _s5_tables(lam_re, lam_im, log_dt, b_re, b_im, c_re, c_im):
    ln, p, gc = S5_CHUNK, S5_STATE, S5_GROUP
    lam = lax.complex(lam_re.astype(F32), lam_im.astype(F32))
    dt = jnp.exp(log_dt.astype(F32))[..., None]
    a_bar = jnp.exp(lam * dt)
    b_scale = (a_bar - 1.0) / lam
    b_mat = lax.complex(b_re.astype(F32), b_im.astype(F32))
    c_mat = lax.complex(c_re.astype(F32), c_im.astype(F32))
    tau = jnp.arange(ln + 1, dtype=F32)
    apow = jnp.exp((lam * dt)[:, :, None, :] * tau[None, None, :, None])
    drive = b_scale[..., None] * b_mat[None]
    kern = jnp.real(jnp.einsum('gcp,dgtp,dgpe->dgtce', c_mat, apow[:, :, :ln], drive))
    kc = jnp.concatenate([kern[1, :, :0:-1], (kern[0, :, :1] + kern[1, :, :1]), kern[0, :, 1:]], axis=1)
    kc_e = kc.transpose(0, 3, 1, 2).reshape(S5_GROUPS, gc, (2 * ln - 1) * gc)
    toep = jnp.stack([kc_e[:, :, (ln - 1 - s) * gc:(2 * ln - 1 - s) * gc] for s in range(ln)], axis=1)
    toep = toep.reshape(S5_GROUPS, ln * gc, ln * gc)
    w_f = apow[0, :, ln - 1::-1][:, :ln, :, None] * drive[0][:, None]
    w_b = apow[1, :, :ln, :, None] * drive[1][:, None]
    def m_in(wc):
        wt = wc.transpose(0, 1, 3, 2).reshape(S5_GROUPS, ln * gc, p)
        return jnp.concatenate([jnp.real(wt), jnp.imag(wt)], axis=-1)
    min_all = jnp.concatenate([m_in(w_f), m_in(w_b)], axis=-1)
    o_f = c_mat[:, None] * apow[0, :, 1:ln + 1][:, :, None, :]
    o_b = c_mat[:, None] * apow[1, :, ln:0:-1][:, :, None, :]
    def m_out(oc):
        ot = oc.transpose(0, 3, 1, 2).reshape(S5_GROUPS, p, ln * gc)
        return jnp.concatenate([jnp.real(ot), -jnp.imag(ot)], axis=1)
    rhs = jnp.concatenate([toep, m_out(o_f), m_out(o_b)], axis=1)
    a_l = apow[:, :, ln].reshape(2, 1, S5_GROUPS * p)
    return min_all.astype(BF16), rhs.astype(BF16), jnp.real(a_l), jnp.imag(a_l)


def _s5_mixer(u_x, u_c, tables, d_skip, w_glu, b_glu):
    min_all, rhs, a_re, a_im = tables
    b, s, _ = u_x.shape
    ln, p, gc, ng = S5_CHUNK, S5_STATE, S5_GROUP, S5_GROUPS

    def chunks(u):
        nc = u.shape[1] // ln
        return u.reshape(b, nc, ln, ng, gc).transpose(3, 0, 1, 2, 4).reshape(ng, b * nc, ln * gc).astype(BF16), nc

    ux, ncx = chunks(u_x)
    uc, ncc = chunks(u_c)
    sx = _group_matmul(ux, min_all, "s5_local_state").reshape(ng, b, ncx, 4, p)
    sc = _group_matmul(uc, min_all, "s5_local_state_ctx").reshape(ng, b, ncc, 4, p)

    def scan_order(part):
        f = jnp.concatenate([sc[:, :, :, part], sx[:, :, :, part]], axis=2)
        r = jnp.concatenate([sc[:, :, ::-1, part + 2], sx[:, :, ::-1, part + 2]], axis=2)
        return jnp.stack([f, r]).transpose(0, 3, 2, 1, 4).reshape(2, ncc + ncx, b, ng * p)

    z_re, z_im = _s5_scan(scan_order(0), scan_order(1), a_re, a_im)

    def latent(z, d):
        zl = z[d, ncc:]
        if d == 1:
            zl = zl[::-1]
        return zl.reshape(ncx, b, ng, p).transpose(2, 1, 0, 3).reshape(ng, b * ncx, p)

    lhs = jnp.concatenate([ux, latent(z_re, 0).astype(BF16), latent(z_im, 0).astype(BF16),
                           latent(z_re, 1).astype(BF16), latent(z_im, 1).astype(BF16)], axis=-1)
    y = _group_matmul(lhs, rhs, "s5_outputs")
    y = y.reshape(ng, b, ncx, ln, gc).transpose(1, 2, 3, 0, 4).reshape(b * s, S5_CHANNELS)
    return _s5_glu(y, u_x.reshape(b * s, S5_CHANNELS), d_skip, w_glu, b_glu, _tile(b * s, (1024, 512, 256)))


def _out_proj_kernel(x_ref, a_ref, b_ref, wa_ref, wb_ref, gt_ref, g_ref, sc_ref, sh_ref, wr_ref, br_ref,
                     xo_ref, xn_ref, lg_ref):
    y = (jnp.dot(a_ref[...], wa_ref[...], preferred_element_type=F32)
         + jnp.dot(b_ref[...], wb_ref[...], preferred_element_type=F32))
    x = x_ref[...] + gt_ref[0] * y
    xo_ref[...] = x
    xn = x * lax.rsqrt(jnp.mean(x * x, axis=-1, keepdims=True) + EPS) * g_ref[...]
    xn = xn * (1.0 + sc_ref[0]) + sh_ref[0]
    hi = xn.astype(BF16)
    xn_ref[...] = hi
    lo = (xn - hi.astype(F32)).astype(BF16)
    part = jnp.dot(hi, wr_ref[...], preferred_element_type=F32)
    part = part[:, :LANES] + part[:, LANES:] + jnp.dot(lo, wr_ref[:, :LANES], preferred_element_type=F32)
    lg_ref[...] = part[:, :lg_ref.shape[1]] + br_ref[...]


def _out_proj(x, row0, a, bmix, wa, wb, gate, g_ffn, scale, shift, mod_map, w_router, b_router, tm):
    r = a.shape[0]
    d = x.shape[1]
    t0 = row0 // tm
    ne = w_router.shape[1]
    w_hi = w_router.astype(BF16)
    w_lo = (w_router - w_hi.astype(F32)).astype(BF16)
    w_router = jnp.concatenate([_pad_cols(w_hi), _pad_cols(w_lo)], axis=1)
    row = lambda w: pl.BlockSpec((tm, w), lambda i: (i, 0))
    mod = pl.BlockSpec((1, 1, d), lambda i: (mod_map(i), 0, 0))
    full = lambda arr: pl.BlockSpec(arr.shape, lambda i: (0, 0))
    return pl.pallas_call(
        _out_proj_kernel,
        out_shape=[jax.ShapeDtypeStruct((r, d), F32), jax.ShapeDtypeStruct((r, d), BF16),
                   jax.ShapeDtypeStruct((r, ne), F32)],
        grid=(r // tm,),
        in_specs=[pl.BlockSpec((tm, d), lambda i: (i + t0, 0)), row(a.shape[1]), row(bmix.shape[1]),
                  full(wa), full(wb), mod, pl.BlockSpec((1, d), lambda i: (0, 0)), mod, mod,
                  full(w_router), pl.BlockSpec((1, ne), lambda i: (0, 0))],
        out_specs=[row(d), row(d), row(ne)],
        compiler_params=_params("parallel"),
        name="out_proj",
    )(x, a, bmix, wa, wb, gate, g_ffn.reshape(1, d), scale, shift, w_router, b_router.reshape(1, ne))


def _route_kernel(lg_ref, idx_ref, gate_ref, pos_ref, cnt_ref, carry):
    tm, ne = lg_ref.shape

    @pl.when(pl.program_id(0) == 0)
    def _():
        carry[...] = jnp.zeros_like(carry)

    work = lg_ref[...]
    lane = lax.broadcasted_iota(I32, (tm, ne), 1).astype(F32)
    out_lane = lax.broadcasted_iota(I32, (tm, LANES), 1)
    vals, hots, idx_out = [], [], jnp.zeros((tm, LANES), I32)
    for kk in range(TOP_K):
        mx = jnp.max(work, axis=1, keepdims=True)
        idx = jnp.min(jnp.where(work == mx, lane, float(ne)), axis=1, keepdims=True)
        hot = lane == idx
        work = jnp.where(hot, -jnp.inf, work)
        vals.append(mx)
        hots.append(hot)
        idx_out = jnp.where(out_lane == kk, idx.astype(I32), idx_out)
    exps = [jnp.exp(vv - vals[0]) for vv in vals]
    tot = exps[0] + exps[1] + exps[2] + exps[3]
    multi = (hots[0] | hots[1] | hots[2] | hots[3]).astype(BF16)
    r_i = lax.broadcasted_iota(I32, (tm, tm), 0)
    c_i = lax.broadcasted_iota(I32, (tm, tm), 1)
    before = (r_i > c_i).astype(BF16)
    prefix = jnp.dot(before, multi, preferred_element_type=F32) + carry[...]
    gate_out = jnp.zeros((tm, LANES), F32)
    pos_out = jnp.zeros((tm, LANES), I32)
    for kk in range(TOP_K):
        gate_out = jnp.where(out_lane == kk, exps[kk] / tot, gate_out)
        pos = jnp.sum(jnp.where(hots[kk], prefix, 0.0), axis=1, keepdims=True).astype(I32)
        pos_out = jnp.where(out_lane == kk, pos, pos_out)
    idx_ref[...] = idx_out
    gate_ref[...] = gate_out
    pos_ref[...] = pos_out
    carry[...] = carry[...] + jnp.sum(multi.astype(F32), axis=0, keepdims=True)
    cnt_ref[...] = carry[...]


def _route(logits, tm, row0, n):
    ne = logits.shape[1]
    t0 = row0 // tm
    wide = pl.BlockSpec((tm, LANES), lambda i: (i, 0))
    return pl.pallas_call(
        _route_kernel,
        out_shape=[jax.ShapeDtypeStruct((n, LANES), I32), jax.ShapeDtypeStruct((n, LANES), F32),
                   jax.ShapeDtypeStruct((n, LANES), I32), jax.ShapeDtypeStruct((1, ne), F32)],
        grid=(n // tm,),
        in_specs=[pl.BlockSpec((tm, ne), lambda i: (i + t0, 0))],
        out_specs=[wide, wide, wide, pl.BlockSpec((1, ne), lambda i: (0, 0))],
        scratch_shapes=[pltpu.VMEM((1, ne), F32)],
        compiler_params=_params("arbitrary"),
        name="moe_route",
    )(logits)


def _expert_kernel(be_ref, nb_ref, x_ref, wgu_ref, bgu_ref, wd_ref, bd_ref, perm_ref, o_ref, wgu_bf, wd_bf):
    i = pl.program_id(0)
    fresh = jnp.logical_or(i == 0, be_ref[i] != be_ref[jnp.maximum(i - 1, 0)])

    @pl.when(jnp.logical_and(i < nb_ref[0], fresh))
    def _():
        tile = perm_ref.shape[0]
        for t in range(wgu_ref.shape[2] // tile):
            sl = slice(t * tile, (t + 1) * tile)
            wgu_bf[:, sl] = jnp.dot(wgu_ref[0, :, sl].astype(BF16), perm_ref[...],
                                    preferred_element_type=F32).astype(BF16)
        wd_bf[...] = wd_ref[0].astype(BF16)

    @pl.when(i < nb_ref[0])
    def _():
        h = jnp.dot(x_ref[...], wgu_bf[...], preferred_element_type=F32) + bgu_ref[0]
        acts = []
        for t in range(h.shape[1] // (2 * LANES)):
            glu = jnp.minimum(h[:, 2 * LANES * t:2 * LANES * t + LANES], SWIGLU_LIMIT)
            lin = jnp.clip(h[:, 2 * LANES * t + LANES:2 * LANES * (t + 1)], -SWIGLU_LIMIT, SWIGLU_LIMIT)
            acts.append((glu * jax.nn.sigmoid(SWIGLU_ALPHA * glu) * (lin + 1.0)).astype(BF16))
        act = jnp.concatenate(acts, axis=1)
        o_ref[...] = (jnp.dot(act, wd_bf[...], preferred_element_type=F32) + bd_ref[0]).astype(o_ref.dtype)

    @pl.when(i >= nb_ref[0])
    def _():
        o_ref[...] = jnp.zeros_like(o_ref)


def _experts(xs, blk_e, n_used, wgu, bgu, wd, bd):
    n, d = xs.shape
    ff2 = wgu.shape[2]
    nblk = n // MOE_ROWS
    tile = 2 * LANES
    j = np.arange(tile)
    col = np.where(j < LANES, 2 * j, 2 * (j - LANES) + 1)
    perm = jnp.asarray(np.arange(tile)[:, None] == col[None, :], BF16)
    wspec = lambda shape: pl.BlockSpec((1,) + shape, lambda i, be, nb: (be[i], 0, 0))
    return pl.pallas_call(
        _expert_kernel,
        out_shape=jax.ShapeDtypeStruct((n, d), BF16),
        grid_spec=pltpu.PrefetchScalarGridSpec(
            num_scalar_prefetch=2,
            grid=(nblk,),
            in_specs=[pl.BlockSpec((MOE_ROWS, d), lambda i, be, nb: (i, 0)),
                      wspec((d, ff2)), wspec((1, ff2)), wspec((ff2 // 2, d)), wspec((1, d)),
                      pl.BlockSpec((tile, tile), lambda i, be, nb: (0, 0))],
            out_specs=pl.BlockSpec((MOE_ROWS, d), lambda i, be, nb: (i, 0)),
            scratch_shapes=[pltpu.VMEM((d, ff2), BF16), pltpu.VMEM((ff2 // 2, d), BF16)]),
        compiler_params=_params("arbitrary"),
        name="moe_experts",
    )(blk_e, n_used, xs, wgu, bgu, wd, bd, perm)


def _combine_kernel(x_ref, y_ref, gate_ref, gt_ref, g_ref, o_ref, *, final_norm):
    gates = gate_ref[...]
    y = gates[:, 0:1] * y_ref[0].astype(F32)
    for kk in range(1, TOP_K):
        y = y + gates[:, kk:kk + 1] * y_ref[kk].astype(F32)
    x = x_ref[...] + gt_ref[0] * y
    if final_norm:
        x = x * lax.rsqrt(jnp.mean(x * x, axis=-1, keepdims=True) + EPS) * g_ref[...]
    o_ref[...] = x


def _combine_alias_kernel(x_ref, y_ref, gate_ref, gt_ref, g_ref, prev_ref, o_ref, *, final_norm):
    del prev_ref
    _combine_kernel(x_ref, y_ref, gate_ref, gt_ref, g_ref, o_ref, final_norm=final_norm)


def _combine(x, y4, gates, gate_mod, mod_map, g_final, final_norm, tm, row0, prev):
    r, d = x.shape
    n = y4.shape[1]
    t0 = row0 // tm
    in_specs = [pl.BlockSpec((tm, d), lambda i: (i + t0, 0)),
                pl.BlockSpec((TOP_K, tm, d), lambda i: (0, i, 0)),
                pl.BlockSpec((tm, LANES), lambda i: (i, 0)),
                pl.BlockSpec((1, 1, d), lambda i: (mod_map(i + t0), 0, 0)),
                pl.BlockSpec((1, d), lambda i: (0, 0))]
    args = [x, y4, gates, gate_mod, g_final.reshape(1, d)]
    body, aliases = _combine_kernel, {}
    if prev is not None:
        in_specs.append(pl.BlockSpec(memory_space=pl.ANY))
        args.append(prev)
        body, aliases = _combine_alias_kernel, {len(args) - 1: 0}
    return pl.pallas_call(
        functools.partial(body, final_norm=final_norm),
        out_shape=jax.ShapeDtypeStruct((r, d), F32),
        grid=(n // tm,),
        in_specs=in_specs,
        out_specs=pl.BlockSpec((tm, d), lambda i: (i + t0, 0)),
        input_output_aliases=aliases,
        compiler_params=_params("parallel"),
        name="moe_combine",
    )(*args)


def _moe_parts(tiles):
    total = sum(MOE_SPLIT)
    sizes = [tiles * w // total for w in MOE_SPLIT[:-1]]
    sizes.append(tiles - sum(sizes))
    return [sz for sz in sizes if sz > 0]


def _moe(x, xn, logits, gate_mod, mod_map, weights, g_final, final_norm, tm):
    wgu, bgu, wd, bd = weights
    r, d = xn.shape
    routed, row0 = [], 0
    for tiles in _moe_parts(r // tm):
        n = tiles * tm
        nblk = n * TOP_K // MOE_ROWS + N_EXPERTS
        blk_row0 = jnp.arange(nblk, dtype=I32) * MOE_ROWS
        idx_w, gates_w, pos_w, counts = _route(logits, tm, row0, n)
        idx, pos = idx_w[:, :TOP_K], pos_w[:, :TOP_K]
        counts = counts[0].astype(I32)
        padded = (counts + MOE_ROWS - 1) // MOE_ROWS * MOE_ROWS
        pad_end = jnp.cumsum(padded)
        pad_start = pad_end - padded
        dest = pad_start[idx] + pos
        blk_e = jnp.minimum(jnp.sum((pad_end[None, :] <= blk_row0[:, None]).astype(I32), axis=1), N_EXPERTS - 1)
        n_used = (pad_end[-1:] // MOE_ROWS).astype(I32)
        key = idx * n + jnp.arange(n, dtype=I32)[:, None]
        tok_sorted = jnp.sort(key.reshape(-1)) % n
        grp_start = jnp.cumsum(counts) - counts
        slot = blk_row0[:, None] + jnp.arange(MOE_ROWS, dtype=I32)[None, :]
        compact = jnp.minimum(slot + (grp_start - pad_start)[blk_e][:, None],
                              (grp_start + counts - 1)[blk_e][:, None])
        src = tok_sorted.at[jnp.clip(compact, 0, n * TOP_K - 1).reshape(-1)].get(mode="promise_in_bounds")
        xs = xn.at[src + row0].get(mode="promise_in_bounds")
        routed.append((xs, blk_e, n_used, dest, gates_w, row0, n))
        row0 += n
    ys = [_experts(xs, blk_e, n_used, wgu, bgu, wd, bd) for xs, blk_e, n_used, _, _, _, _ in routed]
    out = None
    for p, (_, _, _, dest, gates_w, row0, n) in enumerate(routed):
        y4 = ys[p].at[dest.T.reshape(-1)].get(mode="promise_in_bounds").reshape(TOP_K, n, d)
        out = _combine(x, y4, gates_w, gate_mod, mod_map, g_final, final_norm, tm, row0, out)
    return out


def _moe_weights(w_gate_up, b_gate_up, w_down, b_down):
    ne, d, ff2 = w_gate_up.shape
    b_tiled = b_gate_up.reshape(ne, ff2 // (2 * LANES), LANES, 2).swapaxes(-1, -2).reshape(ne, 1, ff2)
    return w_gate_up, b_tiled, w_down, b_down[:, None, :]


def _rope_tables(n_tokens):
    rows = n_tokens // GRID_W
    row = jnp.repeat(jnp.arange(rows, dtype=I32), GRID_W).astype(F32)
    col = jnp.tile(jnp.arange(GRID_W, dtype=I32), rows).astype(F32)
    inv = ROPE_BASE ** (-jnp.arange(0, ROPE_AXIS_DIM, 2, dtype=F32) / ROPE_AXIS_DIM)
    ang_r, ang_c = row[:, None] * inv, col[:, None] * inv
    cos = jnp.concatenate([jnp.cos(ang_r)] * 2 + [jnp.cos(ang_c)] * 2, axis=-1)
    sin = jnp.concatenate([jnp.sin(ang_r)] * 2 + [jnp.sin(ang_c)] * 2, axis=-1)
    return cos, sin


def _rot_perm():
    q = ROPE_AXIS_DIM // 2
    d = np.arange(HEAD_DIM)
    first = (d % ROPE_AXIS_DIM) < q
    perm = np.where(first, d + q, d - q)
    sign = np.where(first, -1.0, 1.0).astype(np.float32)
    return perm, sign


def _rot_cols(w, n_heads):
    perm, sign = _rot_perm()
    k = w.shape[0]
    wh = w.reshape(k, n_heads, HEAD_DIM)
    return (wh[:, :, perm] * sign).reshape(k, n_heads * HEAD_DIM)


def _to_heads(t, b, n_heads):
    return t.reshape(b, -1, n_heads, HEAD_DIM).transpose(0, 2, 1, 3)


def _from_heads(t):
    b, h, tt, dh = t.shape
    return t.transpose(0, 2, 1, 3).reshape(b * tt, h * dh)


def _layer_even(xa, mod, b, s, lc, g_mix, g_ffn, w_in, w_out, gate_bias, mlstm_norm, sink, router, moe_w):
    d = xa.shape[1]
    nctx = b * lc
    tm = _tile(math.gcd(nctx, s), (512, 256, 128))
    mod_map = lambda i: jnp.where(i * tm < nctx, b, (i * tm - nctx) // s)
    sh1, sc1, gt1, sh2, sc2, gt2 = [mod[:, j][:, None, :] for j in range(6)]

    hm, hs, hk = MLSTM_HEADS * HEAD_DIM, SWA_HEADS * HEAD_DIM, SWA_KV_HEADS * HEAD_DIM
    o0 = np.cumsum([0, hm, hm, hm, hm, 4 * MLSTM_HEADS, hs, hk, hk])
    seg = lambda j: w_in[:, o0[j]:o0[j + 1]]
    w_cat = jnp.concatenate([seg(0), seg(1) * ATTN_SCALE, seg(2), seg(3), _pad_cols(seg(4))], axis=1).astype(BF16)
    w_t = jnp.concatenate([seg(5), _rot_cols(seg(5), SWA_HEADS), seg(6), _rot_cols(seg(6), SWA_KV_HEADS), seg(7)],
                          axis=1).T.astype(BF16)
    widths = [hm, hm, hm, hm, LANES]
    dts = [BF16] * 4 + [F32]
    qa, ka, va, oa, gts, qt, qt_r, kt, kt_r, vt = _in_proj(
        xa, g_mix, sc1, sh1, mod_map, w_cat, widths, dts, tm,
        wt=w_t, t_widths=[hs, hs, hk, hk, hk], t_dtypes=[BF16] * 5)

    hf, hb = _mlstm(qa, ka, va, gts, _pad_cols(gate_bias.astype(F32).reshape(1, -1)), b, lc, s)
    mix_a = _mlstm_out(hf, hb, oa, mlstm_norm.reshape(-1), tm)

    cos, sin = _rope_tables(s)
    ones = jnp.ones((HEAD_DIM,), F32)
    g = SWA_HEADS // SWA_KV_HEADS
    heads = lambda t, n: _to_heads(t, b, n)
    log2e = math.log2(math.e)
    q_x = _q_prep_t(qt, qt_r, nctx, b * s, cos.T, sin.T, ones, ones, ATTN_SCALE * log2e, norm=False)
    k_x = _q_prep_t(kt, kt_r, nctx, b * s, cos.T, sin.T, ones, ones, 1.0, norm=False)
    k_c, v_c = heads(kt[:, :nctx].T, SWA_KV_HEADS), heads(vt[:, :nctx].T, SWA_KV_HEADS)
    q_c = heads(qt[:, :nctx].T, SWA_HEADS)
    sink_h = sink.astype(F32).reshape(SWA_KV_HEADS, g, 1)
    att_x = _swa(q_x, k_x, (kt, vt), jnp.broadcast_to(sink_h * log2e, (SWA_KV_HEADS, g, 2 * SWA_BLOCK)), b, s, lc)
    att_c = _ctx_attn(q_c, k_c, v_c, jnp.broadcast_to(sink_h[..., None], (SWA_KV_HEADS, g, lc, 1))
                      .reshape(SWA_KV_HEADS, g * lc, 1), True)
    mix_b = jnp.concatenate([_from_heads(att_c), att_x], axis=0)

    w_router, b_router = router
    xa, xn2, logits = _out_proj(xa, 0, mix_a, mix_b, w_out[:hm].astype(BF16), w_out[hm:].astype(BF16), gt1, g_ffn,
                                sc2, sh2, mod_map, w_router.astype(F32), b_router.astype(F32), tm)
    return _moe(xa, xn2, logits, gt2, mod_map, moe_w, g_ffn, False, tm)


def _layer_odd_last(xa, mod, b, s, lc, g_mix, g_ffn, w_in, w_out, s5_params, d_skip, w_glu, b_glu,
                    q_norm, k_norm, router, moe_w, g_final):
    nctx = b * lc
    tm = _tile(math.gcd(nctx, s), (512, 256, 128))
    mod_map = lambda i: jnp.where(i * tm < nctx, b, (i * tm - nctx) // s)
    lat_map = lambda i: i * tm // s
    sh1, sc1, gt1, sh2, sc2, gt2 = [mod[:, j][:, None, :] for j in range(6)]

    hq, hk = ATT_HEADS * HEAD_DIM, ATT_KV_HEADS * HEAD_DIM
    o1 = np.cumsum([0, S5_CHANNELS, hq, hk, hk])
    seg = lambda j: w_in[:, o1[j]:o1[j + 1]]
    kpad = lambda w: _pad_cols(w, 2 * LANES)
    w_cat = jnp.concatenate([seg(0), kpad(seg(2)), kpad(_rot_cols(seg(2), ATT_KV_HEADS)), kpad(seg(3))],
                            axis=1).astype(BF16)
    w_q = jnp.concatenate([seg(1), _rot_cols(seg(1), ATT_HEADS)], axis=1).T.astype(BF16)
    widths = [S5_CHANNELS, 2 * LANES, 2 * LANES, 2 * LANES]
    dts = [F32, BF16, BF16, BF16]
    u, k, k_r, v, qt, qt_r = _in_proj(xa, g_mix, sc1, sh1, mod_map, w_cat, widths, dts, tm,
                                      wt=w_q, t_widths=[hq, hq], t_dtypes=[BF16, BF16])

    mix_a = _s5_mixer(u[nctx:].reshape(b, s, S5_CHANNELS), u[:nctx].reshape(b, lc, S5_CHANNELS),
                      _s5_tables(*s5_params), d_skip, w_glu, b_glu)

    cos, sin = _rope_tables(s)
    perm, _ = _rot_perm()
    qn, kn = q_norm.astype(F32), k_norm.astype(F32)
    heads = lambda t, n: _to_heads(t[:, :n * HEAD_DIM], b, n)
    q_x = _q_prep_t(qt, qt_r, nctx, b * s, cos.T, sin.T, qn, qn[perm], ATTN_SCALE * math.log2(math.e))
    k_x = _qk_prep(heads(k[nctx:], ATT_KV_HEADS), heads(k_r[nctx:], ATT_KV_HEADS), cos, sin, kn, kn[perm],
                   norm=True, rope=True, scale=1.0)
    kc_raw = heads(k[:nctx], ATT_KV_HEADS)
    k_c = _qk_prep(kc_raw, kc_raw, cos[:lc], sin[:lc], kn, kn, norm=True, rope=False, scale=1.0)
    k_all = jnp.concatenate([k_c, k_x], axis=2)
    v_all = jnp.concatenate([heads(v[:nctx], ATT_KV_HEADS), heads(v[nctx:], ATT_KV_HEADS)], axis=2)
    mix_b = _flash(q_x, k_all, v_all.swapaxes(-1, -2), b, s)

    w_router, b_router = router
    hs = S5_CHANNELS
    x, xn2, logits = _out_proj(xa, nctx, mix_a, mix_b, w_out[:hs].astype(BF16), w_out[hs:].astype(BF16), gt1, g_ffn,
                               sc2, sh2, lat_map, w_router.astype(F32), b_router.astype(F32), tm)
    return _moe(x, xn2, logits, gt2, lat_map, moe_w, g_final, True, tm)


def kernel(x, c, ctx, c_ctx, l0_w_mod, l0_b_mod, l0_g_mix, l0_g_ffn, l0_w_in, l0_w_out, l0_gate_bias, l0_mlstm_norm, l0_sink, l0_w_router, l0_b_router, l0_w_gate_up, l0_b_gate_up, l0_w_down, l0_b_down, l1_w_mod, l1_b_mod, l1_g_mix, l1_g_ffn, l1_w_in, l1_w_out, l1_lam_re, l1_lam_im, l1_log_dt, l1_b_re, l1_b_im, l1_c_re, l1_c_im, l1_d_skip, l1_w_glu, l1_b_glu, l1_q_norm, l1_k_norm, l1_w_router, l1_b_router, l1_w_gate_up, l1_b_gate_up, l1_w_down, l1_b_down, g_final):
    b, s, d = x.shape
    lc = ctx.shape[1]
    cond = jnp.concatenate([c, c_ctx[None, :]], axis=0)
    cond = jnp.pad(cond, ((0, (-(b + 1)) % 8), (0, 0)))
    mod0 = _silu_linear(cond, l0_w_mod, l0_b_mod)[:b + 1].reshape(b + 1, 6, d)
    mod1 = _silu_linear(cond, l1_w_mod, l1_b_mod)[:b + 1].reshape(b + 1, 6, d)

    moe0 = _moe_weights(l0_w_gate_up, l0_b_gate_up, l0_w_down, l0_b_down)
    moe1 = _moe_weights(l1_w_gate_up, l1_b_gate_up, l1_w_down, l1_b_down)
    xa = jnp.concatenate([ctx.reshape(b * lc, d), x.reshape(b * s, d)], axis=0)
    xa = _layer_even(xa, mod0, b, s, lc, l0_g_mix, l0_g_ffn, l0_w_in, l0_w_out, l0_gate_bias, l0_mlstm_norm,
                     l0_sink, (l0_w_router, l0_b_router), moe0)
    out = _layer_odd_last(xa, mod1, b, s, lc, l1_g_mix, l1_g_ffn, l1_w_in, l1_w_out,
                          (l1_lam_re, l1_lam_im, l1_log_dt, l1_b_re, l1_b_im, l1_c_re, l1_c_im),
                          l1_d_skip, l1_w_glu, l1_b_glu, l1_q_norm, l1_k_norm, (l1_w_router, l1_b_router),
                          moe1, g_final)
    return out.reshape(b, s, d)
```

```python
import functools
import math

import jax
import jax.numpy as jnp
import numpy as np
from jax import lax
from jax.experimental import pallas as pl
from jax.experimental.pallas import tpu as pltpu

F32 = jnp.float32
BF16 = jnp.bfloat16
I32 = jnp.int32

GRID_W = 64
HEAD_DIM = 64
ATTN_SCALE = HEAD_DIM ** -0.5
ROPE_AXIS_DIM = HEAD_DIM // 2
ROPE_BASE = 10000.0
EPS = 1e-6
NEG_INF = -1e30

MLSTM_HEADS = 8
MLSTM_CHUNKS = (256, 128, 64)
SWA_HEADS = 8
SWA_KV_HEADS = 2
SWA_WINDOW = 128
SWA_BLOCK = 128
S5_CHANNELS = 256
S5_GROUP = 16
S5_GROUPS = S5_CHANNELS // S5_GROUP
S5_STATE = 64
S5_CHUNK = 64
ATT_HEADS = 12
ATT_KV_HEADS = 3
N_EXPERTS = 32
TOP_K = 4
SWIGLU_LIMIT = 7.0
SWIGLU_ALPHA = 1.702

LANES = 128
VMEM_LIMIT = 56 * 1024 * 1024
MOE_ROWS = 512
MOE_SPLIT = (1, 1)
HIGHEST = lax.Precision.HIGHEST


def _params(*sem):
    return pltpu.CompilerParams(dimension_semantics=sem, vmem_limit_bytes=VMEM_LIMIT)


def _tile(n, prefs):
    for t in prefs:
        if n % t == 0:
            return t
    return n


def _pad_cols(w, mult=LANES):
    pad = (-w.shape[-1]) % mult
    if pad:
        w = jnp.pad(w, [(0, 0)] * (w.ndim - 1) + [(0, pad)])
    return w


def _linear_kernel(x_ref, w_ref, b_ref, o_ref):
    x = x_ref[...]
    x = x * jax.nn.sigmoid(x)
    o_ref[...] = jnp.dot(x, w_ref[...], precision=HIGHEST, preferred_element_type=F32) + b_ref[...]


def _silu_linear(x, w, b):
    m, k = x.shape
    n = w.shape[1]
    tn = _tile(n, (1024, 512, 256, 128))
    return pl.pallas_call(
        _linear_kernel,
        out_shape=jax.ShapeDtypeStruct((m, n), F32),
        grid=(n // tn,),
        in_specs=[pl.BlockSpec((m, k), lambda j: (0, 0)),
                  pl.BlockSpec((k, tn), lambda j: (0, j)),
                  pl.BlockSpec((1, tn), lambda j: (0, j))],
        out_specs=pl.BlockSpec((m, tn), lambda j: (0, j)),
        compiler_params=_params("arbitrary"),
        name="adaln_linear",
    )(x, w, b.reshape(1, n))


def _in_proj_kernel(x_ref, g_ref, sc_ref, sh_ref, w_ref, wt_ref, *out_refs, widths, t_widths):
    x = x_ref[...]
    xn = x * lax.rsqrt(jnp.mean(x * x, axis=-1, keepdims=True) + EPS) * g_ref[...]
    xb = (xn * (1.0 + sc_ref[0]) + sh_ref[0]).astype(BF16)
    off = 0
    for o_ref, w in zip(out_refs, widths):
        o_ref[...] = jnp.dot(xb, w_ref[:, off:off + w], preferred_element_type=F32).astype(o_ref.dtype)
        off += w
    off = 0
    for o_ref, w in zip(out_refs[len(widths):], t_widths):
        o_ref[...] = lax.dot_general(wt_ref[off:off + w, :], xb, (((1,), (1,)), ((), ())),
                                     preferred_element_type=F32).astype(o_ref.dtype)
        off += w


def _in_proj(x, g, scale, shift, mod_map, w, widths, dtypes, tm, wt=None, t_widths=(), t_dtypes=()):
    r, d = x.shape
    if wt is None:
        wt = jnp.zeros((8, d), BF16)
    return pl.pallas_call(
        functools.partial(_in_proj_kernel, widths=tuple(widths), t_widths=tuple(t_widths)),
        out_shape=([jax.ShapeDtypeStruct((r, wd), dt) for wd, dt in zip(widths, dtypes)]
                   + [jax.ShapeDtypeStruct((wd, r), dt) for wd, dt in zip(t_widths, t_dtypes)]),
        grid=(r // tm,),
        in_specs=[pl.BlockSpec((tm, d), lambda i: (i, 0)),
                  pl.BlockSpec((1, d), lambda i: (0, 0)),
                  pl.BlockSpec((1, 1, d), lambda i: (mod_map(i), 0, 0)),
                  pl.BlockSpec((1, 1, d), lambda i: (mod_map(i), 0, 0)),
                  pl.BlockSpec(w.shape, lambda i: (0, 0)),
                  pl.BlockSpec(wt.shape, lambda i: (0, 0))],
        out_specs=([pl.BlockSpec((tm, wd), lambda i: (i, 0)) for wd in widths]
                   + [pl.BlockSpec((wd, tm), lambda i: (0, i)) for wd in t_widths]),
        compiler_params=_params("parallel"),
        name="in_proj",
    )(x, g.reshape(1, d), scale, shift, w, wt)


def _qk_prep_kernel(x_ref, xr_ref, cos_ref, sin_ref, g_ref, gr_ref, o_ref, *, norm, rope, scale):
    x = x_ref[0].astype(F32)
    if norm:
        s = lax.rsqrt(jnp.mean(x * x, axis=-1, keepdims=True) + EPS)
        x = x * s * g_ref[...]
    if rope:
        xr = xr_ref[0].astype(F32)
        if norm:
            xr = xr * s * gr_ref[...]
        x = x * cos_ref[...] + xr * sin_ref[...]
    o_ref[0] = (x * scale).astype(o_ref.dtype)


def _qk_prep(x, x_rot, cos, sin, gain, gain_rot, *, norm, rope, scale):
    b, h, t, dh = x.shape
    tt = _tile(t, (512, 256, 128))
    blk = pl.BlockSpec((1, h, tt, dh), lambda i, j: (i, 0, j, 0))
    tab = pl.BlockSpec((tt, dh), lambda i, j: (j, 0))
    vec = pl.BlockSpec((1, dh), lambda i, j: (0, 0))
    return pl.pallas_call(
        functools.partial(_qk_prep_kernel, norm=norm, rope=rope, scale=scale),
        out_shape=jax.ShapeDtypeStruct(x.shape, BF16),
        grid=(b, t // tt),
        in_specs=[blk, blk, tab, tab, vec, vec],
        out_specs=blk,
        compiler_params=_params("parallel", "parallel"),
        name="qk_prep",
    )(x, x_rot, cos, sin, gain.reshape(1, dh), gain_rot.reshape(1, dh))


def _q_prep_t_kernel(x_ref, xr_ref, cos_ref, sin_ref, g_ref, gr_ref, o_ref, *, norm, scale):
    w, tt = x_ref.shape
    nh = w // HEAD_DIM
    x = x_ref[...].astype(F32).reshape(nh, HEAD_DIM, tt)
    xr = xr_ref[...].astype(F32).reshape(nh, HEAD_DIM, tt)
    if norm:
        s = lax.rsqrt(jnp.mean(x * x, axis=1, keepdims=True) + EPS)
        x, xr = x * s * g_ref[...], xr * s * gr_ref[...]
    y = x * cos_ref[...] + xr * sin_ref[...]
    o_ref[...] = (y * scale).reshape(w, tt).astype(o_ref.dtype)


def _q_prep_t(xt, xt_rot, col0, n_cols, cos_t, sin_t, gain, gain_rot, scale, norm=True):
    w = xt.shape[0]
    s = cos_t.shape[1]
    tt = _tile(math.gcd(col0, s), (512, 256, 128))
    blk = pl.BlockSpec((w, tt), lambda j: (0, j + col0 // tt))
    tab = pl.BlockSpec((HEAD_DIM, tt), lambda j: (0, j % (s // tt)))
    vec = pl.BlockSpec((HEAD_DIM, 1), lambda j: (0, 0))
    return pl.pallas_call(
        functools.partial(_q_prep_t_kernel, norm=norm, scale=scale),
        out_shape=jax.ShapeDtypeStruct((w, n_cols), BF16),
        grid=(n_cols // tt,),
        in_specs=[blk, blk, tab, tab, vec, vec],
        out_specs=pl.BlockSpec((w, tt), lambda j: (0, j)),
        compiler_params=_params("parallel"),
        name="q_prep_t",
    )(xt, xt_rot, cos_t, sin_t, gain.reshape(HEAD_DIM, 1), gain_rot.reshape(HEAD_DIM, 1))


def _log_sigmoid(x):
    return jnp.minimum(x, 0.0) - jnp.log(1.0 + jnp.exp(-jnp.abs(x)))


def _mlstm_kernel(qf, kf, vf, gf, qb, kb, vb, gb, bias_ref, hf_ref, hb_ref, c_scr, n_scr, m_scr):
    ch, nh, dh = qf.shape[0], MLSTM_HEADS, HEAD_DIM

    @pl.when(pl.program_id(1) == 0)
    def _():
        c_scr[...] = jnp.zeros_like(c_scr)
        n_scr[...] = jnp.zeros_like(n_scr)
        m_scr[...] = jnp.full_like(m_scr, NEG_INF)

    row = lax.broadcasted_iota(I32, (ch, ch), 0)
    col = lax.broadcasted_iota(I32, (ch, ch), 1)
    lane_lo = lax.broadcasted_iota(I32, (ch, 2 * dh), 1) < dh
    row_lo = lax.broadcasted_iota(I32, (2 * dh, 2 * dh), 0) < dh
    col_lo = lax.broadcasted_iota(I32, (2 * dh, 2 * dh), 1) < dh
    vec_lo = lax.broadcasted_iota(I32, (1, 2 * dh), 1) < dh
    pair = lambda a, b: jnp.where(lane_lo, a, b)
    refs = ((qf, kf, vf, gf, hf_ref), (qb, kb, vb, gb, hb_ref))
    jobs = [(d, p) for d in range(2) for p in range(nh // 2)]
    sl = lambda p: slice(2 * dh * p, 2 * dh * (p + 1))

    qk = {}
    for d, p in jobs:
        q2, k2 = refs[d][0][:, sl(p)], refs[d][1][:, sl(p)]
        zero = jnp.zeros_like(q2)
        q_st = jnp.concatenate([jnp.where(lane_lo, q2, zero), jnp.where(lane_lo, zero, q2)], axis=0)
        qk[d, p] = lax.dot_general(q_st, k2, (((1,), (1,)), ((), ())), preferred_element_type=F32)

    st = {}
    for d in range(2):
        seen = (col <= row) if d == 0 else (col >= row)
        tri = seen.astype(F32)
        tri_t = ((row <= col) if d == 0 else (row >= col)).astype(F32)
        last = ch - 1 if d == 0 else 0
        g = refs[d][3][...] + bias_ref[...]
        g_t = g.T
        lo = 2 * nh * d
        log2e = math.log2(math.e)
        li_col, lf_col = g[:, lo:lo + nh] * log2e, _log_sigmoid(g[:, lo + nh:lo + 2 * nh]) * log2e
        li_row, lf_row = g_t[lo:lo + nh, :] * log2e, _log_sigmoid(g_t[lo + nh:lo + 2 * nh, :]) * log2e
        b_col = jnp.dot(tri, lf_col, precision=HIGHEST, preferred_element_type=F32)
        b_row = jnp.dot(lf_row, tri_t, precision=HIGHEST, preferred_element_type=F32)
        src_row = li_row - b_row
        for h in range(nh):
            bc, br = b_col[:, h:h + 1], b_row[h:h + 1, :]
            lic = li_col[:, h:h + 1]
            b_last = br[:, last:last + 1]
            m_prev = m_scr[d, h // 2][:, (h % 2) * dh:(h % 2) * dh + 1]
            d_log = jnp.where(seen, bc + src_row[h:h + 1, :], NEG_INF)
            inter_log = bc + m_prev
            m_t = jnp.maximum(inter_log, jnp.max(d_log, axis=1, keepdims=True))
            w_log = b_last - bc + lic
            m_new = jnp.maximum(b_last + m_prev, jnp.max(w_log, axis=0, keepdims=True))
            st[d, h] = dict(dmat=jnp.exp2(d_log - m_t), inter=jnp.exp2(inter_log - m_t), floor=jnp.exp2(-m_t),
                            m_new=m_new, decay=jnp.exp2(b_last + m_prev - m_new), wn=jnp.exp2(w_log - m_new))

    mm = {}
    for d, p in jobs:
        a, b = st[d, 2 * p], st[d, 2 * p + 1]
        q2, k2, v2 = refs[d][0][:, sl(p)], refs[d][1][:, sl(p)], refs[d][2][:, sl(p)]
        s = qk[d, p] * jnp.concatenate([a["dmat"], b["dmat"]], axis=0)
        kw = k2.astype(F32) * pair(a["wn"], b["wn"])
        mm[d, p] = dict(
            s_sum=jnp.sum(s, axis=1, keepdims=True), kw_sum=jnp.sum(kw, axis=0, keepdims=True),
            sv=jnp.dot(s.astype(BF16), v2, preferred_element_type=F32),
            q_c=jnp.dot(q2, c_scr[d, p].astype(BF16), preferred_element_type=F32),
            kv=lax.dot_general(kw.astype(BF16), v2, (((0,), (0,)), ((), ())), preferred_element_type=F32))

    for d, p in jobs:
        a, b, r = st[d, 2 * p], st[d, 2 * p + 1], mm[d, p]
        q2 = refs[d][0][:, sl(p)]
        n_prev = n_scr[d, p]
        num = pair(a["inter"], b["inter"]) * r["q_c"] + jnp.where(lane_lo, r["sv"][:ch], r["sv"][ch:])
        qn = q2.astype(F32) * n_prev
        qn_a = jnp.sum(jnp.where(lane_lo, qn, 0.0), axis=1, keepdims=True)
        qn_b = jnp.sum(jnp.where(lane_lo, 0.0, qn), axis=1, keepdims=True)
        den_a = jnp.maximum(jnp.abs(a["inter"] * qn_a + r["s_sum"][:ch]), a["floor"])
        den_b = jnp.maximum(jnp.abs(b["inter"] * qn_b + r["s_sum"][ch:]), b["floor"])
        refs[d][4][:, sl(p)] = num / pair(den_a, den_b)
        c_scr[d, p] = (jnp.where(row_lo, a["decay"], b["decay"]) * c_scr[d, p]
                       + jnp.where(row_lo == col_lo, r["kv"], 0.0))
        n_scr[d, p] = jnp.where(vec_lo, a["decay"], b["decay"]) * n_prev + r["kw_sum"]
        m_scr[d, p] = jnp.where(vec_lo, a["m_new"], b["m_new"])


def _mlstm(q, k, v, gates, bias, b, lc, s):
    r, w = q.shape
    ch = _tile(math.gcd(lc, s), MLSTM_CHUNKS)
    ncc, ncx = lc // ch, s // ch
    base = b * ncc

    def fwd(i, c):
        return jnp.where(c < ncc, i * ncc + c, base + i * ncx + (c - ncc)), 0

    def bwd(i, c):
        return jnp.where(c < ncc, i * ncc + (ncc - 1 - c), base + i * ncx + (ncx - 1 - (c - ncc))), 0

    spec = lambda width, m: pl.BlockSpec((ch, width), m)
    npair = MLSTM_HEADS // 2
    return pl.pallas_call(
        _mlstm_kernel,
        out_shape=[jax.ShapeDtypeStruct((r, w), F32)] * 2,
        grid=(b, ncc + ncx),
        in_specs=[spec(w, fwd), spec(w, fwd), spec(w, fwd), spec(LANES, fwd),
                  spec(w, bwd), spec(w, bwd), spec(w, bwd), spec(LANES, bwd),
                  pl.BlockSpec((1, LANES), lambda i, c: (0, 0))],
        out_specs=[spec(w, fwd), spec(w, bwd)],
        scratch_shapes=[pltpu.VMEM((2, npair, 2 * HEAD_DIM, 2 * HEAD_DIM), F32),
                        pltpu.VMEM((2, npair, 1, 2 * HEAD_DIM), F32),
                        pltpu.VMEM((2, npair, 1, 2 * HEAD_DIM), F32)],
        compiler_params=_params("parallel", "arbitrary"),
        name="mlstm_scan",
    )(q, k, v, gates, q, k, v, gates, bias)


def _mlstm_out_kernel(hf_ref, hb_ref, o_ref, nrm_ref, ones_ref, y_ref):
    h = hf_ref[...] + hb_ref[...]
    ms = jnp.dot((h * h).astype(BF16), ones_ref[...], preferred_element_type=F32) * (1.0 / HEAD_DIM)
    hn = h * lax.rsqrt(ms + EPS) * nrm_ref[...]
    y_ref[...] = (jax.nn.sigmoid(o_ref[...].astype(F32)) * hn).astype(y_ref.dtype)


def _head_ones(width):
    idx = np.arange(width) // HEAD_DIM
    return jnp.asarray(idx[:, None] == idx[None, :], BF16)


def _mlstm_out(hf, hb, o, norm, tm):
    r, w = hf.shape
    blk = pl.BlockSpec((tm, w), lambda i: (i, 0))
    return pl.pallas_call(
        _mlstm_out_kernel,
        out_shape=jax.ShapeDtypeStruct((r, w), BF16),
        grid=(r // tm,),
        in_specs=[blk, blk, blk,
                  pl.BlockSpec((1, w), lambda i: (0, 0)),
                  pl.BlockSpec((w, w), lambda i: (0, 0))],
        out_specs=blk,
        compiler_params=_params("parallel"),
        name="mlstm_out",
    )(hf, hb, o, norm.reshape(1, w), _head_ones(w))


def _swa_kernel(q_ref, k0, k1, k2, k3, v0, v1, v2, v3, kc_ref, vc_ref, sink_ref, o_ref, *, seq):
    w, dh = SWA_BLOCK, HEAD_DIM
    tq = q_ref.shape[1]
    g = q_ref.shape[0] // dh
    i = pl.program_id(2)
    k_loc = jnp.concatenate([k0[...], k1[...], k2[...], k3[...]], axis=1)
    v_loc = jnp.concatenate([v0[...], v1[...], v2[...], v3[...]], axis=1)
    kpos = (2 * i - 1) * w + lax.broadcasted_iota(I32, (4 * w, tq), 0)
    qpos = i * tq + lax.broadcasted_iota(I32, (4 * w, tq), 1)
    valid = (jnp.abs(qpos - kpos) <= SWA_WINDOW) & (kpos >= 0) & (kpos < seq)
    t_dot = lambda kt, q: lax.dot_general(kt, q, (((0,), (0,)), ((), ())), preferred_element_type=F32)
    qs = [q_ref[h * dh:(h + 1) * dh, :] for h in range(g)]
    s_loc = [jnp.where(valid, t_dot(k_loc, qs[h]), NEG_INF) for h in range(g)]
    s_ctx = [t_dot(kc_ref[...], qs[h]) for h in range(g)]
    outs = []
    for h in range(g):
        sink = sink_ref[0, h:h + 1, :]
        m = jnp.maximum(jnp.maximum(jnp.max(s_loc[h], axis=0, keepdims=True),
                                    jnp.max(s_ctx[h], axis=0, keepdims=True)), sink)
        p_loc = jnp.exp2(s_loc[h] - m)
        p_ctx = jnp.exp2(s_ctx[h] - m)
        den = jnp.sum(p_loc, axis=0, keepdims=True) + jnp.sum(p_ctx, axis=0, keepdims=True) + jnp.exp2(sink - m)
        o = (jnp.dot(v_loc, p_loc.astype(BF16), preferred_element_type=F32)
             + jnp.dot(vc_ref[...], p_ctx.astype(BF16), preferred_element_type=F32))
        outs.append(o / den)
    o_ref[...] = jnp.concatenate(outs, axis=0).T.astype(o_ref.dtype)


def _swa(qt, kt, kvt, sink2, b, s, lc):
    dh = HEAD_DIM
    nh, hkv = qt.shape[0] // dh, kt.shape[0] // dh
    g = nh // hkv
    w = SWA_BLOCK
    tq = 2 * w
    nb, nq = s // w, s // tq
    c0 = b * lc // w
    clampi = lambda j: jnp.clip(j, 0, nb - 1)
    kspec = lambda o: pl.BlockSpec((dh, w), lambda bi, hi, i: (hi, bi * nb + clampi(2 * i + o)))
    vspec = lambda o: pl.BlockSpec((dh, w), lambda bi, hi, i: (hi, c0 + bi * nb + clampi(2 * i + o)))
    cspec = pl.BlockSpec((dh, lc), lambda bi, hi, i: (hi, bi))
    k_all, v_all = kvt
    return pl.pallas_call(
        functools.partial(_swa_kernel, seq=s),
        out_shape=jax.ShapeDtypeStruct((b * s, nh * dh), BF16),
        grid=(b, hkv, nq),
        in_specs=[pl.BlockSpec((g * dh, tq), lambda bi, hi, i: (hi, bi * nq + i)),
                  kspec(-1), kspec(0), kspec(1), kspec(2), vspec(-1), vspec(0), vspec(1), vspec(2),
                  cspec, cspec, pl.BlockSpec((1, g, tq), lambda bi, hi, i: (hi, 0, 0))],
        out_specs=pl.BlockSpec((tq, g * dh), lambda bi, hi, i: (bi * nq + i, hi)),
        compiler_params=_params("parallel", "parallel", "arbitrary"),
        name="window_attention",
    )(qt, kt, kt, kt, kt, v_all, v_all, v_all, v_all, k_all, v_all, sink2)


def _ctx_attn_kernel(q_ref, k_ref, v_ref, sink_ref, o_ref, *, use_sink):
    g, lq, dh = q_ref.shape[1:]
    q = q_ref[0].reshape(g * lq, dh)
    s = lax.dot_general(q, k_ref[0, 0], (((1,), (1,)), ((), ())), preferred_element_type=F32) * ATTN_SCALE
    m = jnp.max(s, axis=1, keepdims=True)
    if use_sink:
        m = jnp.maximum(m, sink_ref[0])
    p = jnp.exp(s - m)
    den = jnp.sum(p, axis=1, keepdims=True)
    if use_sink:
        den = den + jnp.exp(sink_ref[0] - m)
    o = jnp.dot(p.astype(BF16), v_ref[0, 0], preferred_element_type=F32) / den
    o_ref[0] = o.reshape(g, lq, dh).astype(o_ref.dtype)


def _ctx_attn(q, k, v, sink_col, use_sink):
    b, nh, lq, dh = q.shape
    hkv = k.shape[1]
    g = nh // hkv
    qblk = pl.BlockSpec((1, g, lq, dh), lambda bi, hi: (bi, hi, 0, 0))
    kblk = pl.BlockSpec((1, 1, k.shape[2], dh), lambda bi, hi: (bi, hi, 0, 0))
    return pl.pallas_call(
        functools.partial(_ctx_attn_kernel, use_sink=use_sink),
        out_shape=jax.ShapeDtypeStruct(q.shape, BF16),
        grid=(b, hkv),
        in_specs=[qblk, kblk, kblk, pl.BlockSpec((1, g * lq, 1), lambda bi, hi: (hi, 0, 0))],
        out_specs=qblk,
        compiler_params=_params("parallel", "parallel"),
        name="context_attention",
    )(q, k, v, sink_col)


def _flash_kernel(q_ref, k_ref, v_ref, o_ref, m_scr, acc_scr, sa_scr, sb_scr, *, tk):
    dh = HEAD_DIM
    g = q_ref.shape[0] // dh
    n = k_ref.shape[2] // tk
    m_scr[...] = jnp.full_like(m_scr, NEG_INF)
    acc_scr[...] = jnp.zeros_like(acc_scr)

    def scores(j, dst):
        kk = k_ref[0, 0, pl.ds(pl.multiple_of(j * tk, tk), tk), :]
        for h in range(g):
            dst[h] = jnp.dot(kk, q_ref[h * dh:(h + 1) * dh, :], preferred_element_type=F32)

    def update(j, src):
        vv = v_ref[0, 0, :, pl.ds(pl.multiple_of(j * tk, tk), tk)]
        ss = [src[h] for h in range(g)]
        m_olds = [m_scr[h] for h in range(g)]
        m_news = [jnp.maximum(m_olds[h], jnp.max(ss[h], axis=0, keepdims=True)) for h in range(g)]
        ps = [jnp.exp2(ss[h] - m_news[h]).astype(BF16) for h in range(g)]
        pvs = [jnp.dot(vv, ps[h], preferred_element_type=F32) for h in range(g)]
        for h in range(g):
            acc_scr[h] = jnp.exp2(m_olds[h] - m_news[h]) * acc_scr[h] + pvs[h]
            m_scr[h] = m_news[h]

    scores(0, sa_scr)

    def body(i, carry):
        scores(2 * i + 1, sb_scr)
        update(2 * i, sa_scr)
        scores(2 * i + 2, sa_scr)
        update(2 * i + 1, sb_scr)
        return carry

    lax.fori_loop(0, (n - 1) // 2, body, 0)
    if n % 2 == 1:
        update(n - 1, sa_scr)
    else:
        scores(n - 1, sb_scr)
        update(n - 2, sa_scr)
        update(n - 1, sb_scr)
    acc = acc_scr[...]
    out = (acc[:, :dh] / acc[:, dh:dh + 1]).reshape(g * dh, -1)
    o_ref[...] = out.T.astype(o_ref.dtype)


def _flash(qt, k, vt, b, s, tq=None, tk=None):
    w = qt.shape[0]
    hkv, t, dh = k.shape[1], k.shape[2], k.shape[3]
    g = w // dh // hkv
    tq = tq or _tile(s, (256, 128))
    tk = tk or _tile(t, (256, 128))
    nq = s // tq
    ones_rows = 8
    vt = jnp.concatenate([vt, jnp.ones((b, hkv, ones_rows, t), vt.dtype)], axis=2)
    nv = dh + ones_rows
    return pl.pallas_call(
        functools.partial(_flash_kernel, tk=tk),
        out_shape=jax.ShapeDtypeStruct((b * s, w), BF16),
        grid=(b, hkv, nq),
        in_specs=[pl.BlockSpec((g * dh, tq), lambda bi, hi, i: (hi, bi * nq + i)),
                  pl.BlockSpec((1, 1, t, dh), lambda bi, hi, i: (bi, hi, 0, 0)),
                  pl.BlockSpec((1, 1, nv, t), lambda bi, hi, i: (bi, hi, 0, 0))],
        out_specs=pl.BlockSpec((tq, g * dh), lambda bi, hi, i: (bi * nq + i, hi)),
        scratch_shapes=[pltpu.VMEM((g, 1, tq), F32), pltpu.VMEM((g, nv, tq), F32),
                        pltpu.VMEM((g, tk, tq), F32), pltpu.VMEM((g, tk, tq), F32)],
        compiler_params=_params("parallel", "parallel", "arbitrary"),
        name="dense_attention",
    )(qt, k, vt)


def _bmm_kernel(a_ref, b_ref, o_ref):
    o_ref[0] = jnp.dot(a_ref[0], b_ref[0], preferred_element_type=F32).astype(o_ref.dtype)


def _group_matmul(a, bmat, name):
    g, m, k = a.shape
    n = bmat.shape[2]
    tm = _tile(m, (1024, 512, 256, 128, 64, 32, 16, 8))
    return pl.pallas_call(
        _bmm_kernel,
        out_shape=jax.ShapeDtypeStruct((g, m, n), F32),
        grid=(g, m // tm),
        in_specs=[pl.BlockSpec((1, tm, k), lambda gi, i: (gi, i, 0)),
                  pl.BlockSpec((1, k, n), lambda gi, i: (gi, 0, 0))],
        out_specs=pl.BlockSpec((1, tm, n), lambda gi, i: (gi, i, 0)),
        compiler_params=_params("parallel", "parallel"),
        name=name,
    )(a, bmat)


def _s5_scan_kernel(sre_ref, sim_ref, are_ref, aim_ref, zre_ref, zim_ref):
    nd, nsteps = sre_ref.shape[0], sre_ref.shape[1]
    for d in range(nd):
        a_re, a_im = are_ref[d], aim_ref[d]

        def body(i, carry, d=d, a_re=a_re, a_im=a_im):
            z_re, z_im = carry
            zre_ref[d, i] = z_re
            zim_ref[d, i] = z_im
            return (a_re * z_re - a_im * z_im + sre_ref[d, i], a_re * z_im + a_im * z_re + sim_ref[d, i])

        zero = jnp.zeros(sre_ref.shape[2:], F32)
        lax.fori_loop(0, nsteps, body, (zero, zero))


def _s5_scan(s_re, s_im, a_re, a_im):
    full = lambda arr: pl.BlockSpec(arr.shape, lambda i: (0,) * arr.ndim)
    return pl.pallas_call(
        _s5_scan_kernel,
        out_shape=[jax.ShapeDtypeStruct(s_re.shape, F32)] * 2,
        grid=(1,),
        in_specs=[full(s_re), full(s_im), full(a_re), full(a_im)],
        out_specs=[full(s_re), full(s_im)],
        compiler_params=_params("arbitrary"),
        name="s5_chunk_scan",
    )(s_re, s_im, a_re, a_im)


def _s5_glu_kernel(y_ref, u_ref, d_ref, w_ref, b_ref, o_ref):
    y = y_ref[...] + d_ref[...] * u_ref[...]
    y = jax.nn.gelu(y)
    gate = jnp.dot(y.astype(BF16), w_ref[...], preferred_element_type=F32) + b_ref[...]
    o_ref[...] = (y * jax.nn.sigmoid(gate)).astype(o_ref.dtype)


def _s5_glu(y, u, d_skip, w_glu, b_glu, tm):
    r, c = y.shape
    blk = pl.BlockSpec((tm, c), lambda i: (i, 0))
    vec = pl.BlockSpec((1, c), lambda i: (0, 0))
    return pl.pallas_call(
        _s5_glu_kernel,
        out_shape=jax.ShapeDtypeStruct((r, c), BF16),
        grid=(r // tm,),
        in_specs=[blk, blk, vec, pl.BlockSpec((c, c), lambda i: (0, 0)), vec],
        out_specs=blk,
        compiler_params=_params("parallel"),
        name="s5_readout_glu",
    )(y, u, d_skip.reshape(1, c), w_glu.astype(BF16), b_glu.reshape(1, c))


def _s5_tables(lam_re, lam_im, log_dt, b_re, b_im, c_re, c_im):
    ln, p, gc = S5_CHUNK, S5_STATE, S5_GROUP
    lam = lax.complex(lam_re.astype(F32), lam_im.astype(F32))
    dt = jnp.exp(log_dt.astype(F32))[..., None]
    a_bar = jnp.exp(lam * dt)
    b_scale = (a_bar - 1.0) / lam
    b_mat = lax.complex(b_re.astype(F32), b_im.astype(F32))
    c_mat = lax.complex(c_re.astype(F32), c_im.astype(F32))
    tau = jnp.arange(ln + 1, dtype=F32)
    apow = jnp.exp((lam * dt)[:, :, None, :] * tau[None, None, :, None])
    drive = b_scale[..., None] * b_mat[None]
    kern = jnp.real(jnp.einsum('gcp,dgtp,dgpe->dgtce', c_mat, apow[:, :, :ln], drive))
    kc = jnp.concatenate([kern[1, :, :0:-1], (kern[0, :, :1] + kern[1, :, :1]), kern[0, :, 1:]], axis=1)
    kc_e = kc.transpose(0, 3, 1, 2).reshape(S5_GROUPS, gc, (2 * ln - 1) * gc)
    toep = jnp.stack([kc_e[:, :, (ln - 1 - s) * gc:(2 * ln - 1 - s) * gc] for s in range(ln)], axis=1)
    toep = toep.reshape(S5_GROUPS, ln * gc, ln * gc)
    w_f = apow[0, :, ln - 1::-1][:, :ln, :, None] * drive[0][:, None]
    w_b = apow[1, :, :ln, :, None] * drive[1][:, None]
    def m_in(wc):
        wt = wc.transpose(0, 1, 3, 2).reshape(S5_GROUPS, ln * gc, p)
        return jnp.concatenate([jnp.real(wt), jnp.imag(wt)], axis=-1)
    min_all = jnp.concatenate([m_in(w_f), m_in(w_b)], axis=-1)
    o_f = c_mat[:, None] * apow[0, :, 1:ln + 1][:, :, None, :]
    o_b = c_mat[:, None] * apow[1, :, ln:0:-1][:, :, None, :]
    def m_out(oc):
        ot = oc.transpose(0, 3, 1, 2).reshape(S5_GROUPS, p, ln * gc)
        return jnp.concatenate([jnp.real(ot), -jnp.imag(ot)], axis=1)
    rhs = jnp.concatenate([toep, m_out(o_f), m_out(o_b)], axis=1)
    a_l = apow[:, :, ln].reshape(2, 1, S5_GROUPS * p)
    return min_all.astype(BF16), rhs.astype(BF16), jnp.real(a_l), jnp.imag(a_l)


def _s5_mixer(u_x, u_c, tables, d_skip, w_glu, b_glu):
    min_all, rhs, a_re, a_im = tables
    b, s, _ = u_x.shape
    ln, p, gc, ng = S5_CHUNK, S5_STATE, S5_GROUP, S5_GROUPS

    def chunks(u):
        nc = u.shape[1] // ln
        return u.reshape(b, nc, ln, ng, gc).transpose(3, 0, 1, 2, 4).reshape(ng, b * nc, ln * gc).astype(BF16), nc

    ux, ncx = chunks(u_x)
    uc, ncc = chunks(u_c)
    sx = _group_matmul(ux, min_all, "s5_local_state").reshape(ng, b, ncx, 4, p)
    sc = _group_matmul(uc, min_all, "s5_local_state_ctx").reshape(ng, b, ncc, 4, p)

    def scan_order(part):
        f = jnp.concatenate([sc[:, :, :, part], sx[:, :, :, part]], axis=2)
        r = jnp.concatenate([sc[:, :, ::-1, part + 2], sx[:, :, ::-1, part + 2]], axis=2)
        return jnp.stack([f, r]).transpose(0, 3, 2, 1, 4).reshape(2, ncc + ncx, b, ng * p)

    z_re, z_im = _s5_scan(scan_order(0), scan_order(1), a_re, a_im)

    def latent(z, d):
        zl = z[d, ncc:]
        if d == 1:
            zl = zl[::-1]
        return zl.reshape(ncx, b, ng, p).transpose(2, 1, 0, 3).reshape(ng, b * ncx, p)

    lhs = jnp.concatenate([ux, latent(z_re, 0).astype(BF16), latent(z_im, 0).astype(BF16),
                           latent(z_re, 1).astype(BF16), latent(z_im, 1).astype(BF16)], axis=-1)
    y = _group_matmul(lhs, rhs, "s5_outputs")
    y = y.reshape(ng, b, ncx, ln, gc).transpose(1, 2, 3, 0, 4).reshape(b * s, S5_CHANNELS)
    return _s5_glu(y, u_x.reshape(b * s, S5_CHANNELS), d_skip, w_glu, b_glu, _tile(b * s, (1024, 512, 256)))


def _out_proj_kernel(x_ref, a_ref, b_ref, wa_ref, wb_ref, gt_ref, g_ref, sc_ref, sh_ref, wr_ref, br_ref,
                     xo_ref, xn_ref, lg_ref):
    y = (jnp.dot(a_ref[...], wa_ref[...], preferred_element_type=F32)
         + jnp.dot(b_ref[...], wb_ref[...], preferred_element_type=F32))
    x = x_ref[...] + gt_ref[0] * y
    xo_ref[...] = x
    xn = x * lax.rsqrt(jnp.mean(x * x, axis=-1, keepdims=True) + EPS) * g_ref[...]
    xn = xn * (1.0 + sc_ref[0]) + sh_ref[0]
    hi = xn.astype(BF16)
    xn_ref[...] = hi
    lo = (xn - hi.astype(F32)).astype(BF16)
    part = jnp.dot(hi, wr_ref[...], preferred_element_type=F32)
    part = part[:, :LANES] + part[:, LANES:] + jnp.dot(lo, wr_ref[:, :LANES], preferred_element_type=F32)
    lg_ref[...] = part[:, :lg_ref.shape[1]] + br_ref[...]


def _out_proj(x, row0, a, bmix, wa, wb, gate, g_ffn, scale, shift, mod_map, w_router, b_router, tm):
    r = a.shape[0]
    d = x.shape[1]
    t0 = row0 // tm
    ne = w_router.shape[1]
    w_hi = w_router.astype(BF16)
    w_lo = (w_router - w_hi.astype(F32)).astype(BF16)
    w_router = jnp.concatenate([_pad_cols(w_hi), _pad_cols(w_lo)], axis=1)
    row = lambda w: pl.BlockSpec((tm, w), lambda i: (i, 0))
    mod = pl.BlockSpec((1, 1, d), lambda i: (mod_map(i), 0, 0))
    full = lambda arr: pl.BlockSpec(arr.shape, lambda i: (0, 0))
    return pl.pallas_call(
        _out_proj_kernel,
        out_shape=[jax.ShapeDtypeStruct((r, d), F32), jax.ShapeDtypeStruct((r, d), BF16),
                   jax.ShapeDtypeStruct((r, ne), F32)],
        grid=(r // tm,),
        in_specs=[pl.BlockSpec((tm, d), lambda i: (i + t0, 0)), row(a.shape[1]), row(bmix.shape[1]),
                  full(wa), full(wb), mod, pl.BlockSpec((1, d), lambda i: (0, 0)), mod, mod,
                  full(w_router), pl.BlockSpec((1, ne), lambda i: (0, 0))],
        out_specs=[row(d), row(d), row(ne)],
        compiler_params=_params("parallel"),
        name="out_proj",
    )(x, a, bmix, wa, wb, gate, g_ffn.reshape(1, d), scale, shift, w_router, b_router.reshape(1, ne))


def _route_kernel(lg_ref, idx_ref, gate_ref, pos_ref, cnt_ref, carry):
    tm, ne = lg_ref.shape

    @pl.when(pl.program_id(0) == 0)
    def _():
        carry[...] = jnp.zeros_like(carry)

    work = lg_ref[...]
    lane = lax.broadcasted_iota(I32, (tm, ne), 1).astype(F32)
    out_lane = lax.broadcasted_iota(I32, (tm, LANES), 1)
    vals, hots, idx_out = [], [], jnp.zeros((tm, LANES), I32)
    for kk in range(TOP_K):
        mx = jnp.max(work, axis=1, keepdims=True)
        idx = jnp.min(jnp.where(work == mx, lane, float(ne)), axis=1, keepdims=True)
        hot = lane == idx
        work = jnp.where(hot, -jnp.inf, work)
        vals.append(mx)
        hots.append(hot)
        idx_out = jnp.where(out_lane == kk, idx.astype(I32), idx_out)
    exps = [jnp.exp(vv - vals[0]) for vv in vals]
    tot = exps[0] + exps[1] + exps[2] + exps[3]
    multi = (hots[0] | hots[1] | hots[2] | hots[3]).astype(BF16)
    r_i = lax.broadcasted_iota(I32, (tm, tm), 0)
    c_i = lax.broadcasted_iota(I32, (tm, tm), 1)
    before = (r_i > c_i).astype(BF16)
    prefix = jnp.dot(before, multi, preferred_element_type=F32) + carry[...]
    gate_out = jnp.zeros((tm, LANES), F32)
    pos_out = jnp.zeros((tm, LANES), I32)
    for kk in range(TOP_K):
        gate_out = jnp.where(out_lane == kk, exps[kk] / tot, gate_out)
        pos = jnp.sum(jnp.where(hots[kk], prefix, 0.0), axis=1, keepdims=True).astype(I32)
        pos_out = jnp.where(out_lane == kk, pos, pos_out)
    idx_ref[...] = idx_out
    gate_ref[...] = gate_out
    pos_ref[...] = pos_out
    carry[...] = carry[...] + jnp.sum(multi.astype(F32), axis=0, keepdims=True)
    cnt_ref[...] = carry[...]


def _route(logits, tm, row0, n):
    ne = logits.shape[1]
    t0 = row0 // tm
    wide = pl.BlockSpec((tm, LANES), lambda i: (i, 0))
    return pl.pallas_call(
        _route_kernel,
        out_shape=[jax.ShapeDtypeStruct((n, LANES), I32), jax.ShapeDtypeStruct((n, LANES), F32),
                   jax.ShapeDtypeStruct((n, LANES), I32), jax.ShapeDtypeStruct((1, ne), F32)],
        grid=(n // tm,),
        in_specs=[pl.BlockSpec((tm, ne), lambda i: (i + t0, 0))],
        out_specs=[wide, wide, wide, pl.BlockSpec((1, ne), lambda i: (0, 0))],
        scratch_shapes=[pltpu.VMEM((1, ne), F32)],
        compiler_params=_params("arbitrary"),
        name="moe_route",
    )(logits)


def _expert_kernel(be_ref, nb_ref, x_ref, wgu_ref, bgu_ref, wd_ref, bd_ref, perm_ref, o_ref, wgu_bf, wd_bf):
    i = pl.program_id(0)
    fresh = jnp.logical_or(i == 0, be_ref[i] != be_ref[jnp.maximum(i - 1, 0)])

    @pl.when(jnp.logical_and(i < nb_ref[0], fresh))
    def _():
        tile = perm_ref.shape[0]
        for t in range(wgu_ref.shape[2] // tile):
            sl = slice(t * tile, (t + 1) * tile)
            wgu_bf[:, sl] = jnp.dot(wgu_ref[0, :, sl].astype(BF16), perm_ref[...],
                                    preferred_element_type=F32).astype(BF16)
        wd_bf[...] = wd_ref[0].astype(BF16)

    @pl.when(i < nb_ref[0])
    def _():
        h = jnp.dot(x_ref[...], wgu_bf[...], preferred_element_type=F32) + bgu_ref[0]
        acts = []
        for t in range(h.shape[1] // (2 * LANES)):
            glu = jnp.minimum(h[:, 2 * LANES * t:2 * LANES * t + LANES], SWIGLU_LIMIT)
            lin = jnp.clip(h[:, 2 * LANES * t + LANES:2 * LANES * (t + 1)], -SWIGLU_LIMIT, SWIGLU_LIMIT)
            acts.append((glu * jax.nn.sigmoid(SWIGLU_ALPHA * glu) * (lin + 1.0)).astype(BF16))
        act = jnp.concatenate(acts, axis=1)
        o_ref[...] = (jnp.dot(act, wd_bf[...], preferred_element_type=F32) + bd_ref[0]).astype(o_ref.dtype)

    @pl.when(i >= nb_ref[0])
    def _():
        o_ref[...] = jnp.zeros_like(o_ref)


def _experts(xs, blk_e, n_used, wgu, bgu, wd, bd):
    n, d = xs.shape
    ff2 = wgu.shape[2]
    nblk = n // MOE_ROWS
    tile = 2 * LANES
    j = np.arange(tile)
    col = np.where(j < LANES, 2 * j, 2 * (j - LANES) + 1)
    perm = jnp.asarray(np.arange(tile)[:, None] == col[None, :], BF16)
    wspec = lambda shape: pl.BlockSpec((1,) + shape, lambda i, be, nb: (be[i], 0, 0))
    return pl.pallas_call(
        _expert_kernel,
        out_shape=jax.ShapeDtypeStruct((n, d), BF16),
        grid_spec=pltpu.PrefetchScalarGridSpec(
            num_scalar_prefetch=2,
            grid=(nblk,),
            in_specs=[pl.BlockSpec((MOE_ROWS, d), lambda i, be, nb: (i, 0)),
                      wspec((d, ff2)), wspec((1, ff2)), wspec((ff2 // 2, d)), wspec((1, d)),
                      pl.BlockSpec((tile, tile), lambda i, be, nb: (0, 0))],
            out_specs=pl.BlockSpec((MOE_ROWS, d), lambda i, be, nb: (i, 0)),
            scratch_shapes=[pltpu.VMEM((d, ff2), BF16), pltpu.VMEM((ff2 // 2, d), BF16)]),
        compiler_params=_params("arbitrary"),
        name="moe_experts",
    )(blk_e, n_used, xs, wgu, bgu, wd, bd, perm)


def _combine_kernel(x_ref, y_ref, gate_ref, gt_ref, g_ref, o_ref, *, final_norm):
    gates = gate_ref[...]
    y = gates[:, 0:1] * y_ref[0].astype(F32)
    for kk in range(1, TOP_K):
        y = y + gates[:, kk:kk + 1] * y_ref[kk].astype(F32)
    x = x_ref[...] + gt_ref[0] * y
    if final_norm:
        x = x * lax.rsqrt(jnp.mean(x * x, axis=-1, keepdims=True) + EPS) * g_ref[...]
    o_ref[...] = x


def _combine_alias_kernel(x_ref, y_ref, gate_ref, gt_ref, g_ref, prev_ref, o_ref, *, final_norm):
    del prev_ref
    _combine_kernel(x_ref, y_ref, gate_ref, gt_ref, g_ref, o_ref, final_norm=final_norm)


def _combine(x, y4, gates, gate_mod, mod_map, g_final, final_norm, tm, row0, prev):
    r, d = x.shape
    n = y4.shape[1]
    t0 = row0 // tm
    in_specs = [pl.BlockSpec((tm, d), lambda i: (i + t0, 0)),
                pl.BlockSpec((TOP_K, tm, d), lambda i: (0, i, 0)),
                pl.BlockSpec((tm, LANES), lambda i: (i, 0)),
                pl.BlockSpec((1, 1, d), lambda i: (mod_map(i + t0), 0, 0)),
                pl.BlockSpec((1, d), lambda i: (0, 0))]
    args = [x, y4, gates, gate_mod, g_final.reshape(1, d)]
    body, aliases = _combine_kernel, {}
    if prev is not None:
        in_specs.append(pl.BlockSpec(memory_space=pl.ANY))
        args.append(prev)
        body, aliases = _combine_alias_kernel, {len(args) - 1: 0}
    return pl.pallas_call(
        functools.partial(body, final_norm=final_norm),
        out_shape=jax.ShapeDtypeStruct((r, d), F32),
        grid=(n // tm,),
        in_specs=in_specs,
        out_specs=pl.BlockSpec((tm, d), lambda i: (i + t0, 0)),
        input_output_aliases=aliases,
        compiler_params=_params("parallel"),
        name="moe_combine",
    )(*args)


def _moe_parts(tiles):
    total = sum(MOE_SPLIT)
    sizes = [tiles * w // total for w in MOE_SPLIT[:-1]]
    sizes.append(tiles - sum(sizes))
    return [sz for sz in sizes if sz > 0]


def _moe(x, xn, logits, gate_mod, mod_map, weights, g_final, final_norm, tm):
    wgu, bgu, wd, bd = weights
    r, d = xn.shape
    routed, row0 = [], 0
    for tiles in _moe_parts(r // tm):
        n = tiles * tm
        nblk = n * TOP_K // MOE_ROWS + N_EXPERTS
        blk_row0 = jnp.arange(nblk, dtype=I32) * MOE_ROWS
        idx_w, gates_w, pos_w, counts = _route(logits, tm, row0, n)
        idx, pos = idx_w[:, :TOP_K], pos_w[:, :TOP_K]
        counts = counts[0].astype(I32)
        padded = (counts + MOE_ROWS - 1) // MOE_ROWS * MOE_ROWS
        pad_end = jnp.cumsum(padded)
        pad_start = pad_end - padded
        dest = pad_start[idx] + pos
        blk_e = jnp.minimum(jnp.sum((pad_end[None, :] <= blk_row0[:, None]).astype(I32), axis=1), N_EXPERTS - 1)
        n_used = (pad_end[-1:] // MOE_ROWS).astype(I32)
        key = idx * n + jnp.arange(n, dtype=I32)[:, None]
        tok_sorted = jnp.sort(key.reshape(-1)) % n
        grp_start = jnp.cumsum(counts) - counts
        slot = blk_row0[:, None] + jnp.arange(MOE_ROWS, dtype=I32)[None, :]
        compact = jnp.minimum(slot + (grp_start - pad_start)[blk_e][:, None],
                              (grp_start + counts - 1)[blk_e][:, None])
        src = tok_sorted.at[jnp.clip(compact, 0, n * TOP_K - 1).reshape(-1)].get(mode="promise_in_bounds")
        xs = xn.at[src + row0].get(mode="promise_in_bounds")
        routed.append((xs, blk_e, n_used, dest, gates_w, row0, n))
        row0 += n
    ys = [_experts(xs, blk_e, n_used, wgu, bgu, wd, bd) for xs, blk_e, n_used, _, _, _, _ in routed]
    out = None
    for p, (_, _, _, dest, gates_w, row0, n) in enumerate(routed):
        y4 = ys[p].at[dest.T.reshape(-1)].get(mode="promise_in_bounds").reshape(TOP_K, n, d)
        out = _combine(x, y4, gates_w, gate_mod, mod_map, g_final, final_norm, tm, row0, out)
    return out


def _moe_weights(w_gate_up, b_gate_up, w_down, b_down):
    ne, d, ff2 = w_gate_up.shape
    b_tiled = b_gate_up.reshape(ne, ff2 // (2 * LANES), LANES, 2).swapaxes(-1, -2).reshape(ne, 1, ff2)
    return w_gate_up, b_tiled, w_down, b_down[:, None, :]


def _rope_tables(n_tokens):
    rows = n_tokens // GRID_W
    row = jnp.repeat(jnp.arange(rows, dtype=I32), GRID_W).astype(F32)
    col = jnp.tile(jnp.arange(GRID_W, dtype=I32), rows).astype(F32)
    inv = ROPE_BASE ** (-jnp.arange(0, ROPE_AXIS_DIM, 2, dtype=F32) / ROPE_AXIS_DIM)
    ang_r, ang_c = row[:, None] * inv, col[:, None] * inv
    cos = jnp.concatenate([jnp.cos(ang_r)] * 2 + [jnp.cos(ang_c)] * 2, axis=-1)
    sin = jnp.concatenate([jnp.sin(ang_r)] * 2 + [jnp.sin(ang_c)] * 2, axis=-1)
    return cos, sin


def _rot_perm():
    q = ROPE_AXIS_DIM // 2
    d = np.arange(HEAD_DIM)
    first = (d % ROPE_AXIS_DIM) < q
    perm = np.where(first, d + q, d - q)
    sign = np.where(first, -1.0, 1.0).astype(np.float32)
    return perm, sign


def _rot_cols(w, n_heads):
    perm, sign = _rot_perm()
    k = w.shape[0]
    wh = w.reshape(k, n_heads, HEAD_DIM)
    return (wh[:, :, perm] * sign).reshape(k, n_heads * HEAD_DIM)


def _to_heads(t, b, n_heads):
    return t.reshape(b, -1, n_heads, HEAD_DIM).transpose(0, 2, 1, 3)


def _from_heads(t):
    b, h, tt, dh = t.shape
    return t.transpose(0, 2, 1, 3).reshape(b * tt, h * dh)


def _layer_even(xa, mod, b, s, lc, g_mix, g_ffn, w_in, w_out, gate_bias, mlstm_norm, sink, router, moe_w):
    d = xa.shape[1]
    nctx = b * lc
    tm = _tile(math.gcd(nctx, s), (512, 256, 128))
    mod_map = lambda i: jnp.where(i * tm < nctx, b, (i * tm - nctx) // s)
    sh1, sc1, gt1, sh2, sc2, gt2 = [mod[:, j][:, None, :] for j in range(6)]

    hm, hs, hk = MLSTM_HEADS * HEAD_DIM, SWA_HEADS * HEAD_DIM, SWA_KV_HEADS * HEAD_DIM
    o0 = np.cumsum([0, hm, hm, hm, hm, 4 * MLSTM_HEADS, hs, hk, hk])
    seg = lambda j: w_in[:, o0[j]:o0[j + 1]]
    w_cat = jnp.concatenate([seg(0), seg(1) * ATTN_SCALE, seg(2), seg(3), _pad_cols(seg(4))], axis=1).astype(BF16)
    w_t = jnp.concatenate([seg(5), _rot_cols(seg(5), SWA_HEADS), seg(6), _rot_cols(seg(6), SWA_KV_HEADS), seg(7)],
                          axis=1).T.astype(BF16)
    widths = [hm, hm, hm, hm, LANES]
    dts = [BF16] * 4 + [F32]
    qa, ka, va, oa, gts, qt, qt_r, kt, kt_r, vt = _in_proj(
        xa, g_mix, sc1, sh1, mod_map, w_cat, widths, dts, tm,
        wt=w_t, t_widths=[hs, hs, hk, hk, hk], t_dtypes=[BF16] * 5)

    hf, hb = _mlstm(qa, ka, va, gts, _pad_cols(gate_bias.astype(F32).reshape(1, -1)), b, lc, s)
    mix_a = _mlstm_out(hf, hb, oa, mlstm_norm.reshape(-1), tm)

    cos, sin = _rope_tables(s)
    ones = jnp.ones((HEAD_DIM,), F32)
    g = SWA_HEADS // SWA_KV_HEADS
    heads = lambda t, n: _to_heads(t, b, n)
    log2e = math.log2(math.e)
    q_x = _q_prep_t(qt, qt_r, nctx, b * s, cos.T, sin.T, ones, ones, ATTN_SCALE * log2e, norm=False)
    k_x = _q_prep_t(kt, kt_r, nctx, b * s, cos.T, sin.T, ones, ones, 1.0, norm=False)
    k_c, v_c = heads(kt[:, :nctx].T, SWA_KV_HEADS), heads(vt[:, :nctx].T, SWA_KV_HEADS)
    q_c = heads(qt[:, :nctx].T, SWA_HEADS)
    sink_h = sink.astype(F32).reshape(SWA_KV_HEADS, g, 1)
    att_x = _swa(q_x, k_x, (kt, vt), jnp.broadcast_to(sink_h * log2e, (SWA_KV_HEADS, g, 2 * SWA_BLOCK)), b, s, lc)
    att_c = _ctx_attn(q_c, k_c, v_c, jnp.broadcast_to(sink_h[..., None], (SWA_KV_HEADS, g, lc, 1))
                      .reshape(SWA_KV_HEADS, g * lc, 1), True)
    mix_b = jnp.concatenate([_from_heads(att_c), att_x], axis=0)

    w_router, b_router = router
    xa, xn2, logits = _out_proj(xa, 0, mix_a, mix_b, w_out[:hm].astype(BF16), w_out[hm:].astype(BF16), gt1, g_ffn,
                                sc2, sh2, mod_map, w_router.astype(F32), b_router.astype(F32), tm)
    return _moe(xa, xn2, logits, gt2, mod_map, moe_w, g_ffn, False, tm)


def _layer_odd_last(xa, mod, b, s, lc, g_mix, g_ffn, w_in, w_out, s5_params, d_skip, w_glu, b_glu,
                    q_norm, k_norm, router, moe_w, g_final):
    nctx = b * lc
    tm = _tile(math.gcd(nctx, s), (512, 256, 128))
    mod_map = lambda i: jnp.where(i * tm < nctx, b, (i * tm - nctx) // s)
    lat_map = lambda i: i * tm // s
    sh1, sc1, gt1, sh2, sc2, gt2 = [mod[:, j][:, None, :] for j in range(6)]

    hq, hk = ATT_HEADS * HEAD_DIM, ATT_KV_HEADS * HEAD_DIM
    o1 = np.cumsum([0, S5_CHANNELS, hq, hk, hk])
    seg = lambda j: w_in[:, o1[j]:o1[j + 1]]
    kpad = lambda w: _pad_cols(w, 2 * LANES)
    w_cat = jnp.concatenate([seg(0), kpad(seg(2)), kpad(_rot_cols(seg(2), ATT_KV_HEADS)), kpad(seg(3))],
                            axis=1).astype(BF16)
    w_q = jnp.concatenate([seg(1), _rot_cols(seg(1), ATT_HEADS)], axis=1).T.astype(BF16)
    widths = [S5_CHANNELS, 2 * LANES, 2 * LANES, 2 * LANES]
    dts = [F32, BF16, BF16, BF16]
    u, k, k_r, v, qt, qt_r = _in_proj(xa, g_mix, sc1, sh1, mod_map, w_cat, widths, dts, tm,
                                      wt=w_q, t_widths=[hq, hq], t_dtypes=[BF16, BF16])

    mix_a = _s5_mixer(u[nctx:].reshape(b, s, S5_CHANNELS), u[:nctx].reshape(b, lc, S5_CHANNELS),
                      _s5_tables(*s5_params), d_skip, w_glu, b_glu)

    cos, sin = _rope_tables(s)
    perm, _ = _rot_perm()
    qn, kn = q_norm.astype(F32), k_norm.astype(F32)
    heads = lambda t, n: _to_heads(t[:, :n * HEAD_DIM], b, n)
    q_x = _q_prep_t(qt, qt_r, nctx, b * s, cos.T, sin.T, qn, qn[perm], ATTN_SCALE * math.log2(math.e))
    k_x = _qk_prep(heads(k[nctx:], ATT_KV_HEADS), heads(k_r[nctx:], ATT_KV_HEADS), cos, sin, kn, kn[perm],
                   norm=True, rope=True, scale=1.0)
    kc_raw = heads(k[:nctx], ATT_KV_HEADS)
    k_c = _qk_prep(kc_raw, kc_raw, cos[:lc], sin[:lc], kn, kn, norm=True, rope=False, scale=1.0)
    k_all = jnp.concatenate([k_c, k_x], axis=2)
    v_all = jnp.concatenate([heads(v[:nctx], ATT_KV_HEADS), heads(v[nctx:], ATT_KV_HEADS)], axis=2)
    mix_b = _flash(q_x, k_all, v_all.swapaxes(-1, -2), b, s)

    w_router, b_router = router
    hs = S5_CHANNELS
    x, xn2, logits = _out_proj(xa, nctx, mix_a, mix_b, w_out[:hs].astype(BF16), w_out[hs:].astype(BF16), gt1, g_ffn,
                               sc2, sh2, lat_map, w_router.astype(F32), b_router.astype(F32), tm)
    return _moe(x, xn2, logits, gt2, lat_map, moe_w, g_final, True, tm)


def kernel(x, c, ctx, c_ctx, l0_w_mod, l0_b_mod, l0_g_mix, l0_g_ffn, l0_w_in, l0_w_out, l0_gate_bias, l0_mlstm_norm, l0_sink, l0_w_router, l0_b_router, l0_w_gate_up, l0_b_gate_up, l0_w_down, l0_b_down, l1_w_mod, l1_b_mod, l1_g_mix, l1_g_ffn, l1_w_in, l1_w_out, l1_lam_re, l1_lam_im, l1_log_dt, l1_b_re, l1_b_im, l1_c_re, l1_c_im, l1_d_skip, l1_w_glu, l1_b_glu, l1_q_norm, l1_k_norm, l1_w_router, l1_b_router, l1_w_gate_up, l1_b_gate_up, l1_w_down, l1_b_down, g_final):
    b, s, d = x.shape
    lc = ctx.shape[1]
    cond = jnp.concatenate([c, c_ctx[None, :]], axis=0)
    cond = jnp.pad(cond, ((0, (-(b + 1)) % 8), (0, 0)))
    mod0 = _silu_linear(cond, l0_w_mod, l0_b_mod)[:b + 1].reshape(b + 1, 6, d)
    mod1 = _silu_linear(cond, l1_w_mod, l1_b_mod)[:b + 1].reshape(b + 1, 6, d)

    moe0 = _moe_weights(l0_w_gate_up, l0_b_gate_up, l0_w_down, l0_b_down)
    moe1 = _moe_weights(l1_w_gate_up, l1_b_gate_up, l1_w_down, l1_b_down)
    xa = jnp.concatenate([ctx.reshape(b * lc, d), x.reshape(b * s, d)], axis=0)
    xa = _layer_even(xa, mod0, b, s, lc, l0_g_mix, l0_g_ffn, l0_w_in, l0_w_out, l0_gate_bias, l0_mlstm_norm,
                     l0_sink, (l0_w_router, l0_b_router), moe0)
    out = _layer_odd_last(xa, mod1, b, s, lc, l1_g_mix, l1_g_ffn, l1_w_in, l1_w_out,
                          (l1_lam_re, l1_lam_im, l1_log_dt, l1_b_re, l1_b_im, l1_c_re, l1_c_im),
                          l1_d_skip, l1_w_glu, l1_b_glu, l1_q_norm, l1_k_norm, (l1_w_router, l1_b_router),
                          moe1, g_final)
    return out.reshape(b, s, d)
```

```python
import functools
import math

import jax
import jax.numpy as jnp
import numpy as np
from jax import lax
from jax.experimental import pallas as pl
from jax.experimental.pallas import tpu as pltpu

F32 = jnp.float32
BF16 = jnp.bfloat16
I32 = jnp.int32

GRID_W = 64
HEAD_DIM = 64
ATTN_SCALE = HEAD_DIM ** -0.5
ROPE_AXIS_DIM = HEAD_DIM // 2
ROPE_BASE = 10000.0
EPS = 1e-6
NEG_INF = -1e30

MLSTM_HEADS = 8
MLSTM_CHUNKS = (256, 128, 64)
SWA_HEADS = 8
SWA_KV_HEADS = 2
SWA_WINDOW = 128
SWA_BLOCK = 128
S5_CHANNELS = 256
S5_GROUP = 16
S5_GROUPS = S5_CHANNELS // S5_GROUP
S5_STATE = 64
S5_CHUNK = 64
ATT_HEADS = 12
ATT_KV_HEADS = 3
N_EXPERTS = 32
TOP_K = 4
SWIGLU_LIMIT = 7.0
SWIGLU_ALPHA = 1.702

LANES = 128
VMEM_LIMIT = 56 * 1024 * 1024
MOE_ROWS = 512
MOE_SPLIT = (1, 1, 1)
HIGHEST = lax.Precision.HIGHEST


def _params(*sem):
    return pltpu.CompilerParams(dimension_semantics=sem, vmem_limit_bytes=VMEM_LIMIT)


def _tile(n, prefs):
    for t in prefs:
        if n % t == 0:
            return t
    return n


def _pad_cols(w, mult=LANES):
    pad = (-w.shape[-1]) % mult
    if pad:
        w = jnp.pad(w, [(0, 0)] * (w.ndim - 1) + [(0, pad)])
    return w


def _linear_kernel(x_ref, w_ref, b_ref, o_ref):
    x = x_ref[...]
    x = x * jax.nn.sigmoid(x)
    o_ref[...] = jnp.dot(x, w_ref[...], precision=HIGHEST, preferred_element_type=F32) + b_ref[...]


def _silu_linear(x, w, b):
    m, k = x.shape
    n = w.shape[1]
    tn = _tile(n, (1024, 512, 256, 128))
    return pl.pallas_call(
        _linear_kernel,
        out_shape=jax.ShapeDtypeStruct((m, n), F32),
        grid=(n // tn,),
        in_specs=[pl.BlockSpec((m, k), lambda j: (0, 0)),
                  pl.BlockSpec((k, tn), lambda j: (0, j)),
                  pl.BlockSpec((1, tn), lambda j: (0, j))],
        out_specs=pl.BlockSpec((m, tn), lambda j: (0, j)),
        compiler_params=_params("arbitrary"),
        name="adaln_linear",
    )(x, w, b.reshape(1, n))


def _in_proj_kernel(x_ref, g_ref, sc_ref, sh_ref, w_ref, wt_ref, *out_refs, widths, t_widths):
    x = x_ref[...]
    xn = x * lax.rsqrt(jnp.mean(x * x, axis=-1, keepdims=True) + EPS) * g_ref[...]
    xb = (xn * (1.0 + sc_ref[0]) + sh_ref[0]).astype(BF16)
    off = 0
    for o_ref, w in zip(out_refs, widths):
        o_ref[...] = jnp.dot(xb, w_ref[:, off:off + w], preferred_element_type=F32).astype(o_ref.dtype)
        off += w
    off = 0
    for o_ref, w in zip(out_refs[len(widths):], t_widths):
        o_ref[...] = lax.dot_general(wt_ref[off:off + w, :], xb, (((1,), (1,)), ((), ())),
                                     preferred_element_type=F32).astype(o_ref.dtype)
        off += w


def _in_proj(x, g, scale, shift, mod_map, w, widths, dtypes, tm, wt=None, t_widths=(), t_dtypes=()):
    r, d = x.shape
    if wt is None:
        wt = jnp.zeros((8, d), BF16)
    return pl.pallas_call(
        functools.partial(_in_proj_kernel, widths=tuple(widths), t_widths=tuple(t_widths)),
        out_shape=([jax.ShapeDtypeStruct((r, wd), dt) for wd, dt in zip(widths, dtypes)]
                   + [jax.ShapeDtypeStruct((wd, r), dt) for wd, dt in zip(t_widths, t_dtypes)]),
        grid=(r // tm,),
        in_specs=[pl.BlockSpec((tm, d), lambda i: (i, 0)),
                  pl.BlockSpec((1, d), lambda i: (0, 0)),
                  pl.BlockSpec((1, 1, d), lambda i: (mod_map(i), 0, 0)),
                  pl.BlockSpec((1, 1, d), lambda i: (mod_map(i), 0, 0)),
                  pl.BlockSpec(w.shape, lambda i: (0, 0)),
                  pl.BlockSpec(wt.shape, lambda i: (0, 0))],
        out_specs=([pl.BlockSpec((tm, wd), lambda i: (i, 0)) for wd in widths]
                   + [pl.BlockSpec((wd, tm), lambda i: (0, i)) for wd in t_widths]),
        compiler_params=_params("parallel"),
        name="in_proj",
    )(x, g.reshape(1, d), scale, shift, w, wt)


def _qk_prep_kernel(x_ref, xr_ref, cos_ref, sin_ref, g_ref, gr_ref, o_ref, *, norm, rope, scale):
    x = x_ref[0].astype(F32)
    if norm:
        s = lax.rsqrt(jnp.mean(x * x, axis=-1, keepdims=True) + EPS)
        x = x * s * g_ref[...]
    if rope:
        xr = xr_ref[0].astype(F32)
        if norm:
            xr = xr * s * gr_ref[...]
        x = x * cos_ref[...] + xr * sin_ref[...]
    o_ref[0] = (x * scale).astype(o_ref.dtype)


def _qk_prep(x, x_rot, cos, sin, gain, gain_rot, *, norm, rope, scale):
    b, h, t, dh = x.shape
    tt = _tile(t, (512, 256, 128))
    blk = pl.BlockSpec((1, h, tt, dh), lambda i, j: (i, 0, j, 0))
    tab = pl.BlockSpec((tt, dh), lambda i, j: (j, 0))
    vec = pl.BlockSpec((1, dh), lambda i, j: (0, 0))
    return pl.pallas_call(
        functools.partial(_qk_prep_kernel, norm=norm, rope=rope, scale=scale),
        out_shape=jax.ShapeDtypeStruct(x.shape, BF16),
        grid=(b, t // tt),
        in_specs=[blk, blk, tab, tab, vec, vec],
        out_specs=blk,
        compiler_params=_params("parallel", "parallel"),
        name="qk_prep",
    )(x, x_rot, cos, sin, gain.reshape(1, dh), gain_rot.reshape(1, dh))


def _q_prep_t_kernel(x_ref, xr_ref, cos_ref, sin_ref, g_ref, gr_ref, o_ref, *, norm, scale):
    w, tt = x_ref.shape
    nh = w // HEAD_DIM
    x = x_ref[...].astype(F32).reshape(nh, HEAD_DIM, tt)
    xr = xr_ref[...].astype(F32).reshape(nh, HEAD_DIM, tt)
    if norm:
        s = lax.rsqrt(jnp.mean(x * x, axis=1, keepdims=True) + EPS)
        x, xr = x * s * g_ref[...], xr * s * gr_ref[...]
    y = x * cos_ref[...] + xr * sin_ref[...]
    o_ref[...] = (y * scale).reshape(w, tt).astype(o_ref.dtype)


def _q_prep_t(xt, xt_rot, col0, n_cols, cos_t, sin_t, gain, gain_rot, scale, norm=True):
    w = xt.shape[0]
    s = cos_t.shape[1]
    tt = _tile(math.gcd(col0, s), (512, 256, 128))
    blk = pl.BlockSpec((w, tt), lambda j: (0, j + col0 // tt))
    tab = pl.BlockSpec((HEAD_DIM, tt), lambda j: (0, j % (s // tt)))
    vec = pl.BlockSpec((HEAD_DIM, 1), lambda j: (0, 0))
    return pl.pallas_call(
        functools.partial(_q_prep_t_kernel, norm=norm, scale=scale),
        out_shape=jax.ShapeDtypeStruct((w, n_cols), BF16),
        grid=(n_cols // tt,),
        in_specs=[blk, blk, tab, tab, vec, vec],
        out_specs=pl.BlockSpec((w, tt), lambda j: (0, j)),
        compiler_params=_params("parallel"),
        name="q_prep_t",
    )(xt, xt_rot, cos_t, sin_t, gain.reshape(HEAD_DIM, 1), gain_rot.reshape(HEAD_DIM, 1))


def _log_sigmoid(x):
    return jnp.minimum(x, 0.0) - jnp.log(1.0 + jnp.exp(-jnp.abs(x)))


def _mlstm_kernel(qf, kf, vf, gf, qb, kb, vb, gb, bias_ref, hf_ref, hb_ref, c_scr, n_scr, m_scr):
    ch, nh, dh = qf.shape[0], MLSTM_HEADS, HEAD_DIM

    @pl.when(pl.program_id(1) == 0)
    def _():
        c_scr[...] = jnp.zeros_like(c_scr)
        n_scr[...] = jnp.zeros_like(n_scr)
        m_scr[...] = jnp.full_like(m_scr, NEG_INF)

    row = lax.broadcasted_iota(I32, (ch, ch), 0)
    col = lax.broadcasted_iota(I32, (ch, ch), 1)
    lane_lo = lax.broadcasted_iota(I32, (ch, 2 * dh), 1) < dh
    row_lo = lax.broadcasted_iota(I32, (2 * dh, 2 * dh), 0) < dh
    col_lo = lax.broadcasted_iota(I32, (2 * dh, 2 * dh), 1) < dh
    vec_lo = lax.broadcasted_iota(I32, (1, 2 * dh), 1) < dh
    pair = lambda a, b: jnp.where(lane_lo, a, b)
    refs = ((qf, kf, vf, gf, hf_ref), (qb, kb, vb, gb, hb_ref))
    jobs = [(d, p) for d in range(2) for p in range(nh // 2)]
    sl = lambda p: slice(2 * dh * p, 2 * dh * (p + 1))

    qk = {}
    for d, p in jobs:
        q2, k2 = refs[d][0][:, sl(p)], refs[d][1][:, sl(p)]
        zero = jnp.zeros_like(q2)
        q_st = jnp.concatenate([jnp.where(lane_lo, q2, zero), jnp.where(lane_lo, zero, q2)], axis=0)
        qk[d, p] = lax.dot_general(q_st, k2, (((1,), (1,)), ((), ())), preferred_element_type=F32)

    st = {}
    for d in range(2):
        seen = (col <= row) if d == 0 else (col >= row)
        tri = seen.astype(F32)
        tri_t = ((row <= col) if d == 0 else (row >= col)).astype(F32)
        last = ch - 1 if d == 0 else 0
        g = refs[d][3][...] + bias_ref[...]
        g_t = g.T
        lo = 2 * nh * d
        log2e = math.log2(math.e)
        li_col, lf_col = g[:, lo:lo + nh] * log2e, _log_sigmoid(g[:, lo + nh:lo + 2 * nh]) * log2e
        li_row, lf_row = g_t[lo:lo + nh, :] * log2e, _log_sigmoid(g_t[lo + nh:lo + 2 * nh, :]) * log2e
        b_col = jnp.dot(tri, lf_col, precision=HIGHEST, preferred_element_type=F32)
        b_row = jnp.dot(lf_row, tri_t, precision=HIGHEST, preferred_element_type=F32)
        src_row = li_row - b_row
        for h in range(nh):
            bc, br = b_col[:, h:h + 1], b_row[h:h + 1, :]
            lic = li_col[:, h:h + 1]
            b_last = br[:, last:last + 1]
            m_prev = m_scr[d, h // 2][:, (h % 2) * dh:(h % 2) * dh + 1]
            d_log = jnp.where(seen, bc + src_row[h:h + 1, :], NEG_INF)
            inter_log = bc + m_prev
            m_t = jnp.maximum(inter_log, jnp.max(d_log, axis=1, keepdims=True))
            w_log = b_last - bc + lic
            m_new = jnp.maximum(b_last + m_prev, jnp.max(w_log, axis=0, keepdims=True))
            st[d, h] = dict(dmat=jnp.exp2(d_log - m_t), inter=jnp.exp2(inter_log - m_t), floor=jnp.exp2(-m_t),
                            m_new=m_new, decay=jnp.exp2(b_last + m_prev - m_new), wn=jnp.exp2(w_log - m_new))

    mm = {}
    for d, p in jobs:
        a, b = st[d, 2 * p], st[d, 2 * p + 1]
        q2, k2, v2 = refs[d][0][:, sl(p)], refs[d][1][:, sl(p)], refs[d][2][:, sl(p)]
        s = qk[d, p] * jnp.concatenate([a["dmat"], b["dmat"]], axis=0)
        kw = k2.astype(F32) * pair(a["wn"], b["wn"])
        mm[d, p] = dict(
            s_sum=jnp.sum(s, axis=1, keepdims=True), kw_sum=jnp.sum(kw, axis=0, keepdims=True),
            sv=jnp.dot(s.astype(BF16), v2, preferred_element_type=F32),
            q_c=jnp.dot(q2, c_scr[d, p].astype(BF16), preferred_element_type=F32),
            kv=lax.dot_general(kw.astype(BF16), v2, (((0,), (0,)), ((), ())), preferred_element_type=F32))

    for d, p in jobs:
        a, b, r = st[d, 2 * p], st[d, 2 * p + 1], mm[d, p]
        q2 = refs[d][0][:, sl(p)]
        n_prev = n_scr[d, p]
        num = pair(a["inter"], b["inter"]) * r["q_c"] + jnp.where(lane_lo, r["sv"][:ch], r["sv"][ch:])
        qn = q2.astype(F32) * n_prev
        qn_a = jnp.sum(jnp.where(lane_lo, qn, 0.0), axis=1, keepdims=True)
        qn_b = jnp.sum(jnp.where(lane_lo, 0.0, qn), axis=1, keepdims=True)
        den_a = jnp.maximum(jnp.abs(a["inter"] * qn_a + r["s_sum"][:ch]), a["floor"])
        den_b = jnp.maximum(jnp.abs(b["inter"] * qn_b + r["s_sum"][ch:]), b["floor"])
        refs[d][4][:, sl(p)] = num / pair(den_a, den_b)
        c_scr[d, p] = (jnp.where(row_lo, a["decay"], b["decay"]) * c_scr[d, p]
                       + jnp.where(row_lo == col_lo, r["kv"], 0.0))
        n_scr[d, p] = jnp.where(vec_lo, a["decay"], b["decay"]) * n_prev + r["kw_sum"]
        m_scr[d, p] = jnp.where(vec_lo, a["m_new"], b["m_new"])


def _mlstm(q, k, v, gates, bias, b, lc, s):
    r, w = q.shape
    ch = _tile(math.gcd(lc, s), MLSTM_CHUNKS)
    ncc, ncx = lc // ch, s // ch
    base = b * ncc

    def fwd(i, c):
        return jnp.where(c < ncc, i * ncc + c, base + i * ncx + (c - ncc)), 0

    def bwd(i, c):
        return jnp.where(c < ncc, i * ncc + (ncc - 1 - c), base + i * ncx + (ncx - 1 - (c - ncc))), 0

    spec = lambda width, m: pl.BlockSpec((ch, width), m)
    npair = MLSTM_HEADS // 2
    return pl.pallas_call(
        _mlstm_kernel,
        out_shape=[jax.ShapeDtypeStruct((r, w), F32)] * 2,
        grid=(b, ncc + ncx),
        in_specs=[spec(w, fwd), spec(w, fwd), spec(w, fwd), spec(LANES, fwd),
                  spec(w, bwd), spec(w, bwd), spec(w, bwd), spec(LANES, bwd),
                  pl.BlockSpec((1, LANES), lambda i, c: (0, 0))],
        out_specs=[spec(w, fwd), spec(w, bwd)],
        scratch_shapes=[pltpu.VMEM((2, npair, 2 * HEAD_DIM, 2 * HEAD_DIM), F32),
                        pltpu.VMEM((2, npair, 1, 2 * HEAD_DIM), F32),
                        pltpu.VMEM((2, npair, 1, 2 * HEAD_DIM), F32)],
        compiler_params=_params("parallel", "arbitrary"),
        name="mlstm_scan",
    )(q, k, v, gates, q, k, v, gates, bias)


def _mlstm_out_kernel(hf_ref, hb_ref, o_ref, nrm_ref, ones_ref, y_ref):
    h = hf_ref[...] + hb_ref[...]
    ms = jnp.dot((h * h).astype(BF16), ones_ref[...], preferred_element_type=F32) * (1.0 / HEAD_DIM)
    hn = h * lax.rsqrt(ms + EPS) * nrm_ref[...]
    y_ref[...] = (jax.nn.sigmoid(o_ref[...].astype(F32)) * hn).astype(y_ref.dtype)


def _head_ones(width):
    idx = np.arange(width) // HEAD_DIM
    return jnp.asarray(idx[:, None] == idx[None, :], BF16)


def _mlstm_out(hf, hb, o, norm, tm):
    r, w = hf.shape
    blk = pl.BlockSpec((tm, w), lambda i: (i, 0))
    return pl.pallas_call(
        _mlstm_out_kernel,
        out_shape=jax.ShapeDtypeStruct((r, w), BF16),
        grid=(r // tm,),
        in_specs=[blk, blk, blk,
                  pl.BlockSpec((1, w), lambda i: (0, 0)),
                  pl.BlockSpec((w, w), lambda i: (0, 0))],
        out_specs=blk,
        compiler_params=_params("parallel"),
        name="mlstm_out",
    )(hf, hb, o, norm.reshape(1, w), _head_ones(w))


def _swa_kernel(q_ref, k0, k1, k2, k3, v0, v1, v2, v3, kc_ref, vc_ref, sink_ref, o_ref, *, seq):
    w, dh = SWA_BLOCK, HEAD_DIM
    tq = q_ref.shape[1]
    g = q_ref.shape[0] // dh
    i = pl.program_id(2)
    k_loc = jnp.concatenate([k0[...], k1[...], k2[...], k3[...]], axis=1)
    v_loc = jnp.concatenate([v0[...], v1[...], v2[...], v3[...]], axis=1)
    kpos = (2 * i - 1) * w + lax.broadcasted_iota(I32, (4 * w, tq), 0)
    qpos = i * tq + lax.broadcasted_iota(I32, (4 * w, tq), 1)
    valid = (jnp.abs(qpos - kpos) <= SWA_WINDOW) & (kpos >= 0) & (kpos < seq)
    t_dot = lambda kt, q: lax.dot_general(kt, q, (((0,), (0,)), ((), ())), preferred_element_type=F32)
    qs = [q_ref[h * dh:(h + 1) * dh, :] for h in range(g)]
    s_loc = [jnp.where(valid, t_dot(k_loc, qs[h]), NEG_INF) for h in range(g)]
    s_ctx = [t_dot(kc_ref[...], qs[h]) for h in range(g)]
    outs = []
    for h in range(g):
        sink = sink_ref[0, h:h + 1, :]
        m = jnp.maximum(jnp.maximum(jnp.max(s_loc[h], axis=0, keepdims=True),
                                    jnp.max(s_ctx[h], axis=0, keepdims=True)), sink)
        p_loc = jnp.exp2(s_loc[h] - m)
        p_ctx = jnp.exp2(s_ctx[h] - m)
        den = jnp.sum(p_loc, axis=0, keepdims=True) + jnp.sum(p_ctx, axis=0, keepdims=True) + jnp.exp2(sink - m)
        o = (jnp.dot(v_loc, p_loc.astype(BF16), preferred_element_type=F32)
             + jnp.dot(vc_ref[...], p_ctx.astype(BF16), preferred_element_type=F32))
        outs.append(o / den)
    o_ref[...] = jnp.concatenate(outs, axis=0).T.astype(o_ref.dtype)


def _swa(qt, kt, kvt, sink2, b, s, lc):
    dh = HEAD_DIM
    nh, hkv = qt.shape[0] // dh, kt.shape[0] // dh
    g = nh // hkv
    w = SWA_BLOCK
    tq = 2 * w
    nb, nq = s // w, s // tq
    c0 = b * lc // w
    clampi = lambda j: jnp.clip(j, 0, nb - 1)
    kspec = lambda o: pl.BlockSpec((dh, w), lambda bi, hi, i: (hi, bi * nb + clampi(2 * i + o)))
    vspec = lambda o: pl.BlockSpec((dh, w), lambda bi, hi, i: (hi, c0 + bi * nb + clampi(2 * i + o)))
    cspec = pl.BlockSpec((dh, lc), lambda bi, hi, i: (hi, bi))
    k_all, v_all = kvt
    return pl.pallas_call(
        functools.partial(_swa_kernel, seq=s),
        out_shape=jax.ShapeDtypeStruct((b * s, nh * dh), BF16),
        grid=(b, hkv, nq),
        in_specs=[pl.BlockSpec((g * dh, tq), lambda bi, hi, i: (hi, bi * nq + i)),
                  kspec(-1), kspec(0), kspec(1), kspec(2), vspec(-1), vspec(0), vspec(1), vspec(2),
                  cspec, cspec, pl.BlockSpec((1, g, tq), lambda bi, hi, i: (hi, 0, 0))],
        out_specs=pl.BlockSpec((tq, g * dh), lambda bi, hi, i: (bi * nq + i, hi)),
        compiler_params=_params("parallel", "parallel", "arbitrary"),
        name="window_attention",
    )(qt, kt, kt, kt, kt, v_all, v_all, v_all, v_all, k_all, v_all, sink2)


def _ctx_attn_kernel(q_ref, k_ref, v_ref, sink_ref, o_ref, *, use_sink):
    g, lq, dh = q_ref.shape[1:]
    q = q_ref[0].reshape(g * lq, dh)
    s = lax.dot_general(q, k_ref[0, 0], (((1,), (1,)), ((), ())), preferred_element_type=F32) * ATTN_SCALE
    m = jnp.max(s, axis=1, keepdims=True)
    if use_sink:
        m = jnp.maximum(m, sink_ref[0])
    p = jnp.exp(s - m)
    den = jnp.sum(p, axis=1, keepdims=True)
    if use_sink:
        den = den + jnp.exp(sink_ref[0] - m)
    o = jnp.dot(p.astype(BF16), v_ref[0, 0], preferred_element_type=F32) / den
    o_ref[0] = o.reshape(g, lq, dh).astype(o_ref.dtype)


def _ctx_attn(q, k, v, sink_col, use_sink):
    b, nh, lq, dh = q.shape
    hkv = k.shape[1]
    g = nh // hkv
    qblk = pl.BlockSpec((1, g, lq, dh), lambda bi, hi: (bi, hi, 0, 0))
    kblk = pl.BlockSpec((1, 1, k.shape[2], dh), lambda bi, hi: (bi, hi, 0, 0))
    return pl.pallas_call(
        functools.partial(_ctx_attn_kernel, use_sink=use_sink),
        out_shape=jax.ShapeDtypeStruct(q.shape, BF16),
        grid=(b, hkv),
        in_specs=[qblk, kblk, kblk, pl.BlockSpec((1, g * lq, 1), lambda bi, hi: (hi, 0, 0))],
        out_specs=qblk,
        compiler_params=_params("parallel", "parallel"),
        name="context_attention",
    )(q, k, v, sink_col)


def _flash_kernel(q_ref, k_ref, v_ref, o_ref, m_scr, acc_scr, sa_scr, sb_scr, *, tk):
    dh = HEAD_DIM
    g = q_ref.shape[0] // dh
    n = k_ref.shape[2] // tk
    m_scr[...] = jnp.full_like(m_scr, NEG_INF)
    acc_scr[...] = jnp.zeros_like(acc_scr)

    def scores(j, dst):
        kk = k_ref[0, 0, pl.ds(pl.multiple_of(j * tk, tk), tk), :]
        for h in range(g):
            dst[h] = jnp.dot(kk, q_ref[h * dh:(h + 1) * dh, :], preferred_element_type=F32)

    def update(j, src):
        vv = v_ref[0, 0, :, pl.ds(pl.multiple_of(j * tk, tk), tk)]
        ss = [src[h] for h in range(g)]
        m_olds = [m_scr[h] for h in range(g)]
        m_news = [jnp.maximum(m_olds[h], jnp.max(ss[h], axis=0, keepdims=True)) for h in range(g)]
        ps = [jnp.exp2(ss[h] - m_news[h]).astype(BF16) for h in range(g)]
        pvs = [jnp.dot(vv, ps[h], preferred_element_type=F32) for h in range(g)]
        for h in range(g):
            acc_scr[h] = jnp.exp2(m_olds[h] - m_news[h]) * acc_scr[h] + pvs[h]
            m_scr[h] = m_news[h]

    scores(0, sa_scr)

    def body(i, carry):
        scores(2 * i + 1, sb_scr)
        update(2 * i, sa_scr)
        scores(2 * i + 2, sa_scr)
        update(2 * i + 1, sb_scr)
        return carry

    lax.fori_loop(0, (n - 1) // 2, body, 0)
    if n % 2 == 1:
        update(n - 1, sa_scr)
    else:
        scores(n - 1, sb_scr)
        update(n - 2, sa_scr)
        update(n - 1, sb_scr)
    acc = acc_scr[...]
    out = (acc[:, :dh] / acc[:, dh:dh + 1]).reshape(g * dh, -1)
    o_ref[...] = out.T.astype(o_ref.dtype)


def _flash(qt, k, vt, b, s, tq=None, tk=None):
    w = qt.shape[0]
    hkv, t, dh = k.shape[1], k.shape[2], k.shape[3]
    g = w // dh // hkv
    tq = tq or _tile(s, (256, 128))
    tk = tk or _tile(t, (256, 128))
    nq = s // tq
    ones_rows = 8
    vt = jnp.concatenate([vt, jnp.ones((b, hkv, ones_rows, t), vt.dtype)], axis=2)
    nv = dh + ones_rows
    return pl.pallas_call(
        functools.partial(_flash_kernel, tk=tk),
        out_shape=jax.ShapeDtypeStruct((b * s, w), BF16),
        grid=(b, hkv, nq),
        in_specs=[pl.BlockSpec((g * dh, tq), lambda bi, hi, i: (hi, bi * nq + i)),
                  pl.BlockSpec((1, 1, t, dh), lambda bi, hi, i: (bi, hi, 0, 0)),
                  pl.BlockSpec((1, 1, nv, t), lambda bi, hi, i: (bi, hi, 0, 0))],
        out_specs=pl.BlockSpec((tq, g * dh), lambda bi, hi, i: (bi * nq + i, hi)),
        scratch_shapes=[pltpu.VMEM((g, 1, tq), F32), pltpu.VMEM((g, nv, tq), F32),
                        pltpu.VMEM((g, tk, tq), F32), pltpu.VMEM((g, tk, tq), F32)],
        compiler_params=_params("parallel", "parallel", "arbitrary"),
        name="dense_attention",
    )(qt, k, vt)


def _bmm_kernel(a_ref, b_ref, o_ref):
    o_ref[0] = jnp.dot(a_ref[0], b_ref[0], preferred_element_type=F32).astype(o_ref.dtype)


def _group_matmul(a, bmat, name):
    g, m, k = a.shape
    n = bmat.shape[2]
    tm = _tile(m, (1024, 512, 256, 128, 64, 32, 16, 8))
    return pl.pallas_call(
        _bmm_kernel,
        out_shape=jax.ShapeDtypeStruct((g, m, n), F32),
        grid=(g, m // tm),
        in_specs=[pl.BlockSpec((1, tm, k), lambda gi, i: (gi, i, 0)),
                  pl.BlockSpec((1, k, n), lambda gi, i: (gi, 0, 0))],
        out_specs=pl.BlockSpec((1, tm, n), lambda gi, i: (gi, i, 0)),
        compiler_params=_params("parallel", "parallel"),
        name=name,
    )(a, bmat)


def _s5_scan_kernel(sre_ref, sim_ref, are_ref, aim_ref, zre_ref, zim_ref):
    nd, nsteps = sre_ref.shape[0], sre_ref.shape[1]
    for d in range(nd):
        a_re, a_im = are_ref[d], aim_ref[d]

        def body(i, carry, d=d, a_re=a_re, a_im=a_im):
            z_re, z_im = carry
            zre_ref[d, i] = z_re
            zim_ref[d, i] = z_im
            return (a_re * z_re - a_im * z_im + sre_ref[d, i], a_re * z_im + a_im * z_re + sim_ref[d, i])

        zero = jnp.zeros(sre_ref.shape[2:], F32)
        lax.fori_loop(0, nsteps, body, (zero, zero))


def _s5_scan(s_re, s_im, a_re, a_im):
    full = lambda arr: pl.BlockSpec(arr.shape, lambda i: (0,) * arr.ndim)
    return pl.pallas_call(
        _s5_scan_kernel,
        out_shape=[jax.ShapeDtypeStruct(s_re.shape, F32)] * 2,
        grid=(1,),
        in_specs=[full(s_re), full(s_im), full(a_re), full(a_im)],
        out_specs=[full(s_re), full(s_im)],
        compiler_params=_params("arbitrary"),
        name="s5_chunk_scan",
    )(s_re, s_im, a_re, a_im)


def _s5_glu_kernel(y_ref, u_ref, d_ref, w_ref, b_ref, o_ref):
    y = y_ref[...] + d_ref[...] * u_ref[...]
    y = jax.nn.gelu(y)
    gate = jnp.dot(y.astype(BF16), w_ref[...], preferred_element_type=F32) + b_ref[...]
    o_ref[...] = (y * jax.nn.sigmoid(gate)).astype(o_ref.dtype)


def _s5_glu(y, u, d_skip, w_glu, b_glu, tm):
    r, c = y.shape
    blk = pl.BlockSpec((tm, c), lambda i: (i, 0))
    vec = pl.BlockSpec((1, c), lambda i: (0, 0))
    return pl.pallas_call(
        _s5_glu_kernel,
        out_shape=jax.ShapeDtypeStruct((r, c), BF16),
        grid=(r // tm,),
        in_specs=[blk, blk, vec, pl.BlockSpec((c, c), lambda i: (0, 0)), vec],
        out_specs=blk,
        compiler_params=_params("parallel"),
        name="s5_readout_glu",
    )(y, u, d_skip.reshape(1, c), w_glu.astype(BF16), b_glu.reshape(1, c))


def _s5_tables(lam_re, lam_im, log_dt, b_re, b_im, c_re, c_im):
    ln, p, gc = S5_CHUNK, S5_STATE, S5_GROUP
    lam = lax.complex(lam_re.astype(F32), lam_im.astype(F32))
    dt = jnp.exp(log_dt.astype(F32))[..., None]
    a_bar = jnp.exp(lam * dt)
    b_scale = (a_bar - 1.0) / lam
    b_mat = lax.complex(b_re.astype(F32), b_im.astype(F32))
    c_mat = lax.complex(c_re.astype(F32), c_im.astype(F32))
    tau = jnp.arange(ln + 1, dtype=F32)
    apow = jnp.exp((lam * dt)[:, :, None, :] * tau[None, None, :, None])
    drive = b_scale[..., None] * b_mat[None]
    kern = jnp.real(jnp.einsum('gcp,dgtp,dgpe->dgtce', c_mat, apow[:, :, :ln], drive))
    kc = jnp.concatenate([kern[1, :, :0:-1], (kern[0, :, :1] + kern[1, :, :1]), kern[0, :, 1:]], axis=1)
    kc_e = kc.transpose(0, 3, 1, 2).reshape(S5_GROUPS, gc, (2 * ln - 1) * gc)
    toep = jnp.stack([kc_e[:, :, (ln - 1 - s) * gc:(2 * ln - 1 - s) * gc] for s in range(ln)], axis=1)
    toep = toep.reshape(S5_GROUPS, ln * gc, ln * gc)
    w_f = apow[0, :, ln - 1::-1][:, :ln, :, None] * drive[0][:, None]
    w_b = apow[1, :, :ln, :, None] * drive[1][:, None]
    def m_in(wc):
        wt = wc.transpose(0, 1, 3, 2).reshape(S5_GROUPS, ln * gc, p)
        return jnp.concatenate([jnp.real(wt), jnp.imag(wt)], axis=-1)
    min_all = jnp.concatenate([m_in(w_f), m_in(w_b)], axis=-1)
    o_f = c_mat[:, None] * apow[0, :, 1:ln + 1][:, :, None, :]
    o_b = c_mat[:, None] * apow[1, :, ln:0:-1][:, :, None, :]
    def m_out(oc):
        ot = oc.transpose(0, 3, 1, 2).reshape(S5_GROUPS, p, ln * gc)
        return jnp.concatenate([jnp.real(ot), -jnp.imag(ot)], axis=1)
    rhs = jnp.concatenate([toep, m_out(o_f), m_out(o_b)], axis=1)
    a_l = apow[:, :, ln].reshape(2, 1, S5_GROUPS * p)
    return min_all.astype(BF16), rhs.astype(BF16), jnp.real(a_l), jnp.imag(a_l)


def _s5_mixer(u_x, u_c, tables, d_skip, w_glu, b_glu):
    min_all, rhs, a_re, a_im = tables
    b, s, _ = u_x.shape
    ln, p, gc, ng = S5_CHUNK, S5_STATE, S5_GROUP, S5_GROUPS

    def chunks(u):
        nc = u.shape[1] // ln
        return u.reshape(b, nc, ln, ng, gc).transpose(3, 0, 1, 2, 4).reshape(ng, b * nc, ln * gc).astype(BF16), nc

    ux, ncx = chunks(u_x)
    uc, ncc = chunks(u_c)
    sx = _group_matmul(ux, min_all, "s5_local_state").reshape(ng, b, ncx, 4, p)
    sc = _group_matmul(uc, min_all, "s5_local_state_ctx").reshape(ng, b, ncc, 4, p)

    def scan_order(part):
        f = jnp.concatenate([sc[:, :, :, part], sx[:, :, :, part]], axis=2)
        r = jnp.concatenate([sc[:, :, ::-1, part + 2], sx[:, :, ::-1, part + 2]], axis=2)
        return jnp.stack([f, r]).transpose(0, 3, 2, 1, 4).reshape(2, ncc + ncx, b, ng * p)

    z_re, z_im = _s5_scan(scan_order(0), scan_order(1), a_re, a_im)

    def latent(z, d):
        zl = z[d, ncc:]
        if d == 1:
            zl = zl[::-1]
        return zl.reshape(ncx, b, ng, p).transpose(2, 1, 0, 3).reshape(ng, b * ncx, p)

    lhs = jnp.concatenate([ux, latent(z_re, 0).astype(BF16), latent(z_im, 0).astype(BF16),
                           latent(z_re, 1).astype(BF16), latent(z_im, 1).astype(BF16)], axis=-1)
    y = _group_matmul(lhs, rhs, "s5_outputs")
    y = y.reshape(ng, b, ncx, ln, gc).transpose(1, 2, 3, 0, 4).reshape(b * s, S5_CHANNELS)
    return _s5_glu(y, u_x.reshape(b * s, S5_CHANNELS), d_skip, w_glu, b_glu, _tile(b * s, (1024, 512, 256)))


def _out_proj_kernel(x_ref, a_ref, b_ref, wa_ref, wb_ref, gt_ref, g_ref, sc_ref, sh_ref, wr_ref, br_ref,
                     xo_ref, xn_ref, lg_ref):
    y = (jnp.dot(a_ref[...], wa_ref[...], preferred_element_type=F32)
         + jnp.dot(b_ref[...], wb_ref[...], preferred_element_type=F32))
    x = x_ref[...] + gt_ref[0] * y
    xo_ref[...] = x
    xn = x * lax.rsqrt(jnp.mean(x * x, axis=-1, keepdims=True) + EPS) * g_ref[...]
    xn = xn * (1.0 + sc_ref[0]) + sh_ref[0]
    hi = xn.astype(BF16)
    xn_ref[...] = hi
    lo = (xn - hi.astype(F32)).astype(BF16)
    part = jnp.dot(hi, wr_ref[...], preferred_element_type=F32)
    part = part[:, :LANES] + part[:, LANES:] + jnp.dot(lo, wr_ref[:, :LANES], preferred_element_type=F32)
    lg_ref[...] = part[:, :lg_ref.shape[1]] + br_ref[...]


def _out_proj(x, row0, a, bmix, wa, wb, gate, g_ffn, scale, shift, mod_map, w_router, b_router, tm):
    r = a.shape[0]
    d = x.shape[1]
    t0 = row0 // tm
    ne = w_router.shape[1]
    w_hi = w_router.astype(BF16)
    w_lo = (w_router - w_hi.astype(F32)).astype(BF16)
    w_router = jnp.concatenate([_pad_cols(w_hi), _pad_cols(w_lo)], axis=1)
    row = lambda w: pl.BlockSpec((tm, w), lambda i: (i, 0))
    mod = pl.BlockSpec((1, 1, d), lambda i: (mod_map(i), 0, 0))
    full = lambda arr: pl.BlockSpec(arr.shape, lambda i: (0, 0))
    return pl.pallas_call(
        _out_proj_kernel,
        out_shape=[jax.ShapeDtypeStruct((r, d), F32), jax.ShapeDtypeStruct((r, d), BF16),
                   jax.ShapeDtypeStruct((r, ne), F32)],
        grid=(r // tm,),
        in_specs=[pl.BlockSpec((tm, d), lambda i: (i + t0, 0)), row(a.shape[1]), row(bmix.shape[1]),
                  full(wa), full(wb), mod, pl.BlockSpec((1, d), lambda i: (0, 0)), mod, mod,
                  full(w_router), pl.BlockSpec((1, ne), lambda i: (0, 0))],
        out_specs=[row(d), row(d), row(ne)],
        compiler_params=_params("parallel"),
        name="out_proj",
    )(x, a, bmix, wa, wb, gate, g_ffn.reshape(1, d), scale, shift, w_router, b_router.reshape(1, ne))


def _route_kernel(lg_ref, idx_ref, gate_ref, pos_ref, cnt_ref, carry):
    tm, ne = lg_ref.shape

    @pl.when(pl.program_id(0) == 0)
    def _():
        carry[...] = jnp.zeros_like(carry)

    work = lg_ref[...]
    lane = lax.broadcasted_iota(I32, (tm, ne), 1).astype(F32)
    out_lane = lax.broadcasted_iota(I32, (tm, LANES), 1)
    vals, hots, idx_out = [], [], jnp.zeros((tm, LANES), I32)
    for kk in range(TOP_K):
        mx = jnp.max(work, axis=1, keepdims=True)
        idx = jnp.min(jnp.where(work == mx, lane, float(ne)), axis=1, keepdims=True)
        hot = lane == idx
        work = jnp.where(hot, -jnp.inf, work)
        vals.append(mx)
        hots.append(hot)
        idx_out = jnp.where(out_lane == kk, idx.astype(I32), idx_out)
    exps = [jnp.exp(vv - vals[0]) for vv in vals]
    tot = exps[0] + exps[1] + exps[2] + exps[3]
    multi = (hots[0] | hots[1] | hots[2] | hots[3]).astype(BF16)
    r_i = lax.broadcasted_iota(I32, (tm, tm), 0)
    c_i = lax.broadcasted_iota(I32, (tm, tm), 1)
    before = (r_i > c_i).astype(BF16)
    prefix = jnp.dot(before, multi, preferred_element_type=F32) + carry[...]
    gate_out = jnp.zeros((tm, LANES), F32)
    pos_out = jnp.zeros((tm, LANES), I32)
    for kk in range(TOP_K):
        gate_out = jnp.where(out_lane == kk, exps[kk] / tot, gate_out)
        pos = jnp.sum(jnp.where(hots[kk], prefix, 0.0), axis=1, keepdims=True).astype(I32)
        pos_out = jnp.where(out_lane == kk, pos, pos_out)
    idx_ref[...] = idx_out
    gate_ref[...] = gate_out
    pos_ref[...] = pos_out
    carry[...] = carry[...] + jnp.sum(multi.astype(F32), axis=0, keepdims=True)
    cnt_ref[...] = carry[...]


def _route(logits, tm, row0, n):
    ne = logits.shape[1]
    t0 = row0 // tm
    wide = pl.BlockSpec((tm, LANES), lambda i: (i, 0))
    return pl.pallas_call(
        _route_kernel,
        out_shape=[jax.ShapeDtypeStruct((n, LANES), I32), jax.ShapeDtypeStruct((n, LANES), F32),
                   jax.ShapeDtypeStruct((n, LANES), I32), jax.ShapeDtypeStruct((1, ne), F32)],
        grid=(n // tm,),
        in_specs=[pl.BlockSpec((tm, ne), lambda i: (i + t0, 0))],
        out_specs=[wide, wide, wide, pl.BlockSpec((1, ne), lambda i: (0, 0))],
        scratch_shapes=[pltpu.VMEM((1, ne), F32)],
        compiler_params=_params("arbitrary"),
        name="moe_route",
    )(logits)


def _expert_kernel(be_ref, nb_ref, x_ref, wgu_ref, bgu_ref, wd_ref, bd_ref, perm_ref, o_ref, wgu_bf, wd_bf):
    i = pl.program_id(0)
    fresh = jnp.logical_or(i == 0, be_ref[i] != be_ref[jnp.maximum(i - 1, 0)])

    @pl.when(jnp.logical_and(i < nb_ref[0], fresh))
    def _():
        tile = perm_ref.shape[0]
        for t in range(wgu_ref.shape[2] // tile):
            sl = slice(t * tile, (t + 1) * tile)
            wgu_bf[:, sl] = jnp.dot(wgu_ref[0, :, sl].astype(BF16), perm_ref[...],
                                    preferred_element_type=F32).astype(BF16)
        wd_bf[...] = wd_ref[0].astype(BF16)

    @pl.when(i < nb_ref[0])
    def _():
        h = jnp.dot(x_ref[...], wgu_bf[...], preferred_element_type=F32) + bgu_ref[0]
        acts = []
        for t in range(h.shape[1] // (2 * LANES)):
            glu = jnp.minimum(h[:, 2 * LANES * t:2 * LANES * t + LANES], SWIGLU_LIMIT)
            lin = jnp.clip(h[:, 2 * LANES * t + LANES:2 * LANES * (t + 1)], -SWIGLU_LIMIT, SWIGLU_LIMIT)
            acts.append((glu * jax.nn.sigmoid(SWIGLU_ALPHA * glu) * (lin + 1.0)).astype(BF16))
        act = jnp.concatenate(acts, axis=1)
        o_ref[...] = (jnp.dot(act, wd_bf[...], preferred_element_type=F32) + bd_ref[0]).astype(o_ref.dtype)

    @pl.when(i >= nb_ref[0])
    def _():
        o_ref[...] = jnp.zeros_like(o_ref)


def _experts(xs, blk_e, n_used, wgu, bgu, wd, bd):
    n, d = xs.shape
    ff2 = wgu.shape[2]
    nblk = n // MOE_ROWS
    tile = 2 * LANES
    j = np.arange(tile)
    col = np.where(j < LANES, 2 * j, 2 * (j - LANES) + 1)
    perm = jnp.asarray(np.arange(tile)[:, None] == col[None, :], BF16)
    wspec = lambda shape: pl.BlockSpec((1,) + shape, lambda i, be, nb: (be[i], 0, 0))
    return pl.pallas_call(
        _expert_kernel,
        out_shape=jax.ShapeDtypeStruct((n, d), BF16),
        grid_spec=pltpu.PrefetchScalarGridSpec(
            num_scalar_prefetch=2,
            grid=(nblk,),
            in_specs=[pl.BlockSpec((MOE_ROWS, d), lambda i, be, nb: (i, 0)),
                      wspec((d, ff2)), wspec((1, ff2)), wspec((ff2 // 2, d)), wspec((1, d)),
                      pl.BlockSpec((tile, tile), lambda i, be, nb: (0, 0))],
            out_specs=pl.BlockSpec((MOE_ROWS, d), lambda i, be, nb: (i, 0)),
            scratch_shapes=[pltpu.VMEM((d, ff2), BF16), pltpu.VMEM((ff2 // 2, d), BF16)]),
        compiler_params=_params("arbitrary"),
        name="moe_experts",
    )(blk_e, n_used, xs, wgu, bgu, wd, bd, perm)


def _combine_kernel(x_ref, y_ref, gate_ref, gt_ref, g_ref, o_ref, *, final_norm):
    gates = gate_ref[...]
    y = gates[:, 0:1] * y_ref[0].astype(F32)
    for kk in range(1, TOP_K):
        y = y + gates[:, kk:kk + 1] * y_ref[kk].astype(F32)
    x = x_ref[...] + gt_ref[0] * y
    if final_norm:
        x = x * lax.rsqrt(jnp.mean(x * x, axis=-1, keepdims=True) + EPS) * g_ref[...]
    o_ref[...] = x


def _combine_alias_kernel(x_ref, y_ref, gate_ref, gt_ref, g_ref, prev_ref, o_ref, *, final_norm):
    del prev_ref
    _combine_kernel(x_ref, y_ref, gate_ref, gt_ref, g_ref, o_ref, final_norm=final_norm)


def _combine(x, y4, gates, gate_mod, mod_map, g_final, final_norm, tm, row0, prev):
    r, d = x.shape
    n = y4.shape[1]
    t0 = row0 // tm
    in_specs = [pl.BlockSpec((tm, d), lambda i: (i + t0, 0)),
                pl.BlockSpec((TOP_K, tm, d), lambda i: (0, i, 0)),
                pl.BlockSpec((tm, LANES), lambda i: (i, 0)),
                pl.BlockSpec((1, 1, d), lambda i: (mod_map(i + t0), 0, 0)),
                pl.BlockSpec((1, d), lambda i: (0, 0))]
    args = [x, y4, gates, gate_mod, g_final.reshape(1, d)]
    body, aliases = _combine_kernel, {}
    if prev is not None:
        in_specs.append(pl.BlockSpec(memory_space=pl.ANY))
        args.append(prev)
        body, aliases = _combine_alias_kernel, {len(args) - 1: 0}
    return pl.pallas_call(
        functools.partial(body, final_norm=final_norm),
        out_shape=jax.ShapeDtypeStruct((r, d), F32),
        grid=(n // tm,),
        in_specs=in_specs,
        out_specs=pl.BlockSpec((tm, d), lambda i: (i + t0, 0)),
        input_output_aliases=aliases,
        compiler_params=_params("parallel"),
        name="moe_combine",
    )(*args)


def _moe_parts(tiles):
    total = sum(MOE_SPLIT)
    sizes = [tiles * w // total for w in MOE_SPLIT[:-1]]
    sizes.append(tiles - sum(sizes))
    return [sz for sz in sizes if sz > 0]


def _moe(x, xn, logits, gate_mod, mod_map, weights, g_final, final_norm, tm):
    wgu, bgu, wd, bd = weights
    r, d = xn.shape
    routed, row0 = [], 0
    for tiles in _moe_parts(r // tm):
        n = tiles * tm
        nblk = n * TOP_K // MOE_ROWS + N_EXPERTS
        blk_row0 = jnp.arange(nblk, dtype=I32) * MOE_ROWS
        idx_w, gates_w, pos_w, counts = _route(logits, tm, row0, n)
        idx, pos = idx_w[:, :TOP_K], pos_w[:, :TOP_K]
        counts = counts[0].astype(I32)
        padded = (counts + MOE_ROWS - 1) // MOE_ROWS * MOE_ROWS
        pad_end = jnp.cumsum(padded)
        pad_start = pad_end - padded
        dest = pad_start[idx] + pos
        blk_e = jnp.minimum(jnp.sum((pad_end[None, :] <= blk_row0[:, None]).astype(I32), axis=1), N_EXPERTS - 1)
        n_used = (pad_end[-1:] // MOE_ROWS).astype(I32)
        key = idx * n + jnp.arange(n, dtype=I32)[:, None]
        tok_sorted = jnp.sort(key.reshape(-1)) % n
        grp_start = jnp.cumsum(counts) - counts
        slot = blk_row0[:, None] + jnp.arange(MOE_ROWS, dtype=I32)[None, :]
        compact = jnp.minimum(slot + (grp_start - pad_start)[blk_e][:, None],
                              (grp_start + counts - 1)[blk_e][:, None])
        src = tok_sorted.at[jnp.clip(compact, 0, n * TOP_K - 1).reshape(-1)].get(mode="promise_in_bounds")
        xs = xn.at[src + row0].get(mode="promise_in_bounds")
        routed.append((xs, blk_e, n_used, dest, gates_w, row0, n))
        row0 += n
    ys = [_experts(xs, blk_e, n_used, wgu, bgu, wd, bd) for xs, blk_e, n_used, _, _, _, _ in routed]
    out = None
    for p, (_, _, _, dest, gates_w, row0, n) in enumerate(routed):
        y4 = ys[p].at[dest.T.reshape(-1)].get(mode="promise_in_bounds").reshape(TOP_K, n, d)
        out = _combine(x, y4, gates_w, gate_mod, mod_map, g_final, final_norm, tm, row0, out)
    return out


def _moe_weights(w_gate_up, b_gate_up, w_down, b_down):
    ne, d, ff2 = w_gate_up.shape
    b_tiled = b_gate_up.reshape(ne, ff2 // (2 * LANES), LANES, 2).swapaxes(-1, -2).reshape(ne, 1, ff2)
    return w_gate_up, b_tiled, w_down, b_down[:, None, :]


def _rope_tables(n_tokens):
    rows = n_tokens // GRID_W
    row = jnp.repeat(jnp.arange(rows, dtype=I32), GRID_W).astype(F32)
    col = jnp.tile(jnp.arange(GRID_W, dtype=I32), rows).astype(F32)
    inv = ROPE_BASE ** (-jnp.arange(0, ROPE_AXIS_DIM, 2, dtype=F32) / ROPE_AXIS_DIM)
    ang_r, ang_c = row[:, None] * inv, col[:, None] * inv
    cos = jnp.concatenate([jnp.cos(ang_r)] * 2 + [jnp.cos(ang_c)] * 2, axis=-1)
    sin = jnp.concatenate([jnp.sin(ang_r)] * 2 + [jnp.sin(ang_c)] * 2, axis=-1)
    return cos, sin


def _rot_perm():
    q = ROPE_AXIS_DIM // 2
    d = np.arange(HEAD_DIM)
    first = (d % ROPE_AXIS_DIM) < q
    perm = np.where(first, d + q, d - q)
    sign = np.where(first, -1.0, 1.0).astype(np.float32)
    return perm, sign


def _rot_cols(w, n_heads):
    perm, sign = _rot_perm()
    k = w.shape[0]
    wh = w.reshape(k, n_heads, HEAD_DIM)
    return (wh[:, :, perm] * sign).reshape(k, n_heads * HEAD_DIM)


def _to_heads(t, b, n_heads):
    return t.reshape(b, -1, n_heads, HEAD_DIM).transpose(0, 2, 1, 3)


def _from_heads(t):
    b, h, tt, dh = t.shape
    return t.transpose(0, 2, 1, 3).reshape(b * tt, h * dh)


def _layer_even(xa, mod, b, s, lc, g_mix, g_ffn, w_in, w_out, gate_bias, mlstm_norm, sink, router, moe_w):
    d = xa.shape[1]
    nctx = b * lc
    tm = _tile(math.gcd(nctx, s), (512, 256, 128))
    mod_map = lambda i: jnp.where(i * tm < nctx, b, (i * tm - nctx) // s)
    sh1, sc1, gt1, sh2, sc2, gt2 = [mod[:, j][:, None, :] for j in range(6)]

    hm, hs, hk = MLSTM_HEADS * HEAD_DIM, SWA_HEADS * HEAD_DIM, SWA_KV_HEADS * HEAD_DIM
    o0 = np.cumsum([0, hm, hm, hm, hm, 4 * MLSTM_HEADS, hs, hk, hk])
    seg = lambda j: w_in[:, o0[j]:o0[j + 1]]
    w_cat = jnp.concatenate([seg(0), seg(1) * ATTN_SCALE, seg(2), seg(3), _pad_cols(seg(4))], axis=1).astype(BF16)
    w_t = jnp.concatenate([seg(5), _rot_cols(seg(5), SWA_HEADS), seg(6), _rot_cols(seg(6), SWA_KV_HEADS), seg(7)],
                          axis=1).T.astype(BF16)
    widths = [hm, hm, hm, hm, LANES]
    dts = [BF16] * 4 + [F32]
    qa, ka, va, oa, gts, qt, qt_r, kt, kt_r, vt = _in_proj(
        xa, g_mix, sc1, sh1, mod_map, w_cat, widths, dts, tm,
        wt=w_t, t_widths=[hs, hs, hk, hk, hk], t_dtypes=[BF16] * 5)

    hf, hb = _mlstm(qa, ka, va, gts, _pad_cols(gate_bias.astype(F32).reshape(1, -1)), b, lc, s)
    mix_a = _mlstm_out(hf, hb, oa, mlstm_norm.reshape(-1), tm)

    cos, sin = _rope_tables(s)
    ones = jnp.ones((HEAD_DIM,), F32)
    g = SWA_HEADS // SWA_KV_HEADS
    heads = lambda t, n: _to_heads(t, b, n)
    log2e = math.log2(math.e)
    q_x = _q_prep_t(qt, qt_r, nctx, b * s, cos.T, sin.T, ones, ones, ATTN_SCALE * log2e, norm=False)
    k_x = _q_prep_t(kt, kt_r, nctx, b * s, cos.T, sin.T, ones, ones, 1.0, norm=False)
    k_c, v_c = heads(kt[:, :nctx].T, SWA_KV_HEADS), heads(vt[:, :nctx].T, SWA_KV_HEADS)
    q_c = heads(qt[:, :nctx].T, SWA_HEADS)
    sink_h = sink.astype(F32).reshape(SWA_KV_HEADS, g, 1)
    att_x = _swa(q_x, k_x, (kt, vt), jnp.broadcast_to(sink_h * log2e, (SWA_KV_HEADS, g, 2 * SWA_BLOCK)), b, s, lc)
    att_c = _ctx_attn(q_c, k_c, v_c, jnp.broadcast_to(sink_h[..., None], (SWA_KV_HEADS, g, lc, 1))
                      .reshape(SWA_KV_HEADS, g * lc, 1), True)
    mix_b = jnp.concatenate([_from_heads(att_c), att_x], axis=0)

    w_router, b_router = router
    xa, xn2, logits = _out_proj(xa, 0, mix_a, mix_b, w_out[:hm].astype(BF16), w_out[hm:].astype(BF16), gt1, g_ffn,
                                sc2, sh2, mod_map, w_router.astype(F32), b_router.astype(F32), tm)
    return _moe(xa, xn2, logits, gt2, mod_map, moe_w, g_ffn, False, tm)


def _layer_odd_last(xa, mod, b, s, lc, g_mix, g_ffn, w_in, w_out, s5_params, d_skip, w_glu, b_glu,
                    q_norm, k_norm, router, moe_w, g_final):
    nctx = b * lc
    tm = _tile(math.gcd(nctx, s), (512, 256, 128))
    mod_map = lambda i: jnp.where(i * tm < nctx, b, (i * tm - nctx) // s)
    lat_map = lambda i: i * tm // s
    sh1, sc1, gt1, sh2, sc2, gt2 = [mod[:, j][:, None, :] for j in range(6)]

    hq, hk = ATT_HEADS * HEAD_DIM, ATT_KV_HEADS * HEAD_DIM
    o1 = np.cumsum([0, S5_CHANNELS, hq, hk, hk])
    seg = lambda j: w_in[:, o1[j]:o1[j + 1]]
    kpad = lambda w: _pad_cols(w, 2 * LANES)
    w_cat = jnp.concatenate([seg(0), kpad(seg(2)), kpad(_rot_cols(seg(2), ATT_KV_HEADS)), kpad(seg(3))],
                            axis=1).astype(BF16)
    w_q = jnp.concatenate([seg(1), _rot_cols(seg(1), ATT_HEADS)], axis=1).T.astype(BF16)
    widths = [S5_CHANNELS, 2 * LANES, 2 * LANES, 2 * LANES]
    dts = [F32, BF16, BF16, BF16]
    u, k, k_r, v, qt, qt_r = _in_proj(xa, g_mix, sc1, sh1, mod_map, w_cat, widths, dts, tm,
                                      wt=w_q, t_widths=[hq, hq], t_dtypes=[BF16, BF16])

    mix_a = _s5_mixer(u[nctx:].reshape(b, s, S5_CHANNELS), u[:nctx].reshape(b, lc, S5_CHANNELS),
                      _s5_tables(*s5_params), d_skip, w_glu, b_glu)

    cos, sin = _rope_tables(s)
    perm, _ = _rot_perm()
    qn, kn = q_norm.astype(F32), k_norm.astype(F32)
    heads = lambda t, n: _to_heads(t[:, :n * HEAD_DIM], b, n)
    q_x = _q_prep_t(qt, qt_r, nctx, b * s, cos.T, sin.T, qn, qn[perm], ATTN_SCALE * math.log2(math.e))
    k_x = _qk_prep(heads(k[nctx:], ATT_KV_HEADS), heads(k_r[nctx:], ATT_KV_HEADS), cos, sin, kn, kn[perm],
                   norm=True, rope=True, scale=1.0)
    kc_raw = heads(k[:nctx], ATT_KV_HEADS)
    k_c = _qk_prep(kc_raw, kc_raw, cos[:lc], sin[:lc], kn, kn, norm=True, rope=False, scale=1.0)
    k_all = jnp.concatenate([k_c, k_x], axis=2)
    v_all = jnp.concatenate([heads(v[:nctx], ATT_KV_HEADS), heads(v[nctx:], ATT_KV_HEADS)], axis=2)
    mix_b = _flash(q_x, k_all, v_all.swapaxes(-1, -2), b, s)

    w_router, b_router = router
    hs = S5_CHANNELS
    x, xn2, logits = _out_proj(xa, nctx, mix_a, mix_b, w_out[:hs].astype(BF16), w_out[hs:].astype(BF16), gt1, g_ffn,
                               sc2, sh2, lat_map, w_router.astype(F32), b_router.astype(F32), tm)
    return _moe(x, xn2, logits, gt2, lat_map, moe_w, g_final, True, tm)


def kernel(x, c, ctx, c_ctx, l0_w_mod, l0_b_mod, l0_g_mix, l0_g_ffn, l0_w_in, l0_w_out, l0_gate_bias, l0_mlstm_norm, l0_sink, l0_w_router, l0_b_router, l0_w_gate_up, l0_b_gate_up, l0_w_down, l0_b_down, l1_w_mod, l1_b_mod, l1_g_mix, l1_g_ffn, l1_w_in, l1_w_out, l1_lam_re, l1_lam_im, l1_log_dt, l1_b_re, l1_b_im, l1_c_re, l1_c_im, l1_d_skip, l1_w_glu, l1_b_glu, l1_q_norm, l1_k_norm, l1_w_router, l1_b_router, l1_w_gate_up, l1_b_gate_up, l1_w_down, l1_b_down, g_final):
    b, s, d = x.shape
    lc = ctx.shape[1]
    cond = jnp.concatenate([c, c_ctx[None, :]], axis=0)
    cond = jnp.pad(cond, ((0, (-(b + 1)) % 8), (0, 0)))
    mod0 = _silu_linear(cond, l0_w_mod, l0_b_mod)[:b + 1].reshape(b + 1, 6, d)
    mod1 = _silu_linear(cond, l1_w_mod, l1_b_mod)[:b + 1].reshape(b + 1, 6, d)

    moe0 = _moe_weights(l0_w_gate_up, l0_b_gate_up, l0_w_down, l0_b_down)
    moe1 = _moe_weights(l1_w_gate_up, l1_b_gate_up, l1_w_down, l1_b_down)
    xa = jnp.concatenate([ctx.reshape(b * lc, d), x.reshape(b * s, d)], axis=0)
    xa = _layer_even(xa, mod0, b, s, lc, l0_g_mix, l0_g_ffn, l0_w_in, l0_w_out, l0_gate_bias, l0_mlstm_norm,
                     l0_sink, (l0_w_router, l0_b_router), moe0)
    out = _layer_odd_last(xa, mod1, b, s, lc, l1_g_mix, l1_g_ffn, l1_w_in, l1_w_out,
                          (l1_lam_re, l1_lam_im, l1_log_dt, l1_b_re, l1_b_im, l1_c_re, l1_c_im),
                          l1_d_skip, l1_w_glu, l1_b_glu, l1_q_norm, l1_k_norm, (l1_w_router, l1_b_router),
                          moe1, g_final)
    return out.reshape(b, s, d)
```

```python
import functools
import math

import jax
import jax.numpy as jnp
import numpy as np
from jax import lax
from jax.experimental import pallas as pl
from jax.experimental.pallas import tpu as pltpu

F32 = jnp.float32
BF16 = jnp.bfloat16
I32 = jnp.int32

GRID_W = 64
HEAD_DIM = 64
ATTN_SCALE = HEAD_DIM ** -0.5
ROPE_AXIS_DIM = HEAD_DIM // 2
ROPE_BASE = 10000.0
EPS = 1e-6
NEG_INF = -1e30

MLSTM_HEADS = 8
MLSTM_CHUNKS = (256, 128, 64)
SWA_HEADS = 8
SWA_KV_HEADS = 2
SWA_WINDOW = 128
SWA_BLOCK = 128
S5_CHANNELS = 256
S5_GROUP = 16
S5_GROUPS = S5_CHANNELS // S5_GROUP
S5_STATE = 64
S5_CHUNK = 64
ATT_HEADS = 12
ATT_KV_HEADS = 3
N_EXPERTS = 32
TOP_K = 4
SWIGLU_LIMIT = 7.0
SWIGLU_ALPHA = 1.702

LANES = 128
VMEM_LIMIT = 56 * 1024 * 1024
MOE_ROWS = 512
MOE_SPLIT = (1, 1)
FLASH_DEPTH = 4
HIGHEST = lax.Precision.HIGHEST


def _params(*sem):
    return pltpu.CompilerParams(dimension_semantics=sem, vmem_limit_bytes=VMEM_LIMIT)


def _tile(n, prefs):
    for t in prefs:
        if n % t == 0:
            return t
    return n


def _pad_cols(w, mult=LANES):
    pad = (-w.shape[-1]) % mult
    if pad:
        w = jnp.pad(w, [(0, 0)] * (w.ndim - 1) + [(0, pad)])
    return w


def _linear_kernel(x_ref, w_ref, b_ref, o_ref):
    x = x_ref[...]
    x = x * jax.nn.sigmoid(x)
    o_ref[...] = jnp.dot(x, w_ref[...], precision=HIGHEST, preferred_element_type=F32) + b_ref[...]


def _silu_linear(x, w, b):
    m, k = x.shape
    n = w.shape[1]
    tn = _tile(n, (1024, 512, 256, 128))
    return pl.pallas_call(
        _linear_kernel,
        out_shape=jax.ShapeDtypeStruct((m, n), F32),
        grid=(n // tn,),
        in_specs=[pl.BlockSpec((m, k), lambda j: (0, 0)),
                  pl.BlockSpec((k, tn), lambda j: (0, j)),
                  pl.BlockSpec((1, tn), lambda j: (0, j))],
        out_specs=pl.BlockSpec((m, tn), lambda j: (0, j)),
        compiler_params=_params("arbitrary"),
        name="adaln_linear",
    )(x, w, b.reshape(1, n))


def _in_proj_kernel(x_ref, g_ref, sc_ref, sh_ref, w_ref, wt_ref, *out_refs, widths, t_widths):
    x = x_ref[...]
    xn = x * lax.rsqrt(jnp.mean(x * x, axis=-1, keepdims=True) + EPS) * g_ref[...]
    xb = (xn * (1.0 + sc_ref[0]) + sh_ref[0]).astype(BF16)
    off = 0
    for o_ref, w in zip(out_refs, widths):
        o_ref[...] = jnp.dot(xb, w_ref[:, off:off + w], preferred_element_type=F32).astype(o_ref.dtype)
        off += w
    off = 0
    for o_ref, w in zip(out_refs[len(widths):], t_widths):
        o_ref[...] = lax.dot_general(wt_ref[off:off + w, :], xb, (((1,), (1,)), ((), ())),
                                     preferred_element_type=F32).astype(o_ref.dtype)
        off += w


def _in_proj(x, g, scale, shift, mod_map, w, widths, dtypes, tm, wt=None, t_widths=(), t_dtypes=()):
    r, d = x.shape
    if wt is None:
        wt = jnp.zeros((8, d), BF16)
    return pl.pallas_call(
        functools.partial(_in_proj_kernel, widths=tuple(widths), t_widths=tuple(t_widths)),
        out_shape=([jax.ShapeDtypeStruct((r, wd), dt) for wd, dt in zip(widths, dtypes)]
                   + [jax.ShapeDtypeStruct((wd, r), dt) for wd, dt in zip(t_widths, t_dtypes)]),
        grid=(r // tm,),
        in_specs=[pl.BlockSpec((tm, d), lambda i: (i, 0)),
                  pl.BlockSpec((1, d), lambda i: (0, 0)),
                  pl.BlockSpec((1, 1, d), lambda i: (mod_map(i), 0, 0)),
                  pl.BlockSpec((1, 1, d), lambda i: (mod_map(i), 0, 0)),
                  pl.BlockSpec(w.shape, lambda i: (0, 0)),
                  pl.BlockSpec(wt.shape, lambda i: (0, 0))],
        out_specs=([pl.BlockSpec((tm, wd), lambda i: (i, 0)) for wd in widths]
                   + [pl.BlockSpec((wd, tm), lambda i: (0, i)) for wd in t_widths]),
        compiler_params=_params("parallel"),
        name="in_proj",
    )(x, g.reshape(1, d), scale, shift, w, wt)


def _qk_prep_kernel(x_ref, xr_ref, cos_ref, sin_ref, g_ref, gr_ref, o_ref, *, norm, rope, scale):
    x = x_ref[0].astype(F32)
    if norm:
        s = lax.rsqrt(jnp.mean(x * x, axis=-1, keepdims=True) + EPS)
        x = x * s * g_ref[...]
    if rope:
        xr = xr_ref[0].astype(F32)
        if norm:
            xr = xr * s * gr_ref[...]
        x = x * cos_ref[...] + xr * sin_ref[...]
    o_ref[0] = (x * scale).astype(o_ref.dtype)


def _qk_prep(x, x_rot, cos, sin, gain, gain_rot, *, norm, rope, scale):
    b, h, t, dh = x.shape
    tt = _tile(t, (512, 256, 128))
    blk = pl.BlockSpec((1, h, tt, dh), lambda i, j: (i, 0, j, 0))
    tab = pl.BlockSpec((tt, dh), lambda i, j: (j, 0))
    vec = pl.BlockSpec((1, dh), lambda i, j: (0, 0))
    return pl.pallas_call(
        functools.partial(_qk_prep_kernel, norm=norm, rope=rope, scale=scale),
        out_shape=jax.ShapeDtypeStruct(x.shape, BF16),
        grid=(b, t // tt),
        in_specs=[blk, blk, tab, tab, vec, vec],
        out_specs=blk,
        compiler_params=_params("parallel", "parallel"),
        name="qk_prep",
    )(x, x_rot, cos, sin, gain.reshape(1, dh), gain_rot.reshape(1, dh))


def _q_prep_t_kernel(x_ref, xr_ref, cos_ref, sin_ref, g_ref, gr_ref, o_ref, *, norm, scale):
    w, tt = x_ref.shape
    nh = w // HEAD_DIM
    x = x_ref[...].astype(F32).reshape(nh, HEAD_DIM, tt)
    xr = xr_ref[...].astype(F32).reshape(nh, HEAD_DIM, tt)
    if norm:
        s = lax.rsqrt(jnp.mean(x * x, axis=1, keepdims=True) + EPS)
        x, xr = x * s * g_ref[...], xr * s * gr_ref[...]
    y = x * cos_ref[...] + xr * sin_ref[...]
    o_ref[...] = (y * scale).reshape(w, tt).astype(o_ref.dtype)


def _q_prep_t(xt, xt_rot, col0, n_cols, cos_t, sin_t, gain, gain_rot, scale, norm=True):
    w = xt.shape[0]
    s = cos_t.shape[1]
    tt = _tile(math.gcd(col0, s), (512, 256, 128))
    blk = pl.BlockSpec((w, tt), lambda j: (0, j + col0 // tt))
    tab = pl.BlockSpec((HEAD_DIM, tt), lambda j: (0, j % (s // tt)))
    vec = pl.BlockSpec((HEAD_DIM, 1), lambda j: (0, 0))
    return pl.pallas_call(
        functools.partial(_q_prep_t_kernel, norm=norm, scale=scale),
        out_shape=jax.ShapeDtypeStruct((w, n_cols), BF16),
        grid=(n_cols // tt,),
        in_specs=[blk, blk, tab, tab, vec, vec],
        out_specs=pl.BlockSpec((w, tt), lambda j: (0, j)),
        compiler_params=_params("parallel"),
        name="q_prep_t",
    )(xt, xt_rot, cos_t, sin_t, gain.reshape(HEAD_DIM, 1), gain_rot.reshape(HEAD_DIM, 1))


def _log_sigmoid(x):
    return jnp.minimum(x, 0.0) - jnp.log(1.0 + jnp.exp(-jnp.abs(x)))


def _mlstm_kernel(qf, kf, vf, gf, qb, kb, vb, gb, bias_ref, hf_ref, hb_ref, c_scr, n_scr, m_scr):
    ch, nh, dh = qf.shape[0], MLSTM_HEADS, HEAD_DIM

    @pl.when(pl.program_id(1) == 0)
    def _():
        c_scr[...] = jnp.zeros_like(c_scr)
        n_scr[...] = jnp.zeros_like(n_scr)
        m_scr[...] = jnp.full_like(m_scr, NEG_INF)

    row = lax.broadcasted_iota(I32, (ch, ch), 0)
    col = lax.broadcasted_iota(I32, (ch, ch), 1)
    lane_lo = lax.broadcasted_iota(I32, (ch, 2 * dh), 1) < dh
    row_lo = lax.broadcasted_iota(I32, (2 * dh, 2 * dh), 0) < dh
    col_lo = lax.broadcasted_iota(I32, (2 * dh, 2 * dh), 1) < dh
    vec_lo = lax.broadcasted_iota(I32, (1, 2 * dh), 1) < dh
    pair = lambda a, b: jnp.where(lane_lo, a, b)
    refs = ((qf, kf, vf, gf, hf_ref), (qb, kb, vb, gb, hb_ref))
    jobs = [(d, p) for d in range(2) for p in range(nh // 2)]
    sl = lambda p: slice(2 * dh * p, 2 * dh * (p + 1))

    qk = {}
    for d, p in jobs:
        q2, k2 = refs[d][0][:, sl(p)], refs[d][1][:, sl(p)]
        zero = jnp.zeros_like(q2)
        q_st = jnp.concatenate([jnp.where(lane_lo, q2, zero), jnp.where(lane_lo, zero, q2)], axis=0)
        qk[d, p] = lax.dot_general(q_st, k2, (((1,), (1,)), ((), ())), preferred_element_type=F32)

    st = {}
    for d in range(2):
        seen = (col <= row) if d == 0 else (col >= row)
        tri = seen.astype(F32)
        tri_t = ((row <= col) if d == 0 else (row >= col)).astype(F32)
        last = ch - 1 if d == 0 else 0
        g = refs[d][3][...] + bias_ref[...]
        g_t = g.T
        lo = 2 * nh * d
        log2e = math.log2(math.e)
        li_col, lf_col = g[:, lo:lo + nh] * log2e, _log_sigmoid(g[:, lo + nh:lo + 2 * nh]) * log2e
        li_row, lf_row = g_t[lo:lo + nh, :] * log2e, _log_sigmoid(g_t[lo + nh:lo + 2 * nh, :]) * log2e
        b_col = jnp.dot(tri, lf_col, precision=HIGHEST, preferred_element_type=F32)
        b_row = jnp.dot(lf_row, tri_t, precision=HIGHEST, preferred_element_type=F32)
        src_row = li_row - b_row
        for h in range(nh):
            bc, br = b_col[:, h:h + 1], b_row[h:h + 1, :]
            lic = li_col[:, h:h + 1]
            b_last = br[:, last:last + 1]
            m_prev = m_scr[d, h // 2][:, (h % 2) * dh:(h % 2) * dh + 1]
            d_log = jnp.where(seen, bc + src_row[h:h + 1, :], NEG_INF)
            inter_log = bc + m_prev
            m_t = jnp.maximum(inter_log, jnp.max(d_log, axis=1, keepdims=True))
            w_log = b_last - bc + lic
            m_new = jnp.maximum(b_last + m_prev, jnp.max(w_log, axis=0, keepdims=True))
            st[d, h] = dict(dmat=jnp.exp2(d_log - m_t), inter=jnp.exp2(inter_log - m_t), floor=jnp.exp2(-m_t),
                            m_new=m_new, decay=jnp.exp2(b_last + m_prev - m_new), wn=jnp.exp2(w_log - m_new))

    mm = {}
    for d, p in jobs:
        a, b = st[d, 2 * p], st[d, 2 * p + 1]
        q2, k2, v2 = refs[d][0][:, sl(p)], refs[d][1][:, sl(p)], refs[d][2][:, sl(p)]
        s = qk[d, p] * jnp.concatenate([a["dmat"], b["dmat"]], axis=0)
        kw = k2.astype(F32) * pair(a["wn"], b["wn"])
        mm[d, p] = dict(
            s_sum=jnp.sum(s, axis=1, keepdims=True), kw_sum=jnp.sum(kw, axis=0, keepdims=True),
            sv=jnp.dot(s.astype(BF16), v2, preferred_element_type=F32),
            q_c=jnp.dot(q2, c_scr[d, p].astype(BF16), preferred_element_type=F32),
            kv=lax.dot_general(kw.astype(BF16), v2, (((0,), (0,)), ((), ())), preferred_element_type=F32))

    for d, p in jobs:
        a, b, r = st[d, 2 * p], st[d, 2 * p + 1], mm[d, p]
        q2 = refs[d][0][:, sl(p)]
        n_prev = n_scr[d, p]
        num = pair(a["inter"], b["inter"]) * r["q_c"] + jnp.where(lane_lo, r["sv"][:ch], r["sv"][ch:])
        qn = q2.astype(F32) * n_prev
        qn_a = jnp.sum(jnp.where(lane_lo, qn, 0.0), axis=1, keepdims=True)
        qn_b = jnp.sum(jnp.where(lane_lo, 0.0, qn), axis=1, keepdims=True)
        den_a = jnp.maximum(jnp.abs(a["inter"] * qn_a + r["s_sum"][:ch]), a["floor"])
        den_b = jnp.maximum(jnp.abs(b["inter"] * qn_b + r["s_sum"][ch:]), b["floor"])
        refs[d][4][:, sl(p)] = num / pair(den_a, den_b)
        c_scr[d, p] = (jnp.where(row_lo, a["decay"], b["decay"]) * c_scr[d, p]
                       + jnp.where(row_lo == col_lo, r["kv"], 0.0))
        n_scr[d, p] = jnp.where(vec_lo, a["decay"], b["decay"]) * n_prev + r["kw_sum"]
        m_scr[d, p] = jnp.where(vec_lo, a["m_new"], b["m_new"])


def _mlstm(q, k, v, gates, bias, b, lc, s):
    r, w = q.shape
    ch = _tile(math.gcd(lc, s), MLSTM_CHUNKS)
    ncc, ncx = lc // ch, s // ch
    base = b * ncc

    def fwd(i, c):
        return jnp.where(c < ncc, i * ncc + c, base + i * ncx + (c - ncc)), 0

    def bwd(i, c):
        return jnp.where(c < ncc, i * ncc + (ncc - 1 - c), base + i * ncx + (ncx - 1 - (c - ncc))), 0

    spec = lambda width, m: pl.BlockSpec((ch, width), m)
    npair = MLSTM_HEADS // 2
    return pl.pallas_call(
        _mlstm_kernel,
        out_shape=[jax.ShapeDtypeStruct((r, w), F32)] * 2,
        grid=(b, ncc + ncx),
        in_specs=[spec(w, fwd), spec(w, fwd), spec(w, fwd), spec(LANES, fwd),
                  spec(w, bwd), spec(w, bwd), spec(w, bwd), spec(LANES, bwd),
                  pl.BlockSpec((1, LANES), lambda i, c: (0, 0))],
        out_specs=[spec(w, fwd), spec(w, bwd)],
        scratch_shapes=[pltpu.VMEM((2, npair, 2 * HEAD_DIM, 2 * HEAD_DIM), F32),
                        pltpu.VMEM((2, npair, 1, 2 * HEAD_DIM), F32),
                        pltpu.VMEM((2, npair, 1, 2 * HEAD_DIM), F32)],
        compiler_params=_params("parallel", "arbitrary"),
        name="mlstm_scan",
    )(q, k, v, gates, q, k, v, gates, bias)


def _mlstm_out_kernel(hf_ref, hb_ref, o_ref, nrm_ref, ones_ref, y_ref):
    h = hf_ref[...] + hb_ref[...]
    ms = jnp.dot((h * h).astype(BF16), ones_ref[...], preferred_element_type=F32) * (1.0 / HEAD_DIM)
    hn = h * lax.rsqrt(ms + EPS) * nrm_ref[...]
    y_ref[...] = (jax.nn.sigmoid(o_ref[...].astype(F32)) * hn).astype(y_ref.dtype)


def _head_ones(width):
    idx = np.arange(width) // HEAD_DIM
    return jnp.asarray(idx[:, None] == idx[None, :], BF16)


def _mlstm_out(hf, hb, o, norm, tm):
    r, w = hf.shape
    blk = pl.BlockSpec((tm, w), lambda i: (i, 0))
    return pl.pallas_call(
        _mlstm_out_kernel,
        out_shape=jax.ShapeDtypeStruct((r, w), BF16),
        grid=(r // tm,),
        in_specs=[blk, blk, blk,
                  pl.BlockSpec((1, w), lambda i: (0, 0)),
                  pl.BlockSpec((w, w), lambda i: (0, 0))],
        out_specs=blk,
        compiler_params=_params("parallel"),
        name="mlstm_out",
    )(hf, hb, o, norm.reshape(1, w), _head_ones(w))


def _swa_kernel(q_ref, k0, k1, k2, k3, v0, v1, v2, v3, kc_ref, vc_ref, sink_ref, o_ref, *, seq):
    w, dh = SWA_BLOCK, HEAD_DIM
    tq = q_ref.shape[1]
    g = q_ref.shape[0] // dh
    i = pl.program_id(2)
    k_loc = jnp.concatenate([k0[...], k1[...], k2[...], k3[...]], axis=1)
    v_loc = jnp.concatenate([v0[...], v1[...], v2[...], v3[...]], axis=1)
    kpos = (2 * i - 1) * w + lax.broadcasted_iota(I32, (4 * w, tq), 0)
    qpos = i * tq + lax.broadcasted_iota(I32, (4 * w, tq), 1)
    valid = (jnp.abs(qpos - kpos) <= SWA_WINDOW) & (kpos >= 0) & (kpos < seq)
    t_dot = lambda kt, q: lax.dot_general(kt, q, (((0,), (0,)), ((), ())), preferred_element_type=F32)
    qs = [q_ref[h * dh:(h + 1) * dh, :] for h in range(g)]
    s_loc = [jnp.where(valid, t_dot(k_loc, qs[h]), NEG_INF) for h in range(g)]
    s_ctx = [t_dot(kc_ref[...], qs[h]) for h in range(g)]
    outs = []
    for h in range(g):
        sink = sink_ref[0, h:h + 1, :]
        m = jnp.maximum(jnp.maximum(jnp.max(s_loc[h], axis=0, keepdims=True),
                                    jnp.max(s_ctx[h], axis=0, keepdims=True)), sink)
        p_loc = jnp.exp2(s_loc[h] - m)
        p_ctx = jnp.exp2(s_ctx[h] - m)
        den = jnp.sum(p_loc, axis=0, keepdims=True) + jnp.sum(p_ctx, axis=0, keepdims=True) + jnp.exp2(sink - m)
        o = (jnp.dot(v_loc, p_loc.astype(BF16), preferred_element_type=F32)
             + jnp.dot(vc_ref[...], p_ctx.astype(BF16), preferred_element_type=F32))
        outs.append(o / den)
    o_ref[...] = jnp.concatenate(outs, axis=0).T.astype(o_ref.dtype)


def _swa(qt, kt, kvt, sink2, b, s, lc):
    dh = HEAD_DIM
    nh, hkv = qt.shape[0] // dh, kt.shape[0] // dh
    g = nh // hkv
    w = SWA_BLOCK
    tq = 2 * w
    nb, nq = s // w, s // tq
    c0 = b * lc // w
    clampi = lambda j: jnp.clip(j, 0, nb - 1)
    kspec = lambda o: pl.BlockSpec((dh, w), lambda bi, hi, i: (hi, bi * nb + clampi(2 * i + o)))
    vspec = lambda o: pl.BlockSpec((dh, w), lambda bi, hi, i: (hi, c0 + bi * nb + clampi(2 * i + o)))
    cspec = pl.BlockSpec((dh, lc), lambda bi, hi, i: (hi, bi))
    k_all, v_all = kvt
    return pl.pallas_call(
        functools.partial(_swa_kernel, seq=s),
        out_shape=jax.ShapeDtypeStruct((b * s, nh * dh), BF16),
        grid=(b, hkv, nq),
        in_specs=[pl.BlockSpec((g * dh, tq), lambda bi, hi, i: (hi, bi * nq + i)),
                  kspec(-1), kspec(0), kspec(1), kspec(2), vspec(-1), vspec(0), vspec(1), vspec(2),
                  cspec, cspec, pl.BlockSpec((1, g, tq), lambda bi, hi, i: (hi, 0, 0))],
        out_specs=pl.BlockSpec((tq, g * dh), lambda bi, hi, i: (bi * nq + i, hi)),
        compiler_params=_params("parallel", "parallel", "arbitrary"),
        name="window_attention",
    )(qt, kt, kt, kt, kt, v_all, v_all, v_all, v_all, k_all, v_all, sink2)


def _ctx_attn_kernel(q_ref, k_ref, v_ref, sink_ref, o_ref, *, use_sink):
    g, lq, dh = q_ref.shape[1:]
    q = q_ref[0].reshape(g * lq, dh)
    s = lax.dot_general(q, k_ref[0, 0], (((1,), (1,)), ((), ())), preferred_element_type=F32) * ATTN_SCALE
    m = jnp.max(s, axis=1, keepdims=True)
    if use_sink:
        m = jnp.maximum(m, sink_ref[0])
    p = jnp.exp(s - m)
    den = jnp.sum(p, axis=1, keepdims=True)
    if use_sink:
        den = den + jnp.exp(sink_ref[0] - m)
    o = jnp.dot(p.astype(BF16), v_ref[0, 0], preferred_element_type=F32) / den
    o_ref[0] = o.reshape(g, lq, dh).astype(o_ref.dtype)


def _ctx_attn(q, k, v, sink_col, use_sink):
    b, nh, lq, dh = q.shape
    hkv = k.shape[1]
    g = nh // hkv
    qblk = pl.BlockSpec((1, g, lq, dh), lambda bi, hi: (bi, hi, 0, 0))
    kblk = pl.BlockSpec((1, 1, k.shape[2], dh), lambda bi, hi: (bi, hi, 0, 0))
    return pl.pallas_call(
        functools.partial(_ctx_attn_kernel, use_sink=use_sink),
        out_shape=jax.ShapeDtypeStruct(q.shape, BF16),
        grid=(b, hkv),
        in_specs=[qblk, kblk, kblk, pl.BlockSpec((1, g * lq, 1), lambda bi, hi: (hi, 0, 0))],
        out_specs=qblk,
        compiler_params=_params("parallel", "parallel"),
        name="context_attention",
    )(q, k, v, sink_col)


def _flash_kernel(q_ref, k_ref, v_ref, o_ref, m_scr, acc_scr, *bufs, tk):
    dh = HEAD_DIM
    g = q_ref.shape[0] // dh
    n = k_ref.shape[2] // tk
    nb = len(bufs)
    depth = nb - 1
    m_scr[...] = jnp.full_like(m_scr, NEG_INF)
    acc_scr[...] = jnp.zeros_like(acc_scr)

    def scores(j, dst):
        kk = k_ref[0, 0, pl.ds(pl.multiple_of(j * tk, tk), tk), :]
        for h in range(g):
            dst[h] = jnp.dot(kk, q_ref[h * dh:(h + 1) * dh, :], preferred_element_type=F32)

    def update(j, src):
        vv = v_ref[0, 0, :, pl.ds(pl.multiple_of(j * tk, tk), tk)]
        ss = [src[h] for h in range(g)]
        m_olds = [m_scr[h] for h in range(g)]
        m_news = [jnp.maximum(m_olds[h], jnp.max(ss[h], axis=0, keepdims=True)) for h in range(g)]
        ps = [jnp.exp2(ss[h] - m_news[h]).astype(BF16) for h in range(g)]
        pvs = [jnp.dot(vv, ps[h], preferred_element_type=F32) for h in range(g)]
        for h in range(g):
            acc_scr[h] = jnp.exp2(m_olds[h] - m_news[h]) * acc_scr[h] + pvs[h]
            m_scr[h] = m_news[h]

    for j in range(min(depth, n)):
        scores(j, bufs[j % nb])
    loops = max(n - depth, 0) // nb

    def body(i, carry):
        for r in range(nb):
            scores(nb * i + r + depth, bufs[(r + depth) % nb])
            update(nb * i + r, bufs[r])
        return carry

    lax.fori_loop(0, loops, body, 0)
    for j in range(loops * nb, n):
        if j + depth < n:
            scores(j + depth, bufs[(j + depth) % nb])
        update(j, bufs[j % nb])
    acc = acc_scr[...]
    out = (acc[:, :dh] / acc[:, dh:dh + 1]).reshape(g * dh, -1)
    o_ref[...] = out.T.astype(o_ref.dtype)


def _flash(qt, k, vt, b, s, tq=None, tk=None):
    w = qt.shape[0]
    hkv, t, dh = k.shape[1], k.shape[2], k.shape[3]
    g = w // dh // hkv
    tq = tq or _tile(s, (256, 128))
    tk = tk or _tile(t, (256, 128))
    nq = s // tq
    ones_rows = 8
    vt = jnp.concatenate([vt, jnp.ones((b, hkv, ones_rows, t), vt.dtype)], axis=2)
    nv = dh + ones_rows
    return pl.pallas_call(
        functools.partial(_flash_kernel, tk=tk),
        out_shape=jax.ShapeDtypeStruct((b * s, w), BF16),
        grid=(b, hkv, nq),
        in_specs=[pl.BlockSpec((g * dh, tq), lambda bi, hi, i: (hi, bi * nq + i)),
                  pl.BlockSpec((1, 1, t, dh), lambda bi, hi, i: (bi, hi, 0, 0)),
                  pl.BlockSpec((1, 1, nv, t), lambda bi, hi, i: (bi, hi, 0, 0))],
        out_specs=pl.BlockSpec((tq, g * dh), lambda bi, hi, i: (bi * nq + i, hi)),
        scratch_shapes=([pltpu.VMEM((g, 1, tq), F32), pltpu.VMEM((g, nv, tq), F32)]
                        + [pltpu.VMEM((g, tk, tq), F32)] * (FLASH_DEPTH + 1)),
        compiler_params=_params("parallel", "parallel", "arbitrary"),
        name="dense_attention",
    )(qt, k, vt)


def _bmm_kernel(a_ref, b_ref, o_ref):
    o_ref[0] = jnp.dot(a_ref[0], b_ref[0], preferred_element_type=F32).astype(o_ref.dtype)


def _group_matmul(a, bmat, name):
    g, m, k = a.shape
    n = bmat.shape[2]
    tm = _tile(m, (1024, 512, 256, 128, 64, 32, 16, 8))
    return pl.pallas_call(
        _bmm_kernel,
        out_shape=jax.ShapeDtypeStruct((g, m, n), F32),
        grid=(g, m // tm),
        in_specs=[pl.BlockSpec((1, tm, k), lambda gi, i: (gi, i, 0)),
                  pl.BlockSpec((1, k, n), lambda gi, i: (gi, 0, 0))],
        out_specs=pl.BlockSpec((1, tm, n), lambda gi, i: (gi, i, 0)),
        compiler_params=_params("parallel", "parallel"),
        name=name,
    )(a, bmat)


def _s5_scan_kernel(sre_ref, sim_ref, are_ref, aim_ref, zre_ref, zim_ref):
    nd, nsteps = sre_ref.shape[0], sre_ref.shape[1]
    for d in range(nd):
        a_re, a_im = are_ref[d], aim_ref[d]

        def body(i, carry, d=d, a_re=a_re, a_im=a_im):
            z_re, z_im = carry
            zre_ref[d, i] = z_re
            zim_ref[d, i] = z_im
            return (a_re * z_re - a_im * z_im + sre_ref[d, i], a_re * z_im + a_im * z_re + sim_ref[d, i])

        zero = jnp.zeros(sre_ref.shape[2:], F32)
        lax.fori_loop(0, nsteps, body, (zero, zero))


def _s5_scan(s_re, s_im, a_re, a_im):
    full = lambda arr: pl.BlockSpec(arr.shape, lambda i: (0,) * arr.ndim)
    return pl.pallas_call(
        _s5_scan_kernel,
        out_shape=[jax.ShapeDtypeStruct(s_re.shape, F32)] * 2,
        grid=(1,),
        in_specs=[full(s_re), full(s_im), full(a_re), full(a_im)],
        out_specs=[full(s_re), full(s_im)],
        compiler_params=_params("arbitrary"),
        name="s5_chunk_scan",
    )(s_re, s_im, a_re, a_im)


def _s5_glu_kernel(y_ref, u_ref, d_ref, w_ref, b_ref, o_ref):
    y = y_ref[...] + d_ref[...] * u_ref[...]
    y = jax.nn.gelu(y)
    gate = jnp.dot(y.astype(BF16), w_ref[...], preferred_element_type=F32) + b_ref[...]
    o_ref[...] = (y * jax.nn.sigmoid(gate)).astype(o_ref.dtype)


def _s5_glu(y, u, d_skip, w_glu, b_glu, tm):
    r, c = y.shape
    blk = pl.BlockSpec((tm, c), lambda i: (i, 0))
    vec = pl.BlockSpec((1, c), lambda i: (0, 0))
    return pl.pallas_call(
        _s5_glu_kernel,
        out_shape=jax.ShapeDtypeStruct((r, c), BF16),
        grid=(r // tm,),
        in_specs=[blk, blk, vec, pl.BlockSpec((c, c), lambda i: (0, 0)), vec],
        out_specs=blk,
        compiler_params=_params("parallel"),
        name="s5_readout_glu",
    )(y, u, d_skip.reshape(1, c), w_glu.astype(BF16), b_glu.reshape(1, c))


def _s5_tables(lam_re, lam_im, log_dt, b_re, b_im, c_re, c_im):
    ln, p, gc = S5_CHUNK, S5_STATE, S5_GROUP
    lam = lax.complex(lam_re.astype(F32), lam_im.astype(F32))
    dt = jnp.exp(log_dt.astype(F32))[..., None]
    a_bar = jnp.exp(lam * dt)
    b_scale = (a_bar - 1.0) / lam
    b_mat = lax.complex(b_re.astype(F32), b_im.astype(F32))
    c_mat = lax.complex(c_re.astype(F32), c_im.astype(F32))
    tau = jnp.arange(ln + 1, dtype=F32)
    apow = jnp.exp((lam * dt)[:, :, None, :] * tau[None, None, :, None])
    drive = b_scale[..., None] * b_mat[None]
    kern = jnp.real(jnp.einsum('gcp,dgtp,dgpe->dgtce', c_mat, apow[:, :, :ln], drive))
    kc = jnp.concatenate([kern[1, :, :0:-1], (kern[0, :, :1] + kern[1, :, :1]), kern[0, :, 1:]], axis=1)
    kc_e = kc.transpose(0, 3, 1, 2).reshape(S5_GROUPS, gc, (2 * ln - 1) * gc)
    toep = jnp.stack([kc_e[:, :, (ln - 1 - s) * gc:(2 * ln - 1 - s) * gc] for s in range(ln)], axis=1)
    toep = toep.reshape(S5_GROUPS, ln * gc, ln * gc)
    w_f = apow[0, :, ln - 1::-1][:, :ln, :, None] * drive[0][:, None]
    w_b = apow[1, :, :ln, :, None] * drive[1][:, None]
    def m_in(wc):
        wt = wc.transpose(0, 1, 3, 2).reshape(S5_GROUPS, ln * gc, p)
        return jnp.concatenate([jnp.real(wt), jnp.imag(wt)], axis=-1)
    min_all = jnp.concatenate([m_in(w_f), m_in(w_b)], axis=-1)
    o_f = c_mat[:, None] * apow[0, :, 1:ln + 1][:, :, None, :]
    o_b = c_mat[:, None] * apow[1, :, ln:0:-1][:, :, None, :]
    def m_out(oc):
        ot = oc.transpose(0, 3, 1, 2).reshape(S5_GROUPS, p, ln * gc)
        return jnp.concatenate([jnp.real(ot), -jnp.imag(ot)], axis=1)
    rhs = jnp.concatenate([toep, m_out(o_f), m_out(o_b)], axis=1)
    a_l = apow[:, :, ln].reshape(2, 1, S5_GROUPS * p)
    return min_all.astype(BF16), rhs.astype(BF16), jnp.real(a_l), jnp.imag(a_l)


def _s5_mixer(u_x, u_c, tables, d_skip, w_glu, b_glu):
    min_all, rhs, a_re, a_im = tables
    b, s, _ = u_x.shape
    ln, p, gc, ng = S5_CHUNK, S5_STATE, S5_GROUP, S5_GROUPS

    def chunks(u):
        nc = u.shape[1] // ln
        return u.reshape(b, nc, ln, ng, gc).transpose(3, 0, 1, 2, 4).reshape(ng, b * nc, ln * gc).astype(BF16), nc

    ux, ncx = chunks(u_x)
    uc, ncc = chunks(u_c)
    sx = _group_matmul(ux, min_all, "s5_local_state").reshape(ng, b, ncx, 4, p)
    sc = _group_matmul(uc, min_all, "s5_local_state_ctx").reshape(ng, b, ncc, 4, p)

    def scan_order(part):
        f = jnp.concatenate([sc[:, :, :, part], sx[:, :, :, part]], axis=2)
        r = jnp.concatenate([sc[:, :, ::-1, part + 2], sx[:, :, ::-1, part + 2]], axis=2)
        return jnp.stack([f, r]).transpose(0, 3, 2, 1, 4).reshape(2, ncc + ncx, b, ng * p)

    z_re, z_im = _s5_scan(scan_order(0), scan_order(1), a_re, a_im)

    def latent(z, d):
        zl = z[d, ncc:]
        if d == 1:
            zl = zl[::-1]
        return zl.reshape(ncx, b, ng, p).transpose(2, 1, 0, 3).reshape(ng, b * ncx, p)

    lhs = jnp.concatenate([ux, latent(z_re, 0).astype(BF16), latent(z_im, 0).astype(BF16),
                           latent(z_re, 1).astype(BF16), latent(z_im, 1).astype(BF16)], axis=-1)
    y = _group_matmul(lhs, rhs, "s5_outputs")
    y = y.reshape(ng, b, ncx, ln, gc).transpose(1, 2, 3, 0, 4).reshape(b * s, S5_CHANNELS)
    return _s5_glu(y, u_x.reshape(b * s, S5_CHANNELS), d_skip, w_glu, b_glu, _tile(b * s, (1024, 512, 256)))


def _out_proj_kernel(x_ref, a_ref, b_ref, wa_ref, wb_ref, gt_ref, g_ref, sc_ref, sh_ref, wr_ref, br_ref,
                     xo_ref, xn_ref, lg_ref):
    y = (jnp.dot(a_ref[...], wa_ref[...], preferred_element_type=F32)
         + jnp.dot(b_ref[...], wb_ref[...], preferred_element_type=F32))
    x = x_ref[...] + gt_ref[0] * y
    xo_ref[...] = x
    xn = x * lax.rsqrt(jnp.mean(x * x, axis=-1, keepdims=True) + EPS) * g_ref[...]
    xn = xn * (1.0 + sc_ref[0]) + sh_ref[0]
    hi = xn.astype(BF16)
    xn_ref[...] = hi
    lo = (xn - hi.astype(F32)).astype(BF16)
    part = jnp.dot(hi, wr_ref[...], preferred_element_type=F32)
    part = part[:, :LANES] + part[:, LANES:] + jnp.dot(lo, wr_ref[:, :LANES], preferred_element_type=F32)
    lg_ref[...] = part[:, :lg_ref.shape[1]] + br_ref[...]


def _out_proj(x, row0, a, bmix, wa, wb, gate, g_ffn, scale, shift, mod_map, w_router, b_router, tm):
    r = a.shape[0]
    d = x.shape[1]
    t0 = row0 // tm
    ne = w_router.shape[1]
    w_hi = w_router.astype(BF16)
    w_lo = (w_router - w_hi.astype(F32)).astype(BF16)
    w_router = jnp.concatenate([_pad_cols(w_hi), _pad_cols(w_lo)], axis=1)
    row = lambda w: pl.BlockSpec((tm, w), lambda i: (i, 0))
    mod = pl.BlockSpec((1, 1, d), lambda i: (mod_map(i), 0, 0))
    full = lambda arr: pl.BlockSpec(arr.shape, lambda i: (0, 0))
    return pl.pallas_call(
        _out_proj_kernel,
        out_shape=[jax.ShapeDtypeStruct((r, d), F32), jax.ShapeDtypeStruct((r, d), BF16),
                   jax.ShapeDtypeStruct((r, ne), F32)],
        grid=(r // tm,),
        in_specs=[pl.BlockSpec((tm, d), lambda i: (i + t0, 0)), row(a.shape[1]), row(bmix.shape[1]),
                  full(wa), full(wb), mod, pl.BlockSpec((1, d), lambda i: (0, 0)), mod, mod,
                  full(w_router), pl.BlockSpec((1, ne), lambda i: (0, 0))],
        out_specs=[row(d), row(d), row(ne)],
        compiler_params=_params("parallel"),
        name="out_proj",
    )(x, a, bmix, wa, wb, gate, g_ffn.reshape(1, d), scale, shift, w_router, b_router.reshape(1, ne))


def _route_kernel(lg_ref, idx_ref, gate_ref, pos_ref, cnt_ref, carry):
    tm, ne = lg_ref.shape

    @pl.when(pl.program_id(0) == 0)
    def _():
        carry[...] = jnp.zeros_like(carry)

    work = lg_ref[...]
    lane = lax.broadcasted_iota(I32, (tm, ne), 1).astype(F32)
    out_lane = lax.broadcasted_iota(I32, (tm, LANES), 1)
    vals, hots, idx_out = [], [], jnp.zeros((tm, LANES), I32)
    for kk in range(TOP_K):
        mx = jnp.max(work, axis=1, keepdims=True)
        idx = jnp.min(jnp.where(work == mx, lane, float(ne)), axis=1, keepdims=True)
        hot = lane == idx
        work = jnp.where(hot, -jnp.inf, work)
        vals.append(mx)
        hots.append(hot)
        idx_out = jnp.where(out_lane == kk, idx.astype(I32), idx_out)
    exps = [jnp.exp(vv - vals[0]) for vv in vals]
    tot = exps[0] + exps[1] + exps[2] + exps[3]
    multi = (hots[0] | hots[1] | hots[2] | hots[3]).astype(BF16)
    r_i = lax.broadcasted_iota(I32, (tm, tm), 0)
    c_i = lax.broadcasted_iota(I32, (tm, tm), 1)
    before = (r_i > c_i).astype(BF16)
    prefix = jnp.dot(before, multi, preferred_element_type=F32) + carry[...]
    gate_out = jnp.zeros((tm, LANES), F32)
    pos_out = jnp.zeros((tm, LANES), I32)
    for kk in range(TOP_K):
        gate_out = jnp.where(out_lane == kk, exps[kk] / tot, gate_out)
        pos = jnp.sum(jnp.where(hots[kk], prefix, 0.0), axis=1, keepdims=True).astype(I32)
        pos_out = jnp.where(out_lane == kk, pos, pos_out)
    idx_ref[...] = idx_out
    gate_ref[...] = gate_out
    pos_ref[...] = pos_out
    carry[...] = carry[...] + jnp.sum(multi.astype(F32), axis=0, keepdims=True)
    cnt_ref[...] = carry[...]


def _route(logits, tm, row0, n):
    ne = logits.shape[1]
    t0 = row0 // tm
    wide = pl.BlockSpec((tm, LANES), lambda i: (i, 0))
    return pl.pallas_call(
        _route_kernel,
        out_shape=[jax.ShapeDtypeStruct((n, LANES), I32), jax.ShapeDtypeStruct((n, LANES), F32),
                   jax.ShapeDtypeStruct((n, LANES), I32), jax.ShapeDtypeStruct((1, ne), F32)],
        grid=(n // tm,),
        in_specs=[pl.BlockSpec((tm, ne), lambda i: (i + t0, 0))],
        out_specs=[wide, wide, wide, pl.BlockSpec((1, ne), lambda i: (0, 0))],
        scratch_shapes=[pltpu.VMEM((1, ne), F32)],
        compiler_params=_params("arbitrary"),
        name="moe_route",
    )(logits)


def _expert_kernel(be_ref, nb_ref, x_ref, wgu_ref, bgu_ref, wd_ref, bd_ref, perm_ref, o_ref, wgu_bf, wd_bf):
    i = pl.program_id(0)
    fresh = jnp.logical_or(i == 0, be_ref[i] != be_ref[jnp.maximum(i - 1, 0)])

    @pl.when(jnp.logical_and(i < nb_ref[0], fresh))
    def _():
        tile = perm_ref.shape[0]
        for t in range(wgu_ref.shape[2] // tile):
            sl = slice(t * tile, (t + 1) * tile)
            wgu_bf[:, sl] = jnp.dot(wgu_ref[0, :, sl].astype(BF16), perm_ref[...],
                                    preferred_element_type=F32).astype(BF16)
        wd_bf[...] = wd_ref[0].astype(BF16)

    @pl.when(i < nb_ref[0])
    def _():
        h = jnp.dot(x_ref[...], wgu_bf[...], preferred_element_type=F32) + bgu_ref[0]
        acts = []
        for t in range(h.shape[1] // (2 * LANES)):
            glu = jnp.minimum(h[:, 2 * LANES * t:2 * LANES * t + LANES], SWIGLU_LIMIT)
            lin = jnp.clip(h[:, 2 * LANES * t + LANES:2 * LANES * (t + 1)], -SWIGLU_LIMIT, SWIGLU_LIMIT)
            acts.append((glu * jax.nn.sigmoid(SWIGLU_ALPHA * glu) * (lin + 1.0)).astype(BF16))
        act = jnp.concatenate(acts, axis=1)
        o_ref[...] = (jnp.dot(act, wd_bf[...], preferred_element_type=F32) + bd_ref[0]).astype(o_ref.dtype)

    @pl.when(i >= nb_ref[0])
    def _():
        o_ref[...] = jnp.zeros_like(o_ref)


def _experts(xs, blk_e, n_used, wgu, bgu, wd, bd):
    n, d = xs.shape
    ff2 = wgu.shape[2]
    nblk = n // MOE_ROWS
    tile = 2 * LANES
    j = np.arange(tile)
    col = np.where(j < LANES, 2 * j, 2 * (j - LANES) + 1)
    perm = jnp.asarray(np.arange(tile)[:, None] == col[None, :], BF16)
    wspec = lambda shape: pl.BlockSpec((1,) + shape, lambda i, be, nb: (be[i], 0, 0))
    return pl.pallas_call(
        _expert_kernel,
        out_shape=jax.ShapeDtypeStruct((n, d), BF16),
        grid_spec=pltpu.PrefetchScalarGridSpec(
            num_scalar_prefetch=2,
            grid=(nblk,),
            in_specs=[pl.BlockSpec((MOE_ROWS, d), lambda i, be, nb: (i, 0)),
                      wspec((d, ff2)), wspec((1, ff2)), wspec((ff2 // 2, d)), wspec((1, d)),
                      pl.BlockSpec((tile, tile), lambda i, be, nb: (0, 0))],
            out_specs=pl.BlockSpec((MOE_ROWS, d), lambda i, be, nb: (i, 0)),
            scratch_shapes=[pltpu.VMEM((d, ff2), BF16), pltpu.VMEM((ff2 // 2, d), BF16)]),
        compiler_params=_params("arbitrary"),
        name="moe_experts",
    )(blk_e, n_used, xs, wgu, bgu, wd, bd, perm)


def _combine_kernel(x_ref, y_ref, gate_ref, gt_ref, g_ref, o_ref, *, final_norm):
    gates = gate_ref[...]
    y = gates[:, 0:1] * y_ref[0].astype(F32)
    for kk in range(1, TOP_K):
        y = y + gates[:, kk:kk + 1] * y_ref[kk].astype(F32)
    x = x_ref[...] + gt_ref[0] * y
    if final_norm:
        x = x * lax.rsqrt(jnp.mean(x * x, axis=-1, keepdims=True) + EPS) * g_ref[...]
    o_ref[...] = x


def _combine_alias_kernel(x_ref, y_ref, gate_ref, gt_ref, g_ref, prev_ref, o_ref, *, final_norm):
    del prev_ref
    _combine_kernel(x_ref, y_ref, gate_ref, gt_ref, g_ref, o_ref, final_norm=final_norm)


def _combine(x, y4, gates, gate_mod, mod_map, g_final, final_norm, tm, row0, prev):
    r, d = x.shape
    n = y4.shape[1]
    t0 = row0 // tm
    in_specs = [pl.BlockSpec((tm, d), lambda i: (i + t0, 0)),
                pl.BlockSpec((TOP_K, tm, d), lambda i: (0, i, 0)),
                pl.BlockSpec((tm, LANES), lambda i: (i, 0)),
                pl.BlockSpec((1, 1, d), lambda i: (mod_map(i + t0), 0, 0)),
                pl.BlockSpec((1, d), lambda i: (0, 0))]
    args = [x, y4, gates, gate_mod, g_final.reshape(1, d)]
    body, aliases = _combine_kernel, {}
    if prev is not None:
        in_specs.append(pl.BlockSpec(memory_space=pl.ANY))
        args.append(prev)
        body, aliases = _combine_alias_kernel, {len(args) - 1: 0}
    return pl.pallas_call(
        functools.partial(body, final_norm=final_norm),
        out_shape=jax.ShapeDtypeStruct((r, d), F32),
        grid=(n // tm,),
        in_specs=in_specs,
        out_specs=pl.BlockSpec((tm, d), lambda i: (i + t0, 0)),
        input_output_aliases=aliases,
        compiler_params=_params("parallel"),
        name="moe_combine",
    )(*args)


def _moe_parts(tiles):
    total = sum(MOE_SPLIT)
    sizes = [tiles * w // total for w in MOE_SPLIT[:-1]]
    sizes.append(tiles - sum(sizes))
    return [sz for sz in sizes if sz > 0]


def _moe(x, xn, logits, gate_mod, mod_map, weights, g_final, final_norm, tm):
    wgu, bgu, wd, bd = weights
    r, d = xn.shape
    routed, row0 = [], 0
    for tiles in _moe_parts(r // tm):
        n = tiles * tm
        nblk = n * TOP_K // MOE_ROWS + N_EXPERTS
        blk_row0 = jnp.arange(nblk, dtype=I32) * MOE_ROWS
        idx_w, gates_w, pos_w, counts = _route(logits, tm, row0, n)
        idx, pos = idx_w[:, :TOP_K], pos_w[:, :TOP_K]
        counts = counts[0].astype(I32)
        padded = (counts + MOE_ROWS - 1) // MOE_ROWS * MOE_ROWS
        pad_end = jnp.cumsum(padded)
        pad_start = pad_end - padded
        dest = pad_start[idx] + pos
        blk_e = jnp.minimum(jnp.sum((pad_end[None, :] <= blk_row0[:, None]).astype(I32), axis=1), N_EXPERTS - 1)
        n_used = (pad_end[-1:] // MOE_ROWS).astype(I32)
        key = idx * n + jnp.arange(n, dtype=I32)[:, None]
        tok_sorted = jnp.sort(key.reshape(-1)) % n
        grp_start = jnp.cumsum(counts) - counts
        slot = blk_row0[:, None] + jnp.arange(MOE_ROWS, dtype=I32)[None, :]
        compact = jnp.minimum(slot + (grp_start - pad_start)[blk_e][:, None],
                              (grp_start + counts - 1)[blk_e][:, None])
        src = tok_sorted.at[jnp.clip(compact, 0, n * TOP_K - 1).reshape(-1)].get(mode="promise_in_bounds")
        xs = xn.at[src + row0].get(mode="promise_in_bounds")
        routed.append((xs, blk_e, n_used, dest, gates_w, row0, n))
        row0 += n
    ys = [_experts(xs, blk_e, n_used, wgu, bgu, wd, bd) for xs, blk_e, n_used, _, _, _, _ in routed]
    out = None
    for p, (_, _, _, dest, gates_w, row0, n) in enumerate(routed):
        y4 = ys[p].at[dest.T.reshape(-1)].get(mode="promise_in_bounds").reshape(TOP_K, n, d)
        out = _combine(x, y4, gates_w, gate_mod, mod_map, g_final, final_norm, tm, row0, out)
    return out


def _moe_weights(w_gate_up, b_gate_up, w_down, b_down):
    ne, d, ff2 = w_gate_up.shape
    b_tiled = b_gate_up.reshape(ne, ff2 // (2 * LANES), LANES, 2).swapaxes(-1, -2).reshape(ne, 1, ff2)
    return w_gate_up, b_tiled, w_down, b_down[:, None, :]


def _rope_tables(n_tokens):
    rows = n_tokens // GRID_W
    row = jnp.repeat(jnp.arange(rows, dtype=I32), GRID_W).astype(F32)
    col = jnp.tile(jnp.arange(GRID_W, dtype=I32), rows).astype(F32)
    inv = ROPE_BASE ** (-jnp.arange(0, ROPE_AXIS_DIM, 2, dtype=F32) / ROPE_AXIS_DIM)
    ang_r, ang_c = row[:, None] * inv, col[:, None] * inv
    cos = jnp.concatenate([jnp.cos(ang_r)] * 2 + [jnp.cos(ang_c)] * 2, axis=-1)
    sin = jnp.concatenate([jnp.sin(ang_r)] * 2 + [jnp.sin(ang_c)] * 2, axis=-1)
    return cos, sin


def _rot_perm():
    q = ROPE_AXIS_DIM // 2
    d = np.arange(HEAD_DIM)
    first = (d % ROPE_AXIS_DIM) < q
    perm = np.where(first, d + q, d - q)
    sign = np.where(first, -1.0, 1.0).astype(np.float32)
    return perm, sign


def _rot_cols(w, n_heads):
    perm, sign = _rot_perm()
    k = w.shape[0]
    wh = w.reshape(k, n_heads, HEAD_DIM)
    return (wh[:, :, perm] * sign).reshape(k, n_heads * HEAD_DIM)


def _to_heads(t, b, n_heads):
    return t.reshape(b, -1, n_heads, HEAD_DIM).transpose(0, 2, 1, 3)


def _from_heads(t):
    b, h, tt, dh = t.shape
    return t.transpose(0, 2, 1, 3).reshape(b * tt, h * dh)


def _layer_even(xa, mod, b, s, lc, g_mix, g_ffn, w_in, w_out, gate_bias, mlstm_norm, sink, router, moe_w):
    d = xa.shape[1]
    nctx = b * lc
    tm = _tile(math.gcd(nctx, s), (512, 256, 128))
    mod_map = lambda i: jnp.where(i * tm < nctx, b, (i * tm - nctx) // s)
    sh1, sc1, gt1, sh2, sc2, gt2 = [mod[:, j][:, None, :] for j in range(6)]

    hm, hs, hk = MLSTM_HEADS * HEAD_DIM, SWA_HEADS * HEAD_DIM, SWA_KV_HEADS * HEAD_DIM
    o0 = np.cumsum([0, hm, hm, hm, hm, 4 * MLSTM_HEADS, hs, hk, hk])
    seg = lambda j: w_in[:, o0[j]:o0[j + 1]]
    w_cat = jnp.concatenate([seg(0), seg(1) * ATTN_SCALE, seg(2), seg(3), _pad_cols(seg(4))], axis=1).astype(BF16)
    w_t = jnp.concatenate([seg(5), _rot_cols(seg(5), SWA_HEADS), seg(6), _rot_cols(seg(6), SWA_KV_HEADS), seg(7)],
                          axis=1).T.astype(BF16)
    widths = [hm, hm, hm, hm, LANES]
    dts = [BF16] * 4 + [F32]
    qa, ka, va, oa, gts, qt, qt_r, kt, kt_r, vt = _in_proj(
        xa, g_mix, sc1, sh1, mod_map, w_cat, widths, dts, tm,
        wt=w_t, t_widths=[hs, hs, hk, hk, hk], t_dtypes=[BF16] * 5)

    hf, hb = _mlstm(qa, ka, va, gts, _pad_cols(gate_bias.astype(F32).reshape(1, -1)), b, lc, s)
    mix_a = _mlstm_out(hf, hb, oa, mlstm_norm.reshape(-1), tm)

    cos, sin = _rope_tables(s)
    ones = jnp.ones((HEAD_DIM,), F32)
    g = SWA_HEADS // SWA_KV_HEADS
    heads = lambda t, n: _to_heads(t, b, n)
    log2e = math.log2(math.e)
    q_x = _q_prep_t(qt, qt_r, nctx, b * s, cos.T, sin.T, ones, ones, ATTN_SCALE * log2e, norm=False)
    k_x = _q_prep_t(kt, kt_r, nctx, b * s, cos.T, sin.T, ones, ones, 1.0, norm=False)
    k_c, v_c = heads(kt[:, :nctx].T, SWA_KV_HEADS), heads(vt[:, :nctx].T, SWA_KV_HEADS)
    q_c = heads(qt[:, :nctx].T, SWA_HEADS)
    sink_h = sink.astype(F32).reshape(SWA_KV_HEADS, g, 1)
    att_x = _swa(q_x, k_x, (kt, vt), jnp.broadcast_to(sink_h * log2e, (SWA_KV_HEADS, g, 2 * SWA_BLOCK)), b, s, lc)
    att_c = _ctx_attn(q_c, k_c, v_c, jnp.broadcast_to(sink_h[..., None], (SWA_KV_HEADS, g, lc, 1))
                      .reshape(SWA_KV_HEADS, g * lc, 1), True)
    mix_b = jnp.concatenate([_from_heads(att_c), att_x], axis=0)

    w_router, b_router = router
    xa, xn2, logits = _out_proj(xa, 0, mix_a, mix_b, w_out[:hm].astype(BF16), w_out[hm:].astype(BF16), gt1, g_ffn,
                                sc2, sh2, mod_map, w_router.astype(F32), b_router.astype(F32), tm)
    return _moe(xa, xn2, logits, gt2, mod_map, moe_w, g_ffn, False, tm)


def _layer_odd_last(xa, mod, b, s, lc, g_mix, g_ffn, w_in, w_out, s5_params, d_skip, w_glu, b_glu,
                    q_norm, k_norm, router, moe_w, g_final):
    nctx = b * lc
    tm = _tile(math.gcd(nctx, s), (512, 256, 128))
    mod_map = lambda i: jnp.where(i * tm < nctx, b, (i * tm - nctx) // s)
    lat_map = lambda i: i * tm // s
    sh1, sc1, gt1, sh2, sc2, gt2 = [mod[:, j][:, None, :] for j in range(6)]

    hq, hk = ATT_HEADS * HEAD_DIM, ATT_KV_HEADS * HEAD_DIM
    o1 = np.cumsum([0, S5_CHANNELS, hq, hk, hk])
    seg = lambda j: w_in[:, o1[j]:o1[j + 1]]
    kpad = lambda w: _pad_cols(w, 2 * LANES)
    w_cat = jnp.concatenate([seg(0), kpad(seg(2)), kpad(_rot_cols(seg(2), ATT_KV_HEADS)), kpad(seg(3))],
                            axis=1).astype(BF16)
    w_q = jnp.concatenate([seg(1), _rot_cols(seg(1), ATT_HEADS)], axis=1).T.astype(BF16)
    widths = [S5_CHANNELS, 2 * LANES, 2 * LANES, 2 * LANES]
    dts = [F32, BF16, BF16, BF16]
    u, k, k_r, v, qt, qt_r = _in_proj(xa, g_mix, sc1, sh1, mod_map, w_cat, widths, dts, tm,
                                      wt=w_q, t_widths=[hq, hq], t_dtypes=[BF16, BF16])

    mix_a = _s5_mixer(u[nctx:].reshape(b, s, S5_CHANNELS), u[:nctx].reshape(b, lc, S5_CHANNELS),
                      _s5_tables(*s5_params), d_skip, w_glu, b_glu)

    cos, sin = _rope_tables(s)
    perm, _ = _rot_perm()
    qn, kn = q_norm.astype(F32), k_norm.astype(F32)
    heads = lambda t, n: _to_heads(t[:, :n * HEAD_DIM], b, n)
    q_x = _q_prep_t(qt, qt_r, nctx, b * s, cos.T, sin.T, qn, qn[perm], ATTN_SCALE * math.log2(math.e))
    k_x = _qk_prep(heads(k[nctx:], ATT_KV_HEADS), heads(k_r[nctx:], ATT_KV_HEADS), cos, sin, kn, kn[perm],
                   norm=True, rope=True, scale=1.0)
    kc_raw = heads(k[:nctx], ATT_KV_HEADS)
    k_c = _qk_prep(kc_raw, kc_raw, cos[:lc], sin[:lc], kn, kn, norm=True, rope=False, scale=1.0)
    k_all = jnp.concatenate([k_c, k_x], axis=2)
    v_all = jnp.concatenate([heads(v[:nctx], ATT_KV_HEADS), heads(v[nctx:], ATT_KV_HEADS)], axis=2)
    mix_b = _flash(q_x, k_all, v_all.swapaxes(-1, -2), b, s)

    w_router, b_router = router
    hs = S5_CHANNELS
    x, xn2, logits = _out_proj(xa, nctx, mix_a, mix_b, w_out[:hs].astype(BF16), w_out[hs:].astype(BF16), gt1, g_ffn,
                               sc2, sh2, lat_map, w_router.astype(F32), b_router.astype(F32), tm)
    return _moe(x, xn2, logits, gt2, lat_map, moe_w, g_final, True, tm)


def kernel(x, c, ctx, c_ctx, l0_w_mod, l0_b_mod, l0_g_mix, l0_g_ffn, l0_w_in, l0_w_out, l0_gate_bias, l0_mlstm_norm, l0_sink, l0_w_router, l0_b_router, l0_w_gate_up, l0_b_gate_up, l0_w_down, l0_b_down, l1_w_mod, l1_b_mod, l1_g_mix, l1_g_ffn, l1_w_in, l1_w_out, l1_lam_re, l1_lam_im, l1_log_dt, l1_b_re, l1_b_im, l1_c_re, l1_c_im, l1_d_skip, l1_w_glu, l1_b_glu, l1_q_norm, l1_k_norm, l1_w_router, l1_b_router, l1_w_gate_up, l1_b_gate_up, l1_w_down, l1_b_down, g_final):
    b, s, d = x.shape
    lc = ctx.shape[1]
    cond = jnp.concatenate([c, c_ctx[None, :]], axis=0)
    cond = jnp.pad(cond, ((0, (-(b + 1)) % 8), (0, 0)))
    mod0 = _silu_linear(cond, l0_w_mod, l0_b_mod)[:b + 1].reshape(b + 1, 6, d)
    mod1 = _silu_linear(cond, l1_w_mod, l1_b_mod)[:b + 1].reshape(b + 1, 6, d)

    moe0 = _moe_weights(l0_w_gate_up, l0_b_gate_up, l0_w_down, l0_b_down)
    moe1 = _moe_weights(l1_w_gate_up, l1_b_gate_up, l1_w_down, l1_b_down)
    xa = jnp.concatenate([ctx.reshape(b * lc, d), x.reshape(b * s, d)], axis=0)
    xa = _layer_even(xa, mod0, b, s, lc, l0_g_mix, l0_g_ffn, l0_w_in, l0_w_out, l0_gate_bias, l0_mlstm_norm,
                     l0_sink, (l0_w_router, l0_b_router), moe0)
    out = _layer_odd_last(xa, mod1, b, s, lc, l1_g_mix, l1_g_ffn, l1_w_in, l1_w_out,
                          (l1_lam_re, l1_lam_im, l1_log_dt, l1_b_re, l1_b_im, l1_c_re, l1_c_im),
                          l1_d_skip, l1_w_glu, l1_b_glu, l1_q_norm, l1_k_norm, (l1_w_router, l1_b_router),
                          moe1, g_final)
    return out.reshape(b, s, d)
```

```python
import functools
import math

import jax
import jax.numpy as jnp
import numpy as np
from jax import lax
from jax.experimental import pallas as pl
from jax.experimental.pallas import tpu as pltpu

F32 = jnp.float32
BF16 = jnp.bfloat16
I32 = jnp.int32

GRID_W = 64
HEAD_DIM = 64
ATTN_SCALE = HEAD_DIM ** -0.5
ROPE_AXIS_DIM = HEAD_DIM // 2
ROPE_BASE = 10000.0
EPS = 1e-6
NEG_INF = -1e30

MLSTM_HEADS = 8
MLSTM_CHUNKS = (256, 128, 64)
SWA_HEADS = 8
SWA_KV_HEADS = 2
SWA_WINDOW = 128
SWA_BLOCK = 128
S5_CHANNELS = 256
S5_GROUP = 16
S5_GROUPS = S5_CHANNELS // S5_GROUP
S5_STATE = 64
S5_CHUNK = 64
ATT_HEADS = 12
ATT_KV_HEADS = 3
N_EXPERTS = 32
TOP_K = 4
SWIGLU_LIMIT = 7.0
SWIGLU_ALPHA = 1.702

LANES = 128
VMEM_LIMIT = 56 * 1024 * 1024
MOE_ROWS = 512
MOE_SPLIT = (1, 1)
FLASH_DEPTH = 6
HIGHEST = lax.Precision.HIGHEST


def _params(*sem):
    return pltpu.CompilerParams(dimension_semantics=sem, vmem_limit_bytes=VMEM_LIMIT)


def _tile(n, prefs):
    for t in prefs:
        if n % t == 0:
            return t
    return n


def _pad_cols(w, mult=LANES):
    pad = (-w.shape[-1]) % mult
    if pad:
        w = jnp.pad(w, [(0, 0)] * (w.ndim - 1) + [(0, pad)])
    return w


def _linear_kernel(x_ref, w_ref, b_ref, o_ref):
    x = x_ref[...]
    x = x * jax.nn.sigmoid(x)
    o_ref[...] = jnp.dot(x, w_ref[...], precision=HIGHEST, preferred_element_type=F32) + b_ref[...]


def _silu_linear(x, w, b):
    m, k = x.shape
    n = w.shape[1]
    tn = _tile(n, (1024, 512, 256, 128))
    return pl.pallas_call(
        _linear_kernel,
        out_shape=jax.ShapeDtypeStruct((m, n), F32),
        grid=(n // tn,),
        in_specs=[pl.BlockSpec((m, k), lambda j: (0, 0)),
                  pl.BlockSpec((k, tn), lambda j: (0, j)),
                  pl.BlockSpec((1, tn), lambda j: (0, j))],
        out_specs=pl.BlockSpec((m, tn), lambda j: (0, j)),
        compiler_params=_params("arbitrary"),
        name="adaln_linear",
    )(x, w, b.reshape(1, n))


def _in_proj_kernel(x_ref, g_ref, sc_ref, sh_ref, w_ref, wt_ref, *out_refs, widths, t_widths):
    x = x_ref[...]
    xn = x * lax.rsqrt(jnp.mean(x * x, axis=-1, keepdims=True) + EPS) * g_ref[...]
    xb = (xn * (1.0 + sc_ref[0]) + sh_ref[0]).astype(BF16)
    off = 0
    for o_ref, w in zip(out_refs, widths):
        o_ref[...] = jnp.dot(xb, w_ref[:, off:off + w], preferred_element_type=F32).astype(o_ref.dtype)
        off += w
    off = 0
    for o_ref, w in zip(out_refs[len(widths):], t_widths):
        o_ref[...] = lax.dot_general(wt_ref[off:off + w, :], xb, (((1,), (1,)), ((), ())),
                                     preferred_element_type=F32).astype(o_ref.dtype)
        off += w


def _in_proj(x, g, scale, shift, mod_map, w, widths, dtypes, tm, wt=None, t_widths=(), t_dtypes=()):
    r, d = x.shape
    if wt is None:
        wt = jnp.zeros((8, d), BF16)
    return pl.pallas_call(
        functools.partial(_in_proj_kernel, widths=tuple(widths), t_widths=tuple(t_widths)),
        out_shape=([jax.ShapeDtypeStruct((r, wd), dt) for wd, dt in zip(widths, dtypes)]
                   + [jax.ShapeDtypeStruct((wd, r), dt) for wd, dt in zip(t_widths, t_dtypes)]),
        grid=(r // tm,),
        in_specs=[pl.BlockSpec((tm, d), lambda i: (i, 0)),
                  pl.BlockSpec((1, d), lambda i: (0, 0)),
                  pl.BlockSpec((1, 1, d), lambda i: (mod_map(i), 0, 0)),
                  pl.BlockSpec((1, 1, d), lambda i: (mod_map(i), 0, 0)),
                  pl.BlockSpec(w.shape, lambda i: (0, 0)),
                  pl.BlockSpec(wt.shape, lambda i: (0, 0))],
        out_specs=([pl.BlockSpec((tm, wd), lambda i: (i, 0)) for wd in widths]
                   + [pl.BlockSpec((wd, tm), lambda i: (0, i)) for wd in t_widths]),
        compiler_params=_params("parallel"),
        name="in_proj",
    )(x, g.reshape(1, d), scale, shift, w, wt)


def _qk_prep_kernel(x_ref, xr_ref, cos_ref, sin_ref, g_ref, gr_ref, o_ref, *, norm, rope, scale):
    x = x_ref[0].astype(F32)
    if norm:
        s = lax.rsqrt(jnp.mean(x * x, axis=-1, keepdims=True) + EPS)
        x = x * s * g_ref[...]
    if rope:
        xr = xr_ref[0].astype(F32)
        if norm:
            xr = xr * s * gr_ref[...]
        x = x * cos_ref[...] + xr * sin_ref[...]
    o_ref[0] = (x * scale).astype(o_ref.dtype)


def _qk_prep(x, x_rot, cos, sin, gain, gain_rot, *, norm, rope, scale):
    b, h, t, dh = x.shape
    tt = _tile(t, (512, 256, 128))
    blk = pl.BlockSpec((1, h, tt, dh), lambda i, j: (i, 0, j, 0))
    tab = pl.BlockSpec((tt, dh), lambda i, j: (j, 0))
    vec = pl.BlockSpec((1, dh), lambda i, j: (0, 0))
    return pl.pallas_call(
        functools.partial(_qk_prep_kernel, norm=norm, rope=rope, scale=scale),
        out_shape=jax.ShapeDtypeStruct(x.shape, BF16),
        grid=(b, t // tt),
        in_specs=[blk, blk, tab, tab, vec, vec],
        out_specs=blk,
        compiler_params=_params("parallel", "parallel"),
        name="qk_prep",
    )(x, x_rot, cos, sin, gain.reshape(1, dh), gain_rot.reshape(1, dh))


def _q_prep_t_kernel(x_ref, xr_ref, cos_ref, sin_ref, g_ref, gr_ref, o_ref, *, norm, scale):
    w, tt = x_ref.shape
    nh = w // HEAD_DIM
    x = x_ref[...].astype(F32).reshape(nh, HEAD_DIM, tt)
    xr = xr_ref[...].astype(F32).reshape(nh, HEAD_DIM, tt)
    if norm:
        s = lax.rsqrt(jnp.mean(x * x, axis=1, keepdims=True) + EPS)
        x, xr = x * s * g_ref[...], xr * s * gr_ref[...]
    y = x * cos_ref[...] + xr * sin_ref[...]
    o_ref[...] = (y * scale).reshape(w, tt).astype(o_ref.dtype)


def _q_prep_t(xt, xt_rot, col0, n_cols, cos_t, sin_t, gain, gain_rot, scale, norm=True):
    w = xt.shape[0]
    s = cos_t.shape[1]
    tt = _tile(math.gcd(col0, s), (512, 256, 128))
    blk = pl.BlockSpec((w, tt), lambda j: (0, j + col0 // tt))
    tab = pl.BlockSpec((HEAD_DIM, tt), lambda j: (0, j % (s // tt)))
    vec = pl.BlockSpec((HEAD_DIM, 1), lambda j: (0, 0))
    return pl.pallas_call(
        functools.partial(_q_prep_t_kernel, norm=norm, scale=scale),
        out_shape=jax.ShapeDtypeStruct((w, n_cols), BF16),
        grid=(n_cols // tt,),
        in_specs=[blk, blk, tab, tab, vec, vec],
        out_specs=pl.BlockSpec((w, tt), lambda j: (0, j)),
        compiler_params=_params("parallel"),
        name="q_prep_t",
    )(xt, xt_rot, cos_t, sin_t, gain.reshape(HEAD_DIM, 1), gain_rot.reshape(HEAD_DIM, 1))


def _log_sigmoid(x):
    return jnp.minimum(x, 0.0) - jnp.log(1.0 + jnp.exp(-jnp.abs(x)))


def _mlstm_kernel(qf, kf, vf, gf, qb, kb, vb, gb, bias_ref, hf_ref, hb_ref, c_scr, n_scr, m_scr):
    @pl.when(pl.program_id(1) == 0)
    def _():
        c_scr[...] = jnp.zeros_like(c_scr)
        n_scr[...] = jnp.zeros_like(n_scr)
        m_scr[...] = jnp.full_like(m_scr, NEG_INF)

    for d in range(2):
        _mlstm_stages((d,), qf, kf, vf, gf, qb, kb, vb, gb, bias_ref, hf_ref, hb_ref, c_scr, n_scr, m_scr)


def _mlstm_stages(dirs, qf, kf, vf, gf, qb, kb, vb, gb, bias_ref, hf_ref, hb_ref, c_scr, n_scr, m_scr):
    ch, nh, dh = qf.shape[0], MLSTM_HEADS, HEAD_DIM
    row = lax.broadcasted_iota(I32, (ch, ch), 0)
    col = lax.broadcasted_iota(I32, (ch, ch), 1)
    lane_lo = lax.broadcasted_iota(I32, (ch, 2 * dh), 1) < dh
    row_lo = lax.broadcasted_iota(I32, (2 * dh, 2 * dh), 0) < dh
    col_lo = lax.broadcasted_iota(I32, (2 * dh, 2 * dh), 1) < dh
    vec_lo = lax.broadcasted_iota(I32, (1, 2 * dh), 1) < dh
    pair = lambda a, b: jnp.where(lane_lo, a, b)
    refs = ((qf, kf, vf, gf, hf_ref), (qb, kb, vb, gb, hb_ref))
    jobs = [(d, p) for d in dirs for p in range(nh // 2)]
    sl = lambda p: slice(2 * dh * p, 2 * dh * (p + 1))

    qk = {}
    for d, p in jobs:
        q2, k2 = refs[d][0][:, sl(p)], refs[d][1][:, sl(p)]
        zero = jnp.zeros_like(q2)
        q_st = jnp.concatenate([jnp.where(lane_lo, q2, zero), jnp.where(lane_lo, zero, q2)], axis=0)
        qk[d, p] = lax.dot_general(q_st, k2, (((1,), (1,)), ((), ())), preferred_element_type=F32)

    st = {}
    for d in dirs:
        seen = (col <= row) if d == 0 else (col >= row)
        tri = seen.astype(F32)
        tri_t = ((row <= col) if d == 0 else (row >= col)).astype(F32)
        last = ch - 1 if d == 0 else 0
        g = refs[d][3][...] + bias_ref[...]
        g_t = g.T
        lo = 2 * nh * d
        log2e = math.log2(math.e)
        li_col, lf_col = g[:, lo:lo + nh] * log2e, _log_sigmoid(g[:, lo + nh:lo + 2 * nh]) * log2e
        li_row, lf_row = g_t[lo:lo + nh, :] * log2e, _log_sigmoid(g_t[lo + nh:lo + 2 * nh, :]) * log2e
        b_col = jnp.dot(tri, lf_col, precision=HIGHEST, preferred_element_type=F32)
        b_row = jnp.dot(lf_row, tri_t, precision=HIGHEST, preferred_element_type=F32)
        src_row = li_row - b_row
        for h in range(nh):
            bc, br = b_col[:, h:h + 1], b_row[h:h + 1, :]
            lic = li_col[:, h:h + 1]
            b_last = br[:, last:last + 1]
            m_prev = m_scr[d, h // 2][:, (h % 2) * dh:(h % 2) * dh + 1]
            d_log = jnp.where(seen, bc + src_row[h:h + 1, :], NEG_INF)
            inter_log = bc + m_prev
            m_t = jnp.maximum(inter_log, jnp.max(d_log, axis=1, keepdims=True))
            w_log = b_last - bc + lic
            m_new = jnp.maximum(b_last + m_prev, jnp.max(w_log, axis=0, keepdims=True))
            st[d, h] = dict(dmat=jnp.exp2(d_log - m_t), inter=jnp.exp2(inter_log - m_t), floor=jnp.exp2(-m_t),
                            m_new=m_new, decay=jnp.exp2(b_last + m_prev - m_new), wn=jnp.exp2(w_log - m_new))

    mm = {}
    for d, p in jobs:
        a, b = st[d, 2 * p], st[d, 2 * p + 1]
        q2, k2, v2 = refs[d][0][:, sl(p)], refs[d][1][:, sl(p)], refs[d][2][:, sl(p)]
        s = qk[d, p] * jnp.concatenate([a["dmat"], b["dmat"]], axis=0)
        kw = k2.astype(F32) * pair(a["wn"], b["wn"])
        mm[d, p] = dict(
            s_sum=jnp.sum(s, axis=1, keepdims=True), kw_sum=jnp.sum(kw, axis=0, keepdims=True),
            sv=jnp.dot(s.astype(BF16), v2, preferred_element_type=F32),
            q_c=jnp.dot(q2, c_scr[d, p].astype(BF16), preferred_element_type=F32),
            kv=lax.dot_general(kw.astype(BF16), v2, (((0,), (0,)), ((), ())), preferred_element_type=F32))

    for d, p in jobs:
        a, b, r = st[d, 2 * p], st[d, 2 * p + 1], mm[d, p]
        q2 = refs[d][0][:, sl(p)]
        n_prev = n_scr[d, p]
        num = pair(a["inter"], b["inter"]) * r["q_c"] + jnp.where(lane_lo, r["sv"][:ch], r["sv"][ch:])
        qn = q2.astype(F32) * n_prev
        qn_a = jnp.sum(jnp.where(lane_lo, qn, 0.0), axis=1, keepdims=True)
        qn_b = jnp.sum(jnp.where(lane_lo, 0.0, qn), axis=1, keepdims=True)
        den_a = jnp.maximum(jnp.abs(a["inter"] * qn_a + r["s_sum"][:ch]), a["floor"])
        den_b = jnp.maximum(jnp.abs(b["inter"] * qn_b + r["s_sum"][ch:]), b["floor"])
        refs[d][4][:, sl(p)] = num / pair(den_a, den_b)
        c_scr[d, p] = (jnp.where(row_lo, a["decay"], b["decay"]) * c_scr[d, p]
                       + jnp.where(row_lo == col_lo, r["kv"], 0.0))
        n_scr[d, p] = jnp.where(vec_lo, a["decay"], b["decay"]) * n_prev + r["kw_sum"]
        m_scr[d, p] = jnp.where(vec_lo, a["m_new"], b["m_new"])


def _mlstm(q, k, v, gates, bias, b, lc, s):
    r, w = q.shape
    ch = _tile(math.gcd(lc, s), MLSTM_CHUNKS)
    ncc, ncx = lc // ch, s // ch
    base = b * ncc

    def fwd(i, c):
        return jnp.where(c < ncc, i * ncc + c, base + i * ncx + (c - ncc)), 0

    def bwd(i, c):
        return jnp.where(c < ncc, i * ncc + (ncc - 1 - c), base + i * ncx + (ncx - 1 - (c - ncc))), 0

    spec = lambda width, m: pl.BlockSpec((ch, width), m)
    npair = MLSTM_HEADS // 2
    return pl.pallas_call(
        _mlstm_kernel,
        out_shape=[jax.ShapeDtypeStruct((r, w), F32)] * 2,
        grid=(b, ncc + ncx),
        in_specs=[spec(w, fwd), spec(w, fwd), spec(w, fwd), spec(LANES, fwd),
                  spec(w, bwd), spec(w, bwd), spec(w, bwd), spec(LANES, bwd),
                  pl.BlockSpec((1, LANES), lambda i, c: (0, 0))],
        out_specs=[spec(w, fwd), spec(w, bwd)],
        scratch_shapes=[pltpu.VMEM((2, npair, 2 * HEAD_DIM, 2 * HEAD_DIM), F32),
                        pltpu.VMEM((2, npair, 1, 2 * HEAD_DIM), F32),
                        pltpu.VMEM((2, npair, 1, 2 * HEAD_DIM), F32)],
        compiler_params=_params("parallel", "arbitrary"),
        name="mlstm_scan",
    )(q, k, v, gates, q, k, v, gates, bias)


def _mlstm_out_kernel(hf_ref, hb_ref, o_ref, nrm_ref, ones_ref, y_ref):
    h = hf_ref[...] + hb_ref[...]
    ms = jnp.dot((h * h).astype(BF16), ones_ref[...], preferred_element_type=F32) * (1.0 / HEAD_DIM)
    hn = h * lax.rsqrt(ms + EPS) * nrm_ref[...]
    y_ref[...] = (jax.nn.sigmoid(o_ref[...].astype(F32)) * hn).astype(y_ref.dtype)


def _head_ones(width):
    idx = np.arange(width) // HEAD_DIM
    return jnp.asarray(idx[:, None] == idx[None, :], BF16)


def _mlstm_out(hf, hb, o, norm, tm):
    r, w = hf.shape
    blk = pl.BlockSpec((tm, w), lambda i: (i, 0))
    return pl.pallas_call(
        _mlstm_out_kernel,
        out_shape=jax.ShapeDtypeStruct((r, w), BF16),
        grid=(r // tm,),
        in_specs=[blk, blk, blk,
                  pl.BlockSpec((1, w), lambda i: (0, 0)),
                  pl.BlockSpec((w, w), lambda i: (0, 0))],
        out_specs=blk,
        compiler_params=_params("parallel"),
        name="mlstm_out",
    )(hf, hb, o, norm.reshape(1, w), _head_ones(w))


def _swa_kernel(q_ref, k0, k1, k2, k3, v0, v1, v2, v3, kc_ref, vc_ref, sink_ref, o_ref, *, seq):
    w, dh = SWA_BLOCK, HEAD_DIM
    tq = q_ref.shape[1]
    g = q_ref.shape[0] // dh
    i = pl.program_id(2)
    k_loc = jnp.concatenate([k0[...], k1[...], k2[...], k3[...]], axis=1)
    v_loc = jnp.concatenate([v0[...], v1[...], v2[...], v3[...]], axis=1)
    kpos = (2 * i - 1) * w + lax.broadcasted_iota(I32, (4 * w, tq), 0)
    qpos = i * tq + lax.broadcasted_iota(I32, (4 * w, tq), 1)
    valid = (jnp.abs(qpos - kpos) <= SWA_WINDOW) & (kpos >= 0) & (kpos < seq)
    t_dot = lambda kt, q: lax.dot_general(kt, q, (((0,), (0,)), ((), ())), preferred_element_type=F32)
    qs = [q_ref[h * dh:(h + 1) * dh, :] for h in range(g)]
    s_loc = [jnp.where(valid, t_dot(k_loc, qs[h]), NEG_INF) for h in range(g)]
    s_ctx = [t_dot(kc_ref[...], qs[h]) for h in range(g)]
    outs = []
    for h in range(g):
        sink = sink_ref[0, h:h + 1, :]
        m = jnp.maximum(jnp.maximum(jnp.max(s_loc[h], axis=0, keepdims=True),
                                    jnp.max(s_ctx[h], axis=0, keepdims=True)), sink)
        p_loc = jnp.exp2(s_loc[h] - m)
        p_ctx = jnp.exp2(s_ctx[h] - m)
        den = jnp.sum(p_loc, axis=0, keepdims=True) + jnp.sum(p_ctx, axis=0, keepdims=True) + jnp.exp2(sink - m)
        o = (jnp.dot(v_loc, p_loc.astype(BF16), preferred_element_type=F32)
             + jnp.dot(vc_ref[...], p_ctx.astype(BF16), preferred_element_type=F32))
        outs.append(o / den)
    o_ref[...] = jnp.concatenate(outs, axis=0).T.astype(o_ref.dtype)


def _swa(qt, kt, kvt, sink2, b, s, lc):
    dh = HEAD_DIM
    nh, hkv = qt.shape[0] // dh, kt.shape[0] // dh
    g = nh // hkv
    w = SWA_BLOCK
    tq = 2 * w
    nb, nq = s // w, s // tq
    c0 = b * lc // w
    clampi = lambda j: jnp.clip(j, 0, nb - 1)
    kspec = lambda o: pl.BlockSpec((dh, w), lambda bi, hi, i: (hi, bi * nb + clampi(2 * i + o)))
    vspec = lambda o: pl.BlockSpec((dh, w), lambda bi, hi, i: (hi, c0 + bi * nb + clampi(2 * i + o)))
    cspec = pl.BlockSpec((dh, lc), lambda bi, hi, i: (hi, bi))
    k_all, v_all = kvt
    return pl.pallas_call(
        functools.partial(_swa_kernel, seq=s),
        out_shape=jax.ShapeDtypeStruct((b * s, nh * dh), BF16),
        grid=(b, hkv, nq),
        in_specs=[pl.BlockSpec((g * dh, tq), lambda bi, hi, i: (hi, bi * nq + i)),
                  kspec(-1), kspec(0), kspec(1), kspec(2), vspec(-1), vspec(0), vspec(1), vspec(2),
                  cspec, cspec, pl.BlockSpec((1, g, tq), lambda bi, hi, i: (hi, 0, 0))],
        out_specs=pl.BlockSpec((tq, g * dh), lambda bi, hi, i: (bi * nq + i, hi)),
        compiler_params=_params("parallel", "parallel", "arbitrary"),
        name="window_attention",
    )(qt, kt, kt, kt, kt, v_all, v_all, v_all, v_all, k_all, v_all, sink2)


def _ctx_attn_kernel(q_ref, k_ref, v_ref, sink_ref, o_ref, *, use_sink):
    g, lq, dh = q_ref.shape[1:]
    q = q_ref[0].reshape(g * lq, dh)
    s = lax.dot_general(q, k_ref[0, 0], (((1,), (1,)), ((), ())), preferred_element_type=F32) * ATTN_SCALE
    m = jnp.max(s, axis=1, keepdims=True)
    if use_sink:
        m = jnp.maximum(m, sink_ref[0])
    p = jnp.exp(s - m)
    den = jnp.sum(p, axis=1, keepdims=True)
    if use_sink:
        den = den + jnp.exp(sink_ref[0] - m)
    o = jnp.dot(p.astype(BF16), v_ref[0, 0], preferred_element_type=F32) / den
    o_ref[0] = o.reshape(g, lq, dh).astype(o_ref.dtype)


def _ctx_attn(q, k, v, sink_col, use_sink):
    b, nh, lq, dh = q.shape
    hkv = k.shape[1]
    g = nh // hkv
    qblk = pl.BlockSpec((1, g, lq, dh), lambda bi, hi: (bi, hi, 0, 0))
    kblk = pl.BlockSpec((1, 1, k.shape[2], dh), lambda bi, hi: (bi, hi, 0, 0))
    return pl.pallas_call(
        functools.partial(_ctx_attn_kernel, use_sink=use_sink),
        out_shape=jax.ShapeDtypeStruct(q.shape, BF16),
        grid=(b, hkv),
        in_specs=[qblk, kblk, kblk, pl.BlockSpec((1, g * lq, 1), lambda bi, hi: (hi, 0, 0))],
        out_specs=qblk,
        compiler_params=_params("parallel", "parallel"),
        name="context_attention",
    )(q, k, v, sink_col)


def _flash_kernel(q_ref, k_ref, v_ref, o_ref, m_scr, acc_scr, *bufs, tk):
    dh = HEAD_DIM
    g = q_ref.shape[0] // dh
    n = k_ref.shape[2] // tk
    nb = len(bufs)
    depth = nb - 1
    m_scr[...] = jnp.full_like(m_scr, NEG_INF)
    acc_scr[...] = jnp.zeros_like(acc_scr)

    def scores(j, dst):
        kk = k_ref[0, 0, pl.ds(pl.multiple_of(j * tk, tk), tk), :]
        for h in range(g):
            dst[h] = jnp.dot(kk, q_ref[h * dh:(h + 1) * dh, :], preferred_element_type=F32)

    def update(j, src):
        vv = v_ref[0, 0, :, pl.ds(pl.multiple_of(j * tk, tk), tk)]
        ss = [src[h] for h in range(g)]
        m_olds = [m_scr[h] for h in range(g)]
        m_news = [jnp.maximum(m_olds[h], jnp.max(ss[h], axis=0, keepdims=True)) for h in range(g)]
        ps = [jnp.exp2(ss[h] - m_news[h]).astype(BF16) for h in range(g)]
        pvs = [jnp.dot(vv, ps[h], preferred_element_type=F32) for h in range(g)]
        for h in range(g):
            acc_scr[h] = jnp.exp2(m_olds[h] - m_news[h]) * acc_scr[h] + pvs[h]
            m_scr[h] = m_news[h]

    for j in range(min(depth, n)):
        scores(j, bufs[j % nb])
    loops = max(n - depth, 0) // nb

    def body(i, carry):
        for r in range(nb):
            scores(nb * i + r + depth, bufs[(r + depth) % nb])
            update(nb * i + r, bufs[r])
        return carry

    lax.fori_loop(0, loops, body, 0)
    for j in range(loops * nb, n):
        if j + depth < n:
            scores(j + depth, bufs[(j + depth) % nb])
        update(j, bufs[j % nb])
    acc = acc_scr[...]
    out = (acc[:, :dh] / acc[:, dh:dh + 1]).reshape(g * dh, -1)
    o_ref[...] = out.T.astype(o_ref.dtype)


def _flash(qt, k, vt, b, s, tq=None, tk=None):
    w = qt.shape[0]
    hkv, t, dh = k.shape[1], k.shape[2], k.shape[3]
    g = w // dh // hkv
    tq = tq or _tile(s, (256, 128))
    tk = tk or _tile(t, (256, 128))
    nq = s // tq
    ones_rows = 8
    vt = jnp.concatenate([vt, jnp.ones((b, hkv, ones_rows, t), vt.dtype)], axis=2)
    nv = dh + ones_rows
    return pl.pallas_call(
        functools.partial(_flash_kernel, tk=tk),
        out_shape=jax.ShapeDtypeStruct((b * s, w), BF16),
        grid=(b, hkv, nq),
        in_specs=[pl.BlockSpec((g * dh, tq), lambda bi, hi, i: (hi, bi * nq + i)),
                  pl.BlockSpec((1, 1, t, dh), lambda bi, hi, i: (bi, hi, 0, 0)),
                  pl.BlockSpec((1, 1, nv, t), lambda bi, hi, i: (bi, hi, 0, 0))],
        out_specs=pl.BlockSpec((tq, g * dh), lambda bi, hi, i: (bi * nq + i, hi)),
        scratch_shapes=([pltpu.VMEM((g, 1, tq), F32), pltpu.VMEM((g, nv, tq), F32)]
                        + [pltpu.VMEM((g, tk, tq), F32)] * (FLASH_DEPTH + 1)),
        compiler_params=_params("parallel", "parallel", "arbitrary"),
        name="dense_attention",
    )(qt, k, vt)


def _bmm_kernel(a_ref, b_ref, o_ref):
    o_ref[0] = jnp.dot(a_ref[0], b_ref[0], preferred_element_type=F32).astype(o_ref.dtype)


def _group_matmul(a, bmat, name):
    g, m, k = a.shape
    n = bmat.shape[2]
    tm = _tile(m, (1024, 512, 256, 128, 64, 32, 16, 8))
    return pl.pallas_call(
        _bmm_kernel,
        out_shape=jax.ShapeDtypeStruct((g, m, n), F32),
        grid=(g, m // tm),
        in_specs=[pl.BlockSpec((1, tm, k), lambda gi, i: (gi, i, 0)),
                  pl.BlockSpec((1, k, n), lambda gi, i: (gi, 0, 0))],
        out_specs=pl.BlockSpec((1, tm, n), lambda gi, i: (gi, i, 0)),
        compiler_params=_params("parallel", "parallel"),
        name=name,
    )(a, bmat)


def _s5_scan_kernel(sre_ref, sim_ref, are_ref, aim_ref, zre_ref, zim_ref):
    nd, nsteps = sre_ref.shape[0], sre_ref.shape[1]
    for d in range(nd):
        a_re, a_im = are_ref[d], aim_ref[d]

        def body(i, carry, d=d, a_re=a_re, a_im=a_im):
            z_re, z_im = carry
            zre_ref[d, i] = z_re
            zim_ref[d, i] = z_im
            return (a_re * z_re - a_im * z_im + sre_ref[d, i], a_re * z_im + a_im * z_re + sim_ref[d, i])

        zero = jnp.zeros(sre_ref.shape[2:], F32)
        lax.fori_loop(0, nsteps, body, (zero, zero))


def _s5_scan(s_re, s_im, a_re, a_im):
    full = lambda arr: pl.BlockSpec(arr.shape, lambda i: (0,) * arr.ndim)
    return pl.pallas_call(
        _s5_scan_kernel,
        out_shape=[jax.ShapeDtypeStruct(s_re.shape, F32)] * 2,
        grid=(1,),
        in_specs=[full(s_re), full(s_im), full(a_re), full(a_im)],
        out_specs=[full(s_re), full(s_im)],
        compiler_params=_params("arbitrary"),
        name="s5_chunk_scan",
    )(s_re, s_im, a_re, a_im)


def _s5_glu_kernel(y_ref, u_ref, d_ref, w_ref, b_ref, o_ref):
    y = y_ref[...] + d_ref[...] * u_ref[...]
    y = jax.nn.gelu(y)
    gate = jnp.dot(y.astype(BF16), w_ref[...], preferred_element_type=F32) + b_ref[...]
    o_ref[...] = (y * jax.nn.sigmoid(gate)).astype(o_ref.dtype)


def _s5_glu(y, u, d_skip, w_glu, b_glu, tm):
    r, c = y.shape
    blk = pl.BlockSpec((tm, c), lambda i: (i, 0))
    vec = pl.BlockSpec((1, c), lambda i: (0, 0))
    return pl.pallas_call(
        _s5_glu_kernel,
        out_shape=jax.ShapeDtypeStruct((r, c), BF16),
        grid=(r // tm,),
        in_specs=[blk, blk, vec, pl.BlockSpec((c, c), lambda i: (0, 0)), vec],
        out_specs=blk,
        compiler_params=_params("parallel"),
        name="s5_readout_glu",
    )(y, u, d_skip.reshape(1, c), w_glu.astype(BF16), b_glu.reshape(1, c))


def _s5_tables(lam_re, lam_im, log_dt, b_re, b_im, c_re, c_im):
    ln, p, gc = S5_CHUNK, S5_STATE, S5_GROUP
    lam = lax.complex(lam_re.astype(F32), lam_im.astype(F32))
    dt = jnp.exp(log_dt.astype(F32))[..., None]
    a_bar = jnp.exp(lam * dt)
    b_scale = (a_bar - 1.0) / lam
    b_mat = lax.complex(b_re.astype(F32), b_im.astype(F32))
    c_mat = lax.complex(c_re.astype(F32), c_im.astype(F32))
    tau = jnp.arange(ln + 1, dtype=F32)
    apow = jnp.exp((lam * dt)[:, :, None, :] * tau[None, None, :, None])
    drive = b_scale[..., None] * b_mat[None]
    kern = jnp.real(jnp.einsum('gcp,dgtp,dgpe->dgtce', c_mat, apow[:, :, :ln], drive))
    kc = jnp.concatenate([kern[1, :, :0:-1], (kern[0, :, :1] + kern[1, :, :1]), kern[0, :, 1:]], axis=1)
    kc_e = kc.transpose(0, 3, 1, 2).reshape(S5_GROUPS, gc, (2 * ln - 1) * gc)
    toep = jnp.stack([kc_e[:, :, (ln - 1 - s) * gc:(2 * ln - 1 - s) * gc] for s in range(ln)], axis=1)
    toep = toep.reshape(S5_GROUPS, ln * gc, ln * gc)
    w_f = apow[0, :, ln - 1::-1][:, :ln, :, None] * drive[0][:, None]
    w_b = apow[1, :, :ln, :, None] * drive[1][:, None]
    def m_in(wc):
        wt = wc.transpose(0, 1, 3, 2).reshape(S5_GROUPS, ln * gc, p)
        return jnp.concatenate([jnp.real(wt), jnp.imag(wt)], axis=-1)
    min_all = jnp.concatenate([m_in(w_f), m_in(w_b)], axis=-1)
    o_f = c_mat[:, None] * apow[0, :, 1:ln + 1][:, :, None, :]
    o_b = c_mat[:, None] * apow[1, :, ln:0:-1][:, :, None, :]
    def m_out(oc):
        ot = oc.transpose(0, 3, 1, 2).reshape(S5_GROUPS, p, ln * gc)
        return jnp.concatenate([jnp.real(ot), -jnp.imag(ot)], axis=1)
    rhs = jnp.concatenate([toep, m_out(o_f), m_out(o_b)], axis=1)
    a_l = apow[:, :, ln].reshape(2, 1, S5_GROUPS * p)
    return min_all.astype(BF16), rhs.astype(BF16), jnp.real(a_l), jnp.imag(a_l)


def _s5_mixer(u_x, u_c, tables, d_skip, w_glu, b_glu):
    min_all, rhs, a_re, a_im = tables
    b, s, _ = u_x.shape
    ln, p, gc, ng = S5_CHUNK, S5_STATE, S5_GROUP, S5_GROUPS

    def chunks(u):
        nc = u.shape[1] // ln
        return u.reshape(b, nc, ln, ng, gc).transpose(3, 0, 1, 2, 4).reshape(ng, b * nc, ln * gc).astype(BF16), nc

    ux, ncx = chunks(u_x)
    uc, ncc = chunks(u_c)
    sx = _group_matmul(ux, min_all, "s5_local_state").reshape(ng, b, ncx, 4, p)
    sc = _group_matmul(uc, min_all, "s5_local_state_ctx").reshape(ng, b, ncc, 4, p)

    def scan_order(part):
        f = jnp.concatenate([sc[:, :, :, part], sx[:, :, :, part]], axis=2)
        r = jnp.concatenate([sc[:, :, ::-1, part + 2], sx[:, :, ::-1, part + 2]], axis=2)
        return jnp.stack([f, r]).transpose(0, 3, 2, 1, 4).reshape(2, ncc + ncx, b, ng * p)

    z_re, z_im = _s5_scan(scan_order(0), scan_order(1), a_re, a_im)

    def latent(z, d):
        zl = z[d, ncc:]
        if d == 1:
            zl = zl[::-1]
        return zl.reshape(ncx, b, ng, p).transpose(2, 1, 0, 3).reshape(ng, b * ncx, p)

    lhs = jnp.concatenate([ux, latent(z_re, 0).astype(BF16), latent(z_im, 0).astype(BF16),
                           latent(z_re, 1).astype(BF16), latent(z_im, 1).astype(BF16)], axis=-1)
    y = _group_matmul(lhs, rhs, "s5_outputs")
    y = y.reshape(ng, b, ncx, ln, gc).transpose(1, 2, 3, 0, 4).reshape(b * s, S5_CHANNELS)
    return _s5_glu(y, u_x.reshape(b * s, S5_CHANNELS), d_skip, w_glu, b_glu, _tile(b * s, (1024, 512, 256)))


def _out_proj_kernel(x_ref, a_ref, b_ref, wa_ref, wb_ref, gt_ref, g_ref, sc_ref, sh_ref, wr_ref, br_ref,
                     xo_ref, xn_ref, lg_ref):
    y = (jnp.dot(a_ref[...], wa_ref[...], preferred_element_type=F32)
         + jnp.dot(b_ref[...], wb_ref[...], preferred_element_type=F32))
    x = x_ref[...] + gt_ref[0] * y
    xo_ref[...] = x
    xn = x * lax.rsqrt(jnp.mean(x * x, axis=-1, keepdims=True) + EPS) * g_ref[...]
    xn = xn * (1.0 + sc_ref[0]) + sh_ref[0]
    hi = xn.astype(BF16)
    xn_ref[...] = hi
    lo = (xn - hi.astype(F32)).astype(BF16)
    part = jnp.dot(hi, wr_ref[...], preferred_element_type=F32)
    part = part[:, :LANES] + part[:, LANES:] + jnp.dot(lo, wr_ref[:, :LANES], preferred_element_type=F32)
    lg_ref[...] = part[:, :lg_ref.shape[1]] + br_ref[...]


def _out_proj(x, row0, a, bmix, wa, wb, gate, g_ffn, scale, shift, mod_map, w_router, b_router, tm):
    r = a.shape[0]
    d = x.shape[1]
    t0 = row0 // tm
    ne = w_router.shape[1]
    w_hi = w_router.astype(BF16)
    w_lo = (w_router - w_hi.astype(F32)).astype(BF16)
    w_router = jnp.concatenate([_pad_cols(w_hi), _pad_cols(w_lo)], axis=1)
    row = lambda w: pl.BlockSpec((tm, w), lambda i: (i, 0))
    mod = pl.BlockSpec((1, 1, d), lambda i: (mod_map(i), 0, 0))
    full = lambda arr: pl.BlockSpec(arr.shape, lambda i: (0, 0))
    return pl.pallas_call(
        _out_proj_kernel,
        out_shape=[jax.ShapeDtypeStruct((r, d), F32), jax.ShapeDtypeStruct((r, d), BF16),
                   jax.ShapeDtypeStruct((r, ne), F32)],
        grid=(r // tm,),
        in_specs=[pl.BlockSpec((tm, d), lambda i: (i + t0, 0)), row(a.shape[1]), row(bmix.shape[1]),
                  full(wa), full(wb), mod, pl.BlockSpec((1, d), lambda i: (0, 0)), mod, mod,
                  full(w_router), pl.BlockSpec((1, ne), lambda i: (0, 0))],
        out_specs=[row(d), row(d), row(ne)],
        compiler_params=_params("parallel"),
        name="out_proj",
    )(x, a, bmix, wa, wb, gate, g_ffn.reshape(1, d), scale, shift, w_router, b_router.reshape(1, ne))


def _route_kernel(lg_ref, idx_ref, gate_ref, pos_ref, cnt_ref, carry):
    tm, ne = lg_ref.shape

    @pl.when(pl.program_id(0) == 0)
    def _():
        carry[...] = jnp.zeros_like(carry)

    work = lg_ref[...]
    lane = lax.broadcasted_iota(I32, (tm, ne), 1).astype(F32)
    out_lane = lax.broadcasted_iota(I32, (tm, LANES), 1)
    vals, hots, idx_out = [], [], jnp.zeros((tm, LANES), I32)
    for kk in range(TOP_K):
        mx = jnp.max(work, axis=1, keepdims=True)
        idx = jnp.min(jnp.where(work == mx, lane, float(ne)), axis=1, keepdims=True)
        hot = lane == idx
        work = jnp.where(hot, -jnp.inf, work)
        vals.append(mx)
        hots.append(hot)
        idx_out = jnp.where(out_lane == kk, idx.astype(I32), idx_out)
    exps = [jnp.exp(vv - vals[0]) for vv in vals]
    tot = exps[0] + exps[1] + exps[2] + exps[3]
    multi = (hots[0] | hots[1] | hots[2] | hots[3]).astype(BF16)
    r_i = lax.broadcasted_iota(I32, (tm, tm), 0)
    c_i = lax.broadcasted_iota(I32, (tm, tm), 1)
    before = (r_i > c_i).astype(BF16)
    prefix = jnp.dot(before, multi, preferred_element_type=F32) + carry[...]
    gate_out = jnp.zeros((tm, LANES), F32)
    pos_out = jnp.zeros((tm, LANES), I32)
    for kk in range(TOP_K):
        gate_out = jnp.where(out_lane == kk, exps[kk] / tot, gate_out)
        pos = jnp.sum(jnp.where(hots[kk], prefix, 0.0), axis=1, keepdims=True).astype(I32)
        pos_out = jnp.where(out_lane == kk, pos, pos_out)
    idx_ref[...] = idx_out
    gate_ref[...] = gate_out
    pos_ref[...] = pos_out
    carry[...] = carry[...] + jnp.sum(multi.astype(F32), axis=0, keepdims=True)
    cnt_ref[...] = carry[...]


def _route(logits, tm, row0, n):
    ne = logits.shape[1]
    t0 = row0 // tm
    wide = pl.BlockSpec((tm, LANES), lambda i: (i, 0))
    return pl.pallas_call(
        _route_kernel,
        out_shape=[jax.ShapeDtypeStruct((n, LANES), I32), jax.ShapeDtypeStruct((n, LANES), F32),
                   jax.ShapeDtypeStruct((n, LANES), I32), jax.ShapeDtypeStruct((1, ne), F32)],
        grid=(n // tm,),
        in_specs=[pl.BlockSpec((tm, ne), lambda i: (i + t0, 0))],
        out_specs=[wide, wide, wide, pl.BlockSpec((1, ne), lambda i: (0, 0))],
        scratch_shapes=[pltpu.VMEM((1, ne), F32)],
        compiler_params=_params("arbitrary"),
        name="moe_route",
    )(logits)


def _expert_kernel(be_ref, nb_ref, x_ref, wgu_ref, bgu_ref, wd_ref, bd_ref, perm_ref, o_ref, wgu_bf, wd_bf):
    i = pl.program_id(0)
    fresh = jnp.logical_or(i == 0, be_ref[i] != be_ref[jnp.maximum(i - 1, 0)])

    @pl.when(jnp.logical_and(i < nb_ref[0], fresh))
    def _():
        tile = perm_ref.shape[0]
        for t in range(wgu_ref.shape[2] // tile):
            sl = slice(t * tile, (t + 1) * tile)
            wgu_bf[:, sl] = jnp.dot(wgu_ref[0, :, sl].astype(BF16), perm_ref[...],
                                    preferred_element_type=F32).astype(BF16)
        wd_bf[...] = wd_ref[0].astype(BF16)

    @pl.when(i < nb_ref[0])
    def _():
        h = jnp.dot(x_ref[...], wgu_bf[...], preferred_element_type=F32) + bgu_ref[0]
        acts = []
        for t in range(h.shape[1] // (2 * LANES)):
            glu = jnp.minimum(h[:, 2 * LANES * t:2 * LANES * t + LANES], SWIGLU_LIMIT)
            lin = jnp.clip(h[:, 2 * LANES * t + LANES:2 * LANES * (t + 1)], -SWIGLU_LIMIT, SWIGLU_LIMIT)
            acts.append((glu * jax.nn.sigmoid(SWIGLU_ALPHA * glu) * (lin + 1.0)).astype(BF16))
        act = jnp.concatenate(acts, axis=1)
        o_ref[...] = (jnp.dot(act, wd_bf[...], preferred_element_type=F32) + bd_ref[0]).astype(o_ref.dtype)

    @pl.when(i >= nb_ref[0])
    def _():
        o_ref[...] = jnp.zeros_like(o_ref)


def _experts(xs, blk_e, n_used, wgu, bgu, wd, bd):
    n, d = xs.shape
    ff2 = wgu.shape[2]
    nblk = n // MOE_ROWS
    tile = 2 * LANES
    j = np.arange(tile)
    col = np.where(j < LANES, 2 * j, 2 * (j - LANES) + 1)
    perm = jnp.asarray(np.arange(tile)[:, None] == col[None, :], BF16)
    wspec = lambda shape: pl.BlockSpec((1,) + shape, lambda i, be, nb: (be[i], 0, 0))
    return pl.pallas_call(
        _expert_kernel,
        out_shape=jax.ShapeDtypeStruct((n, d), BF16),
        grid_spec=pltpu.PrefetchScalarGridSpec(
            num_scalar_prefetch=2,
            grid=(nblk,),
            in_specs=[pl.BlockSpec((MOE_ROWS, d), lambda i, be, nb: (i, 0)),
                      wspec((d, ff2)), wspec((1, ff2)), wspec((ff2 // 2, d)), wspec((1, d)),
                      pl.BlockSpec((tile, tile), lambda i, be, nb: (0, 0))],
            out_specs=pl.BlockSpec((MOE_ROWS, d), lambda i, be, nb: (i, 0)),
            scratch_shapes=[pltpu.VMEM((d, ff2), BF16), pltpu.VMEM((ff2 // 2, d), BF16)]),
        compiler_params=_params("arbitrary"),
        name="moe_experts",
    )(blk_e, n_used, xs, wgu, bgu, wd, bd, perm)


def _combine_kernel(x_ref, y_ref, gate_ref, gt_ref, g_ref, o_ref, *, final_norm):
    gates = gate_ref[...]
    y = gates[:, 0:1] * y_ref[0].astype(F32)
    for kk in range(1, TOP_K):
        y = y + gates[:, kk:kk + 1] * y_ref[kk].astype(F32)
    x = x_ref[...] + gt_ref[0] * y
    if final_norm:
        x = x * lax.rsqrt(jnp.mean(x * x, axis=-1, keepdims=True) + EPS) * g_ref[...]
    o_ref[...] = x


def _combine_alias_kernel(x_ref, y_ref, gate_ref, gt_ref, g_ref, prev_ref, o_ref, *, final_norm):
    del prev_ref
    _combine_kernel(x_ref, y_ref, gate_ref, gt_ref, g_ref, o_ref, final_norm=final_norm)


def _combine(x, y4, gates, gate_mod, mod_map, g_final, final_norm, tm, row0, prev):
    r, d = x.shape
    n = y4.shape[1]
    t0 = row0 // tm
    in_specs = [pl.BlockSpec((tm, d), lambda i: (i + t0, 0)),
                pl.BlockSpec((TOP_K, tm, d), lambda i: (0, i, 0)),
                pl.BlockSpec((tm, LANES), lambda i: (i, 0)),
                pl.BlockSpec((1, 1, d), lambda i: (mod_map(i + t0), 0, 0)),
                pl.BlockSpec((1, d), lambda i: (0, 0))]
    args = [x, y4, gates, gate_mod, g_final.reshape(1, d)]
    body, aliases = _combine_kernel, {}
    if prev is not None:
        in_specs.append(pl.BlockSpec(memory_space=pl.ANY))
        args.append(prev)
        body, aliases = _combine_alias_kernel, {len(args) - 1: 0}
    return pl.pallas_call(
        functools.partial(body, final_norm=final_norm),
        out_shape=jax.ShapeDtypeStruct((r, d), F32),
        grid=(n // tm,),
        in_specs=in_specs,
        out_specs=pl.BlockSpec((tm, d), lambda i: (i + t0, 0)),
        input_output_aliases=aliases,
        compiler_params=_params("parallel"),
        name="moe_combine",
    )(*args)


def _moe_parts(tiles):
    total = sum(MOE_SPLIT)
    sizes = [tiles * w // total for w in MOE_SPLIT[:-1]]
    sizes.append(tiles - sum(sizes))
    return [sz for sz in sizes if sz > 0]


def _moe(x, xn, logits, gate_mod, mod_map, weights, g_final, final_norm, tm):
    wgu, bgu, wd, bd = weights
    r, d = xn.shape
    routed, row0 = [], 0
    for tiles in _moe_parts(r // tm):
        n = tiles * tm
        nblk = n * TOP_K // MOE_ROWS + N_EXPERTS
        blk_row0 = jnp.arange(nblk, dtype=I32) * MOE_ROWS
        idx_w, gates_w, pos_w, counts = _route(logits, tm, row0, n)
        idx, pos = idx_w[:, :TOP_K], pos_w[:, :TOP_K]
        counts = counts[0].astype(I32)
        padded = (counts + MOE_ROWS - 1) // MOE_ROWS * MOE_ROWS
        pad_end = jnp.cumsum(padded)
        pad_start = pad_end - padded
        dest = pad_start[idx] + pos
        blk_e = jnp.minimum(jnp.sum((pad_end[None, :] <= blk_row0[:, None]).astype(I32), axis=1), N_EXPERTS - 1)
        n_used = (pad_end[-1:] // MOE_ROWS).astype(I32)
        key = idx * n + jnp.arange(n, dtype=I32)[:, None]
        tok_sorted = jnp.sort(key.reshape(-1)) % n
        grp_start = jnp.cumsum(counts) - counts
        slot = blk_row0[:, None] + jnp.arange(MOE_ROWS, dtype=I32)[None, :]
        compact = jnp.minimum(slot + (grp_start - pad_start)[blk_e][:, None],
                              (grp_start + counts - 1)[blk_e][:, None])
        src = tok_sorted.at[jnp.clip(compact, 0, n * TOP_K - 1).reshape(-1)].get(mode="promise_in_bounds")
        xs = xn.at[src + row0].get(mode="promise_in_bounds")
        routed.append((xs, blk_e, n_used, dest, gates_w, row0, n))
        row0 += n
    ys = [_experts(xs, blk_e, n_used, wgu, bgu, wd, bd) for xs, blk_e, n_used, _, _, _, _ in routed]
    out = None
    for p, (_, _, _, dest, gates_w, row0, n) in enumerate(routed):
        y4 = ys[p].at[dest.T.reshape(-1)].get(mode="promise_in_bounds").reshape(TOP_K, n, d)
        out = _combine(x, y4, gates_w, gate_mod, mod_map, g_final, final_norm, tm, row0, out)
    return out


def _moe_weights(w_gate_up, b_gate_up, w_down, b_down):
    ne, d, ff2 = w_gate_up.shape
    b_tiled = b_gate_up.reshape(ne, ff2 // (2 * LANES), LANES, 2).swapaxes(-1, -2).reshape(ne, 1, ff2)
    return w_gate_up, b_tiled, w_down, b_down[:, None, :]


def _rope_tables(n_tokens):
    rows = n_tokens // GRID_W
    row = jnp.repeat(jnp.arange(rows, dtype=I32), GRID_W).astype(F32)
    col = jnp.tile(jnp.arange(GRID_W, dtype=I32), rows).astype(F32)
    inv = ROPE_BASE ** (-jnp.arange(0, ROPE_AXIS_DIM, 2, dtype=F32) / ROPE_AXIS_DIM)
    ang_r, ang_c = row[:, None] * inv, col[:, None] * inv
    cos = jnp.concatenate([jnp.cos(ang_r)] * 2 + [jnp.cos(ang_c)] * 2, axis=-1)
    sin = jnp.concatenate([jnp.sin(ang_r)] * 2 + [jnp.sin(ang_c)] * 2, axis=-1)
    return cos, sin


def _rot_perm():
    q = ROPE_AXIS_DIM // 2
    d = np.arange(HEAD_DIM)
    first = (d % ROPE_AXIS_DIM) < q
    perm = np.where(first, d + q, d - q)
    sign = np.where(first, -1.0, 1.0).astype(np.float32)
    return perm, sign


def _rot_cols(w, n_heads):
    perm, sign = _rot_perm()
    k = w.shape[0]
    wh = w.reshape(k, n_heads, HEAD_DIM)
    return (wh[:, :, perm] * sign).reshape(k, n_heads * HEAD_DIM)


def _to_heads(t, b, n_heads):
    return t.reshape(b, -1, n_heads, HEAD_DIM).transpose(0, 2, 1, 3)


def _from_heads(t):
    b, h, tt, dh = t.shape
    return t.transpose(0, 2, 1, 3).reshape(b * tt, h * dh)


def _layer_even(xa, mod, b, s, lc, g_mix, g_ffn, w_in, w_out, gate_bias, mlstm_norm, sink, router, moe_w):
    d = xa.shape[1]
    nctx = b * lc
    tm = _tile(math.gcd(nctx, s), (512, 256, 128))
    mod_map = lambda i: jnp.where(i * tm < nctx, b, (i * tm - nctx) // s)
    sh1, sc1, gt1, sh2, sc2, gt2 = [mod[:, j][:, None, :] for j in range(6)]

    hm, hs, hk = MLSTM_HEADS * HEAD_DIM, SWA_HEADS * HEAD_DIM, SWA_KV_HEADS * HEAD_DIM
    o0 = np.cumsum([0, hm, hm, hm, hm, 4 * MLSTM_HEADS, hs, hk, hk])
    seg = lambda j: w_in[:, o0[j]:o0[j + 1]]
    w_cat = jnp.concatenate([seg(0), seg(1) * ATTN_SCALE, seg(2), seg(3), _pad_cols(seg(4))], axis=1).astype(BF16)
    w_t = jnp.concatenate([seg(5), _rot_cols(seg(5), SWA_HEADS), seg(6), _rot_cols(seg(6), SWA_KV_HEADS), seg(7)],
                          axis=1).T.astype(BF16)
    widths = [hm, hm, hm, hm, LANES]
    dts = [BF16] * 4 + [F32]
    qa, ka, va, oa, gts, qt, qt_r, kt, kt_r, vt = _in_proj(
        xa, g_mix, sc1, sh1, mod_map, w_cat, widths, dts, tm,
        wt=w_t, t_widths=[hs, hs, hk, hk, hk], t_dtypes=[BF16] * 5)

    hf, hb = _mlstm(qa, ka, va, gts, _pad_cols(gate_bias.astype(F32).reshape(1, -1)), b, lc, s)
    mix_a = _mlstm_out(hf, hb, oa, mlstm_norm.reshape(-1), tm)

    cos, sin = _rope_tables(s)
    ones = jnp.ones((HEAD_DIM,), F32)
    g = SWA_HEADS // SWA_KV_HEADS
    heads = lambda t, n: _to_heads(t, b, n)
    log2e = math.log2(math.e)
    q_x = _q_prep_t(qt, qt_r, nctx, b * s, cos.T, sin.T, ones, ones, ATTN_SCALE * log2e, norm=False)
    k_x = _q_prep_t(kt, kt_r, nctx, b * s, cos.T, sin.T, ones, ones, 1.0, norm=False)
    k_c, v_c = heads(kt[:, :nctx].T, SWA_KV_HEADS), heads(vt[:, :nctx].T, SWA_KV_HEADS)
    q_c = heads(qt[:, :nctx].T, SWA_HEADS)
    sink_h = sink.astype(F32).reshape(SWA_KV_HEADS, g, 1)
    att_x = _swa(q_x, k_x, (kt, vt), jnp.broadcast_to(sink_h * log2e, (SWA_KV_HEADS, g, 2 * SWA_BLOCK)), b, s, lc)
    att_c = _ctx_attn(q_c, k_c, v_c, jnp.broadcast_to(sink_h[..., None], (SWA_KV_HEADS, g, lc, 1))
                      .reshape(SWA_KV_HEADS, g * lc, 1), True)
    mix_b = jnp.concatenate([_from_heads(att_c), att_x], axis=0)

    w_router, b_router = router
    xa, xn2, logits = _out_proj(xa, 0, mix_a, mix_b, w_out[:hm].astype(BF16), w_out[hm:].astype(BF16), gt1, g_ffn,
                                sc2, sh2, mod_map, w_router.astype(F32), b_router.astype(F32), tm)
    return _moe(xa, xn2, logits, gt2, mod_map, moe_w, g_ffn, False, tm)


def _layer_odd_last(xa, mod, b, s, lc, g_mix, g_ffn, w_in, w_out, s5_params, d_skip, w_glu, b_glu,
                    q_norm, k_norm, router, moe_w, g_final):
    nctx = b * lc
    tm = _tile(math.gcd(nctx, s), (512, 256, 128))
    mod_map = lambda i: jnp.where(i * tm < nctx, b, (i * tm - nctx) // s)
    lat_map = lambda i: i * tm // s
    sh1, sc1, gt1, sh2, sc2, gt2 = [mod[:, j][:, None, :] for j in range(6)]

    hq, hk = ATT_HEADS * HEAD_DIM, ATT_KV_HEADS * HEAD_DIM
    o1 = np.cumsum([0, S5_CHANNELS, hq, hk, hk])
    seg = lambda j: w_in[:, o1[j]:o1[j + 1]]
    kpad = lambda w: _pad_cols(w, 2 * LANES)
    w_cat = jnp.concatenate([seg(0), kpad(seg(2)), kpad(_rot_cols(seg(2), ATT_KV_HEADS)), kpad(seg(3))],
                            axis=1).astype(BF16)
    w_q = jnp.concatenate([seg(1), _rot_cols(seg(1), ATT_HEADS)], axis=1).T.astype(BF16)
    widths = [S5_CHANNELS, 2 * LANES, 2 * LANES, 2 * LANES]
    dts = [F32, BF16, BF16, BF16]
    u, k, k_r, v, qt, qt_r = _in_proj(xa, g_mix, sc1, sh1, mod_map, w_cat, widths, dts, tm,
                                      wt=w_q, t_widths=[hq, hq], t_dtypes=[BF16, BF16])

    mix_a = _s5_mixer(u[nctx:].reshape(b, s, S5_CHANNELS), u[:nctx].reshape(b, lc, S5_CHANNELS),
                      _s5_tables(*s5_params), d_skip, w_glu, b_glu)

    cos, sin = _rope_tables(s)
    perm, _ = _rot_perm()
    qn, kn = q_norm.astype(F32), k_norm.astype(F32)
    heads = lambda t, n: _to_heads(t[:, :n * HEAD_DIM], b, n)
    q_x = _q_prep_t(qt, qt_r, nctx, b * s, cos.T, sin.T, qn, qn[perm], ATTN_SCALE * math.log2(math.e))
    k_x = _qk_prep(heads(k[nctx:], ATT_KV_HEADS), heads(k_r[nctx:], ATT_KV_HEADS), cos, sin, kn, kn[perm],
                   norm=True, rope=True, scale=1.0)
    kc_raw = heads(k[:nctx], ATT_KV_HEADS)
    k_c = _qk_prep(kc_raw, kc_raw, cos[:lc], sin[:lc], kn, kn, norm=True, rope=False, scale=1.0)
    k_all = jnp.concatenate([k_c, k_x], axis=2)
    v_all = jnp.concatenate([heads(v[:nctx], ATT_KV_HEADS), heads(v[nctx:], ATT_KV_HEADS)], axis=2)
    mix_b = _flash(q_x, k_all, v_all.swapaxes(-1, -2), b, s)

    w_router, b_router = router
    hs = S5_CHANNELS
    x, xn2, logits = _out_proj(xa, nctx, mix_a, mix_b, w_out[:hs].astype(BF16), w_out[hs:].astype(BF16), gt1, g_ffn,
                               sc2, sh2, lat_map, w_router.astype(F32), b_router.astype(F32), tm)
    return _moe(x, xn2, logits, gt2, lat_map, moe_w, g_final, True, tm)


def kernel(x, c, ctx, c_ctx, l0_w_mod, l0_b_mod, l0_g_mix, l0_g_ffn, l0_w_in, l0_w_out, l0_gate_bias, l0_mlstm_norm, l0_sink, l0_w_router, l0_b_router, l0_w_gate_up, l0_b_gate_up, l0_w_down, l0_b_down, l1_w_mod, l1_b_mod, l1_g_mix, l1_g_ffn, l1_w_in, l1_w_out, l1_lam_re, l1_lam_im, l1_log_dt, l1_b_re, l1_b_im, l1_c_re, l1_c_im, l1_d_skip, l1_w_glu, l1_b_glu, l1_q_norm, l1_k_norm, l1_w_router, l1_b_router, l1_w_gate_up, l1_b_gate_up, l1_w_down, l1_b_down, g_final):
    b, s, d = x.shape
    lc = ctx.shape[1]
    cond = jnp.concatenate([c, c_ctx[None, :]], axis=0)
    cond = jnp.pad(cond, ((0, (-(b + 1)) % 8), (0, 0)))
    mod0 = _silu_linear(cond, l0_w_mod, l0_b_mod)[:b + 1].reshape(b + 1, 6, d)
    mod1 = _silu_linear(cond, l1_w_mod, l1_b_mod)[:b + 1].reshape(b + 1, 6, d)

    moe0 = _moe_weights(l0_w_gate_up, l0_b_gate_up, l0_w_down, l0_b_down)
    moe1 = _moe_weights(l1_w_gate_up, l1_b_gate_up, l1_w_down, l1_b_down)
    xa = jnp.concatenate([ctx.reshape(b * lc, d), x.reshape(b * s, d)], axis=0)
    xa = _layer_even(xa, mod0, b, s, lc, l0_g_mix, l0_g_ffn, l0_w_in, l0_w_out, l0_gate_bias, l0_mlstm_norm,
                     l0_sink, (l0_w_router, l0_b_router), moe0)
    out = _layer_odd_last(xa, mod1, b, s, lc, l1_g_mix, l1_g_ffn, l1_w_in, l1_w_out,
                          (l1_lam_re, l1_lam_im, l1_log_dt, l1_b_re, l1_b_im, l1_c_re, l1_c_im),
                          l1_d_skip, l1_w_glu, l1_b_glu, l1_q_norm, l1_k_norm, (l1_w_router, l1_b_router),
                          moe1, g_final)
    return out.reshape(b, s, d)
```

```python
import functools
import math

import jax
import jax.numpy as jnp
import numpy as np
from jax import lax
from jax.experimental import pallas as pl
from jax.experimental.pallas import tpu as pltpu

F32 = jnp.float32
BF16 = jnp.bfloat16
I32 = jnp.int32

GRID_W = 64
HEAD_DIM = 64
ATTN_SCALE = HEAD_DIM ** -0.5
ROPE_AXIS_DIM = HEAD_DIM // 2
ROPE_BASE = 10000.0
EPS = 1e-6
NEG_INF = -1e30

MLSTM_HEADS = 8
MLSTM_CHUNKS = (256, 128, 64)
SWA_HEADS = 8
SWA_KV_HEADS = 2
SWA_WINDOW = 128
SWA_BLOCK = 128
S5_CHANNELS = 256
S5_GROUP = 16
S5_GROUPS = S5_CHANNELS // S5_GROUP
S5_STATE = 64
S5_CHUNK = 64
ATT_HEADS = 12
ATT_KV_HEADS = 3
N_EXPERTS = 32
TOP_K = 4
SWIGLU_LIMIT = 7.0
SWIGLU_ALPHA = 1.702

LANES = 128
VMEM_LIMIT = 56 * 1024 * 1024
MOE_ROWS = 512
MOE_SPLIT = (1, 1)
FLASH_DEPTH = 6
HIGHEST = lax.Precision.HIGHEST


def _params(*sem):
    return pltpu.CompilerParams(dimension_semantics=sem, vmem_limit_bytes=VMEM_LIMIT)


def _tile(n, prefs):
    for t in prefs:
        if n % t == 0:
            return t
    return n


def _pad_cols(w, mult=LANES):
    pad = (-w.shape[-1]) % mult
    if pad:
        w = jnp.pad(w, [(0, 0)] * (w.ndim - 1) + [(0, pad)])
    return w


def _linear_kernel(x_ref, w_ref, b_ref, o_ref):
    x = x_ref[...]
    x = x * jax.nn.sigmoid(x)
    o_ref[...] = jnp.dot(x, w_ref[...], precision=HIGHEST, preferred_element_type=F32) + b_ref[...]


def _silu_linear(x, w, b):
    m, k = x.shape
    n = w.shape[1]
    tn = _tile(n, (1024, 512, 256, 128))
    return pl.pallas_call(
        _linear_kernel,
        out_shape=jax.ShapeDtypeStruct((m, n), F32),
        grid=(n // tn,),
        in_specs=[pl.BlockSpec((m, k), lambda j: (0, 0)),
                  pl.BlockSpec((k, tn), lambda j: (0, j)),
                  pl.BlockSpec((1, tn), lambda j: (0, j))],
        out_specs=pl.BlockSpec((m, tn), lambda j: (0, j)),
        compiler_params=_params("arbitrary"),
        name="adaln_linear",
    )(x, w, b.reshape(1, n))


def _in_proj_kernel(x_ref, g_ref, sc_ref, sh_ref, w_ref, wt_ref, *out_refs, widths, t_widths):
    x = x_ref[...]
    xn = x * lax.rsqrt(jnp.mean(x * x, axis=-1, keepdims=True) + EPS) * g_ref[...]
    xb = (xn * (1.0 + sc_ref[0]) + sh_ref[0]).astype(BF16)
    off = 0
    for o_ref, w in zip(out_refs, widths):
        o_ref[...] = jnp.dot(xb, w_ref[:, off:off + w], preferred_element_type=F32).astype(o_ref.dtype)
        off += w
    off = 0
    for o_ref, w in zip(out_refs[len(widths):], t_widths):
        o_ref[...] = lax.dot_general(wt_ref[off:off + w, :], xb, (((1,), (1,)), ((), ())),
                                     preferred_element_type=F32).astype(o_ref.dtype)
        off += w


def _in_proj(x, g, scale, shift, mod_map, w, widths, dtypes, tm, wt=None, t_widths=(), t_dtypes=()):
    r, d = x.shape
    if wt is None:
        wt = jnp.zeros((8, d), BF16)
    return pl.pallas_call(
        functools.partial(_in_proj_kernel, widths=tuple(widths), t_widths=tuple(t_widths)),
        out_shape=([jax.ShapeDtypeStruct((r, wd), dt) for wd, dt in zip(widths, dtypes)]
                   + [jax.ShapeDtypeStruct((wd, r), dt) for wd, dt in zip(t_widths, t_dtypes)]),
        grid=(r // tm,),
        in_specs=[pl.BlockSpec((tm, d), lambda i: (i, 0)),
                  pl.BlockSpec((1, d), lambda i: (0, 0)),
                  pl.BlockSpec((1, 1, d), lambda i: (mod_map(i), 0, 0)),
                  pl.BlockSpec((1, 1, d), lambda i: (mod_map(i), 0, 0)),
                  pl.BlockSpec(w.shape, lambda i: (0, 0)),
                  pl.BlockSpec(wt.shape, lambda i: (0, 0))],
        out_specs=([pl.BlockSpec((tm, wd), lambda i: (i, 0)) for wd in widths]
                   + [pl.BlockSpec((wd, tm), lambda i: (0, i)) for wd in t_widths]),
        compiler_params=_params("parallel"),
        name="in_proj",
    )(x, g.reshape(1, d), scale, shift, w, wt)


def _qk_prep_kernel(x_ref, xr_ref, cos_ref, sin_ref, g_ref, gr_ref, o_ref, *, norm, rope, scale):
    x = x_ref[0].astype(F32)
    if norm:
        s = lax.rsqrt(jnp.mean(x * x, axis=-1, keepdims=True) + EPS)
        x = x * s * g_ref[...]
    if rope:
        xr = xr_ref[0].astype(F32)
        if norm:
            xr = xr * s * gr_ref[...]
        x = x * cos_ref[...] + xr * sin_ref[...]
    o_ref[0] = (x * scale).astype(o_ref.dtype)


def _qk_prep(x, x_rot, cos, sin, gain, gain_rot, *, norm, rope, scale):
    b, h, t, dh = x.shape
    tt = _tile(t, (512, 256, 128))
    blk = pl.BlockSpec((1, h, tt, dh), lambda i, j: (i, 0, j, 0))
    tab = pl.BlockSpec((tt, dh), lambda i, j: (j, 0))
    vec = pl.BlockSpec((1, dh), lambda i, j: (0, 0))
    return pl.pallas_call(
        functools.partial(_qk_prep_kernel, norm=norm, rope=rope, scale=scale),
        out_shape=jax.ShapeDtypeStruct(x.shape, BF16),
        grid=(b, t // tt),
        in_specs=[blk, blk, tab, tab, vec, vec],
        out_specs=blk,
        compiler_params=_params("parallel", "parallel"),
        name="qk_prep",
    )(x, x_rot, cos, sin, gain.reshape(1, dh), gain_rot.reshape(1, dh))


def _q_prep_t_kernel(x_ref, xr_ref, cos_ref, sin_ref, g_ref, gr_ref, o_ref, *, norm, scale):
    w, tt = x_ref.shape
    nh = w // HEAD_DIM
    x = x_ref[...].astype(F32).reshape(nh, HEAD_DIM, tt)
    xr = xr_ref[...].astype(F32).reshape(nh, HEAD_DIM, tt)
    if norm:
        s = lax.rsqrt(jnp.mean(x * x, axis=1, keepdims=True) + EPS)
        x, xr = x * s * g_ref[...], xr * s * gr_ref[...]
    y = x * cos_ref[...] + xr * sin_ref[...]
    o_ref[...] = (y * scale).reshape(w, tt).astype(o_ref.dtype)


def _q_prep_t(xt, xt_rot, col0, n_cols, cos_t, sin_t, gain, gain_rot, scale, norm=True):
    w = xt.shape[0]
    s = cos_t.shape[1]
    tt = _tile(math.gcd(col0, s), (512, 256, 128))
    blk = pl.BlockSpec((w, tt), lambda j: (0, j + col0 // tt))
    tab = pl.BlockSpec((HEAD_DIM, tt), lambda j: (0, j % (s // tt)))
    vec = pl.BlockSpec((HEAD_DIM, 1), lambda j: (0, 0))
    return pl.pallas_call(
        functools.partial(_q_prep_t_kernel, norm=norm, scale=scale),
        out_shape=jax.ShapeDtypeStruct((w, n_cols), BF16),
        grid=(n_cols // tt,),
        in_specs=[blk, blk, tab, tab, vec, vec],
        out_specs=pl.BlockSpec((w, tt), lambda j: (0, j)),
        compiler_params=_params("parallel"),
        name="q_prep_t",
    )(xt, xt_rot, cos_t, sin_t, gain.reshape(HEAD_DIM, 1), gain_rot.reshape(HEAD_DIM, 1))


def _log_sigmoid(x):
    return jnp.minimum(x, 0.0) - jnp.log(1.0 + jnp.exp(-jnp.abs(x)))


def _mlstm_kernel(qf, kf, vf, gf, qb, kb, vb, gb, bias_ref, hf_ref, hb_ref, c_scr, n_scr, m_scr):
    @pl.when(pl.program_id(1) == 0)
    def _():
        c_scr[...] = jnp.zeros_like(c_scr)
        n_scr[...] = jnp.zeros_like(n_scr)
        m_scr[...] = jnp.full_like(m_scr, NEG_INF)

    for d in range(2):
        _mlstm_stages((d,), qf, kf, vf, gf, qb, kb, vb, gb, bias_ref, hf_ref, hb_ref, c_scr, n_scr, m_scr)


def _mlstm_stages(dirs, qf, kf, vf, gf, qb, kb, vb, gb, bias_ref, hf_ref, hb_ref, c_scr, n_scr, m_scr):
    ch, nh, dh = qf.shape[0], MLSTM_HEADS, HEAD_DIM
    row = lax.broadcasted_iota(I32, (ch, ch), 0)
    col = lax.broadcasted_iota(I32, (ch, ch), 1)
    lane_lo = lax.broadcasted_iota(I32, (ch, 2 * dh), 1) < dh
    row_lo = lax.broadcasted_iota(I32, (2 * dh, 2 * dh), 0) < dh
    col_lo = lax.broadcasted_iota(I32, (2 * dh, 2 * dh), 1) < dh
    vec_lo = lax.broadcasted_iota(I32, (1, 2 * dh), 1) < dh
    pair = lambda a, b: jnp.where(lane_lo, a, b)
    refs = ((qf, kf, vf, gf, hf_ref), (qb, kb, vb, gb, hb_ref))
    jobs = [(d, p) for d in dirs for p in range(nh // 2)]
    sl = lambda p: slice(2 * dh * p, 2 * dh * (p + 1))

    qk = {}
    for d, p in jobs:
        q2, k2 = refs[d][0][:, sl(p)], refs[d][1][:, sl(p)]
        zero = jnp.zeros_like(q2)
        q_st = jnp.concatenate([jnp.where(lane_lo, q2, zero), jnp.where(lane_lo, zero, q2)], axis=0)
        qk[d, p] = lax.dot_general(q_st, k2, (((1,), (1,)), ((), ())), preferred_element_type=F32)

    st = {}
    for d in dirs:
        seen = (col <= row) if d == 0 else (col >= row)
        tri = seen.astype(F32)
        tri_t = ((row <= col) if d == 0 else (row >= col)).astype(F32)
        last = ch - 1 if d == 0 else 0
        g = refs[d][3][...] + bias_ref[...]
        g_t = g.T
        lo = 2 * nh * d
        log2e = math.log2(math.e)
        li_col, lf_col = g[:, lo:lo + nh] * log2e, _log_sigmoid(g[:, lo + nh:lo + 2 * nh]) * log2e
        li_row, lf_row = g_t[lo:lo + nh, :] * log2e, _log_sigmoid(g_t[lo + nh:lo + 2 * nh, :]) * log2e
        b_col = jnp.dot(tri, lf_col, precision=HIGHEST, preferred_element_type=F32)
        b_row = jnp.dot(lf_row, tri_t, precision=HIGHEST, preferred_element_type=F32)
        src_row = li_row - b_row
        for h in range(nh):
            bc, br = b_col[:, h:h + 1], b_row[h:h + 1, :]
            lic = li_col[:, h:h + 1]
            b_last = br[:, last:last + 1]
            m_prev = m_scr[d, h // 2][:, (h % 2) * dh:(h % 2) * dh + 1]
            d_log = jnp.where(seen, bc + src_row[h:h + 1, :], NEG_INF)
            inter_log = bc + m_prev
            m_t = jnp.maximum(inter_log, jnp.max(d_log, axis=1, keepdims=True))
            w_log = b_last - bc + lic
            m_new = jnp.maximum(b_last + m_prev, jnp.max(w_log, axis=0, keepdims=True))
            st[d, h] = dict(dmat=jnp.exp2(d_log - m_t), inter=jnp.exp2(inter_log - m_t), floor=jnp.exp2(-m_t),
                            m_new=m_new, decay=jnp.exp2(b_last + m_prev - m_new), wn=jnp.exp2(w_log - m_new))

    mm = {}
    for d, p in jobs:
        a, b = st[d, 2 * p], st[d, 2 * p + 1]
        q2, k2, v2 = refs[d][0][:, sl(p)], refs[d][1][:, sl(p)], refs[d][2][:, sl(p)]
        s = qk[d, p] * jnp.concatenate([a["dmat"], b["dmat"]], axis=0)
        kw = k2.astype(F32) * pair(a["wn"], b["wn"])
        mm[d, p] = dict(
            s_sum=jnp.sum(s, axis=1, keepdims=True), kw_sum=jnp.sum(kw, axis=0, keepdims=True),
            sv=jnp.dot(s.astype(BF16), v2, preferred_element_type=F32),
            q_c=jnp.dot(q2, c_scr[d, p].astype(BF16), preferred_element_type=F32),
            kv=lax.dot_general(kw.astype(BF16), v2, (((0,), (0,)), ((), ())), preferred_element_type=F32))

    for d, p in jobs:
        a, b, r = st[d, 2 * p], st[d, 2 * p + 1], mm[d, p]
        q2 = refs[d][0][:, sl(p)]
        n_prev = n_scr[d, p]
        num = pair(a["inter"], b["inter"]) * r["q_c"] + jnp.where(lane_lo, r["sv"][:ch], r["sv"][ch:])
        qn = q2.astype(F32) * n_prev
        qn_a = jnp.sum(jnp.where(lane_lo, qn, 0.0), axis=1, keepdims=True)
        qn_b = jnp.sum(jnp.where(lane_lo, 0.0, qn), axis=1, keepdims=True)
        den_a = jnp.maximum(jnp.abs(a["inter"] * qn_a + r["s_sum"][:ch]), a["floor"])
        den_b = jnp.maximum(jnp.abs(b["inter"] * qn_b + r["s_sum"][ch:]), b["floor"])
        refs[d][4][:, sl(p)] = num / pair(den_a, den_b)
        c_scr[d, p] = (jnp.where(row_lo, a["decay"], b["decay"]) * c_scr[d, p]
                       + jnp.where(row_lo == col_lo, r["kv"], 0.0))
        n_scr[d, p] = jnp.where(vec_lo, a["decay"], b["decay"]) * n_prev + r["kw_sum"]
        m_scr[d, p] = jnp.where(vec_lo, a["m_new"], b["m_new"])


def _mlstm(q, k, v, gates, bias, b, lc, s):
    r, w = q.shape
    ch = _tile(math.gcd(lc, s), MLSTM_CHUNKS)
    ncc, ncx = lc // ch, s // ch
    base = b * ncc

    def fwd(i, c):
        return jnp.where(c < ncc, i * ncc + c, base + i * ncx + (c - ncc)), 0

    def bwd(i, c):
        return jnp.where(c < ncc, i * ncc + (ncc - 1 - c), base + i * ncx + (ncx - 1 - (c - ncc))), 0

    spec = lambda width, m: pl.BlockSpec((ch, width), m)
    npair = MLSTM_HEADS // 2
    return pl.pallas_call(
        _mlstm_kernel,
        out_shape=[jax.ShapeDtypeStruct((r, w), F32)] * 2,
        grid=(b, ncc + ncx),
        in_specs=[spec(w, fwd), spec(w, fwd), spec(w, fwd), spec(LANES, fwd),
                  spec(w, bwd), spec(w, bwd), spec(w, bwd), spec(LANES, bwd),
                  pl.BlockSpec((1, LANES), lambda i, c: (0, 0))],
        out_specs=[spec(w, fwd), spec(w, bwd)],
        scratch_shapes=[pltpu.VMEM((2, npair, 2 * HEAD_DIM, 2 * HEAD_DIM), F32),
                        pltpu.VMEM((2, npair, 1, 2 * HEAD_DIM), F32),
                        pltpu.VMEM((2, npair, 1, 2 * HEAD_DIM), F32)],
        compiler_params=_params("parallel", "arbitrary"),
        name="mlstm_scan",
    )(q, k, v, gates, q, k, v, gates, bias)


def _mlstm_out_kernel(hf_ref, hb_ref, o_ref, nrm_ref, ones_ref, y_ref):
    h = hf_ref[...] + hb_ref[...]
    ms = jnp.dot((h * h).astype(BF16), ones_ref[...], preferred_element_type=F32) * (1.0 / HEAD_DIM)
    hn = h * lax.rsqrt(ms + EPS) * nrm_ref[...]
    y_ref[...] = (jax.nn.sigmoid(o_ref[...].astype(F32)) * hn).astype(y_ref.dtype)


def _head_ones(width):
    idx = np.arange(width) // HEAD_DIM
    return jnp.asarray(idx[:, None] == idx[None, :], BF16)


def _mlstm_out(hf, hb, o, norm, tm):
    r, w = hf.shape
    blk = pl.BlockSpec((tm, w), lambda i: (i, 0))
    return pl.pallas_call(
        _mlstm_out_kernel,
        out_shape=jax.ShapeDtypeStruct((r, w), BF16),
        grid=(r // tm,),
        in_specs=[blk, blk, blk,
                  pl.BlockSpec((1, w), lambda i: (0, 0)),
                  pl.BlockSpec((w, w), lambda i: (0, 0))],
        out_specs=blk,
        compiler_params=_params("parallel"),
        name="mlstm_out",
    )(hf, hb, o, norm.reshape(1, w), _head_ones(w))


def _swa_kernel(q_ref, k0, k1, k2, k3, v0, v1, v2, v3, kc_ref, vc_ref, sink_ref, o_ref, *, seq):
    w, dh = SWA_BLOCK, HEAD_DIM
    tq = q_ref.shape[1]
    g = q_ref.shape[0] // dh
    i = pl.program_id(2)
    k_loc = jnp.concatenate([k0[...], k1[...], k2[...], k3[...]], axis=1)
    v_loc = jnp.concatenate([v0[...], v1[...], v2[...], v3[...]], axis=1)
    kpos = (2 * i - 1) * w + lax.broadcasted_iota(I32, (4 * w, tq), 0)
    qpos = i * tq + lax.broadcasted_iota(I32, (4 * w, tq), 1)
    valid = (jnp.abs(qpos - kpos) <= SWA_WINDOW) & (kpos >= 0) & (kpos < seq)
    t_dot = lambda kt, q: lax.dot_general(kt, q, (((0,), (0,)), ((), ())), preferred_element_type=F32)
    qs = [q_ref[h * dh:(h + 1) * dh, :] for h in range(g)]
    s_loc = [jnp.where(valid, t_dot(k_loc, qs[h]), NEG_INF) for h in range(g)]
    s_ctx = [t_dot(kc_ref[...], qs[h]) for h in range(g)]
    outs = []
    for h in range(g):
        sink = sink_ref[0, h:h + 1, :]
        m = jnp.maximum(jnp.maximum(jnp.max(s_loc[h], axis=0, keepdims=True),
                                    jnp.max(s_ctx[h], axis=0, keepdims=True)), sink)
        p_loc = jnp.exp2(s_loc[h] - m)
        p_ctx = jnp.exp2(s_ctx[h] - m)
        den = jnp.sum(p_loc, axis=0, keepdims=True) + jnp.sum(p_ctx, axis=0, keepdims=True) + jnp.exp2(sink - m)
        o = (jnp.dot(v_loc, p_loc.astype(BF16), preferred_element_type=F32)
             + jnp.dot(vc_ref[...], p_ctx.astype(BF16), preferred_element_type=F32))
        outs.append(o / den)
    o_ref[...] = jnp.concatenate(outs, axis=0).T.astype(o_ref.dtype)


def _swa(qt, kt, kvt, sink2, b, s, lc):
    dh = HEAD_DIM
    nh, hkv = qt.shape[0] // dh, kt.shape[0] // dh
    g = nh // hkv
    w = SWA_BLOCK
    tq = 2 * w
    nb, nq = s // w, s // tq
    c0 = b * lc // w
    clampi = lambda j: jnp.clip(j, 0, nb - 1)
    kspec = lambda o: pl.BlockSpec((dh, w), lambda bi, hi, i: (hi, bi * nb + clampi(2 * i + o)))
    vspec = lambda o: pl.BlockSpec((dh, w), lambda bi, hi, i: (hi, c0 + bi * nb + clampi(2 * i + o)))
    cspec = pl.BlockSpec((dh, lc), lambda bi, hi, i: (hi, bi))
    k_all, v_all = kvt
    return pl.pallas_call(
        functools.partial(_swa_kernel, seq=s),
        out_shape=jax.ShapeDtypeStruct((b * s, nh * dh), BF16),
        grid=(b, hkv, nq),
        in_specs=[pl.BlockSpec((g * dh, tq), lambda bi, hi, i: (hi, bi * nq + i)),
                  kspec(-1), kspec(0), kspec(1), kspec(2), vspec(-1), vspec(0), vspec(1), vspec(2),
                  cspec, cspec, pl.BlockSpec((1, g, tq), lambda bi, hi, i: (hi, 0, 0))],
        out_specs=pl.BlockSpec((tq, g * dh), lambda bi, hi, i: (bi * nq + i, hi)),
        compiler_params=_params("parallel", "parallel", "arbitrary"),
        name="window_attention",
    )(qt, kt, kt, kt, kt, v_all, v_all, v_all, v_all, k_all, v_all, sink2)


def _ctx_attn_kernel(q_ref, k_ref, v_ref, sink_ref, o_ref, *, use_sink):
    g, lq, dh = q_ref.shape[1:]
    q = q_ref[0].reshape(g * lq, dh)
    s = lax.dot_general(q, k_ref[0, 0], (((1,), (1,)), ((), ())), preferred_element_type=F32) * ATTN_SCALE
    m = jnp.max(s, axis=1, keepdims=True)
    if use_sink:
        m = jnp.maximum(m, sink_ref[0])
    p = jnp.exp(s - m)
    den = jnp.sum(p, axis=1, keepdims=True)
    if use_sink:
        den = den + jnp.exp(sink_ref[0] - m)
    o = jnp.dot(p.astype(BF16), v_ref[0, 0], preferred_element_type=F32) / den
    o_ref[0] = o.reshape(g, lq, dh).astype(o_ref.dtype)


def _ctx_attn(q, k, v, sink_col, use_sink):
    b, nh, lq, dh = q.shape
    hkv = k.shape[1]
    g = nh // hkv
    qblk = pl.BlockSpec((1, g, lq, dh), lambda bi, hi: (bi, hi, 0, 0))
    kblk = pl.BlockSpec((1, 1, k.shape[2], dh), lambda bi, hi: (bi, hi, 0, 0))
    return pl.pallas_call(
        functools.partial(_ctx_attn_kernel, use_sink=use_sink),
        out_shape=jax.ShapeDtypeStruct(q.shape, BF16),
        grid=(b, hkv),
        in_specs=[qblk, kblk, kblk, pl.BlockSpec((1, g * lq, 1), lambda bi, hi: (hi, 0, 0))],
        out_specs=qblk,
        compiler_params=_params("parallel", "parallel"),
        name="context_attention",
    )(q, k, v, sink_col)


def _flash_kernel(q_ref, k_ref, v_ref, o_ref, m_scr, acc_scr, *bufs, tk):
    dh = HEAD_DIM
    g = q_ref.shape[0] // dh
    n = k_ref.shape[2] // tk
    nb = len(bufs) // 2
    depth = nb - 1
    s_bufs, c_bufs = bufs[:nb], bufs[nb:]
    m_scr[...] = jnp.full_like(m_scr, NEG_INF)
    acc_scr[...] = jnp.zeros_like(acc_scr)

    def scores(j, slot):
        kk = k_ref[0, 0, pl.ds(pl.multiple_of(j * tk, tk), tk), :]
        for h in range(g):
            sc = jnp.dot(kk, q_ref[h * dh:(h + 1) * dh, :], preferred_element_type=F32)
            s_bufs[slot][h] = sc
            c_bufs[slot][h] = jnp.max(sc, axis=0, keepdims=True)

    def update(j, slot):
        vv = v_ref[0, 0, :, pl.ds(pl.multiple_of(j * tk, tk), tk)]
        m_olds = [m_scr[h] for h in range(g)]
        m_news = [jnp.maximum(m_olds[h], c_bufs[slot][h]) for h in range(g)]
        ps = [jnp.exp2(s_bufs[slot][h] - m_news[h]).astype(BF16) for h in range(g)]
        pvs = [jnp.dot(vv, ps[h], preferred_element_type=F32) for h in range(g)]
        for h in range(g):
            acc_scr[h] = jnp.exp2(m_olds[h] - m_news[h]) * acc_scr[h] + pvs[h]
            m_scr[h] = m_news[h]

    for j in range(min(depth, n)):
        scores(j, j % nb)
    loops = max(n - depth, 0) // nb

    def body(i, carry):
        for r in range(nb):
            scores(nb * i + r + depth, (r + depth) % nb)
            update(nb * i + r, r)
        return carry

    lax.fori_loop(0, loops, body, 0)
    for j in range(loops * nb, n):
        if j + depth < n:
            scores(j + depth, (j + depth) % nb)
        update(j, j % nb)
    acc = acc_scr[...]
    out = (acc[:, :dh] / acc[:, dh:dh + 1]).reshape(g * dh, -1)
    o_ref[...] = out.T.astype(o_ref.dtype)


def _flash(qt, k, vt, b, s, tq=None, tk=None):
    w = qt.shape[0]
    hkv, t, dh = k.shape[1], k.shape[2], k.shape[3]
    g = w // dh // hkv
    tq = tq or _tile(s, (256, 128))
    tk = tk or _tile(t, (256, 128))
    nq = s // tq
    ones_rows = 8
    vt = jnp.concatenate([vt, jnp.ones((b, hkv, ones_rows, t), vt.dtype)], axis=2)
    nv = dh + ones_rows
    return pl.pallas_call(
        functools.partial(_flash_kernel, tk=tk),
        out_shape=jax.ShapeDtypeStruct((b * s, w), BF16),
        grid=(b, hkv, nq),
        in_specs=[pl.BlockSpec((g * dh, tq), lambda bi, hi, i: (hi, bi * nq + i)),
                  pl.BlockSpec((1, 1, t, dh), lambda bi, hi, i: (bi, hi, 0, 0)),
                  pl.BlockSpec((1, 1, nv, t), lambda bi, hi, i: (bi, hi, 0, 0))],
        out_specs=pl.BlockSpec((tq, g * dh), lambda bi, hi, i: (bi * nq + i, hi)),
        scratch_shapes=([pltpu.VMEM((g, 1, tq), F32), pltpu.VMEM((g, nv, tq), F32)]
                        + [pltpu.VMEM((g, tk, tq), F32)] * (FLASH_DEPTH + 1)
                        + [pltpu.VMEM((g, 1, tq), F32)] * (FLASH_DEPTH + 1)),
        compiler_params=_params("parallel", "parallel", "arbitrary"),
        name="dense_attention",
    )(qt, k, vt)


def _bmm_kernel(a_ref, b_ref, o_ref):
    o_ref[0] = jnp.dot(a_ref[0], b_ref[0], preferred_element_type=F32).astype(o_ref.dtype)


def _group_matmul(a, bmat, name):
    g, m, k = a.shape
    n = bmat.shape[2]
    tm = _tile(m, (1024, 512, 256, 128, 64, 32, 16, 8))
    return pl.pallas_call(
        _bmm_kernel,
        out_shape=jax.ShapeDtypeStruct((g, m, n), F32),
        grid=(g, m // tm),
        in_specs=[pl.BlockSpec((1, tm, k), lambda gi, i: (gi, i, 0)),
                  pl.BlockSpec((1, k, n), lambda gi, i: (gi, 0, 0))],
        out_specs=pl.BlockSpec((1, tm, n), lambda gi, i: (gi, i, 0)),
        compiler_params=_params("parallel", "parallel"),
        name=name,
    )(a, bmat)


def _s5_scan_kernel(sre_ref, sim_ref, are_ref, aim_ref, zre_ref, zim_ref):
    nd, nsteps = sre_ref.shape[0], sre_ref.shape[1]
    for d in range(nd):
        a_re, a_im = are_ref[d], aim_ref[d]

        def body(i, carry, d=d, a_re=a_re, a_im=a_im):
            z_re, z_im = carry
            zre_ref[d, i] = z_re
            zim_ref[d, i] = z_im
            return (a_re * z_re - a_im * z_im + sre_ref[d, i], a_re * z_im + a_im * z_re + sim_ref[d, i])

        zero = jnp.zeros(sre_ref.shape[2:], F32)
        lax.fori_loop(0, nsteps, body, (zero, zero))


def _s5_scan(s_re, s_im, a_re, a_im):
    full = lambda arr: pl.BlockSpec(arr.shape, lambda i: (0,) * arr.ndim)
    return pl.pallas_call(
        _s5_scan_kernel,
        out_shape=[jax.ShapeDtypeStruct(s_re.shape, F32)] * 2,
        grid=(1,),
        in_specs=[full(s_re), full(s_im), full(a_re), full(a_im)],
        out_specs=[full(s_re), full(s_im)],
        compiler_params=_params("arbitrary"),
        name="s5_chunk_scan",
    )(s_re, s_im, a_re, a_im)


def _s5_glu_kernel(y_ref, u_ref, d_ref, w_ref, b_ref, o_ref):
    y = y_ref[...] + d_ref[...] * u_ref[...]
    y = jax.nn.gelu(y)
    gate = jnp.dot(y.astype(BF16), w_ref[...], preferred_element_type=F32) + b_ref[...]
    o_ref[...] = (y * jax.nn.sigmoid(gate)).astype(o_ref.dtype)


def _s5_glu(y, u, d_skip, w_glu, b_glu, tm):
    r, c = y.shape
    blk = pl.BlockSpec((tm, c), lambda i: (i, 0))
    vec = pl.BlockSpec((1, c), lambda i: (0, 0))
    return pl.pallas_call(
        _s5_glu_kernel,
        out_shape=jax.ShapeDtypeStruct((r, c), BF16),
        grid=(r // tm,),
        in_specs=[blk, blk, vec, pl.BlockSpec((c, c), lambda i: (0, 0)), vec],
        out_specs=blk,
        compiler_params=_params("parallel"),
        name="s5_readout_glu",
    )(y, u, d_skip.reshape(1, c), w_glu.astype(BF16), b_glu.reshape(1, c))


def _s5_tables(lam_re, lam_im, log_dt, b_re, b_im, c_re, c_im):
    ln, p, gc = S5_CHUNK, S5_STATE, S5_GROUP
    lam = lax.complex(lam_re.astype(F32), lam_im.astype(F32))
    dt = jnp.exp(log_dt.astype(F32))[..., None]
    a_bar = jnp.exp(lam * dt)
    b_scale = (a_bar - 1.0) / lam
    b_mat = lax.complex(b_re.astype(F32), b_im.astype(F32))
    c_mat = lax.complex(c_re.astype(F32), c_im.astype(F32))
    tau = jnp.arange(ln + 1, dtype=F32)
    apow = jnp.exp((lam * dt)[:, :, None, :] * tau[None, None, :, None])
    drive = b_scale[..., None] * b_mat[None]
    kern = jnp.real(jnp.einsum('gcp,dgtp,dgpe->dgtce', c_mat, apow[:, :, :ln], drive))
    kc = jnp.concatenate([kern[1, :, :0:-1], (kern[0, :, :1] + kern[1, :, :1]), kern[0, :, 1:]], axis=1)
    kc_e = kc.transpose(0, 3, 1, 2).reshape(S5_GROUPS, gc, (2 * ln - 1) * gc)
    toep = jnp.stack([kc_e[:, :, (ln - 1 - s) * gc:(2 * ln - 1 - s) * gc] for s in range(ln)], axis=1)
    toep = toep.reshape(S5_GROUPS, ln * gc, ln * gc)
    w_f = apow[0, :, ln - 1::-1][:, :ln, :, None] * drive[0][:, None]
    w_b = apow[1, :, :ln, :, None] * drive[1][:, None]
    def m_in(wc):
        wt = wc.transpose(0, 1, 3, 2).reshape(S5_GROUPS, ln * gc, p)
        return jnp.concatenate([jnp.real(wt), jnp.imag(wt)], axis=-1)
    min_all = jnp.concatenate([m_in(w_f), m_in(w_b)], axis=-1)
    o_f = c_mat[:, None] * apow[0, :, 1:ln + 1][:, :, None, :]
    o_b = c_mat[:, None] * apow[1, :, ln:0:-1][:, :, None, :]
    def m_out(oc):
        ot = oc.transpose(0, 3, 1, 2).reshape(S5_GROUPS, p, ln * gc)
        return jnp.concatenate([jnp.real(ot), -jnp.imag(ot)], axis=1)
    rhs = jnp.concatenate([toep, m_out(o_f), m_out(o_b)], axis=1)
    a_l = apow[:, :, ln].reshape(2, 1, S5_GROUPS * p)
    return min_all.astype(BF16), rhs.astype(BF16), jnp.real(a_l), jnp.imag(a_l)


def _s5_mixer(u_x, u_c, tables, d_skip, w_glu, b_glu):
    min_all, rhs, a_re, a_im = tables
    b, s, _ = u_x.shape
    ln, p, gc, ng = S5_CHUNK, S5_STATE, S5_GROUP, S5_GROUPS

    def chunks(u):
        nc = u.shape[1] // ln
        return u.reshape(b, nc, ln, ng, gc).transpose(3, 0, 1, 2, 4).reshape(ng, b * nc, ln * gc).astype(BF16), nc

    ux, ncx = chunks(u_x)
    uc, ncc = chunks(u_c)
    sx = _group_matmul(ux, min_all, "s5_local_state").reshape(ng, b, ncx, 4, p)
    sc = _group_matmul(uc, min_all, "s5_local_state_ctx").reshape(ng, b, ncc, 4, p)

    def scan_order(part):
        f = jnp.concatenate([sc[:, :, :, part], sx[:, :, :, part]], axis=2)
        r = jnp.concatenate([sc[:, :, ::-1, part + 2], sx[:, :, ::-1, part + 2]], axis=2)
        return jnp.stack([f, r]).transpose(0, 3, 2, 1, 4).reshape(2, ncc + ncx, b, ng * p)

    z_re, z_im = _s5_scan(scan_order(0), scan_order(1), a_re, a_im)

    def latent(z, d):
        zl = z[d, ncc:]
        if d == 1:
            zl = zl[::-1]
        return zl.reshape(ncx, b, ng, p).transpose(2, 1, 0, 3).reshape(ng, b * ncx, p)

    lhs = jnp.concatenate([ux, latent(z_re, 0).astype(BF16), latent(z_im, 0).astype(BF16),
                           latent(z_re, 1).astype(BF16), latent(z_im, 1).astype(BF16)], axis=-1)
    y = _group_matmul(lhs, rhs, "s5_outputs")
    y = y.reshape(ng, b, ncx, ln, gc).transpose(1, 2, 3, 0, 4).reshape(b * s, S5_CHANNELS)
    return _s5_glu(y, u_x.reshape(b * s, S5_CHANNELS), d_skip, w_glu, b_glu, _tile(b * s, (1024, 512, 256)))


def _out_proj_kernel(x_ref, a_ref, b_ref, wa_ref, wb_ref, gt_ref, g_ref, sc_ref, sh_ref, wr_ref, br_ref,
                     xo_ref, xn_ref, lg_ref):
    y = (jnp.dot(a_ref[...], wa_ref[...], preferred_element_type=F32)
         + jnp.dot(b_ref[...], wb_ref[...], preferred_element_type=F32))
    x = x_ref[...] + gt_ref[0] * y
    xo_ref[...] = x
    xn = x * lax.rsqrt(jnp.mean(x * x, axis=-1, keepdims=True) + EPS) * g_ref[...]
    xn = xn * (1.0 + sc_ref[0]) + sh_ref[0]
    hi = xn.astype(BF16)
    xn_ref[...] = hi
    lo = (xn - hi.astype(F32)).astype(BF16)
    part = jnp.dot(hi, wr_ref[...], preferred_element_type=F32)
    part = part[:, :LANES] + part[:, LANES:] + jnp.dot(lo, wr_ref[:, :LANES], preferred_element_type=F32)
    lg_ref[...] = part[:, :lg_ref.shape[1]] + br_ref[...]


def _out_proj(x, row0, a, bmix, wa, wb, gate, g_ffn, scale, shift, mod_map, w_router, b_router, tm):
    r = a.shape[0]
    d = x.shape[1]
    t0 = row0 // tm
    ne = w_router.shape[1]
    w_hi = w_router.astype(BF16)
    w_lo = (w_router - w_hi.astype(F32)).astype(BF16)
    w_router = jnp.concatenate([_pad_cols(w_hi), _pad_cols(w_lo)], axis=1)
    row = lambda w: pl.BlockSpec((tm, w), lambda i: (i, 0))
    mod = pl.BlockSpec((1, 1, d), lambda i: (mod_map(i), 0, 0))
    full = lambda arr: pl.BlockSpec(arr.shape, lambda i: (0, 0))
    return pl.pallas_call(
        _out_proj_kernel,
        out_shape=[jax.ShapeDtypeStruct((r, d), F32), jax.ShapeDtypeStruct((r, d), BF16),
                   jax.ShapeDtypeStruct((r, ne), F32)],
        grid=(r // tm,),
        in_specs=[pl.BlockSpec((tm, d), lambda i: (i + t0, 0)), row(a.shape[1]), row(bmix.shape[1]),
                  full(wa), full(wb), mod, pl.BlockSpec((1, d), lambda i: (0, 0)), mod, mod,
                  full(w_router), pl.BlockSpec((1, ne), lambda i: (0, 0))],
        out_specs=[row(d), row(d), row(ne)],
        compiler_params=_params("parallel"),
        name="out_proj",
    )(x, a, bmix, wa, wb, gate, g_ffn.reshape(1, d), scale, shift, w_router, b_router.reshape(1, ne))


def _route_kernel(lg_ref, idx_ref, gate_ref, pos_ref, cnt_ref, carry):
    tm, ne = lg_ref.shape

    @pl.when(pl.program_id(0) == 0)
    def _():
        carry[...] = jnp.zeros_like(carry)

    work = lg_ref[...]
    lane = lax.broadcasted_iota(I32, (tm, ne), 1).astype(F32)
    out_lane = lax.broadcasted_iota(I32, (tm, LANES), 1)
    vals, hots, idx_out = [], [], jnp.zeros((tm, LANES), I32)
    for kk in range(TOP_K):
        mx = jnp.max(work, axis=1, keepdims=True)
        idx = jnp.min(jnp.where(work == mx, lane, float(ne)), axis=1, keepdims=True)
        hot = lane == idx
        work = jnp.where(hot, -jnp.inf, work)
        vals.append(mx)
        hots.append(hot)
        idx_out = jnp.where(out_lane == kk, idx.astype(I32), idx_out)
    exps = [jnp.exp(vv - vals[0]) for vv in vals]
    tot = exps[0] + exps[1] + exps[2] + exps[3]
    multi = (hots[0] | hots[1] | hots[2] | hots[3]).astype(BF16)
    r_i = lax.broadcasted_iota(I32, (tm, tm), 0)
    c_i = lax.broadcasted_iota(I32, (tm, tm), 1)
    before = (r_i > c_i).astype(BF16)
    prefix = jnp.dot(before, multi, preferred_element_type=F32) + carry[...]
    gate_out = jnp.zeros((tm, LANES), F32)
    pos_out = jnp.zeros((tm, LANES), I32)
    for kk in range(TOP_K):
        gate_out = jnp.where(out_lane == kk, exps[kk] / tot, gate_out)
        pos = jnp.sum(jnp.where(hots[kk], prefix, 0.0), axis=1, keepdims=True).astype(I32)
        pos_out = jnp.where(out_lane == kk, pos, pos_out)
    idx_ref[...] = idx_out
    gate_ref[...] = gate_out
    pos_ref[...] = pos_out
    carry[...] = carry[...] + jnp.sum(multi.astype(F32), axis=0, keepdims=True)
    cnt_ref[...] = carry[...]


def _route(logits, tm, row0, n):
    ne = logits.shape[1]
    t0 = row0 // tm
    wide = pl.BlockSpec((tm, LANES), lambda i: (i, 0))
    return pl.pallas_call(
        _route_kernel,
        out_shape=[jax.ShapeDtypeStruct((n, LANES), I32), jax.ShapeDtypeStruct((n, LANES), F32),
                   jax.ShapeDtypeStruct((n, LANES), I32), jax.ShapeDtypeStruct((1, ne), F32)],
        grid=(n // tm,),
        in_specs=[pl.BlockSpec((tm, ne), lambda i: (i + t0, 0))],
        out_specs=[wide, wide, wide, pl.BlockSpec((1, ne), lambda i: (0, 0))],
        scratch_shapes=[pltpu.VMEM((1, ne), F32)],
        compiler_params=_params("arbitrary"),
        name="moe_route",
    )(logits)


def _expert_kernel(be_ref, nb_ref, x_ref, wgu_ref, bgu_ref, wd_ref, bd_ref, perm_ref, o_ref, wgu_bf, wd_bf):
    i = pl.program_id(0)
    fresh = jnp.logical_or(i == 0, be_ref[i] != be_ref[jnp.maximum(i - 1, 0)])

    @pl.when(jnp.logical_and(i < nb_ref[0], fresh))
    def _():
        tile = perm_ref.shape[0]
        for t in range(wgu_ref.shape[2] // tile):
            sl = slice(t * tile, (t + 1) * tile)
            wgu_bf[:, sl] = jnp.dot(wgu_ref[0, :, sl].astype(BF16), perm_ref[...],
                                    preferred_element_type=F32).astype(BF16)
        wd_bf[...] = wd_ref[0].astype(BF16)

    @pl.when(i < nb_ref[0])
    def _():
        h = jnp.dot(x_ref[...], wgu_bf[...], preferred_element_type=F32) + bgu_ref[0]
        acts = []
        for t in range(h.shape[1] // (2 * LANES)):
            glu = jnp.minimum(h[:, 2 * LANES * t:2 * LANES * t + LANES], SWIGLU_LIMIT)
            lin = jnp.clip(h[:, 2 * LANES * t + LANES:2 * LANES * (t + 1)], -SWIGLU_LIMIT, SWIGLU_LIMIT)
            acts.append((glu * jax.nn.sigmoid(SWIGLU_ALPHA * glu) * (lin + 1.0)).astype(BF16))
        act = jnp.concatenate(acts, axis=1)
        o_ref[...] = (jnp.dot(act, wd_bf[...], preferred_element_type=F32) + bd_ref[0]).astype(o_ref.dtype)

    @pl.when(i >= nb_ref[0])
    def _():
        o_ref[...] = jnp.zeros_like(o_ref)


def _experts(xs, blk_e, n_used, wgu, bgu, wd, bd):
    n, d = xs.shape
    ff2 = wgu.shape[2]
    nblk = n // MOE_ROWS
    tile = 2 * LANES
    j = np.arange(tile)
    col = np.where(j < LANES, 2 * j, 2 * (j - LANES) + 1)
    perm = jnp.asarray(np.arange(tile)[:, None] == col[None, :], BF16)
    wspec = lambda shape: pl.BlockSpec((1,) + shape, lambda i, be, nb: (be[i], 0, 0))
    return pl.pallas_call(
        _expert_kernel,
        out_shape=jax.ShapeDtypeStruct((n, d), BF16),
        grid_spec=pltpu.PrefetchScalarGridSpec(
            num_scalar_prefetch=2,
            grid=(nblk,),
            in_specs=[pl.BlockSpec((MOE_ROWS, d), lambda i, be, nb: (i, 0)),
                      wspec((d, ff2)), wspec((1, ff2)), wspec((ff2 // 2, d)), wspec((1, d)),
                      pl.BlockSpec((tile, tile), lambda i, be, nb: (0, 0))],
            out_specs=pl.BlockSpec((MOE_ROWS, d), lambda i, be, nb: (i, 0)),
            scratch_shapes=[pltpu.VMEM((d, ff2), BF16), pltpu.VMEM((ff2 // 2, d), BF16)]),
        compiler_params=_params("arbitrary"),
        name="moe_experts",
    )(blk_e, n_used, xs, wgu, bgu, wd, bd, perm)


def _combine_kernel(x_ref, y_ref, gate_ref, gt_ref, g_ref, o_ref, *, final_norm):
    gates = gate_ref[...]
    y = gates[:, 0:1] * y_ref[0].astype(F32)
    for kk in range(1, TOP_K):
        y = y + gates[:, kk:kk + 1] * y_ref[kk].astype(F32)
    x = x_ref[...] + gt_ref[0] * y
    if final_norm:
        x = x * lax.rsqrt(jnp.mean(x * x, axis=-1, keepdims=True) + EPS) * g_ref[...]
    o_ref[...] = x


def _combine_alias_kernel(x_ref, y_ref, gate_ref, gt_ref, g_ref, prev_ref, o_ref, *, final_norm):
    del prev_ref
    _combine_kernel(x_ref, y_ref, gate_ref, gt_ref, g_ref, o_ref, final_norm=final_norm)


def _combine(x, y4, gates, gate_mod, mod_map, g_final, final_norm, tm, row0, prev):
    r, d = x.shape
    n = y4.shape[1]
    t0 = row0 // tm
    in_specs = [pl.BlockSpec((tm, d), lambda i: (i + t0, 0)),
                pl.BlockSpec((TOP_K, tm, d), lambda i: (0, i, 0)),
                pl.BlockSpec((tm, LANES), lambda i: (i, 0)),
                pl.BlockSpec((1, 1, d), lambda i: (mod_map(i + t0), 0, 0)),
                pl.BlockSpec((1, d), lambda i: (0, 0))]
    args = [x, y4, gates, gate_mod, g_final.reshape(1, d)]
    body, aliases = _combine_kernel, {}
    if prev is not None:
        in_specs.append(pl.BlockSpec(memory_space=pl.ANY))
        args.append(prev)
        body, aliases = _combine_alias_kernel, {len(args) - 1: 0}
    return pl.pallas_call(
        functools.partial(body, final_norm=final_norm),
        out_shape=jax.ShapeDtypeStruct((r, d), F32),
        grid=(n // tm,),
        in_specs=in_specs,
        out_specs=pl.BlockSpec((tm, d), lambda i: (i + t0, 0)),
        input_output_aliases=aliases,
        compiler_params=_params("parallel"),
        name="moe_combine",
    )(*args)


def _moe_parts(tiles):
    total = sum(MOE_SPLIT)
    sizes = [tiles * w // total for w in MOE_SPLIT[:-1]]
    sizes.append(tiles - sum(sizes))
    return [sz for sz in sizes if sz > 0]


def _moe(x, xn, logits, gate_mod, mod_map, weights, g_final, final_norm, tm):
    wgu, bgu, wd, bd = weights
    r, d = xn.shape
    routed, row0 = [], 0
    for tiles in _moe_parts(r // tm):
        n = tiles * tm
        nblk = n * TOP_K // MOE_ROWS + N_EXPERTS
        blk_row0 = jnp.arange(nblk, dtype=I32) * MOE_ROWS
        idx_w, gates_w, pos_w, counts = _route(logits, tm, row0, n)
        idx, pos = idx_w[:, :TOP_K], pos_w[:, :TOP_K]
        counts = counts[0].astype(I32)
        padded = (counts + MOE_ROWS - 1) // MOE_ROWS * MOE_ROWS
        pad_end = jnp.cumsum(padded)
        pad_start = pad_end - padded
        dest = pad_start[idx] + pos
        blk_e = jnp.minimum(jnp.sum((pad_end[None, :] <= blk_row0[:, None]).astype(I32), axis=1), N_EXPERTS - 1)
        n_used = (pad_end[-1:] // MOE_ROWS).astype(I32)
        key = idx * n + jnp.arange(n, dtype=I32)[:, None]
        tok_sorted = jnp.sort(key.reshape(-1)) % n
        grp_start = jnp.cumsum(counts) - counts
        slot = blk_row0[:, None] + jnp.arange(MOE_ROWS, dtype=I32)[None, :]
        compact = jnp.minimum(slot + (grp_start - pad_start)[blk_e][:, None],
                              (grp_start + counts - 1)[blk_e][:, None])
        src = tok_sorted.at[jnp.clip(compact, 0, n * TOP_K - 1).reshape(-1)].get(mode="promise_in_bounds")
        xs = xn.at[src + row0].get(mode="promise_in_bounds")
        routed.append((xs, blk_e, n_used, dest, gates_w, row0, n))
        row0 += n
    ys = [_experts(xs, blk_e, n_used, wgu, bgu, wd, bd) for xs, blk_e, n_used, _, _, _, _ in routed]
    out = None
    for p, (_, _, _, dest, gates_w, row0, n) in enumerate(routed):
        y4 = ys[p].at[dest.T.reshape(-1)].get(mode="promise_in_bounds").reshape(TOP_K, n, d)
        out = _combine(x, y4, gates_w, gate_mod, mod_map, g_final, final_norm, tm, row0, out)
    return out


def _moe_weights(w_gate_up, b_gate_up, w_down, b_down):
    ne, d, ff2 = w_gate_up.shape
    b_tiled = b_gate_up.reshape(ne, ff2 // (2 * LANES), LANES, 2).swapaxes(-1, -2).reshape(ne, 1, ff2)
    return w_gate_up, b_tiled, w_down, b_down[:, None, :]


def _rope_tables(n_tokens):
    rows = n_tokens // GRID_W
    row = jnp.repeat(jnp.arange(rows, dtype=I32), GRID_W).astype(F32)
    col = jnp.tile(jnp.arange(GRID_W, dtype=I32), rows).astype(F32)
    inv = ROPE_BASE ** (-jnp.arange(0, ROPE_AXIS_DIM, 2, dtype=F32) / ROPE_AXIS_DIM)
    ang_r, ang_c = row[:, None] * inv, col[:, None] * inv
    cos = jnp.concatenate([jnp.cos(ang_r)] * 2 + [jnp.cos(ang_c)] * 2, axis=-1)
    sin = jnp.concatenate([jnp.sin(ang_r)] * 2 + [jnp.sin(ang_c)] * 2, axis=-1)
    return cos, sin


def _rot_perm():
    q = ROPE_AXIS_DIM // 2
    d = np.arange(HEAD_DIM)
    first = (d % ROPE_AXIS_DIM) < q
    perm = np.where(first, d + q, d - q)
    sign = np.where(first, -1.0, 1.0).astype(np.float32)
    return perm, sign


def _rot_cols(w, n_heads):
    perm, sign = _rot_perm()
    k = w.shape[0]
    wh = w.reshape(k, n_heads, HEAD_DIM)
    return (wh[:, :, perm] * sign).reshape(k, n_heads * HEAD_DIM)


def _to_heads(t, b, n_heads):
    return t.reshape(b, -1, n_heads, HEAD_DIM).transpose(0, 2, 1, 3)


def _from_heads(t):
    b, h, tt, dh = t.shape
    return t.transpose(0, 2, 1, 3).reshape(b * tt, h * dh)


def _layer_even(xa, mod, b, s, lc, g_mix, g_ffn, w_in, w_out, gate_bias, mlstm_norm, sink, router, moe_w):
    d = xa.shape[1]
    nctx = b * lc
    tm = _tile(math.gcd(nctx, s), (512, 256, 128))
    mod_map = lambda i: jnp.where(i * tm < nctx, b, (i * tm - nctx) // s)
    sh1, sc1, gt1, sh2, sc2, gt2 = [mod[:, j][:, None, :] for j in range(6)]

    hm, hs, hk = MLSTM_HEADS * HEAD_DIM, SWA_HEADS * HEAD_DIM, SWA_KV_HEADS * HEAD_DIM
    o0 = np.cumsum([0, hm, hm, hm, hm, 4 * MLSTM_HEADS, hs, hk, hk])
    seg = lambda j: w_in[:, o0[j]:o0[j + 1]]
    w_cat = jnp.concatenate([seg(0), seg(1) * ATTN_SCALE, seg(2), seg(3), _pad_cols(seg(4))], axis=1).astype(BF16)
    w_t = jnp.concatenate([seg(5), _rot_cols(seg(5), SWA_HEADS), seg(6), _rot_cols(seg(6), SWA_KV_HEADS), seg(7)],
                          axis=1).T.astype(BF16)
    widths = [hm, hm, hm, hm, LANES]
    dts = [BF16] * 4 + [F32]
    qa, ka, va, oa, gts, qt, qt_r, kt, kt_r, vt = _in_proj(
        xa, g_mix, sc1, sh1, mod_map, w_cat, widths, dts, tm,
        wt=w_t, t_widths=[hs, hs, hk, hk, hk], t_dtypes=[BF16] * 5)

    hf, hb = _mlstm(qa, ka, va, gts, _pad_cols(gate_bias.astype(F32).reshape(1, -1)), b, lc, s)
    mix_a = _mlstm_out(hf, hb, oa, mlstm_norm.reshape(-1), tm)

    cos, sin = _rope_tables(s)
    ones = jnp.ones((HEAD_DIM,), F32)
    g = SWA_HEADS // SWA_KV_HEADS
    heads = lambda t, n: _to_heads(t, b, n)
    log2e = math.log2(math.e)
    q_x = _q_prep_t(qt, qt_r, nctx, b * s, cos.T, sin.T, ones, ones, ATTN_SCALE * log2e, norm=False)
    k_x = _q_prep_t(kt, kt_r, nctx, b * s, cos.T, sin.T, ones, ones, 1.0, norm=False)
    k_c, v_c = heads(kt[:, :nctx].T, SWA_KV_HEADS), heads(vt[:, :nctx].T, SWA_KV_HEADS)
    q_c = heads(qt[:, :nctx].T, SWA_HEADS)
    sink_h = sink.astype(F32).reshape(SWA_KV_HEADS, g, 1)
    att_x = _swa(q_x, k_x, (kt, vt), jnp.broadcast_to(sink_h * log2e, (SWA_KV_HEADS, g, 2 * SWA_BLOCK)), b, s, lc)
    att_c = _ctx_attn(q_c, k_c, v_c, jnp.broadcast_to(sink_h[..., None], (SWA_KV_HEADS, g, lc, 1))
                      .reshape(SWA_KV_HEADS, g * lc, 1), True)
    mix_b = jnp.concatenate([_from_heads(att_c), att_x], axis=0)

    w_router, b_router = router
    xa, xn2, logits = _out_proj(xa, 0, mix_a, mix_b, w_out[:hm].astype(BF16), w_out[hm:].astype(BF16), gt1, g_ffn,
                                sc2, sh2, mod_map, w_router.astype(F32), b_router.astype(F32), tm)
    return _moe(xa, xn2, logits, gt2, mod_map, moe_w, g_ffn, False, tm)


def _layer_odd_last(xa, mod, b, s, lc, g_mix, g_ffn, w_in, w_out, s5_params, d_skip, w_glu, b_glu,
                    q_norm, k_norm, router, moe_w, g_final):
    nctx = b * lc
    tm = _tile(math.gcd(nctx, s), (512, 256, 128))
    mod_map = lambda i: jnp.where(i * tm < nctx, b, (i * tm - nctx) // s)
    lat_map = lambda i: i * tm // s
    sh1, sc1, gt1, sh2, sc2, gt2 = [mod[:, j][:, None, :] for j in range(6)]

    hq, hk = ATT_HEADS * HEAD_DIM, ATT_KV_HEADS * HEAD_DIM
    o1 = np.cumsum([0, S5_CHANNELS, hq, hk, hk])
    seg = lambda j: w_in[:, o1[j]:o1[j + 1]]
    kpad = lambda w: _pad_cols(w, 2 * LANES)
    w_cat = jnp.concatenate([seg(0), kpad(seg(2)), kpad(_rot_cols(seg(2), ATT_KV_HEADS)), kpad(seg(3))],
                            axis=1).astype(BF16)
    w_q = jnp.concatenate([seg(1), _rot_cols(seg(1), ATT_HEADS)], axis=1).T.astype(BF16)
    widths = [S5_CHANNELS, 2 * LANES, 2 * LANES, 2 * LANES]
    dts = [F32, BF16, BF16, BF16]
    u, k, k_r, v, qt, qt_r = _in_proj(xa, g_mix, sc1, sh1, mod_map, w_cat, widths, dts, tm,
                                      wt=w_q, t_widths=[hq, hq], t_dtypes=[BF16, BF16])

    mix_a = _s5_mixer(u[nctx:].reshape(b, s, S5_CHANNELS), u[:nctx].reshape(b, lc, S5_CHANNELS),
                      _s5_tables(*s5_params), d_skip, w_glu, b_glu)

    cos, sin = _rope_tables(s)
    perm, _ = _rot_perm()
    qn, kn = q_norm.astype(F32), k_norm.astype(F32)
    heads = lambda t, n: _to_heads(t[:, :n * HEAD_DIM], b, n)
    q_x = _q_prep_t(qt, qt_r, nctx, b * s, cos.T, sin.T, qn, qn[perm], ATTN_SCALE * math.log2(math.e))
    k_x = _qk_prep(heads(k[nctx:], ATT_KV_HEADS), heads(k_r[nctx:], ATT_KV_HEADS), cos, sin, kn, kn[perm],
                   norm=True, rope=True, scale=1.0)
    kc_raw = heads(k[:nctx], ATT_KV_HEADS)
    k_c = _qk_prep(kc_raw, kc_raw, cos[:lc], sin[:lc], kn, kn, norm=True, rope=False, scale=1.0)
    k_all = jnp.concatenate([k_c, k_x], axis=2)
    v_all = jnp.concatenate([heads(v[:nctx], ATT_KV_HEADS), heads(v[nctx:], ATT_KV_HEADS)], axis=2)
    mix_b = _flash(q_x, k_all, v_all.swapaxes(-1, -2), b, s)

    w_router, b_router = router
    hs = S5_CHANNELS
    x, xn2, logits = _out_proj(xa, nctx, mix_a, mix_b, w_out[:hs].astype(BF16), w_out[hs:].astype(BF16), gt1, g_ffn,
                               sc2, sh2, lat_map, w_router.astype(F32), b_router.astype(F32), tm)
    return _moe(x, xn2, logits, gt2, lat_map, moe_w, g_final, True, tm)


def kernel(x, c, ctx, c_ctx, l0_w_mod, l0_b_mod, l0_g_mix, l0_g_ffn, l0_w_in, l0_w_out, l0_gate_bias, l0_mlstm_norm, l0_sink, l0_w_router, l0_b_router, l0_w_gate_up, l0_b_gate_up, l0_w_down, l0_b_down, l1_w_mod, l1_b_mod, l1_g_mix, l1_g_ffn, l1_w_in, l1_w_out, l1_lam_re, l1_lam_im, l1_log_dt, l1_b_re, l1_b_im, l1_c_re, l1_c_im, l1_d_skip, l1_w_glu, l1_b_glu, l1_q_norm, l1_k_norm, l1_w_router, l1_b_router, l1_w_gate_up, l1_b_gate_up, l1_w_down, l1_b_down, g_final):
    b, s, d = x.shape
    lc = ctx.shape[1]
    cond = jnp.concatenate([c, c_ctx[None, :]], axis=0)
    cond = jnp.pad(cond, ((0, (-(b + 1)) % 8), (0, 0)))
    mod0 = _silu_linear(cond, l0_w_mod, l0_b_mod)[:b + 1].reshape(b + 1, 6, d)
    mod1 = _silu_linear(cond, l1_w_mod, l1_b_mod)[:b + 1].reshape(b + 1, 6, d)

    moe0 = _moe_weights(l0_w_gate_up, l0_b_gate_up, l0_w_down, l0_b_down)
    moe1 = _moe_weights(l1_w_gate_up, l1_b_gate_up, l1_w_down, l1_b_down)
    xa = jnp.concatenate([ctx.reshape(b * lc, d), x.reshape(b * s, d)], axis=0)
    xa = _layer_even(xa, mod0, b, s, lc, l0_g_mix, l0_g_ffn, l0_w_in, l0_w_out, l0_gate_bias, l0_mlstm_norm,
                     l0_sink, (l0_w_router, l0_b_router), moe0)
    out = _layer_odd_last(xa, mod1, b, s, lc, l1_g_mix, l1_g_ffn, l1_w_in, l1_w_out,
                          (l1_lam_re, l1_lam_im, l1_log_dt, l1_b_re, l1_b_im, l1_c_re, l1_c_im),
                          l1_d_skip, l1_w_glu, l1_b_glu, l1_q_norm, l1_k_norm, (l1_w_router, l1_b_router),
                          moe1, g_final)
    return out.reshape(b, s, d)
```

```python
import functools
import math

import jax
import jax.numpy as jnp
import numpy as np
from jax import lax
from jax.experimental import pallas as pl
from jax.experimental.pallas import tpu as pltpu

F32 = jnp.float32
BF16 = jnp.bfloat16
I32 = jnp.int32

GRID_W = 64
HEAD_DIM = 64
ATTN_SCALE = HEAD_DIM ** -0.5
ROPE_AXIS_DIM = HEAD_DIM // 2
ROPE_BASE = 10000.0
EPS = 1e-6
NEG_INF = -1e30

MLSTM_HEADS = 8
MLSTM_CHUNKS = (256, 128, 64)
SWA_HEADS = 8
SWA_KV_HEADS = 2
SWA_WINDOW = 128
SWA_BLOCK = 128
S5_CHANNELS = 256
S5_GROUP = 16
S5_GROUPS = S5_CHANNELS // S5_GROUP
S5_STATE = 64
S5_CHUNK = 64
ATT_HEADS = 12
ATT_KV_HEADS = 3
N_EXPERTS = 32
TOP_K = 4
SWIGLU_LIMIT = 7.0
SWIGLU_ALPHA = 1.702

LANES = 128
VMEM_LIMIT = 56 * 1024 * 1024
MOE_ROWS = 512
MOE_SPLIT = (1, 1)
FLASH_DEPTH = 6
HIGHEST = lax.Precision.HIGHEST


def _params(*sem):
    return pltpu.CompilerParams(dimension_semantics=sem, vmem_limit_bytes=VMEM_LIMIT)


def _tile(n, prefs):
    for t in prefs:
        if n % t == 0:
            return t
    return n


def _pad_cols(w, mult=LANES):
    pad = (-w.shape[-1]) % mult
    if pad:
        w = jnp.pad(w, [(0, 0)] * (w.ndim - 1) + [(0, pad)])
    return w


def _linear_kernel(x_ref, w_ref, b_ref, o_ref):
    x = x_ref[...]
    x = x * jax.nn.sigmoid(x)
    o_ref[...] = jnp.dot(x, w_ref[...], precision=HIGHEST, preferred_element_type=F32) + b_ref[...]


def _silu_linear(x, w, b):
    m, k = x.shape
    n = w.shape[1]
    tn = _tile(n, (1024, 512, 256, 128))
    return pl.pallas_call(
        _linear_kernel,
        out_shape=jax.ShapeDtypeStruct((m, n), F32),
        grid=(n // tn,),
        in_specs=[pl.BlockSpec((m, k), lambda j: (0, 0)),
                  pl.BlockSpec((k, tn), lambda j: (0, j)),
                  pl.BlockSpec((1, tn), lambda j: (0, j))],
        out_specs=pl.BlockSpec((m, tn), lambda j: (0, j)),
        compiler_params=_params("arbitrary"),
        name="adaln_linear",
    )(x, w, b.reshape(1, n))


def _in_proj_kernel(x_ref, g_ref, sc_ref, sh_ref, w_ref, wt_ref, *out_refs, widths, t_widths):
    x = x_ref[...]
    xn = x * lax.rsqrt(jnp.mean(x * x, axis=-1, keepdims=True) + EPS) * g_ref[...]
    xb = (xn * (1.0 + sc_ref[0]) + sh_ref[0]).astype(BF16)
    off = 0
    for o_ref, w in zip(out_refs, widths):
        o_ref[...] = jnp.dot(xb, w_ref[:, off:off + w], preferred_element_type=F32).astype(o_ref.dtype)
        off += w
    off = 0
    for o_ref, w in zip(out_refs[len(widths):], t_widths):
        o_ref[...] = lax.dot_general(wt_ref[off:off + w, :], xb, (((1,), (1,)), ((), ())),
                                     preferred_element_type=F32).astype(o_ref.dtype)
        off += w


def _in_proj(x, g, scale, shift, mod_map, w, widths, dtypes, tm, wt=None, t_widths=(), t_dtypes=()):
    r, d = x.shape
    if wt is None:
        wt = jnp.zeros((8, d), BF16)
    return pl.pallas_call(
        functools.partial(_in_proj_kernel, widths=tuple(widths), t_widths=tuple(t_widths)),
        out_shape=([jax.ShapeDtypeStruct((r, wd), dt) for wd, dt in zip(widths, dtypes)]
                   + [jax.ShapeDtypeStruct((wd, r), dt) for wd, dt in zip(t_widths, t_dtypes)]),
        grid=(r // tm,),
        in_specs=[pl.BlockSpec((tm, d), lambda i: (i, 0)),
                  pl.BlockSpec((1, d), lambda i: (0, 0)),
                  pl.BlockSpec((1, 1, d), lambda i: (mod_map(i), 0, 0)),
                  pl.BlockSpec((1, 1, d), lambda i: (mod_map(i), 0, 0)),
                  pl.BlockSpec(w.shape, lambda i: (0, 0)),
                  pl.BlockSpec(wt.shape, lambda i: (0, 0))],
        out_specs=([pl.BlockSpec((tm, wd), lambda i: (i, 0)) for wd in widths]
                   + [pl.BlockSpec((wd, tm), lambda i: (0, i)) for wd in t_widths]),
        compiler_params=_params("parallel"),
        name="in_proj",
    )(x, g.reshape(1, d), scale, shift, w, wt)


def _qk_prep_kernel(x_ref, xr_ref, cos_ref, sin_ref, g_ref, gr_ref, o_ref, *, norm, rope, scale):
    x = x_ref[0].astype(F32)
    if norm:
        s = lax.rsqrt(jnp.mean(x * x, axis=-1, keepdims=True) + EPS)
        x = x * s * g_ref[...]
    if rope:
        xr = xr_ref[0].astype(F32)
        if norm:
            xr = xr * s * gr_ref[...]
        x = x * cos_ref[...] + xr * sin_ref[...]
    o_ref[0] = (x * scale).astype(o_ref.dtype)


def _qk_prep(x, x_rot, cos, sin, gain, gain_rot, *, norm, rope, scale):
    b, h, t, dh = x.shape
    tt = _tile(t, (512, 256, 128))
    blk = pl.BlockSpec((1, h, tt, dh), lambda i, j: (i, 0, j, 0))
    tab = pl.BlockSpec((tt, dh), lambda i, j: (j, 0))
    vec = pl.BlockSpec((1, dh), lambda i, j: (0, 0))
    return pl.pallas_call(
        functools.partial(_qk_prep_kernel, norm=norm, rope=rope, scale=scale),
        out_shape=jax.ShapeDtypeStruct(x.shape, BF16),
        grid=(b, t // tt),
        in_specs=[blk, blk, tab, tab, vec, vec],
        out_specs=blk,
        compiler_params=_params("parallel", "parallel"),
        name="qk_prep",
    )(x, x_rot, cos, sin, gain.reshape(1, dh), gain_rot.reshape(1, dh))


def _q_prep_t_kernel(x_ref, xr_ref, cos_ref, sin_ref, g_ref, gr_ref, o_ref, *, norm, scale):
    w, tt = x_ref.shape
    nh = w // HEAD_DIM
    x = x_ref[...].astype(F32).reshape(nh, HEAD_DIM, tt)
    xr = xr_ref[...].astype(F32).reshape(nh, HEAD_DIM, tt)
    if norm:
        s = lax.rsqrt(jnp.mean(x * x, axis=1, keepdims=True) + EPS)
        x, xr = x * s * g_ref[...], xr * s * gr_ref[...]
    y = x * cos_ref[...] + xr * sin_ref[...]
    o_ref[...] = (y * scale).reshape(w, tt).astype(o_ref.dtype)


def _q_prep_t(xt, xt_rot, col0, n_cols, cos_t, sin_t, gain, gain_rot, scale, norm=True):
    w = xt.shape[0]
    s = cos_t.shape[1]
    tt = _tile(math.gcd(col0, s), (512, 256, 128))
    blk = pl.BlockSpec((w, tt), lambda j: (0, j + col0 // tt))
    tab = pl.BlockSpec((HEAD_DIM, tt), lambda j: (0, j % (s // tt)))
    vec = pl.BlockSpec((HEAD_DIM, 1), lambda j: (0, 0))
    return pl.pallas_call(
        functools.partial(_q_prep_t_kernel, norm=norm, scale=scale),
        out_shape=jax.ShapeDtypeStruct((w, n_cols), BF16),
        grid=(n_cols // tt,),
        in_specs=[blk, blk, tab, tab, vec, vec],
        out_specs=pl.BlockSpec((w, tt), lambda j: (0, j)),
        compiler_params=_params("parallel"),
        name="q_prep_t",
    )(xt, xt_rot, cos_t, sin_t, gain.reshape(HEAD_DIM, 1), gain_rot.reshape(HEAD_DIM, 1))


def _log_sigmoid(x):
    return jnp.minimum(x, 0.0) - jnp.log(1.0 + jnp.exp(-jnp.abs(x)))


def _mlstm_kernel(qf, kf, vf, gf, qb, kb, vb, gb, bias_ref, hf_ref, hb_ref, c_scr, n_scr, m_scr):
    @pl.when(pl.program_id(1) == 0)
    def _():
        c_scr[...] = jnp.zeros_like(c_scr)
        n_scr[...] = jnp.zeros_like(n_scr)
        m_scr[...] = jnp.full_like(m_scr, NEG_INF)

    gates = [_mlstm_gates(d, g_ref, bias_ref) for d, g_ref in enumerate((gf, gb))]
    for d in range(2):
        _mlstm_stages((d,), gates, qf, kf, vf, qb, kb, vb, hf_ref, hb_ref, c_scr, n_scr, m_scr)


def _mlstm_gates(d, g_ref, bias_ref):
    ch, nh = g_ref.shape[0], MLSTM_HEADS
    row = lax.broadcasted_iota(I32, (ch, ch), 0)
    col = lax.broadcasted_iota(I32, (ch, ch), 1)
    seen = (col <= row) if d == 0 else (col >= row)
    tri = seen.astype(F32)
    tri_t = ((row <= col) if d == 0 else (row >= col)).astype(F32)
    g = g_ref[...] + bias_ref[...]
    g_t = g.T
    lo = 2 * nh * d
    log2e = math.log2(math.e)
    li_col, lf_col = g[:, lo:lo + nh] * log2e, _log_sigmoid(g[:, lo + nh:lo + 2 * nh]) * log2e
    li_row, lf_row = g_t[lo:lo + nh, :] * log2e, _log_sigmoid(g_t[lo + nh:lo + 2 * nh, :]) * log2e
    b_col = jnp.dot(tri, lf_col, precision=HIGHEST, preferred_element_type=F32)
    b_row = jnp.dot(lf_row, tri_t, precision=HIGHEST, preferred_element_type=F32)
    return seen, b_col, b_row, li_col, li_row - b_row


def _mlstm_stages(dirs, gates, qf, kf, vf, qb, kb, vb, hf_ref, hb_ref, c_scr, n_scr, m_scr):
    ch, nh, dh = qf.shape[0], MLSTM_HEADS, HEAD_DIM
    lane_lo = lax.broadcasted_iota(I32, (ch, 2 * dh), 1) < dh
    row_lo = lax.broadcasted_iota(I32, (2 * dh, 2 * dh), 0) < dh
    col_lo = lax.broadcasted_iota(I32, (2 * dh, 2 * dh), 1) < dh
    vec_lo = lax.broadcasted_iota(I32, (1, 2 * dh), 1) < dh
    pair = lambda a, b: jnp.where(lane_lo, a, b)
    refs = ((qf, kf, vf, hf_ref), (qb, kb, vb, hb_ref))
    jobs = [(d, p) for d in dirs for p in range(nh // 2)]
    sl = lambda p: slice(2 * dh * p, 2 * dh * (p + 1))

    qk = {}
    for d, p in jobs:
        q2, k2 = refs[d][0][:, sl(p)], refs[d][1][:, sl(p)]
        zero = jnp.zeros_like(q2)
        q_st = jnp.concatenate([jnp.where(lane_lo, q2, zero), jnp.where(lane_lo, zero, q2)], axis=0)
        qk[d, p] = lax.dot_general(q_st, k2, (((1,), (1,)), ((), ())), preferred_element_type=F32)

    st = {}
    for d in dirs:
        seen, b_col, b_row, li_col, src_row = gates[d]
        last = ch - 1 if d == 0 else 0
        for h in range(nh):
            bc, br = b_col[:, h:h + 1], b_row[h:h + 1, :]
            lic = li_col[:, h:h + 1]
            b_last = br[:, last:last + 1]
            m_prev = m_scr[d, h // 2][:, (h % 2) * dh:(h % 2) * dh + 1]
            d_log = jnp.where(seen, bc + src_row[h:h + 1, :], NEG_INF)
            inter_log = bc + m_prev
            m_t = jnp.maximum(inter_log, jnp.max(d_log, axis=1, keepdims=True))
            w_log = b_last - bc + lic
            m_new = jnp.maximum(b_last + m_prev, jnp.max(w_log, axis=0, keepdims=True))
            st[d, h] = dict(dmat=jnp.exp2(d_log - m_t), inter=jnp.exp2(inter_log - m_t), floor=jnp.exp2(-m_t),
                            m_new=m_new, decay=jnp.exp2(b_last + m_prev - m_new), wn=jnp.exp2(w_log - m_new))

    mm = {}
    for d, p in jobs:
        a, b = st[d, 2 * p], st[d, 2 * p + 1]
        q2, k2, v2 = refs[d][0][:, sl(p)], refs[d][1][:, sl(p)], refs[d][2][:, sl(p)]
        s = qk[d, p] * jnp.concatenate([a["dmat"], b["dmat"]], axis=0)
        kw = k2.astype(F32) * pair(a["wn"], b["wn"])
        mm[d, p] = dict(
            s_sum=jnp.sum(s, axis=1, keepdims=True), kw_sum=jnp.sum(kw, axis=0, keepdims=True),
            sv=jnp.dot(s.astype(BF16), v2, preferred_element_type=F32),
            q_c=jnp.dot(q2, c_scr[d, p].astype(BF16), preferred_element_type=F32),
            kv=lax.dot_general(kw.astype(BF16), v2, (((0,), (0,)), ((), ())), preferred_element_type=F32))

    for d, p in jobs:
        a, b, r = st[d, 2 * p], st[d, 2 * p + 1], mm[d, p]
        q2 = refs[d][0][:, sl(p)]
        n_prev = n_scr[d, p]
        num = pair(a["inter"], b["inter"]) * r["q_c"] + jnp.where(lane_lo, r["sv"][:ch], r["sv"][ch:])
        qn = q2.astype(F32) * n_prev
        qn_a = jnp.sum(jnp.where(lane_lo, qn, 0.0), axis=1, keepdims=True)
        qn_b = jnp.sum(jnp.where(lane_lo, 0.0, qn), axis=1, keepdims=True)
        den_a = jnp.maximum(jnp.abs(a["inter"] * qn_a + r["s_sum"][:ch]), a["floor"])
        den_b = jnp.maximum(jnp.abs(b["inter"] * qn_b + r["s_sum"][ch:]), b["floor"])
        refs[d][3][:, sl(p)] = num / pair(den_a, den_b)
        c_scr[d, p] = (jnp.where(row_lo, a["decay"], b["decay"]) * c_scr[d, p]
                       + jnp.where(row_lo == col_lo, r["kv"], 0.0))
        n_scr[d, p] = jnp.where(vec_lo, a["decay"], b["decay"]) * n_prev + r["kw_sum"]
        m_scr[d, p] = jnp.where(vec_lo, a["m_new"], b["m_new"])


def _mlstm(q, k, v, gates, bias, b, lc, s):
    r, w = q.shape
    ch = _tile(math.gcd(lc, s), MLSTM_CHUNKS)
    ncc, ncx = lc // ch, s // ch
    base = b * ncc

    def fwd(i, c):
        return jnp.where(c < ncc, i * ncc + c, base + i * ncx + (c - ncc)), 0

    def bwd(i, c):
        return jnp.where(c < ncc, i * ncc + (ncc - 1 - c), base + i * ncx + (ncx - 1 - (c - ncc))), 0

    spec = lambda width, m: pl.BlockSpec((ch, width), m)
    npair = MLSTM_HEADS // 2
    return pl.pallas_call(
        _mlstm_kernel,
        out_shape=[jax.ShapeDtypeStruct((r, w), F32)] * 2,
        grid=(b, ncc + ncx),
        in_specs=[spec(w, fwd), spec(w, fwd), spec(w, fwd), spec(LANES, fwd),
                  spec(w, bwd), spec(w, bwd), spec(w, bwd), spec(LANES, bwd),
                  pl.BlockSpec((1, LANES), lambda i, c: (0, 0))],
        out_specs=[spec(w, fwd), spec(w, bwd)],
        scratch_shapes=[pltpu.VMEM((2, npair, 2 * HEAD_DIM, 2 * HEAD_DIM), F32),
                        pltpu.VMEM((2, npair, 1, 2 * HEAD_DIM), F32),
                        pltpu.VMEM((2, npair, 1, 2 * HEAD_DIM), F32)],
        compiler_params=_params("parallel", "arbitrary"),
        name="mlstm_scan",
    )(q, k, v, gates, q, k, v, gates, bias)


def _mlstm_out_kernel(hf_ref, hb_ref, o_ref, nrm_ref, ones_ref, y_ref):
    h = hf_ref[...] + hb_ref[...]
    ms = jnp.dot((h * h).astype(BF16), ones_ref[...], preferred_element_type=F32) * (1.0 / HEAD_DIM)
    hn = h * lax.rsqrt(ms + EPS) * nrm_ref[...]
    y_ref[...] = (jax.nn.sigmoid(o_ref[...].astype(F32)) * hn).astype(y_ref.dtype)


def _head_ones(width):
    idx = np.arange(width) // HEAD_DIM
    return jnp.asarray(idx[:, None] == idx[None, :], BF16)


def _mlstm_out(hf, hb, o, norm, tm):
    r, w = hf.shape
    blk = pl.BlockSpec((tm, w), lambda i: (i, 0))
    return pl.pallas_call(
        _mlstm_out_kernel,
        out_shape=jax.ShapeDtypeStruct((r, w), BF16),
        grid=(r // tm,),
        in_specs=[blk, blk, blk,
                  pl.BlockSpec((1, w), lambda i: (0, 0)),
                  pl.BlockSpec((w, w), lambda i: (0, 0))],
        out_specs=blk,
        compiler_params=_params("parallel"),
        name="mlstm_out",
    )(hf, hb, o, norm.reshape(1, w), _head_ones(w))


def _swa_kernel(q_ref, k0, k1, k2, k3, v0, v1, v2, v3, kc_ref, vc_ref, sink_ref, o_ref, *, seq):
    w, dh = SWA_BLOCK, HEAD_DIM
    tq = q_ref.shape[1]
    g = q_ref.shape[0] // dh
    i = pl.program_id(2)
    k_loc = jnp.concatenate([k0[...], k1[...], k2[...], k3[...]], axis=1)
    v_loc = jnp.concatenate([v0[...], v1[...], v2[...], v3[...]], axis=1)
    kpos = (2 * i - 1) * w + lax.broadcasted_iota(I32, (4 * w, tq), 0)
    qpos = i * tq + lax.broadcasted_iota(I32, (4 * w, tq), 1)
    valid = (jnp.abs(qpos - kpos) <= SWA_WINDOW) & (kpos >= 0) & (kpos < seq)
    t_dot = lambda kt, q: lax.dot_general(kt, q, (((0,), (0,)), ((), ())), preferred_element_type=F32)
    qs = [q_ref[h * dh:(h + 1) * dh, :] for h in range(g)]
    s_loc = [jnp.where(valid, t_dot(k_loc, qs[h]), NEG_INF) for h in range(g)]
    s_ctx = [t_dot(kc_ref[...], qs[h]) for h in range(g)]
    outs = []
    for h in range(g):
        sink = sink_ref[0, h:h + 1, :]
        m = jnp.maximum(jnp.maximum(jnp.max(s_loc[h], axis=0, keepdims=True),
                                    jnp.max(s_ctx[h], axis=0, keepdims=True)), sink)
        p_loc = jnp.exp2(s_loc[h] - m)
        p_ctx = jnp.exp2(s_ctx[h] - m)
        den = jnp.sum(p_loc, axis=0, keepdims=True) + jnp.sum(p_ctx, axis=0, keepdims=True) + jnp.exp2(sink - m)
        o = (jnp.dot(v_loc, p_loc.astype(BF16), preferred_element_type=F32)
             + jnp.dot(vc_ref[...], p_ctx.astype(BF16), preferred_element_type=F32))
        outs.append(o / den)
    o_ref[...] = jnp.concatenate(outs, axis=0).T.astype(o_ref.dtype)


def _swa(qt, kt, kvt, sink2, b, s, lc):
    dh = HEAD_DIM
    nh, hkv = qt.shape[0] // dh, kt.shape[0] // dh
    g = nh // hkv
    w = SWA_BLOCK
    tq = 2 * w
    nb, nq = s // w, s // tq
    c0 = b * lc // w
    clampi = lambda j: jnp.clip(j, 0, nb - 1)
    kspec = lambda o: pl.BlockSpec((dh, w), lambda bi, hi, i: (hi, bi * nb + clampi(2 * i + o)))
    vspec = lambda o: pl.BlockSpec((dh, w), lambda bi, hi, i: (hi, c0 + bi * nb + clampi(2 * i + o)))
    cspec = pl.BlockSpec((dh, lc), lambda bi, hi, i: (hi, bi))
    k_all, v_all = kvt
    return pl.pallas_call(
        functools.partial(_swa_kernel, seq=s),
        out_shape=jax.ShapeDtypeStruct((b * s, nh * dh), BF16),
        grid=(b, hkv, nq),
        in_specs=[pl.BlockSpec((g * dh, tq), lambda bi, hi, i: (hi, bi * nq + i)),
                  kspec(-1), kspec(0), kspec(1), kspec(2), vspec(-1), vspec(0), vspec(1), vspec(2),
                  cspec, cspec, pl.BlockSpec((1, g, tq), lambda bi, hi, i: (hi, 0, 0))],
        out_specs=pl.BlockSpec((tq, g * dh), lambda bi, hi, i: (bi * nq + i, hi)),
        compiler_params=_params("parallel", "parallel", "arbitrary"),
        name="window_attention",
    )(qt, kt, kt, kt, kt, v_all, v_all, v_all, v_all, k_all, v_all, sink2)


def _ctx_attn_kernel(q_ref, k_ref, v_ref, sink_ref, o_ref, *, use_sink):
    g, lq, dh = q_ref.shape[1:]
    q = q_ref[0].reshape(g * lq, dh)
    s = lax.dot_general(q, k_ref[0, 0], (((1,), (1,)), ((), ())), preferred_element_type=F32) * ATTN_SCALE
    m = jnp.max(s, axis=1, keepdims=True)
    if use_sink:
        m = jnp.maximum(m, sink_ref[0])
    p = jnp.exp(s - m)
    den = jnp.sum(p, axis=1, keepdims=True)
    if use_sink:
        den = den + jnp.exp(sink_ref[0] - m)
    o = jnp.dot(p.astype(BF16), v_ref[0, 0], preferred_element_type=F32) / den
    o_ref[0] = o.reshape(g, lq, dh).astype(o_ref.dtype)


def _ctx_attn(q, k, v, sink_col, use_sink):
    b, nh, lq, dh = q.shape
    hkv = k.shape[1]
    g = nh // hkv
    qblk = pl.BlockSpec((1, g, lq, dh), lambda bi, hi: (bi, hi, 0, 0))
    kblk = pl.BlockSpec((1, 1, k.shape[2], dh), lambda bi, hi: (bi, hi, 0, 0))
    return pl.pallas_call(
        functools.partial(_ctx_attn_kernel, use_sink=use_sink),
        out_shape=jax.ShapeDtypeStruct(q.shape, BF16),
        grid=(b, hkv),
        in_specs=[qblk, kblk, kblk, pl.BlockSpec((1, g * lq, 1), lambda bi, hi: (hi, 0, 0))],
        out_specs=qblk,
        compiler_params=_params("parallel", "parallel"),
        name="context_attention",
    )(q, k, v, sink_col)


def _flash_kernel(q_ref, k_ref, v_ref, o_ref, m_scr, acc_scr, *bufs, tk):
    dh = HEAD_DIM
    g = q_ref.shape[0] // dh
    n = k_ref.shape[2] // tk
    nb = len(bufs) // 2
    depth = nb - 1
    s_bufs, c_bufs = bufs[:nb], bufs[nb:]
    m_scr[...] = jnp.full_like(m_scr, NEG_INF)
    acc_scr[...] = jnp.zeros_like(acc_scr)

    def scores(j, slot):
        kk = k_ref[0, 0, pl.ds(pl.multiple_of(j * tk, tk), tk), :]
        for h in range(g):
            sc = jnp.dot(kk, q_ref[h * dh:(h + 1) * dh, :], preferred_element_type=F32)
            s_bufs[slot][h] = sc
            c_bufs[slot][h] = jnp.max(sc, axis=0, keepdims=True)

    def update(j, slot):
        vv = v_ref[0, 0, :, pl.ds(pl.multiple_of(j * tk, tk), tk)]
        m_olds = [m_scr[h] for h in range(g)]
        m_news = [jnp.maximum(m_olds[h], c_bufs[slot][h]) for h in range(g)]
        ps = [jnp.exp2(s_bufs[slot][h] - m_news[h]).astype(BF16) for h in range(g)]
        pvs = [jnp.dot(vv, ps[h], preferred_element_type=F32) for h in range(g)]
        for h in range(g):
            acc_scr[h] = jnp.exp2(m_olds[h] - m_news[h]) * acc_scr[h] + pvs[h]
            m_scr[h] = m_news[h]

    for j in range(min(depth, n)):
        scores(j, j % nb)
    loops = max(n - depth, 0) // nb

    def body(i, carry):
        for r in range(nb):
            scores(nb * i + r + depth, (r + depth) % nb)
            update(nb * i + r, r)
        return carry

    lax.fori_loop(0, loops, body, 0)
    for j in range(loops * nb, n):
        if j + depth < n:
            scores(j + depth, (j + depth) % nb)
        update(j, j % nb)
    acc = acc_scr[...]
    out = (acc[:, :dh] / acc[:, dh:dh + 1]).reshape(g * dh, -1)
    o_ref[...] = out.T.astype(o_ref.dtype)


def _flash(qt, k, vt, b, s, tq=None, tk=None):
    w = qt.shape[0]
    hkv, t, dh = k.shape[1], k.shape[2], k.shape[3]
    g = w // dh // hkv
    tq = tq or _tile(s, (256, 128))
    tk = tk or _tile(t, (256, 128))
    nq = s // tq
    ones_rows = 8
    vt = jnp.concatenate([vt, jnp.ones((b, hkv, ones_rows, t), vt.dtype)], axis=2)
    nv = dh + ones_rows
    return pl.pallas_call(
        functools.partial(_flash_kernel, tk=tk),
        out_shape=jax.ShapeDtypeStruct((b * s, w), BF16),
        grid=(b, hkv, nq),
        in_specs=[pl.BlockSpec((g * dh, tq), lambda bi, hi, i: (hi, bi * nq + i)),
                  pl.BlockSpec((1, 1, t, dh), lambda bi, hi, i: (bi, hi, 0, 0)),
                  pl.BlockSpec((1, 1, nv, t), lambda bi, hi, i: (bi, hi, 0, 0))],
        out_specs=pl.BlockSpec((tq, g * dh), lambda bi, hi, i: (bi * nq + i, hi)),
        scratch_shapes=([pltpu.VMEM((g, 1, tq), F32), pltpu.VMEM((g, nv, tq), F32)]
                        + [pltpu.VMEM((g, tk, tq), F32)] * (FLASH_DEPTH + 1)
                        + [pltpu.VMEM((g, 1, tq), F32)] * (FLASH_DEPTH + 1)),
        compiler_params=_params("parallel", "parallel", "arbitrary"),
        name="dense_attention",
    )(qt, k, vt)


def _bmm_kernel(a_ref, b_ref, o_ref):
    o_ref[0] = jnp.dot(a_ref[0], b_ref[0], preferred_element_type=F32).astype(o_ref.dtype)


def _group_matmul(a, bmat, name):
    g, m, k = a.shape
    n = bmat.shape[2]
    tm = _tile(m, (1024, 512, 256, 128, 64, 32, 16, 8))
    return pl.pallas_call(
        _bmm_kernel,
        out_shape=jax.ShapeDtypeStruct((g, m, n), F32),
        grid=(g, m // tm),
        in_specs=[pl.BlockSpec((1, tm, k), lambda gi, i: (gi, i, 0)),
                  pl.BlockSpec((1, k, n), lambda gi, i: (gi, 0, 0))],
        out_specs=pl.BlockSpec((1, tm, n), lambda gi, i: (gi, i, 0)),
        compiler_params=_params("parallel", "parallel"),
        name=name,
    )(a, bmat)


def _s5_scan_kernel(sre_ref, sim_ref, are_ref, aim_ref, zre_ref, zim_ref):
    nd, nsteps = sre_ref.shape[0], sre_ref.shape[1]
    for d in range(nd):
        a_re, a_im = are_ref[d], aim_ref[d]

        def body(i, carry, d=d, a_re=a_re, a_im=a_im):
            z_re, z_im = carry
            zre_ref[d, i] = z_re
            zim_ref[d, i] = z_im
            return (a_re * z_re - a_im * z_im + sre_ref[d, i], a_re * z_im + a_im * z_re + sim_ref[d, i])

        zero = jnp.zeros(sre_ref.shape[2:], F32)
        lax.fori_loop(0, nsteps, body, (zero, zero))


def _s5_scan(s_re, s_im, a_re, a_im):
    full = lambda arr: pl.BlockSpec(arr.shape, lambda i: (0,) * arr.ndim)
    return pl.pallas_call(
        _s5_scan_kernel,
        out_shape=[jax.ShapeDtypeStruct(s_re.shape, F32)] * 2,
        grid=(1,),
        in_specs=[full(s_re), full(s_im), full(a_re), full(a_im)],
        out_specs=[full(s_re), full(s_im)],
        compiler_params=_params("arbitrary"),
        name="s5_chunk_scan",
    )(s_re, s_im, a_re, a_im)


def _s5_glu_kernel(y_ref, u_ref, d_ref, w_ref, b_ref, o_ref):
    y = y_ref[...] + d_ref[...] * u_ref[...]
    y = jax.nn.gelu(y)
    gate = jnp.dot(y.astype(BF16), w_ref[...], preferred_element_type=F32) + b_ref[...]
    o_ref[...] = (y * jax.nn.sigmoid(gate)).astype(o_ref.dtype)


def _s5_glu(y, u, d_skip, w_glu, b_glu, tm):
    r, c = y.shape
    blk = pl.BlockSpec((tm, c), lambda i: (i, 0))
    vec = pl.BlockSpec((1, c), lambda i: (0, 0))
    return pl.pallas_call(
        _s5_glu_kernel,
        out_shape=jax.ShapeDtypeStruct((r, c), BF16),
        grid=(r // tm,),
        in_specs=[blk, blk, vec, pl.BlockSpec((c, c), lambda i: (0, 0)), vec],
        out_specs=blk,
        compiler_params=_params("parallel"),
        name="s5_readout_glu",
    )(y, u, d_skip.reshape(1, c), w_glu.astype(BF16), b_glu.reshape(1, c))


def _s5_tables(lam_re, lam_im, log_dt, b_re, b_im, c_re, c_im):
    ln, p, gc = S5_CHUNK, S5_STATE, S5_GROUP
    lam = lax.complex(lam_re.astype(F32), lam_im.astype(F32))
    dt = jnp.exp(log_dt.astype(F32))[..., None]
    a_bar = jnp.exp(lam * dt)
    b_scale = (a_bar - 1.0) / lam
    b_mat = lax.complex(b_re.astype(F32), b_im.astype(F32))
    c_mat = lax.complex(c_re.astype(F32), c_im.astype(F32))
    tau = jnp.arange(ln + 1, dtype=F32)
    apow = jnp.exp((lam * dt)[:, :, None, :] * tau[None, None, :, None])
    drive = b_scale[..., None] * b_mat[None]
    kern = jnp.real(jnp.einsum('gcp,dgtp,dgpe->dgtce', c_mat, apow[:, :, :ln], drive))
    kc = jnp.concatenate([kern[1, :, :0:-1], (kern[0, :, :1] + kern[1, :, :1]), kern[0, :, 1:]], axis=1)
    kc_e = kc.transpose(0, 3, 1, 2).reshape(S5_GROUPS, gc, (2 * ln - 1) * gc)
    toep = jnp.stack([kc_e[:, :, (ln - 1 - s) * gc:(2 * ln - 1 - s) * gc] for s in range(ln)], axis=1)
    toep = toep.reshape(S5_GROUPS, ln * gc, ln * gc)
    w_f = apow[0, :, ln - 1::-1][:, :ln, :, None] * drive[0][:, None]
    w_b = apow[1, :, :ln, :, None] * drive[1][:, None]
    def m_in(wc):
        wt = wc.transpose(0, 1, 3, 2).reshape(S5_GROUPS, ln * gc, p)
        return jnp.concatenate([jnp.real(wt), jnp.imag(wt)], axis=-1)
    min_all = jnp.concatenate([m_in(w_f), m_in(w_b)], axis=-1)
    o_f = c_mat[:, None] * apow[0, :, 1:ln + 1][:, :, None, :]
    o_b = c_mat[:, None] * apow[1, :, ln:0:-1][:, :, None, :]
    def m_out(oc):
        ot = oc.transpose(0, 3, 1, 2).reshape(S5_GROUPS, p, ln * gc)
        return jnp.concatenate([jnp.real(ot), -jnp.imag(ot)], axis=1)
    rhs = jnp.concatenate([toep, m_out(o_f), m_out(o_b)], axis=1)
    a_l = apow[:, :, ln].reshape(2, 1, S5_GROUPS * p)
    return min_all.astype(BF16), rhs.astype(BF16), jnp.real(a_l), jnp.imag(a_l)


def _s5_mixer(u_x, u_c, tables, d_skip, w_glu, b_glu):
    min_all, rhs, a_re, a_im = tables
    b, s, _ = u_x.shape
    ln, p, gc, ng = S5_CHUNK, S5_STATE, S5_GROUP, S5_GROUPS

    def chunks(u):
        nc = u.shape[1] // ln
        return u.reshape(b, nc, ln, ng, gc).transpose(3, 0, 1, 2, 4).reshape(ng, b * nc, ln * gc).astype(BF16), nc

    ux, ncx = chunks(u_x)
    uc, ncc = chunks(u_c)
    sx = _group_matmul(ux, min_all, "s5_local_state").reshape(ng, b, ncx, 4, p)
    sc = _group_matmul(uc, min_all, "s5_local_state_ctx").reshape(ng, b, ncc, 4, p)

    def scan_order(part):
        f = jnp.concatenate([sc[:, :, :, part], sx[:, :, :, part]], axis=2)
        r = jnp.concatenate([sc[:, :, ::-1, part + 2], sx[:, :, ::-1, part + 2]], axis=2)
        return jnp.stack([f, r]).transpose(0, 3, 2, 1, 4).reshape(2, ncc + ncx, b, ng * p)

    z_re, z_im = _s5_scan(scan_order(0), scan_order(1), a_re, a_im)

    def latent(z, d):
        zl = z[d, ncc:]
        if d == 1:
            zl = zl[::-1]
        return zl.reshape(ncx, b, ng, p).transpose(2, 1, 0, 3).reshape(ng, b * ncx, p)

    lhs = jnp.concatenate([ux, latent(z_re, 0).astype(BF16), latent(z_im, 0).astype(BF16),
                           latent(z_re, 1).astype(BF16), latent(z_im, 1).astype(BF16)], axis=-1)
    y = _group_matmul(lhs, rhs, "s5_outputs")
    y = y.reshape(ng, b, ncx, ln, gc).transpose(1, 2, 3, 0, 4).reshape(b * s, S5_CHANNELS)
    return _s5_glu(y, u_x.reshape(b * s, S5_CHANNELS), d_skip, w_glu, b_glu, _tile(b * s, (1024, 512, 256)))


def _out_proj_kernel(x_ref, a_ref, b_ref, wa_ref, wb_ref, gt_ref, g_ref, sc_ref, sh_ref, wr_ref, br_ref,
                     xo_ref, xn_ref, lg_ref):
    y = (jnp.dot(a_ref[...], wa_ref[...], preferred_element_type=F32)
         + jnp.dot(b_ref[...], wb_ref[...], preferred_element_type=F32))
    x = x_ref[...] + gt_ref[0] * y
    xo_ref[...] = x
    xn = x * lax.rsqrt(jnp.mean(x * x, axis=-1, keepdims=True) + EPS) * g_ref[...]
    xn = xn * (1.0 + sc_ref[0]) + sh_ref[0]
    hi = xn.astype(BF16)
    xn_ref[...] = hi
    lo = (xn - hi.astype(F32)).astype(BF16)
    part = jnp.dot(hi, wr_ref[...], preferred_element_type=F32)
    part = part[:, :LANES] + part[:, LANES:] + jnp.dot(lo, wr_ref[:, :LANES], preferred_element_type=F32)
    lg_ref[...] = part[:, :lg_ref.shape[1]] + br_ref[...]


def _out_proj(x, row0, a, bmix, wa, wb, gate, g_ffn, scale, shift, mod_map, w_router, b_router, tm):
    r = a.shape[0]
    d = x.shape[1]
    t0 = row0 // tm
    ne = w_router.shape[1]
    w_hi = w_router.astype(BF16)
    w_lo = (w_router - w_hi.astype(F32)).astype(BF16)
    w_router = jnp.concatenate([_pad_cols(w_hi), _pad_cols(w_lo)], axis=1)
    row = lambda w: pl.BlockSpec((tm, w), lambda i: (i, 0))
    mod = pl.BlockSpec((1, 1, d), lambda i: (mod_map(i), 0, 0))
    full = lambda arr: pl.BlockSpec(arr.shape, lambda i: (0, 0))
    return pl.pallas_call(
        _out_proj_kernel,
        out_shape=[jax.ShapeDtypeStruct((r, d), F32), jax.ShapeDtypeStruct((r, d), BF16),
                   jax.ShapeDtypeStruct((r, ne), F32)],
        grid=(r // tm,),
        in_specs=[pl.BlockSpec((tm, d), lambda i: (i + t0, 0)), row(a.shape[1]), row(bmix.shape[1]),
                  full(wa), full(wb), mod, pl.BlockSpec((1, d), lambda i: (0, 0)), mod, mod,
                  full(w_router), pl.BlockSpec((1, ne), lambda i: (0, 0))],
        out_specs=[row(d), row(d), row(ne)],
        compiler_params=_params("parallel"),
        name="out_proj",
    )(x, a, bmix, wa, wb, gate, g_ffn.reshape(1, d), scale, shift, w_router, b_router.reshape(1, ne))


def _route_kernel(lg_ref, idx_ref, gate_ref, pos_ref, cnt_ref, carry):
    tm, ne = lg_ref.shape

    @pl.when(pl.program_id(0) == 0)
    def _():
        carry[...] = jnp.zeros_like(carry)

    work = lg_ref[...]
    lane = lax.broadcasted_iota(I32, (tm, ne), 1).astype(F32)
    out_lane = lax.broadcasted_iota(I32, (tm, LANES), 1)
    vals, hots, idx_out = [], [], jnp.zeros((tm, LANES), I32)
    for kk in range(TOP_K):
        mx = jnp.max(work, axis=1, keepdims=True)
        idx = jnp.min(jnp.where(work == mx, lane, float(ne)), axis=1, keepdims=True)
        hot = lane == idx
        work = jnp.where(hot, -jnp.inf, work)
        vals.append(mx)
        hots.append(hot)
        idx_out = jnp.where(out_lane == kk, idx.astype(I32), idx_out)
    exps = [jnp.exp(vv - vals[0]) for vv in vals]
    tot = exps[0] + exps[1] + exps[2] + exps[3]
    multi = (hots[0] | hots[1] | hots[2] | hots[3]).astype(BF16)
    r_i = lax.broadcasted_iota(I32, (tm, tm), 0)
    c_i = lax.broadcasted_iota(I32, (tm, tm), 1)
    before = (r_i > c_i).astype(BF16)
    prefix = jnp.dot(before, multi, preferred_element_type=F32) + carry[...]
    gate_out = jnp.zeros((tm, LANES), F32)
    pos_out = jnp.zeros((tm, LANES), I32)
    for kk in range(TOP_K):
        gate_out = jnp.where(out_lane == kk, exps[kk] / tot, gate_out)
        pos = jnp.sum(jnp.where(hots[kk], prefix, 0.0), axis=1, keepdims=True).astype(I32)
        pos_out = jnp.where(out_lane == kk, pos, pos_out)
    idx_ref[...] = idx_out
    gate_ref[...] = gate_out
    pos_ref[...] = pos_out
    carry[...] = carry[...] + jnp.sum(multi.astype(F32), axis=0, keepdims=True)
    cnt_ref[...] = carry[...]


def _route(logits, tm, row0, n):
    ne = logits.shape[1]
    t0 = row0 // tm
    wide = pl.BlockSpec((tm, LANES), lambda i: (i, 0))
    return pl.pallas_call(
        _route_kernel,
        out_shape=[jax.ShapeDtypeStruct((n, LANES), I32), jax.ShapeDtypeStruct((n, LANES), F32),
                   jax.ShapeDtypeStruct((n, LANES), I32), jax.ShapeDtypeStruct((1, ne), F32)],
        grid=(n // tm,),
        in_specs=[pl.BlockSpec((tm, ne), lambda i: (i + t0, 0))],
        out_specs=[wide, wide, wide, pl.BlockSpec((1, ne), lambda i: (0, 0))],
        scratch_shapes=[pltpu.VMEM((1, ne), F32)],
        compiler_params=_params("arbitrary"),
        name="moe_route",
    )(logits)


def _expert_kernel(be_ref, nb_ref, x_ref, wgu_ref, bgu_ref, wd_ref, bd_ref, perm_ref, o_ref, wgu_bf, wd_bf):
    i = pl.program_id(0)
    fresh = jnp.logical_or(i == 0, be_ref[i] != be_ref[jnp.maximum(i - 1, 0)])

    @pl.when(jnp.logical_and(i < nb_ref[0], fresh))
    def _():
        tile = perm_ref.shape[0]
        for t in range(wgu_ref.shape[2] // tile):
            sl = slice(t * tile, (t + 1) * tile)
            wgu_bf[:, sl] = jnp.dot(wgu_ref[0, :, sl].astype(BF16), perm_ref[...],
                                    preferred_element_type=F32).astype(BF16)
        wd_bf[...] = wd_ref[0].astype(BF16)

    @pl.when(i < nb_ref[0])
    def _():
        h = jnp.dot(x_ref[...], wgu_bf[...], preferred_element_type=F32) + bgu_ref[0]
        acts = []
        for t in range(h.shape[1] // (2 * LANES)):
            glu = jnp.minimum(h[:, 2 * LANES * t:2 * LANES * t + LANES], SWIGLU_LIMIT)
            lin = jnp.clip(h[:, 2 * LANES * t + LANES:2 * LANES * (t + 1)], -SWIGLU_LIMIT, SWIGLU_LIMIT)
            acts.append((glu * jax.nn.sigmoid(SWIGLU_ALPHA * glu) * (lin + 1.0)).astype(BF16))
        act = jnp.concatenate(acts, axis=1)
        o_ref[...] = (jnp.dot(act, wd_bf[...], preferred_element_type=F32) + bd_ref[0]).astype(o_ref.dtype)

    @pl.when(i >= nb_ref[0])
    def _():
        o_ref[...] = jnp.zeros_like(o_ref)


def _experts(xs, blk_e, n_used, wgu, bgu, wd, bd):
    n, d = xs.shape
    ff2 = wgu.shape[2]
    nblk = n // MOE_ROWS
    tile = 2 * LANES
    j = np.arange(tile)
    col = np.where(j < LANES, 2 * j, 2 * (j - LANES) + 1)
    perm = jnp.asarray(np.arange(tile)[:, None] == col[None, :], BF16)
    wspec = lambda shape: pl.BlockSpec((1,) + shape, lambda i, be, nb: (be[i], 0, 0))
    return pl.pallas_call(
        _expert_kernel,
        out_shape=jax.ShapeDtypeStruct((n, d), BF16),
        grid_spec=pltpu.PrefetchScalarGridSpec(
            num_scalar_prefetch=2,
            grid=(nblk,),
            in_specs=[pl.BlockSpec((MOE_ROWS, d), lambda i, be, nb: (i, 0)),
                      wspec((d, ff2)), wspec((1, ff2)), wspec((ff2 // 2, d)), wspec((1, d)),
                      pl.BlockSpec((tile, tile), lambda i, be, nb: (0, 0))],
            out_specs=pl.BlockSpec((MOE_ROWS, d), lambda i, be, nb: (i, 0)),
            scratch_shapes=[pltpu.VMEM((d, ff2), BF16), pltpu.VMEM((ff2 // 2, d), BF16)]),
        compiler_params=_params("arbitrary"),
        name="moe_experts",
    )(blk_e, n_used, xs, wgu, bgu, wd, bd, perm)


def _combine_kernel(x_ref, y_ref, gate_ref, gt_ref, g_ref, o_ref, *, final_norm):
    gates = gate_ref[...]
    y = gates[:, 0:1] * y_ref[0].astype(F32)
    for kk in range(1, TOP_K):
        y = y + gates[:, kk:kk + 1] * y_ref[kk].astype(F32)
    x = x_ref[...] + gt_ref[0] * y
    if final_norm:
        x = x * lax.rsqrt(jnp.mean(x * x, axis=-1, keepdims=True) + EPS) * g_ref[...]
    o_ref[...] = x


def _combine_alias_kernel(x_ref, y_ref, gate_ref, gt_ref, g_ref, prev_ref, o_ref, *, final_norm):
    del prev_ref
    _combine_kernel(x_ref, y_ref, gate_ref, gt_ref, g_ref, o_ref, final_norm=final_norm)


def _combine(x, y4, gates, gate_mod, mod_map, g_final, final_norm, tm, row0, prev):
    r, d = x.shape
    n = y4.shape[1]
    t0 = row0 // tm
    in_specs = [pl.BlockSpec((tm, d), lambda i: (i + t0, 0)),
                pl.BlockSpec((TOP_K, tm, d), lambda i: (0, i, 0)),
                pl.BlockSpec((tm, LANES), lambda i: (i, 0)),
                pl.BlockSpec((1, 1, d), lambda i: (mod_map(i + t0), 0, 0)),
                pl.BlockSpec((1, d), lambda i: (0, 0))]
    args = [x, y4, gates, gate_mod, g_final.reshape(1, d)]
    body, aliases = _combine_kernel, {}
    if prev is not None:
        in_specs.append(pl.BlockSpec(memory_space=pl.ANY))
        args.append(prev)
        body, aliases = _combine_alias_kernel, {len(args) - 1: 0}
    return pl.pallas_call(
        functools.partial(body, final_norm=final_norm),
        out_shape=jax.ShapeDtypeStruct((r, d), F32),
        grid=(n // tm,),
        in_specs=in_specs,
        out_specs=pl.BlockSpec((tm, d), lambda i: (i + t0, 0)),
        input_output_aliases=aliases,
        compiler_params=_params("parallel"),
        name="moe_combine",
    )(*args)


def _moe_parts(tiles):
    total = sum(MOE_SPLIT)
    sizes = [tiles * w // total for w in MOE_SPLIT[:-1]]
    sizes.append(tiles - sum(sizes))
    return [sz for sz in sizes if sz > 0]


def _moe(x, xn, logits, gate_mod, mod_map, weights, g_final, final_norm, tm):
    wgu, bgu, wd, bd = weights
    r, d = xn.shape
    routed, row0 = [], 0
    for tiles in _moe_parts(r // tm):
        n = tiles * tm
        nblk = n * TOP_K // MOE_ROWS + N_EXPERTS
        blk_row0 = jnp.arange(nblk, dtype=I32) * MOE_ROWS
        idx_w, gates_w, pos_w, counts = _route(logits, tm, row0, n)
        idx, pos = idx_w[:, :TOP_K], pos_w[:, :TOP_K]
        counts = counts[0].astype(I32)
        padded = (counts + MOE_ROWS - 1) // MOE_ROWS * MOE_ROWS
        pad_end = jnp.cumsum(padded)
        pad_start = pad_end - padded
        dest = pad_start[idx] + pos
        blk_e = jnp.minimum(jnp.sum((pad_end[None, :] <= blk_row0[:, None]).astype(I32), axis=1), N_EXPERTS - 1)
        n_used = (pad_end[-1:] // MOE_ROWS).astype(I32)
        key = idx * n + jnp.arange(n, dtype=I32)[:, None]
        tok_sorted = jnp.sort(key.reshape(-1)) % n
        grp_start = jnp.cumsum(counts) - counts
        slot = blk_row0[:, None] + jnp.arange(MOE_ROWS, dtype=I32)[None, :]
        compact = jnp.minimum(slot + (grp_start - pad_start)[blk_e][:, None],
                              (grp_start + counts - 1)[blk_e][:, None])
        src = tok_sorted.at[jnp.clip(compact, 0, n * TOP_K - 1).reshape(-1)].get(mode="promise_in_bounds")
        xs = xn.at[src + row0].get(mode="promise_in_bounds")
        routed.append((xs, blk_e, n_used, dest, gates_w, row0, n))
        row0 += n
    ys = [_experts(xs, blk_e, n_used, wgu, bgu, wd, bd) for xs, blk_e, n_used, _, _, _, _ in routed]
    out = None
    for p, (_, _, _, dest, gates_w, row0, n) in enumerate(routed):
        y4 = ys[p].at[dest.T.reshape(-1)].get(mode="promise_in_bounds").reshape(TOP_K, n, d)
        out = _combine(x, y4, gates_w, gate_mod, mod_map, g_final, final_norm, tm, row0, out)
    return out


def _moe_weights(w_gate_up, b_gate_up, w_down, b_down):
    ne, d, ff2 = w_gate_up.shape
    b_tiled = b_gate_up.reshape(ne, ff2 // (2 * LANES), LANES, 2).swapaxes(-1, -2).reshape(ne, 1, ff2)
    return w_gate_up, b_tiled, w_down, b_down[:, None, :]


def _rope_tables(n_tokens):
    rows = n_tokens // GRID_W
    row = jnp.repeat(jnp.arange(rows, dtype=I32), GRID_W).astype(F32)
    col = jnp.tile(jnp.arange(GRID_W, dtype=I32), rows).astype(F32)
    inv = ROPE_BASE ** (-jnp.arange(0, ROPE_AXIS_DIM, 2, dtype=F32) / ROPE_AXIS_DIM)
    ang_r, ang_c = row[:, None] * inv, col[:, None] * inv
    cos = jnp.concatenate([jnp.cos(ang_r)] * 2 + [jnp.cos(ang_c)] * 2, axis=-1)
    sin = jnp.concatenate([jnp.sin(ang_r)] * 2 + [jnp.sin(ang_c)] * 2, axis=-1)
    return cos, sin


def _rot_perm():
    q = ROPE_AXIS_DIM // 2
    d = np.arange(HEAD_DIM)
    first = (d % ROPE_AXIS_DIM) < q
    perm = np.where(first, d + q, d - q)
    sign = np.where(first, -1.0, 1.0).astype(np.float32)
    return perm, sign


def _rot_cols(w, n_heads):
    perm, sign = _rot_perm()
    k = w.shape[0]
    wh = w.reshape(k, n_heads, HEAD_DIM)
    return (wh[:, :, perm] * sign).reshape(k, n_heads * HEAD_DIM)


def _to_heads(t, b, n_heads):
    return t.reshape(b, -1, n_heads, HEAD_DIM).transpose(0, 2, 1, 3)


def _from_heads(t):
    b, h, tt, dh = t.shape
    return t.transpose(0, 2, 1, 3).reshape(b * tt, h * dh)


def _layer_even(xa, mod, b, s, lc, g_mix, g_ffn, w_in, w_out, gate_bias, mlstm_norm, sink, router, moe_w):
    d = xa.shape[1]
    nctx = b * lc
    tm = _tile(math.gcd(nctx, s), (512, 256, 128))
    mod_map = lambda i: jnp.where(i * tm < nctx, b, (i * tm - nctx) // s)
    sh1, sc1, gt1, sh2, sc2, gt2 = [mod[:, j][:, None, :] for j in range(6)]

    hm, hs, hk = MLSTM_HEADS * HEAD_DIM, SWA_HEADS * HEAD_DIM, SWA_KV_HEADS * HEAD_DIM
    o0 = np.cumsum([0, hm, hm, hm, hm, 4 * MLSTM_HEADS, hs, hk, hk])
    seg = lambda j: w_in[:, o0[j]:o0[j + 1]]
    w_cat = jnp.concatenate([seg(0), seg(1) * ATTN_SCALE, seg(2), seg(3), _pad_cols(seg(4))], axis=1).astype(BF16)
    w_t = jnp.concatenate([seg(5), _rot_cols(seg(5), SWA_HEADS), seg(6), _rot_cols(seg(6), SWA_KV_HEADS), seg(7)],
                          axis=1).T.astype(BF16)
    widths = [hm, hm, hm, hm, LANES]
    dts = [BF16] * 4 + [F32]
    qa, ka, va, oa, gts, qt, qt_r, kt, kt_r, vt = _in_proj(
        xa, g_mix, sc1, sh1, mod_map, w_cat, widths, dts, tm,
        wt=w_t, t_widths=[hs, hs, hk, hk, hk], t_dtypes=[BF16] * 5)

    hf, hb = _mlstm(qa, ka, va, gts, _pad_cols(gate_bias.astype(F32).reshape(1, -1)), b, lc, s)
    mix_a = _mlstm_out(hf, hb, oa, mlstm_norm.reshape(-1), tm)

    cos, sin = _rope_tables(s)
    ones = jnp.ones((HEAD_DIM,), F32)
    g = SWA_HEADS // SWA_KV_HEADS
    heads = lambda t, n: _to_heads(t, b, n)
    log2e = math.log2(math.e)
    q_x = _q_prep_t(qt, qt_r, nctx, b * s, cos.T, sin.T, ones, ones, ATTN_SCALE * log2e, norm=False)
    k_x = _q_prep_t(kt, kt_r, nctx, b * s, cos.T, sin.T, ones, ones, 1.0, norm=False)
    k_c, v_c = heads(kt[:, :nctx].T, SWA_KV_HEADS), heads(vt[:, :nctx].T, SWA_KV_HEADS)
    q_c = heads(qt[:, :nctx].T, SWA_HEADS)
    sink_h = sink.astype(F32).reshape(SWA_KV_HEADS, g, 1)
    att_x = _swa(q_x, k_x, (kt, vt), jnp.broadcast_to(sink_h * log2e, (SWA_KV_HEADS, g, 2 * SWA_BLOCK)), b, s, lc)
    att_c = _ctx_attn(q_c, k_c, v_c, jnp.broadcast_to(sink_h[..., None], (SWA_KV_HEADS, g, lc, 1))
                      .reshape(SWA_KV_HEADS, g * lc, 1), True)
    mix_b = jnp.concatenate([_from_heads(att_c), att_x], axis=0)

    w_router, b_router = router
    xa, xn2, logits = _out_proj(xa, 0, mix_a, mix_b, w_out[:hm].astype(BF16), w_out[hm:].astype(BF16), gt1, g_ffn,
                                sc2, sh2, mod_map, w_router.astype(F32), b_router.astype(F32), tm)
    return _moe(xa, xn2, logits, gt2, mod_map, moe_w, g_ffn, False, tm)


def _layer_odd_last(xa, mod, b, s, lc, g_mix, g_ffn, w_in, w_out, s5_params, d_skip, w_glu, b_glu,
                    q_norm, k_norm, router, moe_w, g_final):
    nctx = b * lc
    tm = _tile(math.gcd(nctx, s), (512, 256, 128))
    mod_map = lambda i: jnp.where(i * tm < nctx, b, (i * tm - nctx) // s)
    lat_map = lambda i: i * tm // s
    sh1, sc1, gt1, sh2, sc2, gt2 = [mod[:, j][:, None, :] for j in range(6)]

    hq, hk = ATT_HEADS * HEAD_DIM, ATT_KV_HEADS * HEAD_DIM
    o1 = np.cumsum([0, S5_CHANNELS, hq, hk, hk])
    seg = lambda j: w_in[:, o1[j]:o1[j + 1]]
    kpad = lambda w: _pad_cols(w, 2 * LANES)
    w_cat = jnp.concatenate([seg(0), kpad(seg(2)), kpad(_rot_cols(seg(2), ATT_KV_HEADS)), kpad(seg(3))],
                            axis=1).astype(BF16)
    w_q = jnp.concatenate([seg(1), _rot_cols(seg(1), ATT_HEADS)], axis=1).T.astype(BF16)
    widths = [S5_CHANNELS, 2 * LANES, 2 * LANES, 2 * LANES]
    dts = [F32, BF16, BF16, BF16]
    u, k, k_r, v, qt, qt_r = _in_proj(xa, g_mix, sc1, sh1, mod_map, w_cat, widths, dts, tm,
                                      wt=w_q, t_widths=[hq, hq], t_dtypes=[BF16, BF16])

    mix_a = _s5_mixer(u[nctx:].reshape(b, s, S5_CHANNELS), u[:nctx].reshape(b, lc, S5_CHANNELS),
                      _s5_tables(*s5_params), d_skip, w_glu, b_glu)

    cos, sin = _rope_tables(s)
    perm, _ = _rot_perm()
    qn, kn = q_norm.astype(F32), k_norm.astype(F32)
    heads = lambda t, n: _to_heads(t[:, :n * HEAD_DIM], b, n)
    q_x = _q_prep_t(qt, qt_r, nctx, b * s, cos.T, sin.T, qn, qn[perm], ATTN_SCALE * math.log2(math.e))
    k_x = _qk_prep(heads(k[nctx:], ATT_KV_HEADS), heads(k_r[nctx:], ATT_KV_HEADS), cos, sin, kn, kn[perm],
                   norm=True, rope=True, scale=1.0)
    kc_raw = heads(k[:nctx], ATT_KV_HEADS)
    k_c = _qk_prep(kc_raw, kc_raw, cos[:lc], sin[:lc], kn, kn, norm=True, rope=False, scale=1.0)
    k_all = jnp.concatenate([k_c, k_x], axis=2)
    v_all = jnp.concatenate([heads(v[:nctx], ATT_KV_HEADS), heads(v[nctx:], ATT_KV_HEADS)], axis=2)
    mix_b = _flash(q_x, k_all, v_all.swapaxes(-1, -2), b, s)

    w_router, b_router = router
    hs = S5_CHANNELS
    x, xn2, logits = _out_proj(xa, nctx, mix_a, mix_b, w_out[:hs].astype(BF16), w_out[hs:].astype(BF16), gt1, g_ffn,
                               sc2, sh2, lat_map, w_router.astype(F32), b_router.astype(F32), tm)
    return _moe(x, xn2, logits, gt2, lat_map, moe_w, g_final, True, tm)


def kernel(x, c, ctx, c_ctx, l0_w_mod, l0_b_mod, l0_g_mix, l0_g_ffn, l0_w_in, l0_w_out, l0_gate_bias, l0_mlstm_norm, l0_sink, l0_w_router, l0_b_router, l0_w_gate_up, l0_b_gate_up, l0_w_down, l0_b_down, l1_w_mod, l1_b_mod, l1_g_mix, l1_g_ffn, l1_w_in, l1_w_out, l1_lam_re, l1_lam_im, l1_log_dt, l1_b_re, l1_b_im, l1_c_re, l1_c_im, l1_d_skip, l1_w_glu, l1_b_glu, l1_q_norm, l1_k_norm, l1_w_router, l1_b_router, l1_w_gate_up, l1_b_gate_up, l1_w_down, l1_b_down, g_final):
    b, s, d = x.shape
    lc = ctx.shape[1]
    cond = jnp.concatenate([c, c_ctx[None, :]], axis=0)
    cond = jnp.pad(cond, ((0, (-(b + 1)) % 8), (0, 0)))
    mod0 = _silu_linear(cond, l0_w_mod, l0_b_mod)[:b + 1].reshape(b + 1, 6, d)
    mod1 = _silu_linear(cond, l1_w_mod, l1_b_mod)[:b + 1].reshape(b + 1, 6, d)

    moe0 = _moe_weights(l0_w_gate_up, l0_b_gate_up, l0_w_down, l0_b_down)
    moe1 = _moe_weights(l1_w_gate_up, l1_b_gate_up, l1_w_down, l1_b_down)
    xa = jnp.concatenate([ctx.reshape(b * lc, d), x.reshape(b * s, d)], axis=0)
    xa = _layer_even(xa, mod0, b, s, lc, l0_g_mix, l0_g_ffn, l0_w_in, l0_w_out, l0_gate_bias, l0_mlstm_norm,
                     l0_sink, (l0_w_router, l0_b_router), moe0)
    out = _layer_odd_last(xa, mod1, b, s, lc, l1_g_mix, l1_g_ffn, l1_w_in, l1_w_out,
                          (l1_lam_re, l1_lam_im, l1_log_dt, l1_b_re, l1_b_im, l1_c_re, l1_c_im),
                          l1_d_skip, l1_w_glu, l1_b_glu, l1_q_norm, l1_k_norm, (l1_w_router, l1_b_router),
                          moe1, g_final)
    return out.reshape(b, s, d)
```

```python
import functools
import math

import jax
import jax.numpy as jnp
import numpy as np
from jax import lax
from jax.experimental import pallas as pl
from jax.experimental.pallas import tpu as pltpu

F32 = jnp.float32
BF16 = jnp.bfloat16
I32 = jnp.int32

GRID_W = 64
HEAD_DIM = 64
ATTN_SCALE = HEAD_DIM ** -0.5
ROPE_AXIS_DIM = HEAD_DIM // 2
ROPE_BASE = 10000.0
EPS = 1e-6
NEG_INF = -1e30

MLSTM_HEADS = 8
MLSTM_CHUNKS = (256, 128, 64)
SWA_HEADS = 8
SWA_KV_HEADS = 2
SWA_WINDOW = 128
SWA_BLOCK = 128
S5_CHANNELS = 256
S5_GROUP = 16
S5_GROUPS = S5_CHANNELS // S5_GROUP
S5_STATE = 64
S5_CHUNK = 64
ATT_HEADS = 12
ATT_KV_HEADS = 3
N_EXPERTS = 32
TOP_K = 4
SWIGLU_LIMIT = 7.0
SWIGLU_ALPHA = 1.702

LANES = 128
VMEM_LIMIT = 56 * 1024 * 1024
MOE_ROWS = 1024
MOE_SPLIT = (1, 1)
FLASH_DEPTH = 6
HIGHEST = lax.Precision.HIGHEST


def _params(*sem):
    return pltpu.CompilerParams(dimension_semantics=sem, vmem_limit_bytes=VMEM_LIMIT)


def _tile(n, prefs):
    for t in prefs:
        if n % t == 0:
            return t
    return n


def _pad_cols(w, mult=LANES):
    pad = (-w.shape[-1]) % mult
    if pad:
        w = jnp.pad(w, [(0, 0)] * (w.ndim - 1) + [(0, pad)])
    return w


def _linear_kernel(x_ref, w_ref, b_ref, o_ref):
    x = x_ref[...]
    x = x * jax.nn.sigmoid(x)
    o_ref[...] = jnp.dot(x, w_ref[...], precision=HIGHEST, preferred_element_type=F32) + b_ref[...]


def _silu_linear(x, w, b):
    m, k = x.shape
    n = w.shape[1]
    tn = _tile(n, (1024, 512, 256, 128))
    return pl.pallas_call(
        _linear_kernel,
        out_shape=jax.ShapeDtypeStruct((m, n), F32),
        grid=(n // tn,),
        in_specs=[pl.BlockSpec((m, k), lambda j: (0, 0)),
                  pl.BlockSpec((k, tn), lambda j: (0, j)),
                  pl.BlockSpec((1, tn), lambda j: (0, j))],
        out_specs=pl.BlockSpec((m, tn), lambda j: (0, j)),
        compiler_params=_params("arbitrary"),
        name="adaln_linear",
    )(x, w, b.reshape(1, n))


def _in_proj_kernel(x_ref, g_ref, sc_ref, sh_ref, w_ref, wt_ref, *out_refs, widths, t_widths):
    x = x_ref[...]
    xn = x * lax.rsqrt(jnp.mean(x * x, axis=-1, keepdims=True) + EPS) * g_ref[...]
    xb = (xn * (1.0 + sc_ref[0]) + sh_ref[0]).astype(BF16)
    off = 0
    for o_ref, w in zip(out_refs, widths):
        o_ref[...] = jnp.dot(xb, w_ref[:, off:off + w], preferred_element_type=F32).astype(o_ref.dtype)
        off += w
    off = 0
    for o_ref, w in zip(out_refs[len(widths):], t_widths):
        o_ref[...] = lax.dot_general(wt_ref[off:off + w, :], xb, (((1,), (1,)), ((), ())),
                                     preferred_element_type=F32).astype(o_ref.dtype)
        off += w


def _in_proj(x, g, scale, shift, mod_map, w, widths, dtypes, tm, wt=None, t_widths=(), t_dtypes=()):
    r, d = x.shape
    if wt is None:
        wt = jnp.zeros((8, d), BF16)
    return pl.pallas_call(
        functools.partial(_in_proj_kernel, widths=tuple(widths), t_widths=tuple(t_widths)),
        out_shape=([jax.ShapeDtypeStruct((r, wd), dt) for wd, dt in zip(widths, dtypes)]
                   + [jax.ShapeDtypeStruct((wd, r), dt) for wd, dt in zip(t_widths, t_dtypes)]),
        grid=(r // tm,),
        in_specs=[pl.BlockSpec((tm, d), lambda i: (i, 0)),
                  pl.BlockSpec((1, d), lambda i: (0, 0)),
                  pl.BlockSpec((1, 1, d), lambda i: (mod_map(i), 0, 0)),
                  pl.BlockSpec((1, 1, d), lambda i: (mod_map(i), 0, 0)),
                  pl.BlockSpec(w.shape, lambda i: (0, 0)),
                  pl.BlockSpec(wt.shape, lambda i: (0, 0))],
        out_specs=([pl.BlockSpec((tm, wd), lambda i: (i, 0)) for wd in widths]
                   + [pl.BlockSpec((wd, tm), lambda i: (0, i)) for wd in t_widths]),
        compiler_params=_params("parallel"),
        name="in_proj",
    )(x, g.reshape(1, d), scale, shift, w, wt)


def _qk_prep_kernel(x_ref, xr_ref, cos_ref, sin_ref, g_ref, gr_ref, o_ref, *, norm, rope, scale):
    x = x_ref[0].astype(F32)
    if norm:
        s = lax.rsqrt(jnp.mean(x * x, axis=-1, keepdims=True) + EPS)
        x = x * s * g_ref[...]
    if rope:
        xr = xr_ref[0].astype(F32)
        if norm:
            xr = xr * s * gr_ref[...]
        x = x * cos_ref[...] + xr * sin_ref[...]
    o_ref[0] = (x * scale).astype(o_ref.dtype)


def _qk_prep(x, x_rot, cos, sin, gain, gain_rot, *, norm, rope, scale):
    b, h, t, dh = x.shape
    tt = _tile(t, (512, 256, 128))
    blk = pl.BlockSpec((1, h, tt, dh), lambda i, j: (i, 0, j, 0))
    tab = pl.BlockSpec((tt, dh), lambda i, j: (j, 0))
    vec = pl.BlockSpec((1, dh), lambda i, j: (0, 0))
    return pl.pallas_call(
        functools.partial(_qk_prep_kernel, norm=norm, rope=rope, scale=scale),
        out_shape=jax.ShapeDtypeStruct(x.shape, BF16),
        grid=(b, t // tt),
        in_specs=[blk, blk, tab, tab, vec, vec],
        out_specs=blk,
        compiler_params=_params("parallel", "parallel"),
        name="qk_prep",
    )(x, x_rot, cos, sin, gain.reshape(1, dh), gain_rot.reshape(1, dh))


def _q_prep_t_kernel(x_ref, xr_ref, cos_ref, sin_ref, g_ref, gr_ref, o_ref, *, norm, scale):
    w, tt = x_ref.shape
    nh = w // HEAD_DIM
    x = x_ref[...].astype(F32).reshape(nh, HEAD_DIM, tt)
    xr = xr_ref[...].astype(F32).reshape(nh, HEAD_DIM, tt)
    if norm:
        s = lax.rsqrt(jnp.mean(x * x, axis=1, keepdims=True) + EPS)
        x, xr = x * s * g_ref[...], xr * s * gr_ref[...]
    y = x * cos_ref[...] + xr * sin_ref[...]
    o_ref[...] = (y * scale).reshape(w, tt).astype(o_ref.dtype)


def _q_prep_t(xt, xt_rot, col0, n_cols, cos_t, sin_t, gain, gain_rot, scale, norm=True):
    w = xt.shape[0]
    s = cos_t.shape[1]
    tt = _tile(math.gcd(col0, s), (512, 256, 128))
    blk = pl.BlockSpec((w, tt), lambda j: (0, j + col0 // tt))
    tab = pl.BlockSpec((HEAD_DIM, tt), lambda j: (0, j % (s // tt)))
    vec = pl.BlockSpec((HEAD_DIM, 1), lambda j: (0, 0))
    return pl.pallas_call(
        functools.partial(_q_prep_t_kernel, norm=norm, scale=scale),
        out_shape=jax.ShapeDtypeStruct((w, n_cols), BF16),
        grid=(n_cols // tt,),
        in_specs=[blk, blk, tab, tab, vec, vec],
        out_specs=pl.BlockSpec((w, tt), lambda j: (0, j)),
        compiler_params=_params("parallel"),
        name="q_prep_t",
    )(xt, xt_rot, cos_t, sin_t, gain.reshape(HEAD_DIM, 1), gain_rot.reshape(HEAD_DIM, 1))


def _log_sigmoid(x):
    return jnp.minimum(x, 0.0) - jnp.log(1.0 + jnp.exp(-jnp.abs(x)))


def _mlstm_kernel(qf, kf, vf, gf, qb, kb, vb, gb, bias_ref, hf_ref, hb_ref, c_scr, n_scr, m_scr):
    @pl.when(pl.program_id(1) == 0)
    def _():
        c_scr[...] = jnp.zeros_like(c_scr)
        n_scr[...] = jnp.zeros_like(n_scr)
        m_scr[...] = jnp.full_like(m_scr, NEG_INF)

    gates = [_mlstm_gates(d, g_ref, bias_ref) for d, g_ref in enumerate((gf, gb))]
    for d in range(2):
        _mlstm_stages((d,), gates, qf, kf, vf, qb, kb, vb, hf_ref, hb_ref, c_scr, n_scr, m_scr)


def _mlstm_gates(d, g_ref, bias_ref):
    ch, nh = g_ref.shape[0], MLSTM_HEADS
    row = lax.broadcasted_iota(I32, (ch, ch), 0)
    col = lax.broadcasted_iota(I32, (ch, ch), 1)
    seen = (col <= row) if d == 0 else (col >= row)
    tri = seen.astype(F32)
    tri_t = ((row <= col) if d == 0 else (row >= col)).astype(F32)
    g = g_ref[...] + bias_ref[...]
    g_t = g.T
    lo = 2 * nh * d
    log2e = math.log2(math.e)
    li_col, lf_col = g[:, lo:lo + nh] * log2e, _log_sigmoid(g[:, lo + nh:lo + 2 * nh]) * log2e
    li_row, lf_row = g_t[lo:lo + nh, :] * log2e, _log_sigmoid(g_t[lo + nh:lo + 2 * nh, :]) * log2e
    b_col = jnp.dot(tri, lf_col, precision=HIGHEST, preferred_element_type=F32)
    b_row = jnp.dot(lf_row, tri_t, precision=HIGHEST, preferred_element_type=F32)
    return seen, b_col, b_row, li_col, li_row - b_row


def _mlstm_stages(dirs, gates, qf, kf, vf, qb, kb, vb, hf_ref, hb_ref, c_scr, n_scr, m_scr):
    ch, nh, dh = qf.shape[0], MLSTM_HEADS, HEAD_DIM
    lane_lo = lax.broadcasted_iota(I32, (ch, 2 * dh), 1) < dh
    row_lo = lax.broadcasted_iota(I32, (2 * dh, 2 * dh), 0) < dh
    col_lo = lax.broadcasted_iota(I32, (2 * dh, 2 * dh), 1) < dh
    vec_lo = lax.broadcasted_iota(I32, (1, 2 * dh), 1) < dh
    pair = lambda a, b: jnp.where(lane_lo, a, b)
    refs = ((qf, kf, vf, hf_ref), (qb, kb, vb, hb_ref))
    jobs = [(d, p) for d in dirs for p in range(nh // 2)]
    sl = lambda p: slice(2 * dh * p, 2 * dh * (p + 1))

    qk = {}
    for d, p in jobs:
        q2, k2 = refs[d][0][:, sl(p)], refs[d][1][:, sl(p)]
        zero = jnp.zeros_like(q2)
        q_st = jnp.concatenate([jnp.where(lane_lo, q2, zero), jnp.where(lane_lo, zero, q2)], axis=0)
        qk[d, p] = lax.dot_general(q_st, k2, (((1,), (1,)), ((), ())), preferred_element_type=F32)

    st = {}
    for d in dirs:
        seen, b_col, b_row, li_col, src_row = gates[d]
        last = ch - 1 if d == 0 else 0
        for h in range(nh):
            bc, br = b_col[:, h:h + 1], b_row[h:h + 1, :]
            lic = li_col[:, h:h + 1]
            b_last = br[:, last:last + 1]
            m_prev = m_scr[d, h // 2][:, (h % 2) * dh:(h % 2) * dh + 1]
            d_log = jnp.where(seen, bc + src_row[h:h + 1, :], NEG_INF)
            inter_log = bc + m_prev
            m_t = jnp.maximum(inter_log, jnp.max(d_log, axis=1, keepdims=True))
            w_log = b_last - bc + lic
            m_new = jnp.maximum(b_last + m_prev, jnp.max(w_log, axis=0, keepdims=True))
            st[d, h] = dict(dmat=jnp.exp2(d_log - m_t), inter=jnp.exp2(inter_log - m_t), floor=jnp.exp2(-m_t),
                            m_new=m_new, decay=jnp.exp2(b_last + m_prev - m_new), wn=jnp.exp2(w_log - m_new))

    mm = {}
    for d, p in jobs:
        a, b = st[d, 2 * p], st[d, 2 * p + 1]
        q2, k2, v2 = refs[d][0][:, sl(p)], refs[d][1][:, sl(p)], refs[d][2][:, sl(p)]
        s = qk[d, p] * jnp.concatenate([a["dmat"], b["dmat"]], axis=0)
        kw = k2.astype(F32) * pair(a["wn"], b["wn"])
        mm[d, p] = dict(
            s_sum=jnp.sum(s, axis=1, keepdims=True), kw_sum=jnp.sum(kw, axis=0, keepdims=True),
            sv=jnp.dot(s.astype(BF16), v2, preferred_element_type=F32),
            q_c=jnp.dot(q2, c_scr[d, p].astype(BF16), preferred_element_type=F32),
            kv=lax.dot_general(kw.astype(BF16), v2, (((0,), (0,)), ((), ())), preferred_element_type=F32))

    for d, p in jobs:
        a, b, r = st[d, 2 * p], st[d, 2 * p + 1], mm[d, p]
        q2 = refs[d][0][:, sl(p)]
        n_prev = n_scr[d, p]
        num = pair(a["inter"], b["inter"]) * r["q_c"] + jnp.where(lane_lo, r["sv"][:ch], r["sv"][ch:])
        qn = q2.astype(F32) * n_prev
        qn_a = jnp.sum(jnp.where(lane_lo, qn, 0.0), axis=1, keepdims=True)
        qn_b = jnp.sum(jnp.where(lane_lo, 0.0, qn), axis=1, keepdims=True)
        den_a = jnp.maximum(jnp.abs(a["inter"] * qn_a + r["s_sum"][:ch]), a["floor"])
        den_b = jnp.maximum(jnp.abs(b["inter"] * qn_b + r["s_sum"][ch:]), b["floor"])
        refs[d][3][:, sl(p)] = num / pair(den_a, den_b)
        c_scr[d, p] = (jnp.where(row_lo, a["decay"], b["decay"]) * c_scr[d, p]
                       + jnp.where(row_lo == col_lo, r["kv"], 0.0))
        n_scr[d, p] = jnp.where(vec_lo, a["decay"], b["decay"]) * n_prev + r["kw_sum"]
        m_scr[d, p] = jnp.where(vec_lo, a["m_new"], b["m_new"])


def _mlstm(q, k, v, gates, bias, b, lc, s):
    r, w = q.shape
    ch = _tile(math.gcd(lc, s), MLSTM_CHUNKS)
    ncc, ncx = lc // ch, s // ch
    base = b * ncc

    def fwd(i, c):
        return jnp.where(c < ncc, i * ncc + c, base + i * ncx + (c - ncc)), 0

    def bwd(i, c):
        return jnp.where(c < ncc, i * ncc + (ncc - 1 - c), base + i * ncx + (ncx - 1 - (c - ncc))), 0

    spec = lambda width, m: pl.BlockSpec((ch, width), m)
    npair = MLSTM_HEADS // 2
    return pl.pallas_call(
        _mlstm_kernel,
        out_shape=[jax.ShapeDtypeStruct((r, w), F32)] * 2,
        grid=(b, ncc + ncx),
        in_specs=[spec(w, fwd), spec(w, fwd), spec(w, fwd), spec(LANES, fwd),
                  spec(w, bwd), spec(w, bwd), spec(w, bwd), spec(LANES, bwd),
                  pl.BlockSpec((1, LANES), lambda i, c: (0, 0))],
        out_specs=[spec(w, fwd), spec(w, bwd)],
        scratch_shapes=[pltpu.VMEM((2, npair, 2 * HEAD_DIM, 2 * HEAD_DIM), F32),
                        pltpu.VMEM((2, npair, 1, 2 * HEAD_DIM), F32),
                        pltpu.VMEM((2, npair, 1, 2 * HEAD_DIM), F32)],
        compiler_params=_params("parallel", "arbitrary"),
        name="mlstm_scan",
    )(q, k, v, gates, q, k, v, gates, bias)


def _mlstm_out_kernel(hf_ref, hb_ref, o_ref, nrm_ref, ones_ref, y_ref):
    h = hf_ref[...] + hb_ref[...]
    ms = jnp.dot((h * h).astype(BF16), ones_ref[...], preferred_element_type=F32) * (1.0 / HEAD_DIM)
    hn = h * lax.rsqrt(ms + EPS) * nrm_ref[...]
    y_ref[...] = (jax.nn.sigmoid(o_ref[...].astype(F32)) * hn).astype(y_ref.dtype)


def _head_ones(width):
    idx = np.arange(width) // HEAD_DIM
    return jnp.asarray(idx[:, None] == idx[None, :], BF16)


def _mlstm_out(hf, hb, o, norm, tm):
    r, w = hf.shape
    blk = pl.BlockSpec((tm, w), lambda i: (i, 0))
    return pl.pallas_call(
        _mlstm_out_kernel,
        out_shape=jax.ShapeDtypeStruct((r, w), BF16),
        grid=(r // tm,),
        in_specs=[blk, blk, blk,
                  pl.BlockSpec((1, w), lambda i: (0, 0)),
                  pl.BlockSpec((w, w), lambda i: (0, 0))],
        out_specs=blk,
        compiler_params=_params("parallel"),
        name="mlstm_out",
    )(hf, hb, o, norm.reshape(1, w), _head_ones(w))


def _swa_kernel(q_ref, k0, k1, k2, k3, v0, v1, v2, v3, kc_ref, vc_ref, sink_ref, o_ref, *, seq):
    w, dh = SWA_BLOCK, HEAD_DIM
    tq = q_ref.shape[1]
    g = q_ref.shape[0] // dh
    i = pl.program_id(2)
    k_loc = jnp.concatenate([k0[...], k1[...], k2[...], k3[...]], axis=1)
    v_loc = jnp.concatenate([v0[...], v1[...], v2[...], v3[...]], axis=1)
    kpos = (2 * i - 1) * w + lax.broadcasted_iota(I32, (4 * w, tq), 0)
    qpos = i * tq + lax.broadcasted_iota(I32, (4 * w, tq), 1)
    valid = (jnp.abs(qpos - kpos) <= SWA_WINDOW) & (kpos >= 0) & (kpos < seq)
    t_dot = lambda kt, q: lax.dot_general(kt, q, (((0,), (0,)), ((), ())), preferred_element_type=F32)
    qs = [q_ref[h * dh:(h + 1) * dh, :] for h in range(g)]
    s_loc = [jnp.where(valid, t_dot(k_loc, qs[h]), NEG_INF) for h in range(g)]
    s_ctx = [t_dot(kc_ref[...], qs[h]) for h in range(g)]
    outs = []
    for h in range(g):
        sink = sink_ref[0, h:h + 1, :]
        m = jnp.maximum(jnp.maximum(jnp.max(s_loc[h], axis=0, keepdims=True),
                                    jnp.max(s_ctx[h], axis=0, keepdims=True)), sink)
        p_loc = jnp.exp2(s_loc[h] - m)
        p_ctx = jnp.exp2(s_ctx[h] - m)
        den = jnp.sum(p_loc, axis=0, keepdims=True) + jnp.sum(p_ctx, axis=0, keepdims=True) + jnp.exp2(sink - m)
        o = (jnp.dot(v_loc, p_loc.astype(BF16), preferred_element_type=F32)
             + jnp.dot(vc_ref[...], p_ctx.astype(BF16), preferred_element_type=F32))
        outs.append(o / den)
    o_ref[...] = jnp.concatenate(outs, axis=0).T.astype(o_ref.dtype)


def _swa(qt, kt, kvt, sink2, b, s, lc):
    dh = HEAD_DIM
    nh, hkv = qt.shape[0] // dh, kt.shape[0] // dh
    g = nh // hkv
    w = SWA_BLOCK
    tq = 2 * w
    nb, nq = s // w, s // tq
    c0 = b * lc // w
    clampi = lambda j: jnp.clip(j, 0, nb - 1)
    kspec = lambda o: pl.BlockSpec((dh, w), lambda bi, hi, i: (hi, bi * nb + clampi(2 * i + o)))
    vspec = lambda o: pl.BlockSpec((dh, w), lambda bi, hi, i: (hi, c0 + bi * nb + clampi(2 * i + o)))
    cspec = pl.BlockSpec((dh, lc), lambda bi, hi, i: (hi, bi))
    k_all, v_all = kvt
    return pl.pallas_call(
        functools.partial(_swa_kernel, seq=s),
        out_shape=jax.ShapeDtypeStruct((b * s, nh * dh), BF16),
        grid=(b, hkv, nq),
        in_specs=[pl.BlockSpec((g * dh, tq), lambda bi, hi, i: (hi, bi * nq + i)),
                  kspec(-1), kspec(0), kspec(1), kspec(2), vspec(-1), vspec(0), vspec(1), vspec(2),
                  cspec, cspec, pl.BlockSpec((1, g, tq), lambda bi, hi, i: (hi, 0, 0))],
        out_specs=pl.BlockSpec((tq, g * dh), lambda bi, hi, i: (bi * nq + i, hi)),
        compiler_params=_params("parallel", "parallel", "arbitrary"),
        name="window_attention",
    )(qt, kt, kt, kt, kt, v_all, v_all, v_all, v_all, k_all, v_all, sink2)


def _ctx_attn_kernel(q_ref, k_ref, v_ref, sink_ref, o_ref, *, use_sink):
    g, lq, dh = q_ref.shape[1:]
    q = q_ref[0].reshape(g * lq, dh)
    s = lax.dot_general(q, k_ref[0, 0], (((1,), (1,)), ((), ())), preferred_element_type=F32) * ATTN_SCALE
    m = jnp.max(s, axis=1, keepdims=True)
    if use_sink:
        m = jnp.maximum(m, sink_ref[0])
    p = jnp.exp(s - m)
    den = jnp.sum(p, axis=1, keepdims=True)
    if use_sink:
        den = den + jnp.exp(sink_ref[0] - m)
    o = jnp.dot(p.astype(BF16), v_ref[0, 0], preferred_element_type=F32) / den
    o_ref[0] = o.reshape(g, lq, dh).astype(o_ref.dtype)


def _ctx_attn(q, k, v, sink_col, use_sink):
    b, nh, lq, dh = q.shape
    hkv = k.shape[1]
    g = nh // hkv
    qblk = pl.BlockSpec((1, g, lq, dh), lambda bi, hi: (bi, hi, 0, 0))
    kblk = pl.BlockSpec((1, 1, k.shape[2], dh), lambda bi, hi: (bi, hi, 0, 0))
    return pl.pallas_call(
        functools.partial(_ctx_attn_kernel, use_sink=use_sink),
        out_shape=jax.ShapeDtypeStruct(q.shape, BF16),
        grid=(b, hkv),
        in_specs=[qblk, kblk, kblk, pl.BlockSpec((1, g * lq, 1), lambda bi, hi: (hi, 0, 0))],
        out_specs=qblk,
        compiler_params=_params("parallel", "parallel"),
        name="context_attention",
    )(q, k, v, sink_col)


def _flash_kernel(q_ref, k_ref, v_ref, o_ref, m_scr, acc_scr, *bufs, tk):
    dh = HEAD_DIM
    g = q_ref.shape[0] // dh
    n = k_ref.shape[2] // tk
    nb = len(bufs) // 2
    depth = nb - 1
    s_bufs, c_bufs = bufs[:nb], bufs[nb:]
    m_scr[...] = jnp.full_like(m_scr, NEG_INF)
    acc_scr[...] = jnp.zeros_like(acc_scr)

    def scores(j, slot):
        kk = k_ref[0, 0, pl.ds(pl.multiple_of(j * tk, tk), tk), :]
        for h in range(g):
            sc = jnp.dot(kk, q_ref[h * dh:(h + 1) * dh, :], preferred_element_type=F32)
            s_bufs[slot][h] = sc
            c_bufs[slot][h] = jnp.max(sc, axis=0, keepdims=True)

    def update(j, slot):
        vv = v_ref[0, 0, :, pl.ds(pl.multiple_of(j * tk, tk), tk)]
        m_olds = [m_scr[h] for h in range(g)]
        m_news = [jnp.maximum(m_olds[h], c_bufs[slot][h]) for h in range(g)]
        ps = [jnp.exp2(s_bufs[slot][h] - m_news[h]).astype(BF16) for h in range(g)]
        pvs = [jnp.dot(vv, ps[h], preferred_element_type=F32) for h in range(g)]
        for h in range(g):
            acc_scr[h] = jnp.exp2(m_olds[h] - m_news[h]) * acc_scr[h] + pvs[h]
            m_scr[h] = m_news[h]

    for j in range(min(depth, n)):
        scores(j, j % nb)
    loops = max(n - depth, 0) // nb

    def body(i, carry):
        for r in range(nb):
            scores(nb * i + r + depth, (r + depth) % nb)
            update(nb * i + r, r)
        return carry

    lax.fori_loop(0, loops, body, 0)
    for j in range(loops * nb, n):
        if j + depth < n:
            scores(j + depth, (j + depth) % nb)
        update(j, j % nb)
    acc = acc_scr[...]
    out = (acc[:, :dh] / acc[:, dh:dh + 1]).reshape(g * dh, -1)
    o_ref[...] = out.T.astype(o_ref.dtype)


def _flash(qt, k, vt, b, s, tq=None, tk=None):
    w = qt.shape[0]
    hkv, t, dh = k.shape[1], k.shape[2], k.shape[3]
    g = w // dh // hkv
    tq = tq or _tile(s, (256, 128))
    tk = tk or _tile(t, (256, 128))
    nq = s // tq
    ones_rows = 8
    vt = jnp.concatenate([vt, jnp.ones((b, hkv, ones_rows, t), vt.dtype)], axis=2)
    nv = dh + ones_rows
    return pl.pallas_call(
        functools.partial(_flash_kernel, tk=tk),
        out_shape=jax.ShapeDtypeStruct((b * s, w), BF16),
        grid=(b, hkv, nq),
        in_specs=[pl.BlockSpec((g * dh, tq), lambda bi, hi, i: (hi, bi * nq + i)),
                  pl.BlockSpec((1, 1, t, dh), lambda bi, hi, i: (bi, hi, 0, 0)),
                  pl.BlockSpec((1, 1, nv, t), lambda bi, hi, i: (bi, hi, 0, 0))],
        out_specs=pl.BlockSpec((tq, g * dh), lambda bi, hi, i: (bi * nq + i, hi)),
        scratch_shapes=([pltpu.VMEM((g, 1, tq), F32), pltpu.VMEM((g, nv, tq), F32)]
                        + [pltpu.VMEM((g, tk, tq), F32)] * (FLASH_DEPTH + 1)
                        + [pltpu.VMEM((g, 1, tq), F32)] * (FLASH_DEPTH + 1)),
        compiler_params=_params("parallel", "parallel", "arbitrary"),
        name="dense_attention",
    )(qt, k, vt)


def _bmm_kernel(a_ref, b_ref, o_ref):
    o_ref[0] = jnp.dot(a_ref[0], b_ref[0], preferred_element_type=F32).astype(o_ref.dtype)


def _group_matmul(a, bmat, name):
    g, m, k = a.shape
    n = bmat.shape[2]
    tm = _tile(m, (1024, 512, 256, 128, 64, 32, 16, 8))
    return pl.pallas_call(
        _bmm_kernel,
        out_shape=jax.ShapeDtypeStruct((g, m, n), F32),
        grid=(g, m // tm),
        in_specs=[pl.BlockSpec((1, tm, k), lambda gi, i: (gi, i, 0)),
                  pl.BlockSpec((1, k, n), lambda gi, i: (gi, 0, 0))],
        out_specs=pl.BlockSpec((1, tm, n), lambda gi, i: (gi, i, 0)),
        compiler_params=_params("parallel", "parallel"),
        name=name,
    )(a, bmat)


def _s5_scan_kernel(sre_ref, sim_ref, are_ref, aim_ref, zre_ref, zim_ref):
    nd, nsteps = sre_ref.shape[0], sre_ref.shape[1]
    for d in range(nd):
        a_re, a_im = are_ref[d], aim_ref[d]

        def body(i, carry, d=d, a_re=a_re, a_im=a_im):
            z_re, z_im = carry
            zre_ref[d, i] = z_re
            zim_ref[d, i] = z_im
            return (a_re * z_re - a_im * z_im + sre_ref[d, i], a_re * z_im + a_im * z_re + sim_ref[d, i])

        zero = jnp.zeros(sre_ref.shape[2:], F32)
        lax.fori_loop(0, nsteps, body, (zero, zero))


def _s5_scan(s_re, s_im, a_re, a_im):
    full = lambda arr: pl.BlockSpec(arr.shape, lambda i: (0,) * arr.ndim)
    return pl.pallas_call(
        _s5_scan_kernel,
        out_shape=[jax.ShapeDtypeStruct(s_re.shape, F32)] * 2,
        grid=(1,),
        in_specs=[full(s_re), full(s_im), full(a_re), full(a_im)],
        out_specs=[full(s_re), full(s_im)],
        compiler_params=_params("arbitrary"),
        name="s5_chunk_scan",
    )(s_re, s_im, a_re, a_im)


def _s5_glu_kernel(y_ref, u_ref, d_ref, w_ref, b_ref, o_ref):
    y = y_ref[...] + d_ref[...] * u_ref[...]
    y = jax.nn.gelu(y)
    gate = jnp.dot(y.astype(BF16), w_ref[...], preferred_element_type=F32) + b_ref[...]
    o_ref[...] = (y * jax.nn.sigmoid(gate)).astype(o_ref.dtype)


def _s5_glu(y, u, d_skip, w_glu, b_glu, tm):
    r, c = y.shape
    blk = pl.BlockSpec((tm, c), lambda i: (i, 0))
    vec = pl.BlockSpec((1, c), lambda i: (0, 0))
    return pl.pallas_call(
        _s5_glu_kernel,
        out_shape=jax.ShapeDtypeStruct((r, c), BF16),
        grid=(r // tm,),
        in_specs=[blk, blk, vec, pl.BlockSpec((c, c), lambda i: (0, 0)), vec],
        out_specs=blk,
        compiler_params=_params("parallel"),
        name="s5_readout_glu",
    )(y, u, d_skip.reshape(1, c), w_glu.astype(BF16), b_glu.reshape(1, c))


def _s5_tables(lam_re, lam_im, log_dt, b_re, b_im, c_re, c_im):
    ln, p, gc = S5_CHUNK, S5_STATE, S5_GROUP
    lam = lax.complex(lam_re.astype(F32), lam_im.astype(F32))
    dt = jnp.exp(log_dt.astype(F32))[..., None]
    a_bar = jnp.exp(lam * dt)
    b_scale = (a_bar - 1.0) / lam
    b_mat = lax.complex(b_re.astype(F32), b_im.astype(F32))
    c_mat = lax.complex(c_re.astype(F32), c_im.astype(F32))
    tau = jnp.arange(ln + 1, dtype=F32)
    apow = jnp.exp((lam * dt)[:, :, None, :] * tau[None, None, :, None])
    drive = b_scale[..., None] * b_mat[None]
    kern = jnp.real(jnp.einsum('gcp,dgtp,dgpe->dgtce', c_mat, apow[:, :, :ln], drive))
    kc = jnp.concatenate([kern[1, :, :0:-1], (kern[0, :, :1] + kern[1, :, :1]), kern[0, :, 1:]], axis=1)
    kc_e = kc.transpose(0, 3, 1, 2).reshape(S5_GROUPS, gc, (2 * ln - 1) * gc)
    toep = jnp.stack([kc_e[:, :, (ln - 1 - s) * gc:(2 * ln - 1 - s) * gc] for s in range(ln)], axis=1)
    toep = toep.reshape(S5_GROUPS, ln * gc, ln * gc)
    w_f = apow[0, :, ln - 1::-1][:, :ln, :, None] * drive[0][:, None]
    w_b = apow[1, :, :ln, :, None] * drive[1][:, None]
    def m_in(wc):
        wt = wc.transpose(0, 1, 3, 2).reshape(S5_GROUPS, ln * gc, p)
        return jnp.concatenate([jnp.real(wt), jnp.imag(wt)], axis=-1)
    min_all = jnp.concatenate([m_in(w_f), m_in(w_b)], axis=-1)
    o_f = c_mat[:, None] * apow[0, :, 1:ln + 1][:, :, None, :]
    o_b = c_mat[:, None] * apow[1, :, ln:0:-1][:, :, None, :]
    def m_out(oc):
        ot = oc.transpose(0, 3, 1, 2).reshape(S5_GROUPS, p, ln * gc)
        return jnp.concatenate([jnp.real(ot), -jnp.imag(ot)], axis=1)
    rhs = jnp.concatenate([toep, m_out(o_f), m_out(o_b)], axis=1)
    a_l = apow[:, :, ln].reshape(2, 1, S5_GROUPS * p)
    return min_all.astype(BF16), rhs.astype(BF16), jnp.real(a_l), jnp.imag(a_l)


def _s5_mixer(u_x, u_c, tables, d_skip, w_glu, b_glu):
    min_all, rhs, a_re, a_im = tables
    b, s, _ = u_x.shape
    ln, p, gc, ng = S5_CHUNK, S5_STATE, S5_GROUP, S5_GROUPS

    def chunks(u):
        nc = u.shape[1] // ln
        return u.reshape(b, nc, ln, ng, gc).transpose(3, 0, 1, 2, 4).reshape(ng, b * nc, ln * gc).astype(BF16), nc

    ux, ncx = chunks(u_x)
    uc, ncc = chunks(u_c)
    sx = _group_matmul(ux, min_all, "s5_local_state").reshape(ng, b, ncx, 4, p)
    sc = _group_matmul(uc, min_all, "s5_local_state_ctx").reshape(ng, b, ncc, 4, p)

    def scan_order(part):
        f = jnp.concatenate([sc[:, :, :, part], sx[:, :, :, part]], axis=2)
        r = jnp.concatenate([sc[:, :, ::-1, part + 2], sx[:, :, ::-1, part + 2]], axis=2)
        return jnp.stack([f, r]).transpose(0, 3, 2, 1, 4).reshape(2, ncc + ncx, b, ng * p)

    z_re, z_im = _s5_scan(scan_order(0), scan_order(1), a_re, a_im)

    def latent(z, d):
        zl = z[d, ncc:]
        if d == 1:
            zl = zl[::-1]
        return zl.reshape(ncx, b, ng, p).transpose(2, 1, 0, 3).reshape(ng, b * ncx, p)

    lhs = jnp.concatenate([ux, latent(z_re, 0).astype(BF16), latent(z_im, 0).astype(BF16),
                           latent(z_re, 1).astype(BF16), latent(z_im, 1).astype(BF16)], axis=-1)
    y = _group_matmul(lhs, rhs, "s5_outputs")
    y = y.reshape(ng, b, ncx, ln, gc).transpose(1, 2, 3, 0, 4).reshape(b * s, S5_CHANNELS)
    return _s5_glu(y, u_x.reshape(b * s, S5_CHANNELS), d_skip, w_glu, b_glu, _tile(b * s, (1024, 512, 256)))


def _out_proj_kernel(x_ref, a_ref, b_ref, wa_ref, wb_ref, gt_ref, g_ref, sc_ref, sh_ref, wr_ref, br_ref,
                     xo_ref, xn_ref, lg_ref):
    y = (jnp.dot(a_ref[...], wa_ref[...], preferred_element_type=F32)
         + jnp.dot(b_ref[...], wb_ref[...], preferred_element_type=F32))
    x = x_ref[...] + gt_ref[0] * y
    xo_ref[...] = x
    xn = x * lax.rsqrt(jnp.mean(x * x, axis=-1, keepdims=True) + EPS) * g_ref[...]
    xn = xn * (1.0 + sc_ref[0]) + sh_ref[0]
    hi = xn.astype(BF16)
    xn_ref[...] = hi
    lo = (xn - hi.astype(F32)).astype(BF16)
    part = jnp.dot(hi, wr_ref[...], preferred_element_type=F32)
    part = part[:, :LANES] + part[:, LANES:] + jnp.dot(lo, wr_ref[:, :LANES], preferred_element_type=F32)
    lg_ref[...] = part[:, :lg_ref.shape[1]] + br_ref[...]


def _out_proj(x, row0, a, bmix, wa, wb, gate, g_ffn, scale, shift, mod_map, w_router, b_router, tm):
    r = a.shape[0]
    d = x.shape[1]
    t0 = row0 // tm
    ne = w_router.shape[1]
    w_hi = w_router.astype(BF16)
    w_lo = (w_router - w_hi.astype(F32)).astype(BF16)
    w_router = jnp.concatenate([_pad_cols(w_hi), _pad_cols(w_lo)], axis=1)
    row = lambda w: pl.BlockSpec((tm, w), lambda i: (i, 0))
    mod = pl.BlockSpec((1, 1, d), lambda i: (mod_map(i), 0, 0))
    full = lambda arr: pl.BlockSpec(arr.shape, lambda i: (0, 0))
    return pl.pallas_call(
        _out_proj_kernel,
        out_shape=[jax.ShapeDtypeStruct((r, d), F32), jax.ShapeDtypeStruct((r, d), BF16),
                   jax.ShapeDtypeStruct((r, ne), F32)],
        grid=(r // tm,),
        in_specs=[pl.BlockSpec((tm, d), lambda i: (i + t0, 0)), row(a.shape[1]), row(bmix.shape[1]),
                  full(wa), full(wb), mod, pl.BlockSpec((1, d), lambda i: (0, 0)), mod, mod,
                  full(w_router), pl.BlockSpec((1, ne), lambda i: (0, 0))],
        out_specs=[row(d), row(d), row(ne)],
        compiler_params=_params("parallel"),
        name="out_proj",
    )(x, a, bmix, wa, wb, gate, g_ffn.reshape(1, d), scale, shift, w_router, b_router.reshape(1, ne))


def _route_kernel(lg_ref, idx_ref, gate_ref, pos_ref, cnt_ref, carry):
    tm, ne = lg_ref.shape

    @pl.when(pl.program_id(0) == 0)
    def _():
        carry[...] = jnp.zeros_like(carry)

    work = lg_ref[...]
    lane = lax.broadcasted_iota(I32, (tm, ne), 1).astype(F32)
    out_lane = lax.broadcasted_iota(I32, (tm, LANES), 1)
    vals, hots, idx_out = [], [], jnp.zeros((tm, LANES), I32)
    for kk in range(TOP_K):
        mx = jnp.max(work, axis=1, keepdims=True)
        idx = jnp.min(jnp.where(work == mx, lane, float(ne)), axis=1, keepdims=True)
        hot = lane == idx
        work = jnp.where(hot, -jnp.inf, work)
        vals.append(mx)
        hots.append(hot)
        idx_out = jnp.where(out_lane == kk, idx.astype(I32), idx_out)
    exps = [jnp.exp(vv - vals[0]) for vv in vals]
    tot = exps[0] + exps[1] + exps[2] + exps[3]
    multi = (hots[0] | hots[1] | hots[2] | hots[3]).astype(BF16)
    r_i = lax.broadcasted_iota(I32, (tm, tm), 0)
    c_i = lax.broadcasted_iota(I32, (tm, tm), 1)
    before = (r_i > c_i).astype(BF16)
    prefix = jnp.dot(before, multi, preferred_element_type=F32) + carry[...]
    gate_out = jnp.zeros((tm, LANES), F32)
    pos_out = jnp.zeros((tm, LANES), I32)
    for kk in range(TOP_K):
        gate_out = jnp.where(out_lane == kk, exps[kk] / tot, gate_out)
        pos = jnp.sum(jnp.where(hots[kk], prefix, 0.0), axis=1, keepdims=True).astype(I32)
        pos_out = jnp.where(out_lane == kk, pos, pos_out)
    idx_ref[...] = idx_out
    gate_ref[...] = gate_out
    pos_ref[...] = pos_out
    carry[...] = carry[...] + jnp.sum(multi.astype(F32), axis=0, keepdims=True)
    cnt_ref[...] = carry[...]


def _route(logits, tm, row0, n):
    ne = logits.shape[1]
    t0 = row0 // tm
    wide = pl.BlockSpec((tm, LANES), lambda i: (i, 0))
    return pl.pallas_call(
        _route_kernel,
        out_shape=[jax.ShapeDtypeStruct((n, LANES), I32), jax.ShapeDtypeStruct((n, LANES), F32),
                   jax.ShapeDtypeStruct((n, LANES), I32), jax.ShapeDtypeStruct((1, ne), F32)],
        grid=(n // tm,),
        in_specs=[pl.BlockSpec((tm, ne), lambda i: (i + t0, 0))],
        out_specs=[wide, wide, wide, pl.BlockSpec((1, ne), lambda i: (0, 0))],
        scratch_shapes=[pltpu.VMEM((1, ne), F32)],
        compiler_params=_params("arbitrary"),
        name="moe_route",
    )(logits)


def _expert_kernel(be_ref, nb_ref, x_ref, wgu_ref, bgu_ref, wd_ref, bd_ref, perm_ref, o_ref, wgu_bf, wd_bf):
    i = pl.program_id(0)
    fresh = jnp.logical_or(i == 0, be_ref[i] != be_ref[jnp.maximum(i - 1, 0)])

    @pl.when(jnp.logical_and(i < nb_ref[0], fresh))
    def _():
        tile = perm_ref.shape[0]
        for t in range(wgu_ref.shape[2] // tile):
            sl = slice(t * tile, (t + 1) * tile)
            wgu_bf[:, sl] = jnp.dot(wgu_ref[0, :, sl].astype(BF16), perm_ref[...],
                                    preferred_element_type=F32).astype(BF16)
        wd_bf[...] = wd_ref[0].astype(BF16)

    @pl.when(i < nb_ref[0])
    def _():
        h = jnp.dot(x_ref[...], wgu_bf[...], preferred_element_type=F32) + bgu_ref[0]
        acts = []
        for t in range(h.shape[1] // (2 * LANES)):
            glu = jnp.minimum(h[:, 2 * LANES * t:2 * LANES * t + LANES], SWIGLU_LIMIT)
            lin = jnp.clip(h[:, 2 * LANES * t + LANES:2 * LANES * (t + 1)], -SWIGLU_LIMIT, SWIGLU_LIMIT)
            acts.append((glu * jax.nn.sigmoid(SWIGLU_ALPHA * glu) * (lin + 1.0)).astype(BF16))
        act = jnp.concatenate(acts, axis=1)
        o_ref[...] = (jnp.dot(act, wd_bf[...], preferred_element_type=F32) + bd_ref[0]).astype(o_ref.dtype)

    @pl.when(i >= nb_ref[0])
    def _():
        o_ref[...] = jnp.zeros_like(o_ref)


def _experts(xs, blk_e, n_used, wgu, bgu, wd, bd):
    n, d = xs.shape
    ff2 = wgu.shape[2]
    nblk = n // MOE_ROWS
    tile = 2 * LANES
    j = np.arange(tile)
    col = np.where(j < LANES, 2 * j, 2 * (j - LANES) + 1)
    perm = jnp.asarray(np.arange(tile)[:, None] == col[None, :], BF16)
    wspec = lambda shape: pl.BlockSpec((1,) + shape, lambda i, be, nb: (be[i], 0, 0))
    return pl.pallas_call(
        _expert_kernel,
        out_shape=jax.ShapeDtypeStruct((n, d), BF16),
        grid_spec=pltpu.PrefetchScalarGridSpec(
            num_scalar_prefetch=2,
            grid=(nblk,),
            in_specs=[pl.BlockSpec((MOE_ROWS, d), lambda i, be, nb: (i, 0)),
                      wspec((d, ff2)), wspec((1, ff2)), wspec((ff2 // 2, d)), wspec((1, d)),
                      pl.BlockSpec((tile, tile), lambda i, be, nb: (0, 0))],
            out_specs=pl.BlockSpec((MOE_ROWS, d), lambda i, be, nb: (i, 0)),
            scratch_shapes=[pltpu.VMEM((d, ff2), BF16), pltpu.VMEM((ff2 // 2, d), BF16)]),
        compiler_params=_params("arbitrary"),
        name="moe_experts",
    )(blk_e, n_used, xs, wgu, bgu, wd, bd, perm)


def _combine_kernel(x_ref, y_ref, gate_ref, gt_ref, g_ref, o_ref, *, final_norm):
    gates = gate_ref[...]
    y = gates[:, 0:1] * y_ref[0].astype(F32)
    for kk in range(1, TOP_K):
        y = y + gates[:, kk:kk + 1] * y_ref[kk].astype(F32)
    x = x_ref[...] + gt_ref[0] * y
    if final_norm:
        x = x * lax.rsqrt(jnp.mean(x * x, axis=-1, keepdims=True) + EPS) * g_ref[...]
    o_ref[...] = x


def _combine_alias_kernel(x_ref, y_ref, gate_ref, gt_ref, g_ref, prev_ref, o_ref, *, final_norm):
    del prev_ref
    _combine_kernel(x_ref, y_ref, gate_ref, gt_ref, g_ref, o_ref, final_norm=final_norm)


def _combine(x, y4, gates, gate_mod, mod_map, g_final, final_norm, tm, row0, prev):
    r, d = x.shape
    n = y4.shape[1]
    t0 = row0 // tm
    in_specs = [pl.BlockSpec((tm, d), lambda i: (i + t0, 0)),
                pl.BlockSpec((TOP_K, tm, d), lambda i: (0, i, 0)),
                pl.BlockSpec((tm, LANES), lambda i: (i, 0)),
                pl.BlockSpec((1, 1, d), lambda i: (mod_map(i + t0), 0, 0)),
                pl.BlockSpec((1, d), lambda i: (0, 0))]
    args = [x, y4, gates, gate_mod, g_final.reshape(1, d)]
    body, aliases = _combine_kernel, {}
    if prev is not None:
        in_specs.append(pl.BlockSpec(memory_space=pl.ANY))
        args.append(prev)
        body, aliases = _combine_alias_kernel, {len(args) - 1: 0}
    return pl.pallas_call(
        functools.partial(body, final_norm=final_norm),
        out_shape=jax.ShapeDtypeStruct((r, d), F32),
        grid=(n // tm,),
        in_specs=in_specs,
        out_specs=pl.BlockSpec((tm, d), lambda i: (i + t0, 0)),
        input_output_aliases=aliases,
        compiler_params=_params("parallel"),
        name="moe_combine",
    )(*args)


def _moe_parts(tiles):
    total = sum(MOE_SPLIT)
    sizes = [tiles * w // total for w in MOE_SPLIT[:-1]]
    sizes.append(tiles - sum(sizes))
    return [sz for sz in sizes if sz > 0]


def _moe(x, xn, logits, gate_mod, mod_map, weights, g_final, final_norm, tm):
    wgu, bgu, wd, bd = weights
    r, d = xn.shape
    routed, row0 = [], 0
    for tiles in _moe_parts(r // tm):
        n = tiles * tm
        nblk = n * TOP_K // MOE_ROWS + N_EXPERTS
        blk_row0 = jnp.arange(nblk, dtype=I32) * MOE_ROWS
        idx_w, gates_w, pos_w, counts = _route(logits, tm, row0, n)
        idx, pos = idx_w[:, :TOP_K], pos_w[:, :TOP_K]
        counts = counts[0].astype(I32)
        padded = (counts + MOE_ROWS - 1) // MOE_ROWS * MOE_ROWS
        pad_end = jnp.cumsum(padded)
        pad_start = pad_end - padded
        dest = pad_start[idx] + pos
        blk_e = jnp.minimum(jnp.sum((pad_end[None, :] <= blk_row0[:, None]).astype(I32), axis=1), N_EXPERTS - 1)
        n_used = (pad_end[-1:] // MOE_ROWS).astype(I32)
        key = idx * n + jnp.arange(n, dtype=I32)[:, None]
        tok_sorted = jnp.sort(key.reshape(-1)) % n
        grp_start = jnp.cumsum(counts) - counts
        slot = blk_row0[:, None] + jnp.arange(MOE_ROWS, dtype=I32)[None, :]
        compact = jnp.minimum(slot + (grp_start - pad_start)[blk_e][:, None],
                              (grp_start + counts - 1)[blk_e][:, None])
        src = tok_sorted.at[jnp.clip(compact, 0, n * TOP_K - 1).reshape(-1)].get(mode="promise_in_bounds")
        xs = xn.at[src + row0].get(mode="promise_in_bounds")
        routed.append((xs, blk_e, n_used, dest, gates_w, row0, n))
        row0 += n
    ys = [_experts(xs, blk_e, n_used, wgu, bgu, wd, bd) for xs, blk_e, n_used, _, _, _, _ in routed]
    out = None
    for p, (_, _, _, dest, gates_w, row0, n) in enumerate(routed):
        y4 = ys[p].at[dest.T.reshape(-1)].get(mode="promise_in_bounds").reshape(TOP_K, n, d)
        out = _combine(x, y4, gates_w, gate_mod, mod_map, g_final, final_norm, tm, row0, out)
    return out


def _moe_weights(w_gate_up, b_gate_up, w_down, b_down):
    ne, d, ff2 = w_gate_up.shape
    b_tiled = b_gate_up.reshape(ne, ff2 // (2 * LANES), LANES, 2).swapaxes(-1, -2).reshape(ne, 1, ff2)
    return w_gate_up, b_tiled, w_down, b_down[:, None, :]


def _rope_tables(n_tokens):
    rows = n_tokens // GRID_W
    row = jnp.repeat(jnp.arange(rows, dtype=I32), GRID_W).astype(F32)
    col = jnp.tile(jnp.arange(GRID_W, dtype=I32), rows).astype(F32)
    inv = ROPE_BASE ** (-jnp.arange(0, ROPE_AXIS_DIM, 2, dtype=F32) / ROPE_AXIS_DIM)
    ang_r, ang_c = row[:, None] * inv, col[:, None] * inv
    cos = jnp.concatenate([jnp.cos(ang_r)] * 2 + [jnp.cos(ang_c)] * 2, axis=-1)
    sin = jnp.concatenate([jnp.sin(ang_r)] * 2 + [jnp.sin(ang_c)] * 2, axis=-1)
    return cos, sin


def _rot_perm():
    q = ROPE_AXIS_DIM // 2
    d = np.arange(HEAD_DIM)
    first = (d % ROPE_AXIS_DIM) < q
    perm = np.where(first, d + q, d - q)
    sign = np.where(first, -1.0, 1.0).astype(np.float32)
    return perm, sign


def _rot_cols(w, n_heads):
    perm, sign = _rot_perm()
    k = w.shape[0]
    wh = w.reshape(k, n_heads, HEAD_DIM)
    return (wh[:, :, perm] * sign).reshape(k, n_heads * HEAD_DIM)


def _to_heads(t, b, n_heads):
    return t.reshape(b, -1, n_heads, HEAD_DIM).transpose(0, 2, 1, 3)


def _from_heads(t):
    b, h, tt, dh = t.shape
    return t.transpose(0, 2, 1, 3).reshape(b * tt, h * dh)


def _layer_even(xa, mod, b, s, lc, g_mix, g_ffn, w_in, w_out, gate_bias, mlstm_norm, sink, router, moe_w):
    d = xa.shape[1]
    nctx = b * lc
    tm = _tile(math.gcd(nctx, s), (512, 256, 128))
    mod_map = lambda i: jnp.where(i * tm < nctx, b, (i * tm - nctx) // s)
    sh1, sc1, gt1, sh2, sc2, gt2 = [mod[:, j][:, None, :] for j in range(6)]

    hm, hs, hk = MLSTM_HEADS * HEAD_DIM, SWA_HEADS * HEAD_DIM, SWA_KV_HEADS * HEAD_DIM
    o0 = np.cumsum([0, hm, hm, hm, hm, 4 * MLSTM_HEADS, hs, hk, hk])
    seg = lambda j: w_in[:, o0[j]:o0[j + 1]]
    w_cat = jnp.concatenate([seg(0), seg(1) * ATTN_SCALE, seg(2), seg(3), _pad_cols(seg(4))], axis=1).astype(BF16)
    w_t = jnp.concatenate([seg(5), _rot_cols(seg(5), SWA_HEADS), seg(6), _rot_cols(seg(6), SWA_KV_HEADS), seg(7)],
                          axis=1).T.astype(BF16)
    widths = [hm, hm, hm, hm, LANES]
    dts = [BF16] * 4 + [F32]
    qa, ka, va, oa, gts, qt, qt_r, kt, kt_r, vt = _in_proj(
        xa, g_mix, sc1, sh1, mod_map, w_cat, widths, dts, tm,
        wt=w_t, t_widths=[hs, hs, hk, hk, hk], t_dtypes=[BF16] * 5)

    hf, hb = _mlstm(qa, ka, va, gts, _pad_cols(gate_bias.astype(F32).reshape(1, -1)), b, lc, s)
    mix_a = _mlstm_out(hf, hb, oa, mlstm_norm.reshape(-1), tm)

    cos, sin = _rope_tables(s)
    ones = jnp.ones((HEAD_DIM,), F32)
    g = SWA_HEADS // SWA_KV_HEADS
    heads = lambda t, n: _to_heads(t, b, n)
    log2e = math.log2(math.e)
    q_x = _q_prep_t(qt, qt_r, nctx, b * s, cos.T, sin.T, ones, ones, ATTN_SCALE * log2e, norm=False)
    k_x = _q_prep_t(kt, kt_r, nctx, b * s, cos.T, sin.T, ones, ones, 1.0, norm=False)
    k_c, v_c = heads(kt[:, :nctx].T, SWA_KV_HEADS), heads(vt[:, :nctx].T, SWA_KV_HEADS)
    q_c = heads(qt[:, :nctx].T, SWA_HEADS)
    sink_h = sink.astype(F32).reshape(SWA_KV_HEADS, g, 1)
    att_x = _swa(q_x, k_x, (kt, vt), jnp.broadcast_to(sink_h * log2e, (SWA_KV_HEADS, g, 2 * SWA_BLOCK)), b, s, lc)
    att_c = _ctx_attn(q_c, k_c, v_c, jnp.broadcast_to(sink_h[..., None], (SWA_KV_HEADS, g, lc, 1))
                      .reshape(SWA_KV_HEADS, g * lc, 1), True)
    mix_b = jnp.concatenate([_from_heads(att_c), att_x], axis=0)

    w_router, b_router = router
    xa, xn2, logits = _out_proj(xa, 0, mix_a, mix_b, w_out[:hm].astype(BF16), w_out[hm:].astype(BF16), gt1, g_ffn,
                                sc2, sh2, mod_map, w_router.astype(F32), b_router.astype(F32), tm)
    return _moe(xa, xn2, logits, gt2, mod_map, moe_w, g_ffn, False, tm)


def _layer_odd_last(xa, mod, b, s, lc, g_mix, g_ffn, w_in, w_out, s5_params, d_skip, w_glu, b_glu,
                    q_norm, k_norm, router, moe_w, g_final):
    nctx = b * lc
    tm = _tile(math.gcd(nctx, s), (512, 256, 128))
    mod_map = lambda i: jnp.where(i * tm < nctx, b, (i * tm - nctx) // s)
    lat_map = lambda i: i * tm // s
    sh1, sc1, gt1, sh2, sc2, gt2 = [mod[:, j][:, None, :] for j in range(6)]

    hq, hk = ATT_HEADS * HEAD_DIM, ATT_KV_HEADS * HEAD_DIM
    o1 = np.cumsum([0, S5_CHANNELS, hq, hk, hk])
    seg = lambda j: w_in[:, o1[j]:o1[j + 1]]
    kpad = lambda w: _pad_cols(w, 2 * LANES)
    w_cat = jnp.concatenate([seg(0), kpad(seg(2)), kpad(_rot_cols(seg(2), ATT_KV_HEADS)), kpad(seg(3))],
                            axis=1).astype(BF16)
    w_q = jnp.concatenate([seg(1), _rot_cols(seg(1), ATT_HEADS)], axis=1).T.astype(BF16)
    widths = [S5_CHANNELS, 2 * LANES, 2 * LANES, 2 * LANES]
    dts = [F32, BF16, BF16, BF16]
    u, k, k_r, v, qt, qt_r = _in_proj(xa, g_mix, sc1, sh1, mod_map, w_cat, widths, dts, tm,
                                      wt=w_q, t_widths=[hq, hq], t_dtypes=[BF16, BF16])

    mix_a = _s5_mixer(u[nctx:].reshape(b, s, S5_CHANNELS), u[:nctx].reshape(b, lc, S5_CHANNELS),
                      _s5_tables(*s5_params), d_skip, w_glu, b_glu)

    cos, sin = _rope_tables(s)
    perm, _ = _rot_perm()
    qn, kn = q_norm.astype(F32), k_norm.astype(F32)
    heads = lambda t, n: _to_heads(t[:, :n * HEAD_DIM], b, n)
    q_x = _q_prep_t(qt, qt_r, nctx, b * s, cos.T, sin.T, qn, qn[perm], ATTN_SCALE * math.log2(math.e))
    k_x = _qk_prep(heads(k[nctx:], ATT_KV_HEADS), heads(k_r[nctx:], ATT_KV_HEADS), cos, sin, kn, kn[perm],
                   norm=True, rope=True, scale=1.0)
    kc_raw = heads(k[:nctx], ATT_KV_HEADS)
    k_c = _qk_prep(kc_raw, kc_raw, cos[:lc], sin[:lc], kn, kn, norm=True, rope=False, scale=1.0)
    k_all = jnp.concatenate([k_c, k_x], axis=2)
    v_all = jnp.concatenate([heads(v[:nctx], ATT_KV_HEADS), heads(v[nctx:], ATT_KV_HEADS)], axis=2)
    mix_b = _flash(q_x, k_all, v_all.swapaxes(-1, -2), b, s)

    w_router, b_router = router
    hs = S5_CHANNELS
    x, xn2, logits = _out_proj(xa, nctx, mix_a, mix_b, w_out[:hs].astype(BF16), w_out[hs:].astype(BF16), gt1, g_ffn,
                               sc2, sh2, lat_map, w_router.astype(F32), b_router.astype(F32), tm)
    return _moe(x, xn2, logits, gt2, lat_map, moe_w, g_final, True, tm)


def kernel(x, c, ctx, c_ctx, l0_w_mod, l0_b_mod, l0_g_mix, l0_g_ffn, l0_w_in, l0_w_out, l0_gate_bias, l0_mlstm_norm, l0_sink, l0_w_router, l0_b_router, l0_w_gate_up, l0_b_gate_up, l0_w_down, l0_b_down, l1_w_mod, l1_b_mod, l1_g_mix, l1_g_ffn, l1_w_in, l1_w_out, l1_lam_re, l1_lam_im, l1_log_dt, l1_b_re, l1_b_im, l1_c_re, l1_c_im, l1_d_skip, l1_w_glu, l1_b_glu, l1_q_norm, l1_k_norm, l1_w_router, l1_b_router, l1_w_gate_up, l1_b_gate_up, l1_w_down, l1_b_down, g_final):
    b, s, d = x.shape
    lc = ctx.shape[1]
    cond = jnp.concatenate([c, c_ctx[None, :]], axis=0)
    cond = jnp.pad(cond, ((0, (-(b + 1)) % 8), (0, 0)))
    mod0 = _silu_linear(cond, l0_w_mod, l0_b_mod)[:b + 1].reshape(b + 1, 6, d)
    mod1 = _silu_linear(cond, l1_w_mod, l1_b_mod)[:b + 1].reshape(b + 1, 6, d)

    moe0 = _moe_weights(l0_w_gate_up, l0_b_gate_up, l0_w_down, l0_b_down)
    moe1 = _moe_weights(l1_w_gate_up, l1_b_gate_up, l1_w_down, l1_b_down)
    xa = jnp.concatenate([ctx.reshape(b * lc, d), x.reshape(b * s, d)], axis=0)
    xa = _layer_even(xa, mod0, b, s, lc, l0_g_mix, l0_g_ffn, l0_w_in, l0_w_out, l0_gate_bias, l0_mlstm_norm,
                     l0_sink, (l0_w_router, l0_b_router), moe0)
    out = _layer_odd_last(xa, mod1, b, s, lc, l1_g_mix, l1_g_ffn, l1_w_in, l1_w_out,
                          (l1_lam_re, l1_lam_im, l1_log_dt, l1_b_re, l1_b_im, l1_c_re, l1_c_im),
                          l1_d_skip, l1_w_glu, l1_b_glu, l1_q_norm, l1_k_norm, (l1_w_router, l1_b_router),
                          moe1, g_final)
    return out.reshape(b, s, d)
```

```python
import functools
import math

import jax
import jax.numpy as jnp
import numpy as np
from jax import lax
from jax.experimental import pallas as pl
from jax.experimental.pallas import tpu as pltpu

F32 = jnp.float32
BF16 = jnp.bfloat16
I32 = jnp.int32

GRID_W = 64
HEAD_DIM = 64
ATTN_SCALE = HEAD_DIM ** -0.5
ROPE_AXIS_DIM = HEAD_DIM // 2
ROPE_BASE = 10000.0
EPS = 1e-6
NEG_INF = -1e30

MLSTM_HEADS = 8
MLSTM_CHUNKS = (256, 128, 64)
SWA_HEADS = 8
SWA_KV_HEADS = 2
SWA_WINDOW = 128
SWA_BLOCK = 128
S5_CHANNELS = 256
S5_GROUP = 16
S5_GROUPS = S5_CHANNELS // S5_GROUP
S5_STATE = 64
S5_CHUNK = 64
ATT_HEADS = 12
ATT_KV_HEADS = 3
N_EXPERTS = 32
TOP_K = 4
SWIGLU_LIMIT = 7.0
SWIGLU_ALPHA = 1.702

LANES = 128
VMEM_LIMIT = 56 * 1024 * 1024
MOE_ROWS = 512
MOE_SPLIT = (1, 1)
FLASH_DEPTH = 6
HIGHEST = lax.Precision.HIGHEST


def _params(*sem):
    return pltpu.CompilerParams(dimension_semantics=sem, vmem_limit_bytes=VMEM_LIMIT)


def _tile(n, prefs):
    for t in prefs:
        if n % t == 0:
            return t
    return n


def _pad_cols(w, mult=LANES):
    pad = (-w.shape[-1]) % mult
    if pad:
        w = jnp.pad(w, [(0, 0)] * (w.ndim - 1) + [(0, pad)])
    return w


def _linear_kernel(x_ref, w_ref, b_ref, o_ref):
    x = x_ref[...]
    x = x * jax.nn.sigmoid(x)
    o_ref[...] = jnp.dot(x, w_ref[...], precision=HIGHEST, preferred_element_type=F32) + b_ref[...]


def _silu_linear(x, w, b):
    m, k = x.shape
    n = w.shape[1]
    tn = _tile(n, (1024, 512, 256, 128))
    return pl.pallas_call(
        _linear_kernel,
        out_shape=jax.ShapeDtypeStruct((m, n), F32),
        grid=(n // tn,),
        in_specs=[pl.BlockSpec((m, k), lambda j: (0, 0)),
                  pl.BlockSpec((k, tn), lambda j: (0, j)),
                  pl.BlockSpec((1, tn), lambda j: (0, j))],
        out_specs=pl.BlockSpec((m, tn), lambda j: (0, j)),
        compiler_params=_params("arbitrary"),
        name="adaln_linear",
    )(x, w, b.reshape(1, n))


def _in_proj_kernel(x_ref, g_ref, sc_ref, sh_ref, w_ref, wt_ref, *out_refs, widths, t_widths):
    x = x_ref[...]
    xn = x * lax.rsqrt(jnp.mean(x * x, axis=-1, keepdims=True) + EPS) * g_ref[...]
    xb = (xn * (1.0 + sc_ref[0]) + sh_ref[0]).astype(BF16)
    off = 0
    for o_ref, w in zip(out_refs, widths):
        o_ref[...] = jnp.dot(xb, w_ref[:, off:off + w], preferred_element_type=F32).astype(o_ref.dtype)
        off += w
    off = 0
    for o_ref, w in zip(out_refs[len(widths):], t_widths):
        o_ref[...] = lax.dot_general(wt_ref[off:off + w, :], xb, (((1,), (1,)), ((), ())),
                                     preferred_element_type=F32).astype(o_ref.dtype)
        off += w


def _in_proj(x, g, scale, shift, mod_map, w, widths, dtypes, tm, wt=None, t_widths=(), t_dtypes=()):
    r, d = x.shape
    if wt is None:
        wt = jnp.zeros((8, d), BF16)
    return pl.pallas_call(
        functools.partial(_in_proj_kernel, widths=tuple(widths), t_widths=tuple(t_widths)),
        out_shape=([jax.ShapeDtypeStruct((r, wd), dt) for wd, dt in zip(widths, dtypes)]
                   + [jax.ShapeDtypeStruct((wd, r), dt) for wd, dt in zip(t_widths, t_dtypes)]),
        grid=(r // tm,),
        in_specs=[pl.BlockSpec((tm, d), lambda i: (i, 0)),
                  pl.BlockSpec((1, d), lambda i: (0, 0)),
                  pl.BlockSpec((1, 1, d), lambda i: (mod_map(i), 0, 0)),
                  pl.BlockSpec((1, 1, d), lambda i: (mod_map(i), 0, 0)),
                  pl.BlockSpec(w.shape, lambda i: (0, 0)),
                  pl.BlockSpec(wt.shape, lambda i: (0, 0))],
        out_specs=([pl.BlockSpec((tm, wd), lambda i: (i, 0)) for wd in widths]
                   + [pl.BlockSpec((wd, tm), lambda i: (0, i)) for wd in t_widths]),
        compiler_params=_params("parallel"),
        name="in_proj",
    )(x, g.reshape(1, d), scale, shift, w, wt)


def _qk_prep_kernel(x_ref, xr_ref, cos_ref, sin_ref, g_ref, gr_ref, o_ref, *, norm, rope, scale):
    x = x_ref[0].astype(F32)
    if norm:
        s = lax.rsqrt(jnp.mean(x * x, axis=-1, keepdims=True) + EPS)
        x = x * s * g_ref[...]
    if rope:
        xr = xr_ref[0].astype(F32)
        if norm:
            xr = xr * s * gr_ref[...]
        x = x * cos_ref[...] + xr * sin_ref[...]
    o_ref[0] = (x * scale).astype(o_ref.dtype)


def _qk_prep(x, x_rot, cos, sin, gain, gain_rot, *, norm, rope, scale):
    b, h, t, dh = x.shape
    tt = _tile(t, (512, 256, 128))
    blk = pl.BlockSpec((1, h, tt, dh), lambda i, j: (i, 0, j, 0))
    tab = pl.BlockSpec((tt, dh), lambda i, j: (j, 0))
    vec = pl.BlockSpec((1, dh), lambda i, j: (0, 0))
    return pl.pallas_call(
        functools.partial(_qk_prep_kernel, norm=norm, rope=rope, scale=scale),
        out_shape=jax.ShapeDtypeStruct(x.shape, BF16),
        grid=(b, t // tt),
        in_specs=[blk, blk, tab, tab, vec, vec],
        out_specs=blk,
        compiler_params=_params("parallel", "parallel"),
        name="qk_prep",
    )(x, x_rot, cos, sin, gain.reshape(1, dh), gain_rot.reshape(1, dh))


def _q_prep_t_kernel(x_ref, xr_ref, cos_ref, sin_ref, g_ref, gr_ref, o_ref, *, norm, scale):
    w, tt = x_ref.shape
    nh = w // HEAD_DIM
    x = x_ref[...].astype(F32).reshape(nh, HEAD_DIM, tt)
    xr = xr_ref[...].astype(F32).reshape(nh, HEAD_DIM, tt)
    if norm:
        s = lax.rsqrt(jnp.mean(x * x, axis=1, keepdims=True) + EPS)
        x, xr = x * s * g_ref[...], xr * s * gr_ref[...]
    y = x * cos_ref[...] + xr * sin_ref[...]
    o_ref[...] = (y * scale).reshape(w, tt).astype(o_ref.dtype)


def _q_prep_t(xt, xt_rot, col0, n_cols, cos_t, sin_t, gain, gain_rot, scale, norm=True):
    w = xt.shape[0]
    s = cos_t.shape[1]
    tt = _tile(math.gcd(col0, s), (512, 256, 128))
    blk = pl.BlockSpec((w, tt), lambda j: (0, j + col0 // tt))
    tab = pl.BlockSpec((HEAD_DIM, tt), lambda j: (0, j % (s // tt)))
    vec = pl.BlockSpec((HEAD_DIM, 1), lambda j: (0, 0))
    return pl.pallas_call(
        functools.partial(_q_prep_t_kernel, norm=norm, scale=scale),
        out_shape=jax.ShapeDtypeStruct((w, n_cols), BF16),
        grid=(n_cols // tt,),
        in_specs=[blk, blk, tab, tab, vec, vec],
        out_specs=pl.BlockSpec((w, tt), lambda j: (0, j)),
        compiler_params=_params("parallel"),
        name="q_prep_t",
    )(xt, xt_rot, cos_t, sin_t, gain.reshape(HEAD_DIM, 1), gain_rot.reshape(HEAD_DIM, 1))


def _log_sigmoid(x):
    return jnp.minimum(x, 0.0) - jnp.log(1.0 + jnp.exp(-jnp.abs(x)))


def _mlstm_kernel(qf, kf, vf, gf, qb, kb, vb, gb, bias_ref, hf_ref, hb_ref, c_scr, n_scr, m_scr):
    @pl.when(pl.program_id(1) == 0)
    def _():
        c_scr[...] = jnp.zeros_like(c_scr)
        n_scr[...] = jnp.zeros_like(n_scr)
        m_scr[...] = jnp.full_like(m_scr, NEG_INF)

    gates = [_mlstm_gates(d, g_ref, bias_ref) for d, g_ref in enumerate((gf, gb))]
    for d in range(2):
        _mlstm_stages((d,), gates, qf, kf, vf, qb, kb, vb, hf_ref, hb_ref, c_scr, n_scr, m_scr)


def _mlstm_gates(d, g_ref, bias_ref):
    ch, nh = g_ref.shape[0], MLSTM_HEADS
    row = lax.broadcasted_iota(I32, (ch, ch), 0)
    col = lax.broadcasted_iota(I32, (ch, ch), 1)
    seen = (col <= row) if d == 0 else (col >= row)
    tri = seen.astype(F32)
    tri_t = ((row <= col) if d == 0 else (row >= col)).astype(F32)
    g = g_ref[...] + bias_ref[...]
    g_t = g.T
    lo = 2 * nh * d
    log2e = math.log2(math.e)
    li_col, lf_col = g[:, lo:lo + nh] * log2e, _log_sigmoid(g[:, lo + nh:lo + 2 * nh]) * log2e
    li_row, lf_row = g_t[lo:lo + nh, :] * log2e, _log_sigmoid(g_t[lo + nh:lo + 2 * nh, :]) * log2e
    b_col = jnp.dot(tri, lf_col, precision=HIGHEST, preferred_element_type=F32)
    b_row = jnp.dot(lf_row, tri_t, precision=HIGHEST, preferred_element_type=F32)
    return seen, b_col, b_row, li_col, li_row - b_row


def _mlstm_stages(dirs, gates, qf, kf, vf, qb, kb, vb, hf_ref, hb_ref, c_scr, n_scr, m_scr):
    ch, nh, dh = qf.shape[0], MLSTM_HEADS, HEAD_DIM
    lane_lo = lax.broadcasted_iota(I32, (ch, 2 * dh), 1) < dh
    row_lo = lax.broadcasted_iota(I32, (2 * dh, 2 * dh), 0) < dh
    col_lo = lax.broadcasted_iota(I32, (2 * dh, 2 * dh), 1) < dh
    vec_lo = lax.broadcasted_iota(I32, (1, 2 * dh), 1) < dh
    pair = lambda a, b: jnp.where(lane_lo, a, b)
    refs = ((qf, kf, vf, hf_ref), (qb, kb, vb, hb_ref))
    jobs = [(d, p) for d in dirs for p in range(nh // 2)]
    sl = lambda p: slice(2 * dh * p, 2 * dh * (p + 1))

    qk = {}
    for d, p in jobs:
        q2, k2 = refs[d][0][:, sl(p)], refs[d][1][:, sl(p)]
        zero = jnp.zeros_like(q2)
        q_st = jnp.concatenate([jnp.where(lane_lo, q2, zero), jnp.where(lane_lo, zero, q2)], axis=0)
        qk[d, p] = lax.dot_general(q_st, k2, (((1,), (1,)), ((), ())), preferred_element_type=F32)

    st = {}
    for d in dirs:
        seen, b_col, b_row, li_col, src_row = gates[d]
        last = ch - 1 if d == 0 else 0
        for h in range(nh):
            bc, br = b_col[:, h:h + 1], b_row[h:h + 1, :]
            lic = li_col[:, h:h + 1]
            b_last = br[:, last:last + 1]
            m_prev = m_scr[d, h // 2][:, (h % 2) * dh:(h % 2) * dh + 1]
            d_log = jnp.where(seen, bc + src_row[h:h + 1, :], NEG_INF)
            inter_log = bc + m_prev
            m_t = jnp.maximum(inter_log, jnp.max(d_log, axis=1, keepdims=True))
            w_log = b_last - bc + lic
            m_new = jnp.maximum(b_last + m_prev, jnp.max(w_log, axis=0, keepdims=True))
            st[d, h] = dict(dmat=jnp.exp2(d_log - m_t), inter=jnp.exp2(inter_log - m_t), floor=jnp.exp2(-m_t),
                            m_new=m_new, decay=jnp.exp2(b_last + m_prev - m_new), wn=jnp.exp2(w_log - m_new))

    mm = {}
    for d, p in jobs:
        a, b = st[d, 2 * p], st[d, 2 * p + 1]
        q2, k2, v2 = refs[d][0][:, sl(p)], refs[d][1][:, sl(p)], refs[d][2][:, sl(p)]
        s = qk[d, p] * jnp.concatenate([a["dmat"], b["dmat"]], axis=0)
        kw = k2.astype(F32) * pair(a["wn"], b["wn"])
        mm[d, p] = dict(
            s_sum=jnp.sum(s, axis=1, keepdims=True), kw_sum=jnp.sum(kw, axis=0, keepdims=True),
            sv=jnp.dot(s.astype(BF16), v2, preferred_element_type=F32),
            q_c=jnp.dot(q2, c_scr[d, p].astype(BF16), preferred_element_type=F32),
            kv=lax.dot_general(kw.astype(BF16), v2, (((0,), (0,)), ((), ())), preferred_element_type=F32))

    for d, p in jobs:
        a, b, r = st[d, 2 * p], st[d, 2 * p + 1], mm[d, p]
        q2 = refs[d][0][:, sl(p)]
        n_prev = n_scr[d, p]
        num = pair(a["inter"], b["inter"]) * r["q_c"] + jnp.where(lane_lo, r["sv"][:ch], r["sv"][ch:])
        qn = q2.astype(F32) * n_prev
        qn_a = jnp.sum(jnp.where(lane_lo, qn, 0.0), axis=1, keepdims=True)
        qn_b = jnp.sum(jnp.where(lane_lo, 0.0, qn), axis=1, keepdims=True)
        den_a = jnp.maximum(jnp.abs(a["inter"] * qn_a + r["s_sum"][:ch]), a["floor"])
        den_b = jnp.maximum(jnp.abs(b["inter"] * qn_b + r["s_sum"][ch:]), b["floor"])
        refs[d][3][:, sl(p)] = (num / pair(den_a, den_b)).astype(refs[d][3].dtype)
        c_scr[d, p] = (jnp.where(row_lo, a["decay"], b["decay"]) * c_scr[d, p]
                       + jnp.where(row_lo == col_lo, r["kv"], 0.0))
        n_scr[d, p] = jnp.where(vec_lo, a["decay"], b["decay"]) * n_prev + r["kw_sum"]
        m_scr[d, p] = jnp.where(vec_lo, a["m_new"], b["m_new"])


def _mlstm(q, k, v, gates, bias, b, lc, s):
    r, w = q.shape
    ch = _tile(math.gcd(lc, s), MLSTM_CHUNKS)
    ncc, ncx = lc // ch, s // ch
    base = b * ncc

    def fwd(i, c):
        return jnp.where(c < ncc, i * ncc + c, base + i * ncx + (c - ncc)), 0

    def bwd(i, c):
        return jnp.where(c < ncc, i * ncc + (ncc - 1 - c), base + i * ncx + (ncx - 1 - (c - ncc))), 0

    spec = lambda width, m: pl.BlockSpec((ch, width), m)
    npair = MLSTM_HEADS // 2
    return pl.pallas_call(
        _mlstm_kernel,
        out_shape=[jax.ShapeDtypeStruct((r, w), BF16)] * 2,
        grid=(b, ncc + ncx),
        in_specs=[spec(w, fwd), spec(w, fwd), spec(w, fwd), spec(LANES, fwd),
                  spec(w, bwd), spec(w, bwd), spec(w, bwd), spec(LANES, bwd),
                  pl.BlockSpec((1, LANES), lambda i, c: (0, 0))],
        out_specs=[spec(w, fwd), spec(w, bwd)],
        scratch_shapes=[pltpu.VMEM((2, npair, 2 * HEAD_DIM, 2 * HEAD_DIM), F32),
                        pltpu.VMEM((2, npair, 1, 2 * HEAD_DIM), F32),
                        pltpu.VMEM((2, npair, 1, 2 * HEAD_DIM), F32)],
        compiler_params=_params("parallel", "arbitrary"),
        name="mlstm_scan",
    )(q, k, v, gates, q, k, v, gates, bias)


def _mlstm_out_kernel(hf_ref, hb_ref, o_ref, nrm_ref, ones_ref, y_ref):
    h = hf_ref[...].astype(F32) + hb_ref[...].astype(F32)
    ms = jnp.dot((h * h).astype(BF16), ones_ref[...], preferred_element_type=F32) * (1.0 / HEAD_DIM)
    hn = h * lax.rsqrt(ms + EPS) * nrm_ref[...]
    y_ref[...] = (jax.nn.sigmoid(o_ref[...].astype(F32)) * hn).astype(y_ref.dtype)


def _head_ones(width):
    idx = np.arange(width) // HEAD_DIM
    return jnp.asarray(idx[:, None] == idx[None, :], BF16)


def _mlstm_out(hf, hb, o, norm, tm):
    r, w = hf.shape
    blk = pl.BlockSpec((tm, w), lambda i: (i, 0))
    return pl.pallas_call(
        _mlstm_out_kernel,
        out_shape=jax.ShapeDtypeStruct((r, w), BF16),
        grid=(r // tm,),
        in_specs=[blk, blk, blk,
                  pl.BlockSpec((1, w), lambda i: (0, 0)),
                  pl.BlockSpec((w, w), lambda i: (0, 0))],
        out_specs=blk,
        compiler_params=_params("parallel"),
        name="mlstm_out",
    )(hf, hb, o, norm.reshape(1, w), _head_ones(w))


def _swa_kernel(q_ref, k0, k1, k2, k3, v0, v1, v2, v3, kc_ref, vc_ref, sink_ref, o_ref, *, seq):
    w, dh = SWA_BLOCK, HEAD_DIM
    tq = q_ref.shape[1]
    g = q_ref.shape[0] // dh
    i = pl.program_id(2)
    k_loc = jnp.concatenate([k0[...], k1[...], k2[...], k3[...]], axis=1)
    v_loc = jnp.concatenate([v0[...], v1[...], v2[...], v3[...]], axis=1)
    kpos = (2 * i - 1) * w + lax.broadcasted_iota(I32, (4 * w, tq), 0)
    qpos = i * tq + lax.broadcasted_iota(I32, (4 * w, tq), 1)
    valid = (jnp.abs(qpos - kpos) <= SWA_WINDOW) & (kpos >= 0) & (kpos < seq)
    t_dot = lambda kt, q: lax.dot_general(kt, q, (((0,), (0,)), ((), ())), preferred_element_type=F32)
    qs = [q_ref[h * dh:(h + 1) * dh, :] for h in range(g)]
    s_loc = [jnp.where(valid, t_dot(k_loc, qs[h]), NEG_INF) for h in range(g)]
    s_ctx = [t_dot(kc_ref[...], qs[h]) for h in range(g)]
    outs = []
    for h in range(g):
        sink = sink_ref[0, h:h + 1, :]
        m = jnp.maximum(jnp.maximum(jnp.max(s_loc[h], axis=0, keepdims=True),
                                    jnp.max(s_ctx[h], axis=0, keepdims=True)), sink)
        p_loc = jnp.exp2(s_loc[h] - m)
        p_ctx = jnp.exp2(s_ctx[h] - m)
        den = jnp.sum(p_loc, axis=0, keepdims=True) + jnp.sum(p_ctx, axis=0, keepdims=True) + jnp.exp2(sink - m)
        o = (jnp.dot(v_loc, p_loc.astype(BF16), preferred_element_type=F32)
             + jnp.dot(vc_ref[...], p_ctx.astype(BF16), preferred_element_type=F32))
        outs.append(o / den)
    o_ref[...] = jnp.concatenate(outs, axis=0).T.astype(o_ref.dtype)


def _swa(qt, kt, kvt, sink2, b, s, lc):
    dh = HEAD_DIM
    nh, hkv = qt.shape[0] // dh, kt.shape[0] // dh
    g = nh // hkv
    w = SWA_BLOCK
    tq = 2 * w
    nb, nq = s // w, s // tq
    c0 = b * lc // w
    clampi = lambda j: jnp.clip(j, 0, nb - 1)
    kspec = lambda o: pl.BlockSpec((dh, w), lambda bi, hi, i: (hi, bi * nb + clampi(2 * i + o)))
    vspec = lambda o: pl.BlockSpec((dh, w), lambda bi, hi, i: (hi, c0 + bi * nb + clampi(2 * i + o)))
    cspec = pl.BlockSpec((dh, lc), lambda bi, hi, i: (hi, bi))
    k_all, v_all = kvt
    return pl.pallas_call(
        functools.partial(_swa_kernel, seq=s),
        out_shape=jax.ShapeDtypeStruct((b * s, nh * dh), BF16),
        grid=(b, hkv, nq),
        in_specs=[pl.BlockSpec((g * dh, tq), lambda bi, hi, i: (hi, bi * nq + i)),
                  kspec(-1), kspec(0), kspec(1), kspec(2), vspec(-1), vspec(0), vspec(1), vspec(2),
                  cspec, cspec, pl.BlockSpec((1, g, tq), lambda bi, hi, i: (hi, 0, 0))],
        out_specs=pl.BlockSpec((tq, g * dh), lambda bi, hi, i: (bi * nq + i, hi)),
        compiler_params=_params("parallel", "parallel", "arbitrary"),
        name="window_attention",
    )(qt, kt, kt, kt, kt, v_all, v_all, v_all, v_all, k_all, v_all, sink2)


def _ctx_attn_kernel(q_ref, k_ref, v_ref, sink_ref, o_ref, *, use_sink):
    g, lq, dh = q_ref.shape[1:]
    q = q_ref[0].reshape(g * lq, dh)
    s = lax.dot_general(q, k_ref[0, 0], (((1,), (1,)), ((), ())), preferred_element_type=F32) * ATTN_SCALE
    m = jnp.max(s, axis=1, keepdims=True)
    if use_sink:
        m = jnp.maximum(m, sink_ref[0])
    p = jnp.exp(s - m)
    den = jnp.sum(p, axis=1, keepdims=True)
    if use_sink:
        den = den + jnp.exp(sink_ref[0] - m)
    o = jnp.dot(p.astype(BF16), v_ref[0, 0], preferred_element_type=F32) / den
    o_ref[0] = o.reshape(g, lq, dh).astype(o_ref.dtype)


def _ctx_attn(q, k, v, sink_col, use_sink):
    b, nh, lq, dh = q.shape
    hkv = k.shape[1]
    g = nh // hkv
    qblk = pl.BlockSpec((1, g, lq, dh), lambda bi, hi: (bi, hi, 0, 0))
    kblk = pl.BlockSpec((1, 1, k.shape[2], dh), lambda bi, hi: (bi, hi, 0, 0))
    return pl.pallas_call(
        functools.partial(_ctx_attn_kernel, use_sink=use_sink),
        out_shape=jax.ShapeDtypeStruct(q.shape, BF16),
        grid=(b, hkv),
        in_specs=[qblk, kblk, kblk, pl.BlockSpec((1, g * lq, 1), lambda bi, hi: (hi, 0, 0))],
        out_specs=qblk,
        compiler_params=_params("parallel", "parallel"),
        name="context_attention",
    )(q, k, v, sink_col)


def _flash_kernel(q_ref, k_ref, v_ref, o_ref, m_scr, acc_scr, *bufs, tk):
    dh = HEAD_DIM
    g = q_ref.shape[0] // dh
    n = k_ref.shape[2] // tk
    nb = len(bufs) // 2
    depth = nb - 1
    s_bufs, c_bufs = bufs[:nb], bufs[nb:]
    m_scr[...] = jnp.full_like(m_scr, NEG_INF)
    acc_scr[...] = jnp.zeros_like(acc_scr)

    def scores(j, slot):
        kk = k_ref[0, 0, pl.ds(pl.multiple_of(j * tk, tk), tk), :]
        for h in range(g):
            sc = jnp.dot(kk, q_ref[h * dh:(h + 1) * dh, :], preferred_element_type=F32)
            s_bufs[slot][h] = sc
            c_bufs[slot][h] = jnp.max(sc, axis=0, keepdims=True)

    def update(j, slot):
        vv = v_ref[0, 0, :, pl.ds(pl.multiple_of(j * tk, tk), tk)]
        m_olds = [m_scr[h] for h in range(g)]
        m_news = [jnp.maximum(m_olds[h], c_bufs[slot][h]) for h in range(g)]
        ps = [jnp.exp2(s_bufs[slot][h] - m_news[h]).astype(BF16) for h in range(g)]
        pvs = [jnp.dot(vv, ps[h], preferred_element_type=F32) for h in range(g)]
        for h in range(g):
            acc_scr[h] = jnp.exp2(m_olds[h] - m_news[h]) * acc_scr[h] + pvs[h]
            m_scr[h] = m_news[h]

    for j in range(min(depth, n)):
        scores(j, j % nb)
    loops = max(n - depth, 0) // nb

    def body(i, carry):
        for r in range(nb):
            scores(nb * i + r + depth, (r + depth) % nb)
            update(nb * i + r, r)
        return carry

    lax.fori_loop(0, loops, body, 0)
    for j in range(loops * nb, n):
        if j + depth < n:
            scores(j + depth, (j + depth) % nb)
        update(j, j % nb)
    acc = acc_scr[...]
    out = (acc[:, :dh] / acc[:, dh:dh + 1]).reshape(g * dh, -1)
    o_ref[...] = out.T.astype(o_ref.dtype)


def _flash(qt, k, vt, b, s, tq=None, tk=None):
    w = qt.shape[0]
    hkv, t, dh = k.shape[1], k.shape[2], k.shape[3]
    g = w // dh // hkv
    tq = tq or _tile(s, (256, 128))
    tk = tk or _tile(t, (256, 128))
    nq = s // tq
    ones_rows = 8
    vt = jnp.concatenate([vt, jnp.ones((b, hkv, ones_rows, t), vt.dtype)], axis=2)
    nv = dh + ones_rows
    return pl.pallas_call(
        functools.partial(_flash_kernel, tk=tk),
        out_shape=jax.ShapeDtypeStruct((b * s, w), BF16),
        grid=(b, hkv, nq),
        in_specs=[pl.BlockSpec((g * dh, tq), lambda bi, hi, i: (hi, bi * nq + i)),
                  pl.BlockSpec((1, 1, t, dh), lambda bi, hi, i: (bi, hi, 0, 0)),
                  pl.BlockSpec((1, 1, nv, t), lambda bi, hi, i: (bi, hi, 0, 0))],
        out_specs=pl.BlockSpec((tq, g * dh), lambda bi, hi, i: (bi * nq + i, hi)),
        scratch_shapes=([pltpu.VMEM((g, 1, tq), F32), pltpu.VMEM((g, nv, tq), F32)]
                        + [pltpu.VMEM((g, tk, tq), F32)] * (FLASH_DEPTH + 1)
                        + [pltpu.VMEM((g, 1, tq), F32)] * (FLASH_DEPTH + 1)),
        compiler_params=_params("parallel", "parallel", "arbitrary"),
        name="dense_attention",
    )(qt, k, vt)


def _bmm_kernel(a_ref, b_ref, o_ref):
    o_ref[0] = jnp.dot(a_ref[0], b_ref[0], preferred_element_type=F32).astype(o_ref.dtype)


def _group_matmul(a, bmat, name):
    g, m, k = a.shape
    n = bmat.shape[2]
    tm = _tile(m, (1024, 512, 256, 128, 64, 32, 16, 8))
    return pl.pallas_call(
        _bmm_kernel,
        out_shape=jax.ShapeDtypeStruct((g, m, n), F32),
        grid=(g, m // tm),
        in_specs=[pl.BlockSpec((1, tm, k), lambda gi, i: (gi, i, 0)),
                  pl.BlockSpec((1, k, n), lambda gi, i: (gi, 0, 0))],
        out_specs=pl.BlockSpec((1, tm, n), lambda gi, i: (gi, i, 0)),
        compiler_params=_params("parallel", "parallel"),
        name=name,
    )(a, bmat)


def _s5_scan_kernel(sre_ref, sim_ref, are_ref, aim_ref, zre_ref, zim_ref):
    nd, nsteps = sre_ref.shape[0], sre_ref.shape[1]
    for d in range(nd):
        a_re, a_im = are_ref[d], aim_ref[d]

        def body(i, carry, d=d, a_re=a_re, a_im=a_im):
            z_re, z_im = carry
            zre_ref[d, i] = z_re
            zim_ref[d, i] = z_im
            return (a_re * z_re - a_im * z_im + sre_ref[d, i], a_re * z_im + a_im * z_re + sim_ref[d, i])

        zero = jnp.zeros(sre_ref.shape[2:], F32)
        lax.fori_loop(0, nsteps, body, (zero, zero))


def _s5_scan(s_re, s_im, a_re, a_im):
    full = lambda arr: pl.BlockSpec(arr.shape, lambda i: (0,) * arr.ndim)
    return pl.pallas_call(
        _s5_scan_kernel,
        out_shape=[jax.ShapeDtypeStruct(s_re.shape, F32)] * 2,
        grid=(1,),
        in_specs=[full(s_re), full(s_im), full(a_re), full(a_im)],
        out_specs=[full(s_re), full(s_im)],
        compiler_params=_params("arbitrary"),
        name="s5_chunk_scan",
    )(s_re, s_im, a_re, a_im)


def _s5_glu_kernel(y_ref, u_ref, d_ref, w_ref, b_ref, o_ref):
    y = y_ref[...] + d_ref[...] * u_ref[...]
    y = jax.nn.gelu(y)
    gate = jnp.dot(y.astype(BF16), w_ref[...], preferred_element_type=F32) + b_ref[...]
    o_ref[...] = (y * jax.nn.sigmoid(gate)).astype(o_ref.dtype)


def _s5_glu(y, u, d_skip, w_glu, b_glu, tm):
    r, c = y.shape
    blk = pl.BlockSpec((tm, c), lambda i: (i, 0))
    vec = pl.BlockSpec((1, c), lambda i: (0, 0))
    return pl.pallas_call(
        _s5_glu_kernel,
        out_shape=jax.ShapeDtypeStruct((r, c), BF16),
        grid=(r // tm,),
        in_specs=[blk, blk, vec, pl.BlockSpec((c, c), lambda i: (0, 0)), vec],
        out_specs=blk,
        compiler_params=_params("parallel"),
        name="s5_readout_glu",
    )(y, u, d_skip.reshape(1, c), w_glu.astype(BF16), b_glu.reshape(1, c))


def _s5_tables(lam_re, lam_im, log_dt, b_re, b_im, c_re, c_im):
    ln, p, gc = S5_CHUNK, S5_STATE, S5_GROUP
    lam = lax.complex(lam_re.astype(F32), lam_im.astype(F32))
    dt = jnp.exp(log_dt.astype(F32))[..., None]
    a_bar = jnp.exp(lam * dt)
    b_scale = (a_bar - 1.0) / lam
    b_mat = lax.complex(b_re.astype(F32), b_im.astype(F32))
    c_mat = lax.complex(c_re.astype(F32), c_im.astype(F32))
    tau = jnp.arange(ln + 1, dtype=F32)
    apow = jnp.exp((lam * dt)[:, :, None, :] * tau[None, None, :, None])
    drive = b_scale[..., None] * b_mat[None]
    kern = jnp.real(jnp.einsum('gcp,dgtp,dgpe->dgtce', c_mat, apow[:, :, :ln], drive))
    kc = jnp.concatenate([kern[1, :, :0:-1], (kern[0, :, :1] + kern[1, :, :1]), kern[0, :, 1:]], axis=1)
    kc_e = kc.transpose(0, 3, 1, 2).reshape(S5_GROUPS, gc, (2 * ln - 1) * gc)
    toep = jnp.stack([kc_e[:, :, (ln - 1 - s) * gc:(2 * ln - 1 - s) * gc] for s in range(ln)], axis=1)
    toep = toep.reshape(S5_GROUPS, ln * gc, ln * gc)
    w_f = apow[0, :, ln - 1::-1][:, :ln, :, None] * drive[0][:, None]
    w_b = apow[1, :, :ln, :, None] * drive[1][:, None]
    def m_in(wc):
        wt = wc.transpose(0, 1, 3, 2).reshape(S5_GROUPS, ln * gc, p)
        return jnp.concatenate([jnp.real(wt), jnp.imag(wt)], axis=-1)
    min_all = jnp.concatenate([m_in(w_f), m_in(w_b)], axis=-1)
    o_f = c_mat[:, None] * apow[0, :, 1:ln + 1][:, :, None, :]
    o_b = c_mat[:, None] * apow[1, :, ln:0:-1][:, :, None, :]
    def m_out(oc):
        ot = oc.transpose(0, 3, 1, 2).reshape(S5_GROUPS, p, ln * gc)
        return jnp.concatenate([jnp.real(ot), -jnp.imag(ot)], axis=1)
    rhs = jnp.concatenate([toep, m_out(o_f), m_out(o_b)], axis=1)
    a_l = apow[:, :, ln].reshape(2, 1, S5_GROUPS * p)
    return min_all.astype(BF16), rhs.astype(BF16), jnp.real(a_l), jnp.imag(a_l)


def _s5_mixer(u_x, u_c, tables, d_skip, w_glu, b_glu):
    min_all, rhs, a_re, a_im = tables
    b, s, _ = u_x.shape
    ln, p, gc, ng = S5_CHUNK, S5_STATE, S5_GROUP, S5_GROUPS

    def chunks(u):
        nc = u.shape[1] // ln
        return u.reshape(b, nc, ln, ng, gc).transpose(3, 0, 1, 2, 4).reshape(ng, b * nc, ln * gc).astype(BF16), nc

    ux, ncx = chunks(u_x)
    uc, ncc = chunks(u_c)
    sx = _group_matmul(ux, min_all, "s5_local_state").reshape(ng, b, ncx, 4, p)
    sc = _group_matmul(uc, min_all, "s5_local_state_ctx").reshape(ng, b, ncc, 4, p)

    def scan_order(part):
        f = jnp.concatenate([sc[:, :, :, part], sx[:, :, :, part]], axis=2)
        r = jnp.concatenate([sc[:, :, ::-1, part + 2], sx[:, :, ::-1, part + 2]], axis=2)
        return jnp.stack([f, r]).transpose(0, 3, 2, 1, 4).reshape(2, ncc + ncx, b, ng * p)

    z_re, z_im = _s5_scan(scan_order(0), scan_order(1), a_re, a_im)

    def latent(z, d):
        zl = z[d, ncc:]
        if d == 1:
            zl = zl[::-1]
        return zl.reshape(ncx, b, ng, p).transpose(2, 1, 0, 3).reshape(ng, b * ncx, p)

    lhs = jnp.concatenate([ux, latent(z_re, 0).astype(BF16), latent(z_im, 0).astype(BF16),
                           latent(z_re, 1).astype(BF16), latent(z_im, 1).astype(BF16)], axis=-1)
    y = _group_matmul(lhs, rhs, "s5_outputs")
    y = y.reshape(ng, b, ncx, ln, gc).transpose(1, 2, 3, 0, 4).reshape(b * s, S5_CHANNELS)
    return _s5_glu(y, u_x.reshape(b * s, S5_CHANNELS), d_skip, w_glu, b_glu, _tile(b * s, (1024, 512, 256)))


def _out_proj_kernel(x_ref, a_ref, b_ref, wa_ref, wb_ref, gt_ref, g_ref, sc_ref, sh_ref, wr_ref, br_ref,
                     xo_ref, xn_ref, lg_ref):
    y = (jnp.dot(a_ref[...], wa_ref[...], preferred_element_type=F32)
         + jnp.dot(b_ref[...], wb_ref[...], preferred_element_type=F32))
    x = x_ref[...] + gt_ref[0] * y
    xo_ref[...] = x
    xn = x * lax.rsqrt(jnp.mean(x * x, axis=-1, keepdims=True) + EPS) * g_ref[...]
    xn = xn * (1.0 + sc_ref[0]) + sh_ref[0]
    hi = xn.astype(BF16)
    xn_ref[...] = hi
    lo = (xn - hi.astype(F32)).astype(BF16)
    part = jnp.dot(hi, wr_ref[...], preferred_element_type=F32)
    part = part[:, :LANES] + part[:, LANES:] + jnp.dot(lo, wr_ref[:, :LANES], preferred_element_type=F32)
    lg_ref[...] = part[:, :lg_ref.shape[1]] + br_ref[...]


def _out_proj(x, row0, a, bmix, wa, wb, gate, g_ffn, scale, shift, mod_map, w_router, b_router, tm):
    r = a.shape[0]
    d = x.shape[1]
    t0 = row0 // tm
    ne = w_router.shape[1]
    w_hi = w_router.astype(BF16)
    w_lo = (w_router - w_hi.astype(F32)).astype(BF16)
    w_router = jnp.concatenate([_pad_cols(w_hi), _pad_cols(w_lo)], axis=1)
    row = lambda w: pl.BlockSpec((tm, w), lambda i: (i, 0))
    mod = pl.BlockSpec((1, 1, d), lambda i: (mod_map(i), 0, 0))
    full = lambda arr: pl.BlockSpec(arr.shape, lambda i: (0, 0))
    return pl.pallas_call(
        _out_proj_kernel,
        out_shape=[jax.ShapeDtypeStruct((r, d), F32), jax.ShapeDtypeStruct((r, d), BF16),
                   jax.ShapeDtypeStruct((r, ne), F32)],
        grid=(r // tm,),
        in_specs=[pl.BlockSpec((tm, d), lambda i: (i + t0, 0)), row(a.shape[1]), row(bmix.shape[1]),
                  full(wa), full(wb), mod, pl.BlockSpec((1, d), lambda i: (0, 0)), mod, mod,
                  full(w_router), pl.BlockSpec((1, ne), lambda i: (0, 0))],
        out_specs=[row(d), row(d), row(ne)],
        compiler_params=_params("parallel"),
        name="out_proj",
    )(x, a, bmix, wa, wb, gate, g_ffn.reshape(1, d), scale, shift, w_router, b_router.reshape(1, ne))


def _route_kernel(lg_ref, idx_ref, gate_ref, pos_ref, cnt_ref, carry):
    tm, ne = lg_ref.shape

    @pl.when(pl.program_id(0) == 0)
    def _():
        carry[...] = jnp.zeros_like(carry)

    work = lg_ref[...]
    lane = lax.broadcasted_iota(I32, (tm, ne), 1).astype(F32)
    out_lane = lax.broadcasted_iota(I32, (tm, LANES), 1)
    vals, hots, idx_out = [], [], jnp.zeros((tm, LANES), I32)
    for kk in range(TOP_K):
        mx = jnp.max(work, axis=1, keepdims=True)
        idx = jnp.min(jnp.where(work == mx, lane, float(ne)), axis=1, keepdims=True)
        hot = lane == idx
        work = jnp.where(hot, -jnp.inf, work)
        vals.append(mx)
        hots.append(hot)
        idx_out = jnp.where(out_lane == kk, idx.astype(I32), idx_out)
    exps = [jnp.exp(vv - vals[0]) for vv in vals]
    tot = exps[0] + exps[1] + exps[2] + exps[3]
    multi = (hots[0] | hots[1] | hots[2] | hots[3]).astype(BF16)
    r_i = lax.broadcasted_iota(I32, (tm, tm), 0)
    c_i = lax.broadcasted_iota(I32, (tm, tm), 1)
    before = (r_i > c_i).astype(BF16)
    prefix = jnp.dot(before, multi, preferred_element_type=F32) + carry[...]
    gate_out = jnp.zeros((tm, LANES), F32)
    pos_out = jnp.zeros((tm, LANES), I32)
    for kk in range(TOP_K):
        gate_out = jnp.where(out_lane == kk, exps[kk] / tot, gate_out)
        pos = jnp.sum(jnp.where(hots[kk], prefix, 0.0), axis=1, keepdims=True).astype(I32)
        pos_out = jnp.where(out_lane == kk, pos, pos_out)
    idx_ref[...] = idx_out
    gate_ref[...] = gate_out
    pos_ref[...] = pos_out
    carry[...] = carry[...] + jnp.sum(multi.astype(F32), axis=0, keepdims=True)
    cnt_ref[...] = carry[...]


def _route(logits, tm, row0, n):
    ne = logits.shape[1]
    t0 = row0 // tm
    wide = pl.BlockSpec((tm, LANES), lambda i: (i, 0))
    return pl.pallas_call(
        _route_kernel,
        out_shape=[jax.ShapeDtypeStruct((n, LANES), I32), jax.ShapeDtypeStruct((n, LANES), F32),
                   jax.ShapeDtypeStruct((n, LANES), I32), jax.ShapeDtypeStruct((1, ne), F32)],
        grid=(n // tm,),
        in_specs=[pl.BlockSpec((tm, ne), lambda i: (i + t0, 0))],
        out_specs=[wide, wide, wide, pl.BlockSpec((1, ne), lambda i: (0, 0))],
        scratch_shapes=[pltpu.VMEM((1, ne), F32)],
        compiler_params=_params("arbitrary"),
        name="moe_route",
    )(logits)


def _expert_kernel(be_ref, nb_ref, x_ref, wgu_ref, bgu_ref, wd_ref, bd_ref, perm_ref, o_ref, wgu_bf, wd_bf):
    i = pl.program_id(0)
    fresh = jnp.logical_or(i == 0, be_ref[i] != be_ref[jnp.maximum(i - 1, 0)])

    @pl.when(jnp.logical_and(i < nb_ref[0], fresh))
    def _():
        tile = perm_ref.shape[0]
        for t in range(wgu_ref.shape[2] // tile):
            sl = slice(t * tile, (t + 1) * tile)
            wgu_bf[:, sl] = jnp.dot(wgu_ref[0, :, sl].astype(BF16), perm_ref[...],
                                    preferred_element_type=F32).astype(BF16)
        wd_bf[...] = wd_ref[0].astype(BF16)

    @pl.when(i < nb_ref[0])
    def _():
        h = jnp.dot(x_ref[...], wgu_bf[...], preferred_element_type=F32) + bgu_ref[0]
        acts = []
        for t in range(h.shape[1] // (2 * LANES)):
            glu = jnp.minimum(h[:, 2 * LANES * t:2 * LANES * t + LANES], SWIGLU_LIMIT)
            lin = jnp.clip(h[:, 2 * LANES * t + LANES:2 * LANES * (t + 1)], -SWIGLU_LIMIT, SWIGLU_LIMIT)
            acts.append((glu * jax.nn.sigmoid(SWIGLU_ALPHA * glu) * (lin + 1.0)).astype(BF16))
        act = jnp.concatenate(acts, axis=1)
        o_ref[...] = (jnp.dot(act, wd_bf[...], preferred_element_type=F32) + bd_ref[0]).astype(o_ref.dtype)

    @pl.when(i >= nb_ref[0])
    def _():
        o_ref[...] = jnp.zeros_like(o_ref)


def _experts(xs, blk_e, n_used, wgu, bgu, wd, bd):
    n, d = xs.shape
    ff2 = wgu.shape[2]
    nblk = n // MOE_ROWS
    tile = 2 * LANES
    j = np.arange(tile)
    col = np.where(j < LANES, 2 * j, 2 * (j - LANES) + 1)
    perm = jnp.asarray(np.arange(tile)[:, None] == col[None, :], BF16)
    wspec = lambda shape: pl.BlockSpec((1,) + shape, lambda i, be, nb: (be[i], 0, 0))
    return pl.pallas_call(
        _expert_kernel,
        out_shape=jax.ShapeDtypeStruct((n, d), BF16),
        grid_spec=pltpu.PrefetchScalarGridSpec(
            num_scalar_prefetch=2,
            grid=(nblk,),
            in_specs=[pl.BlockSpec((MOE_ROWS, d), lambda i, be, nb: (i, 0)),
                      wspec((d, ff2)), wspec((1, ff2)), wspec((ff2 // 2, d)), wspec((1, d)),
                      pl.BlockSpec((tile, tile), lambda i, be, nb: (0, 0))],
            out_specs=pl.BlockSpec((MOE_ROWS, d), lambda i, be, nb: (i, 0)),
            scratch_shapes=[pltpu.VMEM((d, ff2), BF16), pltpu.VMEM((ff2 // 2, d), BF16)]),
        compiler_params=_params("arbitrary"),
        name="moe_experts",
    )(blk_e, n_used, xs, wgu, bgu, wd, bd, perm)


def _combine_kernel(x_ref, y_ref, gate_ref, gt_ref, g_ref, o_ref, *, final_norm):
    gates = gate_ref[...]
    y = gates[:, 0:1] * y_ref[0].astype(F32)
    for kk in range(1, TOP_K):
        y = y + gates[:, kk:kk + 1] * y_ref[kk].astype(F32)
    x = x_ref[...] + gt_ref[0] * y
    if final_norm:
        x = x * lax.rsqrt(jnp.mean(x * x, axis=-1, keepdims=True) + EPS) * g_ref[...]
    o_ref[...] = x


def _combine_alias_kernel(x_ref, y_ref, gate_ref, gt_ref, g_ref, prev_ref, o_ref, *, final_norm):
    del prev_ref
    _combine_kernel(x_ref, y_ref, gate_ref, gt_ref, g_ref, o_ref, final_norm=final_norm)


def _combine(x, y4, gates, gate_mod, mod_map, g_final, final_norm, tm, row0, prev):
    r, d = x.shape
    n = y4.shape[1]
    t0 = row0 // tm
    in_specs = [pl.BlockSpec((tm, d), lambda i: (i + t0, 0)),
                pl.BlockSpec((TOP_K, tm, d), lambda i: (0, i, 0)),
                pl.BlockSpec((tm, LANES), lambda i: (i, 0)),
                pl.BlockSpec((1, 1, d), lambda i: (mod_map(i + t0), 0, 0)),
                pl.BlockSpec((1, d), lambda i: (0, 0))]
    args = [x, y4, gates, gate_mod, g_final.reshape(1, d)]
    body, aliases = _combine_kernel, {}
    if prev is not None:
        in_specs.append(pl.BlockSpec(memory_space=pl.ANY))
        args.append(prev)
        body, aliases = _combine_alias_kernel, {len(args) - 1: 0}
    return pl.pallas_call(
        functools.partial(body, final_norm=final_norm),
        out_shape=jax.ShapeDtypeStruct((r, d), F32),
        grid=(n // tm,),
        in_specs=in_specs,
        out_specs=pl.BlockSpec((tm, d), lambda i: (i + t0, 0)),
        input_output_aliases=aliases,
        compiler_params=_params("parallel"),
        name="moe_combine",
    )(*args)


def _moe_parts(tiles):
    total = sum(MOE_SPLIT)
    sizes = [tiles * w // total for w in MOE_SPLIT[:-1]]
    sizes.append(tiles - sum(sizes))
    return [sz for sz in sizes if sz > 0]


def _moe(x, xn, logits, gate_mod, mod_map, weights, g_final, final_norm, tm):
    wgu, bgu, wd, bd = weights
    r, d = xn.shape
    routed, row0 = [], 0
    for tiles in _moe_parts(r // tm):
        n = tiles * tm
        nblk = n * TOP_K // MOE_ROWS + N_EXPERTS
        blk_row0 = jnp.arange(nblk, dtype=I32) * MOE_ROWS
        idx_w, gates_w, pos_w, counts = _route(logits, tm, row0, n)
        idx, pos = idx_w[:, :TOP_K], pos_w[:, :TOP_K]
        counts = counts[0].astype(I32)
        padded = (counts + MOE_ROWS - 1) // MOE_ROWS * MOE_ROWS
        pad_end = jnp.cumsum(padded)
        pad_start = pad_end - padded
        dest = pad_start[idx] + pos
        blk_e = jnp.minimum(jnp.sum((pad_end[None, :] <= blk_row0[:, None]).astype(I32), axis=1), N_EXPERTS - 1)
        n_used = (pad_end[-1:] // MOE_ROWS).astype(I32)
        key = idx * n + jnp.arange(n, dtype=I32)[:, None]
        tok_sorted = jnp.sort(key.reshape(-1)) % n
        grp_start = jnp.cumsum(counts) - counts
        slot = blk_row0[:, None] + jnp.arange(MOE_ROWS, dtype=I32)[None, :]
        compact = jnp.minimum(slot + (grp_start - pad_start)[blk_e][:, None],
                              (grp_start + counts - 1)[blk_e][:, None])
        src = tok_sorted.at[jnp.clip(compact, 0, n * TOP_K - 1).reshape(-1)].get(mode="promise_in_bounds")
        xs = xn.at[src + row0].get(mode="promise_in_bounds")
        routed.append((xs, blk_e, n_used, dest, gates_w, row0, n))
        row0 += n
    ys = [_experts(xs, blk_e, n_used, wgu, bgu, wd, bd) for xs, blk_e, n_used, _, _, _, _ in routed]
    out = None
    for p, (_, _, _, dest, gates_w, row0, n) in enumerate(routed):
        y4 = ys[p].at[dest.T.reshape(-1)].get(mode="promise_in_bounds").reshape(TOP_K, n, d)
        out = _combine(x, y4, gates_w, gate_mod, mod_map, g_final, final_norm, tm, row0, out)
    return out


def _moe_weights(w_gate_up, b_gate_up, w_down, b_down):
    ne, d, ff2 = w_gate_up.shape
    b_tiled = b_gate_up.reshape(ne, ff2 // (2 * LANES), LANES, 2).swapaxes(-1, -2).reshape(ne, 1, ff2)
    return w_gate_up, b_tiled, w_down, b_down[:, None, :]


def _rope_tables(n_tokens):
    rows = n_tokens // GRID_W
    row = jnp.repeat(jnp.arange(rows, dtype=I32), GRID_W).astype(F32)
    col = jnp.tile(jnp.arange(GRID_W, dtype=I32), rows).astype(F32)
    inv = ROPE_BASE ** (-jnp.arange(0, ROPE_AXIS_DIM, 2, dtype=F32) / ROPE_AXIS_DIM)
    ang_r, ang_c = row[:, None] * inv, col[:, None] * inv
    cos = jnp.concatenate([jnp.cos(ang_r)] * 2 + [jnp.cos(ang_c)] * 2, axis=-1)
    sin = jnp.concatenate([jnp.sin(ang_r)] * 2 + [jnp.sin(ang_c)] * 2, axis=-1)
    return cos, sin


def _rot_perm():
    q = ROPE_AXIS_DIM // 2
    d = np.arange(HEAD_DIM)
    first = (d % ROPE_AXIS_DIM) < q
    perm = np.where(first, d + q, d - q)
    sign = np.where(first, -1.0, 1.0).astype(np.float32)
    return perm, sign


def _rot_cols(w, n_heads):
    perm, sign = _rot_perm()
    k = w.shape[0]
    wh = w.reshape(k, n_heads, HEAD_DIM)
    return (wh[:, :, perm] * sign).reshape(k, n_heads * HEAD_DIM)


def _to_heads(t, b, n_heads):
    return t.reshape(b, -1, n_heads, HEAD_DIM).transpose(0, 2, 1, 3)


def _from_heads(t):
    b, h, tt, dh = t.shape
    return t.transpose(0, 2, 1, 3).reshape(b * tt, h * dh)


def _layer_even(xa, mod, b, s, lc, g_mix, g_ffn, w_in, w_out, gate_bias, mlstm_norm, sink, router, moe_w):
    d = xa.shape[1]
    nctx = b * lc
    tm = _tile(math.gcd(nctx, s), (512, 256, 128))
    mod_map = lambda i: jnp.where(i * tm < nctx, b, (i * tm - nctx) // s)
    sh1, sc1, gt1, sh2, sc2, gt2 = [mod[:, j][:, None, :] for j in range(6)]

    hm, hs, hk = MLSTM_HEADS * HEAD_DIM, SWA_HEADS * HEAD_DIM, SWA_KV_HEADS * HEAD_DIM
    o0 = np.cumsum([0, hm, hm, hm, hm, 4 * MLSTM_HEADS, hs, hk, hk])
    seg = lambda j: w_in[:, o0[j]:o0[j + 1]]
    w_cat = jnp.concatenate([seg(0), seg(1) * ATTN_SCALE, seg(2), seg(3), _pad_cols(seg(4))], axis=1).astype(BF16)
    w_t = jnp.concatenate([seg(5), _rot_cols(seg(5), SWA_HEADS), seg(6), _rot_cols(seg(6), SWA_KV_HEADS), seg(7)],
                          axis=1).T.astype(BF16)
    widths = [hm, hm, hm, hm, LANES]
    dts = [BF16] * 4 + [F32]
    qa, ka, va, oa, gts, qt, qt_r, kt, kt_r, vt = _in_proj(
        xa, g_mix, sc1, sh1, mod_map, w_cat, widths, dts, tm,
        wt=w_t, t_widths=[hs, hs, hk, hk, hk], t_dtypes=[BF16] * 5)

    hf, hb = _mlstm(qa, ka, va, gts, _pad_cols(gate_bias.astype(F32).reshape(1, -1)), b, lc, s)
    mix_a = _mlstm_out(hf, hb, oa, mlstm_norm.reshape(-1), tm)

    cos, sin = _rope_tables(s)
    ones = jnp.ones((HEAD_DIM,), F32)
    g = SWA_HEADS // SWA_KV_HEADS
    heads = lambda t, n: _to_heads(t, b, n)
    log2e = math.log2(math.e)
    q_x = _q_prep_t(qt, qt_r, nctx, b * s, cos.T, sin.T, ones, ones, ATTN_SCALE * log2e, norm=False)
    k_x = _q_prep_t(kt, kt_r, nctx, b * s, cos.T, sin.T, ones, ones, 1.0, norm=False)
    k_c, v_c = heads(kt[:, :nctx].T, SWA_KV_HEADS), heads(vt[:, :nctx].T, SWA_KV_HEADS)
    q_c = heads(qt[:, :nctx].T, SWA_HEADS)
    sink_h = sink.astype(F32).reshape(SWA_KV_HEADS, g, 1)
    att_x = _swa(q_x, k_x, (kt, vt), jnp.broadcast_to(sink_h * log2e, (SWA_KV_HEADS, g, 2 * SWA_BLOCK)), b, s, lc)
    att_c = _ctx_attn(q_c, k_c, v_c, jnp.broadcast_to(sink_h[..., None], (SWA_KV_HEADS, g, lc, 1))
                      .reshape(SWA_KV_HEADS, g * lc, 1), True)
    mix_b = jnp.concatenate([_from_heads(att_c), att_x], axis=0)

    w_router, b_router = router
    xa, xn2, logits = _out_proj(xa, 0, mix_a, mix_b, w_out[:hm].astype(BF16), w_out[hm:].astype(BF16), gt1, g_ffn,
                                sc2, sh2, mod_map, w_router.astype(F32), b_router.astype(F32), tm)
    return _moe(xa, xn2, logits, gt2, mod_map, moe_w, g_ffn, False, tm)


def _layer_odd_last(xa, mod, b, s, lc, g_mix, g_ffn, w_in, w_out, s5_params, d_skip, w_glu, b_glu,
                    q_norm, k_norm, router, moe_w, g_final):
    nctx = b * lc
    tm = _tile(math.gcd(nctx, s), (512, 256, 128))
    mod_map = lambda i: jnp.where(i * tm < nctx, b, (i * tm - nctx) // s)
    lat_map = lambda i: i * tm // s
    sh1, sc1, gt1, sh2, sc2, gt2 = [mod[:, j][:, None, :] for j in range(6)]

    hq, hk = ATT_HEADS * HEAD_DIM, ATT_KV_HEADS * HEAD_DIM
    o1 = np.cumsum([0, S5_CHANNELS, hq, hk, hk])
    seg = lambda j: w_in[:, o1[j]:o1[j + 1]]
    kpad = lambda w: _pad_cols(w, 2 * LANES)
    w_cat = jnp.concatenate([seg(0), kpad(seg(2)), kpad(_rot_cols(seg(2), ATT_KV_HEADS)), kpad(seg(3))],
                            axis=1).astype(BF16)
    w_q = jnp.concatenate([seg(1), _rot_cols(seg(1), ATT_HEADS)], axis=1).T.astype(BF16)
    widths = [S5_CHANNELS, 2 * LANES, 2 * LANES, 2 * LANES]
    dts = [F32, BF16, BF16, BF16]
    u, k, k_r, v, qt, qt_r = _in_proj(xa, g_mix, sc1, sh1, mod_map, w_cat, widths, dts, tm,
                                      wt=w_q, t_widths=[hq, hq], t_dtypes=[BF16, BF16])

    mix_a = _s5_mixer(u[nctx:].reshape(b, s, S5_CHANNELS), u[:nctx].reshape(b, lc, S5_CHANNELS),
                      _s5_tables(*s5_params), d_skip, w_glu, b_glu)

    cos, sin = _rope_tables(s)
    perm, _ = _rot_perm()
    qn, kn = q_norm.astype(F32), k_norm.astype(F32)
    heads = lambda t, n: _to_heads(t[:, :n * HEAD_DIM], b, n)
    q_x = _q_prep_t(qt, qt_r, nctx, b * s, cos.T, sin.T, qn, qn[perm], ATTN_SCALE * math.log2(math.e))
    k_x = _qk_prep(heads(k[nctx:], ATT_KV_HEADS), heads(k_r[nctx:], ATT_KV_HEADS), cos, sin, kn, kn[perm],
                   norm=True, rope=True, scale=1.0)
    kc_raw = heads(k[:nctx], ATT_KV_HEADS)
    k_c = _qk_prep(kc_raw, kc_raw, cos[:lc], sin[:lc], kn, kn, norm=True, rope=False, scale=1.0)
    k_all = jnp.concatenate([k_c, k_x], axis=2)
    v_all = jnp.concatenate([heads(v[:nctx], ATT_KV_HEADS), heads(v[nctx:], ATT_KV_HEADS)], axis=2)
    mix_b = _flash(q_x, k_all, v_all.swapaxes(-1, -2), b, s)

    w_router, b_router = router
    hs = S5_CHANNELS
    x, xn2, logits = _out_proj(xa, nctx, mix_a, mix_b, w_out[:hs].astype(BF16), w_out[hs:].astype(BF16), gt1, g_ffn,
                               sc2, sh2, lat_map, w_router.astype(F32), b_router.astype(F32), tm)
    return _moe(x, xn2, logits, gt2, lat_map, moe_w, g_final, True, tm)


def kernel(x, c, ctx, c_ctx, l0_w_mod, l0_b_mod, l0_g_mix, l0_g_ffn, l0_w_in, l0_w_out, l0_gate_bias, l0_mlstm_norm, l0_sink, l0_w_router, l0_b_router, l0_w_gate_up, l0_b_gate_up, l0_w_down, l0_b_down, l1_w_mod, l1_b_mod, l1_g_mix, l1_g_ffn, l1_w_in, l1_w_out, l1_lam_re, l1_lam_im, l1_log_dt, l1_b_re, l1_b_im, l1_c_re, l1_c_im, l1_d_skip, l1_w_glu, l1_b_glu, l1_q_norm, l1_k_norm, l1_w_router, l1_b_router, l1_w_gate_up, l1_b_gate_up, l1_w_down, l1_b_down, g_final):
    b, s, d = x.shape
    lc = ctx.shape[1]
    cond = jnp.concatenate([c, c_ctx[None, :]], axis=0)
    cond = jnp.pad(cond, ((0, (-(b + 1)) % 8), (0, 0)))
    mod0 = _silu_linear(cond, l0_w_mod, l0_b_mod)[:b + 1].reshape(b + 1, 6, d)
    mod1 = _silu_linear(cond, l1_w_mod, l1_b_mod)[:b + 1].reshape(b + 1, 6, d)

    moe0 = _moe_weights(l0_w_gate_up, l0_b_gate_up, l0_w_down, l0_b_down)
    moe1 = _moe_weights(l1_w_gate_up, l1_b_gate_up, l1_w_down, l1_b_down)
    xa = jnp.concatenate([ctx.reshape(b * lc, d), x.reshape(b * s, d)], axis=0)
    xa = _layer_even(xa, mod0, b, s, lc, l0_g_mix, l0_g_ffn, l0_w_in, l0_w_out, l0_gate_bias, l0_mlstm_norm,
                     l0_sink, (l0_w_router, l0_b_router), moe0)
    out = _layer_odd_last(xa, mod1, b, s, lc, l1_g_mix, l1_g_ffn, l1_w_in, l1_w_out,
                          (l1_lam_re, l1_lam_im, l1_log_dt, l1_b_re, l1_b_im, l1_c_re, l1_c_im),
                          l1_d_skip, l1_w_glu, l1_b_glu, l1_q_norm, l1_k_norm, (l1_w_router, l1_b_router),
                          moe1, g_final)
    return out.reshape(b, s, d)
```
